```python
import math
import jax, jax.numpy as jnp
from jax import lax
import numpy as np

D_MODEL = 1024
BATCH = 32
SEQ = 2048
DEPTH = 2

EPS = 1e-6
N_BRANCH = 3
BRANCH_W = D_MODEL // 2
SSM_GROUP = 16
SSM_GROUPS = BRANCH_W // SSM_GROUP
SSM_STATE = 64
DT_MIN = 1e-3
DT_MAX = 1e-1
HEAD_DIM = 64
ATT_HEADS = BRANCH_W // HEAD_DIM
DILATED_PATTERNS = ((128, 1), (512, 4), (2048, 16))
N_PAT = len(DILATED_PATTERNS)
ATT_BLOCK = 128
ATT_SCALE = HEAD_DIM ** -0.5
CONV_WIDTH = 31
D_FF = -(-(8 * D_MODEL) // (3 * 256)) * 256

COL_U = BRANCH_W
COL_Q = N_PAT * BRANCH_W
COL_KV = BRANCH_W
COL_CONV = 2 * BRANCH_W
COL_GATE = N_BRANCH * D_MODEL
SPLIT_POINTS = (COL_U,
                COL_U + COL_Q,
                COL_U + COL_Q + COL_KV,
                COL_U + COL_Q + 2 * COL_KV,
                COL_U + COL_Q + 2 * COL_KV + COL_CONV)
IN_COLS = SPLIT_POINTS[-1] + COL_GATE

kernel_name = 'hybrid_s5_dilated_attn_conformer_conv_block'


def rms_norm(x, g):
    xf = x.astype(jnp.float32)
    y = xf * lax.rsqrt(jnp.mean(xf * xf, axis=-1, keepdims=True) + EPS)
    return (y * g.astype(jnp.float32)).astype(x.dtype)


def _complex_affine_combine(e1, e2):
    a1r, a1i, b1r, b1i = e1
    a2r, a2i, b2r, b2i = e2
    return (a1r * a2r - a1i * a2i,
            a1r * a2i + a1i * a2r,
            a2r * b1r - a2i * b1i + b2r,
            a2r * b1i + a2i * b1r + b2i)


def s5_branch(u, lam_re, lam_im, log_dt, b_re, b_im, c_re, c_im, d_skip, w_glu):
    bsz, seq, _ = u.shape
    f32 = jnp.float32
    uf = u.astype(f32).reshape(bsz, seq, SSM_GROUPS, SSM_GROUP)
    lam_re = lam_re.astype(f32)
    lam_im = lam_im.astype(f32)
    dt = jnp.exp(log_dt.astype(f32))[:, None]
    mag = jnp.exp(lam_re * dt)
    ab_re = mag * jnp.cos(lam_im * dt)
    ab_im = mag * jnp.sin(lam_im * dt)
    nr, ni = ab_re - 1.0, ab_im
    den = lam_re * lam_re + lam_im * lam_im
    z_re = ((nr * lam_re + ni * lam_im) / den)[..., None]
    z_im = ((ni * lam_re - nr * lam_im) / den)[..., None]
    b_re = b_re.astype(f32)
    b_im = b_im.astype(f32)
    bb_re = z_re * b_re - z_im * b_im
    bb_im = z_re * b_im + z_im * b_re
    bu_re = jnp.einsum('blgh,gph->blgp', uf, bb_re)
    bu_im = jnp.einsum('blgh,gph->blgp', uf, bb_im)
    a_re = jnp.broadcast_to(ab_re, (1, seq, SSM_GROUPS, SSM_STATE))
    a_im = jnp.broadcast_to(ab_im, (1, seq, SSM_GROUPS, SSM_STATE))
    _, _, s_re, s_im = lax.associative_scan(
        _complex_affine_combine, (a_re, a_im, bu_re, bu_im), axis=1)
    y = (jnp.einsum('blgp,ghp->blgh', s_re, c_re.astype(f32))
         - jnp.einsum('blgp,ghp->blgh', s_im, c_im.astype(f32)))
    y = y.reshape(bsz, seq, BRANCH_W) + d_skip.astype(f32) * uf.reshape(bsz, seq, BRANCH_W)
    y = jax.nn.gelu(y).astype(u.dtype)
    z = y @ w_glu
    return z[..., :D_MODEL] * jax.nn.sigmoid(z[..., D_MODEL:])


def _dilated_group(q, k, v, window, dilation):
    bsz, seq, nh, hd = q.shape
    ls = seq // dilation
    nb = -(-ls // ATT_BLOCK)
    lp = nb * ATT_BLOCK
    w_sub = window // dilation

    def to_sub(t):
        t = t.reshape(bsz, ls, dilation, nh, hd).transpose(0, 2, 3, 1, 4)
        t = jnp.pad(t, ((0, 0), (0, 0), (0, 0), (0, lp - ls), (0, 0)))
        return t.reshape(bsz, dilation, nh, nb, ATT_BLOCK, hd)

    def with_prev(t):
        prev = jnp.pad(t, ((0, 0), (0, 0), (0, 0), (1, 0), (0, 0), (0, 0)))[:, :, :, :-1]
        return jnp.concatenate([prev, t], axis=4)

    qb = to_sub(q)
    kc = with_prev(to_sub(k))
    vc = with_prev(to_sub(v))
    s = jnp.einsum('bdhnqe,bdhnke->bdhnqk', qb, kc).astype(jnp.float32) * ATT_SCALE
    qi = jnp.arange(ATT_BLOCK)[:, None]
    kj = jnp.arange(2 * ATT_BLOCK)[None, :]
    dist = qi - kj + ATT_BLOCK
    kpos = jnp.arange(nb)[:, None, None] * ATT_BLOCK + kj[None] - ATT_BLOCK
    valid = (dist >= 0) & (dist <= w_sub) & (kpos >= 0)
    s = jnp.where(valid, s, -jnp.inf)
    m = jnp.max(s, axis=-1, keepdims=True)
    p = jnp.exp(s - m)
    den = jnp.sum(p, axis=-1, keepdims=True)
    o = jnp.einsum('bdhnqk,bdhnke->bdhnqe', p, vc.astype(jnp.float32)) / den
    lse = (m + jnp.log(den))[..., 0]
    o = o.reshape(bsz, dilation, nh, lp, hd)[:, :, :, :ls]
    o = o.transpose(0, 3, 1, 2, 4).reshape(bsz, seq, nh, hd)
    lse = lse.reshape(bsz, dilation, nh, lp)[:, :, :, :ls]
    lse = lse.transpose(0, 3, 1, 2).reshape(bsz, seq, nh)
    return o, lse


def dilated_attention(q, k, v):
    outs, lses = [], []
    for p_idx, (window, dilation) in enumerate(DILATED_PATTERNS):
        o, lse = _dilated_group(q[:, :, p_idx], k, v, window, dilation)
        outs.append(o)
        lses.append(lse)
    wts = jax.nn.softmax(jnp.stack(lses, axis=0), axis=0)
    return jnp.sum(wts[..., None] * jnp.stack(outs, axis=0), axis=0)


def conformer_conv(cv, conv_w, conv_b, ln_g, ln_b, w_pw2):
    a, g = jnp.split(cv, 2, axis=-1)
    h = a * jax.nn.sigmoid(g)
    h = lax.conv_general_dilated(
        h, conv_w[:, None, :], window_strides=(1,),
        padding=[(CONV_WIDTH - 1, 0)],
        dimension_numbers=('NWC', 'WIO', 'NWC'),
        feature_group_count=BRANCH_W) + conv_b
    hf = h.astype(jnp.float32)
    mu = jnp.mean(hf, axis=-1, keepdims=True)
    var = jnp.mean(jnp.square(hf - mu), axis=-1, keepdims=True)
    hn = (hf - mu) * lax.rsqrt(var + EPS) * ln_g.astype(jnp.float32) + ln_b.astype(jnp.float32)
    h = jax.nn.silu(hn).astype(cv.dtype)
    return h @ w_pw2


def _fwd_setup_inputs(seed: int = 0) -> dict:
    key = jax.random.key(seed)
    ks = jax.random.split(key, 24)
    f32 = jnp.float32

    def nrm(k, shape, scale):
        return jax.random.normal(k, shape, f32) * scale

    lam_im_base = jnp.pi * jnp.arange(SSM_STATE, dtype=f32)
    return {
        'x': nrm(ks[0], (BATCH, SEQ, D_MODEL), 1.0),
        'norm1_g': 1.0 + nrm(ks[1], (DEPTH, D_MODEL), 0.05),
        'w_in': nrm(ks[2], (DEPTH, D_MODEL, IN_COLS), D_MODEL ** -0.5),
        'b_gate': nrm(ks[3], (DEPTH, COL_GATE), 0.02),
        'ssm_lambda_re': -0.5 + nrm(ks[4], (DEPTH, SSM_GROUPS, SSM_STATE), 0.01),
        'ssm_lambda_im': lam_im_base + nrm(ks[5], (DEPTH, SSM_GROUPS, SSM_STATE), 0.01),
        'ssm_log_dt': jax.random.uniform(ks[6], (DEPTH, SSM_GROUPS), f32,
                                         math.log(DT_MIN), math.log(DT_MAX)),
        'ssm_b_re': nrm(ks[7], (DEPTH, SSM_GROUPS, SSM_STATE, SSM_GROUP), (2 * SSM_GROUP) ** -0.5),
        'ssm_b_im': nrm(ks[8], (DEPTH, SSM_GROUPS, SSM_STATE, SSM_GROUP), (2 * SSM_GROUP) ** -0.5),
        'ssm_c_re': nrm(ks[9], (DEPTH, SSM_GROUPS, SSM_GROUP, SSM_STATE), SSM_STATE ** -0.25),
        'ssm_c_im': nrm(ks[10], (DEPTH, SSM_GROUPS, SSM_GROUP, SSM_STATE), SSM_STATE ** -0.25),
        'ssm_d': nrm(ks[11], (DEPTH, BRANCH_W), 1.0),
        'w_ssm_glu': nrm(ks[12], (DEPTH, BRANCH_W, 2 * D_MODEL), BRANCH_W ** -0.5),
        'w_att_up': nrm(ks[13], (DEPTH, BRANCH_W, D_MODEL), BRANCH_W ** -0.5),
        'conv_w': nrm(ks[14], (DEPTH, CONV_WIDTH, BRANCH_W), CONV_WIDTH ** -0.5),
        'conv_b': nrm(ks[15], (DEPTH, BRANCH_W), 0.02),
        'conv_ln_g': 1.0 + nrm(ks[16], (DEPTH, BRANCH_W), 0.05),
        'conv_ln_b': nrm(ks[17], (DEPTH, BRANCH_W), 0.02),
        'w_conv_pw2': nrm(ks[18], (DEPTH, BRANCH_W, D_MODEL), BRANCH_W ** -0.5),
        'w_out': nrm(ks[19], (DEPTH, D_MODEL, D_MODEL), D_MODEL ** -0.5),
        'norm2_g': 1.0 + nrm(ks[20], (DEPTH, D_MODEL), 0.05),
        'w_ffn_in': nrm(ks[21], (DEPTH, D_MODEL, 2 * D_FF), D_MODEL ** -0.5),
        'w_ffn_out': nrm(ks[22], (DEPTH, D_FF, D_MODEL), D_FF ** -0.5),
        'final_g': 1.0 + nrm(ks[23], (D_MODEL,), 0.05),
    }


def _fwd_reference(x, norm1_g, w_in, b_gate, ssm_lambda_re, ssm_lambda_im, ssm_log_dt,
              ssm_b_re, ssm_b_im, ssm_c_re, ssm_c_im, ssm_d, w_ssm_glu, w_att_up,
              conv_w, conv_b, conv_ln_g, conv_ln_b, w_conv_pw2, w_out,
              norm2_g, w_ffn_in, w_ffn_out, final_g):
    bsz, seq, _ = x.shape
    for l in range(DEPTH):
        h = rms_norm(x, norm1_g[l])
        proj = h @ w_in[l]
        u, q, k, v, cv, g = jnp.split(proj, SPLIT_POINTS, axis=-1)
        y_s = s5_branch(u, ssm_lambda_re[l], ssm_lambda_im[l], ssm_log_dt[l],
                        ssm_b_re[l], ssm_b_im[l], ssm_c_re[l], ssm_c_im[l],
                        ssm_d[l], w_ssm_glu[l])
        q = q.reshape(bsz, seq, N_PAT, ATT_HEADS, HEAD_DIM)
        k = k.reshape(bsz, seq, ATT_HEADS, HEAD_DIM)
        v = v.reshape(bsz, seq, ATT_HEADS, HEAD_DIM)
        o = dilated_attention(q, k, v).astype(x.dtype).reshape(bsz, seq, BRANCH_W)
        y_a = o @ w_att_up[l]
        y_c = conformer_conv(cv, conv_w[l], conv_b[l], conv_ln_g[l], conv_ln_b[l], w_conv_pw2[l])
        gate = jax.nn.sigmoid((g + b_gate[l]).astype(jnp.float32)).astype(x.dtype)
        gate = gate.reshape(bsz, seq, N_BRANCH, D_MODEL)
        merged = gate[:, :, 0] * y_s + gate[:, :, 1] * y_a + gate[:, :, 2] * y_c
        x = x + merged @ w_out[l]
        h = rms_norm(x, norm2_g[l])
        z = h @ w_ffn_in[l]
        x = x + (jax.nn.silu(z[..., :D_FF]) * z[..., D_FF:]) @ w_ffn_out[l]
    return rms_norm(x, final_g)


import jax as _jax
import jax.numpy as _jnp

TWIN_FORMAT = 'train_step'
FWD_PARAMS = ['x', 'norm1_g', 'w_in', 'b_gate', 'ssm_lambda_re', 'ssm_lambda_im', 'ssm_log_dt', 'ssm_b_re', 'ssm_b_im', 'ssm_c_re', 'ssm_c_im', 'ssm_d', 'w_ssm_glu', 'w_att_up', 'conv_w', 'conv_b', 'conv_ln_g', 'conv_ln_b', 'w_conv_pw2', 'w_out', 'norm2_g', 'w_ffn_in', 'w_ffn_out', 'final_g']
TWIN_WEIGHTS = ['norm1_g', 'w_in', 'b_gate', 'ssm_lambda_re', 'ssm_lambda_im', 'ssm_log_dt', 'ssm_b_re', 'ssm_b_im', 'ssm_c_re', 'ssm_c_im', 'ssm_d', 'w_ssm_glu', 'w_att_up', 'conv_w', 'conv_b', 'conv_ln_g', 'conv_ln_b', 'w_conv_pw2', 'w_out', 'norm2_g', 'w_ffn_in', 'w_ffn_out', 'final_g']
TWIN_DIFF_INPUT = 'x'
TWIN_INPUTS = ['x', 'norm1_g', 'w_in', 'b_gate', 'ssm_lambda_re', 'ssm_lambda_im', 'ssm_log_dt', 'ssm_b_re', 'ssm_b_im', 'ssm_c_re', 'ssm_c_im', 'ssm_d', 'w_ssm_glu', 'w_att_up', 'conv_w', 'conv_b', 'conv_ln_g', 'conv_ln_b', 'w_conv_pw2', 'w_out', 'norm2_g', 'w_ffn_in', 'w_ffn_out', 'final_g', 'loss_target', 'm_norm1_g', 'm_w_in', 'm_b_gate', 'm_ssm_lambda_re', 'm_ssm_lambda_im', 'm_ssm_log_dt', 'm_ssm_b_re', 'm_ssm_b_im', 'm_ssm_c_re', 'm_ssm_c_im', 'm_ssm_d', 'm_w_ssm_glu', 'm_w_att_up', 'm_conv_w', 'm_conv_b', 'm_conv_ln_g', 'm_conv_ln_b', 'm_w_conv_pw2', 'm_w_out', 'm_norm2_g', 'm_w_ffn_in', 'm_w_ffn_out', 'm_final_g', 'v_norm1_g', 'v_w_in', 'v_b_gate', 'v_ssm_lambda_re', 'v_ssm_lambda_im', 'v_ssm_log_dt', 'v_ssm_b_re', 'v_ssm_b_im', 'v_ssm_c_re', 'v_ssm_c_im', 'v_ssm_d', 'v_w_ssm_glu', 'v_w_att_up', 'v_conv_w', 'v_conv_b', 'v_conv_ln_g', 'v_conv_ln_b', 'v_w_conv_pw2', 'v_w_out', 'v_norm2_g', 'v_w_ffn_in', 'v_w_ffn_out', 'v_final_g']
TWIN_OUTPUTS = ['loss', 'grad_x', 'grad_norm1_g', 'grad_w_in', 'grad_b_gate', 'grad_ssm_lambda_re', 'grad_ssm_lambda_im', 'grad_ssm_log_dt', 'grad_ssm_b_re', 'grad_ssm_b_im', 'grad_ssm_c_re', 'grad_ssm_c_im', 'grad_ssm_d', 'grad_w_ssm_glu', 'grad_w_att_up', 'grad_conv_w', 'grad_conv_b', 'grad_conv_ln_g', 'grad_conv_ln_b', 'grad_w_conv_pw2', 'grad_w_out', 'grad_norm2_g', 'grad_w_ffn_in', 'grad_w_ffn_out', 'grad_final_g', 'delta_norm1_g', 'delta_w_in', 'delta_b_gate', 'delta_ssm_lambda_re', 'delta_ssm_lambda_im', 'delta_ssm_log_dt', 'delta_ssm_b_re', 'delta_ssm_b_im', 'delta_ssm_c_re', 'delta_ssm_c_im', 'delta_ssm_d', 'delta_w_ssm_glu', 'delta_w_att_up', 'delta_conv_w', 'delta_conv_b', 'delta_conv_ln_g', 'delta_conv_ln_b', 'delta_w_conv_pw2', 'delta_w_out', 'delta_norm2_g', 'delta_w_ffn_in', 'delta_w_ffn_out', 'delta_final_g', 'new_m_norm1_g', 'new_m_w_in', 'new_m_b_gate', 'new_m_ssm_lambda_re', 'new_m_ssm_lambda_im', 'new_m_ssm_log_dt', 'new_m_ssm_b_re', 'new_m_ssm_b_im', 'new_m_ssm_c_re', 'new_m_ssm_c_im', 'new_m_ssm_d', 'new_m_w_ssm_glu', 'new_m_w_att_up', 'new_m_conv_w', 'new_m_conv_b', 'new_m_conv_ln_g', 'new_m_conv_ln_b', 'new_m_w_conv_pw2', 'new_m_w_out', 'new_m_norm2_g', 'new_m_w_ffn_in', 'new_m_w_ffn_out', 'new_m_final_g', 'new_v_norm1_g', 'new_v_w_in', 'new_v_b_gate', 'new_v_ssm_lambda_re', 'new_v_ssm_lambda_im', 'new_v_ssm_log_dt', 'new_v_ssm_b_re', 'new_v_ssm_b_im', 'new_v_ssm_c_re', 'new_v_ssm_c_im', 'new_v_ssm_d', 'new_v_w_ssm_glu', 'new_v_w_att_up', 'new_v_conv_w', 'new_v_conv_b', 'new_v_conv_ln_g', 'new_v_conv_ln_b', 'new_v_w_conv_pw2', 'new_v_w_out', 'new_v_norm2_g', 'new_v_w_ffn_in', 'new_v_w_ffn_out', 'new_v_final_g']
TWIN_LEAF_KINDS = {'loss': 'loss', 'grad_x': 'grad_x', 'grad_norm1_g': 'grad_w', 'grad_w_in': 'grad_w', 'grad_b_gate': 'grad_w', 'grad_ssm_lambda_re': 'grad_w', 'grad_ssm_lambda_im': 'grad_w', 'grad_ssm_log_dt': 'grad_w', 'grad_ssm_b_re': 'grad_w', 'grad_ssm_b_im': 'grad_w', 'grad_ssm_c_re': 'grad_w', 'grad_ssm_c_im': 'grad_w', 'grad_ssm_d': 'grad_w', 'grad_w_ssm_glu': 'grad_w', 'grad_w_att_up': 'grad_w', 'grad_conv_w': 'grad_w', 'grad_conv_b': 'grad_w', 'grad_conv_ln_g': 'grad_w', 'grad_conv_ln_b': 'grad_w', 'grad_w_conv_pw2': 'grad_w', 'grad_w_out': 'grad_w', 'grad_norm2_g': 'grad_w', 'grad_w_ffn_in': 'grad_w', 'grad_w_ffn_out': 'grad_w', 'grad_final_g': 'grad_w', 'delta_norm1_g': 'delta_w', 'delta_w_in': 'delta_w', 'delta_b_gate': 'delta_w', 'delta_ssm_lambda_re': 'delta_w', 'delta_ssm_lambda_im': 'delta_w', 'delta_ssm_log_dt': 'delta_w', 'delta_ssm_b_re': 'delta_w', 'delta_ssm_b_im': 'delta_w', 'delta_ssm_c_re': 'delta_w', 'delta_ssm_c_im': 'delta_w', 'delta_ssm_d': 'delta_w', 'delta_w_ssm_glu': 'delta_w', 'delta_w_att_up': 'delta_w', 'delta_conv_w': 'delta_w', 'delta_conv_b': 'delta_w', 'delta_conv_ln_g': 'delta_w', 'delta_conv_ln_b': 'delta_w', 'delta_w_conv_pw2': 'delta_w', 'delta_w_out': 'delta_w', 'delta_norm2_g': 'delta_w', 'delta_w_ffn_in': 'delta_w', 'delta_w_ffn_out': 'delta_w', 'delta_final_g': 'delta_w', 'new_m_norm1_g': 'new_m', 'new_m_w_in': 'new_m', 'new_m_b_gate': 'new_m', 'new_m_ssm_lambda_re': 'new_m', 'new_m_ssm_lambda_im': 'new_m', 'new_m_ssm_log_dt': 'new_m', 'new_m_ssm_b_re': 'new_m', 'new_m_ssm_b_im': 'new_m', 'new_m_ssm_c_re': 'new_m', 'new_m_ssm_c_im': 'new_m', 'new_m_ssm_d': 'new_m', 'new_m_w_ssm_glu': 'new_m', 'new_m_w_att_up': 'new_m', 'new_m_conv_w': 'new_m', 'new_m_conv_b': 'new_m', 'new_m_conv_ln_g': 'new_m', 'new_m_conv_ln_b': 'new_m', 'new_m_w_conv_pw2': 'new_m', 'new_m_w_out': 'new_m', 'new_m_norm2_g': 'new_m', 'new_m_w_ffn_in': 'new_m', 'new_m_w_ffn_out': 'new_m', 'new_m_final_g': 'new_m', 'new_v_norm1_g': 'new_v', 'new_v_w_in': 'new_v', 'new_v_b_gate': 'new_v', 'new_v_ssm_lambda_re': 'new_v', 'new_v_ssm_lambda_im': 'new_v', 'new_v_ssm_log_dt': 'new_v', 'new_v_ssm_b_re': 'new_v', 'new_v_ssm_b_im': 'new_v', 'new_v_ssm_c_re': 'new_v', 'new_v_ssm_c_im': 'new_v', 'new_v_ssm_d': 'new_v', 'new_v_w_ssm_glu': 'new_v', 'new_v_w_att_up': 'new_v', 'new_v_conv_w': 'new_v', 'new_v_conv_b': 'new_v', 'new_v_conv_ln_g': 'new_v', 'new_v_conv_ln_b': 'new_v', 'new_v_w_conv_pw2': 'new_v', 'new_v_w_out': 'new_v', 'new_v_norm2_g': 'new_v', 'new_v_w_ffn_in': 'new_v', 'new_v_w_ffn_out': 'new_v', 'new_v_final_g': 'new_v'}


def _forward(args):
    return _fwd_reference(*[args[k] for k in FWD_PARAMS])


def _output_shape():
    out = _jax.eval_shape(lambda: _forward(_fwd_setup_inputs(0)))
    return out.shape, out.dtype

N_MICROBATCH = 1
ADAM_LR = 0.001
ADAM_B1 = 0.9
ADAM_B2 = 0.999
ADAM_EPS = 1e-08
ADAM_WD = 0.01
ADAM_STEP = 10
PER_EXAMPLE_BATCH_AXIS = {'x': 0, 'loss_target': 0}
SHARED_INPUTS = []
_WEIGHT_DTYPES = {'norm1_g': _jnp.float32, 'w_in': _jnp.float32, 'b_gate': _jnp.float32, 'ssm_lambda_re': _jnp.float32, 'ssm_lambda_im': _jnp.float32, 'ssm_log_dt': _jnp.float32, 'ssm_b_re': _jnp.float32, 'ssm_b_im': _jnp.float32, 'ssm_c_re': _jnp.float32, 'ssm_c_im': _jnp.float32, 'ssm_d': _jnp.float32, 'w_ssm_glu': _jnp.float32, 'w_att_up': _jnp.float32, 'conv_w': _jnp.float32, 'conv_b': _jnp.float32, 'conv_ln_g': _jnp.float32, 'conv_ln_b': _jnp.float32, 'w_conv_pw2': _jnp.float32, 'w_out': _jnp.float32, 'norm2_g': _jnp.float32, 'w_ffn_in': _jnp.float32, 'w_ffn_out': _jnp.float32, 'final_g': _jnp.float32}
MOMENT_SCALE = {'norm1_g': 1.232972e-01, 'w_in': 4.628610e-02, 'b_gate': 2.388657e-02, 'ssm_lambda_re': 1.661114e-02, 'ssm_lambda_im': 1.841255e-02, 'ssm_log_dt': 8.438828e+00, 'ssm_b_re': 1.068083e-02, 'ssm_b_im': 1.090700e-02, 'ssm_c_re': 5.289609e-03, 'ssm_c_im': 5.424982e-03, 'ssm_d': 9.153582e-02, 'w_ssm_glu': 4.218429e-02, 'w_att_up': 3.212928e-02, 'conv_w': 1.225719e-01, 'conv_b': 2.809790e-01, 'conv_ln_g': 1.643320e-01, 'conv_ln_b': 1.822972e-01, 'w_conv_pw2': 9.106225e-02, 'w_out': 1.052280e-01, 'norm2_g': 1.769795e-01, 'w_ffn_in': 7.542381e-02, 'w_ffn_out': 1.238880e-01, 'final_g': 6.392474e+01}


def _to_microbatches(a, axis):
    t = _jnp.moveaxis(a, axis, 0)
    t = t.reshape((N_MICROBATCH, t.shape[0] // N_MICROBATCH) + t.shape[1:])
    return _jnp.moveaxis(t, 1, axis + 1)


def setup_inputs(seed: int = 0) -> dict:
    inp = _fwd_setup_inputs(seed)
    key = _jax.random.fold_in(_jax.random.key(seed), 7919)
    shape, _ = _output_shape()
    out = dict(inp)
    out["loss_target"] = _jax.random.normal(_jax.random.fold_in(key, 0), shape, _jnp.float32)
    for i, name in enumerate(TWIN_WEIGHTS):
        w = inp[name].astype(_jnp.float32)
        if MOMENT_SCALE is None:
            s = _jnp.sqrt(_jnp.mean(_jnp.square(w)) + 1e-30)
        else:
            s = MOMENT_SCALE[name]
        km, kv = _jax.random.split(_jax.random.fold_in(key, i + 1))
        out[name] = w
        out["m_" + name] = s * _jax.random.normal(km, w.shape, _jnp.float32)
        out["v_" + name] = (s * s) * _jax.random.uniform(kv, w.shape, _jnp.float32, 0.5, 1.5)
    if N_MICROBATCH > 1:
        for name, axis in PER_EXAMPLE_BATCH_AXIS.items():
            out[name] = _to_microbatches(out[name], axis)
    return {'x': out['x'], 'norm1_g': out['norm1_g'], 'w_in': out['w_in'], 'b_gate': out['b_gate'], 'ssm_lambda_re': out['ssm_lambda_re'], 'ssm_lambda_im': out['ssm_lambda_im'], 'ssm_log_dt': out['ssm_log_dt'], 'ssm_b_re': out['ssm_b_re'], 'ssm_b_im': out['ssm_b_im'], 'ssm_c_re': out['ssm_c_re'], 'ssm_c_im': out['ssm_c_im'], 'ssm_d': out['ssm_d'], 'w_ssm_glu': out['w_ssm_glu'], 'w_att_up': out['w_att_up'], 'conv_w': out['conv_w'], 'conv_b': out['conv_b'], 'conv_ln_g': out['conv_ln_g'], 'conv_ln_b': out['conv_ln_b'], 'w_conv_pw2': out['w_conv_pw2'], 'w_out': out['w_out'], 'norm2_g': out['norm2_g'], 'w_ffn_in': out['w_ffn_in'], 'w_ffn_out': out['w_ffn_out'], 'final_g': out['final_g'], 'loss_target': out['loss_target'], 'm_norm1_g': out['m_norm1_g'], 'm_w_in': out['m_w_in'], 'm_b_gate': out['m_b_gate'], 'm_ssm_lambda_re': out['m_ssm_lambda_re'], 'm_ssm_lambda_im': out['m_ssm_lambda_im'], 'm_ssm_log_dt': out['m_ssm_log_dt'], 'm_ssm_b_re': out['m_ssm_b_re'], 'm_ssm_b_im': out['m_ssm_b_im'], 'm_ssm_c_re': out['m_ssm_c_re'], 'm_ssm_c_im': out['m_ssm_c_im'], 'm_ssm_d': out['m_ssm_d'], 'm_w_ssm_glu': out['m_w_ssm_glu'], 'm_w_att_up': out['m_w_att_up'], 'm_conv_w': out['m_conv_w'], 'm_conv_b': out['m_conv_b'], 'm_conv_ln_g': out['m_conv_ln_g'], 'm_conv_ln_b': out['m_conv_ln_b'], 'm_w_conv_pw2': out['m_w_conv_pw2'], 'm_w_out': out['m_w_out'], 'm_norm2_g': out['m_norm2_g'], 'm_w_ffn_in': out['m_w_ffn_in'], 'm_w_ffn_out': out['m_w_ffn_out'], 'm_final_g': out['m_final_g'], 'v_norm1_g': out['v_norm1_g'], 'v_w_in': out['v_w_in'], 'v_b_gate': out['v_b_gate'], 'v_ssm_lambda_re': out['v_ssm_lambda_re'], 'v_ssm_lambda_im': out['v_ssm_lambda_im'], 'v_ssm_log_dt': out['v_ssm_log_dt'], 'v_ssm_b_re': out['v_ssm_b_re'], 'v_ssm_b_im': out['v_ssm_b_im'], 'v_ssm_c_re': out['v_ssm_c_re'], 'v_ssm_c_im': out['v_ssm_c_im'], 'v_ssm_d': out['v_ssm_d'], 'v_w_ssm_glu': out['v_w_ssm_glu'], 'v_w_att_up': out['v_w_att_up'], 'v_conv_w': out['v_conv_w'], 'v_conv_b': out['v_conv_b'], 'v_conv_ln_g': out['v_conv_ln_g'], 'v_conv_ln_b': out['v_conv_ln_b'], 'v_w_conv_pw2': out['v_w_conv_pw2'], 'v_w_out': out['v_w_out'], 'v_norm2_g': out['v_norm2_g'], 'v_w_ffn_in': out['v_w_ffn_in'], 'v_w_ffn_out': out['v_w_ffn_out'], 'v_final_g': out['v_final_g']}


def _loss(weights, diff, rest, loss_target):
    with _jax.named_scope("forward"):
        args = {**rest, TWIN_DIFF_INPUT: diff, **{k: w.astype(_WEIGHT_DTYPES[k]) for k, w in weights.items()}}
        y = _forward(args)
    with _jax.named_scope("loss_head"):
        err = _jnp.square(y.astype(_jnp.float32) - loss_target)
        return 0.5 * _jnp.sum(_jnp.mean(err, axis=-1)) if err.ndim else 0.5 * err


def _adamw(w, g, m, v):
    m = ADAM_B1 * m + (1.0 - ADAM_B1) * g
    v = ADAM_B2 * v + (1.0 - ADAM_B2) * _jnp.square(g)
    m_hat = m / (1.0 - ADAM_B1 ** ADAM_STEP)
    v_hat = v / (1.0 - ADAM_B2 ** ADAM_STEP)
    delta = -ADAM_LR * (m_hat / (_jnp.sqrt(v_hat) + ADAM_EPS) + ADAM_WD * w)
    return delta, m, v


def reference(x, norm1_g, w_in, b_gate, ssm_lambda_re, ssm_lambda_im, ssm_log_dt, ssm_b_re, ssm_b_im, ssm_c_re, ssm_c_im, ssm_d, w_ssm_glu, w_att_up, conv_w, conv_b, conv_ln_g, conv_ln_b, w_conv_pw2, w_out, norm2_g, w_ffn_in, w_ffn_out, final_g, loss_target, m_norm1_g, m_w_in, m_b_gate, m_ssm_lambda_re, m_ssm_lambda_im, m_ssm_log_dt, m_ssm_b_re, m_ssm_b_im, m_ssm_c_re, m_ssm_c_im, m_ssm_d, m_w_ssm_glu, m_w_att_up, m_conv_w, m_conv_b, m_conv_ln_g, m_conv_ln_b, m_w_conv_pw2, m_w_out, m_norm2_g, m_w_ffn_in, m_w_ffn_out, m_final_g, v_norm1_g, v_w_in, v_b_gate, v_ssm_lambda_re, v_ssm_lambda_im, v_ssm_log_dt, v_ssm_b_re, v_ssm_b_im, v_ssm_c_re, v_ssm_c_im, v_ssm_d, v_w_ssm_glu, v_w_att_up, v_conv_w, v_conv_b, v_conv_ln_g, v_conv_ln_b, v_w_conv_pw2, v_w_out, v_norm2_g, v_w_ffn_in, v_w_ffn_out, v_final_g):
    given = dict(x=x, norm1_g=norm1_g, w_in=w_in, b_gate=b_gate, ssm_lambda_re=ssm_lambda_re, ssm_lambda_im=ssm_lambda_im, ssm_log_dt=ssm_log_dt, ssm_b_re=ssm_b_re, ssm_b_im=ssm_b_im, ssm_c_re=ssm_c_re, ssm_c_im=ssm_c_im, ssm_d=ssm_d, w_ssm_glu=w_ssm_glu, w_att_up=w_att_up, conv_w=conv_w, conv_b=conv_b, conv_ln_g=conv_ln_g, conv_ln_b=conv_ln_b, w_conv_pw2=w_conv_pw2, w_out=w_out, norm2_g=norm2_g, w_ffn_in=w_ffn_in, w_ffn_out=w_ffn_out, final_g=final_g, loss_target=loss_target, m_norm1_g=m_norm1_g, m_w_in=m_w_in, m_b_gate=m_b_gate, m_ssm_lambda_re=m_ssm_lambda_re, m_ssm_lambda_im=m_ssm_lambda_im, m_ssm_log_dt=m_ssm_log_dt, m_ssm_b_re=m_ssm_b_re, m_ssm_b_im=m_ssm_b_im, m_ssm_c_re=m_ssm_c_re, m_ssm_c_im=m_ssm_c_im, m_ssm_d=m_ssm_d, m_w_ssm_glu=m_w_ssm_glu, m_w_att_up=m_w_att_up, m_conv_w=m_conv_w, m_conv_b=m_conv_b, m_conv_ln_g=m_conv_ln_g, m_conv_ln_b=m_conv_ln_b, m_w_conv_pw2=m_w_conv_pw2, m_w_out=m_w_out, m_norm2_g=m_norm2_g, m_w_ffn_in=m_w_ffn_in, m_w_ffn_out=m_w_ffn_out, m_final_g=m_final_g, v_norm1_g=v_norm1_g, v_w_in=v_w_in, v_b_gate=v_b_gate, v_ssm_lambda_re=v_ssm_lambda_re, v_ssm_lambda_im=v_ssm_lambda_im, v_ssm_log_dt=v_ssm_log_dt, v_ssm_b_re=v_ssm_b_re, v_ssm_b_im=v_ssm_b_im, v_ssm_c_re=v_ssm_c_re, v_ssm_c_im=v_ssm_c_im, v_ssm_d=v_ssm_d, v_w_ssm_glu=v_w_ssm_glu, v_w_att_up=v_w_att_up, v_conv_w=v_conv_w, v_conv_b=v_conv_b, v_conv_ln_g=v_conv_ln_g, v_conv_ln_b=v_conv_ln_b, v_w_conv_pw2=v_w_conv_pw2, v_w_out=v_w_out, v_norm2_g=v_norm2_g, v_w_ffn_in=v_w_ffn_in, v_w_ffn_out=v_w_ffn_out, v_final_g=v_final_g)
    weights = {n: given[n] for n in TWIN_WEIGHTS}
    shared = {n: given[n] for n in SHARED_INPUTS}
    per_example = {n: given[n] for n in ['x']}
    grad_fn = _jax.value_and_grad(_loss, argnums=(0, 1))

    def one_microbatch(ex, loss_target):
        ex = dict(ex)
        diff = ex.pop(TWIN_DIFF_INPUT)
        return grad_fn(weights, diff, {**shared, **ex}, loss_target)

    if N_MICROBATCH == 1:
        loss, (grad_w, grad_x) = one_microbatch(per_example, given["loss_target"])
    else:
        def body(carry, xs):
            loss_sum, grad_sum = carry
            l_k, (gw_k, gx_k) = one_microbatch(xs[0], xs[1])
            with _jax.named_scope("update"):
                return (loss_sum + l_k, _jax.tree.map(_jnp.add, grad_sum, gw_k)), gx_k

        init = (_jnp.zeros((), _jnp.float32), _jax.tree.map(_jnp.zeros_like, weights))
        (loss, grad_w), grad_x = _jax.lax.scan(body, init, (per_example, given["loss_target"]))
    with _jax.named_scope("update"):
        delta_w, new_m, new_v = {}, {}, {}
        for n in TWIN_WEIGHTS:
            delta_w[n], new_m[n], new_v[n] = _adamw(weights[n], grad_w[n], given["m_" + n], given["v_" + n])
    return (loss, grad_x, *[grad_w[n] for n in TWIN_WEIGHTS], *[delta_w[n] for n in TWIN_WEIGHTS],
            *[new_m[n] for n in TWIN_WEIGHTS], *[new_v[n] for n in TWIN_WEIGHTS])
```

```python
import functools
import math

import jax
import jax.numpy as jnp
from jax import lax
from jax.experimental import pallas as pl
from jax.experimental.pallas import tpu as pltpu

F32 = jnp.float32
BF16 = jnp.bfloat16
VMEM_LIMIT = 56 * 1024 * 1024


def _cparams(sem):
    return pltpu.CompilerParams(dimension_semantics=sem, vmem_limit_bytes=VMEM_LIMIT)


def _pick(n, pref):
    for t in pref:
        if n % t == 0:
            return t
    return n


_DIMS = {"nn": (((1,), (0,)), ((), ())), "nt": (((1,), (1,)), ((), ())), "tn": (((0,), (0,)), ((), ()))}


def _mm(a, b, form, out_dtype, name, res=None, tm=None, tn=None, tk=None):
    if form == "nn":
        (m, kd), (_, n) = a.shape, b.shape
    elif form == "nt":
        (m, kd), (n, _) = a.shape, b.shape
    else:
        (kd, m), (_, n) = a.shape, b.shape
    tm = tm or _pick(m, (512, 256, 128))
    tn = tn or _pick(n, (512, 256, 128))
    tk = tk or _pick(kd, (1024, 1408, 512, 256, 128))
    nk = kd // tk
    a_spec = {"nn": pl.BlockSpec((tm, tk), lambda i, j, k: (i, k)),
              "nt": pl.BlockSpec((tm, tk), lambda i, j, k: (i, k)),
              "tn": pl.BlockSpec((tk, tm), lambda i, j, k: (k, i))}[form]
    b_spec = {"nn": pl.BlockSpec((tk, tn), lambda i, j, k: (k, j)),
              "nt": pl.BlockSpec((tn, tk), lambda i, j, k: (j, k)),
              "tn": pl.BlockSpec((tk, tn), lambda i, j, k: (k, j))}[form]
    o_spec = pl.BlockSpec((tm, tn), lambda i, j, k: (i, j))
    dims = _DIMS[form]

    def body(*refs):
        if res is None:
            a_ref, b_ref, o_ref, acc = refs
            r_ref = None
        else:
            a_ref, b_ref, r_ref, o_ref, acc = refs
        k = pl.program_id(2)
        p = lax.dot_general(a_ref[...].astype(BF16), b_ref[...].astype(BF16), dims, preferred_element_type=F32)

        @pl.when(k == 0)
        def _():
            acc[...] = p

        @pl.when(k > 0)
        def _():
            acc[...] += p

        @pl.when(k == nk - 1)
        def _():
            r = acc[...]
            if r_ref is not None:
                r = r + r_ref[...]
            o_ref[...] = r.astype(out_dtype)

    ins = [a, b] + ([] if res is None else [res])
    in_specs = [a_spec, b_spec] + ([] if res is None else [o_spec])
    return pl.pallas_call(
        body, name=name, grid=(m // tm, n // tn, nk), in_specs=in_specs, out_specs=o_spec,
        out_shape=jax.ShapeDtypeStruct((m, n), out_dtype), scratch_shapes=[pltpu.VMEM((tm, tn), F32)],
        compiler_params=_cparams(("parallel", "parallel", "arbitrary")))(*ins)


def _rowwise(fn, name, n_rows, ins, outs, accs=(), tm=512):
    n_in, n_out, n_acc = len(ins), len(outs), len(accs)
    in_specs, args = [], []
    for spec in ins:
        if spec[0] == "row":
            _, arr, w, cb = spec
            in_specs.append(pl.BlockSpec((tm, w), lambda i, cb=cb: (i, cb)))
        else:
            arr = spec[1]
            in_specs.append(pl.BlockSpec(arr.shape, lambda i: (0, 0)))
        args.append(arr)
    out_specs = [pl.BlockSpec((tm, w), lambda i: (i, 0)) for w, _ in outs]
    out_specs += [pl.BlockSpec((1, w), lambda i: (0, 0)) for w in accs]
    out_shape = [jax.ShapeDtypeStruct((n_rows, w), dt) for w, dt in outs]
    out_shape += [jax.ShapeDtypeStruct((1, w), F32) for w in accs]

    def body(*refs):
        i = pl.program_id(0)
        res = fn(*[r[...] for r in refs[:n_in]])
        for o_ref, r in zip(refs[n_in:n_in + n_out], res[:n_out]):
            o_ref[...] = r.astype(o_ref.dtype)
        for a_ref, r in zip(refs[n_in + n_out:], res[n_out:]):
            @pl.when(i == 0)
            def _(a_ref=a_ref, r=r):
                a_ref[...] = r

            @pl.when(i > 0)
            def _(a_ref=a_ref, r=r):
                a_ref[...] += r

    return pl.pallas_call(
        body, name=name, grid=(n_rows // tm,), in_specs=in_specs, out_specs=out_specs, out_shape=out_shape,
        compiler_params=_cparams(("arbitrary",)))(*args)


EPS = 1e-6


def _sig(x):
    return 1.0 / (1.0 + jnp.exp(-x))


def _colsum(x):
    return jnp.sum(x, axis=0, keepdims=True)


def _rms_fwd_fn(x, g):
    r = lax.rsqrt(jnp.mean(x * x, axis=-1, keepdims=True) + EPS)
    return (x * r * g,)


def _rms_bwd_fn(x, g, dh, dres):
    dh = dh.astype(F32)
    r = lax.rsqrt(jnp.mean(x * x, axis=-1, keepdims=True) + EPS)
    xh = x * r
    dyg = dh * g
    dx = r * (dyg - xh * jnp.mean(dyg * xh, axis=-1, keepdims=True)) + dres
    return dx, _colsum(dh * xh)


def _loss_fn(x, g, t):
    d = x.shape[-1]
    r = lax.rsqrt(jnp.mean(x * x, axis=-1, keepdims=True) + EPS)
    xh = x * r
    err = xh * g - t
    dy = err * (1.0 / d)
    dyg = dy * g
    dx = r * (dyg - xh * jnp.mean(dyg * xh, axis=-1, keepdims=True))
    return dx, _colsum(err * err), _colsum(dy * xh)


def _swiglu_fwd_fn(z):
    f = z.shape[-1] // 2
    z1, z2 = z[:, :f].astype(F32), z[:, f:].astype(F32)
    return (z1 * _sig(z1) * z2,)


def _swiglu_bwd_fn(z, da):
    f = z.shape[-1] // 2
    z1, z2, da = z[:, :f].astype(F32), z[:, f:].astype(F32), da.astype(F32)
    s = _sig(z1)
    dz1 = da * z2 * (s * (1.0 + z1 * (1.0 - s)))
    dz2 = da * (z1 * s)
    return (jnp.concatenate([dz1, dz2], axis=1),)


def _merge_fwd_fn(g0, g1, g2, bg, zs, ya, yc):
    d = ya.shape[-1]
    bg = bg.astype(F32)
    zs = zs.astype(F32)
    ys = zs[:, :d] * _sig(zs[:, d:])
    m = _sig(g0.astype(F32) + bg[:, :d]) * ys
    m = m + _sig(g1.astype(F32) + bg[:, d:2 * d]) * ya.astype(F32)
    m = m + _sig(g2.astype(F32) + bg[:, 2 * d:]) * yc.astype(F32)
    return (m,)


def _merge_bwd_fn(g0, g1, g2, bg, zs, ya, yc, dm):
    d = ya.shape[-1]
    bg = bg.astype(F32)
    zs = zs.astype(F32)
    dm = dm.astype(F32)
    z1, s2 = zs[:, :d], _sig(zs[:, d:])
    ys = z1 * s2
    s0 = _sig(g0.astype(F32) + bg[:, :d])
    s1 = _sig(g1.astype(F32) + bg[:, d:2 * d])
    s3 = _sig(g2.astype(F32) + bg[:, 2 * d:])
    dgl = jnp.concatenate([dm * ys * s0 * (1.0 - s0), dm * ya.astype(F32) * s1 * (1.0 - s1),
                           dm * yc.astype(F32) * s3 * (1.0 - s3)], axis=1)
    dys = dm * s0
    dzs = jnp.concatenate([dys * s2, dys * z1 * s2 * (1.0 - s2)], axis=1)
    return dgl, dzs, dm * s1, dm * s3, _colsum(dgl)


def _combine_fwd_fn(o0, o1, o2, l0, l1, l2):
    m = jnp.maximum(jnp.maximum(l0, l1), l2)
    e0, e1, e2 = jnp.exp(l0 - m), jnp.exp(l1 - m), jnp.exp(l2 - m)
    inv = 1.0 / (e0 + e1 + e2)
    return ((e0 * o0.astype(F32) + e1 * o1.astype(F32) + e2 * o2.astype(F32)) * inv,)


def _combine_bwd_fn(o0, o1, o2, l0, l1, l2, do, jmat):
    m = jnp.maximum(jnp.maximum(l0, l1), l2)
    e0, e1, e2 = jnp.exp(l0 - m), jnp.exp(l1 - m), jnp.exp(l2 - m)
    inv = 1.0 / (e0 + e1 + e2)
    w0, w1, w2 = e0 * inv, e1 * inv, e2 * inv
    do = do.astype(F32)

    def headsum(x):
        return jnp.dot(x, jmat, preferred_element_type=F32, precision=lax.Precision.HIGHEST)

    dw0, dw1, dw2 = headsum(do * o0.astype(F32)), headsum(do * o1.astype(F32)), headsum(do * o2.astype(F32))
    mean = w0 * dw0 + w1 * dw1 + w2 * dw2
    return w0 * do, w1 * do, w2 * do, w0 * (dw0 - mean), w1 * (dw1 - mean), w2 * (dw2 - mean)


ADAM_LR, ADAM_B1, ADAM_B2, ADAM_EPS, ADAM_WD, ADAM_STEP = 0.001, 0.9, 0.999, 1e-08, 0.01, 10


def _adamw_fn(w, g, m, v):
    m = ADAM_B1 * m + (1.0 - ADAM_B1) * g
    v = ADAM_B2 * v + (1.0 - ADAM_B2) * (g * g)
    m_hat = m / (1.0 - ADAM_B1 ** ADAM_STEP)
    v_hat = v / (1.0 - ADAM_B2 ** ADAM_STEP)
    delta = -ADAM_LR * (m_hat / (jnp.sqrt(v_hat) + ADAM_EPS) + ADAM_WD * w)
    return delta, m, v


CONV_WIDTH = 31


def _conv_fwd(proj, cb, w32, conv_b, ln_g, ln_b, bl, c, name, tm=512):
    n = proj.shape[0]
    hp = (CONV_WIDTH - 1) * bl
    nt = n // tm

    def body(ap_ref, gp_ref, a_ref, g_ref, w_ref, cb_ref, lg_ref, lb_ref, hc_ref, hconv_ref, ext):
        i = pl.program_id(0)
        ext[pl.ds(hp, tm), :] = a_ref[...].astype(F32) * _sig(g_ref[...].astype(F32))
        hgp = ap_ref[pl.ds(tm - hp, hp), :].astype(F32) * _sig(gp_ref[pl.ds(tm - hp, hp), :].astype(F32))
        ext[pl.ds(0, hp), :] = jnp.where(i > 0, hgp, 0.0)
        acc = jnp.zeros((tm, c), F32) + cb_ref[...]
        for j in range(CONV_WIDTH):
            acc = acc + w_ref[j:j + 1, :] * ext[pl.ds(j * bl, tm), :]
        hconv_ref[...] = acc.astype(hconv_ref.dtype)
        h = hconv_ref[...].astype(F32)
        mu = jnp.mean(h, axis=-1, keepdims=True)
        xc = h - mu
        var = jnp.mean(xc * xc, axis=-1, keepdims=True)
        hn = xc * lax.rsqrt(var + EPS) * lg_ref[...] + lb_ref[...]
        hc_ref[...] = (hn * _sig(hn)).astype(hc_ref.dtype)

    prev = lambda i, k: (jnp.maximum(i - 1, 0), k)
    par = lambda arr: pl.BlockSpec(arr.shape, lambda i: (0, 0))
    return pl.pallas_call(
        body, name=name, grid=(nt,),
        in_specs=[pl.BlockSpec((tm, c), functools.partial(prev, k=cb)), pl.BlockSpec((tm, c), functools.partial(prev, k=cb + 1)),
                  pl.BlockSpec((tm, c), lambda i: (i, cb)), pl.BlockSpec((tm, c), lambda i: (i, cb + 1)),
                  par(w32), par(conv_b), par(ln_g), par(ln_b)],
        out_specs=[pl.BlockSpec((tm, c), lambda i: (i, 0))] * 2,
        out_shape=[jax.ShapeDtypeStruct((n, c), BF16)] * 2,
        scratch_shapes=[pltpu.VMEM((hp + tm, c), F32)],
        compiler_params=_cparams(("arbitrary",)))(proj, proj, proj, proj, w32, conv_b, ln_g, ln_b)


def _conv_bwd(proj, cb, dhc, hconv, w32, ln_g, ln_b, bl, c, name, tm=512):
    n = proj.shape[0]
    hp = (CONV_WIDTH - 1) * bl
    nt = n // tm

    def ln_bwd(d, h, lg, lb):
        d, h = d.astype(F32), h.astype(F32)
        mu = jnp.mean(h, axis=-1, keepdims=True)
        xc = h - mu
        rstd = lax.rsqrt(jnp.mean(xc * xc, axis=-1, keepdims=True) + EPS)
        xh = xc * rstd
        hn = xh * lg + lb
        s = _sig(hn)
        dhn = d * (s * (1.0 + hn * (1.0 - s)))
        dxh = dhn * lg
        dh = rstd * (dxh - jnp.mean(dxh, axis=-1, keepdims=True) - xh * jnp.mean(dxh * xh, axis=-1, keepdims=True))
        return dh, dhn, xh

    def body(ap_ref, gp_ref, a_ref, g_ref, d_ref, dn_ref, h_ref, hn_ref, w_ref, lg_ref, lb_ref,
             dcv_ref, dw_ref, dcb_ref, dlg_ref, dlb_ref, ext_h, ext_d):
        i = pl.program_id(0)
        lg, lb = lg_ref[...], lb_ref[...]
        a, g = a_ref[...].astype(F32), g_ref[...].astype(F32)
        sg = _sig(g)
        ext_h[pl.ds(hp, tm), :] = a * sg
        hgp = ap_ref[pl.ds(tm - hp, hp), :].astype(F32) * _sig(gp_ref[pl.ds(tm - hp, hp), :].astype(F32))
        ext_h[pl.ds(0, hp), :] = jnp.where(i > 0, hgp, 0.0)
        dh, dhn, xh = ln_bwd(d_ref[...], h_ref[...], lg, lb)
        ext_d[pl.ds(0, tm), :] = dh
        dh_n, _, _ = ln_bwd(dn_ref[pl.ds(0, hp), :], hn_ref[pl.ds(0, hp), :], lg, lb)
        ext_d[pl.ds(tm, hp), :] = jnp.where(i < nt - 1, dh_n, 0.0)

        @pl.when(i == 0)
        def _():
            dw_ref[...] = jnp.zeros_like(dw_ref)
            dcb_ref[...] = jnp.zeros_like(dcb_ref)
            dlg_ref[...] = jnp.zeros_like(dlg_ref)
            dlb_ref[...] = jnp.zeros_like(dlb_ref)

        dcb_ref[...] += _colsum(dh)
        dlg_ref[...] += _colsum(dhn * xh)
        dlb_ref[...] += _colsum(dhn)
        dhg = jnp.zeros((tm, c), F32)
        for j in range(CONV_WIDTH):
            dhg = dhg + w_ref[j:j + 1, :] * ext_d[pl.ds((CONV_WIDTH - 1 - j) * bl, tm), :]
            dw_ref[j:j + 1, :] += _colsum(dh * ext_h[pl.ds(j * bl, tm), :])
        dcv_ref[...] = jnp.concatenate([dhg * sg, dhg * a * sg * (1.0 - sg)], axis=1).astype(dcv_ref.dtype)

    prev = lambda i, k: (jnp.maximum(i - 1, 0), k)
    nxt = lambda i: (jnp.minimum(i + 1, nt - 1), 0)
    cur = lambda i: (i, 0)
    par = lambda arr: pl.BlockSpec(arr.shape, lambda i: (0, 0))
    acc = lambda r: pl.BlockSpec((r, c), lambda i: (0, 0))
    return pl.pallas_call(
        body, name=name, grid=(nt,),
        in_specs=[pl.BlockSpec((tm, c), functools.partial(prev, k=cb)), pl.BlockSpec((tm, c), functools.partial(prev, k=cb + 1)),
                  pl.BlockSpec((tm, c), lambda i: (i, cb)), pl.BlockSpec((tm, c), lambda i: (i, cb + 1)),
                  pl.BlockSpec((tm, c), cur), pl.BlockSpec((tm, c), nxt), pl.BlockSpec((tm, c), cur), pl.BlockSpec((tm, c), nxt),
                  par(w32), par(ln_g), par(ln_b)],
        out_specs=[pl.BlockSpec((tm, 2 * c), cur), acc(32), acc(1), acc(1), acc(1)],
        out_shape=[jax.ShapeDtypeStruct((n, 2 * c), BF16), jax.ShapeDtypeStruct((32, c), F32)] + [jax.ShapeDtypeStruct((1, c), F32)] * 3,
        scratch_shapes=[pltpu.VMEM((hp + tm, c), F32), pltpu.VMEM((hp + tm, c), F32)],
        compiler_params=_cparams(("arbitrary",)))(proj, proj, proj, proj, dhc, dhc, hconv, hconv, w32, ln_g, ln_b)


SSM_CH = 128
_GELU_C = 0.7978845608028654


def _gelu(x):
    return 0.5 * x * (1.0 + jnp.tanh(_GELU_C * (x + 0.044715 * x * x * x)))


def _gelu_grad(x):
    th = jnp.tanh(_GELU_C * (x + 0.044715 * x * x * x))
    return 0.5 * (1.0 + th) + 0.5 * x * (1.0 - th * th) * (_GELU_C * (1.0 + 3.0 * 0.044715 * x * x))


def _ssm_disc(lam_re, lam_im, log_dt, b_re, b_im):
    dt = jnp.exp(log_dt)[:, None]
    mag = jnp.exp(lam_re * dt)
    ab_re = mag * jnp.cos(lam_im * dt)
    ab_im = mag * jnp.sin(lam_im * dt)
    nr, ni = ab_re - 1.0, ab_im
    den = lam_re * lam_re + lam_im * lam_im
    z_re = ((nr * lam_re + ni * lam_im) / den)[..., None]
    z_im = ((ni * lam_re - nr * lam_im) / den)[..., None]
    return ab_re, ab_im, z_re * b_re - z_im * b_im, z_re * b_im + z_im * b_re


def _ssm_pack(ab_re, ab_im, bb_re, bb_im, c_re, c_im):
    g, p, h = bb_re.shape
    gc = SSM_CH // h
    nc = g // gc
    eye = jnp.eye(gc, dtype=F32)
    blk = lambda x: jnp.einsum("qgph,gk->qghkp", x.reshape(nc, gc, p, h), eye).reshape(nc, gc * h, gc * p)
    bbd = jnp.concatenate([blk(bb_re), blk(bb_im)], axis=2).astype(BF16)
    blc = lambda x: jnp.einsum("qghp,gk->qgpkh", x.reshape(nc, gc, h, p), eye).reshape(nc, gc * p, gc * h)
    cdm = jnp.concatenate([blc(c_re), blc(-c_im)], axis=1).astype(BF16)
    a = jnp.concatenate([ab_re.reshape(nc, gc * p), ab_im.reshape(nc, gc * p)], axis=1)
    a8 = jnp.broadcast_to(a[:, None, :], (nc, 8, 2 * gc * p)).reshape(nc * 8, 2 * gc * p)
    return bbd, cdm, a8


def _ssm_unpack(dbb, dcd, da, g, p, h):
    gc = SSM_CH // h
    nc = g // gc
    ph = gc * p
    eye = jnp.eye(gc, dtype=F32)
    dia = lambda x, o: jnp.einsum("qgpkh,gk->" + o, x.reshape(nc, gc, p, gc, h), eye).reshape((g, p, h) if o == "qgph" else (g, h, p))
    das = da.reshape(nc, 8, 2 * ph).sum(axis=1)
    return (das[:, :ph].reshape(g, p), das[:, ph:].reshape(g, p), dia(dbb[:, :ph], "qgph"), dia(dbb[:, ph:], "qgph"),
            dia(dcd[:, :ph], "qghp"), -dia(dcd[:, ph:], "qghp"))


def _ssm_fwd(proj, bbd, cdm, a8, dskip, bl, name, tm=1024):
    n = proj.shape[0]
    nc, ch, p2 = bbd.shape
    ph = p2 // 2
    nt = n // tm
    nsub = 8 // bl

    def body(u_ref, bb_ref, cd_ref, a_ref, d_ref, ypre_ref, yg_ref, s_ref, bu, carry):
        t = pl.program_id(1)

        @pl.when(t == 0)
        def _():
            carry[...] = jnp.zeros_like(carry)

        u = u_ref[...]
        bu[...] = jnp.dot(u, bb_ref[0], preferred_element_type=F32)
        a_re, a_im = a_ref[:, :ph], a_ref[:, ph:]
        row = lax.broadcasted_iota(jnp.int32, (8, ph), 0)

        def step(k, c):
            cre, cim = c
            r0 = pl.multiple_of(k * 8, 8)
            bre, bim = bu[pl.ds(r0, 8), :ph], bu[pl.ds(r0, 8), ph:]
            sre, sim = cre, cim
            for sub in range(nsub):
                xre, xim = pltpu.roll(cre, bl, 0), pltpu.roll(cim, bl, 0)
                cre = a_re * xre - a_im * xim + bre
                cim = a_re * xim + a_im * xre + bim
                if sub == 0:
                    sre, sim = cre, cim
                else:
                    sel = row >= sub * bl
                    sre, sim = jnp.where(sel, cre, sre), jnp.where(sel, cim, sim)
            bu[pl.ds(r0, 8), :ph] = sre
            bu[pl.ds(r0, 8), ph:] = sim
            return sre, sim

        cre, cim = lax.fori_loop(0, tm // 8, step, (carry[:, :ph], carry[:, ph:]))
        carry[:, :ph] = cre
        carry[:, ph:] = cim
        s16 = bu[...].astype(BF16)
        s_ref[...] = s16
        y = jnp.dot(s16, cd_ref[0], preferred_element_type=F32) + d_ref[...] * u.astype(F32)
        ypre_ref[...] = y
        yg_ref[...] = _gelu(y).astype(yg_ref.dtype)

    return pl.pallas_call(
        body, name=name, grid=(nc, nt),
        in_specs=[pl.BlockSpec((tm, ch), lambda q, t: (t, q)), pl.BlockSpec((1, ch, p2), lambda q, t: (q, 0, 0)),
                  pl.BlockSpec((1, p2, ch), lambda q, t: (q, 0, 0)), pl.BlockSpec((8, p2), lambda q, t: (q, 0)),
                  pl.BlockSpec((1, ch), lambda q, t: (0, q))],
        out_specs=[pl.BlockSpec((tm, ch), lambda q, t: (t, q)), pl.BlockSpec((tm, ch), lambda q, t: (t, q)),
                   pl.BlockSpec((tm, p2), lambda q, t: (t, q))],
        out_shape=[jax.ShapeDtypeStruct((n, nc * ch), F32), jax.ShapeDtypeStruct((n, nc * ch), BF16),
                   jax.ShapeDtypeStruct((n, nc * p2), BF16)],
        scratch_shapes=[pltpu.VMEM((tm, p2), F32), pltpu.VMEM((8, p2), F32)],
        compiler_params=_cparams(("parallel", "arbitrary")))(proj, bbd, cdm, a8, dskip)


def _ssm_bwd(dyg, ypre, proj, s_all, cdt, bbt, a8, dskip, bl, name, tm=1024):
    n = proj.shape[0]
    nc, ch, p2 = cdt.shape
    ph = p2 // 2
    nt = n // tm
    nsub = 8 // bl
    tn_dims = (((0,), (0,)), ((), ()))

    def body(dyg_ref, ypre_ref, u_ref, s_ref, cdt_ref, bbt_ref, a_ref, d_ref,
             du_ref, dbb_ref, dcd_ref, da_ref, dd_ref, ds, s32, carry):
        t = pl.program_id(1)

        @pl.when(t == 0)
        def _():
            carry[...] = jnp.zeros_like(carry)
            dbb_ref[...] = jnp.zeros_like(dbb_ref)
            dcd_ref[...] = jnp.zeros_like(dcd_ref)
            da_ref[...] = jnp.zeros_like(da_ref)
            dd_ref[...] = jnp.zeros_like(dd_ref)

        dyp = dyg_ref[...].astype(F32) * _gelu_grad(ypre_ref[...])
        u = u_ref[...]
        dd_ref[...] += _colsum(dyp * u.astype(F32))
        dyp16 = dyp.astype(BF16)
        ds[...] = jnp.dot(dyp16, cdt_ref[0], preferred_element_type=F32)
        s16 = s_ref[...]
        s32[...] = s16.astype(F32)
        a_re, a_im = a_ref[:, :ph], a_ref[:, ph:]
        row = lax.broadcasted_iota(jnp.int32, (8, ph), 0)
        back = 8 - bl

        def step(kk, c):
            lre, lim, acr, aci = c
            r0 = pl.multiple_of((tm // 8 - 1 - kk) * 8, 8)
            dre, dim = ds[pl.ds(r0, 8), :ph], ds[pl.ds(r0, 8), ph:]
            sre, sim = s32[pl.ds(r0, 8), :ph], s32[pl.ds(r0, 8), ph:]
            ore, oim, ire, iim = lre, lim, lre, lim
            for sub in range(nsub - 1, -1, -1):
                xre, xim = pltpu.roll(lre, back, 0), pltpu.roll(lim, back, 0)
                lre = a_re * xre + a_im * xim + dre
                lim = a_re * xim - a_im * xre + dim
                if sub == nsub - 1:
                    ore, oim, ire, iim = lre, lim, xre, xim
                else:
                    sel = row < (sub + 1) * bl
                    ore, oim = jnp.where(sel, lre, ore), jnp.where(sel, lim, oim)
                    ire, iim = jnp.where(sel, xre, ire), jnp.where(sel, xim, iim)
            ds[pl.ds(r0, 8), :ph] = ore
            ds[pl.ds(r0, 8), ph:] = oim
            acr = acr + sre * ire + sim * iim
            aci = aci + sre * iim - sim * ire
            return ore, oim, acr, aci

        z = jnp.zeros((8, ph), F32)
        lre, lim, acr, aci = lax.fori_loop(0, tm // 8, step, (carry[:, :ph], carry[:, ph:], z, z))
        carry[:, :ph] = lre
        carry[:, ph:] = lim
        da_ref[:, :ph] += acr
        da_ref[:, ph:] += aci
        lam16 = ds[...].astype(BF16)
        du = jnp.dot(lam16, bbt_ref[0], preferred_element_type=F32) + d_ref[...] * dyp
        du_ref[...] = du.astype(du_ref.dtype)
        dbb_ref[0] += lax.dot_general(lam16, u, tn_dims, preferred_element_type=F32)
        dcd_ref[0] += lax.dot_general(s16, dyp16, tn_dims, preferred_element_type=F32)

    rev = lambda q, t: (nt - 1 - t, q)
    return pl.pallas_call(
        body, name=name, grid=(nc, nt),
        in_specs=[pl.BlockSpec((tm, ch), rev), pl.BlockSpec((tm, ch), rev), pl.BlockSpec((tm, ch), rev),
                  pl.BlockSpec((tm, p2), rev), pl.BlockSpec((1, ch, p2), lambda q, t: (q, 0, 0)),
                  pl.BlockSpec((1, p2, ch), lambda q, t: (q, 0, 0)), pl.BlockSpec((8, p2), lambda q, t: (q, 0)),
                  pl.BlockSpec((1, ch), lambda q, t: (0, q))],
        out_specs=[pl.BlockSpec((tm, ch), rev), pl.BlockSpec((1, p2, ch), lambda q, t: (q, 0, 0)),
                   pl.BlockSpec((1, p2, ch), lambda q, t: (q, 0, 0)), pl.BlockSpec((8, p2), lambda q, t: (q, 0)),
                   pl.BlockSpec((1, ch), lambda q, t: (0, q))],
        out_shape=[jax.ShapeDtypeStruct((n, nc * ch), BF16), jax.ShapeDtypeStruct((nc, p2, ch), F32),
                   jax.ShapeDtypeStruct((nc, p2, ch), F32), jax.ShapeDtypeStruct((nc * 8, p2), F32),
                   jax.ShapeDtypeStruct((1, nc * ch), F32)],
        scratch_shapes=[pltpu.VMEM((tm, p2), F32), pltpu.VMEM((tm, p2), F32), pltpu.VMEM((8, p2), F32)],
        compiler_params=_cparams(("parallel", "arbitrary")))(dyg, ypre, proj, s_all, cdt, bbt, a8, dskip)


ATT_BLOCK = 128
ATT_UNITS = 8
DILATIONS = (1, 4, 16)
_NT = (((1,), (1,)), ((), ()))
_TN = (((0,), (0,)), ((), ()))


def _attn_scores(q, k, kprev, first, scale):
    kk = lax.broadcasted_iota(jnp.int32, (ATT_BLOCK, ATT_BLOCK), 0)
    qq = lax.broadcasted_iota(jnp.int32, (ATT_BLOCK, ATT_BLOCK), 1)
    sc = lax.dot_general(k, q, _NT, preferred_element_type=F32) * scale
    sp = lax.dot_general(kprev, q, _NT, preferred_element_type=F32) * scale
    sc = jnp.where(kk <= qq, sc, -jnp.inf)
    sp = jnp.where(jnp.logical_and(kk >= qq, jnp.logical_not(first)), sp, -jnp.inf)
    return sc, sp


def _attn_first(u, blocks_per_seq_m1):
    return ((pl.program_id(1) * ATT_UNITS + u) & blocks_per_seq_m1) == 0


def _attn_fwd(q, k, v, blocks0, name):
    npat, r, e = q.shape
    nu = r // ATT_BLOCK
    scale = e ** -0.5
    ub = ATT_UNITS * ATT_BLOCK

    def body(q_ref, k_ref, kp_ref, v_ref, vp_ref, o_ref, lse_ref):
        m1 = (blocks0 >> (2 * pl.program_id(0))) - 1
        for u in range(ATT_UNITS):
            sl = pl.ds(u * ATT_BLOCK, ATT_BLOCK)
            qu, ku, vu = q_ref[0, sl, :], k_ref[0, sl, :], v_ref[0, sl, :]
            if u == 0:
                kpu, vpu = kp_ref[0], vp_ref[0]
            else:
                kpu, vpu = k_ref[0, pl.ds((u - 1) * ATT_BLOCK, ATT_BLOCK), :], v_ref[0, pl.ds((u - 1) * ATT_BLOCK, ATT_BLOCK), :]
            sc, sp = _attn_scores(qu, ku, kpu, _attn_first(u, m1), scale)
            m = jnp.maximum(jnp.max(sc, axis=0, keepdims=True), jnp.max(sp, axis=0, keepdims=True))
            pc, pp = jnp.exp(sc - m), jnp.exp(sp - m)
            den = jnp.sum(pc, axis=0, keepdims=True) + jnp.sum(pp, axis=0, keepdims=True)
            inv = 1.0 / den
            o = lax.dot_general((pc * inv).astype(BF16), vu, _TN, preferred_element_type=F32)
            o = o + lax.dot_general((pp * inv).astype(BF16), vpu, _TN, preferred_element_type=F32)
            o_ref[0, sl, :] = o.astype(o_ref.dtype)
            lse_ref[0, u:u + 1, :] = m + jnp.log(den)

    cur = pl.BlockSpec((1, ub, e), lambda p, j: (p, j, 0))
    prv = pl.BlockSpec((1, ATT_BLOCK, e), lambda p, j: (p, jnp.maximum(j * ATT_UNITS - 1, 0), 0))
    return pl.pallas_call(
        body, name=name, grid=(npat, nu // ATT_UNITS), in_specs=[cur, cur, prv, cur, prv],
        out_specs=[cur, pl.BlockSpec((1, ATT_UNITS, ATT_BLOCK), lambda p, j: (p, j, 0))],
        out_shape=[jax.ShapeDtypeStruct((npat, r, e), BF16), jax.ShapeDtypeStruct((npat, nu, ATT_BLOCK), F32)],
        compiler_params=_cparams(("parallel", "parallel")))(q, k, k, v, v)


def _attn_bwd(q, k, v, do, lse, dlse, blocks0, name):
    npat, r, e = q.shape
    nu = r // ATT_BLOCK
    scale = e ** -0.5
    ub = ATT_UNITS * ATT_BLOCK

    def body(q_ref, k_ref, kp_ref, v_ref, vp_ref, do_ref, lse_ref, dlse_ref, dq_ref, dkc_ref, dkp_ref, dvc_ref, dvp_ref):
        m1 = (blocks0 >> (2 * pl.program_id(0))) - 1
        for u in range(ATT_UNITS):
            sl = pl.ds(u * ATT_BLOCK, ATT_BLOCK)
            qu, ku, vu, dou = q_ref[0, sl, :], k_ref[0, sl, :], v_ref[0, sl, :], do_ref[0, sl, :]
            if u == 0:
                kpu, vpu = kp_ref[0], vp_ref[0]
            else:
                kpu, vpu = k_ref[0, pl.ds((u - 1) * ATT_BLOCK, ATT_BLOCK), :], v_ref[0, pl.ds((u - 1) * ATT_BLOCK, ATT_BLOCK), :]
            sc, sp = _attn_scores(qu, ku, kpu, _attn_first(u, m1), scale)
            lse_u = lse_ref[0, u:u + 1, :]
            pc, pp = jnp.exp(sc - lse_u), jnp.exp(sp - lse_u)
            dpc = lax.dot_general(vu, dou, _NT, preferred_element_type=F32)
            dpp = lax.dot_general(vpu, dou, _NT, preferred_element_type=F32)
            dd = jnp.sum(pc * dpc, axis=0, keepdims=True) + jnp.sum(pp * dpp, axis=0, keepdims=True)
            corr = dlse_ref[0, u:u + 1, :] - dd
            dsc = (pc * (dpc + corr) * scale).astype(BF16)
            dsp = (pp * (dpp + corr) * scale).astype(BF16)
            dq = lax.dot_general(dsc, ku, _TN, preferred_element_type=F32)
            dq = dq + lax.dot_general(dsp, kpu, _TN, preferred_element_type=F32)
            dq_ref[0, sl, :] = dq.astype(dq_ref.dtype)
            dkc_ref[0, sl, :] = jnp.dot(dsc, qu, preferred_element_type=F32).astype(dkc_ref.dtype)
            dkp_ref[0, sl, :] = jnp.dot(dsp, qu, preferred_element_type=F32).astype(dkp_ref.dtype)
            dvc_ref[0, sl, :] = jnp.dot(pc.astype(BF16), dou, preferred_element_type=F32).astype(dvc_ref.dtype)
            dvp_ref[0, sl, :] = jnp.dot(pp.astype(BF16), dou, preferred_element_type=F32).astype(dvp_ref.dtype)

    cur = pl.BlockSpec((1, ub, e), lambda p, j: (p, j, 0))
    prv = pl.BlockSpec((1, ATT_BLOCK, e), lambda p, j: (p, jnp.maximum(j * ATT_UNITS - 1, 0), 0))
    stat = pl.BlockSpec((1, ATT_UNITS, ATT_BLOCK), lambda p, j: (p, j, 0))
    return pl.pallas_call(
        body, name=name, grid=(npat, nu // ATT_UNITS), in_specs=[cur, cur, prv, cur, prv, cur, stat, stat],
        out_specs=[cur] * 5, out_shape=[jax.ShapeDtypeStruct((npat, r, e), BF16)] * 5,
        compiler_params=_cparams(("parallel", "parallel")))(q, k, k, v, v, do, lse, dlse)


def _to_units(x, d, bl, heads):
    n, c = x.shape
    e = c // heads
    ls = n // bl // d
    return x.reshape(ls, d, bl, heads, e).transpose(2, 1, 3, 0, 4).reshape(n * heads, e)


def _from_units(y, d, bl, heads):
    r, e = y.shape
    n = r // heads
    ls = n // bl // d
    return y.reshape(bl, d, heads, ls, e).transpose(3, 1, 0, 2, 4).reshape(n, heads * e)


def _stat_from_units(s, d, bl, heads, e):
    n = s.size // heads
    ls = n // bl // d
    t = s.reshape(bl, d, heads, ls).transpose(3, 1, 0, 2).reshape(n, heads)
    return jnp.repeat(t, e, axis=1)


def _stat_to_units(t, d, bl, heads, e):
    n = t.shape[0]
    ls = n // bl // d
    return t[:, ::e].reshape(ls, d, bl, heads).transpose(2, 1, 3, 0).reshape(n * heads // ATT_BLOCK, ATT_BLOCK)


_MESH = pl.DeviceIdType.MESH
_HBM = pl.BlockSpec(memory_space=pltpu.HBM)


def _position():
    return lax.axis_index("x"), lax.axis_index("y"), lax.axis_index("c")


def _other_chips(x, y):
    return [((1 - x, y), 2 * (1 - x) + y), ((x, 1 - y), 2 * x + 1 - y), ((1 - x, 1 - y), 2 * (1 - x) + 1 - y)]


def _swap_sibling(v, name):
    def body(v_ref, got_ref, send_sem, recv_sem):
        x, y, c = _position()
        cp = pltpu.make_async_remote_copy(src_ref=v_ref, dst_ref=got_ref, send_sem=send_sem, recv_sem=recv_sem,
                                          device_id=(x, y, 1 - c), device_id_type=_MESH)
        cp.start()
        cp.wait()

    return pl.pallas_call(
        body, name=name, in_specs=[_HBM], out_specs=_HBM, out_shape=jax.ShapeDtypeStruct(v.shape, v.dtype),
        scratch_shapes=[pltpu.SemaphoreType.DMA, pltpu.SemaphoreType.DMA])(v)


def _chip_exchange(v, scatter, name):
    shape = v.shape[1:] if scatter else v.shape

    def body(v_ref, out_ref, send_sems, recv_sems, local_sem):
        x, y, c = _position()
        me = 2 * x + y
        piece = (lambda j: v_ref.at[j]) if scatter else (lambda j: v_ref)
        mine = pltpu.make_async_copy(piece(me), out_ref.at[me], local_sem)
        mine.start()
        sends = []
        for k, (chip, idx) in enumerate(_other_chips(x, y)):
            cp = pltpu.make_async_remote_copy(src_ref=piece(idx), dst_ref=out_ref.at[me], send_sem=send_sems.at[k],
                                              recv_sem=recv_sems.at[k], device_id=(*chip, c), device_id_type=_MESH)
            cp.start()
            sends.append(cp)
        for k, (chip, idx) in enumerate(_other_chips(x, y)):
            pltpu.make_async_remote_copy(src_ref=piece(idx), dst_ref=out_ref.at[idx], send_sem=send_sems.at[k],
                                         recv_sem=recv_sems.at[k], device_id=(*chip, c), device_id_type=_MESH).wait_recv()
        for cp in sends:
            cp.wait_send()
        mine.wait()

    return pl.pallas_call(
        body, name=name, in_specs=[_HBM], out_specs=_HBM, out_shape=jax.ShapeDtypeStruct((4,) + tuple(shape), v.dtype),
        scratch_shapes=[pltpu.SemaphoreType.DMA((3,)), pltpu.SemaphoreType.DMA((3,)), pltpu.SemaphoreType.DMA])(v)


def _gather_weights(w):
    r = w.shape[0]
    half = r // 2

    def body(w_ref, out_ref, send_sems, recv_sems, local_sem):
        x, y, c = _position()
        me = 2 * x + y
        rows = pl.ds(c * half, half)
        mine = pltpu.make_async_copy(w_ref, out_ref.at[me], local_sem)
        mine.start()
        others = _other_chips(x, y)
        sends = []
        for k, (chip, idx) in enumerate(others):
            cp = pltpu.make_async_remote_copy(src_ref=w_ref.at[rows], dst_ref=out_ref.at[me, rows], send_sem=send_sems.at[k],
                                              recv_sem=recv_sems.at[k], device_id=(*chip, c), device_id_type=_MESH)
            cp.start()
            sends.append(cp)
        for k, (chip, idx) in enumerate(others):
            landed = out_ref.at[idx, rows]
            pltpu.make_async_remote_copy(src_ref=landed, dst_ref=landed, send_sem=send_sems.at[k], recv_sem=recv_sems.at[k],
                                         device_id=(*chip, c), device_id_type=_MESH).wait_recv()
            cp = pltpu.make_async_remote_copy(src_ref=landed, dst_ref=landed, send_sem=send_sems.at[3 + k],
                                              recv_sem=recv_sems.at[3 + k], device_id=(x, y, 1 - c), device_id_type=_MESH)
            cp.start()
            sends.append(cp)
        for k, (chip, idx) in enumerate(others):
            theirs = out_ref.at[idx, pl.ds((1 - c) * half, half)]
            pltpu.make_async_remote_copy(src_ref=theirs, dst_ref=theirs, send_sem=send_sems.at[3 + k], recv_sem=recv_sems.at[3 + k],
                                         device_id=(x, y, 1 - c), device_id_type=_MESH).wait_recv()
        for cp in sends:
            cp.wait_send()
        mine.wait()

    return pl.pallas_call(
        body, name="gather_weights", in_specs=[_HBM], out_specs=_HBM, out_shape=jax.ShapeDtypeStruct((4,) + tuple(w.shape), w.dtype),
        scratch_shapes=[pltpu.SemaphoreType.DMA((6,)), pltpu.SemaphoreType.DMA((6,)), pltpu.SemaphoreType.DMA])(w)


def _attention_fwd(q3, k, v, bl, heads):
    n, c = k.shape
    e = c // heads
    qs = jnp.stack([_to_units(q3[:, p * c:(p + 1) * c], d, bl, heads) for p, d in enumerate(DILATIONS)])
    ks = jnp.stack([_to_units(k, d, bl, heads) for d in DILATIONS])
    vs = jnp.stack([_to_units(v, d, bl, heads) for d in DILATIONS])
    o_u, lse_u = _attn_fwd(qs, ks, vs, n // bl // ATT_BLOCK, "attn_fwd")
    o_t = jnp.concatenate([_from_units(o_u[p], d, bl, heads) for p, d in enumerate(DILATIONS)], axis=1)
    lse_t = jnp.concatenate([_stat_from_units(lse_u[p], d, bl, heads, e) for p, d in enumerate(DILATIONS)], axis=1)
    ins = [("row", o_t, c, p) for p in range(3)] + [("row", lse_t, c, p) for p in range(3)]
    o, = _rowwise(_combine_fwd_fn, "comb_fwd", n, ins, [(c, BF16)])
    return o, (qs, ks, vs, lse_u, o_t, lse_t)


def _attention_bwd(do, saved, bl, heads):
    qs, ks, vs, lse_u, o_t, lse_t = saved
    n, c = do.shape
    e = c // heads
    lane = jnp.arange(c) // e
    jmat = (lane[:, None] == lane[None, :]).astype(F32)
    ins = [("row", o_t, c, p) for p in range(3)] + [("row", lse_t, c, p) for p in range(3)] + [("row", do, c, 0), ("par", jmat)]
    res = _rowwise(_combine_bwd_fn, "comb_bwd", n, ins, [(c, BF16)] * 3 + [(c, F32)] * 3)
    dos = jnp.stack([_to_units(res[p], d, bl, heads) for p, d in enumerate(DILATIONS)])
    dls = jnp.stack([_stat_to_units(res[3 + p], d, bl, heads, e) for p, d in enumerate(DILATIONS)])
    dq_u, dkc, dkp, dvc, dvp = _attn_bwd(qs, ks, vs, dos, lse_u, dls, n // bl // ATT_BLOCK, "attn_bwd")

    def fold(cur, prv):
        nxt = jnp.concatenate([prv[:, ATT_BLOCK:], jnp.zeros_like(prv[:, :ATT_BLOCK])], axis=1)
        return cur.astype(F32) + nxt.astype(F32)

    dk_u, dv_u = fold(dkc, dkp), fold(dvc, dvp)
    dq3 = jnp.concatenate([_from_units(dq_u[p], d, bl, heads) for p, d in enumerate(DILATIONS)], axis=1)
    dk = sum(_from_units(dk_u[p], d, bl, heads) for p, d in enumerate(DILATIONS))
    dv = sum(_from_units(dv_u[p], d, bl, heads) for p, d in enumerate(DILATIONS))
    return dq3, dk, dv


ATT_HEADS = 8
SSM_GROUPS, SSM_STATE, SSM_GROUP = 32, 64, 16


def _row(v):
    return v.reshape(1, -1)


def _layer_fwd(x, w, p, bl):
    n, d = x.shape
    c = d // 2
    h, = _rowwise(_rms_fwd_fn, "rms_fwd", n, [("row", x, d, 0), ("par", _row(p["norm1_g"]))], [(d, BF16)])
    proj = _mm(h, w["w_in"], "nn", BF16, "mm_in")
    disc, disc_vjp = jax.vjp(_ssm_disc, p["ssm_lambda_re"], p["ssm_lambda_im"], p["ssm_log_dt"], p["ssm_b_re"], p["ssm_b_im"])
    bbd, cdm, a8 = _ssm_pack(*disc, p["ssm_c_re"], p["ssm_c_im"])
    ypre, yg, s_all = _ssm_fwd(proj, bbd, cdm, a8, _row(p["ssm_d"]), bl, "ssm_fwd")
    zs = _mm(yg, w["w_ssm_glu"], "nn", BF16, "mm_glu")
    o, att = _attention_fwd(proj[:, c:4 * c], proj[:, 4 * c:5 * c], proj[:, 5 * c:6 * c], bl, ATT_HEADS)
    ya = _mm(o, w["w_att_up"], "nn", BF16, "mm_att")
    w32 = jnp.concatenate([p["conv_w"], jnp.zeros((1, c), F32)], axis=0)
    hc, hconv = _conv_fwd(proj, 6, w32, _row(p["conv_b"]), _row(p["conv_ln_g"]), _row(p["conv_ln_b"]), bl, c, "conv_fwd")
    yc = _mm(hc, w["w_conv_pw2"], "nn", BF16, "mm_pw2")
    gates = [("row", proj, d, 4), ("row", proj, d, 5), ("row", proj, d, 6), ("par", _row(p["b_gate"]))]
    branches = [("row", zs, 2 * d, 0), ("row", ya, d, 0), ("row", yc, d, 0)]
    merged, = _rowwise(_merge_fwd_fn, "merge_fwd", n, gates + branches, [(d, BF16)])
    xm = _mm(merged, w["w_out"], "nn", F32, "mm_out", res=x)
    h2, = _rowwise(_rms_fwd_fn, "rms_fwd", n, [("row", xm, d, 0), ("par", _row(p["norm2_g"]))], [(d, BF16)])
    z = _mm(h2, w["w_ffn_in"], "nn", BF16, "mm_ffn_in")
    f = z.shape[1] // 2
    a, = _rowwise(_swiglu_fwd_fn, "swiglu_fwd", n, [("row", z, 2 * f, 0)], [(f, BF16)], tm=256)
    xo = _mm(a, w["w_ffn_out"], "nn", F32, "mm_ffn_out", res=xm)
    saved = dict(x=x, h=h, proj=proj, disc_vjp=disc_vjp, bbd=bbd, cdm=cdm, a8=a8, ypre=ypre, yg=yg, s_all=s_all, zs=zs, o=o,
                 att=att, ya=ya, w32=w32, hc=hc, hconv=hconv, yc=yc, gates=gates, branches=branches, merged=merged, xm=xm,
                 h2=h2, z=z, a=a)
    return xo, saved


def _layer_bwd(dxo, s, w, p, bl):
    n, d = dxo.shape
    c = d // 2
    g = {}
    f = s["a"].shape[1]
    da = _mm(dxo, w["w_ffn_out"], "nt", BF16, "mm_ffn_out_dx")
    g["w_ffn_out"] = _mm(s["a"], dxo, "tn", F32, "mm_ffn_out_dw")
    dz, = _rowwise(_swiglu_bwd_fn, "swiglu_bwd", n, [("row", s["z"], 2 * f, 0), ("row", da, f, 0)], [(2 * f, BF16)], tm=256)
    dh2 = _mm(dz, w["w_ffn_in"], "nt", F32, "mm_ffn_in_dx")
    g["w_ffn_in"] = _mm(s["h2"], dz, "tn", F32, "mm_ffn_in_dw")
    dxm, dg2 = _rowwise(_rms_bwd_fn, "rms_bwd", n, [("row", s["xm"], d, 0), ("par", _row(p["norm2_g"])), ("row", dh2, d, 0),
                                                   ("row", dxo, d, 0)], [(d, F32)], [d])
    g["norm2_g"] = dg2[0]
    dmerged = _mm(dxm, w["w_out"], "nt", BF16, "mm_out_dx")
    g["w_out"] = _mm(s["merged"], dxm, "tn", F32, "mm_out_dw")
    dgl, dzs, dya, dyc, dbg = _rowwise(_merge_bwd_fn, "merge_bwd", n, s["gates"] + s["branches"] + [("row", dmerged, d, 0)],
                                       [(3 * d, BF16), (2 * d, BF16), (d, BF16), (d, BF16)], [3 * d], tm=256)
    g["b_gate"] = dbg[0]
    dyg = _mm(dzs, w["w_ssm_glu"], "nt", BF16, "mm_glu_dx")
    g["w_ssm_glu"] = _mm(s["yg"], dzs, "tn", F32, "mm_glu_dw")
    du, dbb, dcd, dab, dd = _ssm_bwd(dyg, s["ypre"], s["proj"], s["s_all"], s["cdm"].transpose(0, 2, 1), s["bbd"].transpose(0, 2, 1),
                                     s["a8"], _row(p["ssm_d"]), bl, "ssm_bwd")
    dab_re, dab_im, dbb_re, dbb_im, g["ssm_c_re"], g["ssm_c_im"] = _ssm_unpack(dbb, dcd, dab, SSM_GROUPS, SSM_STATE, SSM_GROUP)
    (g["ssm_lambda_re"], g["ssm_lambda_im"], g["ssm_log_dt"], g["ssm_b_re"],
     g["ssm_b_im"]) = s["disc_vjp"]((dab_re, dab_im, dbb_re, dbb_im))
    g["ssm_d"] = dd[0]
    do = _mm(dya, w["w_att_up"], "nt", BF16, "mm_att_dx")
    g["w_att_up"] = _mm(s["o"], dya, "tn", F32, "mm_att_dw")
    dq3, dk, dv = _attention_bwd(do, s["att"], bl, ATT_HEADS)
    dhc = _mm(dyc, w["w_conv_pw2"], "nt", BF16, "mm_pw2_dx")
    g["w_conv_pw2"] = _mm(s["hc"], dyc, "tn", F32, "mm_pw2_dw")
    dcv, dcw, dcb, dlg, dlb = _conv_bwd(s["proj"], 6, dhc, s["hconv"], s["w32"], _row(p["conv_ln_g"]), _row(p["conv_ln_b"]), bl, c, "conv_bwd")
    g["conv_w"], g["conv_b"], g["conv_ln_g"], g["conv_ln_b"] = dcw, dcb[0], dlg[0], dlb[0]
    dproj = jnp.concatenate([du, dq3, dk.astype(BF16), dv.astype(BF16), dcv, dgl], axis=1)
    dh = _mm(dproj, w["w_in"], "nt", F32, "mm_in_dx")
    g["w_in"] = _mm(s["h"], dproj, "tn", F32, "mm_in_dw")
    dx, dg1 = _rowwise(_rms_bwd_fn, "rms_bwd", n, [("row", s["x"], d, 0), ("par", _row(p["norm1_g"])), ("row", dh, d, 0),
                                                  ("row", dxm, d, 0)], [(d, F32)], [d])
    g["norm1_g"] = dg1[0]
    return dx, g


WEIGHTS = ['norm1_g', 'w_in', 'b_gate', 'ssm_lambda_re', 'ssm_lambda_im', 'ssm_log_dt', 'ssm_b_re', 'ssm_b_im', 'ssm_c_re',
           'ssm_c_im', 'ssm_d', 'w_ssm_glu', 'w_att_up', 'conv_w', 'conv_b', 'conv_ln_g', 'conv_ln_b', 'w_conv_pw2', 'w_out',
           'norm2_g', 'w_ffn_in', 'w_ffn_out', 'final_g']
BIG = ['w_in', 'w_ssm_glu', 'w_att_up', 'conv_w', 'w_conv_pw2', 'w_out', 'w_ffn_in', 'w_ffn_out']
ROW_SHARDED = ('w_out', 'w_ffn_out')
SMALL = [k for k in WEIGHTS if k not in BIG]
LANES = 1024
FLAT_TM = 608
N_CHIPS = 4


def _pad_rows(a, rows):
    return jnp.concatenate([a, jnp.zeros((rows - a.shape[0],) + a.shape[1:], a.dtype)], axis=0) if rows > a.shape[0] else a


def _conv_pad(a):
    return jnp.concatenate([a, jnp.zeros_like(a[:, :1])], axis=1)


def _flat_rows(shapes, mult):
    rows = sum(math.prod(s) // LANES for s in shapes)
    return -(-rows // mult) * mult


def _pack(arrs, rows):
    return _pad_rows(jnp.concatenate([a.reshape(-1, LANES) for a in arrs], axis=0), rows)


def _unpack(flat, shapes):
    out, off = [], 0
    for s in shapes:
        r = 1
        for k in s:
            r *= k
        r //= LANES
        out.append(flat[..., off:off + r, :].reshape(flat.shape[:-2] + tuple(s)))
        off += r
    return out


def _flat_fn(fn, name, ins, n_out, rows, out_dtypes=None):
    out_dtypes = out_dtypes or [F32] * n_out
    return _rowwise(fn, name, rows, [("row", a, LANES, 0) for a in ins], [(LANES, dt) for dt in out_dtypes], tm=FLAT_TM if rows % FLAT_TM == 0 else rows)


def _add_cast_fn(a, b):
    s = a + b.astype(F32)
    return s, s


def _sum4_fn(a, b, c, d):
    return (((a.astype(F32) + b.astype(F32)) + c.astype(F32)) + d.astype(F32),)


def _add2_fn(a, b):
    return (a + b,)


def kernel(x, norm1_g, w_in, b_gate, ssm_lambda_re, ssm_lambda_im, ssm_log_dt, ssm_b_re, ssm_b_im, ssm_c_re, ssm_c_im, ssm_d, w_ssm_glu, w_att_up, conv_w, conv_b, conv_ln_g, conv_ln_b, w_conv_pw2, w_out, norm2_g, w_ffn_in, w_ffn_out, final_g, loss_target, m_norm1_g, m_w_in, m_b_gate, m_ssm_lambda_re, m_ssm_lambda_im, m_ssm_log_dt, m_ssm_b_re, m_ssm_b_im, m_ssm_c_re, m_ssm_c_im, m_ssm_d, m_w_ssm_glu, m_w_att_up, m_conv_w, m_conv_b, m_conv_ln_g, m_conv_ln_b, m_w_conv_pw2, m_w_out, m_norm2_g, m_w_ffn_in, m_w_ffn_out, m_final_g, v_norm1_g, v_w_in, v_b_gate, v_ssm_lambda_re, v_ssm_lambda_im, v_ssm_log_dt, v_ssm_b_re, v_ssm_b_im, v_ssm_c_re, v_ssm_c_im, v_ssm_d, v_w_ssm_glu, v_w_att_up, v_conv_w, v_conv_b, v_conv_ln_g, v_conv_ln_b, v_w_conv_pw2, v_w_out, v_norm2_g, v_w_ffn_in, v_w_ffn_out, v_final_g):
    args = dict(locals())
    wts = {k: args[k] for k in WEIGHTS}
    mom = {k: args["m_" + k] for k in WEIGHTS}
    var = {k: args["v_" + k] for k in WEIGHTS}
    bl, seq, d = x.shape
    n = bl * seq
    depth = norm1_g.shape[0]
    cx, cy, cc = _position()
    me = 2 * cx + cy

    shard = lambda t: {k: (_conv_pad(t[k]) if k == "conv_w" else t[k]) for k in BIG}
    w_sh = shard(wts)
    shapes = [w_sh[k].shape for k in BIG]
    rs = _flat_rows(shapes, 2 * FLAT_TM)
    half = rs // 2
    w_all = _gather_weights(_pack([w_sh[k] for k in BIG], rs).astype(BF16))
    full = {}
    for k, piece in zip(BIG, _unpack(w_all, shapes)):
        _, ks, cs = w_sh[k].shape
        if k in ROW_SHARDED:
            full[k] = piece.transpose(1, 0, 2, 3).reshape(depth, N_CHIPS * ks, cs)
        else:
            full[k] = piece.transpose(1, 2, 0, 3).reshape(depth, ks, N_CHIPS * cs)
    conv_full = full["conv_w"][:, :CONV_WIDTH].astype(F32)

    def layer_params(l):
        p = {k: wts[k][l] for k in SMALL if k != "final_g"}
        p["conv_w"] = conv_full[l]
        return {k: full[k][l] for k in BIG if k != "conv_w"}, p

    to_rows = lambda t: t.transpose(1, 0, 2).reshape(n, d)
    xs = to_rows(x)
    saved = []
    for l in range(depth):
        xs, s = _layer_fwd(xs, *layer_params(l), bl)
        saved.append(s)
    dx, sq, dgf = _rowwise(_loss_fn, "loss_head", n, [("row", xs, d, 0), ("par", _row(final_g)), ("row", to_rows(loss_target), d, 0)],
                           [(d, F32)], [d, d])
    loss = lax.psum(0.5 * jnp.sum(sq) / d, ("x", "y", "c"))

    grads = {"final_g": dgf[0]}
    per_layer = []
    for l in reversed(range(depth)):
        dx, g = _layer_bwd(dx, saved[l], *layer_params(l), bl)
        per_layer.append(g)
    per_layer.reverse()
    for k in WEIGHTS:
        if k != "final_g":
            grads[k] = jnp.stack([g[k] for g in per_layer])
    grad_x = dx.reshape(seq, bl, d).transpose(1, 0, 2)

    pieces = []
    for k in BIG:
        _, ks, cs = w_sh[k].shape
        gk = grads[k]
        if k in ROW_SHARDED:
            pieces.append(gk.reshape(depth, N_CHIPS, ks, cs).transpose(1, 0, 2, 3).reshape(N_CHIPS, -1, LANES))
        else:
            pieces.append(gk.reshape(depth, ks, N_CHIPS, cs).transpose(2, 0, 1, 3).reshape(N_CHIPS, -1, LANES))
    g_flat = jnp.concatenate(pieces, axis=1)
    g_flat = jnp.concatenate([g_flat, jnp.zeros((N_CHIPS, rs - g_flat.shape[1], LANES), F32)], axis=1)
    keep = lax.dynamic_slice_in_dim(g_flat, cc * half, half, axis=1).reshape(N_CHIPS * half, LANES)
    send = lax.dynamic_slice_in_dim(g_flat, (1 - cc) * half, half, axis=1).astype(BF16)
    got = _swap_sibling(send, "rs_swap").reshape(N_CHIPS * half, LANES)
    p32, p16 = _flat_fn(_add_cast_fn, "rs_add", [keep, got], 2, N_CHIPS * half, [F32, BF16])
    landed = _chip_exchange(p16.reshape(N_CHIPS, half, LANES), True, "rs_scatter")
    mine = lax.dynamic_index_in_dim(p32.reshape(N_CHIPS, half, LANES), me, 0, keepdims=False)
    others = [lax.dynamic_index_in_dim(landed, idx, 0, keepdims=False) for _, idx in _other_chips(cx, cy)]
    red, = _flat_fn(_sum4_fn, "rs_sum", [mine] + others, 1, half)
    sib = _swap_sibling(red, "rs_gather")
    g_red = jnp.where(cc == 0, jnp.concatenate([red, sib], axis=0), jnp.concatenate([sib, red], axis=0))

    sm_shapes = [wts[k].shape for k in SMALL]
    sm_rows = -(-sum(wts[k].size for k in SMALL) // (8 * LANES)) * 8
    flat1 = lambda t: _pad_rows(jnp.concatenate([t[k].reshape(-1) for k in SMALL]), sm_rows * LANES).reshape(sm_rows, LANES)
    gs = flat1(grads)
    chip_sum, = _flat_fn(_add2_fn, "ar_add", [gs, _swap_sibling(gs, "ar_swap")], 1, sm_rows)
    slots = _chip_exchange(chip_sum, False, "ar_gather")
    gs_red, = _flat_fn(_sum4_fn, "ar_sum", [slots[j] for j in range(N_CHIPS)], 1, sm_rows)

    big_out = _flat_fn(_adamw_fn, "adamw_big", [_pack([w_sh[k] for k in BIG], rs), g_red, _pack([shard(mom)[k] for k in BIG], rs),
                                               _pack([shard(var)[k] for k in BIG], rs)], 3, rs)
    sm_out = _flat_fn(_adamw_fn, "adamw_small", [flat1(wts), gs_red, flat1(mom), flat1(var)], 3, sm_rows)
    outs = {}
    for tag, bflat, sflat in zip(("grad", "delta", "m", "v"), (g_red,) + tuple(big_out), (gs_red,) + tuple(sm_out)):
        for k, a in zip(BIG, _unpack(bflat, shapes)):
            outs[tag, k] = a[:, :CONV_WIDTH] if k == "conv_w" else a
        off = 0
        sv = sflat.reshape(-1)
        for k, shp in zip(SMALL, sm_shapes):
            outs[tag, k] = sv[off:off + wts[k].size].reshape(shp)
            off += wts[k].size
    return (loss, grad_x, *[outs["grad", k] for k in WEIGHTS], *[outs["delta", k] for k in WEIGHTS],
            *[outs["m", k] for k in WEIGHTS], *[outs["v", k] for k in WEIGHTS])
```

```python
import functools
import math

import jax
import jax.numpy as jnp
from jax import lax
from jax.experimental import pallas as pl
from jax.experimental.pallas import tpu as pltpu

F32 = jnp.float32
BF16 = jnp.bfloat16
VMEM_LIMIT = 56 * 1024 * 1024


def _cparams(sem):
    return pltpu.CompilerParams(dimension_semantics=sem, vmem_limit_bytes=VMEM_LIMIT)


_DIMS = {"nn": (((1,), (0,)), ((), ())), "nt": (((1,), (1,)), ((), ())), "tn": (((0,), (0,)), ((), ()))}


MM_ROWS = 1024


def _div_tile(n, cap):
    best = None
    for t in range(128, min(n, cap) + 1, 128):
        if n % t == 0:
            best = t
    return best or n


def _mm(a, b, form, out_dtype, name, res=None):
    sharded = b.ndim == 3
    kdim, cs = b.shape[-2], b.shape[-1]
    s = b.shape[0] if sharded else 1
    m = a.shape[0]
    tm = MM_ROWS if m % MM_ROWS == 0 else _div_tile(m, MM_ROWS)
    if form == "nn":
        n, kd = s * cs, kdim
        tn, tk = _div_tile(cs, 1792), _div_tile(kdim, 2048)
        per = cs // tn
        b_blk = (tk, tn)
        b_idx = (lambda i, j, k: (j // per, k, j % per)) if sharded else (lambda i, j, k: (k, j))
    else:
        n, kd = kdim, s * cs
        tn, tk = _div_tile(kdim, 1408), _div_tile(cs, 1792)
        per = cs // tk
        b_blk = (tn, tk)
        b_idx = (lambda i, j, k: (k // per, j, k % per)) if sharded else (lambda i, j, k: (j, k))
    nk = kd // tk
    a_spec = pl.BlockSpec((tm, tk), lambda i, j, k: (i, k))
    b_spec = pl.BlockSpec(((None,) + b_blk) if sharded else b_blk, b_idx)
    o_spec = pl.BlockSpec((tm, tn), lambda i, j, k: (i, j))
    dims = _DIMS[form]

    def body(*refs):
        a_ref, b_ref = refs[:2]
        r_ref = refs[2] if res is not None else None
        o_ref = refs[3] if res is not None else refs[2]
        p = lax.dot_general(a_ref[...].astype(BF16), b_ref[...], dims, preferred_element_type=F32)

        def finish(r):
            if r_ref is not None:
                r = r + r_ref[...]
            o_ref[...] = r.astype(out_dtype)

        if nk == 1:
            finish(p)
            return
        acc = refs[-1]
        k = pl.program_id(2)

        @pl.when(k == 0)
        def _():
            acc[...] = p

        @pl.when(k > 0)
        def _():
            acc[...] += p

        @pl.when(k == nk - 1)
        def _():
            finish(acc[...])

    ins = [a, b] + ([] if res is None else [res])
    in_specs = [a_spec, b_spec] + ([] if res is None else [o_spec])
    return pl.pallas_call(
        body, name=name, grid=(m // tm, n // tn, nk), in_specs=in_specs, out_specs=o_spec,
        out_shape=jax.ShapeDtypeStruct((m, n), out_dtype), scratch_shapes=[pltpu.VMEM((tm, tn), F32)] if nk > 1 else [],
        compiler_params=_cparams(("parallel", "parallel", "arbitrary")))(*ins)


def _mm_dw(a, dy, name, shards, layer, depth, into=None):
    r, m = a.shape
    c = dy.shape[1]
    cs = c // shards
    tm, tn, tk = _div_tile(m, 1024), _div_tile(cs, 1792), _div_tile(r, 512)
    per = cs // tn
    nk = r // tk

    def body(*refs):
        a_ref, b_ref = refs[:2]
        o32, o16, acc = refs[-3:]
        k = pl.program_id(2)
        p = lax.dot_general(a_ref[...].astype(BF16), b_ref[...].astype(BF16), _DIMS["tn"], preferred_element_type=F32)

        @pl.when(k == 0)
        def _():
            acc[...] = p

        @pl.when(k > 0)
        def _():
            acc[...] += p

        @pl.when(k == nk - 1)
        def _():
            o32[...] = acc[...]
            o16[...] = acc[...].astype(BF16)

    o_spec = pl.BlockSpec((None, None, tm, tn), lambda i, j, k: (layer, j // per, i, j % per))
    shape = (depth, shards, m, cs)
    ins = [a, dy] + (list(into) if into is not None else [])
    in_specs = [pl.BlockSpec((tk, tm), lambda i, j, k: (k, i)), pl.BlockSpec((tk, tn), lambda i, j, k: (k, j))]
    in_specs += [pl.BlockSpec(memory_space=pltpu.HBM)] * (2 if into is not None else 0)
    return pl.pallas_call(
        body, name=name, grid=(m // tm, c // tn, nk), in_specs=in_specs, out_specs=[o_spec, o_spec],
        out_shape=[jax.ShapeDtypeStruct(shape, F32), jax.ShapeDtypeStruct(shape, BF16)],
        scratch_shapes=[pltpu.VMEM((tm, tn), F32)], input_output_aliases={2: 0, 3: 1} if into is not None else {},
        compiler_params=_cparams(("parallel", "parallel", "arbitrary")))(*ins)


def _core_index():
    return lax.axis_index("c")


def _chip_index():
    return 2 * lax.axis_index("x") + lax.axis_index("y")


def _rowwise(fn, name, n_rows, ins, outs, accs=(), tm=512):
    n_in, n_out, n_acc = len(ins), len(outs), len(accs)
    in_specs, args = [], []
    for spec in ins:
        if spec[0] == "row":
            _, arr, w, cb = spec
            in_specs.append(pl.BlockSpec((tm, w), lambda i, cb=cb: (i, cb)))
        elif spec[0] == "rowoff":
            _, arr, w, cb, index_fn, span = spec
            in_specs.append(pl.BlockSpec((tm, w), lambda i, cb=cb, index_fn=index_fn, nb=span // tm: (index_fn() * nb + i, cb)))
        elif spec[0] == "rowblk":
            _, arr, w, cb, start = spec
            in_specs.append(pl.BlockSpec((tm, w), lambda i, cb=cb, nb=start // tm: (nb + i, cb)))
        else:
            arr = spec[1]
            in_specs.append(pl.BlockSpec(arr.shape, lambda i: (0, 0)))
        args.append(arr)
    out_specs = [pl.BlockSpec((tm, w), lambda i: (i, 0)) for w, _ in outs]
    out_specs += [pl.BlockSpec((1, w), lambda i: (0, 0)) for w in accs]
    out_shape = [jax.ShapeDtypeStruct((n_rows, w), dt) for w, dt in outs]
    out_shape += [jax.ShapeDtypeStruct((1, w), F32) for w in accs]

    def body(*refs):
        i = pl.program_id(0)
        res = fn(*[r[...] for r in refs[:n_in]])
        for o_ref, r in zip(refs[n_in:n_in + n_out], res[:n_out]):
            o_ref[...] = r.astype(o_ref.dtype)
        for a_ref, r in zip(refs[n_in + n_out:], res[n_out:]):
            @pl.when(i == 0)
            def _(a_ref=a_ref, r=r):
                a_ref[...] = r

            @pl.when(i > 0)
            def _(a_ref=a_ref, r=r):
                a_ref[...] += r

    return pl.pallas_call(
        body, name=name, grid=(n_rows // tm,), in_specs=in_specs, out_specs=out_specs, out_shape=out_shape,
        compiler_params=_cparams(("arbitrary",)))(*args)


EPS = 1e-6


def _sig(x):
    return 1.0 / (1.0 + jnp.exp(-x))


def _colsum(x):
    return jnp.sum(x, axis=0, keepdims=True)


def _rms_fwd_fn(x, g):
    r = lax.rsqrt(jnp.mean(x * x, axis=-1, keepdims=True) + EPS)
    return (x * r * g,)


def _rms_bwd_fn(x, g, dh, dres):
    dh = dh.astype(F32)
    r = lax.rsqrt(jnp.mean(x * x, axis=-1, keepdims=True) + EPS)
    xh = x * r
    dyg = dh * g
    dx = r * (dyg - xh * jnp.mean(dyg * xh, axis=-1, keepdims=True)) + dres
    return dx, _colsum(dh * xh)


def _loss_fn(x, g, t):
    d = x.shape[-1]
    r = lax.rsqrt(jnp.mean(x * x, axis=-1, keepdims=True) + EPS)
    xh = x * r
    err = xh * g - t
    dy = err * (1.0 / d)
    dyg = dy * g
    dx = r * (dyg - xh * jnp.mean(dyg * xh, axis=-1, keepdims=True))
    return dx, _colsum(err * err), _colsum(dy * xh)


def _swiglu_fwd_fn(z):
    f = z.shape[-1] // 2
    z1, z2 = z[:, :f].astype(F32), z[:, f:].astype(F32)
    return (z1 * _sig(z1) * z2,)


def _swiglu_bwd_fn(z, da):
    f = z.shape[-1] // 2
    z1, z2, da = z[:, :f].astype(F32), z[:, f:].astype(F32), da.astype(F32)
    s = _sig(z1)
    dz1 = da * z2 * (s * (1.0 + z1 * (1.0 - s)))
    dz2 = da * (z1 * s)
    return (jnp.concatenate([dz1, dz2], axis=1),)


def _merge_fwd_fn(g0, g1, g2, bg, zs, ya, yc):
    d = ya.shape[-1]
    bg = bg.astype(F32)
    zs = zs.astype(F32)
    ys = zs[:, :d] * _sig(zs[:, d:])
    m = _sig(g0.astype(F32) + bg[:, :d]) * ys
    m = m + _sig(g1.astype(F32) + bg[:, d:2 * d]) * ya.astype(F32)
    m = m + _sig(g2.astype(F32) + bg[:, 2 * d:]) * yc.astype(F32)
    return (m,)


def _merge_bwd_fn(g0, g1, g2, bg, zs, ya, yc, dm):
    d = ya.shape[-1]
    bg = bg.astype(F32)
    zs = zs.astype(F32)
    dm = dm.astype(F32)
    z1, s2 = zs[:, :d], _sig(zs[:, d:])
    ys = z1 * s2
    s0 = _sig(g0.astype(F32) + bg[:, :d])
    s1 = _sig(g1.astype(F32) + bg[:, d:2 * d])
    s3 = _sig(g2.astype(F32) + bg[:, 2 * d:])
    dgl = jnp.concatenate([dm * ys * s0 * (1.0 - s0), dm * ya.astype(F32) * s1 * (1.0 - s1),
                           dm * yc.astype(F32) * s3 * (1.0 - s3)], axis=1)
    dys = dm * s0
    dzs = jnp.concatenate([dys * s2, dys * z1 * s2 * (1.0 - s2)], axis=1)
    return dgl, dzs, dm * s1, dm * s3, _colsum(dgl)


def _combine_fwd_fn(o0, o1, o2, l0, l1, l2):
    m = jnp.maximum(jnp.maximum(l0, l1), l2)
    e0, e1, e2 = jnp.exp(l0 - m), jnp.exp(l1 - m), jnp.exp(l2 - m)
    inv = 1.0 / (e0 + e1 + e2)
    return ((e0 * o0.astype(F32) + e1 * o1.astype(F32) + e2 * o2.astype(F32)) * inv,)


def _combine_bwd_fn(o0, o1, o2, l0, l1, l2, do, jmat):
    m = jnp.maximum(jnp.maximum(l0, l1), l2)
    e0, e1, e2 = jnp.exp(l0 - m), jnp.exp(l1 - m), jnp.exp(l2 - m)
    inv = 1.0 / (e0 + e1 + e2)
    w0, w1, w2 = e0 * inv, e1 * inv, e2 * inv
    do = do.astype(F32)

    def headsum(x):
        return jnp.dot(x, jmat, preferred_element_type=F32, precision=lax.Precision.HIGHEST)

    dw0, dw1, dw2 = headsum(do * o0.astype(F32)), headsum(do * o1.astype(F32)), headsum(do * o2.astype(F32))
    mean = w0 * dw0 + w1 * dw1 + w2 * dw2
    return w0 * do, w1 * do, w2 * do, w0 * (dw0 - mean), w1 * (dw1 - mean), w2 * (dw2 - mean)


ADAM_LR, ADAM_B1, ADAM_B2, ADAM_EPS, ADAM_WD, ADAM_STEP = 0.001, 0.9, 0.999, 1e-08, 0.01, 10


def _adamw_fn(w, g, m, v):
    m = ADAM_B1 * m + (1.0 - ADAM_B1) * g
    v = ADAM_B2 * v + (1.0 - ADAM_B2) * (g * g)
    m_hat = m / (1.0 - ADAM_B1 ** ADAM_STEP)
    v_hat = v / (1.0 - ADAM_B2 ** ADAM_STEP)
    delta = -ADAM_LR * (m_hat / (jnp.sqrt(v_hat) + ADAM_EPS) + ADAM_WD * w)
    return delta, m, v


CONV_WIDTH = 31


def _conv_fwd(proj, cb, w32, conv_b, ln_g, ln_b, bl, c, name, tm=512):
    n = proj.shape[0]
    hp = (CONV_WIDTH - 1) * bl
    nt = n // tm

    def body(ap_ref, gp_ref, a_ref, g_ref, w_ref, cb_ref, lg_ref, lb_ref, hc_ref, hconv_ref, ext):
        i = pl.program_id(0)
        ext[pl.ds(hp, tm), :] = a_ref[...].astype(F32) * _sig(g_ref[...].astype(F32))
        hgp = ap_ref[pl.ds(tm - hp, hp), :].astype(F32) * _sig(gp_ref[pl.ds(tm - hp, hp), :].astype(F32))
        ext[pl.ds(0, hp), :] = jnp.where(i > 0, hgp, 0.0)
        acc = jnp.zeros((tm, c), F32) + cb_ref[...]
        for j in range(CONV_WIDTH):
            acc = acc + w_ref[j:j + 1, :] * ext[pl.ds(j * bl, tm), :]
        hconv_ref[...] = acc.astype(hconv_ref.dtype)
        h = hconv_ref[...].astype(F32)
        mu = jnp.mean(h, axis=-1, keepdims=True)
        xc = h - mu
        var = jnp.mean(xc * xc, axis=-1, keepdims=True)
        hn = xc * lax.rsqrt(var + EPS) * lg_ref[...] + lb_ref[...]
        hc_ref[...] = (hn * _sig(hn)).astype(hc_ref.dtype)

    prev = lambda i, k: (jnp.maximum(i - 1, 0), k)
    par = lambda arr: pl.BlockSpec(arr.shape, lambda i: (0, 0))
    return pl.pallas_call(
        body, name=name, grid=(nt,),
        in_specs=[pl.BlockSpec((tm, c), functools.partial(prev, k=cb)), pl.BlockSpec((tm, c), functools.partial(prev, k=cb + 1)),
                  pl.BlockSpec((tm, c), lambda i: (i, cb)), pl.BlockSpec((tm, c), lambda i: (i, cb + 1)),
                  par(w32), par(conv_b), par(ln_g), par(ln_b)],
        out_specs=[pl.BlockSpec((tm, c), lambda i: (i, 0))] * 2,
        out_shape=[jax.ShapeDtypeStruct((n, c), BF16)] * 2,
        scratch_shapes=[pltpu.VMEM((hp + tm, c), F32)],
        compiler_params=_cparams(("arbitrary",)))(proj, proj, proj, proj, w32, conv_b, ln_g, ln_b)


def _conv_bwd(proj, cb, dhc, hconv, w32, ln_g, ln_b, bl, c, name, tm=512):
    n = proj.shape[0]
    hp = (CONV_WIDTH - 1) * bl
    nt = n // tm

    def ln_bwd(d, h, lg, lb):
        d, h = d.astype(F32), h.astype(F32)
        mu = jnp.mean(h, axis=-1, keepdims=True)
        xc = h - mu
        rstd = lax.rsqrt(jnp.mean(xc * xc, axis=-1, keepdims=True) + EPS)
        xh = xc * rstd
        hn = xh * lg + lb
        s = _sig(hn)
        dhn = d * (s * (1.0 + hn * (1.0 - s)))
        dxh = dhn * lg
        dh = rstd * (dxh - jnp.mean(dxh, axis=-1, keepdims=True) - xh * jnp.mean(dxh * xh, axis=-1, keepdims=True))
        return dh, dhn, xh

    def body(ap_ref, gp_ref, a_ref, g_ref, d_ref, dn_ref, h_ref, hn_ref, w_ref, lg_ref, lb_ref,
             dcv_ref, dw_ref, dcb_ref, dlg_ref, dlb_ref, ext_h, ext_d):
        i = pl.program_id(0)
        lg, lb = lg_ref[...], lb_ref[...]
        a, g = a_ref[...].astype(F32), g_ref[...].astype(F32)
        sg = _sig(g)
        ext_h[pl.ds(hp, tm), :] = a * sg
        hgp = ap_ref[pl.ds(tm - hp, hp), :].astype(F32) * _sig(gp_ref[pl.ds(tm - hp, hp), :].astype(F32))
        ext_h[pl.ds(0, hp), :] = jnp.where(i > 0, hgp, 0.0)
        dh, dhn, xh = ln_bwd(d_ref[...], h_ref[...], lg, lb)
        ext_d[pl.ds(0, tm), :] = dh
        dh_n, _, _ = ln_bwd(dn_ref[pl.ds(0, hp), :], hn_ref[pl.ds(0, hp), :], lg, lb)
        ext_d[pl.ds(tm, hp), :] = jnp.where(i < nt - 1, dh_n, 0.0)

        @pl.when(i == 0)
        def _():
            dw_ref[...] = jnp.zeros_like(dw_ref)
            dcb_ref[...] = jnp.zeros_like(dcb_ref)
            dlg_ref[...] = jnp.zeros_like(dlg_ref)
            dlb_ref[...] = jnp.zeros_like(dlb_ref)

        dcb_ref[...] += _colsum(dh)
        dlg_ref[...] += _colsum(dhn * xh)
        dlb_ref[...] += _colsum(dhn)
        dhg = jnp.zeros((tm, c), F32)
        for j in range(CONV_WIDTH):
            dhg = dhg + w_ref[j:j + 1, :] * ext_d[pl.ds((CONV_WIDTH - 1 - j) * bl, tm), :]
            dw_ref[j:j + 1, :] += _colsum(dh * ext_h[pl.ds(j * bl, tm), :])
        dcv_ref[...] = jnp.concatenate([dhg * sg, dhg * a * sg * (1.0 - sg)], axis=1).astype(dcv_ref.dtype)

    prev = lambda i, k: (jnp.maximum(i - 1, 0), k)
    nxt = lambda i: (jnp.minimum(i + 1, nt - 1), 0)
    cur = lambda i: (i, 0)
    par = lambda arr: pl.BlockSpec(arr.shape, lambda i: (0, 0))
    acc = lambda r: pl.BlockSpec((r, c), lambda i: (0, 0))
    return pl.pallas_call(
        body, name=name, grid=(nt,),
        in_specs=[pl.BlockSpec((tm, c), functools.partial(prev, k=cb)), pl.BlockSpec((tm, c), functools.partial(prev, k=cb + 1)),
                  pl.BlockSpec((tm, c), lambda i: (i, cb)), pl.BlockSpec((tm, c), lambda i: (i, cb + 1)),
                  pl.BlockSpec((tm, c), cur), pl.BlockSpec((tm, c), nxt), pl.BlockSpec((tm, c), cur), pl.BlockSpec((tm, c), nxt),
                  par(w32), par(ln_g), par(ln_b)],
        out_specs=[pl.BlockSpec((tm, 2 * c), cur), acc(32), acc(1), acc(1), acc(1)],
        out_shape=[jax.ShapeDtypeStruct((n, 2 * c), BF16), jax.ShapeDtypeStruct((32, c), F32)] + [jax.ShapeDtypeStruct((1, c), F32)] * 3,
        scratch_shapes=[pltpu.VMEM((hp + tm, c), F32), pltpu.VMEM((hp + tm, c), F32)],
        compiler_params=_cparams(("arbitrary",)))(proj, proj, proj, proj, dhc, dhc, hconv, hconv, w32, ln_g, ln_b)


SSM_CH = 128
_GELU_C = 0.7978845608028654


def _gelu(x):
    return 0.5 * x * (1.0 + jnp.tanh(_GELU_C * (x + 0.044715 * x * x * x)))


def _gelu_grad(x):
    th = jnp.tanh(_GELU_C * (x + 0.044715 * x * x * x))
    return 0.5 * (1.0 + th) + 0.5 * x * (1.0 - th * th) * (_GELU_C * (1.0 + 3.0 * 0.044715 * x * x))


def _ssm_disc(lam_re, lam_im, log_dt, b_re, b_im):
    dt = jnp.exp(log_dt)[:, None]
    mag = jnp.exp(lam_re * dt)
    ab_re = mag * jnp.cos(lam_im * dt)
    ab_im = mag * jnp.sin(lam_im * dt)
    nr, ni = ab_re - 1.0, ab_im
    den = lam_re * lam_re + lam_im * lam_im
    z_re = ((nr * lam_re + ni * lam_im) / den)[..., None]
    z_im = ((ni * lam_re - nr * lam_im) / den)[..., None]
    return ab_re, ab_im, z_re * b_re - z_im * b_im, z_re * b_im + z_im * b_re


def _ssm_pack(ab_re, ab_im, bb_re, bb_im, c_re, c_im):
    g, p, h = bb_re.shape
    gc = SSM_CH // h
    nc = g // gc
    eye = jnp.eye(gc, dtype=F32)
    blk = lambda x: jnp.einsum("qgph,gk->qghkp", x.reshape(nc, gc, p, h), eye).reshape(nc, gc * h, gc * p)
    bbd = jnp.concatenate([blk(bb_re), blk(bb_im)], axis=2).astype(BF16)
    blc = lambda x: jnp.einsum("qghp,gk->qgpkh", x.reshape(nc, gc, h, p), eye).reshape(nc, gc * p, gc * h)
    cdm = jnp.concatenate([blc(c_re), blc(-c_im)], axis=1).astype(BF16)
    a = jnp.concatenate([ab_re.reshape(nc, gc * p), ab_im.reshape(nc, gc * p)], axis=1)
    a8 = jnp.broadcast_to(a[:, None, :], (nc, 8, 2 * gc * p)).reshape(nc * 8, 2 * gc * p)
    return bbd, cdm, a8


def _ssm_unpack(dbb, dcd, da, g, p, h):
    gc = SSM_CH // h
    nc = g // gc
    ph = gc * p
    eye = jnp.eye(gc, dtype=F32)
    dia = lambda x, o: jnp.einsum("qgpkh,gk->" + o, x.reshape(nc, gc, p, gc, h), eye).reshape((g, p, h) if o == "qgph" else (g, h, p))
    das = da.reshape(nc, 8, 2 * ph).sum(axis=1)
    return (das[:, :ph].reshape(g, p), das[:, ph:].reshape(g, p), dia(dbb[:, :ph], "qgph"), dia(dbb[:, ph:], "qgph"),
            dia(dcd[:, :ph], "qghp"), -dia(dcd[:, ph:], "qghp"))


def _ssm_fwd(proj, bbd, cdm, a8, dskip, bl, name, tm=1024):
    n = proj.shape[0]
    nc, ch, p2 = bbd.shape
    ph = p2 // 2
    nt = n // tm
    nsub = 8 // bl

    def body(u_ref, bb_ref, cd_ref, a_ref, d_ref, ypre_ref, yg_ref, s_ref, bu, carry):
        t = pl.program_id(1)

        @pl.when(t == 0)
        def _():
            carry[...] = jnp.zeros_like(carry)

        u = u_ref[...]
        bu[...] = jnp.dot(u, bb_ref[0], preferred_element_type=F32)
        a_re, a_im = a_ref[:, :ph], a_ref[:, ph:]
        row = lax.broadcasted_iota(jnp.int32, (8, ph), 0)

        def step(k, c):
            cre, cim = c
            r0 = pl.multiple_of(k * 8, 8)
            bre, bim = bu[pl.ds(r0, 8), :ph], bu[pl.ds(r0, 8), ph:]
            sre, sim = cre, cim
            for sub in range(nsub):
                xre, xim = pltpu.roll(cre, bl, 0), pltpu.roll(cim, bl, 0)
                cre = a_re * xre - a_im * xim + bre
                cim = a_re * xim + a_im * xre + bim
                if sub == 0:
                    sre, sim = cre, cim
                else:
                    sel = row >= sub * bl
                    sre, sim = jnp.where(sel, cre, sre), jnp.where(sel, cim, sim)
            bu[pl.ds(r0, 8), :ph] = sre
            bu[pl.ds(r0, 8), ph:] = sim
            return sre, sim

        cre, cim = lax.fori_loop(0, tm // 8, step, (carry[:, :ph], carry[:, ph:]))
        carry[:, :ph] = cre
        carry[:, ph:] = cim
        s16 = bu[...].astype(BF16)
        s_ref[...] = s16
        y = jnp.dot(s16, cd_ref[0], preferred_element_type=F32) + d_ref[...] * u.astype(F32)
        ypre_ref[...] = y
        yg_ref[...] = _gelu(y).astype(yg_ref.dtype)

    return pl.pallas_call(
        body, name=name, grid=(nc, nt),
        in_specs=[pl.BlockSpec((tm, ch), lambda q, t: (t, q)), pl.BlockSpec((1, ch, p2), lambda q, t: (q, 0, 0)),
                  pl.BlockSpec((1, p2, ch), lambda q, t: (q, 0, 0)), pl.BlockSpec((8, p2), lambda q, t: (q, 0)),
                  pl.BlockSpec((1, ch), lambda q, t: (0, q))],
        out_specs=[pl.BlockSpec((tm, ch), lambda q, t: (t, q)), pl.BlockSpec((tm, ch), lambda q, t: (t, q)),
                   pl.BlockSpec((tm, p2), lambda q, t: (t, q))],
        out_shape=[jax.ShapeDtypeStruct((n, nc * ch), F32), jax.ShapeDtypeStruct((n, nc * ch), BF16),
                   jax.ShapeDtypeStruct((n, nc * p2), BF16)],
        scratch_shapes=[pltpu.VMEM((tm, p2), F32), pltpu.VMEM((8, p2), F32)],
        compiler_params=_cparams(("parallel", "arbitrary")))(proj, bbd, cdm, a8, dskip)


def _ssm_bwd(dyg, ypre, proj, s_all, cdt, bbt, a8, dskip, bl, name, tm=1024):
    n = proj.shape[0]
    nc, ch, p2 = cdt.shape
    ph = p2 // 2
    nt = n // tm
    nsub = 8 // bl
    tn_dims = (((0,), (0,)), ((), ()))

    def body(dyg_ref, ypre_ref, u_ref, s_ref, cdt_ref, bbt_ref, a_ref, d_ref,
             du_ref, dbb_ref, dcd_ref, da_ref, dd_ref, ds, s32, carry):
        t = pl.program_id(1)

        @pl.when(t == 0)
        def _():
            carry[...] = jnp.zeros_like(carry)
            dbb_ref[...] = jnp.zeros_like(dbb_ref)
            dcd_ref[...] = jnp.zeros_like(dcd_ref)
            da_ref[...] = jnp.zeros_like(da_ref)
            dd_ref[...] = jnp.zeros_like(dd_ref)

        dyp = dyg_ref[...].astype(F32) * _gelu_grad(ypre_ref[...])
        u = u_ref[...]
        dd_ref[...] += _colsum(dyp * u.astype(F32))
        dyp16 = dyp.astype(BF16)
        ds[...] = jnp.dot(dyp16, cdt_ref[0], preferred_element_type=F32)
        s16 = s_ref[...]
        s32[...] = s16.astype(F32)
        a_re, a_im = a_ref[:, :ph], a_ref[:, ph:]
        row = lax.broadcasted_iota(jnp.int32, (8, ph), 0)
        back = 8 - bl

        def step(kk, c):
            lre, lim, acr, aci = c
            r0 = pl.multiple_of((tm // 8 - 1 - kk) * 8, 8)
            dre, dim = ds[pl.ds(r0, 8), :ph], ds[pl.ds(r0, 8), ph:]
            sre, sim = s32[pl.ds(r0, 8), :ph], s32[pl.ds(r0, 8), ph:]
            ore, oim, ire, iim = lre, lim, lre, lim
            for sub in range(nsub - 1, -1, -1):
                xre, xim = pltpu.roll(lre, back, 0), pltpu.roll(lim, back, 0)
                lre = a_re * xre + a_im * xim + dre
                lim = a_re * xim - a_im * xre + dim
                if sub == nsub - 1:
                    ore, oim, ire, iim = lre, lim, xre, xim
                else:
                    sel = row < (sub + 1) * bl
                    ore, oim = jnp.where(sel, lre, ore), jnp.where(sel, lim, oim)
                    ire, iim = jnp.where(sel, xre, ire), jnp.where(sel, xim, iim)
            ds[pl.ds(r0, 8), :ph] = ore
            ds[pl.ds(r0, 8), ph:] = oim
            acr = acr + sre * ire + sim * iim
            aci = aci + sre * iim - sim * ire
            return ore, oim, acr, aci

        z = jnp.zeros((8, ph), F32)
        lre, lim, acr, aci = lax.fori_loop(0, tm // 8, step, (carry[:, :ph], carry[:, ph:], z, z))
        carry[:, :ph] = lre
        carry[:, ph:] = lim
        da_ref[:, :ph] += acr
        da_ref[:, ph:] += aci
        lam16 = ds[...].astype(BF16)
        du = jnp.dot(lam16, bbt_ref[0], preferred_element_type=F32) + d_ref[...] * dyp
        du_ref[...] = du.astype(du_ref.dtype)
        dbb_ref[0] += lax.dot_general(lam16, u, tn_dims, preferred_element_type=F32)
        dcd_ref[0] += lax.dot_general(s16, dyp16, tn_dims, preferred_element_type=F32)

    rev = lambda q, t: (nt - 1 - t, q)
    return pl.pallas_call(
        body, name=name, grid=(nc, nt),
        in_specs=[pl.BlockSpec((tm, ch), rev), pl.BlockSpec((tm, ch), rev), pl.BlockSpec((tm, ch), rev),
                  pl.BlockSpec((tm, p2), rev), pl.BlockSpec((1, ch, p2), lambda q, t: (q, 0, 0)),
                  pl.BlockSpec((1, p2, ch), lambda q, t: (q, 0, 0)), pl.BlockSpec((8, p2), lambda q, t: (q, 0)),
                  pl.BlockSpec((1, ch), lambda q, t: (0, q))],
        out_specs=[pl.BlockSpec((tm, ch), rev), pl.BlockSpec((1, p2, ch), lambda q, t: (q, 0, 0)),
                   pl.BlockSpec((1, p2, ch), lambda q, t: (q, 0, 0)), pl.BlockSpec((8, p2), lambda q, t: (q, 0)),
                   pl.BlockSpec((1, ch), lambda q, t: (0, q))],
        out_shape=[jax.ShapeDtypeStruct((n, nc * ch), BF16), jax.ShapeDtypeStruct((nc, p2, ch), F32),
                   jax.ShapeDtypeStruct((nc, p2, ch), F32), jax.ShapeDtypeStruct((nc * 8, p2), F32),
                   jax.ShapeDtypeStruct((1, nc * ch), F32)],
        scratch_shapes=[pltpu.VMEM((tm, p2), F32), pltpu.VMEM((tm, p2), F32), pltpu.VMEM((8, p2), F32)],
        compiler_params=_cparams(("parallel", "arbitrary")))(dyg, ypre, proj, s_all, cdt, bbt, a8, dskip)


ATT_BLOCK = 128
ATT_UNITS = 8
DILATIONS = (1, 4, 16)
_NT = (((1,), (1,)), ((), ()))
_TN = (((0,), (0,)), ((), ()))


def _attn_scores(q, k, kprev, first, scale):
    kk = lax.broadcasted_iota(jnp.int32, (ATT_BLOCK, ATT_BLOCK), 0)
    qq = lax.broadcasted_iota(jnp.int32, (ATT_BLOCK, ATT_BLOCK), 1)
    sc = lax.dot_general(k, q, _NT, preferred_element_type=F32) * scale
    sp = lax.dot_general(kprev, q, _NT, preferred_element_type=F32) * scale
    sc = jnp.where(kk <= qq, sc, -jnp.inf)
    sp = jnp.where(jnp.logical_and(kk >= qq, jnp.logical_not(first)), sp, -jnp.inf)
    return sc, sp


def _attn_first(u, blocks_per_seq_m1):
    return ((pl.program_id(1) * ATT_UNITS + u) & blocks_per_seq_m1) == 0


def _attn_fwd(q, k, v, blocks0, name):
    npat, r, e = q.shape
    nu = r // ATT_BLOCK
    scale = e ** -0.5
    ub = ATT_UNITS * ATT_BLOCK

    def body(q_ref, k_ref, kp_ref, v_ref, vp_ref, o_ref, lse_ref):
        m1 = (blocks0 >> (2 * pl.program_id(0))) - 1
        for u in range(ATT_UNITS):
            sl = pl.ds(u * ATT_BLOCK, ATT_BLOCK)
            qu, ku, vu = q_ref[0, sl, :], k_ref[0, sl, :], v_ref[0, sl, :]
            if u == 0:
                kpu, vpu = kp_ref[0], vp_ref[0]
            else:
                kpu, vpu = k_ref[0, pl.ds((u - 1) * ATT_BLOCK, ATT_BLOCK), :], v_ref[0, pl.ds((u - 1) * ATT_BLOCK, ATT_BLOCK), :]
            sc, sp = _attn_scores(qu, ku, kpu, _attn_first(u, m1), scale)
            m = jnp.maximum(jnp.max(sc, axis=0, keepdims=True), jnp.max(sp, axis=0, keepdims=True))
            pc, pp = jnp.exp(sc - m), jnp.exp(sp - m)
            den = jnp.sum(pc, axis=0, keepdims=True) + jnp.sum(pp, axis=0, keepdims=True)
            inv = 1.0 / den
            o = lax.dot_general((pc * inv).astype(BF16), vu, _TN, preferred_element_type=F32)
            o = o + lax.dot_general((pp * inv).astype(BF16), vpu, _TN, preferred_element_type=F32)
            o_ref[0, sl, :] = o.astype(o_ref.dtype)
            lse_ref[0, u:u + 1, :] = m + jnp.log(den)

    cur = pl.BlockSpec((1, ub, e), lambda p, j: (p, j, 0))
    prv = pl.BlockSpec((1, ATT_BLOCK, e), lambda p, j: (p, jnp.maximum(j * ATT_UNITS - 1, 0), 0))
    return pl.pallas_call(
        body, name=name, grid=(npat, nu // ATT_UNITS), in_specs=[cur, cur, prv, cur, prv],
        out_specs=[cur, pl.BlockSpec((1, ATT_UNITS, ATT_BLOCK), lambda p, j: (p, j, 0))],
        out_shape=[jax.ShapeDtypeStruct((npat, r, e), BF16), jax.ShapeDtypeStruct((npat, nu, ATT_BLOCK), F32)],
        compiler_params=_cparams(("parallel", "parallel")))(q, k, k, v, v)


def _attn_bwd(q, k, v, do, lse, dlse, blocks0, name):
    npat, r, e = q.shape
    nu = r // ATT_BLOCK
    scale = e ** -0.5
    ub = ATT_UNITS * ATT_BLOCK

    def body(q_ref, k_ref, kp_ref, v_ref, vp_ref, do_ref, lse_ref, dlse_ref, dq_ref, dkc_ref, dkp_ref, dvc_ref, dvp_ref):
        m1 = (blocks0 >> (2 * pl.program_id(0))) - 1
        for u in range(ATT_UNITS):
            sl = pl.ds(u * ATT_BLOCK, ATT_BLOCK)
            qu, ku, vu, dou = q_ref[0, sl, :], k_ref[0, sl, :], v_ref[0, sl, :], do_ref[0, sl, :]
            if u == 0:
                kpu, vpu = kp_ref[0], vp_ref[0]
            else:
                kpu, vpu = k_ref[0, pl.ds((u - 1) * ATT_BLOCK, ATT_BLOCK), :], v_ref[0, pl.ds((u - 1) * ATT_BLOCK, ATT_BLOCK), :]
            sc, sp = _attn_scores(qu, ku, kpu, _attn_first(u, m1), scale)
            lse_u = lse_ref[0, u:u + 1, :]
            pc, pp = jnp.exp(sc - lse_u), jnp.exp(sp - lse_u)
            dpc = lax.dot_general(vu, dou, _NT, preferred_element_type=F32)
            dpp = lax.dot_general(vpu, dou, _NT, preferred_element_type=F32)
            dd = jnp.sum(pc * dpc, axis=0, keepdims=True) + jnp.sum(pp * dpp, axis=0, keepdims=True)
            corr = dlse_ref[0, u:u + 1, :] - dd
            dsc = (pc * (dpc + corr) * scale).astype(BF16)
            dsp = (pp * (dpp + corr) * scale).astype(BF16)
            dq = lax.dot_general(dsc, ku, _TN, preferred_element_type=F32)
            dq = dq + lax.dot_general(dsp, kpu, _TN, preferred_element_type=F32)
            dq_ref[0, sl, :] = dq.astype(dq_ref.dtype)
            dkc_ref[0, sl, :] = jnp.dot(dsc, qu, preferred_element_type=F32).astype(dkc_ref.dtype)
            dkp_ref[0, sl, :] = jnp.dot(dsp, qu, preferred_element_type=F32).astype(dkp_ref.dtype)
            dvc_ref[0, sl, :] = jnp.dot(pc.astype(BF16), dou, preferred_element_type=F32).astype(dvc_ref.dtype)
            dvp_ref[0, sl, :] = jnp.dot(pp.astype(BF16), dou, preferred_element_type=F32).astype(dvp_ref.dtype)

    cur = pl.BlockSpec((1, ub, e), lambda p, j: (p, j, 0))
    prv = pl.BlockSpec((1, ATT_BLOCK, e), lambda p, j: (p, jnp.maximum(j * ATT_UNITS - 1, 0), 0))
    stat = pl.BlockSpec((1, ATT_UNITS, ATT_BLOCK), lambda p, j: (p, j, 0))
    return pl.pallas_call(
        body, name=name, grid=(npat, nu // ATT_UNITS), in_specs=[cur, cur, prv, cur, prv, cur, stat, stat],
        out_specs=[cur] * 5, out_shape=[jax.ShapeDtypeStruct((npat, r, e), BF16)] * 5,
        compiler_params=_cparams(("parallel", "parallel")))(q, k, k, v, v, do, lse, dlse)


def _to_units(x, d, bl, heads):
    n, c = x.shape
    e = c // heads
    ls = n // bl // d
    return x.reshape(ls, d, bl, heads, e).transpose(2, 1, 3, 0, 4).reshape(n * heads, e)


def _from_units(y, d, bl, heads):
    r, e = y.shape
    n = r // heads
    ls = n // bl // d
    return y.reshape(bl, d, heads, ls, e).transpose(3, 1, 0, 2, 4).reshape(n, heads * e)


def _stat_from_units(s, d, bl, heads, e):
    n = s.size // heads
    ls = n // bl // d
    t = s.reshape(bl, d, heads, ls).transpose(3, 1, 0, 2).reshape(n, heads)
    return jnp.repeat(t, e, axis=1)


def _stat_to_units(t, d, bl, heads, e):
    n = t.shape[0]
    ls = n // bl // d
    return t[:, ::e].reshape(ls, d, bl, heads).transpose(2, 1, 3, 0).reshape(n * heads // ATT_BLOCK, ATT_BLOCK)


_MESH = pl.DeviceIdType.MESH
_HBM = pl.BlockSpec(memory_space=pltpu.HBM)


def _position():
    return lax.axis_index("x"), lax.axis_index("y"), lax.axis_index("c")


def _other_chips(x, y):
    return [((1 - x, y), 2 * (1 - x) + y), ((x, 1 - y), 2 * x + 1 - y), ((1 - x, 1 - y), 2 * (1 - x) + 1 - y)]


def _swap_sibling(v, name):
    def body(v_ref, got_ref, send_sem, recv_sem):
        x, y, c = _position()
        cp = pltpu.make_async_remote_copy(src_ref=v_ref, dst_ref=got_ref, send_sem=send_sem, recv_sem=recv_sem,
                                          device_id=(x, y, 1 - c), device_id_type=_MESH)
        cp.start()
        cp.wait()

    return pl.pallas_call(
        body, name=name, in_specs=[_HBM], out_specs=_HBM, out_shape=jax.ShapeDtypeStruct(v.shape, v.dtype),
        scratch_shapes=[pltpu.SemaphoreType.DMA, pltpu.SemaphoreType.DMA])(v)


def _chip_allgather(v, name):
    def body(v_ref, out_ref, send_sems, recv_sems, local_sem):
        x, y, c = _position()
        me = 2 * x + y
        mine = pltpu.make_async_copy(v_ref, out_ref.at[me], local_sem)
        mine.start()
        sends = []
        for k, (chip, idx) in enumerate(_other_chips(x, y)):
            cp = pltpu.make_async_remote_copy(src_ref=v_ref, dst_ref=out_ref.at[me], send_sem=send_sems.at[k],
                                              recv_sem=recv_sems.at[k], device_id=(*chip, c), device_id_type=_MESH)
            cp.start()
            sends.append(cp)
        for k, (chip, idx) in enumerate(_other_chips(x, y)):
            pltpu.make_async_remote_copy(src_ref=v_ref, dst_ref=out_ref.at[idx], send_sem=send_sems.at[k],
                                         recv_sem=recv_sems.at[k], device_id=(*chip, c), device_id_type=_MESH).wait_recv()
        for cp in sends:
            cp.wait_send()
        mine.wait()

    return pl.pallas_call(
        body, name=name, in_specs=[_HBM], out_specs=_HBM, out_shape=jax.ShapeDtypeStruct((4,) + tuple(v.shape), v.dtype),
        scratch_shapes=[pltpu.SemaphoreType.DMA((3,)), pltpu.SemaphoreType.DMA((3,)), pltpu.SemaphoreType.DMA])(v)


def _remote(src, dst, send_sems, recv_sems, s, device):
    return pltpu.make_async_remote_copy(src_ref=src, dst_ref=dst, send_sem=send_sems.at[s], recv_sem=recv_sems.at[s],
                                        device_id=device, device_id_type=_MESH)


def _comm_call(body, name, ins, out_shapes, n_sems, n_local=0):
    scratch = [pltpu.SemaphoreType.DMA((n_sems,)), pltpu.SemaphoreType.DMA((n_sems,))]
    scratch += [pltpu.SemaphoreType.DMA((n_local,))] if n_local else []
    return pl.pallas_call(body, name=name, in_specs=[_HBM] * len(ins), out_specs=[_HBM] * len(out_shapes),
                          out_shape=out_shapes, scratch_shapes=scratch)(*ins)


def _gather_layers(ws):
    n = len(ws)

    def body(*refs):
        w_refs, out_refs, (send_sems, recv_sems, local_sems) = refs[:n], refs[n:2 * n], refs[2 * n:]
        x, y, c = _position()
        me = 2 * x + y
        sibling = (x, y, 1 - c)
        others = _other_chips(x, y)
        pending = []
        for i in range(n):
            for l in range(2):
                cp = pltpu.make_async_copy(w_refs[i].at[l], out_refs[i].at[l, me], local_sems.at[2 * i + l])
                cp.start()
                pending.append(cp.wait)
            for k, (chip, idx) in enumerate(others):
                cp = _remote(w_refs[i].at[c], out_refs[i].at[c, me], send_sems, recv_sems, 6 * i + k, (*chip, c))
                cp.start()
                pending.append(cp.wait_send)
        for i in range(n):
            for k, (chip, idx) in enumerate(others):
                landed = out_refs[i].at[c, idx]
                _remote(landed, landed, send_sems, recv_sems, 6 * i + k, (*chip, c)).wait_recv()
                cp = _remote(landed, landed, send_sems, recv_sems, 6 * i + 3 + k, sibling)
                cp.start()
                pending.append(cp.wait_send)
        for i in range(n):
            for k, (chip, idx) in enumerate(others):
                theirs = out_refs[i].at[1 - c, idx]
                _remote(theirs, theirs, send_sems, recv_sems, 6 * i + 3 + k, sibling).wait_recv()
        for wait in pending:
            wait()

    shapes = [jax.ShapeDtypeStruct((w.shape[0], N_CHIPS) + tuple(w.shape[1:]), w.dtype) for w in ws]
    return _comm_call(body, "gather_weights", ws, shapes, 6 * n, 2 * n)


def _swap_layers(gs):
    n = len(gs)

    def body(*refs):
        g_refs, out_refs, (send_sems, recv_sems) = refs[:n], refs[n:2 * n], refs[2 * n:]
        x, y, c = _position()
        cps = [_remote(g_refs[i].at[1 - c], out_refs[i], send_sems, recv_sems, i, (x, y, 1 - c)) for i in range(n)]
        for cp in cps:
            cp.start()
        for cp in cps:
            cp.wait()

    return _comm_call(body, "rs_swap", gs, [jax.ShapeDtypeStruct(g.shape[1:], g.dtype) for g in gs], n)


def _scatter_chips(ps):
    n = len(ps)

    def body(*refs):
        p_refs, out_refs, (send_sems, recv_sems) = refs[:n], refs[n:2 * n], refs[2 * n:]
        x, y, c = _position()
        cps = [_remote(p_refs[i].at[idx], out_refs[i].at[k], send_sems, recv_sems, 3 * i + k, (*chip, c))
               for i in range(n) for k, (chip, idx) in enumerate(_other_chips(x, y))]
        for cp in cps:
            cp.start()
        for cp in cps:
            cp.wait()

    return _comm_call(body, "rs_scatter", ps, [jax.ShapeDtypeStruct((3,) + tuple(p.shape[1:]), p.dtype) for p in ps], 3 * n)


def _exchange_layers(rs):
    n = len(rs)

    def body(*refs):
        r_refs, out_refs, (send_sems, recv_sems, local_sems) = refs[:n], refs[n:2 * n], refs[2 * n:]
        x, y, c = _position()
        cps = []
        for i in range(n):
            cps.append(pltpu.make_async_copy(r_refs[i], out_refs[i].at[c], local_sems.at[i]))
            cps.append(_remote(r_refs[i], out_refs[i].at[c], send_sems, recv_sems, i, (x, y, 1 - c)))
        for cp in cps:
            cp.start()
        for i in range(n):
            cps[2 * i].wait()
            cps[2 * i + 1].wait_send()
            theirs = out_refs[i].at[1 - c]
            _remote(theirs, theirs, send_sems, recv_sems, i, (x, y, 1 - c)).wait_recv()

    return _comm_call(body, "rs_gather", rs, [jax.ShapeDtypeStruct((2,) + tuple(r.shape), r.dtype) for r in rs], n, n)


def _attention_fwd(q3, k, v, bl, heads):
    n, c = k.shape
    e = c // heads
    qs = jnp.stack([_to_units(q3[:, p * c:(p + 1) * c], d, bl, heads) for p, d in enumerate(DILATIONS)])
    ks = jnp.stack([_to_units(k, d, bl, heads) for d in DILATIONS])
    vs = jnp.stack([_to_units(v, d, bl, heads) for d in DILATIONS])
    o_u, lse_u = _attn_fwd(qs, ks, vs, n // bl // ATT_BLOCK, "attn_fwd")
    o_t = jnp.concatenate([_from_units(o_u[p], d, bl, heads) for p, d in enumerate(DILATIONS)], axis=1)
    lse_t = jnp.concatenate([_stat_from_units(lse_u[p], d, bl, heads, e) for p, d in enumerate(DILATIONS)], axis=1)
    ins = [("row", o_t, c, p) for p in range(3)] + [("row", lse_t, c, p) for p in range(3)]
    o, = _rowwise(_combine_fwd_fn, "comb_fwd", n, ins, [(c, BF16)])
    return o, (qs, ks, vs, lse_u, o_t, lse_t)


def _attention_bwd(do, saved, bl, heads):
    qs, ks, vs, lse_u, o_t, lse_t = saved
    n, c = do.shape
    e = c // heads
    lane = jnp.arange(c) // e
    jmat = (lane[:, None] == lane[None, :]).astype(F32)
    ins = [("row", o_t, c, p) for p in range(3)] + [("row", lse_t, c, p) for p in range(3)] + [("row", do, c, 0), ("par", jmat)]
    res = _rowwise(_combine_bwd_fn, "comb_bwd", n, ins, [(c, BF16)] * 3 + [(c, F32)] * 3)
    dos = jnp.stack([_to_units(res[p], d, bl, heads) for p, d in enumerate(DILATIONS)])
    dls = jnp.stack([_stat_to_units(res[3 + p], d, bl, heads, e) for p, d in enumerate(DILATIONS)])
    dq_u, dkc, dkp, dvc, dvp = _attn_bwd(qs, ks, vs, dos, lse_u, dls, n // bl // ATT_BLOCK, "attn_bwd")

    def fold(cur, prv):
        nxt = jnp.concatenate([prv[:, ATT_BLOCK:], jnp.zeros_like(prv[:, :ATT_BLOCK])], axis=1)
        return cur.astype(F32) + nxt.astype(F32)

    dk_u, dv_u = fold(dkc, dkp), fold(dvc, dvp)
    dq3 = jnp.concatenate([_from_units(dq_u[p], d, bl, heads) for p, d in enumerate(DILATIONS)], axis=1)
    dk = sum(_from_units(dk_u[p], d, bl, heads) for p, d in enumerate(DILATIONS))
    dv = sum(_from_units(dv_u[p], d, bl, heads) for p, d in enumerate(DILATIONS))
    return dq3, dk, dv


ATT_HEADS = 8
SSM_GROUPS, SSM_STATE, SSM_GROUP = 32, 64, 16


def _row(v):
    return v.reshape(1, -1)


def _layer_fwd(x, w, p, bl):
    n, d = x.shape
    c = d // 2
    h, = _rowwise(_rms_fwd_fn, "rms_fwd", n, [("row", x, d, 0), ("par", _row(p["norm1_g"]))], [(d, BF16)])
    proj = _mm(h, w["w_in"], "nn", BF16, "mm_in")
    disc, disc_vjp = jax.vjp(_ssm_disc, p["ssm_lambda_re"], p["ssm_lambda_im"], p["ssm_log_dt"], p["ssm_b_re"], p["ssm_b_im"])
    bbd, cdm, a8 = _ssm_pack(*disc, p["ssm_c_re"], p["ssm_c_im"])
    ypre, yg, s_all = _ssm_fwd(proj, bbd, cdm, a8, _row(p["ssm_d"]), bl, "ssm_fwd")
    zs = _mm(yg, w["w_ssm_glu"], "nn", BF16, "mm_glu")
    o, att = _attention_fwd(proj[:, c:4 * c], proj[:, 4 * c:5 * c], proj[:, 5 * c:6 * c], bl, ATT_HEADS)
    ya = _mm(o, w["w_att_up"], "nn", BF16, "mm_att")
    w32 = jnp.concatenate([p["conv_w"], jnp.zeros((1, c), F32)], axis=0)
    hc, hconv = _conv_fwd(proj, 6, w32, _row(p["conv_b"]), _row(p["conv_ln_g"]), _row(p["conv_ln_b"]), bl, c, "conv_fwd")
    yc = _mm(hc, w["w_conv_pw2"], "nn", BF16, "mm_pw2")
    gates = [("row", proj, d, 4), ("row", proj, d, 5), ("row", proj, d, 6), ("par", _row(p["b_gate"]))]
    branches = [("row", zs, 2 * d, 0), ("row", ya, d, 0), ("row", yc, d, 0)]
    merged, = _rowwise(_merge_fwd_fn, "merge_fwd", n, gates + branches, [(d, BF16)])
    xm = _mm(merged, w["w_out"], "nn", F32, "mm_out", res=x)
    h2, = _rowwise(_rms_fwd_fn, "rms_fwd", n, [("row", xm, d, 0), ("par", _row(p["norm2_g"]))], [(d, BF16)])
    z = _mm(h2, w["w_ffn_in"], "nn", BF16, "mm_ffn_in")
    f = z.shape[1] // 2
    a, = _rowwise(_swiglu_fwd_fn, "swiglu_fwd", n, [("row", z, 2 * f, 0)], [(f, BF16)], tm=256)
    xo = _mm(a, w["w_ffn_out"], "nn", F32, "mm_ffn_out", res=xm)
    saved = dict(x=x, h=h, proj=proj, disc_vjp=disc_vjp, bbd=bbd, cdm=cdm, a8=a8, ypre=ypre, yg=yg, s_all=s_all, zs=zs, o=o,
                 att=att, ya=ya, w32=w32, hc=hc, hconv=hconv, yc=yc, gates=gates, branches=branches, merged=merged, xm=xm,
                 h2=h2, z=z, a=a)
    return xo, saved


def _layer_bwd(dxo, s, w, p, bl, layer, depth, bufs):
    n, d = dxo.shape
    c = d // 2
    g = {}
    f = s["a"].shape[1]

    def dw(key, a, dy, name):
        bufs[key] = _mm_dw(a, dy, name, 1 if key in ROW_SHARDED else N_CHIPS, layer, depth, bufs.get(key))

    da = _mm(dxo, w["w_ffn_out"], "nt", BF16, "mm_ffn_out_dx")
    dw("w_ffn_out", s["a"], dxo, "mm_ffn_out_dw")
    dz, = _rowwise(_swiglu_bwd_fn, "swiglu_bwd", n, [("row", s["z"], 2 * f, 0), ("row", da, f, 0)], [(2 * f, BF16)], tm=256)
    dh2 = _mm(dz, w["w_ffn_in"], "nt", F32, "mm_ffn_in_dx")
    dw("w_ffn_in", s["h2"], dz, "mm_ffn_in_dw")
    dxm, dg2 = _rowwise(_rms_bwd_fn, "rms_bwd", n, [("row", s["xm"], d, 0), ("par", _row(p["norm2_g"])), ("row", dh2, d, 0),
                                                   ("row", dxo, d, 0)], [(d, F32)], [d])
    g["norm2_g"] = dg2[0]
    dmerged = _mm(dxm, w["w_out"], "nt", BF16, "mm_out_dx")
    dw("w_out", s["merged"], dxm, "mm_out_dw")
    dgl, dzs, dya, dyc, dbg = _rowwise(_merge_bwd_fn, "merge_bwd", n, s["gates"] + s["branches"] + [("row", dmerged, d, 0)],
                                       [(3 * d, BF16), (2 * d, BF16), (d, BF16), (d, BF16)], [3 * d], tm=256)
    g["b_gate"] = dbg[0]
    dyg = _mm(dzs, w["w_ssm_glu"], "nt", BF16, "mm_glu_dx")
    dw("w_ssm_glu", s["yg"], dzs, "mm_glu_dw")
    du, dbb, dcd, dab, dd = _ssm_bwd(dyg, s["ypre"], s["proj"], s["s_all"], s["cdm"].transpose(0, 2, 1), s["bbd"].transpose(0, 2, 1),
                                     s["a8"], _row(p["ssm_d"]), bl, "ssm_bwd")
    dab_re, dab_im, dbb_re, dbb_im, g["ssm_c_re"], g["ssm_c_im"] = _ssm_unpack(dbb, dcd, dab, SSM_GROUPS, SSM_STATE, SSM_GROUP)
    (g["ssm_lambda_re"], g["ssm_lambda_im"], g["ssm_log_dt"], g["ssm_b_re"],
     g["ssm_b_im"]) = s["disc_vjp"]((dab_re, dab_im, dbb_re, dbb_im))
    g["ssm_d"] = dd[0]
    do = _mm(dya, w["w_att_up"], "nt", BF16, "mm_att_dx")
    dw("w_att_up", s["o"], dya, "mm_att_dw")
    dq3, dk, dv = _attention_bwd(do, s["att"], bl, ATT_HEADS)
    dhc = _mm(dyc, w["w_conv_pw2"], "nt", BF16, "mm_pw2_dx")
    dw("w_conv_pw2", s["hc"], dyc, "mm_pw2_dw")
    dcv, dcw, dcb, dlg, dlb = _conv_bwd(s["proj"], 6, dhc, s["hconv"], s["w32"], _row(p["conv_ln_g"]), _row(p["conv_ln_b"]), bl, c, "conv_bwd")
    g["conv_w"], g["conv_b"], g["conv_ln_g"], g["conv_ln_b"] = dcw, dcb[0], dlg[0], dlb[0]
    dproj = jnp.concatenate([du, dq3, dk.astype(BF16), dv.astype(BF16), dcv, dgl], axis=1)
    dh = _mm(dproj, w["w_in"], "nt", F32, "mm_in_dx")
    dw("w_in", s["h"], dproj, "mm_in_dw")
    dx, dg1 = _rowwise(_rms_bwd_fn, "rms_bwd", n, [("row", s["x"], d, 0), ("par", _row(p["norm1_g"])), ("row", dh, d, 0),
                                                  ("row", dxm, d, 0)], [(d, F32)], [d])
    g["norm1_g"] = dg1[0]
    return dx, g, bufs


WEIGHTS = ['norm1_g', 'w_in', 'b_gate', 'ssm_lambda_re', 'ssm_lambda_im', 'ssm_log_dt', 'ssm_b_re', 'ssm_b_im', 'ssm_c_re',
           'ssm_c_im', 'ssm_d', 'w_ssm_glu', 'w_att_up', 'conv_w', 'conv_b', 'conv_ln_g', 'conv_ln_b', 'w_conv_pw2', 'w_out',
           'norm2_g', 'w_ffn_in', 'w_ffn_out', 'final_g']
BIG = ['w_in', 'w_ssm_glu', 'w_att_up', 'w_conv_pw2', 'w_out', 'w_ffn_in', 'w_ffn_out']
ROW_SHARDED = ('w_out', 'w_ffn_out')
SMALL = [k for k in WEIGHTS if k not in BIG]
LANES = 1024
N_CHIPS = 4
ROW_TILE_BYTES = 36 * 1024 * 1024


def _pad_rows(a, rows):
    return jnp.concatenate([a, jnp.zeros((rows - a.shape[0],) + a.shape[1:], a.dtype)], axis=0) if rows > a.shape[0] else a


def _row_tile(rows, width, n_arrays):
    best = 16
    for t in range(16, rows + 1, 16):
        if rows % t == 0 and t * width * 4 * n_arrays * 2 <= ROW_TILE_BYTES:
            best = t
    return best


def _flat_fn(fn, name, ins, n_out, rows):
    return _rowwise(fn, name, rows, [("row", a, LANES, 0) for a in ins], [(LANES, F32)] * n_out, tm=rows)


def _reduce_big(bufs):
    landed = _swap_layers([b16 for _, b16 in bufs])
    p32s, p16s = [], []
    for (b32, _), la in zip(bufs, landed):
        depth, s, m, cs = b32.shape
        rows = s * m
        tm = _row_tile(rows // N_CHIPS, cs, 4)
        p32, p16 = _rowwise(_add_cast_fn, "rs_add", rows, [("rowoff", b32.reshape(depth * rows, cs), cs, 0, _core_index, rows),
                                                          ("row", la.reshape(rows, cs), cs, 0)], [(cs, F32), (cs, BF16)], tm=tm)
        p32s.append(p32)
        p16s.append(p16.reshape(N_CHIPS, rows // N_CHIPS, cs))
    reduced = []
    for p32, lb in zip(p32s, _scatter_chips(p16s)):
        _, rows, cs = lb.shape
        lb2 = lb.reshape(3 * rows, cs)
        mine = ("rowoff", p32, cs, 0, _chip_index, rows)
        red, = _rowwise(_sum4_fn, "rs_sum", rows, [mine] + [("rowblk", lb2, cs, 0, j * rows) for j in range(3)], [(cs, F32)],
                        tm=_row_tile(rows, cs, 5))
        reduced.append(red)
    return _exchange_layers(reduced)


def _add_cast_fn(a, b):
    s = a + b.astype(F32)
    return s, s


def _sum4_fn(a, b, c, d):
    return (((a.astype(F32) + b.astype(F32)) + c.astype(F32)) + d.astype(F32),)


def _add2_fn(a, b):
    return (a + b,)


def kernel(x, norm1_g, w_in, b_gate, ssm_lambda_re, ssm_lambda_im, ssm_log_dt, ssm_b_re, ssm_b_im, ssm_c_re, ssm_c_im, ssm_d, w_ssm_glu, w_att_up, conv_w, conv_b, conv_ln_g, conv_ln_b, w_conv_pw2, w_out, norm2_g, w_ffn_in, w_ffn_out, final_g, loss_target, m_norm1_g, m_w_in, m_b_gate, m_ssm_lambda_re, m_ssm_lambda_im, m_ssm_log_dt, m_ssm_b_re, m_ssm_b_im, m_ssm_c_re, m_ssm_c_im, m_ssm_d, m_w_ssm_glu, m_w_att_up, m_conv_w, m_conv_b, m_conv_ln_g, m_conv_ln_b, m_w_conv_pw2, m_w_out, m_norm2_g, m_w_ffn_in, m_w_ffn_out, m_final_g, v_norm1_g, v_w_in, v_b_gate, v_ssm_lambda_re, v_ssm_lambda_im, v_ssm_log_dt, v_ssm_b_re, v_ssm_b_im, v_ssm_c_re, v_ssm_c_im, v_ssm_d, v_w_ssm_glu, v_w_att_up, v_conv_w, v_conv_b, v_conv_ln_g, v_conv_ln_b, v_w_conv_pw2, v_w_out, v_norm2_g, v_w_ffn_in, v_w_ffn_out, v_final_g):
    args = dict(locals())
    wts = {k: args[k] for k in WEIGHTS}
    mom = {k: args["m_" + k] for k in WEIGHTS}
    var = {k: args["v_" + k] for k in WEIGHTS}
    bl, seq, d = x.shape
    n = bl * seq
    depth = norm1_g.shape[0]
    cx, cy, cc = _position()
    me = 2 * cx + cy

    gathered = _gather_layers([wts[k].astype(BF16) for k in BIG] + [conv_w])
    full = {}
    for k, a in zip(BIG, gathered):
        full[k] = a.reshape(depth, N_CHIPS * a.shape[2], a.shape[3]) if k in ROW_SHARDED else a
    conv_full = gathered[-1].transpose(0, 2, 1, 3).reshape(depth, CONV_WIDTH, -1)

    def layer_params(l):
        p = {k: wts[k][l] for k in SMALL if k not in ("final_g", "conv_w")}
        p["conv_w"] = conv_full[l]
        return {k: full[k][l] for k in BIG}, p

    to_rows = lambda t: t.transpose(1, 0, 2).reshape(n, d)
    xs = to_rows(x)
    saved = []
    for l in range(depth):
        xs, s = _layer_fwd(xs, *layer_params(l), bl)
        saved.append(s)
    dx, sq, dgf = _rowwise(_loss_fn, "loss_head", n, [("row", xs, d, 0), ("par", _row(final_g)), ("row", to_rows(loss_target), d, 0)],
                           [(d, F32)], [d, d])
    loss = lax.psum(0.5 * jnp.sum(sq) / d, ("x", "y", "c"))

    grads = {"final_g": dgf[0]}
    per_layer, bufs = [], {}
    for l in reversed(range(depth)):
        dx, g, bufs = _layer_bwd(dx, saved[l], *layer_params(l), bl, l, depth, bufs)
        per_layer.append(g)
    per_layer.reverse()
    for k in SMALL:
        if k != "final_g":
            grads[k] = jnp.stack([g[k] for g in per_layer])
    grad_x = dx.reshape(seq, bl, d).transpose(1, 0, 2)
    outs = {}

    for k, gk in zip(BIG, _reduce_big([bufs[k] for k in BIG])):
        shp = wts[k].shape
        rows, cs = shp[0] * shp[1], shp[2]
        flat = lambda t: t.reshape(rows, cs)
        res = _rowwise(_adamw_fn, "adamw", rows, [("row", flat(t), cs, 0) for t in (wts[k], gk, mom[k], var[k])], [(cs, F32)] * 3,
                       tm=_row_tile(rows, cs, 7))
        for tag, a in zip(("grad", "delta", "m", "v"), (gk,) + tuple(res)):
            outs[tag, k] = a.reshape(shp)

    def flat1(t):
        v = jnp.concatenate([t[k].reshape(-1) for k in SMALL])
        rows = -(-v.size // (8 * LANES)) * 8
        return _pad_rows(v, rows * LANES).reshape(rows, LANES), rows

    def unflat1(flat, shapes):
        out, off, v = {}, 0, flat.reshape(-1)
        for k in SMALL:
            size = math.prod(shapes[k])
            out[k] = v[off:off + size].reshape(shapes[k])
            off += size
        return out

    grads["conv_w"] = grads["conv_w"][:, :CONV_WIDTH]
    gs, rows = flat1(grads)
    chip_sum, = _flat_fn(_add2_fn, "ar_add", [gs, _swap_sibling(gs, "ar_swap")], 1, rows)
    slots = _chip_allgather(chip_sum, "ar_gather")
    gs_red, = _flat_fn(_sum4_fn, "ar_sum", [slots[j] for j in range(N_CHIPS)], 1, rows)
    g_sm = unflat1(gs_red, {k: grads[k].shape for k in SMALL})
    cs = conv_w.shape[2]
    g_sm["conv_w"] = lax.dynamic_slice_in_dim(g_sm["conv_w"], me * cs, cs, axis=2)
    (w1, rows), (g1, _), (m1, _), (v1, _) = flat1(wts), flat1(g_sm), flat1(mom), flat1(var)
    sm_out = _flat_fn(_adamw_fn, "adamw_small", [w1, g1, m1, v1], 3, rows)
    shapes = {k: wts[k].shape for k in SMALL}
    for tag, a in zip(("delta", "m", "v"), sm_out):
        for k, t in unflat1(a, shapes).items():
            outs[tag, k] = t
    for k in SMALL:
        outs["grad", k] = g_sm[k]
    return (loss, grad_x, *[outs["grad", k] for k in WEIGHTS], *[outs["delta", k] for k in WEIGHTS],
            *[outs["m", k] for k in WEIGHTS], *[outs["v", k] for k in WEIGHTS])
```

```python
import functools
import math

import jax
import jax.numpy as jnp
from jax import lax
from jax.experimental import pallas as pl
from jax.experimental.pallas import tpu as pltpu

F32 = jnp.float32
BF16 = jnp.bfloat16
VMEM_LIMIT = 56 * 1024 * 1024


def _cparams(sem):
    return pltpu.CompilerParams(dimension_semantics=sem, vmem_limit_bytes=VMEM_LIMIT)


_DIMS = {"nn": (((1,), (0,)), ((), ())), "nt": (((1,), (1,)), ((), ())), "tn": (((0,), (0,)), ((), ()))}


MM_ROWS = 1024
MM_DW_INPUT_ELEMS = 2 * 1024 * 1024


def _div_tile(n, cap):
    best = None
    for t in range(128, min(n, cap) + 1, 128):
        if n % t == 0:
            best = t
    return best or n


def _mm(a, b, form, out_dtype, name, res=None):
    sharded = b.ndim == 3
    kdim, cs = b.shape[-2], b.shape[-1]
    s = b.shape[0] if sharded else 1
    m = a.shape[0]
    tm = MM_ROWS if m % MM_ROWS == 0 else _div_tile(m, MM_ROWS)
    if form == "nn":
        n, kd = s * cs, kdim
        tn, tk = _div_tile(cs, 1792), _div_tile(kdim, 2048)
        per = cs // tn
        b_blk = (tk, tn)
        b_idx = (lambda i, j, k: (j // per, k, j % per)) if sharded else (lambda i, j, k: (k, j))
    else:
        n, kd = kdim, s * cs
        tn, tk = _div_tile(kdim, 1408), _div_tile(cs, 1792)
        per = cs // tk
        b_blk = (tn, tk)
        b_idx = (lambda i, j, k: (k // per, j, k % per)) if sharded else (lambda i, j, k: (j, k))
    nk = kd // tk
    a_spec = pl.BlockSpec((tm, tk), lambda i, j, k: (i, k))
    b_spec = pl.BlockSpec(((None,) + b_blk) if sharded else b_blk, b_idx)
    o_spec = pl.BlockSpec((tm, tn), lambda i, j, k: (i, j))
    dims = _DIMS[form]

    def body(*refs):
        a_ref, b_ref = refs[:2]
        r_ref = refs[2] if res is not None else None
        o_ref = refs[3] if res is not None else refs[2]
        p = lax.dot_general(a_ref[...].astype(BF16), b_ref[...], dims, preferred_element_type=F32)

        def finish(r):
            if r_ref is not None:
                r = r + r_ref[...]
            o_ref[...] = r.astype(out_dtype)

        if nk == 1:
            finish(p)
            return
        acc = refs[-1]
        k = pl.program_id(2)

        @pl.when(k == 0)
        def _():
            acc[...] = p

        @pl.when(k > 0)
        def _():
            acc[...] += p

        @pl.when(k == nk - 1)
        def _():
            finish(acc[...])

    ins = [a, b] + ([] if res is None else [res])
    in_specs = [a_spec, b_spec] + ([] if res is None else [o_spec])
    return pl.pallas_call(
        body, name=name, grid=(m // tm, n // tn, nk), in_specs=in_specs, out_specs=o_spec,
        out_shape=jax.ShapeDtypeStruct((m, n), out_dtype), scratch_shapes=[pltpu.VMEM((tm, tn), F32)] if nk > 1 else [],
        compiler_params=_cparams(("parallel", "parallel", "arbitrary")))(*ins)


def _mm_dw(a, dy, name, shards, layer, depth, into=None):
    r, m = a.shape
    c = dy.shape[1]
    cs = c // shards
    tm, tn = _div_tile(m, 1408), _div_tile(cs, 1792)
    tk = _div_tile(r, max(512, min(2048, MM_DW_INPUT_ELEMS // (tm + tn))))
    per = cs // tn
    nk = r // tk

    def body(*refs):
        a_ref, b_ref = refs[:2]
        o32, o16, acc = refs[-3:]
        k = pl.program_id(2)
        p = lax.dot_general(a_ref[...].astype(BF16), b_ref[...].astype(BF16), _DIMS["tn"], preferred_element_type=F32)

        @pl.when(k == 0)
        def _():
            acc[...] = p

        @pl.when(k > 0)
        def _():
            acc[...] += p

        @pl.when(k == nk - 1)
        def _():
            o32[...] = acc[...]
            o16[...] = acc[...].astype(BF16)

    o_spec = pl.BlockSpec((None, None, tm, tn), lambda i, j, k: (layer, j // per, i, j % per))
    shape = (depth, shards, m, cs)
    ins = [a, dy] + (list(into) if into is not None else [])
    in_specs = [pl.BlockSpec((tk, tm), lambda i, j, k: (k, i)), pl.BlockSpec((tk, tn), lambda i, j, k: (k, j))]
    in_specs += [pl.BlockSpec(memory_space=pltpu.HBM)] * (2 if into is not None else 0)
    return pl.pallas_call(
        body, name=name, grid=(m // tm, c // tn, nk), in_specs=in_specs, out_specs=[o_spec, o_spec],
        out_shape=[jax.ShapeDtypeStruct(shape, F32), jax.ShapeDtypeStruct(shape, BF16)],
        scratch_shapes=[pltpu.VMEM((tm, tn), F32)], input_output_aliases={2: 0, 3: 1} if into is not None else {},
        compiler_params=_cparams(("parallel", "parallel", "arbitrary")))(*ins)


def _core_index():
    return lax.axis_index("c")


def _chip_index():
    return 2 * lax.axis_index("x") + lax.axis_index("y")


def _rowwise(fn, name, n_rows, ins, outs, accs=(), tm=512):
    n_in, n_out, n_acc = len(ins), len(outs), len(accs)
    in_specs, args = [], []
    for spec in ins:
        if spec[0] == "row":
            _, arr, w, cb = spec
            in_specs.append(pl.BlockSpec((tm, w), lambda i, cb=cb: (i, cb)))
        elif spec[0] == "rowoff":
            _, arr, w, cb, index_fn, span = spec
            in_specs.append(pl.BlockSpec((tm, w), lambda i, cb=cb, index_fn=index_fn, nb=span // tm: (index_fn() * nb + i, cb)))
        elif spec[0] == "rowblk":
            _, arr, w, cb, start = spec
            in_specs.append(pl.BlockSpec((tm, w), lambda i, cb=cb, nb=start // tm: (nb + i, cb)))
        else:
            arr = spec[1]
            in_specs.append(pl.BlockSpec(arr.shape, lambda i: (0, 0)))
        args.append(arr)
    out_specs = [pl.BlockSpec((tm, w), lambda i: (i, 0)) for w, _ in outs]
    out_specs += [pl.BlockSpec((1, w), lambda i: (0, 0)) for w in accs]
    out_shape = [jax.ShapeDtypeStruct((n_rows, w), dt) for w, dt in outs]
    out_shape += [jax.ShapeDtypeStruct((1, w), F32) for w in accs]

    def body(*refs):
        i = pl.program_id(0)
        res = fn(*[r[...] for r in refs[:n_in]])
        for o_ref, r in zip(refs[n_in:n_in + n_out], res[:n_out]):
            o_ref[...] = r.astype(o_ref.dtype)
        for a_ref, r in zip(refs[n_in + n_out:], res[n_out:]):
            @pl.when(i == 0)
            def _(a_ref=a_ref, r=r):
                a_ref[...] = r

            @pl.when(i > 0)
            def _(a_ref=a_ref, r=r):
                a_ref[...] += r

    return pl.pallas_call(
        body, name=name, grid=(n_rows // tm,), in_specs=in_specs, out_specs=out_specs, out_shape=out_shape,
        compiler_params=_cparams(("arbitrary",)))(*args)


EPS = 1e-6


def _sig(x):
    return 1.0 / (1.0 + jnp.exp(-x))


def _colsum(x):
    return jnp.sum(x, axis=0, keepdims=True)


def _rms_fwd_fn(x, g):
    r = lax.rsqrt(jnp.mean(x * x, axis=-1, keepdims=True) + EPS)
    return (x * r * g,)


def _rms_bwd_fn(x, g, dh, dres):
    dh = dh.astype(F32)
    r = lax.rsqrt(jnp.mean(x * x, axis=-1, keepdims=True) + EPS)
    xh = x * r
    dyg = dh * g
    dx = r * (dyg - xh * jnp.mean(dyg * xh, axis=-1, keepdims=True)) + dres
    return dx, _colsum(dh * xh)


def _loss_fn(x, g, t):
    d = x.shape[-1]
    r = lax.rsqrt(jnp.mean(x * x, axis=-1, keepdims=True) + EPS)
    xh = x * r
    err = xh * g - t
    dy = err * (1.0 / d)
    dyg = dy * g
    dx = r * (dyg - xh * jnp.mean(dyg * xh, axis=-1, keepdims=True))
    return dx, _colsum(err * err), _colsum(dy * xh)


def _swiglu_fwd_fn(z):
    f = z.shape[-1] // 2
    z1, z2 = z[:, :f].astype(F32), z[:, f:].astype(F32)
    return (z1 * _sig(z1) * z2,)


def _swiglu_bwd_fn(z, da):
    f = z.shape[-1] // 2
    z1, z2, da = z[:, :f].astype(F32), z[:, f:].astype(F32), da.astype(F32)
    s = _sig(z1)
    dz1 = da * z2 * (s * (1.0 + z1 * (1.0 - s)))
    dz2 = da * (z1 * s)
    return (jnp.concatenate([dz1, dz2], axis=1),)


def _merge_fwd_fn(g0, g1, g2, bg, zs, ya, yc):
    d = ya.shape[-1]
    bg = bg.astype(F32)
    zs = zs.astype(F32)
    ys = zs[:, :d] * _sig(zs[:, d:])
    m = _sig(g0.astype(F32) + bg[:, :d]) * ys
    m = m + _sig(g1.astype(F32) + bg[:, d:2 * d]) * ya.astype(F32)
    m = m + _sig(g2.astype(F32) + bg[:, 2 * d:]) * yc.astype(F32)
    return (m,)


def _merge_bwd_fn(g0, g1, g2, bg, zs, ya, yc, dm):
    d = ya.shape[-1]
    bg = bg.astype(F32)
    zs = zs.astype(F32)
    dm = dm.astype(F32)
    z1, s2 = zs[:, :d], _sig(zs[:, d:])
    ys = z1 * s2
    s0 = _sig(g0.astype(F32) + bg[:, :d])
    s1 = _sig(g1.astype(F32) + bg[:, d:2 * d])
    s3 = _sig(g2.astype(F32) + bg[:, 2 * d:])
    dgl = jnp.concatenate([dm * ys * s0 * (1.0 - s0), dm * ya.astype(F32) * s1 * (1.0 - s1),
                           dm * yc.astype(F32) * s3 * (1.0 - s3)], axis=1)
    dys = dm * s0
    dzs = jnp.concatenate([dys * s2, dys * z1 * s2 * (1.0 - s2)], axis=1)
    return dgl, dzs, dm * s1, dm * s3, _colsum(dgl)


def _combine_fwd_fn(o0, o1, o2, l0, l1, l2):
    m = jnp.maximum(jnp.maximum(l0, l1), l2)
    e0, e1, e2 = jnp.exp(l0 - m), jnp.exp(l1 - m), jnp.exp(l2 - m)
    inv = 1.0 / (e0 + e1 + e2)
    return ((e0 * o0.astype(F32) + e1 * o1.astype(F32) + e2 * o2.astype(F32)) * inv,)


def _combine_bwd_fn(o0, o1, o2, l0, l1, l2, do, jmat):
    m = jnp.maximum(jnp.maximum(l0, l1), l2)
    e0, e1, e2 = jnp.exp(l0 - m), jnp.exp(l1 - m), jnp.exp(l2 - m)
    inv = 1.0 / (e0 + e1 + e2)
    w0, w1, w2 = e0 * inv, e1 * inv, e2 * inv
    do = do.astype(F32)

    def headsum(x):
        return jnp.dot(x, jmat, preferred_element_type=F32, precision=lax.Precision.HIGHEST)

    dw0, dw1, dw2 = headsum(do * o0.astype(F32)), headsum(do * o1.astype(F32)), headsum(do * o2.astype(F32))
    mean = w0 * dw0 + w1 * dw1 + w2 * dw2
    return w0 * do, w1 * do, w2 * do, w0 * (dw0 - mean), w1 * (dw1 - mean), w2 * (dw2 - mean)


ADAM_LR, ADAM_B1, ADAM_B2, ADAM_EPS, ADAM_WD, ADAM_STEP = 0.001, 0.9, 0.999, 1e-08, 0.01, 10


def _adamw_fn(w, g, m, v):
    m = ADAM_B1 * m + (1.0 - ADAM_B1) * g
    v = ADAM_B2 * v + (1.0 - ADAM_B2) * (g * g)
    m_hat = m / (1.0 - ADAM_B1 ** ADAM_STEP)
    v_hat = v / (1.0 - ADAM_B2 ** ADAM_STEP)
    delta = -ADAM_LR * (m_hat / (jnp.sqrt(v_hat) + ADAM_EPS) + ADAM_WD * w)
    return delta, m, v


CONV_WIDTH = 31


def _conv_fwd(proj, cb, w32, conv_b, ln_g, ln_b, bl, c, name, tm=512):
    n = proj.shape[0]
    hp = (CONV_WIDTH - 1) * bl
    nt = n // tm

    def body(ap_ref, gp_ref, a_ref, g_ref, w_ref, cb_ref, lg_ref, lb_ref, hc_ref, hconv_ref, ext):
        i = pl.program_id(0)
        ext[pl.ds(hp, tm), :] = a_ref[...].astype(F32) * _sig(g_ref[...].astype(F32))
        hgp = ap_ref[pl.ds(tm - hp, hp), :].astype(F32) * _sig(gp_ref[pl.ds(tm - hp, hp), :].astype(F32))
        ext[pl.ds(0, hp), :] = jnp.where(i > 0, hgp, 0.0)
        acc = jnp.zeros((tm, c), F32) + cb_ref[...]
        for j in range(CONV_WIDTH):
            acc = acc + w_ref[j:j + 1, :] * ext[pl.ds(j * bl, tm), :]
        hconv_ref[...] = acc.astype(hconv_ref.dtype)
        h = hconv_ref[...].astype(F32)
        mu = jnp.mean(h, axis=-1, keepdims=True)
        xc = h - mu
        var = jnp.mean(xc * xc, axis=-1, keepdims=True)
        hn = xc * lax.rsqrt(var + EPS) * lg_ref[...] + lb_ref[...]
        hc_ref[...] = (hn * _sig(hn)).astype(hc_ref.dtype)

    prev = lambda i, k: (jnp.maximum(i - 1, 0), k)
    par = lambda arr: pl.BlockSpec(arr.shape, lambda i: (0, 0))
    return pl.pallas_call(
        body, name=name, grid=(nt,),
        in_specs=[pl.BlockSpec((tm, c), functools.partial(prev, k=cb)), pl.BlockSpec((tm, c), functools.partial(prev, k=cb + 1)),
                  pl.BlockSpec((tm, c), lambda i: (i, cb)), pl.BlockSpec((tm, c), lambda i: (i, cb + 1)),
                  par(w32), par(conv_b), par(ln_g), par(ln_b)],
        out_specs=[pl.BlockSpec((tm, c), lambda i: (i, 0))] * 2,
        out_shape=[jax.ShapeDtypeStruct((n, c), BF16)] * 2,
        scratch_shapes=[pltpu.VMEM((hp + tm, c), F32)],
        compiler_params=_cparams(("arbitrary",)))(proj, proj, proj, proj, w32, conv_b, ln_g, ln_b)


def _conv_bwd(proj, cb, dhc, hconv, w32, ln_g, ln_b, bl, c, name, tm=512):
    n = proj.shape[0]
    hp = (CONV_WIDTH - 1) * bl
    nt = n // tm

    def ln_bwd(d, h, lg, lb):
        d, h = d.astype(F32), h.astype(F32)
        mu = jnp.mean(h, axis=-1, keepdims=True)
        xc = h - mu
        rstd = lax.rsqrt(jnp.mean(xc * xc, axis=-1, keepdims=True) + EPS)
        xh = xc * rstd
        hn = xh * lg + lb
        s = _sig(hn)
        dhn = d * (s * (1.0 + hn * (1.0 - s)))
        dxh = dhn * lg
        dh = rstd * (dxh - jnp.mean(dxh, axis=-1, keepdims=True) - xh * jnp.mean(dxh * xh, axis=-1, keepdims=True))
        return dh, dhn, xh

    def body(ap_ref, gp_ref, a_ref, g_ref, d_ref, dn_ref, h_ref, hn_ref, w_ref, lg_ref, lb_ref,
             dcv_ref, dw_ref, dcb_ref, dlg_ref, dlb_ref, ext_h, ext_d):
        i = pl.program_id(0)
        lg, lb = lg_ref[...], lb_ref[...]
        a, g = a_ref[...].astype(F32), g_ref[...].astype(F32)
        sg = _sig(g)
        ext_h[pl.ds(hp, tm), :] = a * sg
        hgp = ap_ref[pl.ds(tm - hp, hp), :].astype(F32) * _sig(gp_ref[pl.ds(tm - hp, hp), :].astype(F32))
        ext_h[pl.ds(0, hp), :] = jnp.where(i > 0, hgp, 0.0)
        dh, dhn, xh = ln_bwd(d_ref[...], h_ref[...], lg, lb)
        ext_d[pl.ds(0, tm), :] = dh
        dh_n, _, _ = ln_bwd(dn_ref[pl.ds(0, hp), :], hn_ref[pl.ds(0, hp), :], lg, lb)
        ext_d[pl.ds(tm, hp), :] = jnp.where(i < nt - 1, dh_n, 0.0)

        @pl.when(i == 0)
        def _():
            dw_ref[...] = jnp.zeros_like(dw_ref)
            dcb_ref[...] = jnp.zeros_like(dcb_ref)
            dlg_ref[...] = jnp.zeros_like(dlg_ref)
            dlb_ref[...] = jnp.zeros_like(dlb_ref)

        dcb_ref[...] += _colsum(dh)
        dlg_ref[...] += _colsum(dhn * xh)
        dlb_ref[...] += _colsum(dhn)
        dhg = jnp.zeros((tm, c), F32)
        for j in range(CONV_WIDTH):
            dhg = dhg + w_ref[j:j + 1, :] * ext_d[pl.ds((CONV_WIDTH - 1 - j) * bl, tm), :]
            dw_ref[j:j + 1, :] += _colsum(dh * ext_h[pl.ds(j * bl, tm), :])
        dcv_ref[...] = jnp.concatenate([dhg * sg, dhg * a * sg * (1.0 - sg)], axis=1).astype(dcv_ref.dtype)

    prev = lambda i, k: (jnp.maximum(i - 1, 0), k)
    nxt = lambda i: (jnp.minimum(i + 1, nt - 1), 0)
    cur = lambda i: (i, 0)
    par = lambda arr: pl.BlockSpec(arr.shape, lambda i: (0, 0))
    acc = lambda r: pl.BlockSpec((r, c), lambda i: (0, 0))
    return pl.pallas_call(
        body, name=name, grid=(nt,),
        in_specs=[pl.BlockSpec((tm, c), functools.partial(prev, k=cb)), pl.BlockSpec((tm, c), functools.partial(prev, k=cb + 1)),
                  pl.BlockSpec((tm, c), lambda i: (i, cb)), pl.BlockSpec((tm, c), lambda i: (i, cb + 1)),
                  pl.BlockSpec((tm, c), cur), pl.BlockSpec((tm, c), nxt), pl.BlockSpec((tm, c), cur), pl.BlockSpec((tm, c), nxt),
                  par(w32), par(ln_g), par(ln_b)],
        out_specs=[pl.BlockSpec((tm, 2 * c), cur), acc(32), acc(1), acc(1), acc(1)],
        out_shape=[jax.ShapeDtypeStruct((n, 2 * c), BF16), jax.ShapeDtypeStruct((32, c), F32)] + [jax.ShapeDtypeStruct((1, c), F32)] * 3,
        scratch_shapes=[pltpu.VMEM((hp + tm, c), F32), pltpu.VMEM((hp + tm, c), F32)],
        compiler_params=_cparams(("arbitrary",)))(proj, proj, proj, proj, dhc, dhc, hconv, hconv, w32, ln_g, ln_b)


SSM_CH = 128
_GELU_C = 0.7978845608028654


def _gelu(x):
    return 0.5 * x * (1.0 + jnp.tanh(_GELU_C * (x + 0.044715 * x * x * x)))


def _gelu_grad(x):
    th = jnp.tanh(_GELU_C * (x + 0.044715 * x * x * x))
    return 0.5 * (1.0 + th) + 0.5 * x * (1.0 - th * th) * (_GELU_C * (1.0 + 3.0 * 0.044715 * x * x))


def _ssm_disc(lam_re, lam_im, log_dt, b_re, b_im):
    dt = jnp.exp(log_dt)[:, None]
    mag = jnp.exp(lam_re * dt)
    ab_re = mag * jnp.cos(lam_im * dt)
    ab_im = mag * jnp.sin(lam_im * dt)
    nr, ni = ab_re - 1.0, ab_im
    den = lam_re * lam_re + lam_im * lam_im
    z_re = ((nr * lam_re + ni * lam_im) / den)[..., None]
    z_im = ((ni * lam_re - nr * lam_im) / den)[..., None]
    return ab_re, ab_im, z_re * b_re - z_im * b_im, z_re * b_im + z_im * b_re


def _ssm_pack(ab_re, ab_im, bb_re, bb_im, c_re, c_im):
    g, p, h = bb_re.shape
    gc = SSM_CH // h
    nc = g // gc
    eye = jnp.eye(gc, dtype=F32)
    blk = lambda x: jnp.einsum("qgph,gk->qghkp", x.reshape(nc, gc, p, h), eye).reshape(nc, gc * h, gc * p)
    bbd = jnp.concatenate([blk(bb_re), blk(bb_im)], axis=2).astype(BF16)
    blc = lambda x: jnp.einsum("qghp,gk->qgpkh", x.reshape(nc, gc, h, p), eye).reshape(nc, gc * p, gc * h)
    cdm = jnp.concatenate([blc(c_re), blc(-c_im)], axis=1).astype(BF16)
    a = jnp.concatenate([ab_re.reshape(nc, gc * p), ab_im.reshape(nc, gc * p)], axis=1)
    a8 = jnp.broadcast_to(a[:, None, :], (nc, 8, 2 * gc * p)).reshape(nc * 8, 2 * gc * p)
    return bbd, cdm, a8


def _ssm_unpack(dbb, dcd, da, g, p, h):
    gc = SSM_CH // h
    nc = g // gc
    ph = gc * p
    eye = jnp.eye(gc, dtype=F32)
    dia = lambda x, o: jnp.einsum("qgpkh,gk->" + o, x.reshape(nc, gc, p, gc, h), eye).reshape((g, p, h) if o == "qgph" else (g, h, p))
    das = da.reshape(nc, 8, 2 * ph).sum(axis=1)
    return (das[:, :ph].reshape(g, p), das[:, ph:].reshape(g, p), dia(dbb[:, :ph], "qgph"), dia(dbb[:, ph:], "qgph"),
            dia(dcd[:, :ph], "qghp"), -dia(dcd[:, ph:], "qghp"))


def _ssm_fwd(proj, bbd, cdm, a8, dskip, bl, name, tm=1024):
    n = proj.shape[0]
    nc, ch, p2 = bbd.shape
    ph = p2 // 2
    nt = n // tm
    nsub = 8 // bl

    def body(u_ref, bb_ref, cd_ref, a_ref, d_ref, ypre_ref, yg_ref, s_ref, bu, carry):
        t = pl.program_id(1)

        @pl.when(t == 0)
        def _():
            carry[...] = jnp.zeros_like(carry)

        u = u_ref[...]
        bu[...] = jnp.dot(u, bb_ref[0], preferred_element_type=F32)
        a_re, a_im = a_ref[:, :ph], a_ref[:, ph:]
        row = lax.broadcasted_iota(jnp.int32, (8, ph), 0)

        def step(k, c):
            cre, cim = c
            r0 = pl.multiple_of(k * 8, 8)
            bre, bim = bu[pl.ds(r0, 8), :ph], bu[pl.ds(r0, 8), ph:]
            sre, sim = cre, cim
            for sub in range(nsub):
                xre, xim = pltpu.roll(cre, bl, 0), pltpu.roll(cim, bl, 0)
                cre = a_re * xre - a_im * xim + bre
                cim = a_re * xim + a_im * xre + bim
                if sub == 0:
                    sre, sim = cre, cim
                else:
                    sel = row >= sub * bl
                    sre, sim = jnp.where(sel, cre, sre), jnp.where(sel, cim, sim)
            bu[pl.ds(r0, 8), :ph] = sre
            bu[pl.ds(r0, 8), ph:] = sim
            return sre, sim

        cre, cim = lax.fori_loop(0, tm // 8, step, (carry[:, :ph], carry[:, ph:]))
        carry[:, :ph] = cre
        carry[:, ph:] = cim
        s16 = bu[...].astype(BF16)
        s_ref[...] = s16
        y = jnp.dot(s16, cd_ref[0], preferred_element_type=F32) + d_ref[...] * u.astype(F32)
        ypre_ref[...] = y
        yg_ref[...] = _gelu(y).astype(yg_ref.dtype)

    return pl.pallas_call(
        body, name=name, grid=(nc, nt),
        in_specs=[pl.BlockSpec((tm, ch), lambda q, t: (t, q)), pl.BlockSpec((1, ch, p2), lambda q, t: (q, 0, 0)),
                  pl.BlockSpec((1, p2, ch), lambda q, t: (q, 0, 0)), pl.BlockSpec((8, p2), lambda q, t: (q, 0)),
                  pl.BlockSpec((1, ch), lambda q, t: (0, q))],
        out_specs=[pl.BlockSpec((tm, ch), lambda q, t: (t, q)), pl.BlockSpec((tm, ch), lambda q, t: (t, q)),
                   pl.BlockSpec((tm, p2), lambda q, t: (t, q))],
        out_shape=[jax.ShapeDtypeStruct((n, nc * ch), F32), jax.ShapeDtypeStruct((n, nc * ch), BF16),
                   jax.ShapeDtypeStruct((n, nc * p2), BF16)],
        scratch_shapes=[pltpu.VMEM((tm, p2), F32), pltpu.VMEM((8, p2), F32)],
        compiler_params=_cparams(("parallel", "arbitrary")))(proj, bbd, cdm, a8, dskip)


def _ssm_bwd(dyg, ypre, proj, s_all, cdt, bbt, a8, dskip, bl, name, tm=1024):
    n = proj.shape[0]
    nc, ch, p2 = cdt.shape
    ph = p2 // 2
    nt = n // tm
    nsub = 8 // bl
    tn_dims = (((0,), (0,)), ((), ()))

    def body(dyg_ref, ypre_ref, u_ref, s_ref, cdt_ref, bbt_ref, a_ref, d_ref,
             du_ref, dbb_ref, dcd_ref, da_ref, dd_ref, ds, s32, carry):
        t = pl.program_id(1)

        @pl.when(t == 0)
        def _():
            carry[...] = jnp.zeros_like(carry)
            dbb_ref[...] = jnp.zeros_like(dbb_ref)
            dcd_ref[...] = jnp.zeros_like(dcd_ref)
            da_ref[...] = jnp.zeros_like(da_ref)
            dd_ref[...] = jnp.zeros_like(dd_ref)

        dyp = dyg_ref[...].astype(F32) * _gelu_grad(ypre_ref[...])
        u = u_ref[...]
        dd_ref[...] += _colsum(dyp * u.astype(F32))
        dyp16 = dyp.astype(BF16)
        ds[...] = jnp.dot(dyp16, cdt_ref[0], preferred_element_type=F32)
        s16 = s_ref[...]
        s32[...] = s16.astype(F32)
        a_re, a_im = a_ref[:, :ph], a_ref[:, ph:]
        row = lax.broadcasted_iota(jnp.int32, (8, ph), 0)
        back = 8 - bl

        def step(kk, c):
            lre, lim, acr, aci = c
            r0 = pl.multiple_of((tm // 8 - 1 - kk) * 8, 8)
            dre, dim = ds[pl.ds(r0, 8), :ph], ds[pl.ds(r0, 8), ph:]
            sre, sim = s32[pl.ds(r0, 8), :ph], s32[pl.ds(r0, 8), ph:]
            ore, oim, ire, iim = lre, lim, lre, lim
            for sub in range(nsub - 1, -1, -1):
                xre, xim = pltpu.roll(lre, back, 0), pltpu.roll(lim, back, 0)
                lre = a_re * xre + a_im * xim + dre
                lim = a_re * xim - a_im * xre + dim
                if sub == nsub - 1:
                    ore, oim, ire, iim = lre, lim, xre, xim
                else:
                    sel = row < (sub + 1) * bl
                    ore, oim = jnp.where(sel, lre, ore), jnp.where(sel, lim, oim)
                    ire, iim = jnp.where(sel, xre, ire), jnp.where(sel, xim, iim)
            ds[pl.ds(r0, 8), :ph] = ore
            ds[pl.ds(r0, 8), ph:] = oim
            acr = acr + sre * ire + sim * iim
            aci = aci + sre * iim - sim * ire
            return ore, oim, acr, aci

        z = jnp.zeros((8, ph), F32)
        lre, lim, acr, aci = lax.fori_loop(0, tm // 8, step, (carry[:, :ph], carry[:, ph:], z, z))
        carry[:, :ph] = lre
        carry[:, ph:] = lim
        da_ref[:, :ph] += acr
        da_ref[:, ph:] += aci
        lam16 = ds[...].astype(BF16)
        du = jnp.dot(lam16, bbt_ref[0], preferred_element_type=F32) + d_ref[...] * dyp
        du_ref[...] = du.astype(du_ref.dtype)
        dbb_ref[0] += lax.dot_general(lam16, u, tn_dims, preferred_element_type=F32)
        dcd_ref[0] += lax.dot_general(s16, dyp16, tn_dims, preferred_element_type=F32)

    rev = lambda q, t: (nt - 1 - t, q)
    return pl.pallas_call(
        body, name=name, grid=(nc, nt),
        in_specs=[pl.BlockSpec((tm, ch), rev), pl.BlockSpec((tm, ch), rev), pl.BlockSpec((tm, ch), rev),
                  pl.BlockSpec((tm, p2), rev), pl.BlockSpec((1, ch, p2), lambda q, t: (q, 0, 0)),
                  pl.BlockSpec((1, p2, ch), lambda q, t: (q, 0, 0)), pl.BlockSpec((8, p2), lambda q, t: (q, 0)),
                  pl.BlockSpec((1, ch), lambda q, t: (0, q))],
        out_specs=[pl.BlockSpec((tm, ch), rev), pl.BlockSpec((1, p2, ch), lambda q, t: (q, 0, 0)),
                   pl.BlockSpec((1, p2, ch), lambda q, t: (q, 0, 0)), pl.BlockSpec((8, p2), lambda q, t: (q, 0)),
                   pl.BlockSpec((1, ch), lambda q, t: (0, q))],
        out_shape=[jax.ShapeDtypeStruct((n, nc * ch), BF16), jax.ShapeDtypeStruct((nc, p2, ch), F32),
                   jax.ShapeDtypeStruct((nc, p2, ch), F32), jax.ShapeDtypeStruct((nc * 8, p2), F32),
                   jax.ShapeDtypeStruct((1, nc * ch), F32)],
        scratch_shapes=[pltpu.VMEM((tm, p2), F32), pltpu.VMEM((tm, p2), F32), pltpu.VMEM((8, p2), F32)],
        compiler_params=_cparams(("parallel", "arbitrary")))(dyg, ypre, proj, s_all, cdt, bbt, a8, dskip)


_MESH = pl.DeviceIdType.MESH
_HBM = pl.BlockSpec(memory_space=pltpu.HBM)


def _position():
    return lax.axis_index("x"), lax.axis_index("y"), lax.axis_index("c")


def _other_chips(x, y):
    return [((1 - x, y), 2 * (1 - x) + y), ((x, 1 - y), 2 * x + 1 - y), ((1 - x, 1 - y), 2 * (1 - x) + 1 - y)]


def _swap_sibling(v, name):
    def body(v_ref, got_ref, send_sem, recv_sem):
        x, y, c = _position()
        cp = pltpu.make_async_remote_copy(src_ref=v_ref, dst_ref=got_ref, send_sem=send_sem, recv_sem=recv_sem,
                                          device_id=(x, y, 1 - c), device_id_type=_MESH)
        cp.start()
        cp.wait()

    return pl.pallas_call(
        body, name=name, in_specs=[_HBM], out_specs=_HBM, out_shape=jax.ShapeDtypeStruct(v.shape, v.dtype),
        scratch_shapes=[pltpu.SemaphoreType.DMA, pltpu.SemaphoreType.DMA])(v)


def _chip_allgather(v, name):
    def body(v_ref, out_ref, send_sems, recv_sems, local_sem):
        x, y, c = _position()
        me = 2 * x + y
        mine = pltpu.make_async_copy(v_ref, out_ref.at[me], local_sem)
        mine.start()
        sends = []
        for k, (chip, idx) in enumerate(_other_chips(x, y)):
            cp = pltpu.make_async_remote_copy(src_ref=v_ref, dst_ref=out_ref.at[me], send_sem=send_sems.at[k],
                                              recv_sem=recv_sems.at[k], device_id=(*chip, c), device_id_type=_MESH)
            cp.start()
            sends.append(cp)
        for k, (chip, idx) in enumerate(_other_chips(x, y)):
            pltpu.make_async_remote_copy(src_ref=v_ref, dst_ref=out_ref.at[idx], send_sem=send_sems.at[k],
                                         recv_sem=recv_sems.at[k], device_id=(*chip, c), device_id_type=_MESH).wait_recv()
        for cp in sends:
            cp.wait_send()
        mine.wait()

    return pl.pallas_call(
        body, name=name, in_specs=[_HBM], out_specs=_HBM, out_shape=jax.ShapeDtypeStruct((4,) + tuple(v.shape), v.dtype),
        scratch_shapes=[pltpu.SemaphoreType.DMA((3,)), pltpu.SemaphoreType.DMA((3,)), pltpu.SemaphoreType.DMA])(v)


def _remote(src, dst, send_sems, recv_sems, s, device):
    return pltpu.make_async_remote_copy(src_ref=src, dst_ref=dst, send_sem=send_sems.at[s], recv_sem=recv_sems.at[s],
                                        device_id=device, device_id_type=_MESH)


def _comm_call(body, name, ins, out_shapes, n_sems, n_local=0):
    scratch = [pltpu.SemaphoreType.DMA((n_sems,)), pltpu.SemaphoreType.DMA((n_sems,))]
    scratch += [pltpu.SemaphoreType.DMA((n_local,))] if n_local else []
    return pl.pallas_call(body, name=name, in_specs=[_HBM] * len(ins), out_specs=[_HBM] * len(out_shapes),
                          out_shape=out_shapes, scratch_shapes=scratch)(*ins)


def _gather_layers(ws):
    n = len(ws)

    def body(*refs):
        w_refs, out_refs, (send_sems, recv_sems, local_sems) = refs[:n], refs[n:2 * n], refs[2 * n:]
        x, y, c = _position()
        me = 2 * x + y
        sibling = (x, y, 1 - c)
        others = _other_chips(x, y)
        pending = []
        for i in range(n):
            for l in range(2):
                cp = pltpu.make_async_copy(w_refs[i].at[l], out_refs[i].at[l, me], local_sems.at[2 * i + l])
                cp.start()
                pending.append(cp.wait)
            for k, (chip, idx) in enumerate(others):
                cp = _remote(w_refs[i].at[c], out_refs[i].at[c, me], send_sems, recv_sems, 6 * i + k, (*chip, c))
                cp.start()
                pending.append(cp.wait_send)
        for i in range(n):
            for k, (chip, idx) in enumerate(others):
                landed = out_refs[i].at[c, idx]
                _remote(landed, landed, send_sems, recv_sems, 6 * i + k, (*chip, c)).wait_recv()
                cp = _remote(landed, landed, send_sems, recv_sems, 6 * i + 3 + k, sibling)
                cp.start()
                pending.append(cp.wait_send)
        for i in range(n):
            for k, (chip, idx) in enumerate(others):
                theirs = out_refs[i].at[1 - c, idx]
                _remote(theirs, theirs, send_sems, recv_sems, 6 * i + 3 + k, sibling).wait_recv()
        for wait in pending:
            wait()

    shapes = [jax.ShapeDtypeStruct((w.shape[0], N_CHIPS) + tuple(w.shape[1:]), w.dtype) for w in ws]
    return _comm_call(body, "gather_weights", ws, shapes, 6 * n, 2 * n)


def _swap_layers(gs):
    n = len(gs)

    def body(*refs):
        g_refs, out_refs, (send_sems, recv_sems) = refs[:n], refs[n:2 * n], refs[2 * n:]
        x, y, c = _position()
        cps = [_remote(g_refs[i].at[1 - c], out_refs[i], send_sems, recv_sems, i, (x, y, 1 - c)) for i in range(n)]
        for cp in cps:
            cp.start()
        for cp in cps:
            cp.wait()

    return _comm_call(body, "rs_swap", gs, [jax.ShapeDtypeStruct(g.shape[1:], g.dtype) for g in gs], n)


def _scatter_chips(ps):
    n = len(ps)

    def body(*refs):
        p_refs, out_refs, (send_sems, recv_sems) = refs[:n], refs[n:2 * n], refs[2 * n:]
        x, y, c = _position()
        cps = [_remote(p_refs[i].at[idx], out_refs[i].at[k], send_sems, recv_sems, 3 * i + k, (*chip, c))
               for i in range(n) for k, (chip, idx) in enumerate(_other_chips(x, y))]
        for cp in cps:
            cp.start()
        for cp in cps:
            cp.wait()

    return _comm_call(body, "rs_scatter", ps, [jax.ShapeDtypeStruct((3,) + tuple(p.shape[1:]), p.dtype) for p in ps], 3 * n)


def _exchange_layers(rs):
    n = len(rs)

    def body(*refs):
        r_refs, out_refs, (send_sems, recv_sems, local_sems) = refs[:n], refs[n:2 * n], refs[2 * n:]
        x, y, c = _position()
        cps = []
        for i in range(n):
            cps.append(pltpu.make_async_copy(r_refs[i], out_refs[i].at[c], local_sems.at[i]))
            cps.append(_remote(r_refs[i], out_refs[i].at[c], send_sems, recv_sems, i, (x, y, 1 - c)))
        for cp in cps:
            cp.start()
        for i in range(n):
            cps[2 * i].wait()
            cps[2 * i + 1].wait_send()
            theirs = out_refs[i].at[1 - c]
            _remote(theirs, theirs, send_sems, recv_sems, i, (x, y, 1 - c)).wait_recv()

    return _comm_call(body, "rs_gather", rs, [jax.ShapeDtypeStruct((2,) + tuple(r.shape), r.dtype) for r in rs], n, n)


ATT_WINDOW = 128
PHASES = 16
_NT = (((1,), (1,)), ((), ()))
_TN = (((0,), (0,)), ((), ()))


def _to_phase_order(x, bl):
    n, c = x.shape
    g = n // bl // PHASES
    return x.reshape(g, PHASES, bl, c).transpose(2, 1, 0, 3).reshape(n, c)


def _from_phase_order(x, bl):
    n, c = x.shape
    g = n // bl // PHASES
    return x.reshape(bl, PHASES, g, c).transpose(2, 1, 0, 3).reshape(n, c)


def _att_geometry(p, n, bl):
    g = n // bl // PHASES
    if p == 0:
        return ((bl, PHASES, g), (bl, g // 16), (None, PHASES, 16),
                lambda sh: (lambda b, a: (b, 0, jnp.maximum(a + sh, 0))), 256, 16, lambda ids: ids[1] == 0)
    if p == 1:
        return ((bl, 4, 4, g), (bl, 4, g // 32), (None, 4, None, 32),
                lambda sh: (lambda b, r, a: (b, 0, r, jnp.maximum(a + sh, 0))), 128, 32, lambda ids: ids[2] == 0)
    return ((bl * PHASES, g), (bl * PHASES,), (None, g), lambda sh: (lambda s: (s, 0)), g, g, None)


def _att_masks(p, qb, chunk):
    def pos(idx):
        return (idx % chunk) * (qb // chunk) + idx // chunk

    dq = pos(lax.broadcasted_iota(jnp.int32, (qb, qb), 0))
    dk = pos(lax.broadcasted_iota(jnp.int32, (qb, qb), 1))
    dist = dq - dk
    return jnp.logical_and(dist >= 0, dist <= ATT_WINDOW), dist + qb <= ATT_WINDOW


def _att_call(p, n, bl, c, body, name, ins, outs):
    prefix, grid, blk, idx_fn, qb, chunk, _ = _att_geometry(p, n, bl)

    def spec(cb, sh):
        f = idx_fn(sh)
        return pl.BlockSpec(blk + (c,), lambda *ids, f=f, cb=cb: f(*ids) + (cb,))

    in_specs = [spec(cb, sh) for _, cb, sh in ins]
    out_specs = [spec(0, 0) for _ in outs]
    out_shape = [jax.ShapeDtypeStruct(prefix + (c,), dt) for dt in outs]
    res = pl.pallas_call(body, name=name, grid=grid, in_specs=in_specs, out_specs=out_specs, out_shape=out_shape,
                         compiler_params=_cparams(("parallel",) * len(grid)))(*[a.reshape(prefix + (a.shape[1],)) for a, _, _ in ins])
    return [r.reshape(n, c) for r in res]


def _att_fwd(p, qkv, qcb, bl, c, heads):
    n = qkv.shape[0]
    _, grid, _, _, qb, chunk, first_fn = _att_geometry(p, n, bl)
    n_grid = len(grid)
    has_prev = first_fn is not None
    e = c // heads
    scale = e ** -0.5

    def body(*refs):
        if has_prev:
            q_ref, kc_ref, kp_ref, vc_ref, vp_ref, o_ref, l_ref = refs
        else:
            q_ref, kc_ref, vc_ref, o_ref, l_ref = refs
        ids = [pl.program_id(a) for a in range(n_grid)]
        mc, mp = _att_masks(p, qb, chunk)
        if has_prev:
            mp = jnp.logical_and(mp, jnp.logical_not(first_fn(ids)))
        lo = lax.broadcasted_iota(jnp.int32, (qb, 128), 1) < e
        shp = o_ref.shape[:-1]
        for t in range(c // 128):
            ls = pl.ds(t * 128, 128)
            load = lambda r: r[..., ls].reshape(qb, 128)
            q2, kc, vc = load(q_ref), load(kc_ref), load(vc_ref)
            if has_prev:
                kp, vp = load(kp_ref), load(vp_ref)
            o_pair, l_pair = None, None
            for h in range(2):
                qm = jnp.where(lo if h == 0 else jnp.logical_not(lo), q2, jnp.zeros_like(q2))
                sc = jnp.where(mc, lax.dot_general(qm, kc, _NT, preferred_element_type=F32) * scale, -jnp.inf)
                m = jnp.max(sc, axis=1, keepdims=True)
                if has_prev:
                    sp = jnp.where(mp, lax.dot_general(qm, kp, _NT, preferred_element_type=F32) * scale, -jnp.inf)
                    m = jnp.maximum(m, jnp.max(sp, axis=1, keepdims=True))
                pc = jnp.exp(sc - m)
                den = jnp.sum(pc, axis=1, keepdims=True)
                acc = jnp.dot(pc.astype(BF16), vc, preferred_element_type=F32)
                if has_prev:
                    pp = jnp.exp(sp - m)
                    den = den + jnp.sum(pp, axis=1, keepdims=True)
                    acc = acc + jnp.dot(pp.astype(BF16), vp, preferred_element_type=F32)
                oh = acc * (1.0 / den)
                lh = jnp.broadcast_to(m + jnp.log(den), (qb, 128))
                o_pair = oh if h == 0 else jnp.where(lo, o_pair, oh)
                l_pair = lh if h == 0 else jnp.where(lo, l_pair, lh)
            o_ref[..., ls] = o_pair.astype(o_ref.dtype).reshape(shp + (128,))
            l_ref[..., ls] = l_pair.reshape(shp + (128,))

    kcb, vcb = 3, 4
    ins = [(qkv, qcb, 0), (qkv, kcb, 0)] + ([(qkv, kcb, -1)] if has_prev else []) + [(qkv, vcb, 0)] + ([(qkv, vcb, -1)] if has_prev else [])
    return _att_call(p, n, bl, c, body, name=f"att_fwd{p}", ins=ins, outs=[BF16, F32])


def _att_bwd(p, qkv, qcb, o, do, lse, dlse, bl, c, heads):
    n = qkv.shape[0]
    _, grid, _, _, qb, chunk, first_fn = _att_geometry(p, n, bl)
    n_grid = len(grid)
    has_prev = first_fn is not None
    e = c // heads
    scale = e ** -0.5

    def body(*refs):
        if has_prev:
            q_ref, kc_ref, kp_ref, vc_ref, vp_ref, o_ref, do_ref, l_ref, dl_ref, dq_ref, dkc_ref, dkp_ref, dvc_ref, dvp_ref = refs
        else:
            q_ref, kc_ref, vc_ref, o_ref, do_ref, l_ref, dl_ref, dq_ref, dkc_ref, dvc_ref = refs
        ids = [pl.program_id(a) for a in range(n_grid)]
        mc, mp = _att_masks(p, qb, chunk)
        if has_prev:
            mp = jnp.logical_and(mp, jnp.logical_not(first_fn(ids)))
        lo = lax.broadcasted_iota(jnp.int32, (qb, 128), 1) < e
        shp = o_ref.shape[:-1]
        for t in range(c // 128):
            ls = pl.ds(t * 128, 128)
            load = lambda r: r[..., ls].reshape(qb, 128)

            def store(r, v, ls=ls):
                r[..., ls] = v.astype(r.dtype).reshape(shp + (128,))

            q2, kc, vc, o2, do2, l2, dl2 = (load(r) for r in (q_ref, kc_ref, vc_ref, o_ref, do_ref, l_ref, dl_ref))
            if has_prev:
                kp, vp = load(kp_ref), load(vp_ref)
            dod = do2.astype(F32) * o2.astype(F32)
            dq_pair = None
            dkc = dvc = dkp = dvp = None
            for h in range(2):
                sel = lo if h == 0 else jnp.logical_not(lo)
                qm = jnp.where(sel, q2, jnp.zeros_like(q2))
                dom = jnp.where(sel, do2, jnp.zeros_like(do2))
                lcol, dlcol = l2[:, h * e:h * e + 1], dl2[:, h * e:h * e + 1]
                corr = dlcol - jnp.sum(jnp.where(sel, dod, 0.0), axis=1, keepdims=True)
                sc = jnp.where(mc, lax.dot_general(qm, kc, _NT, preferred_element_type=F32) * scale, -jnp.inf)
                pc = jnp.exp(sc - lcol)
                dsc = (pc * (lax.dot_general(dom, vc, _NT, preferred_element_type=F32) + corr) * scale).astype(BF16)
                pc = pc.astype(BF16)
                dqh = jnp.dot(dsc, kc, preferred_element_type=F32)
                add = lambda a, b: b if a is None else a + b
                dkc = add(dkc, lax.dot_general(dsc, qm, _TN, preferred_element_type=F32))
                dvc = add(dvc, lax.dot_general(pc, dom, _TN, preferred_element_type=F32))
                if has_prev:
                    sp = jnp.where(mp, lax.dot_general(qm, kp, _NT, preferred_element_type=F32) * scale, -jnp.inf)
                    pp = jnp.exp(sp - lcol)
                    dsp = (pp * (lax.dot_general(dom, vp, _NT, preferred_element_type=F32) + corr) * scale).astype(BF16)
                    pp = pp.astype(BF16)
                    dqh = dqh + jnp.dot(dsp, kp, preferred_element_type=F32)
                    dkp = add(dkp, lax.dot_general(dsp, qm, _TN, preferred_element_type=F32))
                    dvp = add(dvp, lax.dot_general(pp, dom, _TN, preferred_element_type=F32))
                dq_pair = dqh if h == 0 else jnp.where(lo, dq_pair, dqh)
            store(dq_ref, dq_pair)
            store(dkc_ref, dkc)
            store(dvc_ref, dvc)
            if has_prev:
                store(dkp_ref, dkp)
                store(dvp_ref, dvp)

    kcb, vcb = 3, 4
    ins = [(qkv, qcb, 0), (qkv, kcb, 0)] + ([(qkv, kcb, -1)] if has_prev else []) + [(qkv, vcb, 0)] + ([(qkv, vcb, -1)] if has_prev else [])
    ins += [(o, 0, 0), (do, 0, 0), (lse, 0, 0), (dlse, 0, 0)]
    res = _att_call(p, n, bl, c, body, name=f"att_bwd{p}", ins=ins, outs=[BF16] * (5 if has_prev else 3))
    if has_prev:
        dq, dkc, dkp, dvc, dvp = res
        return dq, dkc, dkp, dvc, dvp
    dq, dkc, dvc = res
    return dq, dkc, None, dvc, None


def _att_fold_prev(p, cur, prv, bl):
    if prv is None:
        return cur.astype(F32)
    n, c = cur.shape
    prefix, _, _, _, _, chunk, _ = _att_geometry(p, n, bl)
    v = prv.reshape(prefix + (c,)).astype(F32)
    shifted = jnp.concatenate([v[..., chunk:, :], jnp.zeros_like(v[..., :chunk, :])], axis=-2)
    return cur.astype(F32) + shifted.reshape(n, c)


def _attention_fwd2(qkv_tb, bl, heads):
    n, w = qkv_tb.shape
    c = w // 5
    qkv = _to_phase_order(qkv_tb, bl)
    outs = [_att_fwd(p, qkv, p, bl, c, heads) for p in range(3)]
    ins = [("row", o, c, 0) for o, _ in outs] + [("row", l, c, 0) for _, l in outs]
    o, = _rowwise(_combine_fwd_fn, "comb_fwd", n, ins, [(c, BF16)])
    return _from_phase_order(o, bl), (qkv, outs)


def _attention_bwd2(do_tb, saved, bl, heads):
    qkv, outs = saved
    n, c = do_tb.shape
    e = c // heads
    lane = jnp.arange(c) // e
    jmat = (lane[:, None] == lane[None, :]).astype(F32)
    do = _to_phase_order(do_tb, bl)
    ins = [("row", o, c, 0) for o, _ in outs] + [("row", l, c, 0) for _, l in outs] + [("row", do, c, 0), ("par", jmat)]
    res = _rowwise(_combine_bwd_fn, "comb_bwd", n, ins, [(c, BF16)] * 3 + [(c, F32)] * 3)
    dqs, dk, dv = [], 0.0, 0.0
    for p in range(3):
        dq, dkc, dkp, dvc, dvp = _att_bwd(p, qkv, p, outs[p][0], res[p], outs[p][1], res[3 + p], bl, c, heads)
        dqs.append(dq)
        dk = dk + _att_fold_prev(p, dkc, dkp, bl)
        dv = dv + _att_fold_prev(p, dvc, dvp, bl)
    dqkv = jnp.concatenate(dqs + [dk.astype(BF16), dv.astype(BF16)], axis=1)
    return _from_phase_order(dqkv, bl)


ATT_HEADS = 8
SSM_GROUPS, SSM_STATE, SSM_GROUP = 32, 64, 16


def _row(v):
    return v.reshape(1, -1)


def _layer_fwd(x, w, p, bl):
    n, d = x.shape
    c = d // 2
    h, = _rowwise(_rms_fwd_fn, "rms_fwd", n, [("row", x, d, 0), ("par", _row(p["norm1_g"]))], [(d, BF16)])
    proj = _mm(h, w["w_in"], "nn", BF16, "mm_in")
    disc, disc_vjp = jax.vjp(_ssm_disc, p["ssm_lambda_re"], p["ssm_lambda_im"], p["ssm_log_dt"], p["ssm_b_re"], p["ssm_b_im"])
    bbd, cdm, a8 = _ssm_pack(*disc, p["ssm_c_re"], p["ssm_c_im"])
    ypre, yg, s_all = _ssm_fwd(proj, bbd, cdm, a8, _row(p["ssm_d"]), bl, "ssm_fwd")
    zs = _mm(yg, w["w_ssm_glu"], "nn", BF16, "mm_glu")
    o, att = _attention_fwd2(proj[:, c:6 * c], bl, ATT_HEADS)
    ya = _mm(o, w["w_att_up"], "nn", BF16, "mm_att")
    w32 = jnp.concatenate([p["conv_w"], jnp.zeros((1, c), F32)], axis=0)
    hc, hconv = _conv_fwd(proj, 6, w32, _row(p["conv_b"]), _row(p["conv_ln_g"]), _row(p["conv_ln_b"]), bl, c, "conv_fwd")
    yc = _mm(hc, w["w_conv_pw2"], "nn", BF16, "mm_pw2")
    gates = [("row", proj, d, 4), ("row", proj, d, 5), ("row", proj, d, 6), ("par", _row(p["b_gate"]))]
    branches = [("row", zs, 2 * d, 0), ("row", ya, d, 0), ("row", yc, d, 0)]
    merged, = _rowwise(_merge_fwd_fn, "merge_fwd", n, gates + branches, [(d, BF16)])
    xm = _mm(merged, w["w_out"], "nn", F32, "mm_out", res=x)
    h2, = _rowwise(_rms_fwd_fn, "rms_fwd", n, [("row", xm, d, 0), ("par", _row(p["norm2_g"]))], [(d, BF16)])
    z = _mm(h2, w["w_ffn_in"], "nn", BF16, "mm_ffn_in")
    f = z.shape[1] // 2
    a, = _rowwise(_swiglu_fwd_fn, "swiglu_fwd", n, [("row", z, 2 * f, 0)], [(f, BF16)], tm=256)
    xo = _mm(a, w["w_ffn_out"], "nn", F32, "mm_ffn_out", res=xm)
    saved = dict(x=x, h=h, proj=proj, disc_vjp=disc_vjp, bbd=bbd, cdm=cdm, a8=a8, ypre=ypre, yg=yg, s_all=s_all, zs=zs, o=o,
                 att=att, ya=ya, w32=w32, hc=hc, hconv=hconv, yc=yc, gates=gates, branches=branches, merged=merged, xm=xm,
                 h2=h2, z=z, a=a)
    return xo, saved


def _layer_bwd(dxo, s, w, p, bl, layer, depth, bufs):
    n, d = dxo.shape
    c = d // 2
    g = {}
    f = s["a"].shape[1]

    def dw(key, a, dy, name):
        bufs[key] = _mm_dw(a, dy, name, 1 if key in ROW_SHARDED else N_CHIPS, layer, depth, bufs.get(key))

    da = _mm(dxo, w["w_ffn_out"], "nt", BF16, "mm_ffn_out_dx")
    dw("w_ffn_out", s["a"], dxo, "mm_ffn_out_dw")
    dz, = _rowwise(_swiglu_bwd_fn, "swiglu_bwd", n, [("row", s["z"], 2 * f, 0), ("row", da, f, 0)], [(2 * f, BF16)], tm=256)
    dh2 = _mm(dz, w["w_ffn_in"], "nt", F32, "mm_ffn_in_dx")
    dw("w_ffn_in", s["h2"], dz, "mm_ffn_in_dw")
    dxm, dg2 = _rowwise(_rms_bwd_fn, "rms_bwd", n, [("row", s["xm"], d, 0), ("par", _row(p["norm2_g"])), ("row", dh2, d, 0),
                                                   ("row", dxo, d, 0)], [(d, F32)], [d])
    g["norm2_g"] = dg2[0]
    dmerged = _mm(dxm, w["w_out"], "nt", BF16, "mm_out_dx")
    dw("w_out", s["merged"], dxm, "mm_out_dw")
    dgl, dzs, dya, dyc, dbg = _rowwise(_merge_bwd_fn, "merge_bwd", n, s["gates"] + s["branches"] + [("row", dmerged, d, 0)],
                                       [(3 * d, BF16), (2 * d, BF16), (d, BF16), (d, BF16)], [3 * d], tm=256)
    g["b_gate"] = dbg[0]
    dyg = _mm(dzs, w["w_ssm_glu"], "nt", BF16, "mm_glu_dx")
    dw("w_ssm_glu", s["yg"], dzs, "mm_glu_dw")
    du, dbb, dcd, dab, dd = _ssm_bwd(dyg, s["ypre"], s["proj"], s["s_all"], s["cdm"].transpose(0, 2, 1), s["bbd"].transpose(0, 2, 1),
                                     s["a8"], _row(p["ssm_d"]), bl, "ssm_bwd")
    dab_re, dab_im, dbb_re, dbb_im, g["ssm_c_re"], g["ssm_c_im"] = _ssm_unpack(dbb, dcd, dab, SSM_GROUPS, SSM_STATE, SSM_GROUP)
    (g["ssm_lambda_re"], g["ssm_lambda_im"], g["ssm_log_dt"], g["ssm_b_re"],
     g["ssm_b_im"]) = s["disc_vjp"]((dab_re, dab_im, dbb_re, dbb_im))
    g["ssm_d"] = dd[0]
    do = _mm(dya, w["w_att_up"], "nt", BF16, "mm_att_dx")
    dw("w_att_up", s["o"], dya, "mm_att_dw")
    dqkv = _attention_bwd2(do, s["att"], bl, ATT_HEADS)
    dhc = _mm(dyc, w["w_conv_pw2"], "nt", BF16, "mm_pw2_dx")
    dw("w_conv_pw2", s["hc"], dyc, "mm_pw2_dw")
    dcv, dcw, dcb, dlg, dlb = _conv_bwd(s["proj"], 6, dhc, s["hconv"], s["w32"], _row(p["conv_ln_g"]), _row(p["conv_ln_b"]), bl, c, "conv_bwd")
    g["conv_w"], g["conv_b"], g["conv_ln_g"], g["conv_ln_b"] = dcw, dcb[0], dlg[0], dlb[0]
    dproj = jnp.concatenate([du, dqkv, dcv, dgl], axis=1)
    dh = _mm(dproj, w["w_in"], "nt", F32, "mm_in_dx")
    dw("w_in", s["h"], dproj, "mm_in_dw")
    dx, dg1 = _rowwise(_rms_bwd_fn, "rms_bwd", n, [("row", s["x"], d, 0), ("par", _row(p["norm1_g"])), ("row", dh, d, 0),
                                                  ("row", dxm, d, 0)], [(d, F32)], [d])
    g["norm1_g"] = dg1[0]
    return dx, g, bufs


WEIGHTS = ['norm1_g', 'w_in', 'b_gate', 'ssm_lambda_re', 'ssm_lambda_im', 'ssm_log_dt', 'ssm_b_re', 'ssm_b_im', 'ssm_c_re',
           'ssm_c_im', 'ssm_d', 'w_ssm_glu', 'w_att_up', 'conv_w', 'conv_b', 'conv_ln_g', 'conv_ln_b', 'w_conv_pw2', 'w_out',
           'norm2_g', 'w_ffn_in', 'w_ffn_out', 'final_g']
BIG = ['w_in', 'w_ssm_glu', 'w_att_up', 'w_conv_pw2', 'w_out', 'w_ffn_in', 'w_ffn_out']
ROW_SHARDED = ('w_out', 'w_ffn_out')
SMALL = [k for k in WEIGHTS if k not in BIG]
LANES = 1024
N_CHIPS = 4
ROW_TILE_BYTES = 36 * 1024 * 1024
MIN_SHARD_TILE = 1024


def _pad_rows(a, rows):
    return jnp.concatenate([a, jnp.zeros((rows - a.shape[0],) + a.shape[1:], a.dtype)], axis=0) if rows > a.shape[0] else a


def _row_tile(rows, width, n_arrays):
    best = 16
    for t in range(16, rows + 1, 16):
        if rows % t == 0 and t * width * 4 * n_arrays * 2 <= ROW_TILE_BYTES:
            best = t
    return best


def _flat_fn(fn, name, ins, n_out, rows):
    return _rowwise(fn, name, rows, [("row", a, LANES, 0) for a in ins], [(LANES, F32)] * n_out, tm=rows)


def _reduce_big(bufs):
    landed = _swap_layers([b16 for _, b16 in bufs])
    p32s, p16s = [], []
    for (b32, _), la in zip(bufs, landed):
        depth, s, m, cs = b32.shape
        rows = s * m
        tm = _row_tile(rows // N_CHIPS, cs, 4)
        p32, p16 = _rowwise(_add_cast_fn, "rs_add", rows, [("rowoff", b32.reshape(depth * rows, cs), cs, 0, _core_index, rows),
                                                          ("row", la.reshape(rows, cs), cs, 0)], [(cs, F32), (cs, BF16)], tm=tm)
        p32s.append(p32)
        p16s.append(p16.reshape(N_CHIPS, rows // N_CHIPS, cs))
    reduced = []
    for p32, lb in zip(p32s, _scatter_chips(p16s)):
        _, rows, cs = lb.shape
        lb2 = lb.reshape(3 * rows, cs)
        mine = ("rowoff", p32, cs, 0, _chip_index, rows)
        red, = _rowwise(_sum4_fn, "rs_sum", rows, [mine] + [("rowblk", lb2, cs, 0, j * rows) for j in range(3)], [(cs, F32)],
                        tm=_row_tile(rows, cs, 5))
        reduced.append(red)
    return _exchange_layers(reduced)


def _add_cast_fn(a, b):
    s = a + b.astype(F32)
    return s, s


def _sum4_fn(a, b, c, d):
    return (((a.astype(F32) + b.astype(F32)) + c.astype(F32)) + d.astype(F32),)


def _add2_fn(a, b):
    return (a + b,)


def kernel(x, norm1_g, w_in, b_gate, ssm_lambda_re, ssm_lambda_im, ssm_log_dt, ssm_b_re, ssm_b_im, ssm_c_re, ssm_c_im, ssm_d, w_ssm_glu, w_att_up, conv_w, conv_b, conv_ln_g, conv_ln_b, w_conv_pw2, w_out, norm2_g, w_ffn_in, w_ffn_out, final_g, loss_target, m_norm1_g, m_w_in, m_b_gate, m_ssm_lambda_re, m_ssm_lambda_im, m_ssm_log_dt, m_ssm_b_re, m_ssm_b_im, m_ssm_c_re, m_ssm_c_im, m_ssm_d, m_w_ssm_glu, m_w_att_up, m_conv_w, m_conv_b, m_conv_ln_g, m_conv_ln_b, m_w_conv_pw2, m_w_out, m_norm2_g, m_w_ffn_in, m_w_ffn_out, m_final_g, v_norm1_g, v_w_in, v_b_gate, v_ssm_lambda_re, v_ssm_lambda_im, v_ssm_log_dt, v_ssm_b_re, v_ssm_b_im, v_ssm_c_re, v_ssm_c_im, v_ssm_d, v_w_ssm_glu, v_w_att_up, v_conv_w, v_conv_b, v_conv_ln_g, v_conv_ln_b, v_w_conv_pw2, v_w_out, v_norm2_g, v_w_ffn_in, v_w_ffn_out, v_final_g):
    args = dict(locals())
    wts = {k: args[k] for k in WEIGHTS}
    mom = {k: args["m_" + k] for k in WEIGHTS}
    var = {k: args["v_" + k] for k in WEIGHTS}
    bl, seq, d = x.shape
    n = bl * seq
    depth = norm1_g.shape[0]
    cx, cy, cc = _position()
    me = 2 * cx + cy

    gathered = _gather_layers([wts[k].astype(BF16) for k in BIG] + [conv_w])
    full = {}
    for k, a in zip(BIG, gathered):
        _, _, ks, cs = a.shape
        if k in ROW_SHARDED:
            full[k] = a.reshape(depth, N_CHIPS * ks, cs)
        elif cs < MIN_SHARD_TILE:
            full[k] = a.transpose(0, 2, 1, 3).reshape(depth, ks, N_CHIPS * cs)
        else:
            full[k] = a
    conv_full = gathered[-1].transpose(0, 2, 1, 3).reshape(depth, CONV_WIDTH, -1)

    def layer_params(l):
        p = {k: wts[k][l] for k in SMALL if k not in ("final_g", "conv_w")}
        p["conv_w"] = conv_full[l]
        return {k: full[k][l] for k in BIG}, p

    to_rows = lambda t: t.transpose(1, 0, 2).reshape(n, d)
    xs = to_rows(x)
    saved = []
    for l in range(depth):
        xs, s = _layer_fwd(xs, *layer_params(l), bl)
        saved.append(s)
    dx, sq, dgf = _rowwise(_loss_fn, "loss_head", n, [("row", xs, d, 0), ("par", _row(final_g)), ("row", to_rows(loss_target), d, 0)],
                           [(d, F32)], [d, d])
    loss = lax.psum(0.5 * jnp.sum(sq) / d, ("x", "y", "c"))

    grads = {"final_g": dgf[0]}
    per_layer, bufs = [], {}
    for l in reversed(range(depth)):
        dx, g, bufs = _layer_bwd(dx, saved[l], *layer_params(l), bl, l, depth, bufs)
        per_layer.append(g)
    per_layer.reverse()
    for k in SMALL:
        if k != "final_g":
            grads[k] = jnp.stack([g[k] for g in per_layer])
    grad_x = dx.reshape(seq, bl, d).transpose(1, 0, 2)
    outs = {}

    for k, gk in zip(BIG, _reduce_big([bufs[k] for k in BIG])):
        shp = wts[k].shape
        rows, cs = shp[0] * shp[1], shp[2]
        flat = lambda t: t.reshape(rows, cs)
        res = _rowwise(_adamw_fn, "adamw", rows, [("row", flat(t), cs, 0) for t in (wts[k], gk, mom[k], var[k])], [(cs, F32)] * 3,
                       tm=_row_tile(rows, cs, 7))
        for tag, a in zip(("grad", "delta", "m", "v"), (gk,) + tuple(res)):
            outs[tag, k] = a.reshape(shp)

    def flat1(t):
        v = jnp.concatenate([t[k].reshape(-1) for k in SMALL])
        rows = -(-v.size // (8 * LANES)) * 8
        return _pad_rows(v, rows * LANES).reshape(rows, LANES), rows

    def unflat1(flat, shapes):
        out, off, v = {}, 0, flat.reshape(-1)
        for k in SMALL:
            size = math.prod(shapes[k])
            out[k] = v[off:off + size].reshape(shapes[k])
            off += size
        return out

    grads["conv_w"] = grads["conv_w"][:, :CONV_WIDTH]
    gs, rows = flat1(grads)
    chip_sum, = _flat_fn(_add2_fn, "ar_add", [gs, _swap_sibling(gs, "ar_swap")], 1, rows)
    slots = _chip_allgather(chip_sum, "ar_gather")
    gs_red, = _flat_fn(_sum4_fn, "ar_sum", [slots[j] for j in range(N_CHIPS)], 1, rows)
    g_sm = unflat1(gs_red, {k: grads[k].shape for k in SMALL})
    cs = conv_w.shape[2]
    g_sm["conv_w"] = lax.dynamic_slice_in_dim(g_sm["conv_w"], me * cs, cs, axis=2)
    (w1, rows), (g1, _), (m1, _), (v1, _) = flat1(wts), flat1(g_sm), flat1(mom), flat1(var)
    sm_out = _flat_fn(_adamw_fn, "adamw_small", [w1, g1, m1, v1], 3, rows)
    shapes = {k: wts[k].shape for k in SMALL}
    for tag, a in zip(("delta", "m", "v"), sm_out):
        for k, t in unflat1(a, shapes).items():
            outs[tag, k] = t
    for k in SMALL:
        outs["grad", k] = g_sm[k]
    return (loss, grad_x, *[outs["grad", k] for k in WEIGHTS], *[outs["delta", k] for k in WEIGHTS],
            *[outs["m", k] for k in WEIGHTS], *[outs["v", k] for k in WEIGHTS])
```

```python
import functools
import math

import jax
import jax.numpy as jnp
from jax import lax
from jax.experimental import pallas as pl
from jax.experimental.pallas import tpu as pltpu

F32 = jnp.float32
BF16 = jnp.bfloat16
VMEM_LIMIT = 56 * 1024 * 1024


def _cparams(sem):
    return pltpu.CompilerParams(dimension_semantics=sem, vmem_limit_bytes=VMEM_LIMIT)


_DIMS = {"nn": (((1,), (0,)), ((), ())), "nt": (((1,), (1,)), ((), ())), "tn": (((0,), (0,)), ((), ()))}


MM_ROWS = 1024
MM_DW_INPUT_ELEMS = 2 * 1024 * 1024


def _div_tile(n, cap):
    best = None
    for t in range(128, min(n, cap) + 1, 128):
        if n % t == 0:
            best = t
    return best or n


def _mm(a, b, form, out_dtype, name, res=None):
    sharded = b.ndim == 3
    kdim, cs = b.shape[-2], b.shape[-1]
    s = b.shape[0] if sharded else 1
    m = a.shape[0]
    tm = MM_ROWS if m % MM_ROWS == 0 else _div_tile(m, MM_ROWS)
    if form == "nn":
        n, kd = s * cs, kdim
        tn, tk = _div_tile(cs, 1792), _div_tile(kdim, 2048)
        per = cs // tn
        b_blk = (tk, tn)
        b_idx = (lambda i, j, k: (j // per, k, j % per)) if sharded else (lambda i, j, k: (k, j))
    else:
        n, kd = kdim, s * cs
        tn, tk = _div_tile(kdim, 1408), _div_tile(cs, 1792)
        per = cs // tk
        b_blk = (tn, tk)
        b_idx = (lambda i, j, k: (k // per, j, k % per)) if sharded else (lambda i, j, k: (j, k))
    nk = kd // tk
    a_spec = pl.BlockSpec((tm, tk), lambda i, j, k: (i, k))
    b_spec = pl.BlockSpec(((None,) + b_blk) if sharded else b_blk, b_idx)
    o_spec = pl.BlockSpec((tm, tn), lambda i, j, k: (i, j))
    dims = _DIMS[form]

    def body(*refs):
        a_ref, b_ref = refs[:2]
        r_ref = refs[2] if res is not None else None
        o_ref = refs[3] if res is not None else refs[2]
        p = lax.dot_general(a_ref[...].astype(BF16), b_ref[...], dims, preferred_element_type=F32)

        def finish(r):
            if r_ref is not None:
                r = r + r_ref[...]
            o_ref[...] = r.astype(out_dtype)

        if nk == 1:
            finish(p)
            return
        acc = refs[-1]
        k = pl.program_id(2)

        @pl.when(k == 0)
        def _():
            acc[...] = p

        @pl.when(k > 0)
        def _():
            acc[...] += p

        @pl.when(k == nk - 1)
        def _():
            finish(acc[...])

    ins = [a, b] + ([] if res is None else [res])
    in_specs = [a_spec, b_spec] + ([] if res is None else [o_spec])
    return pl.pallas_call(
        body, name=name, grid=(m // tm, n // tn, nk), in_specs=in_specs, out_specs=o_spec,
        out_shape=jax.ShapeDtypeStruct((m, n), out_dtype), scratch_shapes=[pltpu.VMEM((tm, tn), F32)] if nk > 1 else [],
        compiler_params=_cparams(("parallel", "parallel", "arbitrary")))(*ins)


def _mm_dw(a, dy, name, shards, layer, depth, into=None):
    r, m = a.shape
    c = dy.shape[1]
    cs = c // shards
    tm, tn = _div_tile(m, 1408), _div_tile(cs, 1792)
    tk = _div_tile(r, max(512, min(2048, MM_DW_INPUT_ELEMS // (tm + tn))))
    per = cs // tn
    nk = r // tk

    def body(*refs):
        a_ref, b_ref = refs[:2]
        o32, o16, acc = refs[-3:]
        k = pl.program_id(2)
        p = lax.dot_general(a_ref[...].astype(BF16), b_ref[...].astype(BF16), _DIMS["tn"], preferred_element_type=F32)

        @pl.when(k == 0)
        def _():
            acc[...] = p

        @pl.when(k > 0)
        def _():
            acc[...] += p

        @pl.when(k == nk - 1)
        def _():
            o32[...] = acc[...]
            o16[...] = acc[...].astype(BF16)

    o_spec = pl.BlockSpec((None, None, tm, tn), lambda i, j, k: (layer, j // per, i, j % per))
    shape = (depth, shards, m, cs)
    ins = [a, dy] + (list(into) if into is not None else [])
    in_specs = [pl.BlockSpec((tk, tm), lambda i, j, k: (k, i)), pl.BlockSpec((tk, tn), lambda i, j, k: (k, j))]
    in_specs += [pl.BlockSpec(memory_space=pltpu.HBM)] * (2 if into is not None else 0)
    return pl.pallas_call(
        body, name=name, grid=(m // tm, c // tn, nk), in_specs=in_specs, out_specs=[o_spec, o_spec],
        out_shape=[jax.ShapeDtypeStruct(shape, F32), jax.ShapeDtypeStruct(shape, BF16)],
        scratch_shapes=[pltpu.VMEM((tm, tn), F32)], input_output_aliases={2: 0, 3: 1} if into is not None else {},
        compiler_params=_cparams(("parallel", "parallel", "arbitrary")))(*ins)


def _core_index():
    return lax.axis_index("c")


def _chip_index():
    return 2 * lax.axis_index("x") + lax.axis_index("y")


def _rowwise(fn, name, n_rows, ins, outs, accs=(), tm=512):
    n_in, n_out, n_acc = len(ins), len(outs), len(accs)
    in_specs, args = [], []
    for spec in ins:
        if spec[0] == "row":
            _, arr, w, cb = spec
            in_specs.append(pl.BlockSpec((tm, w), lambda i, cb=cb: (i, cb)))
        elif spec[0] == "rowoff":
            _, arr, w, cb, index_fn, span = spec
            in_specs.append(pl.BlockSpec((tm, w), lambda i, cb=cb, index_fn=index_fn, nb=span // tm: (index_fn() * nb + i, cb)))
        elif spec[0] == "rowblk":
            _, arr, w, cb, start = spec
            in_specs.append(pl.BlockSpec((tm, w), lambda i, cb=cb, nb=start // tm: (nb + i, cb)))
        else:
            arr = spec[1]
            in_specs.append(pl.BlockSpec(arr.shape, lambda i: (0, 0)))
        args.append(arr)
    out_specs = [pl.BlockSpec((tm, w), lambda i: (i, 0)) for w, _ in outs]
    out_specs += [pl.BlockSpec((1, w), lambda i: (0, 0)) for w in accs]
    out_shape = [jax.ShapeDtypeStruct((n_rows, w), dt) for w, dt in outs]
    out_shape += [jax.ShapeDtypeStruct((1, w), F32) for w in accs]

    def body(*refs):
        i = pl.program_id(0)
        res = fn(*[r[...] for r in refs[:n_in]])
        for o_ref, r in zip(refs[n_in:n_in + n_out], res[:n_out]):
            o_ref[...] = r.astype(o_ref.dtype)
        for a_ref, r in zip(refs[n_in + n_out:], res[n_out:]):
            @pl.when(i == 0)
            def _(a_ref=a_ref, r=r):
                a_ref[...] = r

            @pl.when(i > 0)
            def _(a_ref=a_ref, r=r):
                a_ref[...] += r

    return pl.pallas_call(
        body, name=name, grid=(n_rows // tm,), in_specs=in_specs, out_specs=out_specs, out_shape=out_shape,
        compiler_params=_cparams(("arbitrary",)))(*args)


EPS = 1e-6


def _sig(x):
    return 1.0 / (1.0 + jnp.exp(-x))


def _colsum(x):
    return jnp.sum(x, axis=0, keepdims=True)


def _rms_fwd_fn(x, g):
    r = lax.rsqrt(jnp.mean(x * x, axis=-1, keepdims=True) + EPS)
    return (x * r * g,)


def _rms_bwd_fn(x, g, dh, dres):
    dh = dh.astype(F32)
    r = lax.rsqrt(jnp.mean(x * x, axis=-1, keepdims=True) + EPS)
    xh = x * r
    dyg = dh * g
    dx = r * (dyg - xh * jnp.mean(dyg * xh, axis=-1, keepdims=True)) + dres
    return dx, _colsum(dh * xh)


def _loss_fn(x, g, t):
    d = x.shape[-1]
    r = lax.rsqrt(jnp.mean(x * x, axis=-1, keepdims=True) + EPS)
    xh = x * r
    err = xh * g - t
    dy = err * (1.0 / d)
    dyg = dy * g
    dx = r * (dyg - xh * jnp.mean(dyg * xh, axis=-1, keepdims=True))
    return dx, _colsum(err * err), _colsum(dy * xh)


def _swiglu_fwd_fn(z):
    f = z.shape[-1] // 2
    z1, z2 = z[:, :f].astype(F32), z[:, f:].astype(F32)
    return (z1 * _sig(z1) * z2,)


def _swiglu_bwd_fn(z, da):
    f = z.shape[-1] // 2
    z1, z2, da = z[:, :f].astype(F32), z[:, f:].astype(F32), da.astype(F32)
    s = _sig(z1)
    dz1 = da * z2 * (s * (1.0 + z1 * (1.0 - s)))
    dz2 = da * (z1 * s)
    return (jnp.concatenate([dz1, dz2], axis=1),)


def _merge_fwd_fn(g0, g1, g2, bg, zs, ya, yc):
    d = ya.shape[-1]
    bg = bg.astype(F32)
    zs = zs.astype(F32)
    ys = zs[:, :d] * _sig(zs[:, d:])
    m = _sig(g0.astype(F32) + bg[:, :d]) * ys
    m = m + _sig(g1.astype(F32) + bg[:, d:2 * d]) * ya.astype(F32)
    m = m + _sig(g2.astype(F32) + bg[:, 2 * d:]) * yc.astype(F32)
    return (m,)


def _merge_bwd_fn(g0, g1, g2, bg, zs, ya, yc, dm):
    d = ya.shape[-1]
    bg = bg.astype(F32)
    zs = zs.astype(F32)
    dm = dm.astype(F32)
    z1, s2 = zs[:, :d], _sig(zs[:, d:])
    ys = z1 * s2
    s0 = _sig(g0.astype(F32) + bg[:, :d])
    s1 = _sig(g1.astype(F32) + bg[:, d:2 * d])
    s3 = _sig(g2.astype(F32) + bg[:, 2 * d:])
    dgl = jnp.concatenate([dm * ys * s0 * (1.0 - s0), dm * ya.astype(F32) * s1 * (1.0 - s1),
                           dm * yc.astype(F32) * s3 * (1.0 - s3)], axis=1)
    dys = dm * s0
    dzs = jnp.concatenate([dys * s2, dys * z1 * s2 * (1.0 - s2)], axis=1)
    return dgl, dzs, dm * s1, dm * s3, _colsum(dgl)


def _combine_fwd_fn(o0, o1, o2, l0, l1, l2):
    m = jnp.maximum(jnp.maximum(l0, l1), l2)
    e0, e1, e2 = jnp.exp(l0 - m), jnp.exp(l1 - m), jnp.exp(l2 - m)
    inv = 1.0 / (e0 + e1 + e2)
    return ((e0 * o0.astype(F32) + e1 * o1.astype(F32) + e2 * o2.astype(F32)) * inv,)


def _combine_bwd_fn(o0, o1, o2, l0, l1, l2, do, jmat):
    m = jnp.maximum(jnp.maximum(l0, l1), l2)
    e0, e1, e2 = jnp.exp(l0 - m), jnp.exp(l1 - m), jnp.exp(l2 - m)
    inv = 1.0 / (e0 + e1 + e2)
    w0, w1, w2 = e0 * inv, e1 * inv, e2 * inv
    do = do.astype(F32)

    def headsum(x):
        return jnp.dot(x, jmat, preferred_element_type=F32, precision=lax.Precision.HIGHEST)

    dw0, dw1, dw2 = headsum(do * o0.astype(F32)), headsum(do * o1.astype(F32)), headsum(do * o2.astype(F32))
    mean = w0 * dw0 + w1 * dw1 + w2 * dw2
    return w0 * do, w1 * do, w2 * do, w0 * (dw0 - mean), w1 * (dw1 - mean), w2 * (dw2 - mean)


ADAM_LR, ADAM_B1, ADAM_B2, ADAM_EPS, ADAM_WD, ADAM_STEP = 0.001, 0.9, 0.999, 1e-08, 0.01, 10


def _adamw_fn(w, g, m, v):
    m = ADAM_B1 * m + (1.0 - ADAM_B1) * g
    v = ADAM_B2 * v + (1.0 - ADAM_B2) * (g * g)
    m_hat = m / (1.0 - ADAM_B1 ** ADAM_STEP)
    v_hat = v / (1.0 - ADAM_B2 ** ADAM_STEP)
    delta = -ADAM_LR * (m_hat / (jnp.sqrt(v_hat) + ADAM_EPS) + ADAM_WD * w)
    return delta, m, v


CONV_WIDTH = 31


def _conv_fwd(proj, cb, w32, conv_b, ln_g, ln_b, bl, c, name, tm=512):
    n = proj.shape[0]
    hp = (CONV_WIDTH - 1) * bl
    nt = n // tm

    def body(ap_ref, gp_ref, a_ref, g_ref, w_ref, cb_ref, lg_ref, lb_ref, hc_ref, hconv_ref, ext):
        i = pl.program_id(0)
        ext[pl.ds(hp, tm), :] = a_ref[...].astype(F32) * _sig(g_ref[...].astype(F32))
        hgp = ap_ref[pl.ds(tm - hp, hp), :].astype(F32) * _sig(gp_ref[pl.ds(tm - hp, hp), :].astype(F32))
        ext[pl.ds(0, hp), :] = jnp.where(i > 0, hgp, 0.0)
        acc = jnp.zeros((tm, c), F32) + cb_ref[...]
        for j in range(CONV_WIDTH):
            acc = acc + w_ref[j:j + 1, :] * ext[pl.ds(j * bl, tm), :]
        hconv_ref[...] = acc.astype(hconv_ref.dtype)
        h = hconv_ref[...].astype(F32)
        mu = jnp.mean(h, axis=-1, keepdims=True)
        xc = h - mu
        var = jnp.mean(xc * xc, axis=-1, keepdims=True)
        hn = xc * lax.rsqrt(var + EPS) * lg_ref[...] + lb_ref[...]
        hc_ref[...] = (hn * _sig(hn)).astype(hc_ref.dtype)

    prev = lambda i, k: (jnp.maximum(i - 1, 0), k)
    par = lambda arr: pl.BlockSpec(arr.shape, lambda i: (0, 0))
    return pl.pallas_call(
        body, name=name, grid=(nt,),
        in_specs=[pl.BlockSpec((tm, c), functools.partial(prev, k=cb)), pl.BlockSpec((tm, c), functools.partial(prev, k=cb + 1)),
                  pl.BlockSpec((tm, c), lambda i: (i, cb)), pl.BlockSpec((tm, c), lambda i: (i, cb + 1)),
                  par(w32), par(conv_b), par(ln_g), par(ln_b)],
        out_specs=[pl.BlockSpec((tm, c), lambda i: (i, 0))] * 2,
        out_shape=[jax.ShapeDtypeStruct((n, c), BF16)] * 2,
        scratch_shapes=[pltpu.VMEM((hp + tm, c), F32)],
        compiler_params=_cparams(("arbitrary",)))(proj, proj, proj, proj, w32, conv_b, ln_g, ln_b)


def _conv_bwd(proj, cb, dhc, hconv, w32, ln_g, ln_b, bl, c, name, tm=512):
    n = proj.shape[0]
    hp = (CONV_WIDTH - 1) * bl
    nt = n // tm

    def ln_bwd(d, h, lg, lb):
        d, h = d.astype(F32), h.astype(F32)
        mu = jnp.mean(h, axis=-1, keepdims=True)
        xc = h - mu
        rstd = lax.rsqrt(jnp.mean(xc * xc, axis=-1, keepdims=True) + EPS)
        xh = xc * rstd
        hn = xh * lg + lb
        s = _sig(hn)
        dhn = d * (s * (1.0 + hn * (1.0 - s)))
        dxh = dhn * lg
        dh = rstd * (dxh - jnp.mean(dxh, axis=-1, keepdims=True) - xh * jnp.mean(dxh * xh, axis=-1, keepdims=True))
        return dh, dhn, xh

    def body(ap_ref, gp_ref, a_ref, g_ref, d_ref, dn_ref, h_ref, hn_ref, w_ref, lg_ref, lb_ref,
             dcv_ref, dw_ref, dcb_ref, dlg_ref, dlb_ref, ext_h, ext_d):
        i = pl.program_id(0)
        lg, lb = lg_ref[...], lb_ref[...]
        a, g = a_ref[...].astype(F32), g_ref[...].astype(F32)
        sg = _sig(g)
        ext_h[pl.ds(hp, tm), :] = a * sg
        hgp = ap_ref[pl.ds(tm - hp, hp), :].astype(F32) * _sig(gp_ref[pl.ds(tm - hp, hp), :].astype(F32))
        ext_h[pl.ds(0, hp), :] = jnp.where(i > 0, hgp, 0.0)
        dh, dhn, xh = ln_bwd(d_ref[...], h_ref[...], lg, lb)
        ext_d[pl.ds(0, tm), :] = dh
        dh_n, _, _ = ln_bwd(dn_ref[pl.ds(0, hp), :], hn_ref[pl.ds(0, hp), :], lg, lb)
        ext_d[pl.ds(tm, hp), :] = jnp.where(i < nt - 1, dh_n, 0.0)

        @pl.when(i == 0)
        def _():
            dw_ref[...] = jnp.zeros_like(dw_ref)
            dcb_ref[...] = jnp.zeros_like(dcb_ref)
            dlg_ref[...] = jnp.zeros_like(dlg_ref)
            dlb_ref[...] = jnp.zeros_like(dlb_ref)

        dcb_ref[...] += _colsum(dh)
        dlg_ref[...] += _colsum(dhn * xh)
        dlb_ref[...] += _colsum(dhn)
        dhg = jnp.zeros((tm, c), F32)
        for j in range(CONV_WIDTH):
            dhg = dhg + w_ref[j:j + 1, :] * ext_d[pl.ds((CONV_WIDTH - 1 - j) * bl, tm), :]
            dw_ref[j:j + 1, :] += _colsum(dh * ext_h[pl.ds(j * bl, tm), :])
        dcv_ref[...] = jnp.concatenate([dhg * sg, dhg * a * sg * (1.0 - sg)], axis=1).astype(dcv_ref.dtype)

    prev = lambda i, k: (jnp.maximum(i - 1, 0), k)
    nxt = lambda i: (jnp.minimum(i + 1, nt - 1), 0)
    cur = lambda i: (i, 0)
    par = lambda arr: pl.BlockSpec(arr.shape, lambda i: (0, 0))
    acc = lambda r: pl.BlockSpec((r, c), lambda i: (0, 0))
    return pl.pallas_call(
        body, name=name, grid=(nt,),
        in_specs=[pl.BlockSpec((tm, c), functools.partial(prev, k=cb)), pl.BlockSpec((tm, c), functools.partial(prev, k=cb + 1)),
                  pl.BlockSpec((tm, c), lambda i: (i, cb)), pl.BlockSpec((tm, c), lambda i: (i, cb + 1)),
                  pl.BlockSpec((tm, c), cur), pl.BlockSpec((tm, c), nxt), pl.BlockSpec((tm, c), cur), pl.BlockSpec((tm, c), nxt),
                  par(w32), par(ln_g), par(ln_b)],
        out_specs=[pl.BlockSpec((tm, 2 * c), cur), acc(32), acc(1), acc(1), acc(1)],
        out_shape=[jax.ShapeDtypeStruct((n, 2 * c), BF16), jax.ShapeDtypeStruct((32, c), F32)] + [jax.ShapeDtypeStruct((1, c), F32)] * 3,
        scratch_shapes=[pltpu.VMEM((hp + tm, c), F32), pltpu.VMEM((hp + tm, c), F32)],
        compiler_params=_cparams(("arbitrary",)))(proj, proj, proj, proj, dhc, dhc, hconv, hconv, w32, ln_g, ln_b)


SSM_CH = 128
_GELU_C = 0.7978845608028654


def _gelu(x):
    return 0.5 * x * (1.0 + jnp.tanh(_GELU_C * (x + 0.044715 * x * x * x)))


def _gelu_grad(x):
    th = jnp.tanh(_GELU_C * (x + 0.044715 * x * x * x))
    return 0.5 * (1.0 + th) + 0.5 * x * (1.0 - th * th) * (_GELU_C * (1.0 + 3.0 * 0.044715 * x * x))


def _ssm_disc(lam_re, lam_im, log_dt, b_re, b_im):
    dt = jnp.exp(log_dt)[:, None]
    mag = jnp.exp(lam_re * dt)
    ab_re = mag * jnp.cos(lam_im * dt)
    ab_im = mag * jnp.sin(lam_im * dt)
    nr, ni = ab_re - 1.0, ab_im
    den = lam_re * lam_re + lam_im * lam_im
    z_re = ((nr * lam_re + ni * lam_im) / den)[..., None]
    z_im = ((ni * lam_re - nr * lam_im) / den)[..., None]
    return ab_re, ab_im, z_re * b_re - z_im * b_im, z_re * b_im + z_im * b_re


def _ssm_pack(ab_re, ab_im, bb_re, bb_im, c_re, c_im):
    g, p, h = bb_re.shape
    gc = SSM_CH // h
    nc = g // gc
    eye = jnp.eye(gc, dtype=F32)
    blk = lambda x: jnp.einsum("qgph,gk->qghkp", x.reshape(nc, gc, p, h), eye).reshape(nc, gc * h, gc * p)
    bbd = jnp.concatenate([blk(bb_re), blk(bb_im)], axis=2).astype(BF16)
    blc = lambda x: jnp.einsum("qghp,gk->qgpkh", x.reshape(nc, gc, h, p), eye).reshape(nc, gc * p, gc * h)
    cdm = jnp.concatenate([blc(c_re), blc(-c_im)], axis=1).astype(BF16)
    a = jnp.concatenate([ab_re.reshape(nc, gc * p), ab_im.reshape(nc, gc * p)], axis=1)
    a8 = jnp.broadcast_to(a[:, None, :], (nc, 8, 2 * gc * p)).reshape(nc * 8, 2 * gc * p)
    return bbd, cdm, a8


def _ssm_unpack(dbb, dcd, da, g, p, h):
    gc = SSM_CH // h
    nc = g // gc
    ph = gc * p
    eye = jnp.eye(gc, dtype=F32)
    dia = lambda x, o: jnp.einsum("qgpkh,gk->" + o, x.reshape(nc, gc, p, gc, h), eye).reshape((g, p, h) if o == "qgph" else (g, h, p))
    das = da.reshape(nc, 8, 2 * ph).sum(axis=1)
    return (das[:, :ph].reshape(g, p), das[:, ph:].reshape(g, p), dia(dbb[:, :ph], "qgph"), dia(dbb[:, ph:], "qgph"),
            dia(dcd[:, :ph], "qghp"), -dia(dcd[:, ph:], "qghp"))


def _ssm_fwd(proj, bbd, cdm, a8, dskip, bl, name, tm=1024):
    n = proj.shape[0]
    nc, ch, p2 = bbd.shape
    ph = p2 // 2
    nt = n // tm
    nsub = 8 // bl

    def body(u_ref, bb_ref, cd_ref, a_ref, d_ref, ypre_ref, yg_ref, s_ref, bu, carry):
        t = pl.program_id(1)

        @pl.when(t == 0)
        def _():
            carry[...] = jnp.zeros_like(carry)

        u = u_ref[...]
        bu[...] = jnp.dot(u, bb_ref[0], preferred_element_type=F32)
        a_re, a_im = a_ref[:, :ph], a_ref[:, ph:]
        row = lax.broadcasted_iota(jnp.int32, (8, ph), 0)

        def step(k, c):
            cre, cim = c
            r0 = pl.multiple_of(k * 8, 8)
            bre, bim = bu[pl.ds(r0, 8), :ph], bu[pl.ds(r0, 8), ph:]
            sre, sim = cre, cim
            for sub in range(nsub):
                xre, xim = pltpu.roll(cre, bl, 0), pltpu.roll(cim, bl, 0)
                cre = a_re * xre - a_im * xim + bre
                cim = a_re * xim + a_im * xre + bim
                if sub == 0:
                    sre, sim = cre, cim
                else:
                    sel = row >= sub * bl
                    sre, sim = jnp.where(sel, cre, sre), jnp.where(sel, cim, sim)
            bu[pl.ds(r0, 8), :ph] = sre
            bu[pl.ds(r0, 8), ph:] = sim
            return sre, sim

        cre, cim = lax.fori_loop(0, tm // 8, step, (carry[:, :ph], carry[:, ph:]))
        carry[:, :ph] = cre
        carry[:, ph:] = cim
        s16 = bu[...].astype(BF16)
        s_ref[...] = s16
        y = jnp.dot(s16, cd_ref[0], preferred_element_type=F32) + d_ref[...] * u.astype(F32)
        ypre_ref[...] = y
        yg_ref[...] = _gelu(y).astype(yg_ref.dtype)

    return pl.pallas_call(
        body, name=name, grid=(nc, nt),
        in_specs=[pl.BlockSpec((tm, ch), lambda q, t: (t, q)), pl.BlockSpec((1, ch, p2), lambda q, t: (q, 0, 0)),
                  pl.BlockSpec((1, p2, ch), lambda q, t: (q, 0, 0)), pl.BlockSpec((8, p2), lambda q, t: (q, 0)),
                  pl.BlockSpec((1, ch), lambda q, t: (0, q))],
        out_specs=[pl.BlockSpec((tm, ch), lambda q, t: (t, q)), pl.BlockSpec((tm, ch), lambda q, t: (t, q)),
                   pl.BlockSpec((tm, p2), lambda q, t: (t, q))],
        out_shape=[jax.ShapeDtypeStruct((n, nc * ch), F32), jax.ShapeDtypeStruct((n, nc * ch), BF16),
                   jax.ShapeDtypeStruct((n, nc * p2), BF16)],
        scratch_shapes=[pltpu.VMEM((tm, p2), F32), pltpu.VMEM((8, p2), F32)],
        compiler_params=_cparams(("parallel", "arbitrary")))(proj, bbd, cdm, a8, dskip)


def _ssm_bwd(dyg, ypre, proj, s_all, cdt, bbt, a8, dskip, bl, name, tm=1024):
    n = proj.shape[0]
    nc, ch, p2 = cdt.shape
    ph = p2 // 2
    nt = n // tm
    nsub = 8 // bl
    tn_dims = (((0,), (0,)), ((), ()))

    def body(dyg_ref, ypre_ref, u_ref, s_ref, cdt_ref, bbt_ref, a_ref, d_ref,
             du_ref, dbb_ref, dcd_ref, da_ref, dd_ref, ds, s32, carry):
        t = pl.program_id(1)

        @pl.when(t == 0)
        def _():
            carry[...] = jnp.zeros_like(carry)
            dbb_ref[...] = jnp.zeros_like(dbb_ref)
            dcd_ref[...] = jnp.zeros_like(dcd_ref)
            da_ref[...] = jnp.zeros_like(da_ref)
            dd_ref[...] = jnp.zeros_like(dd_ref)

        dyp = dyg_ref[...].astype(F32) * _gelu_grad(ypre_ref[...])
        u = u_ref[...]
        dd_ref[...] += _colsum(dyp * u.astype(F32))
        dyp16 = dyp.astype(BF16)
        ds[...] = jnp.dot(dyp16, cdt_ref[0], preferred_element_type=F32)
        s16 = s_ref[...]
        s32[...] = s16.astype(F32)
        a_re, a_im = a_ref[:, :ph], a_ref[:, ph:]
        row = lax.broadcasted_iota(jnp.int32, (8, ph), 0)
        back = 8 - bl

        def step(kk, c):
            lre, lim, acr, aci = c
            r0 = pl.multiple_of((tm // 8 - 1 - kk) * 8, 8)
            dre, dim = ds[pl.ds(r0, 8), :ph], ds[pl.ds(r0, 8), ph:]
            sre, sim = s32[pl.ds(r0, 8), :ph], s32[pl.ds(r0, 8), ph:]
            ore, oim, ire, iim = lre, lim, lre, lim
            for sub in range(nsub - 1, -1, -1):
                xre, xim = pltpu.roll(lre, back, 0), pltpu.roll(lim, back, 0)
                lre = a_re * xre + a_im * xim + dre
                lim = a_re * xim - a_im * xre + dim
                if sub == nsub - 1:
                    ore, oim, ire, iim = lre, lim, xre, xim
                else:
                    sel = row < (sub + 1) * bl
                    ore, oim = jnp.where(sel, lre, ore), jnp.where(sel, lim, oim)
                    ire, iim = jnp.where(sel, xre, ire), jnp.where(sel, xim, iim)
            ds[pl.ds(r0, 8), :ph] = ore
            ds[pl.ds(r0, 8), ph:] = oim
            acr = acr + sre * ire + sim * iim
            aci = aci + sre * iim - sim * ire
            return ore, oim, acr, aci

        z = jnp.zeros((8, ph), F32)
        lre, lim, acr, aci = lax.fori_loop(0, tm // 8, step, (carry[:, :ph], carry[:, ph:], z, z))
        carry[:, :ph] = lre
        carry[:, ph:] = lim
        da_ref[:, :ph] += acr
        da_ref[:, ph:] += aci
        lam16 = ds[...].astype(BF16)
        du = jnp.dot(lam16, bbt_ref[0], preferred_element_type=F32) + d_ref[...] * dyp
        du_ref[...] = du.astype(du_ref.dtype)
        dbb_ref[0] += lax.dot_general(lam16, u, tn_dims, preferred_element_type=F32)
        dcd_ref[0] += lax.dot_general(s16, dyp16, tn_dims, preferred_element_type=F32)

    rev = lambda q, t: (nt - 1 - t, q)
    return pl.pallas_call(
        body, name=name, grid=(nc, nt),
        in_specs=[pl.BlockSpec((tm, ch), rev), pl.BlockSpec((tm, ch), rev), pl.BlockSpec((tm, ch), rev),
                  pl.BlockSpec((tm, p2), rev), pl.BlockSpec((1, ch, p2), lambda q, t: (q, 0, 0)),
                  pl.BlockSpec((1, p2, ch), lambda q, t: (q, 0, 0)), pl.BlockSpec((8, p2), lambda q, t: (q, 0)),
                  pl.BlockSpec((1, ch), lambda q, t: (0, q))],
        out_specs=[pl.BlockSpec((tm, ch), rev), pl.BlockSpec((1, p2, ch), lambda q, t: (q, 0, 0)),
                   pl.BlockSpec((1, p2, ch), lambda q, t: (q, 0, 0)), pl.BlockSpec((8, p2), lambda q, t: (q, 0)),
                   pl.BlockSpec((1, ch), lambda q, t: (0, q))],
        out_shape=[jax.ShapeDtypeStruct((n, nc * ch), BF16), jax.ShapeDtypeStruct((nc, p2, ch), F32),
                   jax.ShapeDtypeStruct((nc, p2, ch), F32), jax.ShapeDtypeStruct((nc * 8, p2), F32),
                   jax.ShapeDtypeStruct((1, nc * ch), F32)],
        scratch_shapes=[pltpu.VMEM((tm, p2), F32), pltpu.VMEM((tm, p2), F32), pltpu.VMEM((8, p2), F32)],
        compiler_params=_cparams(("parallel", "arbitrary")))(dyg, ypre, proj, s_all, cdt, bbt, a8, dskip)


_MESH = pl.DeviceIdType.MESH
_HBM = pl.BlockSpec(memory_space=pltpu.HBM)


def _position():
    return lax.axis_index("x"), lax.axis_index("y"), lax.axis_index("c")


def _other_chips(x, y):
    return [((1 - x, y), 2 * (1 - x) + y), ((x, 1 - y), 2 * x + 1 - y), ((1 - x, 1 - y), 2 * (1 - x) + 1 - y)]


def _swap_sibling(v, name):
    def body(v_ref, got_ref, send_sem, recv_sem):
        x, y, c = _position()
        cp = pltpu.make_async_remote_copy(src_ref=v_ref, dst_ref=got_ref, send_sem=send_sem, recv_sem=recv_sem,
                                          device_id=(x, y, 1 - c), device_id_type=_MESH)
        cp.start()
        cp.wait()

    return pl.pallas_call(
        body, name=name, in_specs=[_HBM], out_specs=_HBM, out_shape=jax.ShapeDtypeStruct(v.shape, v.dtype),
        scratch_shapes=[pltpu.SemaphoreType.DMA, pltpu.SemaphoreType.DMA])(v)


def _chip_allgather(v, name):
    def body(v_ref, out_ref, send_sems, recv_sems, local_sem):
        x, y, c = _position()
        me = 2 * x + y
        mine = pltpu.make_async_copy(v_ref, out_ref.at[me], local_sem)
        mine.start()
        sends = []
        for k, (chip, idx) in enumerate(_other_chips(x, y)):
            cp = pltpu.make_async_remote_copy(src_ref=v_ref, dst_ref=out_ref.at[me], send_sem=send_sems.at[k],
                                              recv_sem=recv_sems.at[k], device_id=(*chip, c), device_id_type=_MESH)
            cp.start()
            sends.append(cp)
        for k, (chip, idx) in enumerate(_other_chips(x, y)):
            pltpu.make_async_remote_copy(src_ref=v_ref, dst_ref=out_ref.at[idx], send_sem=send_sems.at[k],
                                         recv_sem=recv_sems.at[k], device_id=(*chip, c), device_id_type=_MESH).wait_recv()
        for cp in sends:
            cp.wait_send()
        mine.wait()

    return pl.pallas_call(
        body, name=name, in_specs=[_HBM], out_specs=_HBM, out_shape=jax.ShapeDtypeStruct((4,) + tuple(v.shape), v.dtype),
        scratch_shapes=[pltpu.SemaphoreType.DMA((3,)), pltpu.SemaphoreType.DMA((3,)), pltpu.SemaphoreType.DMA])(v)


def _remote(src, dst, send_sems, recv_sems, s, device):
    return pltpu.make_async_remote_copy(src_ref=src, dst_ref=dst, send_sem=send_sems.at[s], recv_sem=recv_sems.at[s],
                                        device_id=device, device_id_type=_MESH)


def _comm_call(body, name, ins, out_shapes, n_sems, n_local=0):
    scratch = [pltpu.SemaphoreType.DMA((n_sems,)), pltpu.SemaphoreType.DMA((n_sems,))]
    scratch += [pltpu.SemaphoreType.DMA((n_local,))] if n_local else []
    return pl.pallas_call(body, name=name, in_specs=[_HBM] * len(ins), out_specs=[_HBM] * len(out_shapes),
                          out_shape=out_shapes, scratch_shapes=scratch)(*ins)


def _gather_layers(ws):
    n = len(ws)

    def body(*refs):
        w_refs, out_refs, (send_sems, recv_sems, local_sems) = refs[:n], refs[n:2 * n], refs[2 * n:]
        x, y, c = _position()
        me = 2 * x + y
        sibling = (x, y, 1 - c)
        others = _other_chips(x, y)
        pending = []
        for i in range(n):
            for l in range(2):
                cp = pltpu.make_async_copy(w_refs[i].at[l], out_refs[i].at[l, me], local_sems.at[2 * i + l])
                cp.start()
                pending.append(cp.wait)
            for k, (chip, idx) in enumerate(others):
                cp = _remote(w_refs[i].at[c], out_refs[i].at[c, me], send_sems, recv_sems, 6 * i + k, (*chip, c))
                cp.start()
                pending.append(cp.wait_send)
        for i in range(n):
            for k, (chip, idx) in enumerate(others):
                landed = out_refs[i].at[c, idx]
                _remote(landed, landed, send_sems, recv_sems, 6 * i + k, (*chip, c)).wait_recv()
                cp = _remote(landed, landed, send_sems, recv_sems, 6 * i + 3 + k, sibling)
                cp.start()
                pending.append(cp.wait_send)
        for i in range(n):
            for k, (chip, idx) in enumerate(others):
                theirs = out_refs[i].at[1 - c, idx]
                _remote(theirs, theirs, send_sems, recv_sems, 6 * i + 3 + k, sibling).wait_recv()
        for wait in pending:
            wait()

    shapes = [jax.ShapeDtypeStruct((w.shape[0], N_CHIPS) + tuple(w.shape[1:]), w.dtype) for w in ws]
    return _comm_call(body, "gather_weights", ws, shapes, 6 * n, 2 * n)


def _swap_layers(gs):
    n = len(gs)

    def body(*refs):
        g_refs, out_refs, (send_sems, recv_sems) = refs[:n], refs[n:2 * n], refs[2 * n:]
        x, y, c = _position()
        cps = [_remote(g_refs[i].at[1 - c], out_refs[i], send_sems, recv_sems, i, (x, y, 1 - c)) for i in range(n)]
        for cp in cps:
            cp.start()
        for cp in cps:
            cp.wait()

    return _comm_call(body, "rs_swap", gs, [jax.ShapeDtypeStruct(g.shape[1:], g.dtype) for g in gs], n)


def _scatter_chips(ps):
    n = len(ps)

    def body(*refs):
        p_refs, out_refs, (send_sems, recv_sems) = refs[:n], refs[n:2 * n], refs[2 * n:]
        x, y, c = _position()
        cps = [_remote(p_refs[i].at[idx], out_refs[i].at[k], send_sems, recv_sems, 3 * i + k, (*chip, c))
               for i in range(n) for k, (chip, idx) in enumerate(_other_chips(x, y))]
        for cp in cps:
            cp.start()
        for cp in cps:
            cp.wait()

    return _comm_call(body, "rs_scatter", ps, [jax.ShapeDtypeStruct((3,) + tuple(p.shape[1:]), p.dtype) for p in ps], 3 * n)


def _exchange_layers(rs):
    n = len(rs)

    def body(*refs):
        r_refs, out_refs, (send_sems, recv_sems, local_sems) = refs[:n], refs[n:2 * n], refs[2 * n:]
        x, y, c = _position()
        cps = []
        for i in range(n):
            cps.append(pltpu.make_async_copy(r_refs[i], out_refs[i].at[c], local_sems.at[i]))
            cps.append(_remote(r_refs[i], out_refs[i].at[c], send_sems, recv_sems, i, (x, y, 1 - c)))
        for cp in cps:
            cp.start()
        for i in range(n):
            cps[2 * i].wait()
            cps[2 * i + 1].wait_send()
            theirs = out_refs[i].at[1 - c]
            _remote(theirs, theirs, send_sems, recv_sems, i, (x, y, 1 - c)).wait_recv()

    return _comm_call(body, "rs_gather", rs, [jax.ShapeDtypeStruct((2,) + tuple(r.shape), r.dtype) for r in rs], n, n)


ATT_WINDOW = 128
PHASES = 16
_NT = (((1,), (1,)), ((), ()))
_TN = (((0,), (0,)), ((), ()))


def _to_phase_order(x, bl):
    n, c = x.shape
    g = n // bl // PHASES
    return x.reshape(g, PHASES, bl, c).transpose(2, 1, 0, 3).reshape(n, c)


def _from_phase_order(x, bl):
    n, c = x.shape
    g = n // bl // PHASES
    return x.reshape(bl, PHASES, g, c).transpose(2, 1, 0, 3).reshape(n, c)


def _att_geometry(p, n, bl):
    g = n // bl // PHASES
    if p == 0:
        return ((bl, PHASES, g), (bl, g // 16), (None, PHASES, 16),
                lambda sh: (lambda b, a: (b, 0, jnp.maximum(a + sh, 0))), 256, 16, lambda ids: ids[1] == 0)
    if p == 1:
        return ((bl, 4, 4, g), (bl, 4, g // 32), (None, 4, None, 32),
                lambda sh: (lambda b, r, a: (b, 0, r, jnp.maximum(a + sh, 0))), 128, 32, lambda ids: ids[2] == 0)
    return ((bl * PHASES, g), (bl * PHASES,), (None, g), lambda sh: (lambda s: (s, 0)), g, g, None)


def _att_masks(p, qb, chunk):
    def pos(idx):
        return (idx % chunk) * (qb // chunk) + idx // chunk

    dq = pos(lax.broadcasted_iota(jnp.int32, (qb, qb), 0))
    dk = pos(lax.broadcasted_iota(jnp.int32, (qb, qb), 1))
    dist = dq - dk
    return jnp.logical_and(dist >= 0, dist <= ATT_WINDOW), dist + qb <= ATT_WINDOW


def _att_call(p, n, bl, c, body, name, ins, outs):
    prefix, grid, blk, idx_fn, qb, chunk, _ = _att_geometry(p, n, bl)

    def spec(cb, sh):
        f = idx_fn(sh)
        return pl.BlockSpec(blk + (c,), lambda *ids, f=f, cb=cb: f(*ids) + (cb,))

    in_specs = [spec(cb, sh) for _, cb, sh in ins]
    out_specs = [spec(0, 0) for _ in outs]
    out_shape = [jax.ShapeDtypeStruct(prefix + (c,), dt) for dt in outs]
    res = pl.pallas_call(body, name=name, grid=grid, in_specs=in_specs, out_specs=out_specs, out_shape=out_shape,
                         compiler_params=_cparams(("parallel",) * len(grid)))(*[a.reshape(prefix + (a.shape[1],)) for a, _, _ in ins])
    return [r.reshape(n, c) for r in res]


def _att_fwd(p, qkv, qcb, bl, c, heads):
    n = qkv.shape[0]
    _, grid, _, _, qb, chunk, first_fn = _att_geometry(p, n, bl)
    n_grid = len(grid)
    has_prev = first_fn is not None
    e = c // heads
    scale = e ** -0.5

    def body(*refs):
        if has_prev:
            q_ref, kc_ref, kp_ref, vc_ref, vp_ref, o_ref, l_ref = refs
        else:
            q_ref, kc_ref, vc_ref, o_ref, l_ref = refs
        ids = [pl.program_id(a) for a in range(n_grid)]
        mc, mp = _att_masks(p, qb, chunk)
        if has_prev:
            mp = jnp.logical_and(mp, jnp.logical_not(first_fn(ids)))
        lo = lax.broadcasted_iota(jnp.int32, (qb, 128), 1) < e
        shp = o_ref.shape[:-1]
        ones = jnp.ones((qb, 128), BF16)
        n_t = c // 128
        load = lambda r, t: r[..., pl.ds(t * 128, 128)].reshape(qb, 128)
        items = [(t, h) for t in range(n_t) for h in range(2)]
        dot = functools.partial(jnp.dot, preferred_element_type=F32)
        q2 = [load(q_ref, t) for t in range(n_t)]
        kc = [load(kc_ref, t) for t in range(n_t)]
        qm = [jnp.where(lo if h == 0 else jnp.logical_not(lo), q2[t], jnp.zeros_like(q2[t])) for t, h in items]
        sc = [jnp.where(mc, lax.dot_general(qm[i], kc[t], _NT, preferred_element_type=F32) * scale, -jnp.inf)
              for i, (t, h) in enumerate(items)]
        m = [jnp.max(s, axis=1, keepdims=True) for s in sc]
        if has_prev:
            kp = [load(kp_ref, t) for t in range(n_t)]
            sp = [jnp.where(mp, lax.dot_general(qm[i], kp[t], _NT, preferred_element_type=F32) * scale, -jnp.inf)
                  for i, (t, h) in enumerate(items)]
            m = [jnp.maximum(a, jnp.max(s, axis=1, keepdims=True)) for a, s in zip(m, sp)]
        pc = [jnp.exp(s - a).astype(BF16) for s, a in zip(sc, m)]
        vc = [load(vc_ref, t) for t in range(n_t)]
        acc = [dot(pc[i], vc[t]) for i, (t, h) in enumerate(items)]
        den = [dot(x, ones) for x in pc]
        if has_prev:
            pp = [jnp.exp(s - a).astype(BF16) for s, a in zip(sp, m)]
            vp = [load(vp_ref, t) for t in range(n_t)]
            acc = [a + dot(pp[i], vp[t]) for i, ((t, h), a) in enumerate(zip(items, acc))]
            den = [d + dot(x, ones) for d, x in zip(den, pp)]
        oh = [a / d for a, d in zip(acc, den)]
        lh = [a + jnp.log(d) for a, d in zip(m, den)]
        for t in range(n_t):
            ls = pl.ds(t * 128, 128)
            o_ref[..., ls] = jnp.where(lo, oh[2 * t], oh[2 * t + 1]).astype(o_ref.dtype).reshape(shp + (128,))
            l_ref[..., ls] = jnp.where(lo, lh[2 * t], lh[2 * t + 1]).reshape(shp + (128,))

    kcb, vcb = 3, 4
    ins = [(qkv, qcb, 0), (qkv, kcb, 0)] + ([(qkv, kcb, -1)] if has_prev else []) + [(qkv, vcb, 0)] + ([(qkv, vcb, -1)] if has_prev else [])
    return _att_call(p, n, bl, c, body, name=f"att_fwd{p}", ins=ins, outs=[BF16, F32])


def _att_bwd(p, qkv, qcb, o, do, lse, dlse, bl, c, heads):
    n = qkv.shape[0]
    _, grid, _, _, qb, chunk, first_fn = _att_geometry(p, n, bl)
    n_grid = len(grid)
    has_prev = first_fn is not None
    e = c // heads
    scale = e ** -0.5

    def body(*refs):
        if has_prev:
            q_ref, kc_ref, kp_ref, vc_ref, vp_ref, o_ref, do_ref, l_ref, dl_ref, dq_ref, dkc_ref, dkp_ref, dvc_ref, dvp_ref = refs
        else:
            q_ref, kc_ref, vc_ref, o_ref, do_ref, l_ref, dl_ref, dq_ref, dkc_ref, dvc_ref = refs
        ids = [pl.program_id(a) for a in range(n_grid)]
        mc, mp = _att_masks(p, qb, chunk)
        if has_prev:
            mp = jnp.logical_and(mp, jnp.logical_not(first_fn(ids)))
        lo = lax.broadcasted_iota(jnp.int32, (qb, 128), 1) < e
        shp = o_ref.shape[:-1]
        n_t = c // 128
        load = lambda r, t: r[..., pl.ds(t * 128, 128)].reshape(qb, 128)
        items = [(t, h) for t in range(n_t) for h in range(2)]
        nt_dot = lambda a, b: lax.dot_general(a, b, _NT, preferred_element_type=F32)
        tn_dot = lambda a, b: lax.dot_general(a, b, _TN, preferred_element_type=F32)
        dot = functools.partial(jnp.dot, preferred_element_type=F32)

        def store(r, t, v):
            r[..., pl.ds(t * 128, 128)] = v.astype(r.dtype).reshape(shp + (128,))

        sel = [lo if h == 0 else jnp.logical_not(lo) for t, h in items]
        q2, kc, vc, do2 = ([load(r, t) for t in range(n_t)] for r in (q_ref, kc_ref, vc_ref, do_ref))
        qm = [jnp.where(sel[i], q2[t], jnp.zeros_like(q2[t])) for i, (t, h) in enumerate(items)]
        dom = [jnp.where(sel[i], do2[t], jnp.zeros_like(do2[t])) for i, (t, h) in enumerate(items)]
        dod = [do2[t].astype(F32) * load(o_ref, t).astype(F32) for t in range(n_t)]
        lcol = [load(l_ref, t)[:, h * e:h * e + 1] for t, h in items]
        corr = [load(dl_ref, t)[:, h * e:h * e + 1] - jnp.sum(jnp.where(sel[i], dod[t], 0.0), axis=1, keepdims=True)
                for i, (t, h) in enumerate(items)]
        pc = [jnp.exp(jnp.where(mc, nt_dot(qm[i], kc[t]) * scale, -jnp.inf) - lcol[i]) for i, (t, h) in enumerate(items)]
        dsc = [(pc[i] * (nt_dot(dom[i], vc[t]) + corr[i]) * scale).astype(BF16) for i, (t, h) in enumerate(items)]
        pc = [x.astype(BF16) for x in pc]
        dq = [dot(dsc[i], kc[t]) for i, (t, h) in enumerate(items)]
        dkc = [tn_dot(dsc[2 * t], qm[2 * t]) + tn_dot(dsc[2 * t + 1], qm[2 * t + 1]) for t in range(n_t)]
        dvc = [tn_dot(pc[2 * t], dom[2 * t]) + tn_dot(pc[2 * t + 1], dom[2 * t + 1]) for t in range(n_t)]
        if has_prev:
            kp, vp = ([load(r, t) for t in range(n_t)] for r in (kp_ref, vp_ref))
            pp = [jnp.exp(jnp.where(mp, nt_dot(qm[i], kp[t]) * scale, -jnp.inf) - lcol[i]) for i, (t, h) in enumerate(items)]
            dsp = [(pp[i] * (nt_dot(dom[i], vp[t]) + corr[i]) * scale).astype(BF16) for i, (t, h) in enumerate(items)]
            pp = [x.astype(BF16) for x in pp]
            dq = [a + dot(dsp[i], kp[t]) for i, ((t, h), a) in enumerate(zip(items, dq))]
            dkp = [tn_dot(dsp[2 * t], qm[2 * t]) + tn_dot(dsp[2 * t + 1], qm[2 * t + 1]) for t in range(n_t)]
            dvp = [tn_dot(pp[2 * t], dom[2 * t]) + tn_dot(pp[2 * t + 1], dom[2 * t + 1]) for t in range(n_t)]
        for t in range(n_t):
            store(dq_ref, t, jnp.where(lo, dq[2 * t], dq[2 * t + 1]))
            store(dkc_ref, t, dkc[t])
            store(dvc_ref, t, dvc[t])
            if has_prev:
                store(dkp_ref, t, dkp[t])
                store(dvp_ref, t, dvp[t])

    kcb, vcb = 3, 4
    ins = [(qkv, qcb, 0), (qkv, kcb, 0)] + ([(qkv, kcb, -1)] if has_prev else []) + [(qkv, vcb, 0)] + ([(qkv, vcb, -1)] if has_prev else [])
    ins += [(o, 0, 0), (do, 0, 0), (lse, 0, 0), (dlse, 0, 0)]
    res = _att_call(p, n, bl, c, body, name=f"att_bwd{p}", ins=ins, outs=[BF16] * (5 if has_prev else 3))
    if has_prev:
        dq, dkc, dkp, dvc, dvp = res
        return dq, dkc, dkp, dvc, dvp
    dq, dkc, dvc = res
    return dq, dkc, None, dvc, None


def _att_fold_prev(p, cur, prv, bl):
    if prv is None:
        return cur.astype(F32)
    n, c = cur.shape
    prefix, _, _, _, _, chunk, _ = _att_geometry(p, n, bl)
    v = prv.reshape(prefix + (c,)).astype(F32)
    shifted = jnp.concatenate([v[..., chunk:, :], jnp.zeros_like(v[..., :chunk, :])], axis=-2)
    return cur.astype(F32) + shifted.reshape(n, c)


def _attention_fwd2(qkv_tb, bl, heads):
    n, w = qkv_tb.shape
    c = w // 5
    qkv = _to_phase_order(qkv_tb, bl)
    outs = [_att_fwd(p, qkv, p, bl, c, heads) for p in range(3)]
    ins = [("row", o, c, 0) for o, _ in outs] + [("row", l, c, 0) for _, l in outs]
    o, = _rowwise(_combine_fwd_fn, "comb_fwd", n, ins, [(c, BF16)])
    return _from_phase_order(o, bl), (qkv, outs)


def _attention_bwd2(do_tb, saved, bl, heads):
    qkv, outs = saved
    n, c = do_tb.shape
    e = c // heads
    lane = jnp.arange(c) // e
    jmat = (lane[:, None] == lane[None, :]).astype(F32)
    do = _to_phase_order(do_tb, bl)
    ins = [("row", o, c, 0) for o, _ in outs] + [("row", l, c, 0) for _, l in outs] + [("row", do, c, 0), ("par", jmat)]
    res = _rowwise(_combine_bwd_fn, "comb_bwd", n, ins, [(c, BF16)] * 3 + [(c, F32)] * 3)
    dqs, dk, dv = [], 0.0, 0.0
    for p in range(3):
        dq, dkc, dkp, dvc, dvp = _att_bwd(p, qkv, p, outs[p][0], res[p], outs[p][1], res[3 + p], bl, c, heads)
        dqs.append(dq)
        dk = dk + _att_fold_prev(p, dkc, dkp, bl)
        dv = dv + _att_fold_prev(p, dvc, dvp, bl)
    dqkv = jnp.concatenate(dqs + [dk.astype(BF16), dv.astype(BF16)], axis=1)
    return _from_phase_order(dqkv, bl)


ATT_HEADS = 8
SSM_GROUPS, SSM_STATE, SSM_GROUP = 32, 64, 16


def _row(v):
    return v.reshape(1, -1)


def _layer_fwd(x, w, p, bl):
    n, d = x.shape
    c = d // 2
    h, = _rowwise(_rms_fwd_fn, "rms_fwd", n, [("row", x, d, 0), ("par", _row(p["norm1_g"]))], [(d, BF16)])
    proj = _mm(h, w["w_in"], "nn", BF16, "mm_in")
    disc, disc_vjp = jax.vjp(_ssm_disc, p["ssm_lambda_re"], p["ssm_lambda_im"], p["ssm_log_dt"], p["ssm_b_re"], p["ssm_b_im"])
    bbd, cdm, a8 = _ssm_pack(*disc, p["ssm_c_re"], p["ssm_c_im"])
    ypre, yg, s_all = _ssm_fwd(proj, bbd, cdm, a8, _row(p["ssm_d"]), bl, "ssm_fwd")
    zs = _mm(yg, w["w_ssm_glu"], "nn", BF16, "mm_glu")
    o, att = _attention_fwd2(proj[:, c:6 * c], bl, ATT_HEADS)
    ya = _mm(o, w["w_att_up"], "nn", BF16, "mm_att")
    w32 = jnp.concatenate([p["conv_w"], jnp.zeros((1, c), F32)], axis=0)
    hc, hconv = _conv_fwd(proj, 6, w32, _row(p["conv_b"]), _row(p["conv_ln_g"]), _row(p["conv_ln_b"]), bl, c, "conv_fwd")
    yc = _mm(hc, w["w_conv_pw2"], "nn", BF16, "mm_pw2")
    gates = [("row", proj, d, 4), ("row", proj, d, 5), ("row", proj, d, 6), ("par", _row(p["b_gate"]))]
    branches = [("row", zs, 2 * d, 0), ("row", ya, d, 0), ("row", yc, d, 0)]
    merged, = _rowwise(_merge_fwd_fn, "merge_fwd", n, gates + branches, [(d, BF16)])
    xm = _mm(merged, w["w_out"], "nn", F32, "mm_out", res=x)
    h2, = _rowwise(_rms_fwd_fn, "rms_fwd", n, [("row", xm, d, 0), ("par", _row(p["norm2_g"]))], [(d, BF16)])
    z = _mm(h2, w["w_ffn_in"], "nn", BF16, "mm_ffn_in")
    f = z.shape[1] // 2
    a, = _rowwise(_swiglu_fwd_fn, "swiglu_fwd", n, [("row", z, 2 * f, 0)], [(f, BF16)], tm=256)
    xo = _mm(a, w["w_ffn_out"], "nn", F32, "mm_ffn_out", res=xm)
    saved = dict(x=x, h=h, proj=proj, disc_vjp=disc_vjp, bbd=bbd, cdm=cdm, a8=a8, ypre=ypre, yg=yg, s_all=s_all, zs=zs, o=o,
                 att=att, ya=ya, w32=w32, hc=hc, hconv=hconv, yc=yc, gates=gates, branches=branches, merged=merged, xm=xm,
                 h2=h2, z=z, a=a)
    return xo, saved


def _layer_bwd(dxo, s, w, p, bl, layer, depth, bufs):
    n, d = dxo.shape
    c = d // 2
    g = {}
    f = s["a"].shape[1]

    def dw(key, a, dy, name):
        bufs[key] = _mm_dw(a, dy, name, 1 if key in ROW_SHARDED else N_CHIPS, layer, depth, bufs.get(key))

    da = _mm(dxo, w["w_ffn_out"], "nt", BF16, "mm_ffn_out_dx")
    dw("w_ffn_out", s["a"], dxo, "mm_ffn_out_dw")
    dz, = _rowwise(_swiglu_bwd_fn, "swiglu_bwd", n, [("row", s["z"], 2 * f, 0), ("row", da, f, 0)], [(2 * f, BF16)], tm=256)
    dh2 = _mm(dz, w["w_ffn_in"], "nt", F32, "mm_ffn_in_dx")
    dw("w_ffn_in", s["h2"], dz, "mm_ffn_in_dw")
    dxm, dg2 = _rowwise(_rms_bwd_fn, "rms_bwd", n, [("row", s["xm"], d, 0), ("par", _row(p["norm2_g"])), ("row", dh2, d, 0),
                                                   ("row", dxo, d, 0)], [(d, F32)], [d])
    g["norm2_g"] = dg2[0]
    dmerged = _mm(dxm, w["w_out"], "nt", BF16, "mm_out_dx")
    dw("w_out", s["merged"], dxm, "mm_out_dw")
    dgl, dzs, dya, dyc, dbg = _rowwise(_merge_bwd_fn, "merge_bwd", n, s["gates"] + s["branches"] + [("row", dmerged, d, 0)],
                                       [(3 * d, BF16), (2 * d, BF16), (d, BF16), (d, BF16)], [3 * d], tm=256)
    g["b_gate"] = dbg[0]
    dyg = _mm(dzs, w["w_ssm_glu"], "nt", BF16, "mm_glu_dx")
    dw("w_ssm_glu", s["yg"], dzs, "mm_glu_dw")
    du, dbb, dcd, dab, dd = _ssm_bwd(dyg, s["ypre"], s["proj"], s["s_all"], s["cdm"].transpose(0, 2, 1), s["bbd"].transpose(0, 2, 1),
                                     s["a8"], _row(p["ssm_d"]), bl, "ssm_bwd")
    dab_re, dab_im, dbb_re, dbb_im, g["ssm_c_re"], g["ssm_c_im"] = _ssm_unpack(dbb, dcd, dab, SSM_GROUPS, SSM_STATE, SSM_GROUP)
    (g["ssm_lambda_re"], g["ssm_lambda_im"], g["ssm_log_dt"], g["ssm_b_re"],
     g["ssm_b_im"]) = s["disc_vjp"]((dab_re, dab_im, dbb_re, dbb_im))
    g["ssm_d"] = dd[0]
    do = _mm(dya, w["w_att_up"], "nt", BF16, "mm_att_dx")
    dw("w_att_up", s["o"], dya, "mm_att_dw")
    dqkv = _attention_bwd2(do, s["att"], bl, ATT_HEADS)
    dhc = _mm(dyc, w["w_conv_pw2"], "nt", BF16, "mm_pw2_dx")
    dw("w_conv_pw2", s["hc"], dyc, "mm_pw2_dw")
    dcv, dcw, dcb, dlg, dlb = _conv_bwd(s["proj"], 6, dhc, s["hconv"], s["w32"], _row(p["conv_ln_g"]), _row(p["conv_ln_b"]), bl, c, "conv_bwd")
    g["conv_w"], g["conv_b"], g["conv_ln_g"], g["conv_ln_b"] = dcw, dcb[0], dlg[0], dlb[0]
    dproj = jnp.concatenate([du, dqkv, dcv, dgl], axis=1)
    dh = _mm(dproj, w["w_in"], "nt", F32, "mm_in_dx")
    dw("w_in", s["h"], dproj, "mm_in_dw")
    dx, dg1 = _rowwise(_rms_bwd_fn, "rms_bwd", n, [("row", s["x"], d, 0), ("par", _row(p["norm1_g"])), ("row", dh, d, 0),
                                                  ("row", dxm, d, 0)], [(d, F32)], [d])
    g["norm1_g"] = dg1[0]
    return dx, g, bufs


WEIGHTS = ['norm1_g', 'w_in', 'b_gate', 'ssm_lambda_re', 'ssm_lambda_im', 'ssm_log_dt', 'ssm_b_re', 'ssm_b_im', 'ssm_c_re',
           'ssm_c_im', 'ssm_d', 'w_ssm_glu', 'w_att_up', 'conv_w', 'conv_b', 'conv_ln_g', 'conv_ln_b', 'w_conv_pw2', 'w_out',
           'norm2_g', 'w_ffn_in', 'w_ffn_out', 'final_g']
BIG = ['w_in', 'w_ssm_glu', 'w_att_up', 'w_conv_pw2', 'w_out', 'w_ffn_in', 'w_ffn_out']
ROW_SHARDED = ('w_out', 'w_ffn_out')
SMALL = [k for k in WEIGHTS if k not in BIG]
LANES = 1024
N_CHIPS = 4
ROW_TILE_BYTES = 36 * 1024 * 1024
MIN_SHARD_TILE = 1024


def _pad_rows(a, rows):
    return jnp.concatenate([a, jnp.zeros((rows - a.shape[0],) + a.shape[1:], a.dtype)], axis=0) if rows > a.shape[0] else a


def _row_tile(rows, width, n_arrays):
    best = 16
    for t in range(16, rows + 1, 16):
        if rows % t == 0 and t * width * 4 * n_arrays * 2 <= ROW_TILE_BYTES:
            best = t
    return best


def _flat_fn(fn, name, ins, n_out, rows):
    return _rowwise(fn, name, rows, [("row", a, LANES, 0) for a in ins], [(LANES, F32)] * n_out, tm=rows)


def _reduce_big(bufs):
    landed = _swap_layers([b16 for _, b16 in bufs])
    p32s, p16s = [], []
    for (b32, _), la in zip(bufs, landed):
        depth, s, m, cs = b32.shape
        rows = s * m
        tm = _row_tile(rows // N_CHIPS, cs, 4)
        p32, p16 = _rowwise(_add_cast_fn, "rs_add", rows, [("rowoff", b32.reshape(depth * rows, cs), cs, 0, _core_index, rows),
                                                          ("row", la.reshape(rows, cs), cs, 0)], [(cs, F32), (cs, BF16)], tm=tm)
        p32s.append(p32)
        p16s.append(p16.reshape(N_CHIPS, rows // N_CHIPS, cs))
    reduced = []
    for p32, lb in zip(p32s, _scatter_chips(p16s)):
        _, rows, cs = lb.shape
        lb2 = lb.reshape(3 * rows, cs)
        mine = ("rowoff", p32, cs, 0, _chip_index, rows)
        red, = _rowwise(_sum4_fn, "rs_sum", rows, [mine] + [("rowblk", lb2, cs, 0, j * rows) for j in range(3)], [(cs, F32)],
                        tm=_row_tile(rows, cs, 5))
        reduced.append(red)
    return _exchange_layers(reduced)


def _add_cast_fn(a, b):
    s = a + b.astype(F32)
    return s, s


def _sum4_fn(a, b, c, d):
    return (((a.astype(F32) + b.astype(F32)) + c.astype(F32)) + d.astype(F32),)


def _add2_fn(a, b):
    return (a + b,)


def kernel(x, norm1_g, w_in, b_gate, ssm_lambda_re, ssm_lambda_im, ssm_log_dt, ssm_b_re, ssm_b_im, ssm_c_re, ssm_c_im, ssm_d, w_ssm_glu, w_att_up, conv_w, conv_b, conv_ln_g, conv_ln_b, w_conv_pw2, w_out, norm2_g, w_ffn_in, w_ffn_out, final_g, loss_target, m_norm1_g, m_w_in, m_b_gate, m_ssm_lambda_re, m_ssm_lambda_im, m_ssm_log_dt, m_ssm_b_re, m_ssm_b_im, m_ssm_c_re, m_ssm_c_im, m_ssm_d, m_w_ssm_glu, m_w_att_up, m_conv_w, m_conv_b, m_conv_ln_g, m_conv_ln_b, m_w_conv_pw2, m_w_out, m_norm2_g, m_w_ffn_in, m_w_ffn_out, m_final_g, v_norm1_g, v_w_in, v_b_gate, v_ssm_lambda_re, v_ssm_lambda_im, v_ssm_log_dt, v_ssm_b_re, v_ssm_b_im, v_ssm_c_re, v_ssm_c_im, v_ssm_d, v_w_ssm_glu, v_w_att_up, v_conv_w, v_conv_b, v_conv_ln_g, v_conv_ln_b, v_w_conv_pw2, v_w_out, v_norm2_g, v_w_ffn_in, v_w_ffn_out, v_final_g):
    args = dict(locals())
    wts = {k: args[k] for k in WEIGHTS}
    mom = {k: args["m_" + k] for k in WEIGHTS}
    var = {k: args["v_" + k] for k in WEIGHTS}
    bl, seq, d = x.shape
    n = bl * seq
    depth = norm1_g.shape[0]
    cx, cy, cc = _position()
    me = 2 * cx + cy

    gathered = _gather_layers([wts[k].astype(BF16) for k in BIG] + [conv_w])
    full = {}
    for k, a in zip(BIG, gathered):
        _, _, ks, cs = a.shape
        if k in ROW_SHARDED:
            full[k] = a.reshape(depth, N_CHIPS * ks, cs)
        elif cs < MIN_SHARD_TILE:
            full[k] = a.transpose(0, 2, 1, 3).reshape(depth, ks, N_CHIPS * cs)
        else:
            full[k] = a
    conv_full = gathered[-1].transpose(0, 2, 1, 3).reshape(depth, CONV_WIDTH, -1)

    def layer_params(l):
        p = {k: wts[k][l] for k in SMALL if k not in ("final_g", "conv_w")}
        p["conv_w"] = conv_full[l]
        return {k: full[k][l] for k in BIG}, p

    to_rows = lambda t: t.transpose(1, 0, 2).reshape(n, d)
    xs = to_rows(x)
    saved = []
    for l in range(depth):
        xs, s = _layer_fwd(xs, *layer_params(l), bl)
        saved.append(s)
    dx, sq, dgf = _rowwise(_loss_fn, "loss_head", n, [("row", xs, d, 0), ("par", _row(final_g)), ("row", to_rows(loss_target), d, 0)],
                           [(d, F32)], [d, d])
    loss = lax.psum(0.5 * jnp.sum(sq) / d, ("x", "y", "c"))

    grads = {"final_g": dgf[0]}
    per_layer, bufs = [], {}
    for l in reversed(range(depth)):
        dx, g, bufs = _layer_bwd(dx, saved[l], *layer_params(l), bl, l, depth, bufs)
        per_layer.append(g)
    per_layer.reverse()
    for k in SMALL:
        if k != "final_g":
            grads[k] = jnp.stack([g[k] for g in per_layer])
    grad_x = dx.reshape(seq, bl, d).transpose(1, 0, 2)
    outs = {}

    for k, gk in zip(BIG, _reduce_big([bufs[k] for k in BIG])):
        shp = wts[k].shape
        rows, cs = shp[0] * shp[1], shp[2]
        flat = lambda t: t.reshape(rows, cs)
        res = _rowwise(_adamw_fn, "adamw", rows, [("row", flat(t), cs, 0) for t in (wts[k], gk, mom[k], var[k])], [(cs, F32)] * 3,
                       tm=_row_tile(rows, cs, 7))
        for tag, a in zip(("grad", "delta", "m", "v"), (gk,) + tuple(res)):
            outs[tag, k] = a.reshape(shp)

    def flat1(t):
        v = jnp.concatenate([t[k].reshape(-1) for k in SMALL])
        rows = -(-v.size // (8 * LANES)) * 8
        return _pad_rows(v, rows * LANES).reshape(rows, LANES), rows

    def unflat1(flat, shapes):
        out, off, v = {}, 0, flat.reshape(-1)
        for k in SMALL:
            size = math.prod(shapes[k])
            out[k] = v[off:off + size].reshape(shapes[k])
            off += size
        return out

    grads["conv_w"] = grads["conv_w"][:, :CONV_WIDTH]
    gs, rows = flat1(grads)
    chip_sum, = _flat_fn(_add2_fn, "ar_add", [gs, _swap_sibling(gs, "ar_swap")], 1, rows)
    slots = _chip_allgather(chip_sum, "ar_gather")
    gs_red, = _flat_fn(_sum4_fn, "ar_sum", [slots[j] for j in range(N_CHIPS)], 1, rows)
    g_sm = unflat1(gs_red, {k: grads[k].shape for k in SMALL})
    cs = conv_w.shape[2]
    g_sm["conv_w"] = lax.dynamic_slice_in_dim(g_sm["conv_w"], me * cs, cs, axis=2)
    (w1, rows), (g1, _), (m1, _), (v1, _) = flat1(wts), flat1(g_sm), flat1(mom), flat1(var)
    sm_out = _flat_fn(_adamw_fn, "adamw_small", [w1, g1, m1, v1], 3, rows)
    shapes = {k: wts[k].shape for k in SMALL}
    for tag, a in zip(("delta", "m", "v"), sm_out):
        for k, t in unflat1(a, shapes).items():
            outs[tag, k] = t
    for k in SMALL:
        outs["grad", k] = g_sm[k]
    return (loss, grad_x, *[outs["grad", k] for k in WEIGHTS], *[outs["delta", k] for k in WEIGHTS],
            *[outs["m", k] for k in WEIGHTS], *[outs["v", k] for k in WEIGHTS])
```

```python
import functools
import math

import jax
import jax.numpy as jnp
from jax import lax
from jax.experimental import pallas as pl
from jax.experimental.pallas import tpu as pltpu

F32 = jnp.float32
BF16 = jnp.bfloat16
VMEM_LIMIT = 56 * 1024 * 1024


def _cparams(sem):
    return pltpu.CompilerParams(dimension_semantics=sem, vmem_limit_bytes=VMEM_LIMIT)


_DIMS = {"nn": (((1,), (0,)), ((), ())), "nt": (((1,), (1,)), ((), ())), "tn": (((0,), (0,)), ((), ()))}


MM_ROWS = 1024
MM_DW_INPUT_ELEMS = 2 * 1024 * 1024


def _div_tile(n, cap):
    best = None
    for t in range(128, min(n, cap) + 1, 128):
        if n % t == 0:
            best = t
    return best or n


def _mm(a, b, form, out_dtype, name, res=None, comm=None):
    sharded = b.ndim == 3
    kdim, cs = b.shape[-2], b.shape[-1]
    s = b.shape[0] if sharded else 1
    m = a.shape[0]
    tm = MM_ROWS if m % MM_ROWS == 0 else _div_tile(m, MM_ROWS)
    if form == "nn":
        n, kd = s * cs, kdim
        tn, tk = _div_tile(cs, 1792), _div_tile(kdim, 2048)
        per = cs // tn
        b_blk = (tk, tn)
        b_idx = (lambda i, j, k: (j // per, k, j % per)) if sharded else (lambda i, j, k: (k, j))
    else:
        n, kd = kdim, s * cs
        tn, tk = _div_tile(kdim, 1408), _div_tile(cs, 1792)
        per = cs // tk
        b_blk = (tn, tk)
        b_idx = (lambda i, j, k: (k // per, j, k % per)) if sharded else (lambda i, j, k: (j, k))
    nk = kd // tk
    a_spec = pl.BlockSpec((tm, tk), lambda i, j, k: (i, k))
    b_spec = pl.BlockSpec(((None,) + b_blk) if sharded else b_blk, b_idx)
    o_spec = pl.BlockSpec((tm, tn), lambda i, j, k: (i, j))
    dims = _DIMS[form]

    def body(*refs):
        a_ref, b_ref = refs[:2]
        r_ref = refs[2] if res is not None else None
        o_ref = refs[3] if res is not None else refs[2]
        p = lax.dot_general(a_ref[...].astype(BF16), b_ref[...], dims, preferred_element_type=F32)

        def finish(r):
            if r_ref is not None:
                r = r + r_ref[...]
            o_ref[...] = r.astype(out_dtype)

        if nk == 1:
            finish(p)
            return
        acc = refs[-1]
        k = pl.program_id(2)

        @pl.when(k == 0)
        def _():
            acc[...] = p

        @pl.when(k > 0)
        def _():
            acc[...] += p

        @pl.when(k == nk - 1)
        def _():
            finish(acc[...])

    ins = [a, b] + ([] if res is None else [res])
    in_specs = [a_spec, b_spec] + ([] if res is None else [o_spec])
    out = _pcall(body, name, (m // tm, n // tn, nk), in_specs, [o_spec], [jax.ShapeDtypeStruct((m, n), out_dtype)],
                 [pltpu.VMEM((tm, tn), F32)] if nk > 1 else [], ("parallel", "parallel", "arbitrary"), ins, comm)
    return out[0] if comm is None else (out[0][0], out[1])


def _mm_dw(a, dy, name, shards):
    r, m = a.shape
    c = dy.shape[1]
    cs = c // shards
    tm, tn = _div_tile(m, 1408), _div_tile(cs, 1792)
    tk = _div_tile(r, max(512, min(2048, MM_DW_INPUT_ELEMS // (tm + tn))))
    per = cs // tn
    nk = r // tk

    def body(a_ref, b_ref, o32, o16, acc):
        k = pl.program_id(2)
        p = lax.dot_general(a_ref[...].astype(BF16), b_ref[...].astype(BF16), _DIMS["tn"], preferred_element_type=F32)

        @pl.when(k == 0)
        def _():
            acc[...] = p

        @pl.when(k > 0)
        def _():
            acc[...] += p

        @pl.when(k == nk - 1)
        def _():
            o32[...] = acc[...]
            o16[...] = acc[...].astype(BF16)

    o_spec = pl.BlockSpec((None, tm, tn), lambda i, j, k: (j // per, i, j % per))
    shape = (shards, m, cs)
    in_specs = [pl.BlockSpec((tk, tm), lambda i, j, k: (k, i)), pl.BlockSpec((tk, tn), lambda i, j, k: (k, j))]
    return _pcall(body, name, (m // tm, c // tn, nk), in_specs, [o_spec, o_spec],
                  [jax.ShapeDtypeStruct(shape, F32), jax.ShapeDtypeStruct(shape, BF16)], [pltpu.VMEM((tm, tn), F32)],
                  ("parallel", "parallel", "arbitrary"), [a, dy])


def _core_index():
    return lax.axis_index("c")


def _chip_index():
    return 2 * lax.axis_index("x") + lax.axis_index("y")


def _rowwise(fn, name, n_rows, ins, outs, accs=(), tm=512):
    n_in, n_out, n_acc = len(ins), len(outs), len(accs)
    in_specs, args = [], []
    for spec in ins:
        if spec[0] == "row":
            _, arr, w, cb = spec
            in_specs.append(pl.BlockSpec((tm, w), lambda i, cb=cb: (i, cb)))
        elif spec[0] == "rowoff":
            _, arr, w, cb, index_fn, span = spec
            in_specs.append(pl.BlockSpec((tm, w), lambda i, cb=cb, index_fn=index_fn, nb=span // tm: (index_fn() * nb + i, cb)))
        elif spec[0] == "rowblk":
            _, arr, w, cb, start = spec
            in_specs.append(pl.BlockSpec((tm, w), lambda i, cb=cb, nb=start // tm: (nb + i, cb)))
        else:
            arr = spec[1]
            in_specs.append(pl.BlockSpec(arr.shape, lambda i: (0, 0)))
        args.append(arr)
    out_specs = [pl.BlockSpec((tm, w), lambda i: (i, 0)) for w, _ in outs]
    out_specs += [pl.BlockSpec((1, w), lambda i: (0, 0)) for w in accs]
    out_shape = [jax.ShapeDtypeStruct((n_rows, w), dt) for w, dt in outs]
    out_shape += [jax.ShapeDtypeStruct((1, w), F32) for w in accs]

    def body(*refs):
        i = pl.program_id(0)
        res = fn(*[r[...] for r in refs[:n_in]])
        for o_ref, r in zip(refs[n_in:n_in + n_out], res[:n_out]):
            o_ref[...] = r.astype(o_ref.dtype)
        for a_ref, r in zip(refs[n_in + n_out:], res[n_out:]):
            @pl.when(i == 0)
            def _(a_ref=a_ref, r=r):
                a_ref[...] = r

            @pl.when(i > 0)
            def _(a_ref=a_ref, r=r):
                a_ref[...] += r

    return pl.pallas_call(
        body, name=name, grid=(n_rows // tm,), in_specs=in_specs, out_specs=out_specs, out_shape=out_shape,
        compiler_params=_cparams(("arbitrary",)))(*args)


EPS = 1e-6


def _sig(x):
    return 1.0 / (1.0 + jnp.exp(-x))


def _colsum(x):
    return jnp.sum(x, axis=0, keepdims=True)


def _rms_fwd_fn(x, g):
    r = lax.rsqrt(jnp.mean(x * x, axis=-1, keepdims=True) + EPS)
    return (x * r * g,)


def _rms_bwd_fn(x, g, dh, dres):
    dh = dh.astype(F32)
    r = lax.rsqrt(jnp.mean(x * x, axis=-1, keepdims=True) + EPS)
    xh = x * r
    dyg = dh * g
    dx = r * (dyg - xh * jnp.mean(dyg * xh, axis=-1, keepdims=True)) + dres
    return dx, _colsum(dh * xh)


def _loss_fn(x, g, t):
    d = x.shape[-1]
    r = lax.rsqrt(jnp.mean(x * x, axis=-1, keepdims=True) + EPS)
    xh = x * r
    err = xh * g - t
    dy = err * (1.0 / d)
    dyg = dy * g
    dx = r * (dyg - xh * jnp.mean(dyg * xh, axis=-1, keepdims=True))
    return dx, _colsum(err * err), _colsum(dy * xh)


def _swiglu_fwd_fn(z):
    f = z.shape[-1] // 2
    z1, z2 = z[:, :f].astype(F32), z[:, f:].astype(F32)
    return (z1 * _sig(z1) * z2,)


def _swiglu_bwd_fn(z, da):
    f = z.shape[-1] // 2
    z1, z2, da = z[:, :f].astype(F32), z[:, f:].astype(F32), da.astype(F32)
    s = _sig(z1)
    dz1 = da * z2 * (s * (1.0 + z1 * (1.0 - s)))
    dz2 = da * (z1 * s)
    return (jnp.concatenate([dz1, dz2], axis=1),)


def _merge_fwd_fn(g0, g1, g2, bg, zs, ya, yc):
    d = ya.shape[-1]
    bg = bg.astype(F32)
    zs = zs.astype(F32)
    ys = zs[:, :d] * _sig(zs[:, d:])
    m = _sig(g0.astype(F32) + bg[:, :d]) * ys
    m = m + _sig(g1.astype(F32) + bg[:, d:2 * d]) * ya.astype(F32)
    m = m + _sig(g2.astype(F32) + bg[:, 2 * d:]) * yc.astype(F32)
    return (m,)


def _merge_bwd_fn(g0, g1, g2, bg, zs, ya, yc, dm):
    d = ya.shape[-1]
    bg = bg.astype(F32)
    zs = zs.astype(F32)
    dm = dm.astype(F32)
    z1, s2 = zs[:, :d], _sig(zs[:, d:])
    ys = z1 * s2
    s0 = _sig(g0.astype(F32) + bg[:, :d])
    s1 = _sig(g1.astype(F32) + bg[:, d:2 * d])
    s3 = _sig(g2.astype(F32) + bg[:, 2 * d:])
    dgl = jnp.concatenate([dm * ys * s0 * (1.0 - s0), dm * ya.astype(F32) * s1 * (1.0 - s1),
                           dm * yc.astype(F32) * s3 * (1.0 - s3)], axis=1)
    dys = dm * s0
    dzs = jnp.concatenate([dys * s2, dys * z1 * s2 * (1.0 - s2)], axis=1)
    return dgl, dzs, dm * s1, dm * s3, _colsum(dgl)


def _combine_fwd_fn(o0, o1, o2, l0, l1, l2):
    m = jnp.maximum(jnp.maximum(l0, l1), l2)
    e0, e1, e2 = jnp.exp(l0 - m), jnp.exp(l1 - m), jnp.exp(l2 - m)
    inv = 1.0 / (e0 + e1 + e2)
    return ((e0 * o0.astype(F32) + e1 * o1.astype(F32) + e2 * o2.astype(F32)) * inv,)


def _combine_bwd_fn(o0, o1, o2, l0, l1, l2, do, jmat):
    m = jnp.maximum(jnp.maximum(l0, l1), l2)
    e0, e1, e2 = jnp.exp(l0 - m), jnp.exp(l1 - m), jnp.exp(l2 - m)
    inv = 1.0 / (e0 + e1 + e2)
    w0, w1, w2 = e0 * inv, e1 * inv, e2 * inv
    do = do.astype(F32)

    def headsum(x):
        return jnp.dot(x, jmat, preferred_element_type=F32, precision=lax.Precision.HIGHEST)

    dw0, dw1, dw2 = headsum(do * o0.astype(F32)), headsum(do * o1.astype(F32)), headsum(do * o2.astype(F32))
    mean = w0 * dw0 + w1 * dw1 + w2 * dw2
    return w0 * do, w1 * do, w2 * do, w0 * (dw0 - mean), w1 * (dw1 - mean), w2 * (dw2 - mean)


ADAM_LR, ADAM_B1, ADAM_B2, ADAM_EPS, ADAM_WD, ADAM_STEP = 0.001, 0.9, 0.999, 1e-08, 0.01, 10


def _adamw_fn(w, g, m, v):
    m = ADAM_B1 * m + (1.0 - ADAM_B1) * g
    v = ADAM_B2 * v + (1.0 - ADAM_B2) * (g * g)
    m_hat = m / (1.0 - ADAM_B1 ** ADAM_STEP)
    v_hat = v / (1.0 - ADAM_B2 ** ADAM_STEP)
    delta = -ADAM_LR * (m_hat / (jnp.sqrt(v_hat) + ADAM_EPS) + ADAM_WD * w)
    return delta, m, v


CONV_WIDTH = 31


def _conv_fwd(proj, cb, w32, conv_b, ln_g, ln_b, bl, c, name, tm=512):
    n = proj.shape[0]
    hp = (CONV_WIDTH - 1) * bl
    nt = n // tm

    def body(ap_ref, gp_ref, a_ref, g_ref, w_ref, cb_ref, lg_ref, lb_ref, hc_ref, hconv_ref, ext):
        i = pl.program_id(0)
        ext[pl.ds(hp, tm), :] = a_ref[...].astype(F32) * _sig(g_ref[...].astype(F32))
        hgp = ap_ref[pl.ds(tm - hp, hp), :].astype(F32) * _sig(gp_ref[pl.ds(tm - hp, hp), :].astype(F32))
        ext[pl.ds(0, hp), :] = jnp.where(i > 0, hgp, 0.0)
        acc = jnp.zeros((tm, c), F32) + cb_ref[...]
        for j in range(CONV_WIDTH):
            acc = acc + w_ref[j:j + 1, :] * ext[pl.ds(j * bl, tm), :]
        hconv_ref[...] = acc.astype(hconv_ref.dtype)
        h = hconv_ref[...].astype(F32)
        mu = jnp.mean(h, axis=-1, keepdims=True)
        xc = h - mu
        var = jnp.mean(xc * xc, axis=-1, keepdims=True)
        hn = xc * lax.rsqrt(var + EPS) * lg_ref[...] + lb_ref[...]
        hc_ref[...] = (hn * _sig(hn)).astype(hc_ref.dtype)

    prev = lambda i, k: (jnp.maximum(i - 1, 0), k)
    par = lambda arr: pl.BlockSpec(arr.shape, lambda i: (0, 0))
    return pl.pallas_call(
        body, name=name, grid=(nt,),
        in_specs=[pl.BlockSpec((tm, c), functools.partial(prev, k=cb)), pl.BlockSpec((tm, c), functools.partial(prev, k=cb + 1)),
                  pl.BlockSpec((tm, c), lambda i: (i, cb)), pl.BlockSpec((tm, c), lambda i: (i, cb + 1)),
                  par(w32), par(conv_b), par(ln_g), par(ln_b)],
        out_specs=[pl.BlockSpec((tm, c), lambda i: (i, 0))] * 2,
        out_shape=[jax.ShapeDtypeStruct((n, c), BF16)] * 2,
        scratch_shapes=[pltpu.VMEM((hp + tm, c), F32)],
        compiler_params=_cparams(("arbitrary",)))(proj, proj, proj, proj, w32, conv_b, ln_g, ln_b)


def _conv_bwd(proj, cb, dhc, hconv, w32, ln_g, ln_b, bl, c, name, tm=512, comm=None):
    n = proj.shape[0]
    hp = (CONV_WIDTH - 1) * bl
    nt = n // tm

    def ln_bwd(d, h, lg, lb):
        d, h = d.astype(F32), h.astype(F32)
        mu = jnp.mean(h, axis=-1, keepdims=True)
        xc = h - mu
        rstd = lax.rsqrt(jnp.mean(xc * xc, axis=-1, keepdims=True) + EPS)
        xh = xc * rstd
        hn = xh * lg + lb
        s = _sig(hn)
        dhn = d * (s * (1.0 + hn * (1.0 - s)))
        dxh = dhn * lg
        dh = rstd * (dxh - jnp.mean(dxh, axis=-1, keepdims=True) - xh * jnp.mean(dxh * xh, axis=-1, keepdims=True))
        return dh, dhn, xh

    def body(ap_ref, gp_ref, a_ref, g_ref, d_ref, dn_ref, h_ref, hn_ref, w_ref, lg_ref, lb_ref,
             dcv_ref, dw_ref, dcb_ref, dlg_ref, dlb_ref, ext_h, ext_d):
        i = pl.program_id(0)
        lg, lb = lg_ref[...], lb_ref[...]
        a, g = a_ref[...].astype(F32), g_ref[...].astype(F32)
        sg = _sig(g)
        ext_h[pl.ds(hp, tm), :] = a * sg
        hgp = ap_ref[pl.ds(tm - hp, hp), :].astype(F32) * _sig(gp_ref[pl.ds(tm - hp, hp), :].astype(F32))
        ext_h[pl.ds(0, hp), :] = jnp.where(i > 0, hgp, 0.0)
        dh, dhn, xh = ln_bwd(d_ref[...], h_ref[...], lg, lb)
        ext_d[pl.ds(0, tm), :] = dh
        dh_n, _, _ = ln_bwd(dn_ref[pl.ds(0, hp), :], hn_ref[pl.ds(0, hp), :], lg, lb)
        ext_d[pl.ds(tm, hp), :] = jnp.where(i < nt - 1, dh_n, 0.0)

        @pl.when(i == 0)
        def _():
            dw_ref[...] = jnp.zeros_like(dw_ref)
            dcb_ref[...] = jnp.zeros_like(dcb_ref)
            dlg_ref[...] = jnp.zeros_like(dlg_ref)
            dlb_ref[...] = jnp.zeros_like(dlb_ref)

        dcb_ref[...] += _colsum(dh)
        dlg_ref[...] += _colsum(dhn * xh)
        dlb_ref[...] += _colsum(dhn)
        dhg = jnp.zeros((tm, c), F32)
        for j in range(CONV_WIDTH):
            dhg = dhg + w_ref[j:j + 1, :] * ext_d[pl.ds((CONV_WIDTH - 1 - j) * bl, tm), :]
            dw_ref[j:j + 1, :] += _colsum(dh * ext_h[pl.ds(j * bl, tm), :])
        dcv_ref[...] = jnp.concatenate([dhg * sg, dhg * a * sg * (1.0 - sg)], axis=1).astype(dcv_ref.dtype)

    prev = lambda i, k: (jnp.maximum(i - 1, 0), k)
    nxt = lambda i: (jnp.minimum(i + 1, nt - 1), 0)
    cur = lambda i: (i, 0)
    par = lambda arr: pl.BlockSpec(arr.shape, lambda i: (0, 0))
    acc = lambda r: pl.BlockSpec((r, c), lambda i: (0, 0))
    in_specs = [pl.BlockSpec((tm, c), functools.partial(prev, k=cb)), pl.BlockSpec((tm, c), functools.partial(prev, k=cb + 1)),
                pl.BlockSpec((tm, c), lambda i: (i, cb)), pl.BlockSpec((tm, c), lambda i: (i, cb + 1)),
                pl.BlockSpec((tm, c), cur), pl.BlockSpec((tm, c), nxt), pl.BlockSpec((tm, c), cur), pl.BlockSpec((tm, c), nxt),
                par(w32), par(ln_g), par(ln_b)]
    out_shape = [jax.ShapeDtypeStruct((n, 2 * c), BF16), jax.ShapeDtypeStruct((32, c), F32)] + [jax.ShapeDtypeStruct((1, c), F32)] * 3
    return _pcall(body, name, (nt,), in_specs, [pl.BlockSpec((tm, 2 * c), cur), acc(32), acc(1), acc(1), acc(1)], out_shape,
                  [pltpu.VMEM((hp + tm, c), F32), pltpu.VMEM((hp + tm, c), F32)], ("arbitrary",),
                  [proj, proj, proj, proj, dhc, dhc, hconv, hconv, w32, ln_g, ln_b], comm)


SSM_CH = 128
_GELU_C = 0.7978845608028654


def _gelu(x):
    return 0.5 * x * (1.0 + jnp.tanh(_GELU_C * (x + 0.044715 * x * x * x)))


def _gelu_grad(x):
    th = jnp.tanh(_GELU_C * (x + 0.044715 * x * x * x))
    return 0.5 * (1.0 + th) + 0.5 * x * (1.0 - th * th) * (_GELU_C * (1.0 + 3.0 * 0.044715 * x * x))


def _ssm_disc(lam_re, lam_im, log_dt, b_re, b_im):
    dt = jnp.exp(log_dt)[:, None]
    mag = jnp.exp(lam_re * dt)
    ab_re = mag * jnp.cos(lam_im * dt)
    ab_im = mag * jnp.sin(lam_im * dt)
    nr, ni = ab_re - 1.0, ab_im
    den = lam_re * lam_re + lam_im * lam_im
    z_re = ((nr * lam_re + ni * lam_im) / den)[..., None]
    z_im = ((ni * lam_re - nr * lam_im) / den)[..., None]
    return ab_re, ab_im, z_re * b_re - z_im * b_im, z_re * b_im + z_im * b_re


def _ssm_pack(ab_re, ab_im, bb_re, bb_im, c_re, c_im):
    g, p, h = bb_re.shape
    gc = SSM_CH // h
    nc = g // gc
    eye = jnp.eye(gc, dtype=F32)
    blk = lambda x: jnp.einsum("qgph,gk->qghkp", x.reshape(nc, gc, p, h), eye).reshape(nc, gc * h, gc * p)
    bbd = jnp.concatenate([blk(bb_re), blk(bb_im)], axis=2).astype(BF16)
    blc = lambda x: jnp.einsum("qghp,gk->qgpkh", x.reshape(nc, gc, h, p), eye).reshape(nc, gc * p, gc * h)
    cdm = jnp.concatenate([blc(c_re), blc(-c_im)], axis=1).astype(BF16)
    a = jnp.concatenate([ab_re.reshape(nc, gc * p), ab_im.reshape(nc, gc * p)], axis=1)
    a8 = jnp.broadcast_to(a[:, None, :], (nc, 8, 2 * gc * p)).reshape(nc * 8, 2 * gc * p)
    return bbd, cdm, a8


def _ssm_unpack(dbb, dcd, da, g, p, h):
    gc = SSM_CH // h
    nc = g // gc
    ph = gc * p
    eye = jnp.eye(gc, dtype=F32)
    dia = lambda x, o: jnp.einsum("qgpkh,gk->" + o, x.reshape(nc, gc, p, gc, h), eye).reshape((g, p, h) if o == "qgph" else (g, h, p))
    das = da.reshape(nc, 8, 2 * ph).sum(axis=1)
    return (das[:, :ph].reshape(g, p), das[:, ph:].reshape(g, p), dia(dbb[:, :ph], "qgph"), dia(dbb[:, ph:], "qgph"),
            dia(dcd[:, :ph], "qghp"), -dia(dcd[:, ph:], "qghp"))


def _ssm_fwd(proj, bbd, cdm, a8, dskip, bl, name, tm=1024, comm=None):
    n = proj.shape[0]
    nc, ch, p2 = bbd.shape
    ph = p2 // 2
    nt = n // tm
    nsub = 8 // bl

    def body(u_ref, bb_ref, cd_ref, a_ref, d_ref, ypre_ref, yg_ref, s_ref, bu, carry):
        t = pl.program_id(1)

        @pl.when(t == 0)
        def _():
            carry[...] = jnp.zeros_like(carry)

        u = u_ref[...]
        bu[...] = jnp.dot(u, bb_ref[0], preferred_element_type=F32)
        a_re, a_im = a_ref[:, :ph], a_ref[:, ph:]
        row = lax.broadcasted_iota(jnp.int32, (8, ph), 0)

        def step(k, c):
            cre, cim = c
            r0 = pl.multiple_of(k * 8, 8)
            bre, bim = bu[pl.ds(r0, 8), :ph], bu[pl.ds(r0, 8), ph:]
            sre, sim = cre, cim
            for sub in range(nsub):
                xre, xim = pltpu.roll(cre, bl, 0), pltpu.roll(cim, bl, 0)
                cre = a_re * xre - a_im * xim + bre
                cim = a_re * xim + a_im * xre + bim
                if sub == 0:
                    sre, sim = cre, cim
                else:
                    sel = row >= sub * bl
                    sre, sim = jnp.where(sel, cre, sre), jnp.where(sel, cim, sim)
            bu[pl.ds(r0, 8), :ph] = sre
            bu[pl.ds(r0, 8), ph:] = sim
            return sre, sim

        cre, cim = lax.fori_loop(0, tm // 8, step, (carry[:, :ph], carry[:, ph:]))
        carry[:, :ph] = cre
        carry[:, ph:] = cim
        s16 = bu[...].astype(BF16)
        s_ref[...] = s16
        y = jnp.dot(s16, cd_ref[0], preferred_element_type=F32) + d_ref[...] * u.astype(F32)
        ypre_ref[...] = y
        yg_ref[...] = _gelu(y).astype(yg_ref.dtype)

    in_specs = [pl.BlockSpec((tm, ch), lambda q, t: (t, q)), pl.BlockSpec((1, ch, p2), lambda q, t: (q, 0, 0)),
                pl.BlockSpec((1, p2, ch), lambda q, t: (q, 0, 0)), pl.BlockSpec((8, p2), lambda q, t: (q, 0)),
                pl.BlockSpec((1, ch), lambda q, t: (0, q))]
    out_specs = [pl.BlockSpec((tm, ch), lambda q, t: (t, q)), pl.BlockSpec((tm, ch), lambda q, t: (t, q)),
                 pl.BlockSpec((tm, p2), lambda q, t: (t, q))]
    out_shape = [jax.ShapeDtypeStruct((n, nc * ch), F32), jax.ShapeDtypeStruct((n, nc * ch), BF16),
                 jax.ShapeDtypeStruct((n, nc * p2), BF16)]
    return _pcall(body, name, (nc, nt), in_specs, out_specs, out_shape, [pltpu.VMEM((tm, p2), F32), pltpu.VMEM((8, p2), F32)],
                  ("parallel", "arbitrary"), [proj, bbd, cdm, a8, dskip], comm)


def _ssm_bwd(dyg, ypre, proj, s_all, cdt, bbt, a8, dskip, bl, name, tm=1024, comm=None):
    n = proj.shape[0]
    nc, ch, p2 = cdt.shape
    ph = p2 // 2
    nt = n // tm
    nsub = 8 // bl
    tn_dims = (((0,), (0,)), ((), ()))

    def body(dyg_ref, ypre_ref, u_ref, s_ref, cdt_ref, bbt_ref, a_ref, d_ref,
             du_ref, dbb_ref, dcd_ref, da_ref, dd_ref, ds, s32, carry):
        t = pl.program_id(1)

        @pl.when(t == 0)
        def _():
            carry[...] = jnp.zeros_like(carry)
            dbb_ref[...] = jnp.zeros_like(dbb_ref)
            dcd_ref[...] = jnp.zeros_like(dcd_ref)
            da_ref[...] = jnp.zeros_like(da_ref)
            dd_ref[...] = jnp.zeros_like(dd_ref)

        dyp = dyg_ref[...].astype(F32) * _gelu_grad(ypre_ref[...])
        u = u_ref[...]
        dd_ref[...] += _colsum(dyp * u.astype(F32))
        dyp16 = dyp.astype(BF16)
        ds[...] = jnp.dot(dyp16, cdt_ref[0], preferred_element_type=F32)
        s16 = s_ref[...]
        s32[...] = s16.astype(F32)
        a_re, a_im = a_ref[:, :ph], a_ref[:, ph:]
        row = lax.broadcasted_iota(jnp.int32, (8, ph), 0)
        back = 8 - bl

        def step(kk, c):
            lre, lim, acr, aci = c
            r0 = pl.multiple_of((tm // 8 - 1 - kk) * 8, 8)
            dre, dim = ds[pl.ds(r0, 8), :ph], ds[pl.ds(r0, 8), ph:]
            sre, sim = s32[pl.ds(r0, 8), :ph], s32[pl.ds(r0, 8), ph:]
            ore, oim, ire, iim = lre, lim, lre, lim
            for sub in range(nsub - 1, -1, -1):
                xre, xim = pltpu.roll(lre, back, 0), pltpu.roll(lim, back, 0)
                lre = a_re * xre + a_im * xim + dre
                lim = a_re * xim - a_im * xre + dim
                if sub == nsub - 1:
                    ore, oim, ire, iim = lre, lim, xre, xim
                else:
                    sel = row < (sub + 1) * bl
                    ore, oim = jnp.where(sel, lre, ore), jnp.where(sel, lim, oim)
                    ire, iim = jnp.where(sel, xre, ire), jnp.where(sel, xim, iim)
            ds[pl.ds(r0, 8), :ph] = ore
            ds[pl.ds(r0, 8), ph:] = oim
            acr = acr + sre * ire + sim * iim
            aci = aci + sre * iim - sim * ire
            return ore, oim, acr, aci

        z = jnp.zeros((8, ph), F32)
        lre, lim, acr, aci = lax.fori_loop(0, tm // 8, step, (carry[:, :ph], carry[:, ph:], z, z))
        carry[:, :ph] = lre
        carry[:, ph:] = lim
        da_ref[:, :ph] += acr
        da_ref[:, ph:] += aci
        lam16 = ds[...].astype(BF16)
        du = jnp.dot(lam16, bbt_ref[0], preferred_element_type=F32) + d_ref[...] * dyp
        du_ref[...] = du.astype(du_ref.dtype)
        dbb_ref[0] += lax.dot_general(lam16, u, tn_dims, preferred_element_type=F32)
        dcd_ref[0] += lax.dot_general(s16, dyp16, tn_dims, preferred_element_type=F32)

    rev = lambda q, t: (nt - 1 - t, q)
    in_specs = [pl.BlockSpec((tm, ch), rev), pl.BlockSpec((tm, ch), rev), pl.BlockSpec((tm, ch), rev),
                pl.BlockSpec((tm, p2), rev), pl.BlockSpec((1, ch, p2), lambda q, t: (q, 0, 0)),
                pl.BlockSpec((1, p2, ch), lambda q, t: (q, 0, 0)), pl.BlockSpec((8, p2), lambda q, t: (q, 0)),
                pl.BlockSpec((1, ch), lambda q, t: (0, q))]
    out_specs = [pl.BlockSpec((tm, ch), rev), pl.BlockSpec((1, p2, ch), lambda q, t: (q, 0, 0)),
                 pl.BlockSpec((1, p2, ch), lambda q, t: (q, 0, 0)), pl.BlockSpec((8, p2), lambda q, t: (q, 0)),
                 pl.BlockSpec((1, ch), lambda q, t: (0, q))]
    out_shape = [jax.ShapeDtypeStruct((n, nc * ch), BF16), jax.ShapeDtypeStruct((nc, p2, ch), F32),
                 jax.ShapeDtypeStruct((nc, p2, ch), F32), jax.ShapeDtypeStruct((nc * 8, p2), F32),
                 jax.ShapeDtypeStruct((1, nc * ch), F32)]
    return _pcall(body, name, (nc, nt), in_specs, out_specs, out_shape,
                  [pltpu.VMEM((tm, p2), F32), pltpu.VMEM((tm, p2), F32), pltpu.VMEM((8, p2), F32)],
                  ("parallel", "arbitrary"), [dyg, ypre, proj, s_all, cdt, bbt, a8, dskip], comm)


_MESH = pl.DeviceIdType.MESH
_HBM = pl.BlockSpec(memory_space=pltpu.HBM)


def _position():
    return lax.axis_index("x"), lax.axis_index("y"), lax.axis_index("c")


def _other_chips(x, y):
    return [((1 - x, y), 2 * (1 - x) + y), ((x, 1 - y), 2 * x + 1 - y), ((1 - x, 1 - y), 2 * (1 - x) + 1 - y)]


def _swap_sibling(v, name):
    def body(v_ref, got_ref, send_sem, recv_sem):
        x, y, c = _position()
        cp = pltpu.make_async_remote_copy(src_ref=v_ref, dst_ref=got_ref, send_sem=send_sem, recv_sem=recv_sem,
                                          device_id=(x, y, 1 - c), device_id_type=_MESH)
        cp.start()
        cp.wait()

    return pl.pallas_call(
        body, name=name, in_specs=[_HBM], out_specs=_HBM, out_shape=jax.ShapeDtypeStruct(v.shape, v.dtype),
        scratch_shapes=[pltpu.SemaphoreType.DMA, pltpu.SemaphoreType.DMA])(v)


def _chip_allgather(v, name):
    def body(v_ref, out_ref, send_sems, recv_sems, local_sem):
        x, y, c = _position()
        me = 2 * x + y
        mine = pltpu.make_async_copy(v_ref, out_ref.at[me], local_sem)
        mine.start()
        sends = []
        for k, (chip, idx) in enumerate(_other_chips(x, y)):
            cp = pltpu.make_async_remote_copy(src_ref=v_ref, dst_ref=out_ref.at[me], send_sem=send_sems.at[k],
                                              recv_sem=recv_sems.at[k], device_id=(*chip, c), device_id_type=_MESH)
            cp.start()
            sends.append(cp)
        for k, (chip, idx) in enumerate(_other_chips(x, y)):
            pltpu.make_async_remote_copy(src_ref=v_ref, dst_ref=out_ref.at[idx], send_sem=send_sems.at[k],
                                         recv_sem=recv_sems.at[k], device_id=(*chip, c), device_id_type=_MESH).wait_recv()
        for cp in sends:
            cp.wait_send()
        mine.wait()

    return pl.pallas_call(
        body, name=name, in_specs=[_HBM], out_specs=_HBM, out_shape=jax.ShapeDtypeStruct((4,) + tuple(v.shape), v.dtype),
        scratch_shapes=[pltpu.SemaphoreType.DMA((3,)), pltpu.SemaphoreType.DMA((3,)), pltpu.SemaphoreType.DMA])(v)


def _remote(src, dst, send_sems, recv_sems, s, device):
    return pltpu.make_async_remote_copy(src_ref=src, dst_ref=dst, send_sem=send_sems.at[s], recv_sem=recv_sems.at[s],
                                        device_id=device, device_id_type=_MESH)


class _Exchange:
    def __init__(self, ins, out_shapes, n_sems, n_local=0):
        self.ins, self.out_shapes, self.n_sems, self.n_local = list(ins), list(out_shapes), n_sems, n_local

    def sem_shapes(self):
        return [pltpu.SemaphoreType.DMA((self.n_sems,)), pltpu.SemaphoreType.DMA((self.n_sems,)),
                pltpu.SemaphoreType.DMA((max(self.n_local, 1),))]


def _halves(ref, c, axis=0):
    h = ref.shape[axis] // 2
    idx = (slice(None),) * axis
    return ref.at[idx + (pl.ds(c * h, h),)], ref.at[idx + (pl.ds((1 - c) * h, h),)]


class _GatherShards(_Exchange):
    def __init__(self, ws):
        super().__init__(ws, [jax.ShapeDtypeStruct((N_CHIPS,) + tuple(w.shape), w.dtype) for w in ws], 6 * len(ws), len(ws))

    def start(self, w_refs, out_refs, sems):
        send_sems, recv_sems, local_sems = sems
        x, y, c = _position()
        me = 2 * x + y
        for i, (w, out) in enumerate(zip(w_refs, out_refs)):
            pltpu.make_async_copy(w, out.at[me], local_sems.at[i]).start()
            for k, (chip, idx) in enumerate(_other_chips(x, y)):
                _remote(_halves(w, c)[0], _halves(out.at[me], c)[0], send_sems, recv_sems, 6 * i + k, (*chip, c)).start()

    def finish(self, w_refs, out_refs, sems):
        send_sems, recv_sems, local_sems = sems
        x, y, c = _position()
        me = 2 * x + y
        sibling = (x, y, 1 - c)
        others = _other_chips(x, y)
        for i, out in enumerate(out_refs):
            for k, (chip, idx) in enumerate(others):
                landed = _halves(out.at[idx], c)[0]
                _remote(landed, landed, send_sems, recv_sems, 6 * i + k, (*chip, c)).wait_recv()
                _remote(landed, landed, send_sems, recv_sems, 6 * i + 3 + k, sibling).start()
        for i, (w, out) in enumerate(zip(w_refs, out_refs)):
            for k, (chip, idx) in enumerate(others):
                mine, theirs = _halves(out.at[idx], c)
                _remote(theirs, theirs, send_sems, recv_sems, 6 * i + 3 + k, sibling).wait_recv()
                _remote(mine, mine, send_sems, recv_sems, 6 * i + 3 + k, sibling).wait_send()
                _remote(_halves(w, c)[0], mine, send_sems, recv_sems, 6 * i + k, (*chip, c)).wait_send()
            pltpu.make_async_copy(w, out.at[me], local_sems.at[i]).wait()


class _SwapHalves(_Exchange):
    def __init__(self, gs):
        shapes = [jax.ShapeDtypeStruct((g.shape[0], g.shape[1] // 2) + tuple(g.shape[2:]), g.dtype) for g in gs]
        super().__init__(gs, shapes, N_CHIPS * len(gs))

    def _copies(self, g_refs, out_refs, sems):
        x, y, c = _position()
        return [_remote(_halves(g.at[j], c)[1], out.at[j], sems[0], sems[1], N_CHIPS * i + j, (x, y, 1 - c))
                for i, (g, out) in enumerate(zip(g_refs, out_refs)) for j in range(N_CHIPS)]

    def start(self, g_refs, out_refs, sems):
        for cp in self._copies(g_refs, out_refs, sems):
            cp.start()

    def finish(self, g_refs, out_refs, sems):
        for cp in self._copies(g_refs, out_refs, sems):
            cp.wait()


class _ScatterPieces(_Exchange):
    def __init__(self, ps):
        super().__init__(ps, [jax.ShapeDtypeStruct((3,) + tuple(p.shape[1:]), p.dtype) for p in ps], 3 * len(ps))

    def _copies(self, p_refs, out_refs, sems):
        x, y, c = _position()
        return [_remote(p.at[idx], out.at[k], sems[0], sems[1], 3 * i + k, (*chip, c))
                for i, (p, out) in enumerate(zip(p_refs, out_refs)) for k, (chip, idx) in enumerate(_other_chips(x, y))]

    def start(self, p_refs, out_refs, sems):
        for cp in self._copies(p_refs, out_refs, sems):
            cp.start()

    def finish(self, p_refs, out_refs, sems):
        for cp in self._copies(p_refs, out_refs, sems):
            cp.wait()


class _JoinHalves(_Exchange):
    def __init__(self, rs):
        super().__init__(rs, [jax.ShapeDtypeStruct((2,) + tuple(r.shape), r.dtype) for r in rs], len(rs), len(rs))

    def start(self, r_refs, out_refs, sems):
        x, y, c = _position()
        for i, (r, out) in enumerate(zip(r_refs, out_refs)):
            pltpu.make_async_copy(r, out.at[c], sems[2].at[i]).start()
            _remote(r, out.at[c], sems[0], sems[1], i, (x, y, 1 - c)).start()

    def finish(self, r_refs, out_refs, sems):
        x, y, c = _position()
        for i, (r, out) in enumerate(zip(r_refs, out_refs)):
            pltpu.make_async_copy(r, out.at[c], sems[2].at[i]).wait()
            _remote(r, out.at[c], sems[0], sems[1], i, (x, y, 1 - c)).wait_send()
            _remote(out.at[1 - c], out.at[1 - c], sems[0], sems[1], i, (x, y, 1 - c)).wait_recv()


def _run_exchange(ex, name):
    def body(*refs):
        ins, outs, sems = refs[:len(ex.ins)], refs[len(ex.ins):len(ex.ins) + len(ex.out_shapes)], refs[-3:]
        ex.start(ins, outs, sems)
        ex.finish(ins, outs, sems)

    return pl.pallas_call(body, name=name, in_specs=[_HBM] * len(ex.ins), out_specs=[_HBM] * len(ex.out_shapes),
                          out_shape=ex.out_shapes, scratch_shapes=ex.sem_shapes())(*ex.ins)


def _pcall(body, name, grid, in_specs, out_specs, out_shape, scratch_shapes, semantics, args, comm=None):
    if comm is None:
        return pl.pallas_call(body, name=name, grid=grid, in_specs=in_specs, out_specs=out_specs, out_shape=out_shape,
                              scratch_shapes=scratch_shapes, compiler_params=_cparams(semantics))(*args)
    n_in, n_out, n_scr, ci, co = len(in_specs), len(out_specs), len(scratch_shapes), len(comm.ins), len(comm.out_shapes)

    def wrapped(*refs):
        parts, a = [], 0
        for k in (n_in, ci, n_out, co, n_scr, 3):
            parts.append(refs[a:a + k])
            a += k
        ins, cins, outs, couts, scr, sems = parts
        ids = [pl.program_id(i) for i in range(len(grid))]
        first = functools.reduce(jnp.logical_and, [i == 0 for i in ids])
        last = functools.reduce(jnp.logical_and, [i == g - 1 for i, g in zip(ids, grid)])

        @pl.when(first)
        def _():
            comm.start(cins, couts, sems)

        body(*ins, *outs, *scr)

        @pl.when(last)
        def _():
            comm.finish(cins, couts, sems)

    res = pl.pallas_call(
        wrapped, name=name, grid=grid, in_specs=list(in_specs) + [_HBM] * ci, out_specs=list(out_specs) + [_HBM] * co,
        out_shape=list(out_shape) + comm.out_shapes, scratch_shapes=list(scratch_shapes) + comm.sem_shapes(),
        compiler_params=_cparams(("arbitrary",) * len(grid)))(*args, *comm.ins)
    return res[:n_out], res[n_out:]


ATT_WINDOW = 128
PHASES = 16
_NT = (((1,), (1,)), ((), ()))
_TN = (((0,), (0,)), ((), ()))


def _to_phase_order(x, bl):
    n, c = x.shape
    g = n // bl // PHASES
    return x.reshape(g, PHASES, bl, c).transpose(2, 1, 0, 3).reshape(n, c)


def _from_phase_order(x, bl):
    n, c = x.shape
    g = n // bl // PHASES
    return x.reshape(bl, PHASES, g, c).transpose(2, 1, 0, 3).reshape(n, c)


def _att_geometry(p, n, bl):
    g = n // bl // PHASES
    if p == 0:
        return ((bl, PHASES, g), (bl, g // 16), (None, PHASES, 16),
                lambda sh: (lambda b, a: (b, 0, jnp.maximum(a + sh, 0))), 256, 16, lambda ids: ids[1] == 0)
    if p == 1:
        return ((bl, 4, 4, g), (bl, 4, g // 32), (None, 4, None, 32),
                lambda sh: (lambda b, r, a: (b, 0, r, jnp.maximum(a + sh, 0))), 128, 32, lambda ids: ids[2] == 0)
    return ((bl * PHASES, g), (bl * PHASES,), (None, g), lambda sh: (lambda s: (s, 0)), g, g, None)


def _att_masks(p, qb, chunk):
    def pos(idx):
        return (idx % chunk) * (qb // chunk) + idx // chunk

    dq = pos(lax.broadcasted_iota(jnp.int32, (qb, qb), 0))
    dk = pos(lax.broadcasted_iota(jnp.int32, (qb, qb), 1))
    dist = dq - dk
    return jnp.logical_and(dist >= 0, dist <= ATT_WINDOW), dist + qb <= ATT_WINDOW


def _att_call(p, n, bl, c, body, name, ins, outs):
    prefix, grid, blk, idx_fn, qb, chunk, _ = _att_geometry(p, n, bl)

    def spec(cb, sh):
        f = idx_fn(sh)
        return pl.BlockSpec(blk + (c,), lambda *ids, f=f, cb=cb: f(*ids) + (cb,))

    in_specs = [spec(cb, sh) for _, cb, sh in ins]
    out_specs = [spec(0, 0) for _ in outs]
    out_shape = [jax.ShapeDtypeStruct(prefix + (c,), dt) for dt in outs]
    res = pl.pallas_call(body, name=name, grid=grid, in_specs=in_specs, out_specs=out_specs, out_shape=out_shape,
                         compiler_params=_cparams(("parallel",) * len(grid)))(*[a.reshape(prefix + (a.shape[1],)) for a, _, _ in ins])
    return [r.reshape(n, c) for r in res]


def _att_fwd(p, qkv, qcb, bl, c, heads):
    n = qkv.shape[0]
    _, grid, _, _, qb, chunk, first_fn = _att_geometry(p, n, bl)
    n_grid = len(grid)
    has_prev = first_fn is not None
    e = c // heads
    scale = e ** -0.5

    def body(*refs):
        if has_prev:
            q_ref, kc_ref, kp_ref, vc_ref, vp_ref, o_ref, l_ref = refs
        else:
            q_ref, kc_ref, vc_ref, o_ref, l_ref = refs
        ids = [pl.program_id(a) for a in range(n_grid)]
        mc, mp = _att_masks(p, qb, chunk)
        if has_prev:
            mp = jnp.logical_and(mp, jnp.logical_not(first_fn(ids)))
        lo = lax.broadcasted_iota(jnp.int32, (qb, 128), 1) < e
        shp = o_ref.shape[:-1]
        ones = jnp.ones((qb, 128), BF16)
        n_t = c // 128
        load = lambda r, t: r[..., pl.ds(t * 128, 128)].reshape(qb, 128)
        items = [(t, h) for t in range(n_t) for h in range(2)]
        dot = functools.partial(jnp.dot, preferred_element_type=F32)
        q2 = [load(q_ref, t) for t in range(n_t)]
        kc = [load(kc_ref, t) for t in range(n_t)]
        qm = [jnp.where(lo if h == 0 else jnp.logical_not(lo), q2[t], jnp.zeros_like(q2[t])) for t, h in items]
        sc = [jnp.where(mc, lax.dot_general(qm[i], kc[t], _NT, preferred_element_type=F32) * scale, -jnp.inf)
              for i, (t, h) in enumerate(items)]
        m = [jnp.max(s, axis=1, keepdims=True) for s in sc]
        if has_prev:
            kp = [load(kp_ref, t) for t in range(n_t)]
            sp = [jnp.where(mp, lax.dot_general(qm[i], kp[t], _NT, preferred_element_type=F32) * scale, -jnp.inf)
                  for i, (t, h) in enumerate(items)]
            m = [jnp.maximum(a, jnp.max(s, axis=1, keepdims=True)) for a, s in zip(m, sp)]
        pc = [jnp.exp(s - a).astype(BF16) for s, a in zip(sc, m)]
        vc = [load(vc_ref, t) for t in range(n_t)]
        acc = [dot(pc[i], vc[t]) for i, (t, h) in enumerate(items)]
        den = [dot(x, ones) for x in pc]
        if has_prev:
            pp = [jnp.exp(s - a).astype(BF16) for s, a in zip(sp, m)]
            vp = [load(vp_ref, t) for t in range(n_t)]
            acc = [a + dot(pp[i], vp[t]) for i, ((t, h), a) in enumerate(zip(items, acc))]
            den = [d + dot(x, ones) for d, x in zip(den, pp)]
        oh = [a / d for a, d in zip(acc, den)]
        lh = [a + jnp.log(d) for a, d in zip(m, den)]
        for t in range(n_t):
            ls = pl.ds(t * 128, 128)
            o_ref[..., ls] = jnp.where(lo, oh[2 * t], oh[2 * t + 1]).astype(o_ref.dtype).reshape(shp + (128,))
            l_ref[..., ls] = jnp.where(lo, lh[2 * t], lh[2 * t + 1]).reshape(shp + (128,))

    kcb, vcb = 3, 4
    ins = [(qkv, qcb, 0), (qkv, kcb, 0)] + ([(qkv, kcb, -1)] if has_prev else []) + [(qkv, vcb, 0)] + ([(qkv, vcb, -1)] if has_prev else [])
    return _att_call(p, n, bl, c, body, name=f"att_fwd{p}", ins=ins, outs=[BF16, F32])


def _att_bwd(p, qkv, qcb, o, do, lse, dlse, bl, c, heads):
    n = qkv.shape[0]
    _, grid, _, _, qb, chunk, first_fn = _att_geometry(p, n, bl)
    n_grid = len(grid)
    has_prev = first_fn is not None
    e = c // heads
    scale = e ** -0.5

    def body(*refs):
        if has_prev:
            q_ref, kc_ref, kp_ref, vc_ref, vp_ref, o_ref, do_ref, l_ref, dl_ref, dq_ref, dkc_ref, dkp_ref, dvc_ref, dvp_ref = refs
        else:
            q_ref, kc_ref, vc_ref, o_ref, do_ref, l_ref, dl_ref, dq_ref, dkc_ref, dvc_ref = refs
        ids = [pl.program_id(a) for a in range(n_grid)]
        mc, mp = _att_masks(p, qb, chunk)
        if has_prev:
            mp = jnp.logical_and(mp, jnp.logical_not(first_fn(ids)))
        lo = lax.broadcasted_iota(jnp.int32, (qb, 128), 1) < e
        shp = o_ref.shape[:-1]
        n_t = c // 128
        load = lambda r, t: r[..., pl.ds(t * 128, 128)].reshape(qb, 128)
        items = [(t, h) for t in range(n_t) for h in range(2)]
        nt_dot = lambda a, b: lax.dot_general(a, b, _NT, preferred_element_type=F32)
        tn_dot = lambda a, b: lax.dot_general(a, b, _TN, preferred_element_type=F32)
        dot = functools.partial(jnp.dot, preferred_element_type=F32)

        def store(r, t, v):
            r[..., pl.ds(t * 128, 128)] = v.astype(r.dtype).reshape(shp + (128,))

        sel = [lo if h == 0 else jnp.logical_not(lo) for t, h in items]
        q2, kc, vc, do2 = ([load(r, t) for t in range(n_t)] for r in (q_ref, kc_ref, vc_ref, do_ref))
        qm = [jnp.where(sel[i], q2[t], jnp.zeros_like(q2[t])) for i, (t, h) in enumerate(items)]
        dom = [jnp.where(sel[i], do2[t], jnp.zeros_like(do2[t])) for i, (t, h) in enumerate(items)]
        dod = [do2[t].astype(F32) * load(o_ref, t).astype(F32) for t in range(n_t)]
        lcol = [load(l_ref, t)[:, h * e:h * e + 1] for t, h in items]
        corr = [load(dl_ref, t)[:, h * e:h * e + 1] - jnp.sum(jnp.where(sel[i], dod[t], 0.0), axis=1, keepdims=True)
                for i, (t, h) in enumerate(items)]
        pc = [jnp.exp(jnp.where(mc, nt_dot(qm[i], kc[t]) * scale, -jnp.inf) - lcol[i]) for i, (t, h) in enumerate(items)]
        dsc = [(pc[i] * (nt_dot(dom[i], vc[t]) + corr[i]) * scale).astype(BF16) for i, (t, h) in enumerate(items)]
        pc = [x.astype(BF16) for x in pc]
        dq = [dot(dsc[i], kc[t]) for i, (t, h) in enumerate(items)]
        dkc = [tn_dot(dsc[2 * t], qm[2 * t]) + tn_dot(dsc[2 * t + 1], qm[2 * t + 1]) for t in range(n_t)]
        dvc = [tn_dot(pc[2 * t], dom[2 * t]) + tn_dot(pc[2 * t + 1], dom[2 * t + 1]) for t in range(n_t)]
        if has_prev:
            kp, vp = ([load(r, t) for t in range(n_t)] for r in (kp_ref, vp_ref))
            pp = [jnp.exp(jnp.where(mp, nt_dot(qm[i], kp[t]) * scale, -jnp.inf) - lcol[i]) for i, (t, h) in enumerate(items)]
            dsp = [(pp[i] * (nt_dot(dom[i], vp[t]) + corr[i]) * scale).astype(BF16) for i, (t, h) in enumerate(items)]
            pp = [x.astype(BF16) for x in pp]
            dq = [a + dot(dsp[i], kp[t]) for i, ((t, h), a) in enumerate(zip(items, dq))]
            dkp = [tn_dot(dsp[2 * t], qm[2 * t]) + tn_dot(dsp[2 * t + 1], qm[2 * t + 1]) for t in range(n_t)]
            dvp = [tn_dot(pp[2 * t], dom[2 * t]) + tn_dot(pp[2 * t + 1], dom[2 * t + 1]) for t in range(n_t)]
        for t in range(n_t):
            store(dq_ref, t, jnp.where(lo, dq[2 * t], dq[2 * t + 1]))
            store(dkc_ref, t, dkc[t])
            store(dvc_ref, t, dvc[t])
            if has_prev:
                store(dkp_ref, t, dkp[t])
                store(dvp_ref, t, dvp[t])

    kcb, vcb = 3, 4
    ins = [(qkv, qcb, 0), (qkv, kcb, 0)] + ([(qkv, kcb, -1)] if has_prev else []) + [(qkv, vcb, 0)] + ([(qkv, vcb, -1)] if has_prev else [])
    ins += [(o, 0, 0), (do, 0, 0), (lse, 0, 0), (dlse, 0, 0)]
    res = _att_call(p, n, bl, c, body, name=f"att_bwd{p}", ins=ins, outs=[BF16] * (5 if has_prev else 3))
    if has_prev:
        dq, dkc, dkp, dvc, dvp = res
        return dq, dkc, dkp, dvc, dvp
    dq, dkc, dvc = res
    return dq, dkc, None, dvc, None


def _att_fold_prev(p, cur, prv, bl):
    if prv is None:
        return cur.astype(F32)
    n, c = cur.shape
    prefix, _, _, _, _, chunk, _ = _att_geometry(p, n, bl)
    v = prv.reshape(prefix + (c,)).astype(F32)
    shifted = jnp.concatenate([v[..., chunk:, :], jnp.zeros_like(v[..., :chunk, :])], axis=-2)
    return cur.astype(F32) + shifted.reshape(n, c)


def _attention_fwd2(qkv_tb, bl, heads):
    n, w = qkv_tb.shape
    c = w // 5
    qkv = _to_phase_order(qkv_tb, bl)
    outs = [_att_fwd(p, qkv, p, bl, c, heads) for p in range(3)]
    ins = [("row", o, c, 0) for o, _ in outs] + [("row", l, c, 0) for _, l in outs]
    o, = _rowwise(_combine_fwd_fn, "comb_fwd", n, ins, [(c, BF16)])
    return _from_phase_order(o, bl), (qkv, outs)


def _attention_bwd2(do_tb, saved, bl, heads):
    qkv, outs = saved
    n, c = do_tb.shape
    e = c // heads
    lane = jnp.arange(c) // e
    jmat = (lane[:, None] == lane[None, :]).astype(F32)
    do = _to_phase_order(do_tb, bl)
    ins = [("row", o, c, 0) for o, _ in outs] + [("row", l, c, 0) for _, l in outs] + [("row", do, c, 0), ("par", jmat)]
    res = _rowwise(_combine_bwd_fn, "comb_bwd", n, ins, [(c, BF16)] * 3 + [(c, F32)] * 3)
    dqs, dk, dv = [], 0.0, 0.0
    for p in range(3):
        dq, dkc, dkp, dvc, dvp = _att_bwd(p, qkv, p, outs[p][0], res[p], outs[p][1], res[3 + p], bl, c, heads)
        dqs.append(dq)
        dk = dk + _att_fold_prev(p, dkc, dkp, bl)
        dv = dv + _att_fold_prev(p, dvc, dvp, bl)
    dqkv = jnp.concatenate(dqs + [dk.astype(BF16), dv.astype(BF16)], axis=1)
    return _from_phase_order(dqkv, bl)


ATT_HEADS = 8
SSM_GROUPS, SSM_STATE, SSM_GROUP = 32, 64, 16


def _row(v):
    return v.reshape(1, -1)


def _carried(result, carry, key, hidden):
    if carry.get(key) is None:
        return result
    result, hidden[key] = result
    return result


def _layer_fwd(x, w, p, bl, carry):
    n, d = x.shape
    c = d // 2
    hidden = {}
    h, = _rowwise(_rms_fwd_fn, "rms_fwd", n, [("row", x, d, 0), ("par", _row(p["norm1_g"]))], [(d, BF16)])
    proj = _mm(h, w["w_in"], "nn", BF16, "mm_in")
    disc, disc_vjp = jax.vjp(_ssm_disc, p["ssm_lambda_re"], p["ssm_lambda_im"], p["ssm_log_dt"], p["ssm_b_re"], p["ssm_b_im"])
    bbd, cdm, a8 = _ssm_pack(*disc, p["ssm_c_re"], p["ssm_c_im"])
    ypre, yg, s_all = _carried(_ssm_fwd(proj, bbd, cdm, a8, _row(p["ssm_d"]), bl, "ssm_fwd", comm=carry.get("ssm_fwd")),
                               carry, "ssm_fwd", hidden)
    zs = _mm(yg, w["w_ssm_glu"], "nn", BF16, "mm_glu")
    o, att = _attention_fwd2(proj[:, c:6 * c], bl, ATT_HEADS)
    ya = _mm(o, w["w_att_up"], "nn", BF16, "mm_att")
    w32 = jnp.concatenate([p["conv_w"], jnp.zeros((1, c), F32)], axis=0)
    hc, hconv = _conv_fwd(proj, 6, w32, _row(p["conv_b"]), _row(p["conv_ln_g"]), _row(p["conv_ln_b"]), bl, c, "conv_fwd")
    yc = _mm(hc, w["w_conv_pw2"], "nn", BF16, "mm_pw2")
    gates = [("row", proj, d, 4), ("row", proj, d, 5), ("row", proj, d, 6), ("par", _row(p["b_gate"]))]
    branches = [("row", zs, 2 * d, 0), ("row", ya, d, 0), ("row", yc, d, 0)]
    merged, = _rowwise(_merge_fwd_fn, "merge_fwd", n, gates + branches, [(d, BF16)])
    xm = _mm(merged, w["w_out"], "nn", F32, "mm_out", res=x)
    h2, = _rowwise(_rms_fwd_fn, "rms_fwd", n, [("row", xm, d, 0), ("par", _row(p["norm2_g"]))], [(d, BF16)])
    z = _carried(_mm(h2, w["w_ffn_in"], "nn", BF16, "mm_ffn_in", comm=carry.get("mm_ffn_in")), carry, "mm_ffn_in", hidden)
    f = z.shape[1] // 2
    a, = _rowwise(_swiglu_fwd_fn, "swiglu_fwd", n, [("row", z, 2 * f, 0)], [(f, BF16)], tm=256)
    xo = _mm(a, w["w_ffn_out"], "nn", F32, "mm_ffn_out", res=xm)
    saved = dict(x=x, h=h, proj=proj, disc_vjp=disc_vjp, bbd=bbd, cdm=cdm, a8=a8, ypre=ypre, yg=yg, s_all=s_all, zs=zs, o=o,
                 att=att, ya=ya, w32=w32, hc=hc, hconv=hconv, yc=yc, gates=gates, branches=branches, merged=merged, xm=xm,
                 h2=h2, z=z, a=a)
    return xo, saved, hidden


def _layer_bwd(dxo, s, w, p, bl, carry):
    n, d = dxo.shape
    c = d // 2
    g, bufs, hidden = {}, {}, {}
    f = s["a"].shape[1]

    def dw(key, a, dy, name):
        bufs[key] = _mm_dw(a, dy, name, 1 if key in ROW_SHARDED else N_CHIPS)

    da = _mm(dxo, w["w_ffn_out"], "nt", BF16, "mm_ffn_out_dx")
    dw("w_ffn_out", s["a"], dxo, "mm_ffn_out_dw")
    dz, = _rowwise(_swiglu_bwd_fn, "swiglu_bwd", n, [("row", s["z"], 2 * f, 0), ("row", da, f, 0)], [(2 * f, BF16)], tm=256)
    dh2 = _mm(dz, w["w_ffn_in"], "nt", F32, "mm_ffn_in_dx")
    dw("w_ffn_in", s["h2"], dz, "mm_ffn_in_dw")
    dxm, dg2 = _rowwise(_rms_bwd_fn, "rms_bwd", n, [("row", s["xm"], d, 0), ("par", _row(p["norm2_g"])), ("row", dh2, d, 0),
                                                   ("row", dxo, d, 0)], [(d, F32)], [d])
    g["norm2_g"] = dg2[0]
    dmerged = _mm(dxm, w["w_out"], "nt", BF16, "mm_out_dx")
    dw("w_out", s["merged"], dxm, "mm_out_dw")
    dgl, dzs, dya, dyc, dbg = _rowwise(_merge_bwd_fn, "merge_bwd", n, s["gates"] + s["branches"] + [("row", dmerged, d, 0)],
                                       [(3 * d, BF16), (2 * d, BF16), (d, BF16), (d, BF16)], [3 * d], tm=256)
    g["b_gate"] = dbg[0]
    dyg = _mm(dzs, w["w_ssm_glu"], "nt", BF16, "mm_glu_dx")
    dw("w_ssm_glu", s["yg"], dzs, "mm_glu_dw")
    du, dbb, dcd, dab, dd = _carried(
        _ssm_bwd(dyg, s["ypre"], s["proj"], s["s_all"], s["cdm"].transpose(0, 2, 1), s["bbd"].transpose(0, 2, 1), s["a8"],
                 _row(p["ssm_d"]), bl, "ssm_bwd", comm=carry.get("ssm_bwd")), carry, "ssm_bwd", hidden)
    dab_re, dab_im, dbb_re, dbb_im, g["ssm_c_re"], g["ssm_c_im"] = _ssm_unpack(dbb, dcd, dab, SSM_GROUPS, SSM_STATE, SSM_GROUP)
    (g["ssm_lambda_re"], g["ssm_lambda_im"], g["ssm_log_dt"], g["ssm_b_re"],
     g["ssm_b_im"]) = s["disc_vjp"]((dab_re, dab_im, dbb_re, dbb_im))
    g["ssm_d"] = dd[0]
    do = _mm(dya, w["w_att_up"], "nt", BF16, "mm_att_dx")
    dw("w_att_up", s["o"], dya, "mm_att_dw")
    dqkv = _attention_bwd2(do, s["att"], bl, ATT_HEADS)
    dhc = _mm(dyc, w["w_conv_pw2"], "nt", BF16, "mm_pw2_dx")
    dw("w_conv_pw2", s["hc"], dyc, "mm_pw2_dw")
    dcv, dcw, dcb, dlg, dlb = _carried(
        _conv_bwd(s["proj"], 6, dhc, s["hconv"], s["w32"], _row(p["conv_ln_g"]), _row(p["conv_ln_b"]), bl, c, "conv_bwd",
                  comm=carry.get("conv_bwd")), carry, "conv_bwd", hidden)
    g["conv_w"], g["conv_b"], g["conv_ln_g"], g["conv_ln_b"] = dcw, dcb[0], dlg[0], dlb[0]
    dproj = jnp.concatenate([du, dqkv, dcv, dgl], axis=1)
    dh = _mm(dproj, w["w_in"], "nt", F32, "mm_in_dx")
    dw("w_in", s["h"], dproj, "mm_in_dw")
    dx, dg1 = _rowwise(_rms_bwd_fn, "rms_bwd", n, [("row", s["x"], d, 0), ("par", _row(p["norm1_g"])), ("row", dh, d, 0),
                                                  ("row", dxm, d, 0)], [(d, F32)], [d])
    g["norm1_g"] = dg1[0]
    return dx, g, bufs, hidden


WEIGHTS = ['norm1_g', 'w_in', 'b_gate', 'ssm_lambda_re', 'ssm_lambda_im', 'ssm_log_dt', 'ssm_b_re', 'ssm_b_im', 'ssm_c_re',
           'ssm_c_im', 'ssm_d', 'w_ssm_glu', 'w_att_up', 'conv_w', 'conv_b', 'conv_ln_g', 'conv_ln_b', 'w_conv_pw2', 'w_out',
           'norm2_g', 'w_ffn_in', 'w_ffn_out', 'final_g']
BIG = ['w_in', 'w_ssm_glu', 'w_att_up', 'w_conv_pw2', 'w_out', 'w_ffn_in', 'w_ffn_out']
ROW_SHARDED = ('w_out', 'w_ffn_out')
SMALL = [k for k in WEIGHTS if k not in BIG]
LANES = 1024
N_CHIPS = 4
ROW_TILE_BYTES = 36 * 1024 * 1024
MIN_SHARD_TILE = 1024


def _pad_rows(a, rows):
    return jnp.concatenate([a, jnp.zeros((rows - a.shape[0],) + a.shape[1:], a.dtype)], axis=0) if rows > a.shape[0] else a


def _row_tile(rows, width, n_arrays):
    best = 16
    for t in range(16, rows + 1, 16):
        if rows % t == 0 and t * width * 4 * n_arrays * 2 <= ROW_TILE_BYTES:
            best = t
    return best


def _flat_fn(fn, name, ins, n_out, rows):
    return _rowwise(fn, name, rows, [("row", a, LANES, 0) for a in ins], [(LANES, F32)] * n_out, tm=rows)


def _reduce_prepare(bufs):
    landed = _run_exchange(_SwapHalves([b16 for _, b16 in bufs]), "rs_swap")
    p32s, p16s = [], []
    for (b32, _), la in zip(bufs, landed):
        s, m, cs = b32.shape
        h = m // 2
        tm = _row_tile(h, cs, 4)

        def body(g_ref, l_ref, o32, o16):
            r = g_ref[...] + l_ref[...].astype(F32)
            o32[...] = r
            o16[...] = r.astype(BF16)

        piece = pl.BlockSpec((None, tm, cs), lambda j, i: (j, i, 0))
        mine = pl.BlockSpec((None, None, tm, cs), lambda j, i: (j, _core_index(), i, 0))
        p32, p16 = pl.pallas_call(
            body, name="rs_add", grid=(s, h // tm), in_specs=[mine, piece], out_specs=[piece, piece],
            out_shape=[jax.ShapeDtypeStruct((s, h, cs), F32), jax.ShapeDtypeStruct((s, h, cs), BF16)],
            compiler_params=_cparams(("parallel", "parallel")))(b32.reshape(s, 2, h, cs), la)
        p32s.append(p32)
        p16s.append(p16)
    return p32s, p16s


def _reduce_finish(p32s, arrived):
    reduced = []
    for p32, lb in zip(p32s, arrived):
        _, h, cs = lb.shape
        tm = _row_tile(h, cs, 5)

        def body(p_ref, a_ref, b_ref, c_ref, o_ref):
            o_ref[...] = ((p_ref[...] + a_ref[...].astype(F32)) + b_ref[...].astype(F32)) + c_ref[...].astype(F32)

        mine = pl.BlockSpec((None, tm, cs), lambda i: (_chip_index(), i, 0))
        other = [pl.BlockSpec((None, tm, cs), lambda i, k=k: (k, i, 0)) for k in range(3)]
        reduced.append(pl.pallas_call(
            body, name="rs_sum", grid=(h // tm,), in_specs=[mine] + other, out_specs=pl.BlockSpec((tm, cs), lambda i: (i, 0)),
            out_shape=jax.ShapeDtypeStruct((h, cs), F32), compiler_params=_cparams(("parallel",)))(p32, lb, lb, lb))
    joined = _run_exchange(_JoinHalves(reduced), "rs_gather")
    return [j.reshape(2 * j.shape[1], j.shape[2]) for j in joined]


def _adamw_layers(w, g_layers, m, v):
    depth, rows, cs = w.shape
    tm = _row_tile(rows, cs, 8)
    nb = rows // tm

    def body(*refs):
        w_ref, m_ref, v_ref = refs[:3]
        g_refs = refs[3:3 + depth]
        go_ref, d_ref, mo_ref, vo_ref = refs[3 + depth:]
        layer = pl.program_id(0)
        g = g_refs[0][...]
        for l in range(1, depth):
            g = jnp.where(layer == l, g_refs[l][...], g)
        delta, mo, vo = _adamw_fn(w_ref[...], g, m_ref[...], v_ref[...])
        go_ref[...], d_ref[...], mo_ref[...], vo_ref[...] = g, delta, mo, vo

    stacked = pl.BlockSpec((None, tm, cs), lambda l, i: (l, i, 0))
    g_specs = [pl.BlockSpec((tm, cs), lambda l, i, k=k: (jnp.where(l == k, i, jnp.where(l < k, 0, nb - 1)), 0)) for k in range(depth)]
    return pl.pallas_call(
        body, name="adamw", grid=(depth, nb), in_specs=[stacked] * 3 + g_specs, out_specs=[stacked] * 4,
        out_shape=[jax.ShapeDtypeStruct(w.shape, F32)] * 4, compiler_params=_cparams(("arbitrary", "arbitrary")))(w, m, v, *g_layers)


def _sum4_fn(a, b, c, d):
    return (((a.astype(F32) + b.astype(F32)) + c.astype(F32)) + d.astype(F32),)


def _add2_fn(a, b):
    return (a + b,)


def kernel(x, norm1_g, w_in, b_gate, ssm_lambda_re, ssm_lambda_im, ssm_log_dt, ssm_b_re, ssm_b_im, ssm_c_re, ssm_c_im, ssm_d, w_ssm_glu, w_att_up, conv_w, conv_b, conv_ln_g, conv_ln_b, w_conv_pw2, w_out, norm2_g, w_ffn_in, w_ffn_out, final_g, loss_target, m_norm1_g, m_w_in, m_b_gate, m_ssm_lambda_re, m_ssm_lambda_im, m_ssm_log_dt, m_ssm_b_re, m_ssm_b_im, m_ssm_c_re, m_ssm_c_im, m_ssm_d, m_w_ssm_glu, m_w_att_up, m_conv_w, m_conv_b, m_conv_ln_g, m_conv_ln_b, m_w_conv_pw2, m_w_out, m_norm2_g, m_w_ffn_in, m_w_ffn_out, m_final_g, v_norm1_g, v_w_in, v_b_gate, v_ssm_lambda_re, v_ssm_lambda_im, v_ssm_log_dt, v_ssm_b_re, v_ssm_b_im, v_ssm_c_re, v_ssm_c_im, v_ssm_d, v_w_ssm_glu, v_w_att_up, v_conv_w, v_conv_b, v_conv_ln_g, v_conv_ln_b, v_w_conv_pw2, v_w_out, v_norm2_g, v_w_ffn_in, v_w_ffn_out, v_final_g):
    args = dict(locals())
    wts = {k: args[k] for k in WEIGHTS}
    mom = {k: args["m_" + k] for k in WEIGHTS}
    var = {k: args["v_" + k] for k in WEIGHTS}
    bl, seq, d = x.shape
    n = bl * seq
    depth = norm1_g.shape[0]
    cx, cy, cc = _position()
    me = 2 * cx + cy

    assert depth == 2, "the exchanges of layer 1 are hidden behind layer 0's kernels"
    first = BIG[:1]
    rest = BIG[1:]

    shards = lambda keys, l: [wts[k][l].astype(BF16) for k in keys]

    def whole(keys, gathered):
        out = {}
        for k, a in zip(keys, gathered):
            _, ks, cs = a.shape
            if k in ROW_SHARDED:
                out[k] = a.reshape(N_CHIPS * ks, cs)
            elif cs < MIN_SHARD_TILE:
                out[k] = a.transpose(1, 0, 2).reshape(ks, N_CHIPS * cs)
            else:
                out[k] = a
        return out

    gathered = _run_exchange(_GatherShards(shards(BIG, 0) + [conv_w]), "gather_weights")
    conv_full = gathered[-1].transpose(1, 2, 0, 3).reshape(depth, CONV_WIDTH, -1)
    full = [whole(BIG, gathered[:-1]), None]
    params = lambda l: dict({k: wts[k][l] for k in SMALL if k not in ("final_g", "conv_w")}, conv_w=conv_full[l])

    to_rows = lambda t: t.transpose(1, 0, 2).reshape(n, d)
    xs, s0, hidden = _layer_fwd(to_rows(x), full[0], params(0), bl,
                                {"ssm_fwd": _GatherShards(shards(first, 1)), "mm_ffn_in": _GatherShards(shards(rest, 1))})
    full[1] = dict(whole(first, hidden["ssm_fwd"]), **whole(rest, hidden["mm_ffn_in"]))
    xs, s1, _ = _layer_fwd(xs, full[1], params(1), bl, {})
    dx, sq, dgf = _rowwise(_loss_fn, "loss_head", n, [("row", xs, d, 0), ("par", _row(final_g)), ("row", to_rows(loss_target), d, 0)],
                           [(d, F32)], [d, d])
    loss = lax.psum(0.5 * jnp.sum(sq) / d, ("x", "y", "c"))

    pieces = lambda bufs, keys: [tuple(b.reshape(N_CHIPS, -1, b.shape[-1]) for b in bufs[k]) for k in keys]
    dx, g1, bufs1, _ = _layer_bwd(dx, s1, full[1], params(1), bl, {})
    p32_1, p16_1 = _reduce_prepare(pieces(bufs1, BIG))
    dx, g0, bufs0, hidden = _layer_bwd(dx, s0, full[0], params(0), bl,
                                       {"ssm_bwd": _ScatterPieces(p16_1[:1]), "conv_bwd": _ScatterPieces(p16_1[1:])})
    red1 = _reduce_finish(p32_1, list(hidden["ssm_bwd"]) + list(hidden["conv_bwd"]))
    p32_0, p16_0 = _reduce_prepare(pieces(bufs0, BIG))
    red0 = _reduce_finish(p32_0, _run_exchange(_ScatterPieces(p16_0), "rs_scatter"))
    grads = {"final_g": dgf[0]}
    for k in SMALL:
        if k != "final_g":
            grads[k] = jnp.stack([g0[k], g1[k]])
    grad_x = dx.reshape(seq, bl, d).transpose(1, 0, 2)
    outs = {}
    for k, r0, r1 in zip(BIG, red0, red1):
        for tag, a in zip(("grad", "delta", "m", "v"), _adamw_layers(wts[k], [r0, r1], mom[k], var[k])):
            outs[tag, k] = a

    def flat1(t):
        v = jnp.concatenate([t[k].reshape(-1) for k in SMALL])
        rows = -(-v.size // (8 * LANES)) * 8
        return _pad_rows(v, rows * LANES).reshape(rows, LANES), rows

    def unflat1(flat, shapes):
        out, off, v = {}, 0, flat.reshape(-1)
        for k in SMALL:
            size = math.prod(shapes[k])
            out[k] = v[off:off + size].reshape(shapes[k])
            off += size
        return out

    grads["conv_w"] = grads["conv_w"][:, :CONV_WIDTH]
    gs, rows = flat1(grads)
    chip_sum, = _flat_fn(_add2_fn, "ar_add", [gs, _swap_sibling(gs, "ar_swap")], 1, rows)
    slots = _chip_allgather(chip_sum, "ar_gather")
    gs_red, = _flat_fn(_sum4_fn, "ar_sum", [slots[j] for j in range(N_CHIPS)], 1, rows)
    g_sm = unflat1(gs_red, {k: grads[k].shape for k in SMALL})
    cs = conv_w.shape[2]
    g_sm["conv_w"] = lax.dynamic_slice_in_dim(g_sm["conv_w"], me * cs, cs, axis=2)
    (w1, rows), (g1, _), (m1, _), (v1, _) = flat1(wts), flat1(g_sm), flat1(mom), flat1(var)
    sm_out = _flat_fn(_adamw_fn, "adamw_small", [w1, g1, m1, v1], 3, rows)
    shapes = {k: wts[k].shape for k in SMALL}
    for tag, a in zip(("delta", "m", "v"), sm_out):
        for k, t in unflat1(a, shapes).items():
            outs[tag, k] = t
    for k in SMALL:
        outs["grad", k] = g_sm[k]
    return (loss, grad_x, *[outs["grad", k] for k in WEIGHTS], *[outs["delta", k] for k in WEIGHTS],
            *[outs["m", k] for k in WEIGHTS], *[outs["v", k] for k in WEIGHTS])
```

```python
import functools
import math

import jax
import jax.numpy as jnp
from jax import lax
from jax.experimental import pallas as pl
from jax.experimental.pallas import tpu as pltpu

F32 = jnp.float32
BF16 = jnp.bfloat16
VMEM_LIMIT = 56 * 1024 * 1024


def _cparams(sem):
    return pltpu.CompilerParams(dimension_semantics=sem, vmem_limit_bytes=VMEM_LIMIT)


_DIMS = {"nn": (((1,), (0,)), ((), ())), "nt": (((1,), (1,)), ((), ())), "tn": (((0,), (0,)), ((), ()))}


MM_ROWS = 1024
MM_DW_INPUT_ELEMS = 2 * 1024 * 1024


def _div_tile(n, cap):
    best = None
    for t in range(128, min(n, cap) + 1, 128):
        if n % t == 0:
            best = t
    return best or n


def _mm(a, b, form, out_dtype, name, res=None, comm=None):
    sharded = b.ndim == 3
    kdim, cs = b.shape[-2], b.shape[-1]
    s = b.shape[0] if sharded else 1
    m = a.shape[0]
    tm = MM_ROWS if m % MM_ROWS == 0 else _div_tile(m, MM_ROWS)
    if form == "nn":
        n, kd = s * cs, kdim
        tn, tk = _div_tile(cs, 1792), _div_tile(kdim, 2048)
        per = cs // tn
        b_blk = (tk, tn)
        b_idx = (lambda i, j, k: (j // per, k, j % per)) if sharded else (lambda i, j, k: (k, j))
    else:
        n, kd = kdim, s * cs
        tn, tk = _div_tile(kdim, 1408), _div_tile(cs, 1792)
        per = cs // tk
        b_blk = (tn, tk)
        b_idx = (lambda i, j, k: (k // per, j, k % per)) if sharded else (lambda i, j, k: (j, k))
    nk = kd // tk
    a_spec = pl.BlockSpec((tm, tk), lambda i, j, k: (i, k))
    b_spec = pl.BlockSpec(((None,) + b_blk) if sharded else b_blk, b_idx)
    o_spec = pl.BlockSpec((tm, tn), lambda i, j, k: (i, j))
    dims = _DIMS[form]

    def body(*refs):
        a_ref, b_ref = refs[:2]
        r_ref = refs[2] if res is not None else None
        o_ref = refs[3] if res is not None else refs[2]
        p = lax.dot_general(a_ref[...].astype(BF16), b_ref[...], dims, preferred_element_type=F32)

        def finish(r):
            if r_ref is not None:
                r = r + r_ref[...]
            o_ref[...] = r.astype(out_dtype)

        if nk == 1:
            finish(p)
            return
        acc = refs[-1]
        k = pl.program_id(2)

        @pl.when(k == 0)
        def _():
            acc[...] = p

        @pl.when(k > 0)
        def _():
            acc[...] += p

        @pl.when(k == nk - 1)
        def _():
            finish(acc[...])

    ins = [a, b] + ([] if res is None else [res])
    in_specs = [a_spec, b_spec] + ([] if res is None else [o_spec])
    out = _pcall(body, name, (m // tm, n // tn, nk), in_specs, [o_spec], [jax.ShapeDtypeStruct((m, n), out_dtype)],
                 [pltpu.VMEM((tm, tn), F32)] if nk > 1 else [], ("parallel", "parallel", "arbitrary"), ins, comm)
    return out[0] if comm is None else (out[0][0], out[1])


def _mm_dw(a, dy, name, shards):
    r, m = a.shape
    c = dy.shape[1]
    cs = c // shards
    tm, tn = _div_tile(m, 1408), _div_tile(cs, 1792)
    tk = _div_tile(r, max(512, min(2048, MM_DW_INPUT_ELEMS // (tm + tn))))
    per = cs // tn
    nk = r // tk

    def body(a_ref, b_ref, o32, o16, acc):
        k = pl.program_id(2)
        p = lax.dot_general(a_ref[...].astype(BF16), b_ref[...].astype(BF16), _DIMS["tn"], preferred_element_type=F32)

        @pl.when(k == 0)
        def _():
            acc[...] = p

        @pl.when(k > 0)
        def _():
            acc[...] += p

        @pl.when(k == nk - 1)
        def _():
            o32[...] = acc[...]
            o16[...] = acc[...].astype(BF16)

    o_spec = pl.BlockSpec((None, tm, tn), lambda i, j, k: (j // per, i, j % per))
    shape = (shards, m, cs)
    in_specs = [pl.BlockSpec((tk, tm), lambda i, j, k: (k, i)), pl.BlockSpec((tk, tn), lambda i, j, k: (k, j))]
    return _pcall(body, name, (m // tm, c // tn, nk), in_specs, [o_spec, o_spec],
                  [jax.ShapeDtypeStruct(shape, F32), jax.ShapeDtypeStruct(shape, BF16)], [pltpu.VMEM((tm, tn), F32)],
                  ("parallel", "parallel", "arbitrary"), [a, dy])


def _core_index():
    return lax.axis_index("c")


def _chip_index():
    return 2 * lax.axis_index("x") + lax.axis_index("y")


def _rowwise(fn, name, n_rows, ins, outs, accs=(), tm=512):
    n_in, n_out, n_acc = len(ins), len(outs), len(accs)
    in_specs, args = [], []
    for spec in ins:
        if spec[0] == "row":
            _, arr, w, cb = spec
            in_specs.append(pl.BlockSpec((tm, w), lambda i, cb=cb: (i, cb)))
        elif spec[0] == "rowoff":
            _, arr, w, cb, index_fn, span = spec
            in_specs.append(pl.BlockSpec((tm, w), lambda i, cb=cb, index_fn=index_fn, nb=span // tm: (index_fn() * nb + i, cb)))
        elif spec[0] == "rowblk":
            _, arr, w, cb, start = spec
            in_specs.append(pl.BlockSpec((tm, w), lambda i, cb=cb, nb=start // tm: (nb + i, cb)))
        else:
            arr = spec[1]
            in_specs.append(pl.BlockSpec(arr.shape, lambda i: (0, 0)))
        args.append(arr)
    out_specs = [pl.BlockSpec((tm, w), lambda i: (i, 0)) for w, _ in outs]
    out_specs += [pl.BlockSpec((1, w), lambda i: (0, 0)) for w in accs]
    out_shape = [jax.ShapeDtypeStruct((n_rows, w), dt) for w, dt in outs]
    out_shape += [jax.ShapeDtypeStruct((1, w), F32) for w in accs]

    def body(*refs):
        i = pl.program_id(0)
        res = fn(*[r[...] for r in refs[:n_in]])
        for o_ref, r in zip(refs[n_in:n_in + n_out], res[:n_out]):
            o_ref[...] = r.astype(o_ref.dtype)
        for a_ref, r in zip(refs[n_in + n_out:], res[n_out:]):
            @pl.when(i == 0)
            def _(a_ref=a_ref, r=r):
                a_ref[...] = r

            @pl.when(i > 0)
            def _(a_ref=a_ref, r=r):
                a_ref[...] += r

    return pl.pallas_call(
        body, name=name, grid=(n_rows // tm,), in_specs=in_specs, out_specs=out_specs, out_shape=out_shape,
        compiler_params=_cparams(("arbitrary",)))(*args)


EPS = 1e-6


def _sig(x):
    return 1.0 / (1.0 + jnp.exp(-x))


def _colsum(x):
    return jnp.sum(x, axis=0, keepdims=True)


def _rms_fwd_fn(x, g):
    r = lax.rsqrt(jnp.mean(x * x, axis=-1, keepdims=True) + EPS)
    return (x * r * g,)


def _rms_bwd_fn(x, g, dh, dres):
    dh = dh.astype(F32)
    r = lax.rsqrt(jnp.mean(x * x, axis=-1, keepdims=True) + EPS)
    xh = x * r
    dyg = dh * g
    dx = r * (dyg - xh * jnp.mean(dyg * xh, axis=-1, keepdims=True)) + dres
    return dx, _colsum(dh * xh)


def _loss_fn(x, g, t):
    d = x.shape[-1]
    r = lax.rsqrt(jnp.mean(x * x, axis=-1, keepdims=True) + EPS)
    xh = x * r
    err = xh * g - t
    dy = err * (1.0 / d)
    dyg = dy * g
    dx = r * (dyg - xh * jnp.mean(dyg * xh, axis=-1, keepdims=True))
    return dx, _colsum(err * err), _colsum(dy * xh)


def _swiglu_fwd_fn(z):
    f = z.shape[-1] // 2
    z1, z2 = z[:, :f].astype(F32), z[:, f:].astype(F32)
    return (z1 * _sig(z1) * z2,)


def _swiglu_bwd_fn(z, da):
    f = z.shape[-1] // 2
    z1, z2, da = z[:, :f].astype(F32), z[:, f:].astype(F32), da.astype(F32)
    s = _sig(z1)
    dz1 = da * z2 * (s * (1.0 + z1 * (1.0 - s)))
    dz2 = da * (z1 * s)
    return (jnp.concatenate([dz1, dz2], axis=1),)


def _merge_fwd_fn(g0, g1, g2, bg, zs, ya, yc):
    d = ya.shape[-1]
    bg = bg.astype(F32)
    zs = zs.astype(F32)
    ys = zs[:, :d] * _sig(zs[:, d:])
    m = _sig(g0.astype(F32) + bg[:, :d]) * ys
    m = m + _sig(g1.astype(F32) + bg[:, d:2 * d]) * ya.astype(F32)
    m = m + _sig(g2.astype(F32) + bg[:, 2 * d:]) * yc.astype(F32)
    return (m,)


def _merge_bwd_fn(g0, g1, g2, bg, zs, ya, yc, dm):
    d = ya.shape[-1]
    bg = bg.astype(F32)
    zs = zs.astype(F32)
    dm = dm.astype(F32)
    z1, s2 = zs[:, :d], _sig(zs[:, d:])
    ys = z1 * s2
    s0 = _sig(g0.astype(F32) + bg[:, :d])
    s1 = _sig(g1.astype(F32) + bg[:, d:2 * d])
    s3 = _sig(g2.astype(F32) + bg[:, 2 * d:])
    dgl = jnp.concatenate([dm * ys * s0 * (1.0 - s0), dm * ya.astype(F32) * s1 * (1.0 - s1),
                           dm * yc.astype(F32) * s3 * (1.0 - s3)], axis=1)
    dys = dm * s0
    dzs = jnp.concatenate([dys * s2, dys * z1 * s2 * (1.0 - s2)], axis=1)
    return dgl, dzs, dm * s1, dm * s3, _colsum(dgl)


def _combine_fwd_fn(o0, o1, o2, l0, l1, l2):
    m = jnp.maximum(jnp.maximum(l0, l1), l2)
    e0, e1, e2 = jnp.exp(l0 - m), jnp.exp(l1 - m), jnp.exp(l2 - m)
    inv = 1.0 / (e0 + e1 + e2)
    return ((e0 * o0.astype(F32) + e1 * o1.astype(F32) + e2 * o2.astype(F32)) * inv,)


def _combine_bwd_fn(o0, o1, o2, l0, l1, l2, do, jmat):
    m = jnp.maximum(jnp.maximum(l0, l1), l2)
    e0, e1, e2 = jnp.exp(l0 - m), jnp.exp(l1 - m), jnp.exp(l2 - m)
    inv = 1.0 / (e0 + e1 + e2)
    w0, w1, w2 = e0 * inv, e1 * inv, e2 * inv
    do = do.astype(F32)

    def headsum(x):
        return jnp.dot(x, jmat, preferred_element_type=F32, precision=lax.Precision.HIGHEST)

    dw0, dw1, dw2 = headsum(do * o0.astype(F32)), headsum(do * o1.astype(F32)), headsum(do * o2.astype(F32))
    mean = w0 * dw0 + w1 * dw1 + w2 * dw2
    return w0 * do, w1 * do, w2 * do, w0 * (dw0 - mean), w1 * (dw1 - mean), w2 * (dw2 - mean)


ADAM_LR, ADAM_B1, ADAM_B2, ADAM_EPS, ADAM_WD, ADAM_STEP = 0.001, 0.9, 0.999, 1e-08, 0.01, 10


def _adamw_fn(w, g, m, v):
    m = ADAM_B1 * m + (1.0 - ADAM_B1) * g
    v = ADAM_B2 * v + (1.0 - ADAM_B2) * (g * g)
    m_hat = m / (1.0 - ADAM_B1 ** ADAM_STEP)
    v_hat = v / (1.0 - ADAM_B2 ** ADAM_STEP)
    delta = -ADAM_LR * (m_hat / (jnp.sqrt(v_hat) + ADAM_EPS) + ADAM_WD * w)
    return delta, m, v


CONV_WIDTH = 31


def _conv_fwd(proj, cb, w32, conv_b, ln_g, ln_b, bl, c, name, tm=512):
    n = proj.shape[0]
    hp = (CONV_WIDTH - 1) * bl
    nt = n // tm

    def body(ap_ref, gp_ref, a_ref, g_ref, w_ref, cb_ref, lg_ref, lb_ref, hc_ref, hconv_ref, ext):
        i = pl.program_id(0)
        ext[pl.ds(hp, tm), :] = a_ref[...].astype(F32) * _sig(g_ref[...].astype(F32))
        hgp = ap_ref[pl.ds(tm - hp, hp), :].astype(F32) * _sig(gp_ref[pl.ds(tm - hp, hp), :].astype(F32))
        ext[pl.ds(0, hp), :] = jnp.where(i > 0, hgp, 0.0)
        acc = jnp.zeros((tm, c), F32) + cb_ref[...]
        for j in range(CONV_WIDTH):
            acc = acc + w_ref[j:j + 1, :] * ext[pl.ds(j * bl, tm), :]
        hconv_ref[...] = acc.astype(hconv_ref.dtype)
        h = hconv_ref[...].astype(F32)
        mu = jnp.mean(h, axis=-1, keepdims=True)
        xc = h - mu
        var = jnp.mean(xc * xc, axis=-1, keepdims=True)
        hn = xc * lax.rsqrt(var + EPS) * lg_ref[...] + lb_ref[...]
        hc_ref[...] = (hn * _sig(hn)).astype(hc_ref.dtype)

    prev = lambda i, k: (jnp.maximum(i - 1, 0), k)
    par = lambda arr: pl.BlockSpec(arr.shape, lambda i: (0, 0))
    return pl.pallas_call(
        body, name=name, grid=(nt,),
        in_specs=[pl.BlockSpec((tm, c), functools.partial(prev, k=cb)), pl.BlockSpec((tm, c), functools.partial(prev, k=cb + 1)),
                  pl.BlockSpec((tm, c), lambda i: (i, cb)), pl.BlockSpec((tm, c), lambda i: (i, cb + 1)),
                  par(w32), par(conv_b), par(ln_g), par(ln_b)],
        out_specs=[pl.BlockSpec((tm, c), lambda i: (i, 0))] * 2,
        out_shape=[jax.ShapeDtypeStruct((n, c), BF16)] * 2,
        scratch_shapes=[pltpu.VMEM((hp + tm, c), F32)],
        compiler_params=_cparams(("arbitrary",)))(proj, proj, proj, proj, w32, conv_b, ln_g, ln_b)


def _conv_bwd(proj, cb, dhc, hconv, w32, ln_g, ln_b, bl, c, name, tm=512, comm=None):
    n = proj.shape[0]
    hp = (CONV_WIDTH - 1) * bl
    nt = n // tm

    def ln_bwd(d, h, lg, lb):
        d, h = d.astype(F32), h.astype(F32)
        mu = jnp.mean(h, axis=-1, keepdims=True)
        xc = h - mu
        rstd = lax.rsqrt(jnp.mean(xc * xc, axis=-1, keepdims=True) + EPS)
        xh = xc * rstd
        hn = xh * lg + lb
        s = _sig(hn)
        dhn = d * (s * (1.0 + hn * (1.0 - s)))
        dxh = dhn * lg
        dh = rstd * (dxh - jnp.mean(dxh, axis=-1, keepdims=True) - xh * jnp.mean(dxh * xh, axis=-1, keepdims=True))
        return dh, dhn, xh

    def body(ap_ref, gp_ref, a_ref, g_ref, d_ref, dn_ref, h_ref, hn_ref, w_ref, lg_ref, lb_ref,
             dcv_ref, dw_ref, dcb_ref, dlg_ref, dlb_ref, ext_h, ext_d):
        i = pl.program_id(0)
        lg, lb = lg_ref[...], lb_ref[...]
        a, g = a_ref[...].astype(F32), g_ref[...].astype(F32)
        sg = _sig(g)
        ext_h[pl.ds(hp, tm), :] = a * sg
        hgp = ap_ref[pl.ds(tm - hp, hp), :].astype(F32) * _sig(gp_ref[pl.ds(tm - hp, hp), :].astype(F32))
        ext_h[pl.ds(0, hp), :] = jnp.where(i > 0, hgp, 0.0)
        dh, dhn, xh = ln_bwd(d_ref[...], h_ref[...], lg, lb)
        ext_d[pl.ds(0, tm), :] = dh
        dh_n, _, _ = ln_bwd(dn_ref[pl.ds(0, hp), :], hn_ref[pl.ds(0, hp), :], lg, lb)
        ext_d[pl.ds(tm, hp), :] = jnp.where(i < nt - 1, dh_n, 0.0)

        @pl.when(i == 0)
        def _():
            dw_ref[...] = jnp.zeros_like(dw_ref)
            dcb_ref[...] = jnp.zeros_like(dcb_ref)
            dlg_ref[...] = jnp.zeros_like(dlg_ref)
            dlb_ref[...] = jnp.zeros_like(dlb_ref)

        dcb_ref[...] += _colsum(dh)
        dlg_ref[...] += _colsum(dhn * xh)
        dlb_ref[...] += _colsum(dhn)
        dhg = jnp.zeros((tm, c), F32)
        for j in range(CONV_WIDTH):
            dhg = dhg + w_ref[j:j + 1, :] * ext_d[pl.ds((CONV_WIDTH - 1 - j) * bl, tm), :]
            dw_ref[j:j + 1, :] += _colsum(dh * ext_h[pl.ds(j * bl, tm), :])
        dcv_ref[...] = jnp.concatenate([dhg * sg, dhg * a * sg * (1.0 - sg)], axis=1).astype(dcv_ref.dtype)

    prev = lambda i, k: (jnp.maximum(i - 1, 0), k)
    nxt = lambda i: (jnp.minimum(i + 1, nt - 1), 0)
    cur = lambda i: (i, 0)
    par = lambda arr: pl.BlockSpec(arr.shape, lambda i: (0, 0))
    acc = lambda r: pl.BlockSpec((r, c), lambda i: (0, 0))
    in_specs = [pl.BlockSpec((tm, c), functools.partial(prev, k=cb)), pl.BlockSpec((tm, c), functools.partial(prev, k=cb + 1)),
                pl.BlockSpec((tm, c), lambda i: (i, cb)), pl.BlockSpec((tm, c), lambda i: (i, cb + 1)),
                pl.BlockSpec((tm, c), cur), pl.BlockSpec((tm, c), nxt), pl.BlockSpec((tm, c), cur), pl.BlockSpec((tm, c), nxt),
                par(w32), par(ln_g), par(ln_b)]
    out_shape = [jax.ShapeDtypeStruct((n, 2 * c), BF16), jax.ShapeDtypeStruct((32, c), F32)] + [jax.ShapeDtypeStruct((1, c), F32)] * 3
    return _pcall(body, name, (nt,), in_specs, [pl.BlockSpec((tm, 2 * c), cur), acc(32), acc(1), acc(1), acc(1)], out_shape,
                  [pltpu.VMEM((hp + tm, c), F32), pltpu.VMEM((hp + tm, c), F32)], ("arbitrary",),
                  [proj, proj, proj, proj, dhc, dhc, hconv, hconv, w32, ln_g, ln_b], comm)


SSM_CH = 128
_GELU_C = 0.7978845608028654


def _gelu(x):
    return 0.5 * x * (1.0 + jnp.tanh(_GELU_C * (x + 0.044715 * x * x * x)))


def _gelu_grad(x):
    th = jnp.tanh(_GELU_C * (x + 0.044715 * x * x * x))
    return 0.5 * (1.0 + th) + 0.5 * x * (1.0 - th * th) * (_GELU_C * (1.0 + 3.0 * 0.044715 * x * x))


def _ssm_disc(lam_re, lam_im, log_dt, b_re, b_im):
    dt = jnp.exp(log_dt)[:, None]
    mag = jnp.exp(lam_re * dt)
    ab_re = mag * jnp.cos(lam_im * dt)
    ab_im = mag * jnp.sin(lam_im * dt)
    nr, ni = ab_re - 1.0, ab_im
    den = lam_re * lam_re + lam_im * lam_im
    z_re = ((nr * lam_re + ni * lam_im) / den)[..., None]
    z_im = ((ni * lam_re - nr * lam_im) / den)[..., None]
    return ab_re, ab_im, z_re * b_re - z_im * b_im, z_re * b_im + z_im * b_re


def _ssm_pack(ab_re, ab_im, bb_re, bb_im, c_re, c_im):
    g, p, h = bb_re.shape
    gc = SSM_CH // h
    nc = g // gc
    eye = jnp.eye(gc, dtype=F32)
    blk = lambda x: jnp.einsum("qgph,gk->qghkp", x.reshape(nc, gc, p, h), eye).reshape(nc, gc * h, gc * p)
    bbd = jnp.concatenate([blk(bb_re), blk(bb_im)], axis=2).astype(BF16)
    blc = lambda x: jnp.einsum("qghp,gk->qgpkh", x.reshape(nc, gc, h, p), eye).reshape(nc, gc * p, gc * h)
    cdm = jnp.concatenate([blc(c_re), blc(-c_im)], axis=1).astype(BF16)
    a = jnp.concatenate([ab_re.reshape(nc, gc * p), ab_im.reshape(nc, gc * p)], axis=1)
    a8 = jnp.broadcast_to(a[:, None, :], (nc, 8, 2 * gc * p)).reshape(nc * 8, 2 * gc * p)
    return bbd, cdm, a8


def _ssm_unpack(dbb, dcd, da, g, p, h):
    gc = SSM_CH // h
    nc = g // gc
    ph = gc * p
    eye = jnp.eye(gc, dtype=F32)
    dia = lambda x, o: jnp.einsum("qgpkh,gk->" + o, x.reshape(nc, gc, p, gc, h), eye).reshape((g, p, h) if o == "qgph" else (g, h, p))
    das = da.reshape(nc, 8, 2 * ph).sum(axis=1)
    return (das[:, :ph].reshape(g, p), das[:, ph:].reshape(g, p), dia(dbb[:, :ph], "qgph"), dia(dbb[:, ph:], "qgph"),
            dia(dcd[:, :ph], "qghp"), -dia(dcd[:, ph:], "qghp"))


def _ssm_fwd(proj, bbd, cdm, a8, dskip, bl, name, tm=1024, comm=None):
    n = proj.shape[0]
    nc, ch, p2 = bbd.shape
    ph = p2 // 2
    nt = n // tm
    nsub = 8 // bl

    def body(u_ref, bb_ref, cd_ref, a_ref, d_ref, ypre_ref, yg_ref, s_ref, bu, carry):
        t = pl.program_id(1)

        @pl.when(t == 0)
        def _():
            carry[...] = jnp.zeros_like(carry)

        u = u_ref[...]
        bu[...] = jnp.dot(u, bb_ref[0], preferred_element_type=F32)
        a_re, a_im = a_ref[:, :ph], a_ref[:, ph:]
        row = lax.broadcasted_iota(jnp.int32, (8, ph), 0)

        def step(k, c):
            cre, cim = c
            r0 = pl.multiple_of(k * 8, 8)
            bre, bim = bu[pl.ds(r0, 8), :ph], bu[pl.ds(r0, 8), ph:]
            sre, sim = cre, cim
            for sub in range(nsub):
                xre, xim = pltpu.roll(cre, bl, 0), pltpu.roll(cim, bl, 0)
                cre = a_re * xre - a_im * xim + bre
                cim = a_re * xim + a_im * xre + bim
                if sub == 0:
                    sre, sim = cre, cim
                else:
                    sel = row >= sub * bl
                    sre, sim = jnp.where(sel, cre, sre), jnp.where(sel, cim, sim)
            bu[pl.ds(r0, 8), :ph] = sre
            bu[pl.ds(r0, 8), ph:] = sim
            return sre, sim

        cre, cim = lax.fori_loop(0, tm // 8, step, (carry[:, :ph], carry[:, ph:]))
        carry[:, :ph] = cre
        carry[:, ph:] = cim
        s16 = bu[...].astype(BF16)
        s_ref[...] = s16
        y = jnp.dot(s16, cd_ref[0], preferred_element_type=F32) + d_ref[...] * u.astype(F32)
        ypre_ref[...] = y
        yg_ref[...] = _gelu(y).astype(yg_ref.dtype)

    in_specs = [pl.BlockSpec((tm, ch), lambda q, t: (t, q)), pl.BlockSpec((1, ch, p2), lambda q, t: (q, 0, 0)),
                pl.BlockSpec((1, p2, ch), lambda q, t: (q, 0, 0)), pl.BlockSpec((8, p2), lambda q, t: (q, 0)),
                pl.BlockSpec((1, ch), lambda q, t: (0, q))]
    out_specs = [pl.BlockSpec((tm, ch), lambda q, t: (t, q)), pl.BlockSpec((tm, ch), lambda q, t: (t, q)),
                 pl.BlockSpec((tm, p2), lambda q, t: (t, q))]
    out_shape = [jax.ShapeDtypeStruct((n, nc * ch), F32), jax.ShapeDtypeStruct((n, nc * ch), BF16),
                 jax.ShapeDtypeStruct((n, nc * p2), BF16)]
    return _pcall(body, name, (nc, nt), in_specs, out_specs, out_shape, [pltpu.VMEM((tm, p2), F32), pltpu.VMEM((8, p2), F32)],
                  ("parallel", "arbitrary"), [proj, bbd, cdm, a8, dskip], comm)


def _ssm_bwd(dyg, ypre, proj, s_all, cdt, bbt, a8, dskip, bl, name, tm=1024, comm=None):
    n = proj.shape[0]
    nc, ch, p2 = cdt.shape
    ph = p2 // 2
    nt = n // tm
    nsub = 8 // bl
    tn_dims = (((0,), (0,)), ((), ()))

    def body(dyg_ref, ypre_ref, u_ref, s_ref, cdt_ref, bbt_ref, a_ref, d_ref,
             du_ref, dbb_ref, dcd_ref, da_ref, dd_ref, ds, s32, carry):
        t = pl.program_id(1)

        @pl.when(t == 0)
        def _():
            carry[...] = jnp.zeros_like(carry)
            dbb_ref[...] = jnp.zeros_like(dbb_ref)
            dcd_ref[...] = jnp.zeros_like(dcd_ref)
            da_ref[...] = jnp.zeros_like(da_ref)
            dd_ref[...] = jnp.zeros_like(dd_ref)

        dyp = dyg_ref[...].astype(F32) * _gelu_grad(ypre_ref[...])
        u = u_ref[...]
        dd_ref[...] += _colsum(dyp * u.astype(F32))
        dyp16 = dyp.astype(BF16)
        ds[...] = jnp.dot(dyp16, cdt_ref[0], preferred_element_type=F32)
        s16 = s_ref[...]
        s32[...] = s16.astype(F32)
        a_re, a_im = a_ref[:, :ph], a_ref[:, ph:]
        row = lax.broadcasted_iota(jnp.int32, (8, ph), 0)
        back = 8 - bl

        def step(kk, c):
            lre, lim, acr, aci = c
            r0 = pl.multiple_of((tm // 8 - 1 - kk) * 8, 8)
            dre, dim = ds[pl.ds(r0, 8), :ph], ds[pl.ds(r0, 8), ph:]
            sre, sim = s32[pl.ds(r0, 8), :ph], s32[pl.ds(r0, 8), ph:]
            ore, oim, ire, iim = lre, lim, lre, lim
            for sub in range(nsub - 1, -1, -1):
                xre, xim = pltpu.roll(lre, back, 0), pltpu.roll(lim, back, 0)
                lre = a_re * xre + a_im * xim + dre
                lim = a_re * xim - a_im * xre + dim
                if sub == nsub - 1:
                    ore, oim, ire, iim = lre, lim, xre, xim
                else:
                    sel = row < (sub + 1) * bl
                    ore, oim = jnp.where(sel, lre, ore), jnp.where(sel, lim, oim)
                    ire, iim = jnp.where(sel, xre, ire), jnp.where(sel, xim, iim)
            ds[pl.ds(r0, 8), :ph] = ore
            ds[pl.ds(r0, 8), ph:] = oim
            acr = acr + sre * ire + sim * iim
            aci = aci + sre * iim - sim * ire
            return ore, oim, acr, aci

        z = jnp.zeros((8, ph), F32)
        lre, lim, acr, aci = lax.fori_loop(0, tm // 8, step, (carry[:, :ph], carry[:, ph:], z, z))
        carry[:, :ph] = lre
        carry[:, ph:] = lim
        da_ref[:, :ph] += acr
        da_ref[:, ph:] += aci
        lam16 = ds[...].astype(BF16)
        du = jnp.dot(lam16, bbt_ref[0], preferred_element_type=F32) + d_ref[...] * dyp
        du_ref[...] = du.astype(du_ref.dtype)
        dbb_ref[0] += lax.dot_general(lam16, u, tn_dims, preferred_element_type=F32)
        dcd_ref[0] += lax.dot_general(s16, dyp16, tn_dims, preferred_element_type=F32)

    rev = lambda q, t: (nt - 1 - t, q)
    in_specs = [pl.BlockSpec((tm, ch), rev), pl.BlockSpec((tm, ch), rev), pl.BlockSpec((tm, ch), rev),
                pl.BlockSpec((tm, p2), rev), pl.BlockSpec((1, ch, p2), lambda q, t: (q, 0, 0)),
                pl.BlockSpec((1, p2, ch), lambda q, t: (q, 0, 0)), pl.BlockSpec((8, p2), lambda q, t: (q, 0)),
                pl.BlockSpec((1, ch), lambda q, t: (0, q))]
    out_specs = [pl.BlockSpec((tm, ch), rev), pl.BlockSpec((1, p2, ch), lambda q, t: (q, 0, 0)),
                 pl.BlockSpec((1, p2, ch), lambda q, t: (q, 0, 0)), pl.BlockSpec((8, p2), lambda q, t: (q, 0)),
                 pl.BlockSpec((1, ch), lambda q, t: (0, q))]
    out_shape = [jax.ShapeDtypeStruct((n, nc * ch), BF16), jax.ShapeDtypeStruct((nc, p2, ch), F32),
                 jax.ShapeDtypeStruct((nc, p2, ch), F32), jax.ShapeDtypeStruct((nc * 8, p2), F32),
                 jax.ShapeDtypeStruct((1, nc * ch), F32)]
    return _pcall(body, name, (nc, nt), in_specs, out_specs, out_shape,
                  [pltpu.VMEM((tm, p2), F32), pltpu.VMEM((tm, p2), F32), pltpu.VMEM((8, p2), F32)],
                  ("parallel", "arbitrary"), [dyg, ypre, proj, s_all, cdt, bbt, a8, dskip], comm)


_MESH = pl.DeviceIdType.MESH
_HBM = pl.BlockSpec(memory_space=pltpu.HBM)


def _position():
    return lax.axis_index("x"), lax.axis_index("y"), lax.axis_index("c")


def _other_chips(x, y):
    return [((1 - x, y), 2 * (1 - x) + y), ((x, 1 - y), 2 * x + 1 - y), ((1 - x, 1 - y), 2 * (1 - x) + 1 - y)]


def _swap_sibling(v, name):
    def body(v_ref, got_ref, send_sem, recv_sem):
        x, y, c = _position()
        cp = pltpu.make_async_remote_copy(src_ref=v_ref, dst_ref=got_ref, send_sem=send_sem, recv_sem=recv_sem,
                                          device_id=(x, y, 1 - c), device_id_type=_MESH)
        cp.start()
        cp.wait()

    return pl.pallas_call(
        body, name=name, in_specs=[_HBM], out_specs=_HBM, out_shape=jax.ShapeDtypeStruct(v.shape, v.dtype),
        scratch_shapes=[pltpu.SemaphoreType.DMA, pltpu.SemaphoreType.DMA])(v)


def _own_slot(gathered, own):
    return lax.dynamic_update_index_in_dim(gathered, own, _chip_index(), 0)


def _chip_allgather(v, name):
    def body(v_ref, out_ref, send_sems, recv_sems):
        x, y, c = _position()
        me = 2 * x + y
        sends = []
        for k, (chip, idx) in enumerate(_other_chips(x, y)):
            cp = pltpu.make_async_remote_copy(src_ref=v_ref, dst_ref=out_ref.at[me], send_sem=send_sems.at[k],
                                              recv_sem=recv_sems.at[k], device_id=(*chip, c), device_id_type=_MESH)
            cp.start()
            sends.append(cp)
        for k, (chip, idx) in enumerate(_other_chips(x, y)):
            pltpu.make_async_remote_copy(src_ref=v_ref, dst_ref=out_ref.at[idx], send_sem=send_sems.at[k],
                                         recv_sem=recv_sems.at[k], device_id=(*chip, c), device_id_type=_MESH).wait_recv()
        for cp in sends:
            cp.wait_send()

    out = pl.pallas_call(
        body, name=name, in_specs=[_HBM], out_specs=_HBM, out_shape=jax.ShapeDtypeStruct((4,) + tuple(v.shape), v.dtype),
        scratch_shapes=[pltpu.SemaphoreType.DMA((3,)), pltpu.SemaphoreType.DMA((3,))])(v)
    return _own_slot(out, v)


def _remote(src, dst, send_sems, recv_sems, s, device):
    return pltpu.make_async_remote_copy(src_ref=src, dst_ref=dst, send_sem=send_sems.at[s], recv_sem=recv_sems.at[s],
                                        device_id=device, device_id_type=_MESH)


class _Exchange:
    def __init__(self, ins, out_shapes, n_sems, aliases=None):
        self.ins, self.out_shapes, self.n_sems, self.aliases = list(ins), list(out_shapes), n_sems, aliases or {}

    def sem_shapes(self):
        return [pltpu.SemaphoreType.DMA((self.n_sems,)), pltpu.SemaphoreType.DMA((self.n_sems,))]


def _halves(ref, c, axis=0):
    h = ref.shape[axis] // 2
    idx = (slice(None),) * axis
    return ref.at[idx + (pl.ds(c * h, h),)], ref.at[idx + (pl.ds((1 - c) * h, h),)]


class _GatherShards(_Exchange):
    def __init__(self, ws):
        super().__init__(ws, [jax.ShapeDtypeStruct((N_CHIPS,) + tuple(w.shape), w.dtype) for w in ws], 6 * len(ws))

    def start(self, w_refs, out_refs, sems):
        send_sems, recv_sems = sems
        x, y, c = _position()
        me = 2 * x + y
        for i, (w, out) in enumerate(zip(w_refs, out_refs)):
            for k, (chip, idx) in enumerate(_other_chips(x, y)):
                _remote(_halves(w, c)[0], _halves(out.at[me], c)[0], send_sems, recv_sems, 6 * i + k, (*chip, c)).start()

    def finish(self, w_refs, out_refs, sems):
        send_sems, recv_sems = sems
        x, y, c = _position()
        sibling = (x, y, 1 - c)
        others = _other_chips(x, y)
        for i, out in enumerate(out_refs):
            for k, (chip, idx) in enumerate(others):
                landed = _halves(out.at[idx], c)[0]
                _remote(landed, landed, send_sems, recv_sems, 6 * i + k, (*chip, c)).wait_recv()
                _remote(landed, landed, send_sems, recv_sems, 6 * i + 3 + k, sibling).start()
        for i, (w, out) in enumerate(zip(w_refs, out_refs)):
            for k, (chip, idx) in enumerate(others):
                mine, theirs = _halves(out.at[idx], c)
                _remote(theirs, theirs, send_sems, recv_sems, 6 * i + 3 + k, sibling).wait_recv()
                _remote(mine, mine, send_sems, recv_sems, 6 * i + 3 + k, sibling).wait_send()
                _remote(_halves(w, c)[0], mine, send_sems, recv_sems, 6 * i + k, (*chip, c)).wait_send()


class _SwapHalves(_Exchange):
    def __init__(self, gs):
        shapes = [jax.ShapeDtypeStruct((g.shape[0], g.shape[1] // 2) + tuple(g.shape[2:]), g.dtype) for g in gs]
        super().__init__(gs, shapes, N_CHIPS * len(gs))

    def _copies(self, g_refs, out_refs, sems):
        x, y, c = _position()
        return [_remote(_halves(g.at[j], c)[1], out.at[j], sems[0], sems[1], N_CHIPS * i + j, (x, y, 1 - c))
                for i, (g, out) in enumerate(zip(g_refs, out_refs)) for j in range(N_CHIPS)]

    def start(self, g_refs, out_refs, sems):
        for cp in self._copies(g_refs, out_refs, sems):
            cp.start()

    def finish(self, g_refs, out_refs, sems):
        for cp in self._copies(g_refs, out_refs, sems):
            cp.wait()


class _ScatterPieces(_Exchange):
    def __init__(self, ps):
        super().__init__(ps, [jax.ShapeDtypeStruct((3,) + tuple(p.shape[1:]), p.dtype) for p in ps], 3 * len(ps))

    def _copies(self, p_refs, out_refs, sems):
        x, y, c = _position()
        return [_remote(p.at[idx], out.at[k], sems[0], sems[1], 3 * i + k, (*chip, c))
                for i, (p, out) in enumerate(zip(p_refs, out_refs)) for k, (chip, idx) in enumerate(_other_chips(x, y))]

    def start(self, p_refs, out_refs, sems):
        for cp in self._copies(p_refs, out_refs, sems):
            cp.start()

    def finish(self, p_refs, out_refs, sems):
        for cp in self._copies(p_refs, out_refs, sems):
            cp.wait()


class _JoinHalves(_Exchange):
    def __init__(self, rs):
        super().__init__(rs, [jax.ShapeDtypeStruct(r.shape, r.dtype) for r in rs], len(rs), {i: i for i in range(len(rs))})

    def start(self, r_refs, out_refs, sems):
        x, y, c = _position()
        for i, out in enumerate(out_refs):
            _remote(out.at[c], out.at[c], sems[0], sems[1], i, (x, y, 1 - c)).start()

    def finish(self, r_refs, out_refs, sems):
        x, y, c = _position()
        for i, out in enumerate(out_refs):
            _remote(out.at[c], out.at[c], sems[0], sems[1], i, (x, y, 1 - c)).wait_send()
            _remote(out.at[1 - c], out.at[1 - c], sems[0], sems[1], i, (x, y, 1 - c)).wait_recv()


def _run_exchange(ex, name):
    def body(*refs):
        ins, outs, sems = refs[:len(ex.ins)], refs[len(ex.ins):len(ex.ins) + len(ex.out_shapes)], refs[-2:]
        ex.start(ins, outs, sems)
        ex.finish(ins, outs, sems)

    return pl.pallas_call(body, name=name, in_specs=[_HBM] * len(ex.ins), out_specs=[_HBM] * len(ex.out_shapes),
                          out_shape=ex.out_shapes, scratch_shapes=ex.sem_shapes(), input_output_aliases=ex.aliases)(*ex.ins)


def _pcall(body, name, grid, in_specs, out_specs, out_shape, scratch_shapes, semantics, args, comm=None):
    if comm is None:
        return pl.pallas_call(body, name=name, grid=grid, in_specs=in_specs, out_specs=out_specs, out_shape=out_shape,
                              scratch_shapes=scratch_shapes, compiler_params=_cparams(semantics))(*args)
    n_in, n_out, n_scr, ci, co = len(in_specs), len(out_specs), len(scratch_shapes), len(comm.ins), len(comm.out_shapes)

    def wrapped(*refs):
        parts, a = [], 0
        for k in (n_in, ci, n_out, co, n_scr, 2):
            parts.append(refs[a:a + k])
            a += k
        ins, cins, outs, couts, scr, sems = parts
        ids = [pl.program_id(i) for i in range(len(grid))]
        first = functools.reduce(jnp.logical_and, [i == 0 for i in ids])
        last = functools.reduce(jnp.logical_and, [i == g - 1 for i, g in zip(ids, grid)])

        @pl.when(first)
        def _():
            comm.start(cins, couts, sems)

        body(*ins, *outs, *scr)

        @pl.when(last)
        def _():
            comm.finish(cins, couts, sems)

    res = pl.pallas_call(
        wrapped, name=name, grid=grid, in_specs=list(in_specs) + [_HBM] * ci, out_specs=list(out_specs) + [_HBM] * co,
        out_shape=list(out_shape) + comm.out_shapes, scratch_shapes=list(scratch_shapes) + comm.sem_shapes(),
        compiler_params=_cparams(("arbitrary",) * len(grid)))(*args, *comm.ins)
    return res[:n_out], res[n_out:]


ATT_WINDOW = 128
PHASES = 16
_NT = (((1,), (1,)), ((), ()))
_TN = (((0,), (0,)), ((), ()))


def _to_phase_order(x, bl):
    n, c = x.shape
    g = n // bl // PHASES
    return x.reshape(g, PHASES, bl, c).transpose(2, 1, 0, 3).reshape(n, c)


def _from_phase_order(x, bl):
    n, c = x.shape
    g = n // bl // PHASES
    return x.reshape(bl, PHASES, g, c).transpose(2, 1, 0, 3).reshape(n, c)


def _att_geometry(p, n, bl):
    g = n // bl // PHASES
    if p == 0:
        return ((bl, PHASES, g), (bl, g // 16), (None, PHASES, 16),
                lambda sh: (lambda b, a: (b, 0, jnp.maximum(a + sh, 0))), 256, 16, lambda ids: ids[1] == 0)
    if p == 1:
        return ((bl, 4, 4, g), (bl, 4, g // 32), (None, 4, None, 32),
                lambda sh: (lambda b, r, a: (b, 0, r, jnp.maximum(a + sh, 0))), 128, 32, lambda ids: ids[2] == 0)
    return ((bl * PHASES, g), (bl * PHASES,), (None, g), lambda sh: (lambda s: (s, 0)), g, g, None)


def _att_masks(p, qb, chunk):
    def pos(idx):
        return (idx % chunk) * (qb // chunk) + idx // chunk

    dq = pos(lax.broadcasted_iota(jnp.int32, (qb, qb), 0))
    dk = pos(lax.broadcasted_iota(jnp.int32, (qb, qb), 1))
    dist = dq - dk
    return jnp.logical_and(dist >= 0, dist <= ATT_WINDOW), dist + qb <= ATT_WINDOW


def _att_call(p, n, bl, c, body, name, ins, outs):
    prefix, grid, blk, idx_fn, qb, chunk, _ = _att_geometry(p, n, bl)

    def spec(cb, sh):
        f = idx_fn(sh)
        return pl.BlockSpec(blk + (c,), lambda *ids, f=f, cb=cb: f(*ids) + (cb,))

    in_specs = [spec(cb, sh) for _, cb, sh in ins]
    out_specs = [spec(0, 0) for _ in outs]
    out_shape = [jax.ShapeDtypeStruct(prefix + (c,), dt) for dt in outs]
    res = pl.pallas_call(body, name=name, grid=grid, in_specs=in_specs, out_specs=out_specs, out_shape=out_shape,
                         compiler_params=_cparams(("parallel",) * len(grid)))(*[a.reshape(prefix + (a.shape[1],)) for a, _, _ in ins])
    return [r.reshape(n, c) for r in res]


def _att_fwd(p, qkv, qcb, bl, c, heads):
    n = qkv.shape[0]
    _, grid, _, _, qb, chunk, first_fn = _att_geometry(p, n, bl)
    n_grid = len(grid)
    has_prev = first_fn is not None
    e = c // heads
    scale = e ** -0.5

    def body(*refs):
        if has_prev:
            q_ref, kc_ref, kp_ref, vc_ref, vp_ref, o_ref, l_ref = refs
        else:
            q_ref, kc_ref, vc_ref, o_ref, l_ref = refs
        ids = [pl.program_id(a) for a in range(n_grid)]
        mc, mp = _att_masks(p, qb, chunk)
        if has_prev:
            mp = jnp.logical_and(mp, jnp.logical_not(first_fn(ids)))
        lo = lax.broadcasted_iota(jnp.int32, (qb, 128), 1) < e
        shp = o_ref.shape[:-1]
        ones = jnp.ones((qb, 128), BF16)
        n_t = c // 128
        load = lambda r, t: r[..., pl.ds(t * 128, 128)].reshape(qb, 128)
        items = [(t, h) for t in range(n_t) for h in range(2)]
        dot = functools.partial(jnp.dot, preferred_element_type=F32)
        q2 = [load(q_ref, t) for t in range(n_t)]
        kc = [load(kc_ref, t) for t in range(n_t)]
        qm = [jnp.where(lo if h == 0 else jnp.logical_not(lo), q2[t], jnp.zeros_like(q2[t])) for t, h in items]
        sc = [jnp.where(mc, lax.dot_general(qm[i], kc[t], _NT, preferred_element_type=F32) * scale, -jnp.inf)
              for i, (t, h) in enumerate(items)]
        m = [jnp.max(s, axis=1, keepdims=True) for s in sc]
        if has_prev:
            kp = [load(kp_ref, t) for t in range(n_t)]
            sp = [jnp.where(mp, lax.dot_general(qm[i], kp[t], _NT, preferred_element_type=F32) * scale, -jnp.inf)
                  for i, (t, h) in enumerate(items)]
            m = [jnp.maximum(a, jnp.max(s, axis=1, keepdims=True)) for a, s in zip(m, sp)]
        pc = [jnp.exp(s - a).astype(BF16) for s, a in zip(sc, m)]
        vc = [load(vc_ref, t) for t in range(n_t)]
        acc = [dot(pc[i], vc[t]) for i, (t, h) in enumerate(items)]
        den = [dot(x, ones) for x in pc]
        if has_prev:
            pp = [jnp.exp(s - a).astype(BF16) for s, a in zip(sp, m)]
            vp = [load(vp_ref, t) for t in range(n_t)]
            acc = [a + dot(pp[i], vp[t]) for i, ((t, h), a) in enumerate(zip(items, acc))]
            den = [d + dot(x, ones) for d, x in zip(den, pp)]
        oh = [a / d for a, d in zip(acc, den)]
        lh = [a + jnp.log(d) for a, d in zip(m, den)]
        for t in range(n_t):
            ls = pl.ds(t * 128, 128)
            o_ref[..., ls] = jnp.where(lo, oh[2 * t], oh[2 * t + 1]).astype(o_ref.dtype).reshape(shp + (128,))
            l_ref[..., ls] = jnp.where(lo, lh[2 * t], lh[2 * t + 1]).reshape(shp + (128,))

    kcb, vcb = 3, 4
    ins = [(qkv, qcb, 0), (qkv, kcb, 0)] + ([(qkv, kcb, -1)] if has_prev else []) + [(qkv, vcb, 0)] + ([(qkv, vcb, -1)] if has_prev else [])
    return _att_call(p, n, bl, c, body, name=f"att_fwd{p}", ins=ins, outs=[BF16, F32])


def _att_bwd(p, qkv, qcb, o, do, lse, dlse, bl, c, heads):
    n = qkv.shape[0]
    _, grid, _, _, qb, chunk, first_fn = _att_geometry(p, n, bl)
    n_grid = len(grid)
    has_prev = first_fn is not None
    e = c // heads
    scale = e ** -0.5

    def body(*refs):
        if has_prev:
            q_ref, kc_ref, kp_ref, vc_ref, vp_ref, o_ref, do_ref, l_ref, dl_ref, dq_ref, dkc_ref, dkp_ref, dvc_ref, dvp_ref = refs
        else:
            q_ref, kc_ref, vc_ref, o_ref, do_ref, l_ref, dl_ref, dq_ref, dkc_ref, dvc_ref = refs
        ids = [pl.program_id(a) for a in range(n_grid)]
        mc, mp = _att_masks(p, qb, chunk)
        if has_prev:
            mp = jnp.logical_and(mp, jnp.logical_not(first_fn(ids)))
        lo = lax.broadcasted_iota(jnp.int32, (qb, 128), 1) < e
        shp = o_ref.shape[:-1]
        n_t = c // 128
        load = lambda r, t: r[..., pl.ds(t * 128, 128)].reshape(qb, 128)
        items = [(t, h) for t in range(n_t) for h in range(2)]
        nt_dot = lambda a, b: lax.dot_general(a, b, _NT, preferred_element_type=F32)
        tn_dot = lambda a, b: lax.dot_general(a, b, _TN, preferred_element_type=F32)
        dot = functools.partial(jnp.dot, preferred_element_type=F32)

        def store(r, t, v):
            r[..., pl.ds(t * 128, 128)] = v.astype(r.dtype).reshape(shp + (128,))

        sel = [lo if h == 0 else jnp.logical_not(lo) for t, h in items]
        q2, kc, vc, do2 = ([load(r, t) for t in range(n_t)] for r in (q_ref, kc_ref, vc_ref, do_ref))
        qm = [jnp.where(sel[i], q2[t], jnp.zeros_like(q2[t])) for i, (t, h) in enumerate(items)]
        dom = [jnp.where(sel[i], do2[t], jnp.zeros_like(do2[t])) for i, (t, h) in enumerate(items)]
        dod = [do2[t].astype(F32) * load(o_ref, t).astype(F32) for t in range(n_t)]
        lcol = [load(l_ref, t)[:, h * e:h * e + 1] for t, h in items]
        corr = [load(dl_ref, t)[:, h * e:h * e + 1] - jnp.sum(jnp.where(sel[i], dod[t], 0.0), axis=1, keepdims=True)
                for i, (t, h) in enumerate(items)]
        pc = [jnp.exp(jnp.where(mc, nt_dot(qm[i], kc[t]) * scale, -jnp.inf) - lcol[i]) for i, (t, h) in enumerate(items)]
        dsc = [(pc[i] * (nt_dot(dom[i], vc[t]) + corr[i]) * scale).astype(BF16) for i, (t, h) in enumerate(items)]
        pc = [x.astype(BF16) for x in pc]
        dq = [dot(dsc[i], kc[t]) for i, (t, h) in enumerate(items)]
        dkc = [tn_dot(dsc[2 * t], qm[2 * t]) + tn_dot(dsc[2 * t + 1], qm[2 * t + 1]) for t in range(n_t)]
        dvc = [tn_dot(pc[2 * t], dom[2 * t]) + tn_dot(pc[2 * t + 1], dom[2 * t + 1]) for t in range(n_t)]
        if has_prev:
            kp, vp = ([load(r, t) for t in range(n_t)] for r in (kp_ref, vp_ref))
            pp = [jnp.exp(jnp.where(mp, nt_dot(qm[i], kp[t]) * scale, -jnp.inf) - lcol[i]) for i, (t, h) in enumerate(items)]
            dsp = [(pp[i] * (nt_dot(dom[i], vp[t]) + corr[i]) * scale).astype(BF16) for i, (t, h) in enumerate(items)]
            pp = [x.astype(BF16) for x in pp]
            dq = [a + dot(dsp[i], kp[t]) for i, ((t, h), a) in enumerate(zip(items, dq))]
            dkp = [tn_dot(dsp[2 * t], qm[2 * t]) + tn_dot(dsp[2 * t + 1], qm[2 * t + 1]) for t in range(n_t)]
            dvp = [tn_dot(pp[2 * t], dom[2 * t]) + tn_dot(pp[2 * t + 1], dom[2 * t + 1]) for t in range(n_t)]
        for t in range(n_t):
            store(dq_ref, t, jnp.where(lo, dq[2 * t], dq[2 * t + 1]))
            store(dkc_ref, t, dkc[t])
            store(dvc_ref, t, dvc[t])
            if has_prev:
                store(dkp_ref, t, dkp[t])
                store(dvp_ref, t, dvp[t])

    kcb, vcb = 3, 4
    ins = [(qkv, qcb, 0), (qkv, kcb, 0)] + ([(qkv, kcb, -1)] if has_prev else []) + [(qkv, vcb, 0)] + ([(qkv, vcb, -1)] if has_prev else [])
    ins += [(o, 0, 0), (do, 0, 0), (lse, 0, 0), (dlse, 0, 0)]
    res = _att_call(p, n, bl, c, body, name=f"att_bwd{p}", ins=ins, outs=[BF16] * (5 if has_prev else 3))
    if has_prev:
        dq, dkc, dkp, dvc, dvp = res
        return dq, dkc, dkp, dvc, dvp
    dq, dkc, dvc = res
    return dq, dkc, None, dvc, None


def _att_fold_prev(p, cur, prv, bl):
    if prv is None:
        return cur.astype(F32)
    n, c = cur.shape
    prefix, _, _, _, _, chunk, _ = _att_geometry(p, n, bl)
    v = prv.reshape(prefix + (c,)).astype(F32)
    shifted = jnp.concatenate([v[..., chunk:, :], jnp.zeros_like(v[..., :chunk, :])], axis=-2)
    return cur.astype(F32) + shifted.reshape(n, c)


def _attention_fwd2(qkv_tb, bl, heads):
    n, w = qkv_tb.shape
    c = w // 5
    qkv = _to_phase_order(qkv_tb, bl)
    outs = [_att_fwd(p, qkv, p, bl, c, heads) for p in range(3)]
    ins = [("row", o, c, 0) for o, _ in outs] + [("row", l, c, 0) for _, l in outs]
    o, = _rowwise(_combine_fwd_fn, "comb_fwd", n, ins, [(c, BF16)])
    return _from_phase_order(o, bl), (qkv, outs)


def _attention_bwd2(do_tb, saved, bl, heads):
    qkv, outs = saved
    n, c = do_tb.shape
    e = c // heads
    lane = jnp.arange(c) // e
    jmat = (lane[:, None] == lane[None, :]).astype(F32)
    do = _to_phase_order(do_tb, bl)
    ins = [("row", o, c, 0) for o, _ in outs] + [("row", l, c, 0) for _, l in outs] + [("row", do, c, 0), ("par", jmat)]
    res = _rowwise(_combine_bwd_fn, "comb_bwd", n, ins, [(c, BF16)] * 3 + [(c, F32)] * 3)
    dqs, dk, dv = [], 0.0, 0.0
    for p in range(3):
        dq, dkc, dkp, dvc, dvp = _att_bwd(p, qkv, p, outs[p][0], res[p], outs[p][1], res[3 + p], bl, c, heads)
        dqs.append(dq)
        dk = dk + _att_fold_prev(p, dkc, dkp, bl)
        dv = dv + _att_fold_prev(p, dvc, dvp, bl)
    dqkv = jnp.concatenate(dqs + [dk.astype(BF16), dv.astype(BF16)], axis=1)
    return _from_phase_order(dqkv, bl)


ATT_HEADS = 8
SSM_GROUPS, SSM_STATE, SSM_GROUP = 32, 64, 16


def _row(v):
    return v.reshape(1, -1)


def _carried(result, carry, key, hidden):
    if carry.get(key) is None:
        return result
    result, hidden[key] = result
    return result


def _layer_fwd(x, w, p, bl, carry, late=None):
    n, d = x.shape
    c = d // 2
    hidden = {}
    h, = _rowwise(_rms_fwd_fn, "rms_fwd", n, [("row", x, d, 0), ("par", _row(p["norm1_g"]))], [(d, BF16)])
    proj = _carried(_mm(h, w["w_in"], "nn", BF16, "mm_in", comm=carry.get("mm_in")), carry, "mm_in", hidden)
    if late is not None:
        w = dict(w, **late(hidden["mm_in"]))
    disc, disc_vjp = jax.vjp(_ssm_disc, p["ssm_lambda_re"], p["ssm_lambda_im"], p["ssm_log_dt"], p["ssm_b_re"], p["ssm_b_im"])
    bbd, cdm, a8 = _ssm_pack(*disc, p["ssm_c_re"], p["ssm_c_im"])
    ypre, yg, s_all = _carried(_ssm_fwd(proj, bbd, cdm, a8, _row(p["ssm_d"]), bl, "ssm_fwd", comm=carry.get("ssm_fwd")),
                               carry, "ssm_fwd", hidden)
    zs = _mm(yg, w["w_ssm_glu"], "nn", BF16, "mm_glu")
    o, att = _attention_fwd2(proj[:, c:6 * c], bl, ATT_HEADS)
    ya = _mm(o, w["w_att_up"], "nn", BF16, "mm_att")
    w32 = jnp.concatenate([p["conv_w"], jnp.zeros((1, c), F32)], axis=0)
    hc, hconv = _conv_fwd(proj, 6, w32, _row(p["conv_b"]), _row(p["conv_ln_g"]), _row(p["conv_ln_b"]), bl, c, "conv_fwd")
    yc = _mm(hc, w["w_conv_pw2"], "nn", BF16, "mm_pw2")
    gates = [("row", proj, d, 4), ("row", proj, d, 5), ("row", proj, d, 6), ("par", _row(p["b_gate"]))]
    branches = [("row", zs, 2 * d, 0), ("row", ya, d, 0), ("row", yc, d, 0)]
    merged, = _rowwise(_merge_fwd_fn, "merge_fwd", n, gates + branches, [(d, BF16)])
    xm = _mm(merged, w["w_out"], "nn", F32, "mm_out", res=x)
    h2, = _rowwise(_rms_fwd_fn, "rms_fwd", n, [("row", xm, d, 0), ("par", _row(p["norm2_g"]))], [(d, BF16)])
    z = _carried(_mm(h2, w["w_ffn_in"], "nn", BF16, "mm_ffn_in", comm=carry.get("mm_ffn_in")), carry, "mm_ffn_in", hidden)
    f = z.shape[1] // 2
    a, = _rowwise(_swiglu_fwd_fn, "swiglu_fwd", n, [("row", z, 2 * f, 0)], [(f, BF16)], tm=256)
    xo = _mm(a, w["w_ffn_out"], "nn", F32, "mm_ffn_out", res=xm)
    saved = dict(x=x, h=h, proj=proj, disc_vjp=disc_vjp, bbd=bbd, cdm=cdm, a8=a8, ypre=ypre, yg=yg, s_all=s_all, zs=zs, o=o,
                 att=att, ya=ya, w32=w32, hc=hc, hconv=hconv, yc=yc, gates=gates, branches=branches, merged=merged, xm=xm,
                 h2=h2, z=z, a=a)
    return xo, saved, hidden, w


def _layer_bwd(dxo, s, w, p, bl, carry):
    n, d = dxo.shape
    c = d // 2
    g, bufs, hidden = {}, {}, {}
    f = s["a"].shape[1]

    def dw(key, a, dy, name):
        bufs[key] = _mm_dw(a, dy, name, 1 if key in ROW_SHARDED else N_CHIPS)

    da = _mm(dxo, w["w_ffn_out"], "nt", BF16, "mm_ffn_out_dx")
    dw("w_ffn_out", s["a"], dxo, "mm_ffn_out_dw")
    dz, = _rowwise(_swiglu_bwd_fn, "swiglu_bwd", n, [("row", s["z"], 2 * f, 0), ("row", da, f, 0)], [(2 * f, BF16)], tm=256)
    dh2 = _mm(dz, w["w_ffn_in"], "nt", F32, "mm_ffn_in_dx")
    dw("w_ffn_in", s["h2"], dz, "mm_ffn_in_dw")
    dxm, dg2 = _rowwise(_rms_bwd_fn, "rms_bwd", n, [("row", s["xm"], d, 0), ("par", _row(p["norm2_g"])), ("row", dh2, d, 0),
                                                   ("row", dxo, d, 0)], [(d, F32)], [d])
    g["norm2_g"] = dg2[0]
    dmerged = _mm(dxm, w["w_out"], "nt", BF16, "mm_out_dx")
    dw("w_out", s["merged"], dxm, "mm_out_dw")
    dgl, dzs, dya, dyc, dbg = _rowwise(_merge_bwd_fn, "merge_bwd", n, s["gates"] + s["branches"] + [("row", dmerged, d, 0)],
                                       [(3 * d, BF16), (2 * d, BF16), (d, BF16), (d, BF16)], [3 * d], tm=256)
    g["b_gate"] = dbg[0]
    dyg = _mm(dzs, w["w_ssm_glu"], "nt", BF16, "mm_glu_dx")
    dw("w_ssm_glu", s["yg"], dzs, "mm_glu_dw")
    du, dbb, dcd, dab, dd = _carried(
        _ssm_bwd(dyg, s["ypre"], s["proj"], s["s_all"], s["cdm"].transpose(0, 2, 1), s["bbd"].transpose(0, 2, 1), s["a8"],
                 _row(p["ssm_d"]), bl, "ssm_bwd", comm=carry.get("ssm_bwd")), carry, "ssm_bwd", hidden)
    dab_re, dab_im, dbb_re, dbb_im, g["ssm_c_re"], g["ssm_c_im"] = _ssm_unpack(dbb, dcd, dab, SSM_GROUPS, SSM_STATE, SSM_GROUP)
    (g["ssm_lambda_re"], g["ssm_lambda_im"], g["ssm_log_dt"], g["ssm_b_re"],
     g["ssm_b_im"]) = s["disc_vjp"]((dab_re, dab_im, dbb_re, dbb_im))
    g["ssm_d"] = dd[0]
    do = _mm(dya, w["w_att_up"], "nt", BF16, "mm_att_dx")
    dw("w_att_up", s["o"], dya, "mm_att_dw")
    dqkv = _attention_bwd2(do, s["att"], bl, ATT_HEADS)
    dhc = _mm(dyc, w["w_conv_pw2"], "nt", BF16, "mm_pw2_dx")
    dw("w_conv_pw2", s["hc"], dyc, "mm_pw2_dw")
    dcv, dcw, dcb, dlg, dlb = _carried(
        _conv_bwd(s["proj"], 6, dhc, s["hconv"], s["w32"], _row(p["conv_ln_g"]), _row(p["conv_ln_b"]), bl, c, "conv_bwd",
                  comm=carry.get("conv_bwd")), carry, "conv_bwd", hidden)
    g["conv_w"], g["conv_b"], g["conv_ln_g"], g["conv_ln_b"] = dcw, dcb[0], dlg[0], dlb[0]
    dproj = jnp.concatenate([du, dqkv, dcv, dgl], axis=1)
    dh = _mm(dproj, w["w_in"], "nt", F32, "mm_in_dx")
    dw("w_in", s["h"], dproj, "mm_in_dw")
    dx, dg1 = _rowwise(_rms_bwd_fn, "rms_bwd", n, [("row", s["x"], d, 0), ("par", _row(p["norm1_g"])), ("row", dh, d, 0),
                                                  ("row", dxm, d, 0)], [(d, F32)], [d])
    g["norm1_g"] = dg1[0]
    return dx, g, bufs, hidden


WEIGHTS = ['norm1_g', 'w_in', 'b_gate', 'ssm_lambda_re', 'ssm_lambda_im', 'ssm_log_dt', 'ssm_b_re', 'ssm_b_im', 'ssm_c_re',
           'ssm_c_im', 'ssm_d', 'w_ssm_glu', 'w_att_up', 'conv_w', 'conv_b', 'conv_ln_g', 'conv_ln_b', 'w_conv_pw2', 'w_out',
           'norm2_g', 'w_ffn_in', 'w_ffn_out', 'final_g']
BIG = ['w_in', 'w_ssm_glu', 'w_att_up', 'w_conv_pw2', 'w_out', 'w_ffn_in', 'w_ffn_out']
ROW_SHARDED = ('w_out', 'w_ffn_out')
SMALL = [k for k in WEIGHTS if k not in BIG]
LANES = 1024
N_CHIPS = 4
ROW_TILE_BYTES = 36 * 1024 * 1024
MIN_SHARD_TILE = 1024


def _pad_rows(a, rows):
    return jnp.concatenate([a, jnp.zeros((rows - a.shape[0],) + a.shape[1:], a.dtype)], axis=0) if rows > a.shape[0] else a


def _row_tile(rows, width, n_arrays):
    best = 16
    for t in range(16, rows + 1, 16):
        if rows % t == 0 and t * width * 4 * n_arrays * 2 <= ROW_TILE_BYTES:
            best = t
    return best


def _flat_fn(fn, name, ins, n_out, rows):
    return _rowwise(fn, name, rows, [("row", a, LANES, 0) for a in ins], [(LANES, F32)] * n_out, tm=rows)


def _reduce_prepare(bufs):
    landed = _run_exchange(_SwapHalves([b16 for _, b16 in bufs]), "rs_swap")
    p32s, p16s = [], []
    for (b32, _), la in zip(bufs, landed):
        s, m, cs = b32.shape
        h = m // 2
        tm = _row_tile(h, cs, 4)

        def body(g_ref, l_ref, o32, o16):
            r = g_ref[...] + l_ref[...].astype(F32)
            o32[...] = r
            o16[...] = r.astype(BF16)

        piece = pl.BlockSpec((None, tm, cs), lambda j, i: (j, i, 0))
        mine = pl.BlockSpec((None, None, tm, cs), lambda j, i: (j, _core_index(), i, 0))
        p32, p16 = pl.pallas_call(
            body, name="rs_add", grid=(s, h // tm), in_specs=[mine, piece], out_specs=[piece, piece],
            out_shape=[jax.ShapeDtypeStruct((s, h, cs), F32), jax.ShapeDtypeStruct((s, h, cs), BF16)],
            compiler_params=_cparams(("parallel", "parallel")))(b32.reshape(s, 2, h, cs), la)
        p32s.append(p32)
        p16s.append(p16)
    return p32s, p16s


def _reduce_finish(p32s, arrived):
    reduced = []
    for p32, lb in zip(p32s, arrived):
        _, h, cs = lb.shape
        tm = _row_tile(h, cs, 5)

        def body(p_ref, a_ref, b_ref, c_ref, o_ref):
            o_ref[...] = ((p_ref[...] + a_ref[...].astype(F32)) + b_ref[...].astype(F32)) + c_ref[...].astype(F32)

        mine = pl.BlockSpec((None, tm, cs), lambda i: (_chip_index(), i, 0))
        other = [pl.BlockSpec((None, tm, cs), lambda i, k=k: (k, i, 0)) for k in range(3)]
        half = pl.BlockSpec((None, tm, cs), lambda i: (_core_index(), i, 0))
        reduced.append(pl.pallas_call(
            body, name="rs_sum", grid=(h // tm,), in_specs=[mine] + other, out_specs=half,
            out_shape=jax.ShapeDtypeStruct((2, h, cs), F32), compiler_params=_cparams(("parallel",)))(p32, lb, lb, lb))
    joined = _run_exchange(_JoinHalves(reduced), "rs_gather")
    return [j.reshape(2 * j.shape[1], j.shape[2]) for j in joined]


def _adamw_layers(w, g_layers, m, v):
    depth, rows, cs = w.shape
    tm = _row_tile(rows, cs, 8)
    nb = rows // tm

    def body(*refs):
        w_ref, m_ref, v_ref = refs[:3]
        g_refs = refs[3:3 + depth]
        go_ref, d_ref, mo_ref, vo_ref = refs[3 + depth:]
        layer = pl.program_id(0)
        g = g_refs[0][...]
        for l in range(1, depth):
            g = jnp.where(layer == l, g_refs[l][...], g)
        delta, mo, vo = _adamw_fn(w_ref[...], g, m_ref[...], v_ref[...])
        go_ref[...], d_ref[...], mo_ref[...], vo_ref[...] = g, delta, mo, vo

    stacked = pl.BlockSpec((None, tm, cs), lambda l, i: (l, i, 0))
    g_specs = [pl.BlockSpec((tm, cs), lambda l, i, k=k: (jnp.where(l == k, i, jnp.where(l < k, 0, nb - 1)), 0)) for k in range(depth)]
    return pl.pallas_call(
        body, name="adamw", grid=(depth, nb), in_specs=[stacked] * 3 + g_specs, out_specs=[stacked] * 4,
        out_shape=[jax.ShapeDtypeStruct(w.shape, F32)] * 4, compiler_params=_cparams(("arbitrary", "arbitrary")))(w, m, v, *g_layers)


def _sum4_fn(a, b, c, d):
    return (((a.astype(F32) + b.astype(F32)) + c.astype(F32)) + d.astype(F32),)


def _add2_fn(a, b):
    return (a + b,)


def kernel(x, norm1_g, w_in, b_gate, ssm_lambda_re, ssm_lambda_im, ssm_log_dt, ssm_b_re, ssm_b_im, ssm_c_re, ssm_c_im, ssm_d, w_ssm_glu, w_att_up, conv_w, conv_b, conv_ln_g, conv_ln_b, w_conv_pw2, w_out, norm2_g, w_ffn_in, w_ffn_out, final_g, loss_target, m_norm1_g, m_w_in, m_b_gate, m_ssm_lambda_re, m_ssm_lambda_im, m_ssm_log_dt, m_ssm_b_re, m_ssm_b_im, m_ssm_c_re, m_ssm_c_im, m_ssm_d, m_w_ssm_glu, m_w_att_up, m_conv_w, m_conv_b, m_conv_ln_g, m_conv_ln_b, m_w_conv_pw2, m_w_out, m_norm2_g, m_w_ffn_in, m_w_ffn_out, m_final_g, v_norm1_g, v_w_in, v_b_gate, v_ssm_lambda_re, v_ssm_lambda_im, v_ssm_log_dt, v_ssm_b_re, v_ssm_b_im, v_ssm_c_re, v_ssm_c_im, v_ssm_d, v_w_ssm_glu, v_w_att_up, v_conv_w, v_conv_b, v_conv_ln_g, v_conv_ln_b, v_w_conv_pw2, v_w_out, v_norm2_g, v_w_ffn_in, v_w_ffn_out, v_final_g):
    args = dict(locals())
    wts = {k: args[k] for k in WEIGHTS}
    mom = {k: args["m_" + k] for k in WEIGHTS}
    var = {k: args["v_" + k] for k in WEIGHTS}
    bl, seq, d = x.shape
    n = bl * seq
    depth = norm1_g.shape[0]
    cx, cy, cc = _position()
    me = 2 * cx + cy

    assert depth == 2, "the exchanges of layer 1 are hidden behind layer 0's kernels"
    first = BIG[:1]
    rest = BIG[1:]

    shards = lambda keys, l: [wts[k][l].astype(BF16) for k in keys]

    def whole(keys, gathered):
        out = {}
        for k, a in zip(keys, gathered):
            _, ks, cs = a.shape
            if k in ROW_SHARDED:
                out[k] = a.reshape(N_CHIPS * ks, cs)
            elif cs < MIN_SHARD_TILE:
                out[k] = a.transpose(1, 0, 2).reshape(ks, N_CHIPS * cs)
            else:
                out[k] = a
        return out

    fill = lambda gathered, own: [_own_slot(g, o) for g, o in zip(gathered, own)]
    own0 = shards(first, 0) + [conv_w]
    gathered = fill(_run_exchange(_GatherShards(own0), "gather_weights"), own0)
    conv_full = gathered[-1].transpose(1, 2, 0, 3).reshape(depth, CONV_WIDTH, -1)
    params = lambda l: dict({k: wts[k][l] for k in SMALL if k not in ("final_g", "conv_w")}, conv_w=conv_full[l])

    to_rows = lambda t: t.transpose(1, 0, 2).reshape(n, d)
    own = {"mm_in": shards(rest, 0), "ssm_fwd": shards(first, 1), "mm_ffn_in": shards(rest, 1)}
    xs, s0, hidden, w0 = _layer_fwd(to_rows(x), whole(first, gathered[:-1]), params(0), bl, {k: _GatherShards(v) for k, v in own.items()},
                                    late=lambda got: whole(rest, fill(got, own["mm_in"])))
    w1 = dict(whole(first, fill(hidden["ssm_fwd"], own["ssm_fwd"])), **whole(rest, fill(hidden["mm_ffn_in"], own["mm_ffn_in"])))
    full = [w0, w1]
    xs, s1, _, _ = _layer_fwd(xs, full[1], params(1), bl, {})
    dx, sq, dgf = _rowwise(_loss_fn, "loss_head", n, [("row", xs, d, 0), ("par", _row(final_g)), ("row", to_rows(loss_target), d, 0)],
                           [(d, F32)], [d, d])
    loss = lax.psum(0.5 * jnp.sum(sq) / d, ("x", "y", "c"))

    pieces = lambda bufs, keys: [tuple(b.reshape(N_CHIPS, -1, b.shape[-1]) for b in bufs[k]) for k in keys]
    dx, g1, bufs1, _ = _layer_bwd(dx, s1, full[1], params(1), bl, {})
    p32_1, p16_1 = _reduce_prepare(pieces(bufs1, BIG))
    dx, g0, bufs0, hidden = _layer_bwd(dx, s0, full[0], params(0), bl,
                                       {"ssm_bwd": _ScatterPieces(p16_1[:1]), "conv_bwd": _ScatterPieces(p16_1[1:])})
    red1 = _reduce_finish(p32_1, list(hidden["ssm_bwd"]) + list(hidden["conv_bwd"]))
    p32_0, p16_0 = _reduce_prepare(pieces(bufs0, BIG))
    red0 = _reduce_finish(p32_0, _run_exchange(_ScatterPieces(p16_0), "rs_scatter"))
    grads = {"final_g": dgf[0]}
    for k in SMALL:
        if k != "final_g":
            grads[k] = jnp.stack([g0[k], g1[k]])
    grad_x = dx.reshape(seq, bl, d).transpose(1, 0, 2)
    outs = {}
    for k, r0, r1 in zip(BIG, red0, red1):
        for tag, a in zip(("grad", "delta", "m", "v"), _adamw_layers(wts[k], [r0, r1], mom[k], var[k])):
            outs[tag, k] = a

    def flat1(t):
        v = jnp.concatenate([t[k].reshape(-1) for k in SMALL])
        rows = -(-v.size // (8 * LANES)) * 8
        return _pad_rows(v, rows * LANES).reshape(rows, LANES), rows

    def unflat1(flat, shapes):
        out, off, v = {}, 0, flat.reshape(-1)
        for k in SMALL:
            size = math.prod(shapes[k])
            out[k] = v[off:off + size].reshape(shapes[k])
            off += size
        return out

    grads["conv_w"] = grads["conv_w"][:, :CONV_WIDTH]
    gs, rows = flat1(grads)
    chip_sum, = _flat_fn(_add2_fn, "ar_add", [gs, _swap_sibling(gs, "ar_swap")], 1, rows)
    slots = _chip_allgather(chip_sum, "ar_gather")
    gs_red, = _flat_fn(_sum4_fn, "ar_sum", [slots[j] for j in range(N_CHIPS)], 1, rows)
    g_sm = unflat1(gs_red, {k: grads[k].shape for k in SMALL})
    cs = conv_w.shape[2]
    g_sm["conv_w"] = lax.dynamic_slice_in_dim(g_sm["conv_w"], me * cs, cs, axis=2)
    (w1, rows), (g1, _), (m1, _), (v1, _) = flat1(wts), flat1(g_sm), flat1(mom), flat1(var)
    sm_out = _flat_fn(_adamw_fn, "adamw_small", [w1, g1, m1, v1], 3, rows)
    shapes = {k: wts[k].shape for k in SMALL}
    for tag, a in zip(("delta", "m", "v"), sm_out):
        for k, t in unflat1(a, shapes).items():
            outs[tag, k] = t
    for k in SMALL:
        outs["grad", k] = g_sm[k]
    return (loss, grad_x, *[outs["grad", k] for k in WEIGHTS], *[outs["delta", k] for k in WEIGHTS],
            *[outs["m", k] for k in WEIGHTS], *[outs["v", k] for k in WEIGHTS])
```

```python
import functools
import math

import jax
import jax.numpy as jnp
from jax import lax
from jax.experimental import pallas as pl
from jax.experimental.pallas import tpu as pltpu

F32 = jnp.float32
BF16 = jnp.bfloat16
VMEM_LIMIT = 56 * 1024 * 1024


def _cparams(sem):
    return pltpu.CompilerParams(dimension_semantics=sem, vmem_limit_bytes=VMEM_LIMIT)


_DIMS = {"nn": (((1,), (0,)), ((), ())), "nt": (((1,), (1,)), ((), ())), "tn": (((0,), (0,)), ((), ()))}


MM_ROWS = 1024
MM_DW_INPUT_ELEMS = 2 * 1024 * 1024


def _div_tile(n, cap):
    best = None
    for t in range(128, min(n, cap) + 1, 128):
        if n % t == 0:
            best = t
    return best or n


def _mm(a, b, form, out_dtype, name, res=None, comm=None):
    sharded = b.ndim == 3
    kdim, cs = b.shape[-2], b.shape[-1]
    s = b.shape[0] if sharded else 1
    m = a.shape[0]
    tm = MM_ROWS if m % MM_ROWS == 0 else _div_tile(m, MM_ROWS)
    if form == "nn":
        n, kd = s * cs, kdim
        tn, tk = _div_tile(cs, 1792), _div_tile(kdim, 2048)
        per = cs // tn
        b_blk = (tk, tn)
        b_idx = (lambda i, j, k: (j // per, k, j % per)) if sharded else (lambda i, j, k: (k, j))
    else:
        n, kd = kdim, s * cs
        tn, tk = _div_tile(kdim, 1408), _div_tile(cs, 1792)
        per = cs // tk
        b_blk = (tn, tk)
        b_idx = (lambda i, j, k: (k // per, j, k % per)) if sharded else (lambda i, j, k: (j, k))
    nk = kd // tk
    a_spec = pl.BlockSpec((tm, tk), lambda i, j, k: (i, k))
    b_spec = pl.BlockSpec(((None,) + b_blk) if sharded else b_blk, b_idx)
    o_spec = pl.BlockSpec((tm, tn), lambda i, j, k: (i, j))
    dims = _DIMS[form]

    def body(*refs):
        a_ref, b_ref = refs[:2]
        r_ref = refs[2] if res is not None else None
        o_ref = refs[3] if res is not None else refs[2]
        p = lax.dot_general(a_ref[...].astype(BF16), b_ref[...], dims, preferred_element_type=F32)

        def finish(r):
            if r_ref is not None:
                r = r + r_ref[...]
            o_ref[...] = r.astype(out_dtype)

        if nk == 1:
            finish(p)
            return
        acc = refs[-1]
        k = pl.program_id(2)

        @pl.when(k == 0)
        def _():
            acc[...] = p

        @pl.when(k > 0)
        def _():
            acc[...] += p

        @pl.when(k == nk - 1)
        def _():
            finish(acc[...])

    ins = [a, b] + ([] if res is None else [res])
    in_specs = [a_spec, b_spec] + ([] if res is None else [o_spec])
    out = _pcall(body, name, (m // tm, n // tn, nk), in_specs, [o_spec], [jax.ShapeDtypeStruct((m, n), out_dtype)],
                 [pltpu.VMEM((tm, tn), F32)] if nk > 1 else [], ("parallel", "parallel", "arbitrary"), ins, comm)
    return out[0] if comm is None else (out[0][0], out[1])


def _mm_dw(a, dy, name, shards):
    r, m = a.shape
    c = dy.shape[1]
    cs = c // shards
    tm, tn = _div_tile(m, 1408), _div_tile(cs, 1792)
    tk = _div_tile(r, max(512, min(2048, MM_DW_INPUT_ELEMS // (tm + tn))))
    per = cs // tn
    nk = r // tk

    def body(a_ref, b_ref, o32, o16, acc):
        k = pl.program_id(2)
        p = lax.dot_general(a_ref[...].astype(BF16), b_ref[...].astype(BF16), _DIMS["tn"], preferred_element_type=F32)

        @pl.when(k == 0)
        def _():
            acc[...] = p

        @pl.when(k > 0)
        def _():
            acc[...] += p

        @pl.when(k == nk - 1)
        def _():
            o32[...] = acc[...]
            o16[...] = acc[...].astype(BF16)

    o_spec = pl.BlockSpec((None, tm, tn), lambda i, j, k: (j // per, i, j % per))
    shape = (shards, m, cs)
    in_specs = [pl.BlockSpec((tk, tm), lambda i, j, k: (k, i)), pl.BlockSpec((tk, tn), lambda i, j, k: (k, j))]
    return _pcall(body, name, (m // tm, c // tn, nk), in_specs, [o_spec, o_spec],
                  [jax.ShapeDtypeStruct(shape, F32), jax.ShapeDtypeStruct(shape, BF16)], [pltpu.VMEM((tm, tn), F32)],
                  ("parallel", "parallel", "arbitrary"), [a, dy])


def _core_index():
    return lax.axis_index("c")


def _chip_index():
    return 2 * lax.axis_index("x") + lax.axis_index("y")


def _rowwise(fn, name, n_rows, ins, outs, accs=(), tm=512):
    n_in, n_out, n_acc = len(ins), len(outs), len(accs)
    in_specs, args = [], []
    for spec in ins:
        if spec[0] == "row":
            _, arr, w, cb = spec
            in_specs.append(pl.BlockSpec((tm, w), lambda i, cb=cb: (i, cb)))
        elif spec[0] == "rowoff":
            _, arr, w, cb, index_fn, span = spec
            in_specs.append(pl.BlockSpec((tm, w), lambda i, cb=cb, index_fn=index_fn, nb=span // tm: (index_fn() * nb + i, cb)))
        elif spec[0] == "rowblk":
            _, arr, w, cb, start = spec
            in_specs.append(pl.BlockSpec((tm, w), lambda i, cb=cb, nb=start // tm: (nb + i, cb)))
        else:
            arr = spec[1]
            in_specs.append(pl.BlockSpec(arr.shape, lambda i: (0, 0)))
        args.append(arr)
    out_specs = [pl.BlockSpec((tm, w), lambda i: (i, 0)) for w, _ in outs]
    out_specs += [pl.BlockSpec((1, w), lambda i: (0, 0)) for w in accs]
    out_shape = [jax.ShapeDtypeStruct((n_rows, w), dt) for w, dt in outs]
    out_shape += [jax.ShapeDtypeStruct((1, w), F32) for w in accs]

    def body(*refs):
        i = pl.program_id(0)
        res = fn(*[r[...] for r in refs[:n_in]])
        for o_ref, r in zip(refs[n_in:n_in + n_out], res[:n_out]):
            o_ref[...] = r.astype(o_ref.dtype)
        for a_ref, r in zip(refs[n_in + n_out:], res[n_out:]):
            @pl.when(i == 0)
            def _(a_ref=a_ref, r=r):
                a_ref[...] = r

            @pl.when(i > 0)
            def _(a_ref=a_ref, r=r):
                a_ref[...] += r

    return pl.pallas_call(
        body, name=name, grid=(n_rows // tm,), in_specs=in_specs, out_specs=out_specs, out_shape=out_shape,
        compiler_params=_cparams(("arbitrary",)))(*args)


EPS = 1e-6


def _sig(x):
    return 1.0 / (1.0 + jnp.exp(-x))


def _colsum(x):
    return jnp.sum(x, axis=0, keepdims=True)


def _rms_fwd_fn(x, g):
    r = lax.rsqrt(jnp.mean(x * x, axis=-1, keepdims=True) + EPS)
    return (x * r * g,)


def _rms_bwd_fn(x, g, dh, dres):
    dh = dh.astype(F32)
    r = lax.rsqrt(jnp.mean(x * x, axis=-1, keepdims=True) + EPS)
    xh = x * r
    dyg = dh * g
    dx = r * (dyg - xh * jnp.mean(dyg * xh, axis=-1, keepdims=True)) + dres
    return dx, _colsum(dh * xh)


def _loss_fn(x, g, t):
    d = x.shape[-1]
    r = lax.rsqrt(jnp.mean(x * x, axis=-1, keepdims=True) + EPS)
    xh = x * r
    err = xh * g - t
    dy = err * (1.0 / d)
    dyg = dy * g
    dx = r * (dyg - xh * jnp.mean(dyg * xh, axis=-1, keepdims=True))
    return dx, _colsum(err * err), _colsum(dy * xh)


def _swiglu_fwd_fn(z):
    f = z.shape[-1] // 2
    z1, z2 = z[:, :f].astype(F32), z[:, f:].astype(F32)
    return (z1 * _sig(z1) * z2,)


def _swiglu_bwd_fn(z, da):
    f = z.shape[-1] // 2
    z1, z2, da = z[:, :f].astype(F32), z[:, f:].astype(F32), da.astype(F32)
    s = _sig(z1)
    dz1 = da * z2 * (s * (1.0 + z1 * (1.0 - s)))
    dz2 = da * (z1 * s)
    return (jnp.concatenate([dz1, dz2], axis=1),)


def _merge_fwd_fn(g0, g1, g2, bg, zs, ya, yc):
    d = ya.shape[-1]
    bg = bg.astype(F32)
    zs = zs.astype(F32)
    ys = zs[:, :d] * _sig(zs[:, d:])
    m = _sig(g0.astype(F32) + bg[:, :d]) * ys
    m = m + _sig(g1.astype(F32) + bg[:, d:2 * d]) * ya.astype(F32)
    m = m + _sig(g2.astype(F32) + bg[:, 2 * d:]) * yc.astype(F32)
    return (m,)


def _merge_bwd_fn(g0, g1, g2, bg, zs, ya, yc, dm):
    d = ya.shape[-1]
    bg = bg.astype(F32)
    zs = zs.astype(F32)
    dm = dm.astype(F32)
    z1, s2 = zs[:, :d], _sig(zs[:, d:])
    ys = z1 * s2
    s0 = _sig(g0.astype(F32) + bg[:, :d])
    s1 = _sig(g1.astype(F32) + bg[:, d:2 * d])
    s3 = _sig(g2.astype(F32) + bg[:, 2 * d:])
    dgl = jnp.concatenate([dm * ys * s0 * (1.0 - s0), dm * ya.astype(F32) * s1 * (1.0 - s1),
                           dm * yc.astype(F32) * s3 * (1.0 - s3)], axis=1)
    dys = dm * s0
    dzs = jnp.concatenate([dys * s2, dys * z1 * s2 * (1.0 - s2)], axis=1)
    return dgl, dzs, dm * s1, dm * s3, _colsum(dgl)


def _combine_fwd_fn(o0, o1, o2, l0, l1, l2):
    m = jnp.maximum(jnp.maximum(l0, l1), l2)
    e0, e1, e2 = jnp.exp(l0 - m), jnp.exp(l1 - m), jnp.exp(l2 - m)
    inv = 1.0 / (e0 + e1 + e2)
    return ((e0 * o0.astype(F32) + e1 * o1.astype(F32) + e2 * o2.astype(F32)) * inv,)


def _combine_bwd_fn(o0, o1, o2, l0, l1, l2, do, jmat):
    m = jnp.maximum(jnp.maximum(l0, l1), l2)
    e0, e1, e2 = jnp.exp(l0 - m), jnp.exp(l1 - m), jnp.exp(l2 - m)
    inv = 1.0 / (e0 + e1 + e2)
    w0, w1, w2 = e0 * inv, e1 * inv, e2 * inv
    do = do.astype(F32)

    def headsum(x):
        return jnp.dot(x, jmat, preferred_element_type=F32, precision=lax.Precision.HIGHEST)

    dw0, dw1, dw2 = headsum(do * o0.astype(F32)), headsum(do * o1.astype(F32)), headsum(do * o2.astype(F32))
    mean = w0 * dw0 + w1 * dw1 + w2 * dw2
    return w0 * do, w1 * do, w2 * do, w0 * (dw0 - mean), w1 * (dw1 - mean), w2 * (dw2 - mean)


ADAM_LR, ADAM_B1, ADAM_B2, ADAM_EPS, ADAM_WD, ADAM_STEP = 0.001, 0.9, 0.999, 1e-08, 0.01, 10


def _adamw_fn(w, g, m, v):
    m = ADAM_B1 * m + (1.0 - ADAM_B1) * g
    v = ADAM_B2 * v + (1.0 - ADAM_B2) * (g * g)
    m_hat = m / (1.0 - ADAM_B1 ** ADAM_STEP)
    v_hat = v / (1.0 - ADAM_B2 ** ADAM_STEP)
    delta = -ADAM_LR * (m_hat / (jnp.sqrt(v_hat) + ADAM_EPS) + ADAM_WD * w)
    return delta, m, v


CONV_WIDTH = 31


def _conv_fwd(proj, cb, w32, conv_b, ln_g, ln_b, bl, c, name, tm=512):
    n = proj.shape[0]
    hp = (CONV_WIDTH - 1) * bl
    nt = n // tm

    def body(ap_ref, gp_ref, a_ref, g_ref, w_ref, cb_ref, lg_ref, lb_ref, hc_ref, hconv_ref, ext):
        i = pl.program_id(0)
        ext[pl.ds(hp, tm), :] = a_ref[...].astype(F32) * _sig(g_ref[...].astype(F32))
        hgp = ap_ref[pl.ds(tm - hp, hp), :].astype(F32) * _sig(gp_ref[pl.ds(tm - hp, hp), :].astype(F32))
        ext[pl.ds(0, hp), :] = jnp.where(i > 0, hgp, 0.0)
        acc = jnp.zeros((tm, c), F32) + cb_ref[...]
        for j in range(CONV_WIDTH):
            acc = acc + w_ref[j:j + 1, :] * ext[pl.ds(j * bl, tm), :]
        hconv_ref[...] = acc.astype(hconv_ref.dtype)
        h = hconv_ref[...].astype(F32)
        mu = jnp.mean(h, axis=-1, keepdims=True)
        xc = h - mu
        var = jnp.mean(xc * xc, axis=-1, keepdims=True)
        hn = xc * lax.rsqrt(var + EPS) * lg_ref[...] + lb_ref[...]
        hc_ref[...] = (hn * _sig(hn)).astype(hc_ref.dtype)

    prev = lambda i, k: (jnp.maximum(i - 1, 0), k)
    par = lambda arr: pl.BlockSpec(arr.shape, lambda i: (0, 0))
    return pl.pallas_call(
        body, name=name, grid=(nt,),
        in_specs=[pl.BlockSpec((tm, c), functools.partial(prev, k=cb)), pl.BlockSpec((tm, c), functools.partial(prev, k=cb + 1)),
                  pl.BlockSpec((tm, c), lambda i: (i, cb)), pl.BlockSpec((tm, c), lambda i: (i, cb + 1)),
                  par(w32), par(conv_b), par(ln_g), par(ln_b)],
        out_specs=[pl.BlockSpec((tm, c), lambda i: (i, 0))] * 2,
        out_shape=[jax.ShapeDtypeStruct((n, c), BF16)] * 2,
        scratch_shapes=[pltpu.VMEM((hp + tm, c), F32)],
        compiler_params=_cparams(("arbitrary",)))(proj, proj, proj, proj, w32, conv_b, ln_g, ln_b)


def _conv_bwd(proj, cb, dhc, hconv, w32, ln_g, ln_b, bl, c, name, tm=512, comm=None):
    n = proj.shape[0]
    hp = (CONV_WIDTH - 1) * bl
    nt = n // tm

    def ln_bwd(d, h, lg, lb):
        d, h = d.astype(F32), h.astype(F32)
        mu = jnp.mean(h, axis=-1, keepdims=True)
        xc = h - mu
        rstd = lax.rsqrt(jnp.mean(xc * xc, axis=-1, keepdims=True) + EPS)
        xh = xc * rstd
        hn = xh * lg + lb
        s = _sig(hn)
        dhn = d * (s * (1.0 + hn * (1.0 - s)))
        dxh = dhn * lg
        dh = rstd * (dxh - jnp.mean(dxh, axis=-1, keepdims=True) - xh * jnp.mean(dxh * xh, axis=-1, keepdims=True))
        return dh, dhn, xh

    def body(ap_ref, gp_ref, a_ref, g_ref, d_ref, dn_ref, h_ref, hn_ref, w_ref, lg_ref, lb_ref,
             dcv_ref, dw_ref, dcb_ref, dlg_ref, dlb_ref, ext_h, ext_d):
        i = pl.program_id(0)
        lg, lb = lg_ref[...], lb_ref[...]
        a, g = a_ref[...].astype(F32), g_ref[...].astype(F32)
        sg = _sig(g)
        ext_h[pl.ds(hp, tm), :] = a * sg
        hgp = ap_ref[pl.ds(tm - hp, hp), :].astype(F32) * _sig(gp_ref[pl.ds(tm - hp, hp), :].astype(F32))
        ext_h[pl.ds(0, hp), :] = jnp.where(i > 0, hgp, 0.0)
        dh, dhn, xh = ln_bwd(d_ref[...], h_ref[...], lg, lb)
        ext_d[pl.ds(0, tm), :] = dh
        dh_n, _, _ = ln_bwd(dn_ref[pl.ds(0, hp), :], hn_ref[pl.ds(0, hp), :], lg, lb)
        ext_d[pl.ds(tm, hp), :] = jnp.where(i < nt - 1, dh_n, 0.0)

        @pl.when(i == 0)
        def _():
            dw_ref[...] = jnp.zeros_like(dw_ref)
            dcb_ref[...] = jnp.zeros_like(dcb_ref)
            dlg_ref[...] = jnp.zeros_like(dlg_ref)
            dlb_ref[...] = jnp.zeros_like(dlb_ref)

        dcb_ref[...] += _colsum(dh)
        dlg_ref[...] += _colsum(dhn * xh)
        dlb_ref[...] += _colsum(dhn)
        dhg = jnp.zeros((tm, c), F32)
        for j in range(CONV_WIDTH):
            dhg = dhg + w_ref[j:j + 1, :] * ext_d[pl.ds((CONV_WIDTH - 1 - j) * bl, tm), :]
            dw_ref[j:j + 1, :] += _colsum(dh * ext_h[pl.ds(j * bl, tm), :])
        dcv_ref[...] = jnp.concatenate([dhg * sg, dhg * a * sg * (1.0 - sg)], axis=1).astype(dcv_ref.dtype)

    prev = lambda i, k: (jnp.maximum(i - 1, 0), k)
    nxt = lambda i: (jnp.minimum(i + 1, nt - 1), 0)
    cur = lambda i: (i, 0)
    par = lambda arr: pl.BlockSpec(arr.shape, lambda i: (0, 0))
    acc = lambda r: pl.BlockSpec((r, c), lambda i: (0, 0))
    in_specs = [pl.BlockSpec((tm, c), functools.partial(prev, k=cb)), pl.BlockSpec((tm, c), functools.partial(prev, k=cb + 1)),
                pl.BlockSpec((tm, c), lambda i: (i, cb)), pl.BlockSpec((tm, c), lambda i: (i, cb + 1)),
                pl.BlockSpec((tm, c), cur), pl.BlockSpec((tm, c), nxt), pl.BlockSpec((tm, c), cur), pl.BlockSpec((tm, c), nxt),
                par(w32), par(ln_g), par(ln_b)]
    out_shape = [jax.ShapeDtypeStruct((n, 2 * c), BF16), jax.ShapeDtypeStruct((32, c), F32)] + [jax.ShapeDtypeStruct((1, c), F32)] * 3
    return _pcall(body, name, (nt,), in_specs, [pl.BlockSpec((tm, 2 * c), cur), acc(32), acc(1), acc(1), acc(1)], out_shape,
                  [pltpu.VMEM((hp + tm, c), F32), pltpu.VMEM((hp + tm, c), F32)], ("arbitrary",),
                  [proj, proj, proj, proj, dhc, dhc, hconv, hconv, w32, ln_g, ln_b], comm)


SSM_CH = 128
_GELU_C = 0.7978845608028654


def _gelu(x):
    return 0.5 * x * (1.0 + jnp.tanh(_GELU_C * (x + 0.044715 * x * x * x)))


def _gelu_grad(x):
    th = jnp.tanh(_GELU_C * (x + 0.044715 * x * x * x))
    return 0.5 * (1.0 + th) + 0.5 * x * (1.0 - th * th) * (_GELU_C * (1.0 + 3.0 * 0.044715 * x * x))


def _ssm_disc(lam_re, lam_im, log_dt, b_re, b_im):
    dt = jnp.exp(log_dt)[:, None]
    mag = jnp.exp(lam_re * dt)
    ab_re = mag * jnp.cos(lam_im * dt)
    ab_im = mag * jnp.sin(lam_im * dt)
    nr, ni = ab_re - 1.0, ab_im
    den = lam_re * lam_re + lam_im * lam_im
    z_re = ((nr * lam_re + ni * lam_im) / den)[..., None]
    z_im = ((ni * lam_re - nr * lam_im) / den)[..., None]
    return ab_re, ab_im, z_re * b_re - z_im * b_im, z_re * b_im + z_im * b_re


def _ssm_pack(ab_re, ab_im, bb_re, bb_im, c_re, c_im):
    g, p, h = bb_re.shape
    gc = SSM_CH // h
    nc = g // gc
    eye = jnp.eye(gc, dtype=F32)
    blk = lambda x: jnp.einsum("qgph,gk->qghkp", x.reshape(nc, gc, p, h), eye).reshape(nc, gc * h, gc * p)
    bbd = jnp.concatenate([blk(bb_re), blk(bb_im)], axis=2).astype(BF16)
    blc = lambda x: jnp.einsum("qghp,gk->qgpkh", x.reshape(nc, gc, h, p), eye).reshape(nc, gc * p, gc * h)
    cdm = jnp.concatenate([blc(c_re), blc(-c_im)], axis=1).astype(BF16)
    a = jnp.concatenate([ab_re.reshape(nc, gc * p), ab_im.reshape(nc, gc * p)], axis=1)
    a8 = jnp.broadcast_to(a[:, None, :], (nc, 8, 2 * gc * p)).reshape(nc * 8, 2 * gc * p)
    return bbd, cdm, a8


def _ssm_unpack(dbb, dcd, da, g, p, h):
    gc = SSM_CH // h
    nc = g // gc
    ph = gc * p
    eye = jnp.eye(gc, dtype=F32)
    dia = lambda x, o: jnp.einsum("qgpkh,gk->" + o, x.reshape(nc, gc, p, gc, h), eye).reshape((g, p, h) if o == "qgph" else (g, h, p))
    das = da.reshape(nc, 8, 2 * ph).sum(axis=1)
    return (das[:, :ph].reshape(g, p), das[:, ph:].reshape(g, p), dia(dbb[:, :ph], "qgph"), dia(dbb[:, ph:], "qgph"),
            dia(dcd[:, :ph], "qghp"), -dia(dcd[:, ph:], "qghp"))


def _ssm_fwd(proj, bbd, cdm, a8, dskip, bl, name, tm=1024, comm=None):
    n = proj.shape[0]
    nc, ch, p2 = bbd.shape
    ph = p2 // 2
    nt = n // tm
    nsub = 8 // bl

    def body(u_ref, bb_ref, cd_ref, a_ref, d_ref, ypre_ref, yg_ref, s_ref, bu, carry):
        t = pl.program_id(1)

        @pl.when(t == 0)
        def _():
            carry[...] = jnp.zeros_like(carry)

        u = u_ref[...]
        bu[...] = jnp.dot(u, bb_ref[0], preferred_element_type=F32)
        a_re, a_im = a_ref[:, :ph], a_ref[:, ph:]
        row = lax.broadcasted_iota(jnp.int32, (8, ph), 0)

        def step(k, c):
            cre, cim = c
            r0 = pl.multiple_of(k * 8, 8)
            bre, bim = bu[pl.ds(r0, 8), :ph], bu[pl.ds(r0, 8), ph:]
            sre, sim = cre, cim
            for sub in range(nsub):
                xre, xim = pltpu.roll(cre, bl, 0), pltpu.roll(cim, bl, 0)
                cre = a_re * xre - a_im * xim + bre
                cim = a_re * xim + a_im * xre + bim
                if sub == 0:
                    sre, sim = cre, cim
                else:
                    sel = row >= sub * bl
                    sre, sim = jnp.where(sel, cre, sre), jnp.where(sel, cim, sim)
            bu[pl.ds(r0, 8), :ph] = sre
            bu[pl.ds(r0, 8), ph:] = sim
            return sre, sim

        cre, cim = lax.fori_loop(0, tm // 8, step, (carry[:, :ph], carry[:, ph:]))
        carry[:, :ph] = cre
        carry[:, ph:] = cim
        s16 = bu[...].astype(BF16)
        s_ref[...] = s16
        y = jnp.dot(s16, cd_ref[0], preferred_element_type=F32) + d_ref[...] * u.astype(F32)
        ypre_ref[...] = y
        yg_ref[...] = _gelu(y).astype(yg_ref.dtype)

    in_specs = [pl.BlockSpec((tm, ch), lambda q, t: (t, q)), pl.BlockSpec((1, ch, p2), lambda q, t: (q, 0, 0)),
                pl.BlockSpec((1, p2, ch), lambda q, t: (q, 0, 0)), pl.BlockSpec((8, p2), lambda q, t: (q, 0)),
                pl.BlockSpec((1, ch), lambda q, t: (0, q))]
    out_specs = [pl.BlockSpec((tm, ch), lambda q, t: (t, q)), pl.BlockSpec((tm, ch), lambda q, t: (t, q)),
                 pl.BlockSpec((tm, p2), lambda q, t: (t, q))]
    out_shape = [jax.ShapeDtypeStruct((n, nc * ch), F32), jax.ShapeDtypeStruct((n, nc * ch), BF16),
                 jax.ShapeDtypeStruct((n, nc * p2), BF16)]
    return _pcall(body, name, (nc, nt), in_specs, out_specs, out_shape, [pltpu.VMEM((tm, p2), F32), pltpu.VMEM((8, p2), F32)],
                  ("parallel", "arbitrary"), [proj, bbd, cdm, a8, dskip], comm)


def _ssm_bwd(dyg, ypre, proj, s_all, cdt, bbt, a8, dskip, bl, name, tm=1024, comm=None):
    n = proj.shape[0]
    nc, ch, p2 = cdt.shape
    ph = p2 // 2
    nt = n // tm
    nsub = 8 // bl
    tn_dims = (((0,), (0,)), ((), ()))

    def body(dyg_ref, ypre_ref, u_ref, s_ref, cdt_ref, bbt_ref, a_ref, d_ref,
             du_ref, dbb_ref, dcd_ref, da_ref, dd_ref, ds, s32, carry):
        t = pl.program_id(1)

        @pl.when(t == 0)
        def _():
            carry[...] = jnp.zeros_like(carry)
            dbb_ref[...] = jnp.zeros_like(dbb_ref)
            dcd_ref[...] = jnp.zeros_like(dcd_ref)
            da_ref[...] = jnp.zeros_like(da_ref)
            dd_ref[...] = jnp.zeros_like(dd_ref)

        dyp = dyg_ref[...].astype(F32) * _gelu_grad(ypre_ref[...])
        u = u_ref[...]
        dd_ref[...] += _colsum(dyp * u.astype(F32))
        dyp16 = dyp.astype(BF16)
        ds[...] = jnp.dot(dyp16, cdt_ref[0], preferred_element_type=F32)
        s16 = s_ref[...]
        s32[...] = s16.astype(F32)
        a_re, a_im = a_ref[:, :ph], a_ref[:, ph:]
        row = lax.broadcasted_iota(jnp.int32, (8, ph), 0)
        back = 8 - bl

        def step(kk, c):
            lre, lim, acr, aci = c
            r0 = pl.multiple_of((tm // 8 - 1 - kk) * 8, 8)
            dre, dim = ds[pl.ds(r0, 8), :ph], ds[pl.ds(r0, 8), ph:]
            sre, sim = s32[pl.ds(r0, 8), :ph], s32[pl.ds(r0, 8), ph:]
            ore, oim, ire, iim = lre, lim, lre, lim
            for sub in range(nsub - 1, -1, -1):
                xre, xim = pltpu.roll(lre, back, 0), pltpu.roll(lim, back, 0)
                lre = a_re * xre + a_im * xim + dre
                lim = a_re * xim - a_im * xre + dim
                if sub == nsub - 1:
                    ore, oim, ire, iim = lre, lim, xre, xim
                else:
                    sel = row < (sub + 1) * bl
                    ore, oim = jnp.where(sel, lre, ore), jnp.where(sel, lim, oim)
                    ire, iim = jnp.where(sel, xre, ire), jnp.where(sel, xim, iim)
            ds[pl.ds(r0, 8), :ph] = ore
            ds[pl.ds(r0, 8), ph:] = oim
            acr = acr + sre * ire + sim * iim
            aci = aci + sre * iim - sim * ire
            return ore, oim, acr, aci

        z = jnp.zeros((8, ph), F32)
        lre, lim, acr, aci = lax.fori_loop(0, tm // 8, step, (carry[:, :ph], carry[:, ph:], z, z))
        carry[:, :ph] = lre
        carry[:, ph:] = lim
        da_ref[:, :ph] += acr
        da_ref[:, ph:] += aci
        lam16 = ds[...].astype(BF16)
        du = jnp.dot(lam16, bbt_ref[0], preferred_element_type=F32) + d_ref[...] * dyp
        du_ref[...] = du.astype(du_ref.dtype)
        dbb_ref[0] += lax.dot_general(lam16, u, tn_dims, preferred_element_type=F32)
        dcd_ref[0] += lax.dot_general(s16, dyp16, tn_dims, preferred_element_type=F32)

    rev = lambda q, t: (nt - 1 - t, q)
    in_specs = [pl.BlockSpec((tm, ch), rev), pl.BlockSpec((tm, ch), rev), pl.BlockSpec((tm, ch), rev),
                pl.BlockSpec((tm, p2), rev), pl.BlockSpec((1, ch, p2), lambda q, t: (q, 0, 0)),
                pl.BlockSpec((1, p2, ch), lambda q, t: (q, 0, 0)), pl.BlockSpec((8, p2), lambda q, t: (q, 0)),
                pl.BlockSpec((1, ch), lambda q, t: (0, q))]
    out_specs = [pl.BlockSpec((tm, ch), rev), pl.BlockSpec((1, p2, ch), lambda q, t: (q, 0, 0)),
                 pl.BlockSpec((1, p2, ch), lambda q, t: (q, 0, 0)), pl.BlockSpec((8, p2), lambda q, t: (q, 0)),
                 pl.BlockSpec((1, ch), lambda q, t: (0, q))]
    out_shape = [jax.ShapeDtypeStruct((n, nc * ch), BF16), jax.ShapeDtypeStruct((nc, p2, ch), F32),
                 jax.ShapeDtypeStruct((nc, p2, ch), F32), jax.ShapeDtypeStruct((nc * 8, p2), F32),
                 jax.ShapeDtypeStruct((1, nc * ch), F32)]
    return _pcall(body, name, (nc, nt), in_specs, out_specs, out_shape,
                  [pltpu.VMEM((tm, p2), F32), pltpu.VMEM((tm, p2), F32), pltpu.VMEM((8, p2), F32)],
                  ("parallel", "arbitrary"), [dyg, ypre, proj, s_all, cdt, bbt, a8, dskip], comm)


_MESH = pl.DeviceIdType.MESH
_HBM = pl.BlockSpec(memory_space=pltpu.HBM)


def _position():
    return lax.axis_index("x"), lax.axis_index("y"), lax.axis_index("c")


def _other_chips(x, y):
    return [((1 - x, y), 2 * (1 - x) + y), ((x, 1 - y), 2 * x + 1 - y), ((1 - x, 1 - y), 2 * (1 - x) + 1 - y)]


def _swap_sibling(v, name):
    def body(v_ref, got_ref, send_sem, recv_sem):
        x, y, c = _position()
        cp = pltpu.make_async_remote_copy(src_ref=v_ref, dst_ref=got_ref, send_sem=send_sem, recv_sem=recv_sem,
                                          device_id=(x, y, 1 - c), device_id_type=_MESH)
        cp.start()
        cp.wait()

    return pl.pallas_call(
        body, name=name, in_specs=[_HBM], out_specs=_HBM, out_shape=jax.ShapeDtypeStruct(v.shape, v.dtype),
        scratch_shapes=[pltpu.SemaphoreType.DMA, pltpu.SemaphoreType.DMA])(v)


def _own_slot(gathered, own):
    return lax.dynamic_update_index_in_dim(gathered, own, _chip_index(), 0)


def _chip_allgather(v, name):
    def body(v_ref, out_ref, send_sems, recv_sems):
        x, y, c = _position()
        me = 2 * x + y
        sends = []
        for k, (chip, idx) in enumerate(_other_chips(x, y)):
            cp = pltpu.make_async_remote_copy(src_ref=v_ref, dst_ref=out_ref.at[me], send_sem=send_sems.at[k],
                                              recv_sem=recv_sems.at[k], device_id=(*chip, c), device_id_type=_MESH)
            cp.start()
            sends.append(cp)
        for k, (chip, idx) in enumerate(_other_chips(x, y)):
            pltpu.make_async_remote_copy(src_ref=v_ref, dst_ref=out_ref.at[idx], send_sem=send_sems.at[k],
                                         recv_sem=recv_sems.at[k], device_id=(*chip, c), device_id_type=_MESH).wait_recv()
        for cp in sends:
            cp.wait_send()

    out = pl.pallas_call(
        body, name=name, in_specs=[_HBM], out_specs=_HBM, out_shape=jax.ShapeDtypeStruct((4,) + tuple(v.shape), v.dtype),
        scratch_shapes=[pltpu.SemaphoreType.DMA((3,)), pltpu.SemaphoreType.DMA((3,))])(v)
    return _own_slot(out, v)


def _remote(src, dst, send_sems, recv_sems, s, device):
    return pltpu.make_async_remote_copy(src_ref=src, dst_ref=dst, send_sem=send_sems.at[s], recv_sem=recv_sems.at[s],
                                        device_id=device, device_id_type=_MESH)


class _Exchange:
    def __init__(self, ins, out_shapes, n_sems, aliases=None):
        self.ins, self.out_shapes, self.n_sems, self.aliases = list(ins), list(out_shapes), n_sems, aliases or {}

    def sem_shapes(self):
        return [pltpu.SemaphoreType.DMA((self.n_sems,)), pltpu.SemaphoreType.DMA((self.n_sems,))]


def _halves(ref, c, axis=0):
    h = ref.shape[axis] // 2
    idx = (slice(None),) * axis
    return ref.at[idx + (pl.ds(c * h, h),)], ref.at[idx + (pl.ds((1 - c) * h, h),)]


class _GatherShards(_Exchange):
    def __init__(self, ws):
        super().__init__(ws, [jax.ShapeDtypeStruct((N_CHIPS,) + tuple(w.shape), w.dtype) for w in ws], 6 * len(ws))

    def start(self, w_refs, out_refs, sems):
        send_sems, recv_sems = sems
        x, y, c = _position()
        me = 2 * x + y
        for i, (w, out) in enumerate(zip(w_refs, out_refs)):
            for k, (chip, idx) in enumerate(_other_chips(x, y)):
                _remote(_halves(w, c)[0], _halves(out.at[me], c)[0], send_sems, recv_sems, 6 * i + k, (*chip, c)).start()

    def finish(self, w_refs, out_refs, sems):
        send_sems, recv_sems = sems
        x, y, c = _position()
        sibling = (x, y, 1 - c)
        others = _other_chips(x, y)
        for i, out in enumerate(out_refs):
            for k, (chip, idx) in enumerate(others):
                landed = _halves(out.at[idx], c)[0]
                _remote(landed, landed, send_sems, recv_sems, 6 * i + k, (*chip, c)).wait_recv()
                _remote(landed, landed, send_sems, recv_sems, 6 * i + 3 + k, sibling).start()
        for i, (w, out) in enumerate(zip(w_refs, out_refs)):
            for k, (chip, idx) in enumerate(others):
                mine, theirs = _halves(out.at[idx], c)
                _remote(theirs, theirs, send_sems, recv_sems, 6 * i + 3 + k, sibling).wait_recv()
                _remote(mine, mine, send_sems, recv_sems, 6 * i + 3 + k, sibling).wait_send()
                _remote(_halves(w, c)[0], mine, send_sems, recv_sems, 6 * i + k, (*chip, c)).wait_send()


class _SwapHalves(_Exchange):
    def __init__(self, gs):
        shapes = [jax.ShapeDtypeStruct((g.shape[0], g.shape[1] // 2) + tuple(g.shape[2:]), g.dtype) for g in gs]
        super().__init__(gs, shapes, N_CHIPS * len(gs))

    def _copies(self, g_refs, out_refs, sems):
        x, y, c = _position()
        return [_remote(_halves(g.at[j], c)[1], out.at[j], sems[0], sems[1], N_CHIPS * i + j, (x, y, 1 - c))
                for i, (g, out) in enumerate(zip(g_refs, out_refs)) for j in range(N_CHIPS)]

    def start(self, g_refs, out_refs, sems):
        for cp in self._copies(g_refs, out_refs, sems):
            cp.start()

    def finish(self, g_refs, out_refs, sems):
        for cp in self._copies(g_refs, out_refs, sems):
            cp.wait()


class _ScatterPieces(_Exchange):
    def __init__(self, ps):
        super().__init__(ps, [jax.ShapeDtypeStruct((3,) + tuple(p.shape[1:]), p.dtype) for p in ps], 3 * len(ps))

    def _copies(self, p_refs, out_refs, sems):
        x, y, c = _position()
        return [_remote(p.at[idx], out.at[k], sems[0], sems[1], 3 * i + k, (*chip, c))
                for i, (p, out) in enumerate(zip(p_refs, out_refs)) for k, (chip, idx) in enumerate(_other_chips(x, y))]

    def start(self, p_refs, out_refs, sems):
        for cp in self._copies(p_refs, out_refs, sems):
            cp.start()

    def finish(self, p_refs, out_refs, sems):
        for cp in self._copies(p_refs, out_refs, sems):
            cp.wait()


class _JoinHalves(_Exchange):
    def __init__(self, rs):
        super().__init__(rs, [jax.ShapeDtypeStruct(r.shape, r.dtype) for r in rs], len(rs), {i: i for i in range(len(rs))})

    def start(self, r_refs, out_refs, sems):
        x, y, c = _position()
        for i, out in enumerate(out_refs):
            _remote(out.at[c], out.at[c], sems[0], sems[1], i, (x, y, 1 - c)).start()

    def finish(self, r_refs, out_refs, sems):
        x, y, c = _position()
        for i, out in enumerate(out_refs):
            _remote(out.at[c], out.at[c], sems[0], sems[1], i, (x, y, 1 - c)).wait_send()
            _remote(out.at[1 - c], out.at[1 - c], sems[0], sems[1], i, (x, y, 1 - c)).wait_recv()


def _run_exchange(ex, name):
    def body(*refs):
        ins, outs, sems = refs[:len(ex.ins)], refs[len(ex.ins):len(ex.ins) + len(ex.out_shapes)], refs[-2:]
        ex.start(ins, outs, sems)
        ex.finish(ins, outs, sems)

    return pl.pallas_call(body, name=name, in_specs=[_HBM] * len(ex.ins), out_specs=[_HBM] * len(ex.out_shapes),
                          out_shape=ex.out_shapes, scratch_shapes=ex.sem_shapes(), input_output_aliases=ex.aliases)(*ex.ins)


def _pcall(body, name, grid, in_specs, out_specs, out_shape, scratch_shapes, semantics, args, comm=None):
    if comm is None:
        return pl.pallas_call(body, name=name, grid=grid, in_specs=in_specs, out_specs=out_specs, out_shape=out_shape,
                              scratch_shapes=scratch_shapes, compiler_params=_cparams(semantics))(*args)
    n_in, n_out, n_scr, ci, co = len(in_specs), len(out_specs), len(scratch_shapes), len(comm.ins), len(comm.out_shapes)

    def wrapped(*refs):
        parts, a = [], 0
        for k in (n_in, ci, n_out, co, n_scr, 2):
            parts.append(refs[a:a + k])
            a += k
        ins, cins, outs, couts, scr, sems = parts
        ids = [pl.program_id(i) for i in range(len(grid))]
        first = functools.reduce(jnp.logical_and, [i == 0 for i in ids])
        last = functools.reduce(jnp.logical_and, [i == g - 1 for i, g in zip(ids, grid)])

        @pl.when(first)
        def _():
            comm.start(cins, couts, sems)

        body(*ins, *outs, *scr)

        @pl.when(last)
        def _():
            comm.finish(cins, couts, sems)

    res = pl.pallas_call(
        wrapped, name=name, grid=grid, in_specs=list(in_specs) + [_HBM] * ci, out_specs=list(out_specs) + [_HBM] * co,
        out_shape=list(out_shape) + comm.out_shapes, scratch_shapes=list(scratch_shapes) + comm.sem_shapes(),
        compiler_params=_cparams(("arbitrary",) * len(grid)))(*args, *comm.ins)
    return res[:n_out], res[n_out:]


ATT_WINDOW = 128
PHASES = 16
_NT = (((1,), (1,)), ((), ()))
_TN = (((0,), (0,)), ((), ()))


PERM_LANES = 512


def _phase_perm(bl):
    t = 16 * PHASES * bl
    col = jnp.arange(t)
    i, r, b = col // (PHASES * bl), (col // bl) % PHASES, col % bl
    return (jnp.arange(t)[:, None] == ((b * PHASES + r) * 16 + i)[None, :]).astype(BF16)


def _to_phase_order(x, bl, col0=0, width=None):
    n = x.shape[0]
    width = width or x.shape[1]
    t = 16 * PHASES * bl
    g = n // bl // PHASES
    tn = min(PERM_LANES, width)

    def body(p_ref, x_ref, o_ref):
        o_ref[...] = jnp.dot(p_ref[...], x_ref[...], preferred_element_type=F32).astype(o_ref.dtype).reshape(o_ref.shape)

    out = pl.pallas_call(
        body, name="to_phase", grid=(n // t, width // tn),
        in_specs=[pl.BlockSpec((t, t), lambda i, j: (0, 0)), pl.BlockSpec((t, tn), lambda i, j: (i, col0 // tn + j))],
        out_specs=pl.BlockSpec((bl * PHASES, 16, tn), lambda i, j: (0, i, j)),
        out_shape=jax.ShapeDtypeStruct((bl * PHASES, g, width), x.dtype),
        compiler_params=_cparams(("parallel", "parallel")))(_phase_perm(bl), x)
    return out.reshape(n, width)


def _from_phase_order(y, bl):
    n, width = y.shape
    t = 16 * PHASES * bl
    g = n // bl // PHASES
    tn = min(PERM_LANES, width)

    def body(p_ref, y_ref, o_ref):
        o_ref[...] = jnp.dot(p_ref[...], y_ref[...].reshape(t, tn), preferred_element_type=F32).astype(o_ref.dtype)

    return pl.pallas_call(
        body, name="from_phase", grid=(n // t, width // tn),
        in_specs=[pl.BlockSpec((t, t), lambda i, j: (0, 0)), pl.BlockSpec((bl * PHASES, 16, tn), lambda i, j: (0, i, j))],
        out_specs=pl.BlockSpec((t, tn), lambda i, j: (i, j)), out_shape=jax.ShapeDtypeStruct((n, width), y.dtype),
        compiler_params=_cparams(("parallel", "parallel")))(_phase_perm(bl).T, y.reshape(bl * PHASES, g, width))


def _att_geometry(p, n, bl):
    g = n // bl // PHASES
    if p == 0:
        return ((bl, PHASES, g), (bl, g // 16), (None, PHASES, 16),
                lambda sh: (lambda b, a: (b, 0, jnp.maximum(a + sh, 0))), 256, 16, lambda ids: ids[1] == 0)
    if p == 1:
        return ((bl, 4, 4, g), (bl, 4, g // 32), (None, 4, None, 32),
                lambda sh: (lambda b, r, a: (b, 0, r, jnp.maximum(a + sh, 0))), 128, 32, lambda ids: ids[2] == 0)
    return ((bl * PHASES, g), (bl * PHASES,), (None, g), lambda sh: (lambda s: (s, 0)), g, g, None)


def _att_masks(p, qb, chunk):
    def pos(idx):
        return (idx % chunk) * (qb // chunk) + idx // chunk

    dq = pos(lax.broadcasted_iota(jnp.int32, (qb, qb), 0))
    dk = pos(lax.broadcasted_iota(jnp.int32, (qb, qb), 1))
    dist = dq - dk
    return jnp.logical_and(dist >= 0, dist <= ATT_WINDOW), dist + qb <= ATT_WINDOW


def _att_call(p, n, bl, c, body, name, ins, outs):
    prefix, grid, blk, idx_fn, qb, chunk, _ = _att_geometry(p, n, bl)

    def spec(cb, sh):
        f = idx_fn(sh)
        return pl.BlockSpec(blk + (c,), lambda *ids, f=f, cb=cb: f(*ids) + (cb,))

    in_specs = [spec(cb, sh) for _, cb, sh in ins]
    out_specs = [spec(0, 0) for _ in outs]
    out_shape = [jax.ShapeDtypeStruct(prefix + (c,), dt) for dt in outs]
    res = pl.pallas_call(body, name=name, grid=grid, in_specs=in_specs, out_specs=out_specs, out_shape=out_shape,
                         compiler_params=_cparams(("parallel",) * len(grid)))(*[a.reshape(prefix + (a.shape[1],)) for a, _, _ in ins])
    return [r.reshape(n, c) for r in res]


def _att_fwd(p, qkv, qcb, bl, c, heads):
    n = qkv.shape[0]
    _, grid, _, _, qb, chunk, first_fn = _att_geometry(p, n, bl)
    n_grid = len(grid)
    has_prev = first_fn is not None
    e = c // heads
    scale = e ** -0.5

    def body(*refs):
        if has_prev:
            q_ref, kc_ref, kp_ref, vc_ref, vp_ref, o_ref, l_ref = refs
        else:
            q_ref, kc_ref, vc_ref, o_ref, l_ref = refs
        ids = [pl.program_id(a) for a in range(n_grid)]
        mc, mp = _att_masks(p, qb, chunk)
        if has_prev:
            mp = jnp.logical_and(mp, jnp.logical_not(first_fn(ids)))
        lo = lax.broadcasted_iota(jnp.int32, (qb, 128), 1) < e
        shp = o_ref.shape[:-1]
        ones = jnp.ones((qb, 128), BF16)
        n_t = c // 128
        load = lambda r, t: r[..., pl.ds(t * 128, 128)].reshape(qb, 128)
        items = [(t, h) for t in range(n_t) for h in range(2)]
        dot = functools.partial(jnp.dot, preferred_element_type=F32)
        q2 = [load(q_ref, t) for t in range(n_t)]
        kc = [load(kc_ref, t) for t in range(n_t)]
        qm = [jnp.where(lo if h == 0 else jnp.logical_not(lo), q2[t], jnp.zeros_like(q2[t])) for t, h in items]
        sc = [jnp.where(mc, lax.dot_general(qm[i], kc[t], _NT, preferred_element_type=F32) * scale, -jnp.inf)
              for i, (t, h) in enumerate(items)]
        m = [jnp.max(s, axis=1, keepdims=True) for s in sc]
        if has_prev:
            kp = [load(kp_ref, t) for t in range(n_t)]
            sp = [jnp.where(mp, lax.dot_general(qm[i], kp[t], _NT, preferred_element_type=F32) * scale, -jnp.inf)
                  for i, (t, h) in enumerate(items)]
            m = [jnp.maximum(a, jnp.max(s, axis=1, keepdims=True)) for a, s in zip(m, sp)]
        pc = [jnp.exp(s - a).astype(BF16) for s, a in zip(sc, m)]
        vc = [load(vc_ref, t) for t in range(n_t)]
        acc = [dot(pc[i], vc[t]) for i, (t, h) in enumerate(items)]
        den = [dot(x, ones) for x in pc]
        if has_prev:
            pp = [jnp.exp(s - a).astype(BF16) for s, a in zip(sp, m)]
            vp = [load(vp_ref, t) for t in range(n_t)]
            acc = [a + dot(pp[i], vp[t]) for i, ((t, h), a) in enumerate(zip(items, acc))]
            den = [d + dot(x, ones) for d, x in zip(den, pp)]
        oh = [a / d for a, d in zip(acc, den)]
        lh = [a + jnp.log(d) for a, d in zip(m, den)]
        for t in range(n_t):
            ls = pl.ds(t * 128, 128)
            o_ref[..., ls] = jnp.where(lo, oh[2 * t], oh[2 * t + 1]).astype(o_ref.dtype).reshape(shp + (128,))
            l_ref[..., ls] = jnp.where(lo, lh[2 * t], lh[2 * t + 1]).reshape(shp + (128,))

    kcb, vcb = 3, 4
    ins = [(qkv, qcb, 0), (qkv, kcb, 0)] + ([(qkv, kcb, -1)] if has_prev else []) + [(qkv, vcb, 0)] + ([(qkv, vcb, -1)] if has_prev else [])
    return _att_call(p, n, bl, c, body, name=f"att_fwd{p}", ins=ins, outs=[BF16, F32])


def _att_bwd(p, qkv, qcb, o, do, lse, dlse, bl, c, heads):
    n = qkv.shape[0]
    _, grid, _, _, qb, chunk, first_fn = _att_geometry(p, n, bl)
    n_grid = len(grid)
    has_prev = first_fn is not None
    e = c // heads
    scale = e ** -0.5

    def body(*refs):
        if has_prev:
            q_ref, kc_ref, kp_ref, vc_ref, vp_ref, o_ref, do_ref, l_ref, dl_ref, dq_ref, dkc_ref, dkp_ref, dvc_ref, dvp_ref = refs
        else:
            q_ref, kc_ref, vc_ref, o_ref, do_ref, l_ref, dl_ref, dq_ref, dkc_ref, dvc_ref = refs
        ids = [pl.program_id(a) for a in range(n_grid)]
        mc, mp = _att_masks(p, qb, chunk)
        if has_prev:
            mp = jnp.logical_and(mp, jnp.logical_not(first_fn(ids)))
        lo = lax.broadcasted_iota(jnp.int32, (qb, 128), 1) < e
        shp = o_ref.shape[:-1]
        n_t = c // 128
        load = lambda r, t: r[..., pl.ds(t * 128, 128)].reshape(qb, 128)
        items = [(t, h) for t in range(n_t) for h in range(2)]
        nt_dot = lambda a, b: lax.dot_general(a, b, _NT, preferred_element_type=F32)
        tn_dot = lambda a, b: lax.dot_general(a, b, _TN, preferred_element_type=F32)
        dot = functools.partial(jnp.dot, preferred_element_type=F32)

        def store(r, t, v):
            r[..., pl.ds(t * 128, 128)] = v.astype(r.dtype).reshape(shp + (128,))

        sel = [lo if h == 0 else jnp.logical_not(lo) for t, h in items]
        q2, kc, vc, do2 = ([load(r, t) for t in range(n_t)] for r in (q_ref, kc_ref, vc_ref, do_ref))
        qm = [jnp.where(sel[i], q2[t], jnp.zeros_like(q2[t])) for i, (t, h) in enumerate(items)]
        dom = [jnp.where(sel[i], do2[t], jnp.zeros_like(do2[t])) for i, (t, h) in enumerate(items)]
        dod = [do2[t].astype(F32) * load(o_ref, t).astype(F32) for t in range(n_t)]
        lcol = [load(l_ref, t)[:, h * e:h * e + 1] for t, h in items]
        corr = [load(dl_ref, t)[:, h * e:h * e + 1] - jnp.sum(jnp.where(sel[i], dod[t], 0.0), axis=1, keepdims=True)
                for i, (t, h) in enumerate(items)]
        pc = [jnp.exp(jnp.where(mc, nt_dot(qm[i], kc[t]) * scale, -jnp.inf) - lcol[i]) for i, (t, h) in enumerate(items)]
        dsc = [(pc[i] * (nt_dot(dom[i], vc[t]) + corr[i]) * scale).astype(BF16) for i, (t, h) in enumerate(items)]
        pc = [x.astype(BF16) for x in pc]
        dq = [dot(dsc[i], kc[t]) for i, (t, h) in enumerate(items)]
        dkc = [tn_dot(dsc[2 * t], qm[2 * t]) + tn_dot(dsc[2 * t + 1], qm[2 * t + 1]) for t in range(n_t)]
        dvc = [tn_dot(pc[2 * t], dom[2 * t]) + tn_dot(pc[2 * t + 1], dom[2 * t + 1]) for t in range(n_t)]
        if has_prev:
            kp, vp = ([load(r, t) for t in range(n_t)] for r in (kp_ref, vp_ref))
            pp = [jnp.exp(jnp.where(mp, nt_dot(qm[i], kp[t]) * scale, -jnp.inf) - lcol[i]) for i, (t, h) in enumerate(items)]
            dsp = [(pp[i] * (nt_dot(dom[i], vp[t]) + corr[i]) * scale).astype(BF16) for i, (t, h) in enumerate(items)]
            pp = [x.astype(BF16) for x in pp]
            dq = [a + dot(dsp[i], kp[t]) for i, ((t, h), a) in enumerate(zip(items, dq))]
            dkp = [tn_dot(dsp[2 * t], qm[2 * t]) + tn_dot(dsp[2 * t + 1], qm[2 * t + 1]) for t in range(n_t)]
            dvp = [tn_dot(pp[2 * t], dom[2 * t]) + tn_dot(pp[2 * t + 1], dom[2 * t + 1]) for t in range(n_t)]
        for t in range(n_t):
            store(dq_ref, t, jnp.where(lo, dq[2 * t], dq[2 * t + 1]))
            store(dkc_ref, t, dkc[t])
            store(dvc_ref, t, dvc[t])
            if has_prev:
                store(dkp_ref, t, dkp[t])
                store(dvp_ref, t, dvp[t])

    kcb, vcb = 3, 4
    ins = [(qkv, qcb, 0), (qkv, kcb, 0)] + ([(qkv, kcb, -1)] if has_prev else []) + [(qkv, vcb, 0)] + ([(qkv, vcb, -1)] if has_prev else [])
    ins += [(o, 0, 0), (do, 0, 0), (lse, 0, 0), (dlse, 0, 0)]
    res = _att_call(p, n, bl, c, body, name=f"att_bwd{p}", ins=ins, outs=[BF16] * (5 if has_prev else 3))
    if has_prev:
        dq, dkc, dkp, dvc, dvp = res
        return dq, dkc, dkp, dvc, dvp
    dq, dkc, dvc = res
    return dq, dkc, None, dvc, None


def _att_fold_prev(p, cur, prv, bl):
    if prv is None:
        return cur.astype(F32)
    n, c = cur.shape
    prefix, _, _, _, _, chunk, _ = _att_geometry(p, n, bl)
    v = prv.reshape(prefix + (c,)).astype(F32)
    shifted = jnp.concatenate([v[..., chunk:, :], jnp.zeros_like(v[..., :chunk, :])], axis=-2)
    return cur.astype(F32) + shifted.reshape(n, c)


def _attention_fwd(proj, c, bl, heads):
    n = proj.shape[0]
    qkv = _to_phase_order(proj, bl, col0=c, width=5 * c)
    outs = [_att_fwd(p, qkv, p, bl, c, heads) for p in range(3)]
    ins = [("row", o, c, 0) for o, _ in outs] + [("row", l, c, 0) for _, l in outs]
    o, = _rowwise(_combine_fwd_fn, "comb_fwd", n, ins, [(c, BF16)])
    return _from_phase_order(o, bl), (qkv, outs)


def _attention_bwd(do_tb, saved, bl, heads):
    qkv, outs = saved
    n, c = do_tb.shape
    e = c // heads
    lane = jnp.arange(c) // e
    jmat = (lane[:, None] == lane[None, :]).astype(F32)
    do = _to_phase_order(do_tb, bl)
    ins = [("row", o, c, 0) for o, _ in outs] + [("row", l, c, 0) for _, l in outs] + [("row", do, c, 0), ("par", jmat)]
    res = _rowwise(_combine_bwd_fn, "comb_bwd", n, ins, [(c, BF16)] * 3 + [(c, F32)] * 3)
    dqs, dk, dv = [], 0.0, 0.0
    for p in range(3):
        dq, dkc, dkp, dvc, dvp = _att_bwd(p, qkv, p, outs[p][0], res[p], outs[p][1], res[3 + p], bl, c, heads)
        dqs.append(dq)
        dk = dk + _att_fold_prev(p, dkc, dkp, bl)
        dv = dv + _att_fold_prev(p, dvc, dvp, bl)
    dqkv = jnp.concatenate(dqs + [dk.astype(BF16), dv.astype(BF16)], axis=1)
    return _from_phase_order(dqkv, bl)


ATT_HEADS = 8
SSM_GROUPS, SSM_STATE, SSM_GROUP = 32, 64, 16


def _row(v):
    return v.reshape(1, -1)


def _carried(result, carry, key, hidden):
    if carry.get(key) is None:
        return result
    result, hidden[key] = result
    return result


def _layer_fwd(x, w, p, bl, carry, late=None):
    n, d = x.shape
    c = d // 2
    hidden = {}
    h, = _rowwise(_rms_fwd_fn, "rms_fwd", n, [("row", x, d, 0), ("par", _row(p["norm1_g"]))], [(d, BF16)])
    proj = _carried(_mm(h, w["w_in"], "nn", BF16, "mm_in", comm=carry.get("mm_in")), carry, "mm_in", hidden)
    if late is not None:
        w = dict(w, **late(hidden["mm_in"]))
    disc, disc_vjp = jax.vjp(_ssm_disc, p["ssm_lambda_re"], p["ssm_lambda_im"], p["ssm_log_dt"], p["ssm_b_re"], p["ssm_b_im"])
    bbd, cdm, a8 = _ssm_pack(*disc, p["ssm_c_re"], p["ssm_c_im"])
    ypre, yg, s_all = _carried(_ssm_fwd(proj, bbd, cdm, a8, _row(p["ssm_d"]), bl, "ssm_fwd", comm=carry.get("ssm_fwd")),
                               carry, "ssm_fwd", hidden)
    zs = _mm(yg, w["w_ssm_glu"], "nn", BF16, "mm_glu")
    o, att = _attention_fwd(proj, c, bl, ATT_HEADS)
    ya = _mm(o, w["w_att_up"], "nn", BF16, "mm_att")
    w32 = jnp.concatenate([p["conv_w"], jnp.zeros((1, c), F32)], axis=0)
    hc, hconv = _conv_fwd(proj, 6, w32, _row(p["conv_b"]), _row(p["conv_ln_g"]), _row(p["conv_ln_b"]), bl, c, "conv_fwd")
    yc = _mm(hc, w["w_conv_pw2"], "nn", BF16, "mm_pw2")
    gates = [("row", proj, d, 4), ("row", proj, d, 5), ("row", proj, d, 6), ("par", _row(p["b_gate"]))]
    branches = [("row", zs, 2 * d, 0), ("row", ya, d, 0), ("row", yc, d, 0)]
    merged, = _rowwise(_merge_fwd_fn, "merge_fwd", n, gates + branches, [(d, BF16)])
    xm = _mm(merged, w["w_out"], "nn", F32, "mm_out", res=x)
    h2, = _rowwise(_rms_fwd_fn, "rms_fwd", n, [("row", xm, d, 0), ("par", _row(p["norm2_g"]))], [(d, BF16)])
    z = _carried(_mm(h2, w["w_ffn_in"], "nn", BF16, "mm_ffn_in", comm=carry.get("mm_ffn_in")), carry, "mm_ffn_in", hidden)
    f = z.shape[1] // 2
    a, = _rowwise(_swiglu_fwd_fn, "swiglu_fwd", n, [("row", z, 2 * f, 0)], [(f, BF16)], tm=256)
    xo = _mm(a, w["w_ffn_out"], "nn", F32, "mm_ffn_out", res=xm)
    saved = dict(x=x, h=h, proj=proj, disc_vjp=disc_vjp, bbd=bbd, cdm=cdm, a8=a8, ypre=ypre, yg=yg, s_all=s_all, zs=zs, o=o,
                 att=att, ya=ya, w32=w32, hc=hc, hconv=hconv, yc=yc, gates=gates, branches=branches, merged=merged, xm=xm,
                 h2=h2, z=z, a=a)
    return xo, saved, hidden, w


def _layer_bwd(dxo, s, w, p, bl, carry):
    n, d = dxo.shape
    c = d // 2
    g, bufs, hidden = {}, {}, {}
    f = s["a"].shape[1]

    def dw(key, a, dy, name):
        bufs[key] = _mm_dw(a, dy, name, 1 if key in ROW_SHARDED else N_CHIPS)

    da = _mm(dxo, w["w_ffn_out"], "nt", BF16, "mm_ffn_out_dx")
    dw("w_ffn_out", s["a"], dxo, "mm_ffn_out_dw")
    dz, = _rowwise(_swiglu_bwd_fn, "swiglu_bwd", n, [("row", s["z"], 2 * f, 0), ("row", da, f, 0)], [(2 * f, BF16)], tm=256)
    dh2 = _mm(dz, w["w_ffn_in"], "nt", F32, "mm_ffn_in_dx")
    dw("w_ffn_in", s["h2"], dz, "mm_ffn_in_dw")
    dxm, dg2 = _rowwise(_rms_bwd_fn, "rms_bwd", n, [("row", s["xm"], d, 0), ("par", _row(p["norm2_g"])), ("row", dh2, d, 0),
                                                   ("row", dxo, d, 0)], [(d, F32)], [d])
    g["norm2_g"] = dg2[0]
    dmerged = _mm(dxm, w["w_out"], "nt", BF16, "mm_out_dx")
    dw("w_out", s["merged"], dxm, "mm_out_dw")
    dgl, dzs, dya, dyc, dbg = _rowwise(_merge_bwd_fn, "merge_bwd", n, s["gates"] + s["branches"] + [("row", dmerged, d, 0)],
                                       [(3 * d, BF16), (2 * d, BF16), (d, BF16), (d, BF16)], [3 * d], tm=256)
    g["b_gate"] = dbg[0]
    dyg = _mm(dzs, w["w_ssm_glu"], "nt", BF16, "mm_glu_dx")
    dw("w_ssm_glu", s["yg"], dzs, "mm_glu_dw")
    du, dbb, dcd, dab, dd = _carried(
        _ssm_bwd(dyg, s["ypre"], s["proj"], s["s_all"], s["cdm"].transpose(0, 2, 1), s["bbd"].transpose(0, 2, 1), s["a8"],
                 _row(p["ssm_d"]), bl, "ssm_bwd", comm=carry.get("ssm_bwd")), carry, "ssm_bwd", hidden)
    dab_re, dab_im, dbb_re, dbb_im, g["ssm_c_re"], g["ssm_c_im"] = _ssm_unpack(dbb, dcd, dab, SSM_GROUPS, SSM_STATE, SSM_GROUP)
    (g["ssm_lambda_re"], g["ssm_lambda_im"], g["ssm_log_dt"], g["ssm_b_re"],
     g["ssm_b_im"]) = s["disc_vjp"]((dab_re, dab_im, dbb_re, dbb_im))
    g["ssm_d"] = dd[0]
    do = _mm(dya, w["w_att_up"], "nt", BF16, "mm_att_dx")
    dw("w_att_up", s["o"], dya, "mm_att_dw")
    dqkv = _attention_bwd(do, s["att"], bl, ATT_HEADS)
    dhc = _mm(dyc, w["w_conv_pw2"], "nt", BF16, "mm_pw2_dx")
    dw("w_conv_pw2", s["hc"], dyc, "mm_pw2_dw")
    dcv, dcw, dcb, dlg, dlb = _carried(
        _conv_bwd(s["proj"], 6, dhc, s["hconv"], s["w32"], _row(p["conv_ln_g"]), _row(p["conv_ln_b"]), bl, c, "conv_bwd",
                  comm=carry.get("conv_bwd")), carry, "conv_bwd", hidden)
    g["conv_w"], g["conv_b"], g["conv_ln_g"], g["conv_ln_b"] = dcw, dcb[0], dlg[0], dlb[0]
    dproj = jnp.concatenate([du, dqkv, dcv, dgl], axis=1)
    dh = _mm(dproj, w["w_in"], "nt", F32, "mm_in_dx")
    dw("w_in", s["h"], dproj, "mm_in_dw")
    dx, dg1 = _rowwise(_rms_bwd_fn, "rms_bwd", n, [("row", s["x"], d, 0), ("par", _row(p["norm1_g"])), ("row", dh, d, 0),
                                                  ("row", dxm, d, 0)], [(d, F32)], [d])
    g["norm1_g"] = dg1[0]
    return dx, g, bufs, hidden


WEIGHTS = ['norm1_g', 'w_in', 'b_gate', 'ssm_lambda_re', 'ssm_lambda_im', 'ssm_log_dt', 'ssm_b_re', 'ssm_b_im', 'ssm_c_re',
           'ssm_c_im', 'ssm_d', 'w_ssm_glu', 'w_att_up', 'conv_w', 'conv_b', 'conv_ln_g', 'conv_ln_b', 'w_conv_pw2', 'w_out',
           'norm2_g', 'w_ffn_in', 'w_ffn_out', 'final_g']
BIG = ['w_in', 'w_ssm_glu', 'w_att_up', 'w_conv_pw2', 'w_out', 'w_ffn_in', 'w_ffn_out']
ROW_SHARDED = ('w_out', 'w_ffn_out')
SMALL = [k for k in WEIGHTS if k not in BIG]
LANES = 1024
N_CHIPS = 4
ROW_TILE_BYTES = 36 * 1024 * 1024
MIN_SHARD_TILE = 1024


def _pad_rows(a, rows):
    return jnp.concatenate([a, jnp.zeros((rows - a.shape[0],) + a.shape[1:], a.dtype)], axis=0) if rows > a.shape[0] else a


def _row_tile(rows, width, n_arrays):
    best = 16
    for t in range(16, rows + 1, 16):
        if rows % t == 0 and t * width * 4 * n_arrays * 2 <= ROW_TILE_BYTES:
            best = t
    return best


def _flat_fn(fn, name, ins, n_out, rows):
    return _rowwise(fn, name, rows, [("row", a, LANES, 0) for a in ins], [(LANES, F32)] * n_out, tm=rows)


def _reduce_prepare(bufs):
    landed = _run_exchange(_SwapHalves([b16 for _, b16 in bufs]), "rs_swap")
    p32s, p16s = [], []
    for (b32, _), la in zip(bufs, landed):
        s, m, cs = b32.shape
        h = m // 2
        tm = _row_tile(h, cs, 4)

        def body(g_ref, l_ref, o32, o16):
            r = g_ref[...] + l_ref[...].astype(F32)
            o32[...] = r
            o16[...] = r.astype(BF16)

        piece = pl.BlockSpec((None, tm, cs), lambda j, i: (j, i, 0))
        mine = pl.BlockSpec((None, None, tm, cs), lambda j, i: (j, _core_index(), i, 0))
        p32, p16 = pl.pallas_call(
            body, name="rs_add", grid=(s, h // tm), in_specs=[mine, piece], out_specs=[piece, piece],
            out_shape=[jax.ShapeDtypeStruct((s, h, cs), F32), jax.ShapeDtypeStruct((s, h, cs), BF16)],
            compiler_params=_cparams(("parallel", "parallel")))(b32.reshape(s, 2, h, cs), la)
        p32s.append(p32)
        p16s.append(p16)
    return p32s, p16s


def _reduce_finish(p32s, arrived):
    reduced = []
    for p32, lb in zip(p32s, arrived):
        _, h, cs = lb.shape
        tm = _row_tile(h, cs, 5)

        def body(p_ref, a_ref, b_ref, c_ref, o_ref):
            o_ref[...] = ((p_ref[...] + a_ref[...].astype(F32)) + b_ref[...].astype(F32)) + c_ref[...].astype(F32)

        mine = pl.BlockSpec((None, tm, cs), lambda i: (_chip_index(), i, 0))
        other = [pl.BlockSpec((None, tm, cs), lambda i, k=k: (k, i, 0)) for k in range(3)]
        half = pl.BlockSpec((None, tm, cs), lambda i: (_core_index(), i, 0))
        reduced.append(pl.pallas_call(
            body, name="rs_sum", grid=(h // tm,), in_specs=[mine] + other, out_specs=half,
            out_shape=jax.ShapeDtypeStruct((2, h, cs), F32), compiler_params=_cparams(("parallel",)))(p32, lb, lb, lb))
    joined = _run_exchange(_JoinHalves(reduced), "rs_gather")
    return [j.reshape(2 * j.shape[1], j.shape[2]) for j in joined]


def _adamw_layers(w, g_layers, m, v):
    depth, rows, cs = w.shape
    tm = _row_tile(rows, cs, 8)
    nb = rows // tm

    def body(*refs):
        w_ref, m_ref, v_ref = refs[:3]
        g_refs = refs[3:3 + depth]
        go_ref, d_ref, mo_ref, vo_ref = refs[3 + depth:]
        layer = pl.program_id(0)
        g = g_refs[0][...]
        for l in range(1, depth):
            g = jnp.where(layer == l, g_refs[l][...], g)
        delta, mo, vo = _adamw_fn(w_ref[...], g, m_ref[...], v_ref[...])
        go_ref[...], d_ref[...], mo_ref[...], vo_ref[...] = g, delta, mo, vo

    stacked = pl.BlockSpec((None, tm, cs), lambda l, i: (l, i, 0))
    g_specs = [pl.BlockSpec((tm, cs), lambda l, i, k=k: (jnp.where(l == k, i, jnp.where(l < k, 0, nb - 1)), 0)) for k in range(depth)]
    return pl.pallas_call(
        body, name="adamw", grid=(depth, nb), in_specs=[stacked] * 3 + g_specs, out_specs=[stacked] * 4,
        out_shape=[jax.ShapeDtypeStruct(w.shape, F32)] * 4, compiler_params=_cparams(("arbitrary", "arbitrary")))(w, m, v, *g_layers)


def _sum4_fn(a, b, c, d):
    return (((a.astype(F32) + b.astype(F32)) + c.astype(F32)) + d.astype(F32),)


def _add2_fn(a, b):
    return (a + b,)


def kernel(x, norm1_g, w_in, b_gate, ssm_lambda_re, ssm_lambda_im, ssm_log_dt, ssm_b_re, ssm_b_im, ssm_c_re, ssm_c_im, ssm_d, w_ssm_glu, w_att_up, conv_w, conv_b, conv_ln_g, conv_ln_b, w_conv_pw2, w_out, norm2_g, w_ffn_in, w_ffn_out, final_g, loss_target, m_norm1_g, m_w_in, m_b_gate, m_ssm_lambda_re, m_ssm_lambda_im, m_ssm_log_dt, m_ssm_b_re, m_ssm_b_im, m_ssm_c_re, m_ssm_c_im, m_ssm_d, m_w_ssm_glu, m_w_att_up, m_conv_w, m_conv_b, m_conv_ln_g, m_conv_ln_b, m_w_conv_pw2, m_w_out, m_norm2_g, m_w_ffn_in, m_w_ffn_out, m_final_g, v_norm1_g, v_w_in, v_b_gate, v_ssm_lambda_re, v_ssm_lambda_im, v_ssm_log_dt, v_ssm_b_re, v_ssm_b_im, v_ssm_c_re, v_ssm_c_im, v_ssm_d, v_w_ssm_glu, v_w_att_up, v_conv_w, v_conv_b, v_conv_ln_g, v_conv_ln_b, v_w_conv_pw2, v_w_out, v_norm2_g, v_w_ffn_in, v_w_ffn_out, v_final_g):
    args = dict(locals())
    wts = {k: args[k] for k in WEIGHTS}
    mom = {k: args["m_" + k] for k in WEIGHTS}
    var = {k: args["v_" + k] for k in WEIGHTS}
    bl, seq, d = x.shape
    n = bl * seq
    depth = norm1_g.shape[0]
    cx, cy, cc = _position()
    me = 2 * cx + cy

    assert depth == 2, "the exchanges of layer 1 are hidden behind layer 0's kernels"
    first = BIG[:1]
    rest = BIG[1:]

    shards = lambda keys, l: [wts[k][l].astype(BF16) for k in keys]

    def whole(keys, gathered):
        out = {}
        for k, a in zip(keys, gathered):
            _, ks, cs = a.shape
            if k in ROW_SHARDED:
                out[k] = a.reshape(N_CHIPS * ks, cs)
            elif cs < MIN_SHARD_TILE:
                out[k] = a.transpose(1, 0, 2).reshape(ks, N_CHIPS * cs)
            else:
                out[k] = a
        return out

    fill = lambda gathered, own: [_own_slot(g, o) for g, o in zip(gathered, own)]
    own0 = shards(first, 0) + [conv_w]
    gathered = fill(_run_exchange(_GatherShards(own0), "gather_weights"), own0)
    conv_full = gathered[-1].transpose(1, 2, 0, 3).reshape(depth, CONV_WIDTH, -1)
    params = lambda l: dict({k: wts[k][l] for k in SMALL if k not in ("final_g", "conv_w")}, conv_w=conv_full[l])

    to_rows = lambda t: t.transpose(1, 0, 2).reshape(n, d)
    own = {"mm_in": shards(rest, 0), "ssm_fwd": shards(first, 1), "mm_ffn_in": shards(rest, 1)}
    xs, s0, hidden, w0 = _layer_fwd(to_rows(x), whole(first, gathered[:-1]), params(0), bl, {k: _GatherShards(v) for k, v in own.items()},
                                    late=lambda got: whole(rest, fill(got, own["mm_in"])))
    w1 = dict(whole(first, fill(hidden["ssm_fwd"], own["ssm_fwd"])), **whole(rest, fill(hidden["mm_ffn_in"], own["mm_ffn_in"])))
    full = [w0, w1]
    xs, s1, _, _ = _layer_fwd(xs, full[1], params(1), bl, {})
    dx, sq, dgf = _rowwise(_loss_fn, "loss_head", n, [("row", xs, d, 0), ("par", _row(final_g)), ("row", to_rows(loss_target), d, 0)],
                           [(d, F32)], [d, d])
    loss = lax.psum(0.5 * jnp.sum(sq) / d, ("x", "y", "c"))

    pieces = lambda bufs, keys: [tuple(b.reshape(N_CHIPS, -1, b.shape[-1]) for b in bufs[k]) for k in keys]
    dx, g1, bufs1, _ = _layer_bwd(dx, s1, full[1], params(1), bl, {})
    p32_1, p16_1 = _reduce_prepare(pieces(bufs1, BIG))
    dx, g0, bufs0, hidden = _layer_bwd(dx, s0, full[0], params(0), bl,
                                       {"ssm_bwd": _ScatterPieces(p16_1[:1]), "conv_bwd": _ScatterPieces(p16_1[1:])})
    red1 = _reduce_finish(p32_1, list(hidden["ssm_bwd"]) + list(hidden["conv_bwd"]))
    p32_0, p16_0 = _reduce_prepare(pieces(bufs0, BIG))
    red0 = _reduce_finish(p32_0, _run_exchange(_ScatterPieces(p16_0), "rs_scatter"))
    grads = {"final_g": dgf[0]}
    for k in SMALL:
        if k != "final_g":
            grads[k] = jnp.stack([g0[k], g1[k]])
    grad_x = dx.reshape(seq, bl, d).transpose(1, 0, 2)
    outs = {}
    for k, r0, r1 in zip(BIG, red0, red1):
        for tag, a in zip(("grad", "delta", "m", "v"), _adamw_layers(wts[k], [r0, r1], mom[k], var[k])):
            outs[tag, k] = a

    def flat1(t):
        v = jnp.concatenate([t[k].reshape(-1) for k in SMALL])
        rows = -(-v.size // (8 * LANES)) * 8
        return _pad_rows(v, rows * LANES).reshape(rows, LANES), rows

    def unflat1(flat, shapes):
        out, off, v = {}, 0, flat.reshape(-1)
        for k in SMALL:
            size = math.prod(shapes[k])
            out[k] = v[off:off + size].reshape(shapes[k])
            off += size
        return out

    grads["conv_w"] = grads["conv_w"][:, :CONV_WIDTH]
    gs, rows = flat1(grads)
    chip_sum, = _flat_fn(_add2_fn, "ar_add", [gs, _swap_sibling(gs, "ar_swap")], 1, rows)
    slots = _chip_allgather(chip_sum, "ar_gather")
    gs_red, = _flat_fn(_sum4_fn, "ar_sum", [slots[j] for j in range(N_CHIPS)], 1, rows)
    g_sm = unflat1(gs_red, {k: grads[k].shape for k in SMALL})
    cs = conv_w.shape[2]
    g_sm["conv_w"] = lax.dynamic_slice_in_dim(g_sm["conv_w"], me * cs, cs, axis=2)
    (w1, rows), (g1, _), (m1, _), (v1, _) = flat1(wts), flat1(g_sm), flat1(mom), flat1(var)
    sm_out = _flat_fn(_adamw_fn, "adamw_small", [w1, g1, m1, v1], 3, rows)
    shapes = {k: wts[k].shape for k in SMALL}
    for tag, a in zip(("delta", "m", "v"), sm_out):
        for k, t in unflat1(a, shapes).items():
            outs[tag, k] = t
    for k in SMALL:
        outs["grad", k] = g_sm[k]
    return (loss, grad_x, *[outs["grad", k] for k in WEIGHTS], *[outs["delta", k] for k in WEIGHTS],
            *[outs["m", k] for k in WEIGHTS], *[outs["v", k] for k in WEIGHTS])
```

```python
import functools
import math

import jax
import jax.numpy as jnp
from jax import lax
from jax.experimental import pallas as pl
from jax.experimental.pallas import tpu as pltpu

F32 = jnp.float32
BF16 = jnp.bfloat16
VMEM_LIMIT = 56 * 1024 * 1024


def _cparams(sem):
    return pltpu.CompilerParams(dimension_semantics=sem, vmem_limit_bytes=VMEM_LIMIT)


_DIMS = {"nn": (((1,), (0,)), ((), ())), "nt": (((1,), (1,)), ((), ())), "tn": (((0,), (0,)), ((), ()))}


MM_ROWS = 1024
MM_DW_VMEM_BYTES = 44 * 1024 * 1024


def _div_tile(n, cap):
    best = None
    for t in range(128, min(n, cap) + 1, 128):
        if n % t == 0:
            best = t
    return best or n


def _mm(a, b, form, out_dtype, name, res=None, comm=None):
    sharded = b.ndim == 3
    kdim, cs = b.shape[-2], b.shape[-1]
    s = b.shape[0] if sharded else 1
    m = a.shape[0]
    tm = MM_ROWS if m % MM_ROWS == 0 else _div_tile(m, MM_ROWS)
    if form == "nn":
        n, kd = s * cs, kdim
        tn, tk = _div_tile(cs, 1792), _div_tile(kdim, 2048)
        per = cs // tn
        b_blk = (tk, tn)
        b_idx = (lambda i, j, k: (j // per, k, j % per)) if sharded else (lambda i, j, k: (k, j))
    else:
        n, kd = kdim, s * cs
        tn, tk = _div_tile(kdim, 1408), _div_tile(cs, 1792)
        per = cs // tk
        b_blk = (tn, tk)
        b_idx = (lambda i, j, k: (k // per, j, k % per)) if sharded else (lambda i, j, k: (j, k))
    nk = kd // tk
    a_spec = pl.BlockSpec((tm, tk), lambda i, j, k: (i, k))
    b_spec = pl.BlockSpec(((None,) + b_blk) if sharded else b_blk, b_idx)
    o_spec = pl.BlockSpec((tm, tn), lambda i, j, k: (i, j))
    dims = _DIMS[form]

    def body(*refs):
        a_ref, b_ref = refs[:2]
        r_ref = refs[2] if res is not None else None
        o_ref = refs[3] if res is not None else refs[2]
        p = lax.dot_general(a_ref[...].astype(BF16), b_ref[...], dims, preferred_element_type=F32)

        def finish(r):
            if r_ref is not None:
                r = r + r_ref[...]
            o_ref[...] = r.astype(out_dtype)

        if nk == 1:
            finish(p)
            return
        acc = refs[-1]
        k = pl.program_id(2)

        @pl.when(k == 0)
        def _():
            acc[...] = p

        @pl.when(k > 0)
        def _():
            acc[...] += p

        @pl.when(k == nk - 1)
        def _():
            finish(acc[...])

    ins = [a, b] + ([] if res is None else [res])
    in_specs = [a_spec, b_spec] + ([] if res is None else [o_spec])
    out = _pcall(body, name, (m // tm, n // tn, nk), in_specs, [o_spec], [jax.ShapeDtypeStruct((m, n), out_dtype)],
                 [pltpu.VMEM((tm, tn), F32)] if nk > 1 else [], ("parallel", "parallel", "arbitrary"), ins, comm)
    return out[0] if comm is None else (out[0][0], out[1])


def _mm_dw(a, dy, name, shards):
    r, m = a.shape
    c = dy.shape[1]
    cs = c // shards
    tm, tn = _div_tile(m, 1408), _div_tile(cs, 1408)
    fixed = tm * tn * (4 + 2 * (4 + 2))
    per_row = 2 * (tm * a.dtype.itemsize + tn * dy.dtype.itemsize)
    tk = max(t for t in (256, 512, 1024, 2048) if r % t == 0 and (t == 256 or fixed + t * per_row <= MM_DW_VMEM_BYTES))
    per = cs // tn
    nk = r // tk

    def body(a_ref, b_ref, o32, o16, acc):
        k = pl.program_id(2)
        p = lax.dot_general(a_ref[...].astype(BF16), b_ref[...].astype(BF16), _DIMS["tn"], preferred_element_type=F32)

        @pl.when(k == 0)
        def _():
            acc[...] = p

        @pl.when(k > 0)
        def _():
            acc[...] += p

        @pl.when(k == nk - 1)
        def _():
            o32[...] = acc[...]
            o16[...] = acc[...].astype(BF16)

    o_spec = pl.BlockSpec((None, tm, tn), lambda i, j, k: (j // per, i, j % per))
    shape = (shards, m, cs)
    in_specs = [pl.BlockSpec((tk, tm), lambda i, j, k: (k, i)), pl.BlockSpec((tk, tn), lambda i, j, k: (k, j))]
    return _pcall(body, name, (m // tm, c // tn, nk), in_specs, [o_spec, o_spec],
                  [jax.ShapeDtypeStruct(shape, F32), jax.ShapeDtypeStruct(shape, BF16)], [pltpu.VMEM((tm, tn), F32)],
                  ("parallel", "parallel", "arbitrary"), [a, dy])


def _core_index():
    return lax.axis_index("c")


def _chip_index():
    return 2 * lax.axis_index("x") + lax.axis_index("y")


def _rowwise(fn, name, n_rows, ins, outs, accs=(), tm=512):
    n_in, n_out, n_acc = len(ins), len(outs), len(accs)
    in_specs, args = [], []
    for spec in ins:
        if spec[0] == "row":
            _, arr, w, cb = spec
            in_specs.append(pl.BlockSpec((tm, w), lambda i, cb=cb: (i, cb)))
        elif spec[0] == "rowoff":
            _, arr, w, cb, index_fn, span = spec
            in_specs.append(pl.BlockSpec((tm, w), lambda i, cb=cb, index_fn=index_fn, nb=span // tm: (index_fn() * nb + i, cb)))
        elif spec[0] == "rowblk":
            _, arr, w, cb, start = spec
            in_specs.append(pl.BlockSpec((tm, w), lambda i, cb=cb, nb=start // tm: (nb + i, cb)))
        else:
            arr = spec[1]
            in_specs.append(pl.BlockSpec(arr.shape, lambda i: (0, 0)))
        args.append(arr)
    out_specs = [pl.BlockSpec((tm, w), lambda i: (i, 0)) for w, _ in outs]
    out_specs += [pl.BlockSpec((1, w), lambda i: (0, 0)) for w in accs]
    out_shape = [jax.ShapeDtypeStruct((n_rows, w), dt) for w, dt in outs]
    out_shape += [jax.ShapeDtypeStruct((1, w), F32) for w in accs]

    def body(*refs):
        i = pl.program_id(0)
        res = fn(*[r[...] for r in refs[:n_in]])
        for o_ref, r in zip(refs[n_in:n_in + n_out], res[:n_out]):
            o_ref[...] = r.astype(o_ref.dtype)
        for a_ref, r in zip(refs[n_in + n_out:], res[n_out:]):
            @pl.when(i == 0)
            def _(a_ref=a_ref, r=r):
                a_ref[...] = r

            @pl.when(i > 0)
            def _(a_ref=a_ref, r=r):
                a_ref[...] += r

    return pl.pallas_call(
        body, name=name, grid=(n_rows // tm,), in_specs=in_specs, out_specs=out_specs, out_shape=out_shape,
        compiler_params=_cparams(("arbitrary",)))(*args)


EPS = 1e-6


def _sig(x):
    return 1.0 / (1.0 + jnp.exp(-x))


def _colsum(x):
    return jnp.sum(x, axis=0, keepdims=True)


def _rms_fwd_fn(x, g):
    r = lax.rsqrt(jnp.mean(x * x, axis=-1, keepdims=True) + EPS)
    return (x * r * g,)


def _rms_bwd_fn(x, g, dh, dres):
    dh = dh.astype(F32)
    r = lax.rsqrt(jnp.mean(x * x, axis=-1, keepdims=True) + EPS)
    xh = x * r
    dyg = dh * g
    dx = r * (dyg - xh * jnp.mean(dyg * xh, axis=-1, keepdims=True)) + dres
    return dx, _colsum(dh * xh)


def _loss_fn(x, g, t):
    d = x.shape[-1]
    r = lax.rsqrt(jnp.mean(x * x, axis=-1, keepdims=True) + EPS)
    xh = x * r
    err = xh * g - t
    dy = err * (1.0 / d)
    dyg = dy * g
    dx = r * (dyg - xh * jnp.mean(dyg * xh, axis=-1, keepdims=True))
    return dx, _colsum(err * err), _colsum(dy * xh)


def _swiglu_fwd_fn(z):
    f = z.shape[-1] // 2
    z1, z2 = z[:, :f].astype(F32), z[:, f:].astype(F32)
    return (z1 * _sig(z1) * z2,)


def _swiglu_bwd_fn(z, da):
    f = z.shape[-1] // 2
    z1, z2, da = z[:, :f].astype(F32), z[:, f:].astype(F32), da.astype(F32)
    s = _sig(z1)
    dz1 = da * z2 * (s * (1.0 + z1 * (1.0 - s)))
    dz2 = da * (z1 * s)
    return (jnp.concatenate([dz1, dz2], axis=1),)


def _merge_fwd_fn(g0, g1, g2, bg, zs, ya, yc):
    d = ya.shape[-1]
    bg = bg.astype(F32)
    zs = zs.astype(F32)
    ys = zs[:, :d] * _sig(zs[:, d:])
    m = _sig(g0.astype(F32) + bg[:, :d]) * ys
    m = m + _sig(g1.astype(F32) + bg[:, d:2 * d]) * ya.astype(F32)
    m = m + _sig(g2.astype(F32) + bg[:, 2 * d:]) * yc.astype(F32)
    return (m,)


def _merge_bwd_fn(g0, g1, g2, bg, zs, ya, yc, dm):
    d = ya.shape[-1]
    bg = bg.astype(F32)
    zs = zs.astype(F32)
    dm = dm.astype(F32)
    z1, s2 = zs[:, :d], _sig(zs[:, d:])
    ys = z1 * s2
    s0 = _sig(g0.astype(F32) + bg[:, :d])
    s1 = _sig(g1.astype(F32) + bg[:, d:2 * d])
    s3 = _sig(g2.astype(F32) + bg[:, 2 * d:])
    dgl = jnp.concatenate([dm * ys * s0 * (1.0 - s0), dm * ya.astype(F32) * s1 * (1.0 - s1),
                           dm * yc.astype(F32) * s3 * (1.0 - s3)], axis=1)
    dys = dm * s0
    dzs = jnp.concatenate([dys * s2, dys * z1 * s2 * (1.0 - s2)], axis=1)
    return dgl, dzs, dm * s1, dm * s3, _colsum(dgl)


def _combine_fwd_fn(o0, o1, o2, l0, l1, l2):
    m = jnp.maximum(jnp.maximum(l0, l1), l2)
    e0, e1, e2 = jnp.exp(l0 - m), jnp.exp(l1 - m), jnp.exp(l2 - m)
    inv = 1.0 / (e0 + e1 + e2)
    return ((e0 * o0.astype(F32) + e1 * o1.astype(F32) + e2 * o2.astype(F32)) * inv,)


def _combine_bwd_fn(o0, o1, o2, l0, l1, l2, do, jmat):
    m = jnp.maximum(jnp.maximum(l0, l1), l2)
    e0, e1, e2 = jnp.exp(l0 - m), jnp.exp(l1 - m), jnp.exp(l2 - m)
    inv = 1.0 / (e0 + e1 + e2)
    w0, w1, w2 = e0 * inv, e1 * inv, e2 * inv
    do = do.astype(F32)

    def headsum(x):
        return jnp.dot(x, jmat, preferred_element_type=F32, precision=lax.Precision.HIGHEST)

    dw0, dw1, dw2 = headsum(do * o0.astype(F32)), headsum(do * o1.astype(F32)), headsum(do * o2.astype(F32))
    mean = w0 * dw0 + w1 * dw1 + w2 * dw2
    return w0 * do, w1 * do, w2 * do, w0 * (dw0 - mean), w1 * (dw1 - mean), w2 * (dw2 - mean)


ADAM_LR, ADAM_B1, ADAM_B2, ADAM_EPS, ADAM_WD, ADAM_STEP = 0.001, 0.9, 0.999, 1e-08, 0.01, 10


def _adamw_fn(w, g, m, v):
    m = ADAM_B1 * m + (1.0 - ADAM_B1) * g
    v = ADAM_B2 * v + (1.0 - ADAM_B2) * (g * g)
    m_hat = m / (1.0 - ADAM_B1 ** ADAM_STEP)
    v_hat = v / (1.0 - ADAM_B2 ** ADAM_STEP)
    delta = -ADAM_LR * (m_hat / (jnp.sqrt(v_hat) + ADAM_EPS) + ADAM_WD * w)
    return delta, m, v


CONV_WIDTH = 31


def _conv_fwd(proj, cb, w32, conv_b, ln_g, ln_b, bl, c, name, tm=512):
    n = proj.shape[0]
    hp = (CONV_WIDTH - 1) * bl
    nt = n // tm

    def body(ap_ref, gp_ref, a_ref, g_ref, w_ref, cb_ref, lg_ref, lb_ref, hc_ref, hconv_ref, ext):
        i = pl.program_id(0)
        ext[pl.ds(hp, tm), :] = a_ref[...].astype(F32) * _sig(g_ref[...].astype(F32))
        hgp = ap_ref[pl.ds(tm - hp, hp), :].astype(F32) * _sig(gp_ref[pl.ds(tm - hp, hp), :].astype(F32))
        ext[pl.ds(0, hp), :] = jnp.where(i > 0, hgp, 0.0)
        acc = jnp.zeros((tm, c), F32) + cb_ref[...]
        for j in range(CONV_WIDTH):
            acc = acc + w_ref[j:j + 1, :] * ext[pl.ds(j * bl, tm), :]
        hconv_ref[...] = acc.astype(hconv_ref.dtype)
        h = hconv_ref[...].astype(F32)
        mu = jnp.mean(h, axis=-1, keepdims=True)
        xc = h - mu
        var = jnp.mean(xc * xc, axis=-1, keepdims=True)
        hn = xc * lax.rsqrt(var + EPS) * lg_ref[...] + lb_ref[...]
        hc_ref[...] = (hn * _sig(hn)).astype(hc_ref.dtype)

    prev = lambda i, k: (jnp.maximum(i - 1, 0), k)
    par = lambda arr: pl.BlockSpec(arr.shape, lambda i: (0, 0))
    return pl.pallas_call(
        body, name=name, grid=(nt,),
        in_specs=[pl.BlockSpec((tm, c), functools.partial(prev, k=cb)), pl.BlockSpec((tm, c), functools.partial(prev, k=cb + 1)),
                  pl.BlockSpec((tm, c), lambda i: (i, cb)), pl.BlockSpec((tm, c), lambda i: (i, cb + 1)),
                  par(w32), par(conv_b), par(ln_g), par(ln_b)],
        out_specs=[pl.BlockSpec((tm, c), lambda i: (i, 0))] * 2,
        out_shape=[jax.ShapeDtypeStruct((n, c), BF16)] * 2,
        scratch_shapes=[pltpu.VMEM((hp + tm, c), F32)],
        compiler_params=_cparams(("arbitrary",)))(proj, proj, proj, proj, w32, conv_b, ln_g, ln_b)


def _conv_bwd(proj, cb, dhc, hconv, w32, ln_g, ln_b, bl, c, name, tm=512, comm=None):
    n = proj.shape[0]
    hp = (CONV_WIDTH - 1) * bl
    nt = n // tm

    def ln_bwd(d, h, lg, lb):
        d, h = d.astype(F32), h.astype(F32)
        mu = jnp.mean(h, axis=-1, keepdims=True)
        xc = h - mu
        rstd = lax.rsqrt(jnp.mean(xc * xc, axis=-1, keepdims=True) + EPS)
        xh = xc * rstd
        hn = xh * lg + lb
        s = _sig(hn)
        dhn = d * (s * (1.0 + hn * (1.0 - s)))
        dxh = dhn * lg
        dh = rstd * (dxh - jnp.mean(dxh, axis=-1, keepdims=True) - xh * jnp.mean(dxh * xh, axis=-1, keepdims=True))
        return dh, dhn, xh

    def body(ap_ref, gp_ref, a_ref, g_ref, d_ref, dn_ref, h_ref, hn_ref, w_ref, lg_ref, lb_ref,
             dcv_ref, dw_ref, dcb_ref, dlg_ref, dlb_ref, ext_h, ext_d):
        i = pl.program_id(0)
        lg, lb = lg_ref[...], lb_ref[...]
        a, g = a_ref[...].astype(F32), g_ref[...].astype(F32)
        sg = _sig(g)
        ext_h[pl.ds(hp, tm), :] = a * sg
        hgp = ap_ref[pl.ds(tm - hp, hp), :].astype(F32) * _sig(gp_ref[pl.ds(tm - hp, hp), :].astype(F32))
        ext_h[pl.ds(0, hp), :] = jnp.where(i > 0, hgp, 0.0)
        dh, dhn, xh = ln_bwd(d_ref[...], h_ref[...], lg, lb)
        ext_d[pl.ds(0, tm), :] = dh
        dh_n, _, _ = ln_bwd(dn_ref[pl.ds(0, hp), :], hn_ref[pl.ds(0, hp), :], lg, lb)
        ext_d[pl.ds(tm, hp), :] = jnp.where(i < nt - 1, dh_n, 0.0)

        @pl.when(i == 0)
        def _():
            dw_ref[...] = jnp.zeros_like(dw_ref)
            dcb_ref[...] = jnp.zeros_like(dcb_ref)
            dlg_ref[...] = jnp.zeros_like(dlg_ref)
            dlb_ref[...] = jnp.zeros_like(dlb_ref)

        dcb_ref[...] += _colsum(dh)
        dlg_ref[...] += _colsum(dhn * xh)
        dlb_ref[...] += _colsum(dhn)
        dhg = jnp.zeros((tm, c), F32)
        for j in range(CONV_WIDTH):
            dhg = dhg + w_ref[j:j + 1, :] * ext_d[pl.ds((CONV_WIDTH - 1 - j) * bl, tm), :]
            dw_ref[j:j + 1, :] += _colsum(dh * ext_h[pl.ds(j * bl, tm), :])
        dcv_ref[...] = jnp.concatenate([dhg * sg, dhg * a * sg * (1.0 - sg)], axis=1).astype(dcv_ref.dtype)

    prev = lambda i, k: (jnp.maximum(i - 1, 0), k)
    nxt = lambda i: (jnp.minimum(i + 1, nt - 1), 0)
    cur = lambda i: (i, 0)
    par = lambda arr: pl.BlockSpec(arr.shape, lambda i: (0, 0))
    acc = lambda r: pl.BlockSpec((r, c), lambda i: (0, 0))
    in_specs = [pl.BlockSpec((tm, c), functools.partial(prev, k=cb)), pl.BlockSpec((tm, c), functools.partial(prev, k=cb + 1)),
                pl.BlockSpec((tm, c), lambda i: (i, cb)), pl.BlockSpec((tm, c), lambda i: (i, cb + 1)),
                pl.BlockSpec((tm, c), cur), pl.BlockSpec((tm, c), nxt), pl.BlockSpec((tm, c), cur), pl.BlockSpec((tm, c), nxt),
                par(w32), par(ln_g), par(ln_b)]
    out_shape = [jax.ShapeDtypeStruct((n, 2 * c), BF16), jax.ShapeDtypeStruct((32, c), F32)] + [jax.ShapeDtypeStruct((1, c), F32)] * 3
    return _pcall(body, name, (nt,), in_specs, [pl.BlockSpec((tm, 2 * c), cur), acc(32), acc(1), acc(1), acc(1)], out_shape,
                  [pltpu.VMEM((hp + tm, c), F32), pltpu.VMEM((hp + tm, c), F32)], ("arbitrary",),
                  [proj, proj, proj, proj, dhc, dhc, hconv, hconv, w32, ln_g, ln_b], comm)


SSM_CH = 128
_GELU_C = 0.7978845608028654


def _gelu(x):
    return 0.5 * x * (1.0 + jnp.tanh(_GELU_C * (x + 0.044715 * x * x * x)))


def _gelu_grad(x):
    th = jnp.tanh(_GELU_C * (x + 0.044715 * x * x * x))
    return 0.5 * (1.0 + th) + 0.5 * x * (1.0 - th * th) * (_GELU_C * (1.0 + 3.0 * 0.044715 * x * x))


def _ssm_disc(lam_re, lam_im, log_dt, b_re, b_im):
    dt = jnp.exp(log_dt)[:, None]
    mag = jnp.exp(lam_re * dt)
    ab_re = mag * jnp.cos(lam_im * dt)
    ab_im = mag * jnp.sin(lam_im * dt)
    nr, ni = ab_re - 1.0, ab_im
    den = lam_re * lam_re + lam_im * lam_im
    z_re = ((nr * lam_re + ni * lam_im) / den)[..., None]
    z_im = ((ni * lam_re - nr * lam_im) / den)[..., None]
    return ab_re, ab_im, z_re * b_re - z_im * b_im, z_re * b_im + z_im * b_re


def _ssm_pack(ab_re, ab_im, bb_re, bb_im, c_re, c_im):
    g, p, h = bb_re.shape
    gc = SSM_CH // h
    nc = g // gc
    eye = jnp.eye(gc, dtype=F32)
    blk = lambda x: jnp.einsum("qgph,gk->qghkp", x.reshape(nc, gc, p, h), eye).reshape(nc, gc * h, gc * p)
    bbd = jnp.concatenate([blk(bb_re), blk(bb_im)], axis=2).astype(BF16)
    blc = lambda x: jnp.einsum("qghp,gk->qgpkh", x.reshape(nc, gc, h, p), eye).reshape(nc, gc * p, gc * h)
    cdm = jnp.concatenate([blc(c_re), blc(-c_im)], axis=1).astype(BF16)
    a = jnp.concatenate([ab_re.reshape(nc, gc * p), ab_im.reshape(nc, gc * p)], axis=1)
    a8 = jnp.broadcast_to(a[:, None, :], (nc, 8, 2 * gc * p)).reshape(nc * 8, 2 * gc * p)
    return bbd, cdm, a8


def _ssm_unpack(dbb, dcd, da, g, p, h):
    gc = SSM_CH // h
    nc = g // gc
    ph = gc * p
    eye = jnp.eye(gc, dtype=F32)
    dia = lambda x, o: jnp.einsum("qgpkh,gk->" + o, x.reshape(nc, gc, p, gc, h), eye).reshape((g, p, h) if o == "qgph" else (g, h, p))
    das = da.reshape(nc, 8, 2 * ph).sum(axis=1)
    return (das[:, :ph].reshape(g, p), das[:, ph:].reshape(g, p), dia(dbb[:, :ph], "qgph"), dia(dbb[:, ph:], "qgph"),
            dia(dcd[:, :ph], "qghp"), -dia(dcd[:, ph:], "qghp"))


def _ssm_fwd(proj, bbd, cdm, a8, dskip, bl, name, tm=1024, comm=None):
    n = proj.shape[0]
    nc, ch, p2 = bbd.shape
    ph = p2 // 2
    nt = n // tm
    nsub = 8 // bl

    def body(u_ref, bb_ref, cd_ref, a_ref, d_ref, ypre_ref, yg_ref, s_ref, bu, carry):
        t = pl.program_id(1)

        @pl.when(t == 0)
        def _():
            carry[...] = jnp.zeros_like(carry)

        u = u_ref[...]
        bu[...] = jnp.dot(u, bb_ref[0], preferred_element_type=F32)
        a_re, a_im = a_ref[:, :ph], a_ref[:, ph:]
        row = lax.broadcasted_iota(jnp.int32, (8, ph), 0)

        def step(k, c):
            cre, cim = c
            r0 = pl.multiple_of(k * 8, 8)
            bre, bim = bu[pl.ds(r0, 8), :ph], bu[pl.ds(r0, 8), ph:]
            sre, sim = cre, cim
            for sub in range(nsub):
                xre, xim = pltpu.roll(cre, bl, 0), pltpu.roll(cim, bl, 0)
                cre = a_re * xre - a_im * xim + bre
                cim = a_re * xim + a_im * xre + bim
                if sub == 0:
                    sre, sim = cre, cim
                else:
                    sel = row >= sub * bl
                    sre, sim = jnp.where(sel, cre, sre), jnp.where(sel, cim, sim)
            bu[pl.ds(r0, 8), :ph] = sre
            bu[pl.ds(r0, 8), ph:] = sim
            return sre, sim

        cre, cim = lax.fori_loop(0, tm // 8, step, (carry[:, :ph], carry[:, ph:]))
        carry[:, :ph] = cre
        carry[:, ph:] = cim
        s16 = bu[...].astype(BF16)
        s_ref[...] = s16
        y = jnp.dot(s16, cd_ref[0], preferred_element_type=F32) + d_ref[...] * u.astype(F32)
        ypre_ref[...] = y
        yg_ref[...] = _gelu(y).astype(yg_ref.dtype)

    in_specs = [pl.BlockSpec((tm, ch), lambda q, t: (t, q)), pl.BlockSpec((1, ch, p2), lambda q, t: (q, 0, 0)),
                pl.BlockSpec((1, p2, ch), lambda q, t: (q, 0, 0)), pl.BlockSpec((8, p2), lambda q, t: (q, 0)),
                pl.BlockSpec((1, ch), lambda q, t: (0, q))]
    out_specs = [pl.BlockSpec((tm, ch), lambda q, t: (t, q)), pl.BlockSpec((tm, ch), lambda q, t: (t, q)),
                 pl.BlockSpec((tm, p2), lambda q, t: (t, q))]
    out_shape = [jax.ShapeDtypeStruct((n, nc * ch), F32), jax.ShapeDtypeStruct((n, nc * ch), BF16),
                 jax.ShapeDtypeStruct((n, nc * p2), BF16)]
    return _pcall(body, name, (nc, nt), in_specs, out_specs, out_shape, [pltpu.VMEM((tm, p2), F32), pltpu.VMEM((8, p2), F32)],
                  ("parallel", "arbitrary"), [proj, bbd, cdm, a8, dskip], comm)


def _ssm_bwd(dyg, ypre, proj, s_all, cdt, bbt, a8, dskip, bl, name, tm=1024, comm=None):
    n = proj.shape[0]
    nc, ch, p2 = cdt.shape
    ph = p2 // 2
    nt = n // tm
    nsub = 8 // bl
    tn_dims = (((0,), (0,)), ((), ()))

    def body(dyg_ref, ypre_ref, u_ref, s_ref, cdt_ref, bbt_ref, a_ref, d_ref,
             du_ref, dbb_ref, dcd_ref, da_ref, dd_ref, ds, s32, carry):
        t = pl.program_id(1)

        @pl.when(t == 0)
        def _():
            carry[...] = jnp.zeros_like(carry)
            dbb_ref[...] = jnp.zeros_like(dbb_ref)
            dcd_ref[...] = jnp.zeros_like(dcd_ref)
            da_ref[...] = jnp.zeros_like(da_ref)
            dd_ref[...] = jnp.zeros_like(dd_ref)

        dyp = dyg_ref[...].astype(F32) * _gelu_grad(ypre_ref[...])
        u = u_ref[...]
        dd_ref[...] += _colsum(dyp * u.astype(F32))
        dyp16 = dyp.astype(BF16)
        ds[...] = jnp.dot(dyp16, cdt_ref[0], preferred_element_type=F32)
        s16 = s_ref[...]
        s32[...] = s16.astype(F32)
        a_re, a_im = a_ref[:, :ph], a_ref[:, ph:]
        row = lax.broadcasted_iota(jnp.int32, (8, ph), 0)
        back = 8 - bl

        def step(kk, c):
            lre, lim, acr, aci = c
            r0 = pl.multiple_of((tm // 8 - 1 - kk) * 8, 8)
            dre, dim = ds[pl.ds(r0, 8), :ph], ds[pl.ds(r0, 8), ph:]
            sre, sim = s32[pl.ds(r0, 8), :ph], s32[pl.ds(r0, 8), ph:]
            ore, oim, ire, iim = lre, lim, lre, lim
            for sub in range(nsub - 1, -1, -1):
                xre, xim = pltpu.roll(lre, back, 0), pltpu.roll(lim, back, 0)
                lre = a_re * xre + a_im * xim + dre
                lim = a_re * xim - a_im * xre + dim
                if sub == nsub - 1:
                    ore, oim, ire, iim = lre, lim, xre, xim
                else:
                    sel = row < (sub + 1) * bl
                    ore, oim = jnp.where(sel, lre, ore), jnp.where(sel, lim, oim)
                    ire, iim = jnp.where(sel, xre, ire), jnp.where(sel, xim, iim)
            ds[pl.ds(r0, 8), :ph] = ore
            ds[pl.ds(r0, 8), ph:] = oim
            acr = acr + sre * ire + sim * iim
            aci = aci + sre * iim - sim * ire
            return ore, oim, acr, aci

        z = jnp.zeros((8, ph), F32)
        lre, lim, acr, aci = lax.fori_loop(0, tm // 8, step, (carry[:, :ph], carry[:, ph:], z, z))
        carry[:, :ph] = lre
        carry[:, ph:] = lim
        da_ref[:, :ph] += acr
        da_ref[:, ph:] += aci
        lam16 = ds[...].astype(BF16)
        du = jnp.dot(lam16, bbt_ref[0], preferred_element_type=F32) + d_ref[...] * dyp
        du_ref[...] = du.astype(du_ref.dtype)
        dbb_ref[0] += lax.dot_general(lam16, u, tn_dims, preferred_element_type=F32)
        dcd_ref[0] += lax.dot_general(s16, dyp16, tn_dims, preferred_element_type=F32)

    rev = lambda q, t: (nt - 1 - t, q)
    in_specs = [pl.BlockSpec((tm, ch), rev), pl.BlockSpec((tm, ch), rev), pl.BlockSpec((tm, ch), rev),
                pl.BlockSpec((tm, p2), rev), pl.BlockSpec((1, ch, p2), lambda q, t: (q, 0, 0)),
                pl.BlockSpec((1, p2, ch), lambda q, t: (q, 0, 0)), pl.BlockSpec((8, p2), lambda q, t: (q, 0)),
                pl.BlockSpec((1, ch), lambda q, t: (0, q))]
    out_specs = [pl.BlockSpec((tm, ch), rev), pl.BlockSpec((1, p2, ch), lambda q, t: (q, 0, 0)),
                 pl.BlockSpec((1, p2, ch), lambda q, t: (q, 0, 0)), pl.BlockSpec((8, p2), lambda q, t: (q, 0)),
                 pl.BlockSpec((1, ch), lambda q, t: (0, q))]
    out_shape = [jax.ShapeDtypeStruct((n, nc * ch), BF16), jax.ShapeDtypeStruct((nc, p2, ch), F32),
                 jax.ShapeDtypeStruct((nc, p2, ch), F32), jax.ShapeDtypeStruct((nc * 8, p2), F32),
                 jax.ShapeDtypeStruct((1, nc * ch), F32)]
    return _pcall(body, name, (nc, nt), in_specs, out_specs, out_shape,
                  [pltpu.VMEM((tm, p2), F32), pltpu.VMEM((tm, p2), F32), pltpu.VMEM((8, p2), F32)],
                  ("parallel", "arbitrary"), [dyg, ypre, proj, s_all, cdt, bbt, a8, dskip], comm)


_MESH = pl.DeviceIdType.MESH
_HBM = pl.BlockSpec(memory_space=pltpu.HBM)


def _position():
    return lax.axis_index("x"), lax.axis_index("y"), lax.axis_index("c")


def _other_chips(x, y):
    return [((1 - x, y), 2 * (1 - x) + y), ((x, 1 - y), 2 * x + 1 - y), ((1 - x, 1 - y), 2 * (1 - x) + 1 - y)]


def _swap_sibling(v, name):
    def body(v_ref, got_ref, send_sem, recv_sem):
        x, y, c = _position()
        cp = pltpu.make_async_remote_copy(src_ref=v_ref, dst_ref=got_ref, send_sem=send_sem, recv_sem=recv_sem,
                                          device_id=(x, y, 1 - c), device_id_type=_MESH)
        cp.start()
        cp.wait()

    return pl.pallas_call(
        body, name=name, in_specs=[_HBM], out_specs=_HBM, out_shape=jax.ShapeDtypeStruct(v.shape, v.dtype),
        scratch_shapes=[pltpu.SemaphoreType.DMA, pltpu.SemaphoreType.DMA])(v)


def _own_slot(gathered, own):
    return lax.dynamic_update_index_in_dim(gathered, own, _chip_index(), 0)


def _chip_allgather(v, name):
    def body(v_ref, out_ref, send_sems, recv_sems):
        x, y, c = _position()
        me = 2 * x + y
        sends = []
        for k, (chip, idx) in enumerate(_other_chips(x, y)):
            cp = pltpu.make_async_remote_copy(src_ref=v_ref, dst_ref=out_ref.at[me], send_sem=send_sems.at[k],
                                              recv_sem=recv_sems.at[k], device_id=(*chip, c), device_id_type=_MESH)
            cp.start()
            sends.append(cp)
        for k, (chip, idx) in enumerate(_other_chips(x, y)):
            pltpu.make_async_remote_copy(src_ref=v_ref, dst_ref=out_ref.at[idx], send_sem=send_sems.at[k],
                                         recv_sem=recv_sems.at[k], device_id=(*chip, c), device_id_type=_MESH).wait_recv()
        for cp in sends:
            cp.wait_send()

    out = pl.pallas_call(
        body, name=name, in_specs=[_HBM], out_specs=_HBM, out_shape=jax.ShapeDtypeStruct((4,) + tuple(v.shape), v.dtype),
        scratch_shapes=[pltpu.SemaphoreType.DMA((3,)), pltpu.SemaphoreType.DMA((3,))])(v)
    return _own_slot(out, v)


def _remote(src, dst, send_sems, recv_sems, s, device):
    return pltpu.make_async_remote_copy(src_ref=src, dst_ref=dst, send_sem=send_sems.at[s], recv_sem=recv_sems.at[s],
                                        device_id=device, device_id_type=_MESH)


class _Exchange:
    def __init__(self, ins, out_shapes, n_sems, aliases=None):
        self.ins, self.out_shapes, self.n_sems, self.aliases = list(ins), list(out_shapes), n_sems, aliases or {}

    def sem_shapes(self):
        return [pltpu.SemaphoreType.DMA((self.n_sems,)), pltpu.SemaphoreType.DMA((self.n_sems,))]


def _halves(ref, c, axis=0):
    h = ref.shape[axis] // 2
    idx = (slice(None),) * axis
    return ref.at[idx + (pl.ds(c * h, h),)], ref.at[idx + (pl.ds((1 - c) * h, h),)]


class _GatherShards(_Exchange):
    def __init__(self, ws):
        super().__init__(ws, [jax.ShapeDtypeStruct((N_CHIPS,) + tuple(w.shape), w.dtype) for w in ws], 6 * len(ws))

    def start(self, w_refs, out_refs, sems):
        send_sems, recv_sems = sems
        x, y, c = _position()
        me = 2 * x + y
        for i, (w, out) in enumerate(zip(w_refs, out_refs)):
            for k, (chip, idx) in enumerate(_other_chips(x, y)):
                _remote(_halves(w, c)[0], _halves(out.at[me], c)[0], send_sems, recv_sems, 6 * i + k, (*chip, c)).start()

    def finish(self, w_refs, out_refs, sems):
        send_sems, recv_sems = sems
        x, y, c = _position()
        sibling = (x, y, 1 - c)
        others = _other_chips(x, y)
        for i, out in enumerate(out_refs):
            for k, (chip, idx) in enumerate(others):
                landed = _halves(out.at[idx], c)[0]
                _remote(landed, landed, send_sems, recv_sems, 6 * i + k, (*chip, c)).wait_recv()
                _remote(landed, landed, send_sems, recv_sems, 6 * i + 3 + k, sibling).start()
        for i, (w, out) in enumerate(zip(w_refs, out_refs)):
            for k, (chip, idx) in enumerate(others):
                mine, theirs = _halves(out.at[idx], c)
                _remote(theirs, theirs, send_sems, recv_sems, 6 * i + 3 + k, sibling).wait_recv()
                _remote(mine, mine, send_sems, recv_sems, 6 * i + 3 + k, sibling).wait_send()
                _remote(_halves(w, c)[0], mine, send_sems, recv_sems, 6 * i + k, (*chip, c)).wait_send()


class _SwapHalves(_Exchange):
    def __init__(self, gs):
        shapes = [jax.ShapeDtypeStruct((g.shape[0], g.shape[1] // 2) + tuple(g.shape[2:]), g.dtype) for g in gs]
        super().__init__(gs, shapes, N_CHIPS * len(gs))

    def _copies(self, g_refs, out_refs, sems):
        x, y, c = _position()
        return [_remote(_halves(g.at[j], c)[1], out.at[j], sems[0], sems[1], N_CHIPS * i + j, (x, y, 1 - c))
                for i, (g, out) in enumerate(zip(g_refs, out_refs)) for j in range(N_CHIPS)]

    def start(self, g_refs, out_refs, sems):
        for cp in self._copies(g_refs, out_refs, sems):
            cp.start()

    def finish(self, g_refs, out_refs, sems):
        for cp in self._copies(g_refs, out_refs, sems):
            cp.wait()


class _ScatterPieces(_Exchange):
    def __init__(self, ps):
        super().__init__(ps, [jax.ShapeDtypeStruct((3,) + tuple(p.shape[1:]), p.dtype) for p in ps], 3 * len(ps))

    def _copies(self, p_refs, out_refs, sems):
        x, y, c = _position()
        return [_remote(p.at[idx], out.at[k], sems[0], sems[1], 3 * i + k, (*chip, c))
                for i, (p, out) in enumerate(zip(p_refs, out_refs)) for k, (chip, idx) in enumerate(_other_chips(x, y))]

    def start(self, p_refs, out_refs, sems):
        for cp in self._copies(p_refs, out_refs, sems):
            cp.start()

    def finish(self, p_refs, out_refs, sems):
        for cp in self._copies(p_refs, out_refs, sems):
            cp.wait()


class _JoinHalves(_Exchange):
    def __init__(self, rs):
        super().__init__(rs, [jax.ShapeDtypeStruct(r.shape, r.dtype) for r in rs], len(rs), {i: i for i in range(len(rs))})

    def start(self, r_refs, out_refs, sems):
        x, y, c = _position()
        for i, out in enumerate(out_refs):
            _remote(out.at[c], out.at[c], sems[0], sems[1], i, (x, y, 1 - c)).start()

    def finish(self, r_refs, out_refs, sems):
        x, y, c = _position()
        for i, out in enumerate(out_refs):
            _remote(out.at[c], out.at[c], sems[0], sems[1], i, (x, y, 1 - c)).wait_send()
            _remote(out.at[1 - c], out.at[1 - c], sems[0], sems[1], i, (x, y, 1 - c)).wait_recv()


def _run_exchange(ex, name):
    def body(*refs):
        ins, outs, sems = refs[:len(ex.ins)], refs[len(ex.ins):len(ex.ins) + len(ex.out_shapes)], refs[-2:]
        ex.start(ins, outs, sems)
        ex.finish(ins, outs, sems)

    return pl.pallas_call(body, name=name, in_specs=[_HBM] * len(ex.ins), out_specs=[_HBM] * len(ex.out_shapes),
                          out_shape=ex.out_shapes, scratch_shapes=ex.sem_shapes(), input_output_aliases=ex.aliases)(*ex.ins)


def _pcall(body, name, grid, in_specs, out_specs, out_shape, scratch_shapes, semantics, args, comm=None):
    if comm is None:
        return pl.pallas_call(body, name=name, grid=grid, in_specs=in_specs, out_specs=out_specs, out_shape=out_shape,
                              scratch_shapes=scratch_shapes, compiler_params=_cparams(semantics))(*args)
    n_in, n_out, n_scr, ci, co = len(in_specs), len(out_specs), len(scratch_shapes), len(comm.ins), len(comm.out_shapes)

    def wrapped(*refs):
        parts, a = [], 0
        for k in (n_in, ci, n_out, co, n_scr, 2):
            parts.append(refs[a:a + k])
            a += k
        ins, cins, outs, couts, scr, sems = parts
        ids = [pl.program_id(i) for i in range(len(grid))]
        first = functools.reduce(jnp.logical_and, [i == 0 for i in ids])
        last = functools.reduce(jnp.logical_and, [i == g - 1 for i, g in zip(ids, grid)])

        @pl.when(first)
        def _():
            comm.start(cins, couts, sems)

        body(*ins, *outs, *scr)

        @pl.when(last)
        def _():
            comm.finish(cins, couts, sems)

    res = pl.pallas_call(
        wrapped, name=name, grid=grid, in_specs=list(in_specs) + [_HBM] * ci, out_specs=list(out_specs) + [_HBM] * co,
        out_shape=list(out_shape) + comm.out_shapes, scratch_shapes=list(scratch_shapes) + comm.sem_shapes(),
        compiler_params=_cparams(("arbitrary",) * len(grid)))(*args, *comm.ins)
    return res[:n_out], res[n_out:]


ATT_WINDOW = 128
PHASES = 16
_NT = (((1,), (1,)), ((), ()))
_TN = (((0,), (0,)), ((), ()))


PERM_LANES = 512


def _phase_perm(bl):
    t = 16 * PHASES * bl
    col = jnp.arange(t)
    i, r, b = col // (PHASES * bl), (col // bl) % PHASES, col % bl
    return (jnp.arange(t)[:, None] == ((b * PHASES + r) * 16 + i)[None, :]).astype(BF16)


def _to_phase_order(x, bl, col0=0, width=None):
    n = x.shape[0]
    width = width or x.shape[1]
    t = 16 * PHASES * bl
    g = n // bl // PHASES
    tn = min(PERM_LANES, width)

    def body(p_ref, x_ref, o_ref):
        o_ref[...] = jnp.dot(p_ref[...], x_ref[...], preferred_element_type=F32).astype(o_ref.dtype).reshape(o_ref.shape)

    out = pl.pallas_call(
        body, name="to_phase", grid=(n // t, width // tn),
        in_specs=[pl.BlockSpec((t, t), lambda i, j: (0, 0)), pl.BlockSpec((t, tn), lambda i, j: (i, col0 // tn + j))],
        out_specs=pl.BlockSpec((bl * PHASES, 16, tn), lambda i, j: (0, i, j)),
        out_shape=jax.ShapeDtypeStruct((bl * PHASES, g, width), x.dtype),
        compiler_params=_cparams(("parallel", "parallel")))(_phase_perm(bl), x)
    return out.reshape(n, width)


def _from_phase_order(y, bl):
    n, width = y.shape
    t = 16 * PHASES * bl
    g = n // bl // PHASES
    tn = min(PERM_LANES, width)

    def body(p_ref, y_ref, o_ref):
        o_ref[...] = jnp.dot(p_ref[...], y_ref[...].reshape(t, tn), preferred_element_type=F32).astype(o_ref.dtype)

    return pl.pallas_call(
        body, name="from_phase", grid=(n // t, width // tn),
        in_specs=[pl.BlockSpec((t, t), lambda i, j: (0, 0)), pl.BlockSpec((bl * PHASES, 16, tn), lambda i, j: (0, i, j))],
        out_specs=pl.BlockSpec((t, tn), lambda i, j: (i, j)), out_shape=jax.ShapeDtypeStruct((n, width), y.dtype),
        compiler_params=_cparams(("parallel", "parallel")))(_phase_perm(bl).T, y.reshape(bl * PHASES, g, width))


def _att_geometry(p, n, bl):
    g = n // bl // PHASES
    if p == 0:
        return ((bl, PHASES, g), (bl, g // 16), (None, PHASES, 16),
                lambda sh: (lambda b, a: (b, 0, jnp.maximum(a + sh, 0))), 256, 16, lambda ids: ids[1] == 0)
    if p == 1:
        return ((bl, 4, 4, g), (bl, 2, g // 32), (None, 4, 2, 32),
                lambda sh: (lambda b, r, a: (b, 0, r, jnp.maximum(a + sh, 0))), 128, 32, lambda ids: ids[2] == 0)
    return ((bl * PHASES, g), (bl * PHASES // 2,), (2, g), lambda sh: (lambda s: (s, 0)), g, g, None)


def _att_units(p, n, bl):
    g = n // bl // PHASES
    full = slice(None)
    if p == 0:
        return [(full, full)], (PHASES, 16)
    if p == 1:
        return [(full, u, full) for u in range(2)], (4, 32)
    return [(u, full) for u in range(2)], (g,)


def _att_masks(p, qb, chunk):
    def pos(idx):
        return (idx % chunk) * (qb // chunk) + idx // chunk

    dq = pos(lax.broadcasted_iota(jnp.int32, (qb, qb), 0))
    dk = pos(lax.broadcasted_iota(jnp.int32, (qb, qb), 1))
    dist = dq - dk
    return jnp.logical_and(dist >= 0, dist <= ATT_WINDOW), dist + qb <= ATT_WINDOW


def _att_call(p, n, bl, c, body, name, ins, outs):
    prefix, grid, blk, idx_fn, qb, chunk, _ = _att_geometry(p, n, bl)

    def spec(cb, sh):
        f = idx_fn(sh)
        return pl.BlockSpec(blk + (c,), lambda *ids, f=f, cb=cb: f(*ids) + (cb,))

    in_specs = [spec(cb, sh) for _, cb, sh in ins]
    out_specs = [spec(0, 0) for _ in outs]
    out_shape = [jax.ShapeDtypeStruct(prefix + (c,), dt) for dt in outs]
    res = pl.pallas_call(body, name=name, grid=grid, in_specs=in_specs, out_specs=out_specs, out_shape=out_shape,
                         compiler_params=_cparams(("parallel",) * len(grid)))(*[a.reshape(prefix + (a.shape[1],)) for a, _, _ in ins])
    return [r.reshape(n, c) for r in res]


def _att_fwd(p, qkv, qcb, bl, c, heads):
    n = qkv.shape[0]
    _, grid, _, _, qb, chunk, first_fn = _att_geometry(p, n, bl)
    units, unit_shape = _att_units(p, n, bl)
    n_grid = len(grid)
    has_prev = first_fn is not None
    e = c // heads
    scale = e ** -0.5

    def body(*refs):
        if has_prev:
            q_ref, kc_ref, kp_ref, vc_ref, vp_ref, o_ref, l_ref = refs
        else:
            q_ref, kc_ref, vc_ref, o_ref, l_ref = refs
        ids = [pl.program_id(a) for a in range(n_grid)]
        mc, mp = _att_masks(p, qb, chunk)
        if has_prev:
            mp = jnp.logical_and(mp, jnp.logical_not(first_fn(ids)))
        lo = lax.broadcasted_iota(jnp.int32, (qb, 128), 1) < e
        ones = jnp.ones((qb, 128), BF16)
        tiles = [(unit, pl.ds(lt * 128, 128)) for unit in units for lt in range(c // 128)]
        n_t = len(tiles)
        load = lambda r, t: r[tiles[t][0] + (tiles[t][1],)].reshape(qb, 128)
        items = [(t, h) for t in range(n_t) for h in range(2)]
        dot = functools.partial(jnp.dot, preferred_element_type=F32)
        q2 = [load(q_ref, t) for t in range(n_t)]
        kc = [load(kc_ref, t) for t in range(n_t)]
        qm = [jnp.where(lo if h == 0 else jnp.logical_not(lo), q2[t], jnp.zeros_like(q2[t])) for t, h in items]
        sc = [jnp.where(mc, lax.dot_general(qm[i], kc[t], _NT, preferred_element_type=F32) * scale, -jnp.inf)
              for i, (t, h) in enumerate(items)]
        m = [jnp.max(s, axis=1, keepdims=True) for s in sc]
        if has_prev:
            kp = [load(kp_ref, t) for t in range(n_t)]
            sp = [jnp.where(mp, lax.dot_general(qm[i], kp[t], _NT, preferred_element_type=F32) * scale, -jnp.inf)
                  for i, (t, h) in enumerate(items)]
            m = [jnp.maximum(a, jnp.max(s, axis=1, keepdims=True)) for a, s in zip(m, sp)]
        pc = [jnp.exp(s - a).astype(BF16) for s, a in zip(sc, m)]
        vc = [load(vc_ref, t) for t in range(n_t)]
        acc = [dot(pc[i], vc[t]) for i, (t, h) in enumerate(items)]
        den = [dot(x, ones) for x in pc]
        if has_prev:
            pp = [jnp.exp(s - a).astype(BF16) for s, a in zip(sp, m)]
            vp = [load(vp_ref, t) for t in range(n_t)]
            acc = [a + dot(pp[i], vp[t]) for i, ((t, h), a) in enumerate(zip(items, acc))]
            den = [d + dot(x, ones) for d, x in zip(den, pp)]
        oh = [a / d for a, d in zip(acc, den)]
        lh = [a + jnp.log(d) for a, d in zip(m, den)]
        for t, (unit, ls) in enumerate(tiles):
            o_ref[unit + (ls,)] = jnp.where(lo, oh[2 * t], oh[2 * t + 1]).astype(o_ref.dtype).reshape(unit_shape + (128,))
            l_ref[unit + (ls,)] = jnp.where(lo, lh[2 * t], lh[2 * t + 1]).reshape(unit_shape + (128,))

    kcb, vcb = 3, 4
    ins = [(qkv, qcb, 0), (qkv, kcb, 0)] + ([(qkv, kcb, -1)] if has_prev else []) + [(qkv, vcb, 0)] + ([(qkv, vcb, -1)] if has_prev else [])
    return _att_call(p, n, bl, c, body, name=f"att_fwd{p}", ins=ins, outs=[BF16, F32])


def _att_bwd(p, qkv, qcb, o, do, lse, dlse, bl, c, heads):
    n = qkv.shape[0]
    _, grid, _, _, qb, chunk, first_fn = _att_geometry(p, n, bl)
    units, unit_shape = _att_units(p, n, bl)
    n_grid = len(grid)
    has_prev = first_fn is not None
    e = c // heads
    scale = e ** -0.5

    def body(*refs):
        if has_prev:
            q_ref, kc_ref, kp_ref, vc_ref, vp_ref, o_ref, do_ref, l_ref, dl_ref, dq_ref, dkc_ref, dkp_ref, dvc_ref, dvp_ref = refs
        else:
            q_ref, kc_ref, vc_ref, o_ref, do_ref, l_ref, dl_ref, dq_ref, dkc_ref, dvc_ref = refs
        ids = [pl.program_id(a) for a in range(n_grid)]
        mc, mp = _att_masks(p, qb, chunk)
        if has_prev:
            mp = jnp.logical_and(mp, jnp.logical_not(first_fn(ids)))
        lo = lax.broadcasted_iota(jnp.int32, (qb, 128), 1) < e
        tiles = [(unit, pl.ds(lt * 128, 128)) for unit in units for lt in range(c // 128)]
        n_t = len(tiles)
        load = lambda r, t: r[tiles[t][0] + (tiles[t][1],)].reshape(qb, 128)
        items = [(t, h) for t in range(n_t) for h in range(2)]
        nt_dot = lambda a, b: lax.dot_general(a, b, _NT, preferred_element_type=F32)
        tn_dot = lambda a, b: lax.dot_general(a, b, _TN, preferred_element_type=F32)
        dot = functools.partial(jnp.dot, preferred_element_type=F32)

        def store(r, t, v):
            r[tiles[t][0] + (tiles[t][1],)] = v.astype(r.dtype).reshape(unit_shape + (128,))

        sel = [lo if h == 0 else jnp.logical_not(lo) for t, h in items]
        q2, kc, vc, do2 = ([load(r, t) for t in range(n_t)] for r in (q_ref, kc_ref, vc_ref, do_ref))
        qm = [jnp.where(sel[i], q2[t], jnp.zeros_like(q2[t])) for i, (t, h) in enumerate(items)]
        dom = [jnp.where(sel[i], do2[t], jnp.zeros_like(do2[t])) for i, (t, h) in enumerate(items)]
        dod = [do2[t].astype(F32) * load(o_ref, t).astype(F32) for t in range(n_t)]
        lcol = [load(l_ref, t)[:, h * e:h * e + 1] for t, h in items]
        corr = [load(dl_ref, t)[:, h * e:h * e + 1] - jnp.sum(jnp.where(sel[i], dod[t], 0.0), axis=1, keepdims=True)
                for i, (t, h) in enumerate(items)]
        pc = [jnp.exp(jnp.where(mc, nt_dot(qm[i], kc[t]) * scale, -jnp.inf) - lcol[i]) for i, (t, h) in enumerate(items)]
        dsc = [(pc[i] * (nt_dot(dom[i], vc[t]) + corr[i]) * scale).astype(BF16) for i, (t, h) in enumerate(items)]
        pc = [x.astype(BF16) for x in pc]
        dq = [dot(dsc[i], kc[t]) for i, (t, h) in enumerate(items)]
        dkc = [tn_dot(dsc[2 * t], qm[2 * t]) + tn_dot(dsc[2 * t + 1], qm[2 * t + 1]) for t in range(n_t)]
        dvc = [tn_dot(pc[2 * t], dom[2 * t]) + tn_dot(pc[2 * t + 1], dom[2 * t + 1]) for t in range(n_t)]
        if has_prev:
            kp, vp = ([load(r, t) for t in range(n_t)] for r in (kp_ref, vp_ref))
            pp = [jnp.exp(jnp.where(mp, nt_dot(qm[i], kp[t]) * scale, -jnp.inf) - lcol[i]) for i, (t, h) in enumerate(items)]
            dsp = [(pp[i] * (nt_dot(dom[i], vp[t]) + corr[i]) * scale).astype(BF16) for i, (t, h) in enumerate(items)]
            pp = [x.astype(BF16) for x in pp]
            dq = [a + dot(dsp[i], kp[t]) for i, ((t, h), a) in enumerate(zip(items, dq))]
            dkp = [tn_dot(dsp[2 * t], qm[2 * t]) + tn_dot(dsp[2 * t + 1], qm[2 * t + 1]) for t in range(n_t)]
            dvp = [tn_dot(pp[2 * t], dom[2 * t]) + tn_dot(pp[2 * t + 1], dom[2 * t + 1]) for t in range(n_t)]
        for t in range(n_t):
            store(dq_ref, t, jnp.where(lo, dq[2 * t], dq[2 * t + 1]))
            store(dkc_ref, t, dkc[t])
            store(dvc_ref, t, dvc[t])
            if has_prev:
                store(dkp_ref, t, dkp[t])
                store(dvp_ref, t, dvp[t])

    kcb, vcb = 3, 4
    ins = [(qkv, qcb, 0), (qkv, kcb, 0)] + ([(qkv, kcb, -1)] if has_prev else []) + [(qkv, vcb, 0)] + ([(qkv, vcb, -1)] if has_prev else [])
    ins += [(o, 0, 0), (do, 0, 0), (lse, 0, 0), (dlse, 0, 0)]
    res = _att_call(p, n, bl, c, body, name=f"att_bwd{p}", ins=ins, outs=[BF16] * (5 if has_prev else 3))
    if has_prev:
        dq, dkc, dkp, dvc, dvp = res
        return dq, dkc, dkp, dvc, dvp
    dq, dkc, dvc = res
    return dq, dkc, None, dvc, None


def _att_fold_prev(p, cur, prv, bl):
    if prv is None:
        return cur.astype(F32)
    n, c = cur.shape
    prefix, _, _, _, _, chunk, _ = _att_geometry(p, n, bl)
    v = prv.reshape(prefix + (c,)).astype(F32)
    shifted = jnp.concatenate([v[..., chunk:, :], jnp.zeros_like(v[..., :chunk, :])], axis=-2)
    return cur.astype(F32) + shifted.reshape(n, c)


def _attention_fwd(proj, c, bl, heads):
    n = proj.shape[0]
    qkv = _to_phase_order(proj, bl, col0=c, width=5 * c)
    outs = [_att_fwd(p, qkv, p, bl, c, heads) for p in range(3)]
    ins = [("row", o, c, 0) for o, _ in outs] + [("row", l, c, 0) for _, l in outs]
    o, = _rowwise(_combine_fwd_fn, "comb_fwd", n, ins, [(c, BF16)])
    return _from_phase_order(o, bl), (qkv, outs)


def _attention_bwd(do_tb, saved, bl, heads):
    qkv, outs = saved
    n, c = do_tb.shape
    e = c // heads
    lane = jnp.arange(c) // e
    jmat = (lane[:, None] == lane[None, :]).astype(F32)
    do = _to_phase_order(do_tb, bl)
    ins = [("row", o, c, 0) for o, _ in outs] + [("row", l, c, 0) for _, l in outs] + [("row", do, c, 0), ("par", jmat)]
    res = _rowwise(_combine_bwd_fn, "comb_bwd", n, ins, [(c, BF16)] * 3 + [(c, F32)] * 3)
    dqs, dk, dv = [], 0.0, 0.0
    for p in range(3):
        dq, dkc, dkp, dvc, dvp = _att_bwd(p, qkv, p, outs[p][0], res[p], outs[p][1], res[3 + p], bl, c, heads)
        dqs.append(dq)
        dk = dk + _att_fold_prev(p, dkc, dkp, bl)
        dv = dv + _att_fold_prev(p, dvc, dvp, bl)
    dqkv = jnp.concatenate(dqs + [dk.astype(BF16), dv.astype(BF16)], axis=1)
    return _from_phase_order(dqkv, bl)


ATT_HEADS = 8
SSM_GROUPS, SSM_STATE, SSM_GROUP = 32, 64, 16


def _row(v):
    return v.reshape(1, -1)


def _carried(result, carry, key, hidden):
    if carry.get(key) is None:
        return result
    result, hidden[key] = result
    return result


def _layer_fwd(x, w, p, bl, carry, late=None):
    n, d = x.shape
    c = d // 2
    hidden = {}
    h, = _rowwise(_rms_fwd_fn, "rms_fwd", n, [("row", x, d, 0), ("par", _row(p["norm1_g"]))], [(d, BF16)])
    proj = _carried(_mm(h, w["w_in"], "nn", BF16, "mm_in", comm=carry.get("mm_in")), carry, "mm_in", hidden)
    if late is not None:
        w = dict(w, **late(hidden["mm_in"]))
    disc, disc_vjp = jax.vjp(_ssm_disc, p["ssm_lambda_re"], p["ssm_lambda_im"], p["ssm_log_dt"], p["ssm_b_re"], p["ssm_b_im"])
    bbd, cdm, a8 = _ssm_pack(*disc, p["ssm_c_re"], p["ssm_c_im"])
    ypre, yg, s_all = _carried(_ssm_fwd(proj, bbd, cdm, a8, _row(p["ssm_d"]), bl, "ssm_fwd", comm=carry.get("ssm_fwd")),
                               carry, "ssm_fwd", hidden)
    zs = _mm(yg, w["w_ssm_glu"], "nn", BF16, "mm_glu")
    o, att = _attention_fwd(proj, c, bl, ATT_HEADS)
    ya = _mm(o, w["w_att_up"], "nn", BF16, "mm_att")
    w32 = jnp.concatenate([p["conv_w"], jnp.zeros((1, c), F32)], axis=0)
    hc, hconv = _conv_fwd(proj, 6, w32, _row(p["conv_b"]), _row(p["conv_ln_g"]), _row(p["conv_ln_b"]), bl, c, "conv_fwd")
    yc = _mm(hc, w["w_conv_pw2"], "nn", BF16, "mm_pw2")
    gates = [("row", proj, d, 4), ("row", proj, d, 5), ("row", proj, d, 6), ("par", _row(p["b_gate"]))]
    branches = [("row", zs, 2 * d, 0), ("row", ya, d, 0), ("row", yc, d, 0)]
    merged, = _rowwise(_merge_fwd_fn, "merge_fwd", n, gates + branches, [(d, BF16)])
    xm = _mm(merged, w["w_out"], "nn", F32, "mm_out", res=x)
    h2, = _rowwise(_rms_fwd_fn, "rms_fwd", n, [("row", xm, d, 0), ("par", _row(p["norm2_g"]))], [(d, BF16)])
    z = _carried(_mm(h2, w["w_ffn_in"], "nn", BF16, "mm_ffn_in", comm=carry.get("mm_ffn_in")), carry, "mm_ffn_in", hidden)
    f = z.shape[1] // 2
    a, = _rowwise(_swiglu_fwd_fn, "swiglu_fwd", n, [("row", z, 2 * f, 0)], [(f, BF16)], tm=256)
    xo = _mm(a, w["w_ffn_out"], "nn", F32, "mm_ffn_out", res=xm)
    saved = dict(x=x, h=h, proj=proj, disc_vjp=disc_vjp, bbd=bbd, cdm=cdm, a8=a8, ypre=ypre, yg=yg, s_all=s_all, zs=zs, o=o,
                 att=att, ya=ya, w32=w32, hc=hc, hconv=hconv, yc=yc, gates=gates, branches=branches, merged=merged, xm=xm,
                 h2=h2, z=z, a=a)
    return xo, saved, hidden, w


def _layer_bwd(dxo, s, w, p, bl, carry):
    n, d = dxo.shape
    c = d // 2
    g, bufs, hidden = {}, {}, {}
    f = s["a"].shape[1]

    def dw(key, a, dy, name):
        bufs[key] = _mm_dw(a, dy, name, 1 if key in ROW_SHARDED else N_CHIPS)

    da = _mm(dxo, w["w_ffn_out"], "nt", BF16, "mm_ffn_out_dx")
    dw("w_ffn_out", s["a"], dxo, "mm_ffn_out_dw")
    dz, = _rowwise(_swiglu_bwd_fn, "swiglu_bwd", n, [("row", s["z"], 2 * f, 0), ("row", da, f, 0)], [(2 * f, BF16)], tm=256)
    dh2 = _mm(dz, w["w_ffn_in"], "nt", F32, "mm_ffn_in_dx")
    dw("w_ffn_in", s["h2"], dz, "mm_ffn_in_dw")
    dxm, dg2 = _rowwise(_rms_bwd_fn, "rms_bwd", n, [("row", s["xm"], d, 0), ("par", _row(p["norm2_g"])), ("row", dh2, d, 0),
                                                   ("row", dxo, d, 0)], [(d, F32)], [d])
    g["norm2_g"] = dg2[0]
    dmerged = _mm(dxm, w["w_out"], "nt", BF16, "mm_out_dx")
    dw("w_out", s["merged"], dxm, "mm_out_dw")
    dgl, dzs, dya, dyc, dbg = _rowwise(_merge_bwd_fn, "merge_bwd", n, s["gates"] + s["branches"] + [("row", dmerged, d, 0)],
                                       [(3 * d, BF16), (2 * d, BF16), (d, BF16), (d, BF16)], [3 * d], tm=256)
    g["b_gate"] = dbg[0]
    dyg = _mm(dzs, w["w_ssm_glu"], "nt", BF16, "mm_glu_dx")
    dw("w_ssm_glu", s["yg"], dzs, "mm_glu_dw")
    du, dbb, dcd, dab, dd = _carried(
        _ssm_bwd(dyg, s["ypre"], s["proj"], s["s_all"], s["cdm"].transpose(0, 2, 1), s["bbd"].transpose(0, 2, 1), s["a8"],
                 _row(p["ssm_d"]), bl, "ssm_bwd", comm=carry.get("ssm_bwd")), carry, "ssm_bwd", hidden)
    dab_re, dab_im, dbb_re, dbb_im, g["ssm_c_re"], g["ssm_c_im"] = _ssm_unpack(dbb, dcd, dab, SSM_GROUPS, SSM_STATE, SSM_GROUP)
    (g["ssm_lambda_re"], g["ssm_lambda_im"], g["ssm_log_dt"], g["ssm_b_re"],
     g["ssm_b_im"]) = s["disc_vjp"]((dab_re, dab_im, dbb_re, dbb_im))
    g["ssm_d"] = dd[0]
    do = _mm(dya, w["w_att_up"], "nt", BF16, "mm_att_dx")
    dw("w_att_up", s["o"], dya, "mm_att_dw")
    dqkv = _attention_bwd(do, s["att"], bl, ATT_HEADS)
    dhc = _mm(dyc, w["w_conv_pw2"], "nt", BF16, "mm_pw2_dx")
    dw("w_conv_pw2", s["hc"], dyc, "mm_pw2_dw")
    dcv, dcw, dcb, dlg, dlb = _carried(
        _conv_bwd(s["proj"], 6, dhc, s["hconv"], s["w32"], _row(p["conv_ln_g"]), _row(p["conv_ln_b"]), bl, c, "conv_bwd",
                  comm=carry.get("conv_bwd")), carry, "conv_bwd", hidden)
    g["conv_w"], g["conv_b"], g["conv_ln_g"], g["conv_ln_b"] = dcw, dcb[0], dlg[0], dlb[0]
    dproj = jnp.concatenate([du, dqkv, dcv, dgl], axis=1)
    dh = _mm(dproj, w["w_in"], "nt", F32, "mm_in_dx")
    dw("w_in", s["h"], dproj, "mm_in_dw")
    dx, dg1 = _rowwise(_rms_bwd_fn, "rms_bwd", n, [("row", s["x"], d, 0), ("par", _row(p["norm1_g"])), ("row", dh, d, 0),
                                                  ("row", dxm, d, 0)], [(d, F32)], [d])
    g["norm1_g"] = dg1[0]
    return dx, g, bufs, hidden


WEIGHTS = ['norm1_g', 'w_in', 'b_gate', 'ssm_lambda_re', 'ssm_lambda_im', 'ssm_log_dt', 'ssm_b_re', 'ssm_b_im', 'ssm_c_re',
           'ssm_c_im', 'ssm_d', 'w_ssm_glu', 'w_att_up', 'conv_w', 'conv_b', 'conv_ln_g', 'conv_ln_b', 'w_conv_pw2', 'w_out',
           'norm2_g', 'w_ffn_in', 'w_ffn_out', 'final_g']
BIG = ['w_in', 'w_ssm_glu', 'w_att_up', 'w_conv_pw2', 'w_out', 'w_ffn_in', 'w_ffn_out']
ROW_SHARDED = ('w_out', 'w_ffn_out')
SMALL = [k for k in WEIGHTS if k not in BIG]
LANES = 1024
N_CHIPS = 4
ROW_TILE_BYTES = 36 * 1024 * 1024
MIN_SHARD_TILE = 1024


def _pad_rows(a, rows):
    return jnp.concatenate([a, jnp.zeros((rows - a.shape[0],) + a.shape[1:], a.dtype)], axis=0) if rows > a.shape[0] else a


def _row_tile(rows, width, n_arrays):
    best = 16
    for t in range(16, rows + 1, 16):
        if rows % t == 0 and t * width * 4 * n_arrays * 2 <= ROW_TILE_BYTES:
            best = t
    return best


def _flat_fn(fn, name, ins, n_out, rows):
    return _rowwise(fn, name, rows, [("row", a, LANES, 0) for a in ins], [(LANES, F32)] * n_out, tm=rows)


def _reduce_prepare(bufs):
    landed = _run_exchange(_SwapHalves([b16 for _, b16 in bufs]), "rs_swap")
    p32s, p16s = [], []
    for (b32, _), la in zip(bufs, landed):
        s, m, cs = b32.shape
        h = m // 2
        tm = _row_tile(h, cs, 4)

        def body(g_ref, l_ref, o32, o16):
            r = g_ref[...] + l_ref[...].astype(F32)
            o32[...] = r
            o16[...] = r.astype(BF16)

        piece = pl.BlockSpec((None, tm, cs), lambda j, i: (j, i, 0))
        mine = pl.BlockSpec((None, None, tm, cs), lambda j, i: (j, _core_index(), i, 0))
        p32, p16 = pl.pallas_call(
            body, name="rs_add", grid=(s, h // tm), in_specs=[mine, piece], out_specs=[piece, piece],
            out_shape=[jax.ShapeDtypeStruct((s, h, cs), F32), jax.ShapeDtypeStruct((s, h, cs), BF16)],
            compiler_params=_cparams(("parallel", "parallel")))(b32.reshape(s, 2, h, cs), la)
        p32s.append(p32)
        p16s.append(p16)
    return p32s, p16s


def _reduce_finish(p32s, arrived):
    reduced = []
    for p32, lb in zip(p32s, arrived):
        _, h, cs = lb.shape
        tm = _row_tile(h, cs, 5)

        def body(p_ref, a_ref, b_ref, c_ref, o_ref):
            o_ref[...] = ((p_ref[...] + a_ref[...].astype(F32)) + b_ref[...].astype(F32)) + c_ref[...].astype(F32)

        mine = pl.BlockSpec((None, tm, cs), lambda i: (_chip_index(), i, 0))
        other = [pl.BlockSpec((None, tm, cs), lambda i, k=k: (k, i, 0)) for k in range(3)]
        half = pl.BlockSpec((None, tm, cs), lambda i: (_core_index(), i, 0))
        reduced.append(pl.pallas_call(
            body, name="rs_sum", grid=(h // tm,), in_specs=[mine] + other, out_specs=half,
            out_shape=jax.ShapeDtypeStruct((2, h, cs), F32), compiler_params=_cparams(("parallel",)))(p32, lb, lb, lb))
    joined = _run_exchange(_JoinHalves(reduced), "rs_gather")
    return [j.reshape(2 * j.shape[1], j.shape[2]) for j in joined]


def _adamw_layers(w, g_layers, m, v):
    depth, rows, cs = w.shape
    tm = _row_tile(rows, cs, 8)
    nb = rows // tm

    def body(*refs):
        w_ref, m_ref, v_ref = refs[:3]
        g_refs = refs[3:3 + depth]
        go_ref, d_ref, mo_ref, vo_ref = refs[3 + depth:]
        layer = pl.program_id(0)
        g = g_refs[0][...]
        for l in range(1, depth):
            g = jnp.where(layer == l, g_refs[l][...], g)
        delta, mo, vo = _adamw_fn(w_ref[...], g, m_ref[...], v_ref[...])
        go_ref[...], d_ref[...], mo_ref[...], vo_ref[...] = g, delta, mo, vo

    stacked = pl.BlockSpec((None, tm, cs), lambda l, i: (l, i, 0))
    g_specs = [pl.BlockSpec((tm, cs), lambda l, i, k=k: (jnp.where(l == k, i, jnp.where(l < k, 0, nb - 1)), 0)) for k in range(depth)]
    return pl.pallas_call(
        body, name="adamw", grid=(depth, nb), in_specs=[stacked] * 3 + g_specs, out_specs=[stacked] * 4,
        out_shape=[jax.ShapeDtypeStruct(w.shape, F32)] * 4, compiler_params=_cparams(("arbitrary", "arbitrary")))(w, m, v, *g_layers)


def _sum4_fn(a, b, c, d):
    return (((a.astype(F32) + b.astype(F32)) + c.astype(F32)) + d.astype(F32),)


def _add2_fn(a, b):
    return (a + b,)


def kernel(x, norm1_g, w_in, b_gate, ssm_lambda_re, ssm_lambda_im, ssm_log_dt, ssm_b_re, ssm_b_im, ssm_c_re, ssm_c_im, ssm_d, w_ssm_glu, w_att_up, conv_w, conv_b, conv_ln_g, conv_ln_b, w_conv_pw2, w_out, norm2_g, w_ffn_in, w_ffn_out, final_g, loss_target, m_norm1_g, m_w_in, m_b_gate, m_ssm_lambda_re, m_ssm_lambda_im, m_ssm_log_dt, m_ssm_b_re, m_ssm_b_im, m_ssm_c_re, m_ssm_c_im, m_ssm_d, m_w_ssm_glu, m_w_att_up, m_conv_w, m_conv_b, m_conv_ln_g, m_conv_ln_b, m_w_conv_pw2, m_w_out, m_norm2_g, m_w_ffn_in, m_w_ffn_out, m_final_g, v_norm1_g, v_w_in, v_b_gate, v_ssm_lambda_re, v_ssm_lambda_im, v_ssm_log_dt, v_ssm_b_re, v_ssm_b_im, v_ssm_c_re, v_ssm_c_im, v_ssm_d, v_w_ssm_glu, v_w_att_up, v_conv_w, v_conv_b, v_conv_ln_g, v_conv_ln_b, v_w_conv_pw2, v_w_out, v_norm2_g, v_w_ffn_in, v_w_ffn_out, v_final_g):
    args = dict(locals())
    wts = {k: args[k] for k in WEIGHTS}
    mom = {k: args["m_" + k] for k in WEIGHTS}
    var = {k: args["v_" + k] for k in WEIGHTS}
    bl, seq, d = x.shape
    n = bl * seq
    depth = norm1_g.shape[0]
    cx, cy, cc = _position()
    me = 2 * cx + cy

    assert depth == 2, "the exchanges of layer 1 are hidden behind layer 0's kernels"
    first = BIG[:1]
    rest = BIG[1:]

    shards = lambda keys, l: [wts[k][l].astype(BF16) for k in keys]

    def whole(keys, gathered):
        out = {}
        for k, a in zip(keys, gathered):
            _, ks, cs = a.shape
            if k in ROW_SHARDED:
                out[k] = a.reshape(N_CHIPS * ks, cs)
            elif cs < MIN_SHARD_TILE:
                out[k] = a.transpose(1, 0, 2).reshape(ks, N_CHIPS * cs)
            else:
                out[k] = a
        return out

    fill = lambda gathered, own: [_own_slot(g, o) for g, o in zip(gathered, own)]
    own0 = shards(first, 0) + [conv_w]
    gathered = fill(_run_exchange(_GatherShards(own0), "gather_weights"), own0)
    conv_full = gathered[-1].transpose(1, 2, 0, 3).reshape(depth, CONV_WIDTH, -1)
    params = lambda l: dict({k: wts[k][l] for k in SMALL if k not in ("final_g", "conv_w")}, conv_w=conv_full[l])

    to_rows = lambda t: t.transpose(1, 0, 2).reshape(n, d)
    own = {"mm_in": shards(rest, 0), "ssm_fwd": shards(first, 1), "mm_ffn_in": shards(rest, 1)}
    xs, s0, hidden, w0 = _layer_fwd(to_rows(x), whole(first, gathered[:-1]), params(0), bl, {k: _GatherShards(v) for k, v in own.items()},
                                    late=lambda got: whole(rest, fill(got, own["mm_in"])))
    w1 = dict(whole(first, fill(hidden["ssm_fwd"], own["ssm_fwd"])), **whole(rest, fill(hidden["mm_ffn_in"], own["mm_ffn_in"])))
    full = [w0, w1]
    xs, s1, _, _ = _layer_fwd(xs, full[1], params(1), bl, {})
    dx, sq, dgf = _rowwise(_loss_fn, "loss_head", n, [("row", xs, d, 0), ("par", _row(final_g)), ("row", to_rows(loss_target), d, 0)],
                           [(d, F32)], [d, d])
    loss = lax.psum(0.5 * jnp.sum(sq) / d, ("x", "y", "c"))

    pieces = lambda bufs, keys: [tuple(b.reshape(N_CHIPS, -1, b.shape[-1]) for b in bufs[k]) for k in keys]
    dx, g1, bufs1, _ = _layer_bwd(dx, s1, full[1], params(1), bl, {})
    p32_1, p16_1 = _reduce_prepare(pieces(bufs1, BIG))
    dx, g0, bufs0, hidden = _layer_bwd(dx, s0, full[0], params(0), bl,
                                       {"ssm_bwd": _ScatterPieces(p16_1[:1]), "conv_bwd": _ScatterPieces(p16_1[1:])})
    red1 = _reduce_finish(p32_1, list(hidden["ssm_bwd"]) + list(hidden["conv_bwd"]))
    p32_0, p16_0 = _reduce_prepare(pieces(bufs0, BIG))
    red0 = _reduce_finish(p32_0, _run_exchange(_ScatterPieces(p16_0), "rs_scatter"))
    grads = {"final_g": dgf[0]}
    for k in SMALL:
        if k != "final_g":
            grads[k] = jnp.stack([g0[k], g1[k]])
    grad_x = dx.reshape(seq, bl, d).transpose(1, 0, 2)
    outs = {}
    for k, r0, r1 in zip(BIG, red0, red1):
        for tag, a in zip(("grad", "delta", "m", "v"), _adamw_layers(wts[k], [r0, r1], mom[k], var[k])):
            outs[tag, k] = a

    def flat1(t):
        v = jnp.concatenate([t[k].reshape(-1) for k in SMALL])
        rows = -(-v.size // (8 * LANES)) * 8
        return _pad_rows(v, rows * LANES).reshape(rows, LANES), rows

    def unflat1(flat, shapes):
        out, off, v = {}, 0, flat.reshape(-1)
        for k in SMALL:
            size = math.prod(shapes[k])
            out[k] = v[off:off + size].reshape(shapes[k])
            off += size
        return out

    grads["conv_w"] = grads["conv_w"][:, :CONV_WIDTH]
    gs, rows = flat1(grads)
    chip_sum, = _flat_fn(_add2_fn, "ar_add", [gs, _swap_sibling(gs, "ar_swap")], 1, rows)
    slots = _chip_allgather(chip_sum, "ar_gather")
    gs_red, = _flat_fn(_sum4_fn, "ar_sum", [slots[j] for j in range(N_CHIPS)], 1, rows)
    g_sm = unflat1(gs_red, {k: grads[k].shape for k in SMALL})
    cs = conv_w.shape[2]
    g_sm["conv_w"] = lax.dynamic_slice_in_dim(g_sm["conv_w"], me * cs, cs, axis=2)
    (w1, rows), (g1, _), (m1, _), (v1, _) = flat1(wts), flat1(g_sm), flat1(mom), flat1(var)
    sm_out = _flat_fn(_adamw_fn, "adamw_small", [w1, g1, m1, v1], 3, rows)
    shapes = {k: wts[k].shape for k in SMALL}
    for tag, a in zip(("delta", "m", "v"), sm_out):
        for k, t in unflat1(a, shapes).items():
            outs[tag, k] = t
    for k in SMALL:
        outs["grad", k] = g_sm[k]
    return (loss, grad_x, *[outs["grad", k] for k in WEIGHTS], *[outs["delta", k] for k in WEIGHTS],
            *[outs["m", k] for k in WEIGHTS], *[outs["v", k] for k in WEIGHTS])
```

```python
import functools
import math

import jax
import jax.numpy as jnp
from jax import lax
from jax.experimental import pallas as pl
from jax.experimental.pallas import tpu as pltpu

F32 = jnp.float32
BF16 = jnp.bfloat16
VMEM_LIMIT = 56 * 1024 * 1024


def _cparams(sem):
    return pltpu.CompilerParams(dimension_semantics=sem, vmem_limit_bytes=VMEM_LIMIT)


_DIMS = {"nn": (((1,), (0,)), ((), ())), "nt": (((1,), (1,)), ((), ())), "tn": (((0,), (0,)), ((), ()))}


MM_ROWS = 1024
MM_DW_VMEM_BYTES = 44 * 1024 * 1024


def _div_tile(n, cap):
    best = None
    for t in range(128, min(n, cap) + 1, 128):
        if n % t == 0:
            best = t
    return best or n


def _mm(a, b, form, out_dtype, name, res=None, comm=None):
    sharded = b.ndim == 3
    kdim, cs = b.shape[-2], b.shape[-1]
    s = b.shape[0] if sharded else 1
    m = a.shape[0]
    tm = MM_ROWS if m % MM_ROWS == 0 else _div_tile(m, MM_ROWS)
    if form == "nn":
        n, kd = s * cs, kdim
        tn, tk = _div_tile(cs, 1792), _div_tile(kdim, 2048)
        per = cs // tn
        b_blk = (tk, tn)
        b_idx = (lambda i, j, k: (j // per, k, j % per)) if sharded else (lambda i, j, k: (k, j))
    else:
        n, kd = kdim, s * cs
        tn, tk = _div_tile(kdim, 1408), _div_tile(cs, 1792)
        per = cs // tk
        b_blk = (tn, tk)
        b_idx = (lambda i, j, k: (k // per, j, k % per)) if sharded else (lambda i, j, k: (j, k))
    nk = kd // tk
    a_spec = pl.BlockSpec((tm, tk), lambda i, j, k: (i, k))
    b_spec = pl.BlockSpec(((None,) + b_blk) if sharded else b_blk, b_idx)
    o_spec = pl.BlockSpec((tm, tn), lambda i, j, k: (i, j))
    dims = _DIMS[form]

    def body(*refs):
        a_ref, b_ref = refs[:2]
        r_ref = refs[2] if res is not None else None
        o_ref = refs[3] if res is not None else refs[2]
        p = lax.dot_general(a_ref[...].astype(BF16), b_ref[...], dims, preferred_element_type=F32)

        def finish(r):
            if r_ref is not None:
                r = r + r_ref[...]
            o_ref[...] = r.astype(out_dtype)

        if nk == 1:
            finish(p)
            return
        acc = refs[-1]
        k = pl.program_id(2)

        @pl.when(k == 0)
        def _():
            acc[...] = p

        @pl.when(k > 0)
        def _():
            acc[...] += p

        @pl.when(k == nk - 1)
        def _():
            finish(acc[...])

    ins = [a, b] + ([] if res is None else [res])
    in_specs = [a_spec, b_spec] + ([] if res is None else [o_spec])
    out = _pcall(body, name, (m // tm, n // tn, nk), in_specs, [o_spec], [jax.ShapeDtypeStruct((m, n), out_dtype)],
                 [pltpu.VMEM((tm, tn), F32)] if nk > 1 else [], ("parallel", "parallel", "arbitrary"), ins, comm)
    return out[0] if comm is None else (out[0][0], out[1])


def _mm_dw(a, dy, name, shards):
    r, m = a.shape
    c = dy.shape[1]
    cs = c // shards
    tm, tn = _div_tile(m, 1408), _div_tile(cs, 1408)
    fixed = tm * tn * (4 + 2 * (4 + 2))
    per_row = 2 * (tm * a.dtype.itemsize + tn * dy.dtype.itemsize)
    tk = max(t for t in (256, 512, 1024, 2048) if r % t == 0 and (t == 256 or fixed + t * per_row <= MM_DW_VMEM_BYTES))
    per = cs // tn
    nk = r // tk

    def body(a_ref, b_ref, o32, o16, acc):
        k = pl.program_id(2)
        p = lax.dot_general(a_ref[...].astype(BF16), b_ref[...].astype(BF16), _DIMS["tn"], preferred_element_type=F32)

        @pl.when(k == 0)
        def _():
            acc[...] = p

        @pl.when(k > 0)
        def _():
            acc[...] += p

        @pl.when(k == nk - 1)
        def _():
            o32[...] = acc[...]
            o16[...] = acc[...].astype(BF16)

    o_spec = pl.BlockSpec((None, tm, tn), lambda i, j, k: (j // per, i, j % per))
    shape = (shards, m, cs)
    in_specs = [pl.BlockSpec((tk, tm), lambda i, j, k: (k, i)), pl.BlockSpec((tk, tn), lambda i, j, k: (k, j))]
    return _pcall(body, name, (m // tm, c // tn, nk), in_specs, [o_spec, o_spec],
                  [jax.ShapeDtypeStruct(shape, F32), jax.ShapeDtypeStruct(shape, BF16)], [pltpu.VMEM((tm, tn), F32)],
                  ("parallel", "parallel", "arbitrary"), [a, dy])


def _core_index():
    return lax.axis_index("c")


def _chip_index():
    return 2 * lax.axis_index("x") + lax.axis_index("y")


def _rowwise(fn, name, n_rows, ins, outs, accs=(), tm=512):
    n_in, n_out, n_acc = len(ins), len(outs), len(accs)
    in_specs, args = [], []
    for spec in ins:
        if spec[0] == "row":
            _, arr, w, cb = spec
            in_specs.append(pl.BlockSpec((tm, w), lambda i, cb=cb: (i, cb)))
        elif spec[0] == "rowoff":
            _, arr, w, cb, index_fn, span = spec
            in_specs.append(pl.BlockSpec((tm, w), lambda i, cb=cb, index_fn=index_fn, nb=span // tm: (index_fn() * nb + i, cb)))
        elif spec[0] == "rowblk":
            _, arr, w, cb, start = spec
            in_specs.append(pl.BlockSpec((tm, w), lambda i, cb=cb, nb=start // tm: (nb + i, cb)))
        else:
            arr = spec[1]
            in_specs.append(pl.BlockSpec(arr.shape, lambda i: (0, 0)))
        args.append(arr)
    out_specs = [pl.BlockSpec((tm, w), lambda i: (i, 0)) for w, _ in outs]
    out_specs += [pl.BlockSpec((1, w), lambda i: (0, 0)) for w in accs]
    out_shape = [jax.ShapeDtypeStruct((n_rows, w), dt) for w, dt in outs]
    out_shape += [jax.ShapeDtypeStruct((1, w), F32) for w in accs]

    def body(*refs):
        i = pl.program_id(0)
        res = fn(*[r[...] for r in refs[:n_in]])
        for o_ref, r in zip(refs[n_in:n_in + n_out], res[:n_out]):
            o_ref[...] = r.astype(o_ref.dtype)
        for a_ref, r in zip(refs[n_in + n_out:], res[n_out:]):
            @pl.when(i == 0)
            def _(a_ref=a_ref, r=r):
                a_ref[...] = r

            @pl.when(i > 0)
            def _(a_ref=a_ref, r=r):
                a_ref[...] += r

    return pl.pallas_call(
        body, name=name, grid=(n_rows // tm,), in_specs=in_specs, out_specs=out_specs, out_shape=out_shape,
        compiler_params=_cparams(("arbitrary",)))(*args)


EPS = 1e-6


def _sig(x):
    return 1.0 / (1.0 + jnp.exp(-x))


def _colsum(x):
    return jnp.sum(x, axis=0, keepdims=True)


def _rms_fwd_fn(x, g):
    r = lax.rsqrt(jnp.mean(x * x, axis=-1, keepdims=True) + EPS)
    return (x * r * g,)


def _rms_bwd_fn(x, g, dh, dres):
    dh = dh.astype(F32)
    r = lax.rsqrt(jnp.mean(x * x, axis=-1, keepdims=True) + EPS)
    xh = x * r
    dyg = dh * g
    dx = r * (dyg - xh * jnp.mean(dyg * xh, axis=-1, keepdims=True)) + dres
    return dx, _colsum(dh * xh)


def _loss_fn(x, g, t):
    d = x.shape[-1]
    r = lax.rsqrt(jnp.mean(x * x, axis=-1, keepdims=True) + EPS)
    xh = x * r
    err = xh * g - t
    dy = err * (1.0 / d)
    dyg = dy * g
    dx = r * (dyg - xh * jnp.mean(dyg * xh, axis=-1, keepdims=True))
    return dx, _colsum(err * err), _colsum(dy * xh)


def _swiglu_fwd_fn(z):
    f = z.shape[-1] // 2
    z1, z2 = z[:, :f].astype(F32), z[:, f:].astype(F32)
    return (z1 * _sig(z1) * z2,)


def _swiglu_bwd_fn(z, da):
    f = z.shape[-1] // 2
    z1, z2, da = z[:, :f].astype(F32), z[:, f:].astype(F32), da.astype(F32)
    s = _sig(z1)
    dz1 = da * z2 * (s * (1.0 + z1 * (1.0 - s)))
    dz2 = da * (z1 * s)
    return (jnp.concatenate([dz1, dz2], axis=1),)


def _merge_fwd_fn(g0, g1, g2, bg, zs, ya, yc):
    d = ya.shape[-1]
    bg = bg.astype(F32)
    zs = zs.astype(F32)
    ys = zs[:, :d] * _sig(zs[:, d:])
    m = _sig(g0.astype(F32) + bg[:, :d]) * ys
    m = m + _sig(g1.astype(F32) + bg[:, d:2 * d]) * ya.astype(F32)
    m = m + _sig(g2.astype(F32) + bg[:, 2 * d:]) * yc.astype(F32)
    return (m,)


def _merge_bwd_fn(g0, g1, g2, bg, zs, ya, yc, dm):
    d = ya.shape[-1]
    bg = bg.astype(F32)
    zs = zs.astype(F32)
    dm = dm.astype(F32)
    z1, s2 = zs[:, :d], _sig(zs[:, d:])
    ys = z1 * s2
    s0 = _sig(g0.astype(F32) + bg[:, :d])
    s1 = _sig(g1.astype(F32) + bg[:, d:2 * d])
    s3 = _sig(g2.astype(F32) + bg[:, 2 * d:])
    dgl = jnp.concatenate([dm * ys * s0 * (1.0 - s0), dm * ya.astype(F32) * s1 * (1.0 - s1),
                           dm * yc.astype(F32) * s3 * (1.0 - s3)], axis=1)
    dys = dm * s0
    dzs = jnp.concatenate([dys * s2, dys * z1 * s2 * (1.0 - s2)], axis=1)
    return dgl, dzs, dm * s1, dm * s3, _colsum(dgl)


def _combine_fwd_fn(o0, o1, o2, l0, l1, l2):
    m = jnp.maximum(jnp.maximum(l0, l1), l2)
    e0, e1, e2 = jnp.exp(l0 - m), jnp.exp(l1 - m), jnp.exp(l2 - m)
    den = e0 + e1 + e2
    return (e0 * o0.astype(F32) + e1 * o1.astype(F32) + e2 * o2.astype(F32)) / den, m + jnp.log(den)


ADAM_LR, ADAM_B1, ADAM_B2, ADAM_EPS, ADAM_WD, ADAM_STEP = 0.001, 0.9, 0.999, 1e-08, 0.01, 10


def _adamw_fn(w, g, m, v):
    m = ADAM_B1 * m + (1.0 - ADAM_B1) * g
    v = ADAM_B2 * v + (1.0 - ADAM_B2) * (g * g)
    m_hat = m / (1.0 - ADAM_B1 ** ADAM_STEP)
    v_hat = v / (1.0 - ADAM_B2 ** ADAM_STEP)
    delta = -ADAM_LR * (m_hat / (jnp.sqrt(v_hat) + ADAM_EPS) + ADAM_WD * w)
    return delta, m, v


CONV_WIDTH = 31


def _conv_fwd(proj, cb, w32, conv_b, ln_g, ln_b, bl, c, name, tm=512):
    n = proj.shape[0]
    hp = (CONV_WIDTH - 1) * bl
    nt = n // tm

    def body(ap_ref, gp_ref, a_ref, g_ref, w_ref, cb_ref, lg_ref, lb_ref, hc_ref, hconv_ref, ext):
        i = pl.program_id(0)
        ext[pl.ds(hp, tm), :] = a_ref[...].astype(F32) * _sig(g_ref[...].astype(F32))
        hgp = ap_ref[pl.ds(tm - hp, hp), :].astype(F32) * _sig(gp_ref[pl.ds(tm - hp, hp), :].astype(F32))
        ext[pl.ds(0, hp), :] = jnp.where(i > 0, hgp, 0.0)
        acc = jnp.zeros((tm, c), F32) + cb_ref[...]
        for j in range(CONV_WIDTH):
            acc = acc + w_ref[j:j + 1, :] * ext[pl.ds(j * bl, tm), :]
        hconv_ref[...] = acc.astype(hconv_ref.dtype)
        h = hconv_ref[...].astype(F32)
        mu = jnp.mean(h, axis=-1, keepdims=True)
        xc = h - mu
        var = jnp.mean(xc * xc, axis=-1, keepdims=True)
        hn = xc * lax.rsqrt(var + EPS) * lg_ref[...] + lb_ref[...]
        hc_ref[...] = (hn * _sig(hn)).astype(hc_ref.dtype)

    prev = lambda i, k: (jnp.maximum(i - 1, 0), k)
    par = lambda arr: pl.BlockSpec(arr.shape, lambda i: (0, 0))
    return pl.pallas_call(
        body, name=name, grid=(nt,),
        in_specs=[pl.BlockSpec((tm, c), functools.partial(prev, k=cb)), pl.BlockSpec((tm, c), functools.partial(prev, k=cb + 1)),
                  pl.BlockSpec((tm, c), lambda i: (i, cb)), pl.BlockSpec((tm, c), lambda i: (i, cb + 1)),
                  par(w32), par(conv_b), par(ln_g), par(ln_b)],
        out_specs=[pl.BlockSpec((tm, c), lambda i: (i, 0))] * 2,
        out_shape=[jax.ShapeDtypeStruct((n, c), BF16)] * 2,
        scratch_shapes=[pltpu.VMEM((hp + tm, c), F32)],
        compiler_params=_cparams(("arbitrary",)))(proj, proj, proj, proj, w32, conv_b, ln_g, ln_b)


def _conv_bwd(proj, cb, dhc, hconv, w32, ln_g, ln_b, bl, c, name, tm=512, comm=None):
    n = proj.shape[0]
    hp = (CONV_WIDTH - 1) * bl
    nt = n // tm

    def ln_bwd(d, h, lg, lb):
        d, h = d.astype(F32), h.astype(F32)
        mu = jnp.mean(h, axis=-1, keepdims=True)
        xc = h - mu
        rstd = lax.rsqrt(jnp.mean(xc * xc, axis=-1, keepdims=True) + EPS)
        xh = xc * rstd
        hn = xh * lg + lb
        s = _sig(hn)
        dhn = d * (s * (1.0 + hn * (1.0 - s)))
        dxh = dhn * lg
        dh = rstd * (dxh - jnp.mean(dxh, axis=-1, keepdims=True) - xh * jnp.mean(dxh * xh, axis=-1, keepdims=True))
        return dh, dhn, xh

    def body(ap_ref, gp_ref, a_ref, g_ref, d_ref, dn_ref, h_ref, hn_ref, w_ref, lg_ref, lb_ref,
             dcv_ref, dw_ref, dcb_ref, dlg_ref, dlb_ref, ext_h, ext_d):
        i = pl.program_id(0)
        lg, lb = lg_ref[...], lb_ref[...]
        a, g = a_ref[...].astype(F32), g_ref[...].astype(F32)
        sg = _sig(g)
        ext_h[pl.ds(hp, tm), :] = a * sg
        hgp = ap_ref[pl.ds(tm - hp, hp), :].astype(F32) * _sig(gp_ref[pl.ds(tm - hp, hp), :].astype(F32))
        ext_h[pl.ds(0, hp), :] = jnp.where(i > 0, hgp, 0.0)
        dh, dhn, xh = ln_bwd(d_ref[...], h_ref[...], lg, lb)
        ext_d[pl.ds(0, tm), :] = dh
        dh_n, _, _ = ln_bwd(dn_ref[pl.ds(0, hp), :], hn_ref[pl.ds(0, hp), :], lg, lb)
        ext_d[pl.ds(tm, hp), :] = jnp.where(i < nt - 1, dh_n, 0.0)

        @pl.when(i == 0)
        def _():
            dw_ref[...] = jnp.zeros_like(dw_ref)
            dcb_ref[...] = jnp.zeros_like(dcb_ref)
            dlg_ref[...] = jnp.zeros_like(dlg_ref)
            dlb_ref[...] = jnp.zeros_like(dlb_ref)

        dcb_ref[...] += _colsum(dh)
        dlg_ref[...] += _colsum(dhn * xh)
        dlb_ref[...] += _colsum(dhn)
        dhg = jnp.zeros((tm, c), F32)
        for j in range(CONV_WIDTH):
            dhg = dhg + w_ref[j:j + 1, :] * ext_d[pl.ds((CONV_WIDTH - 1 - j) * bl, tm), :]
            dw_ref[j:j + 1, :] += _colsum(dh * ext_h[pl.ds(j * bl, tm), :])
        dcv_ref[...] = jnp.concatenate([dhg * sg, dhg * a * sg * (1.0 - sg)], axis=1).astype(dcv_ref.dtype)

    prev = lambda i, k: (jnp.maximum(i - 1, 0), k)
    nxt = lambda i: (jnp.minimum(i + 1, nt - 1), 0)
    cur = lambda i: (i, 0)
    par = lambda arr: pl.BlockSpec(arr.shape, lambda i: (0, 0))
    acc = lambda r: pl.BlockSpec((r, c), lambda i: (0, 0))
    in_specs = [pl.BlockSpec((tm, c), functools.partial(prev, k=cb)), pl.BlockSpec((tm, c), functools.partial(prev, k=cb + 1)),
                pl.BlockSpec((tm, c), lambda i: (i, cb)), pl.BlockSpec((tm, c), lambda i: (i, cb + 1)),
                pl.BlockSpec((tm, c), cur), pl.BlockSpec((tm, c), nxt), pl.BlockSpec((tm, c), cur), pl.BlockSpec((tm, c), nxt),
                par(w32), par(ln_g), par(ln_b)]
    out_shape = [jax.ShapeDtypeStruct((n, 2 * c), BF16), jax.ShapeDtypeStruct((32, c), F32)] + [jax.ShapeDtypeStruct((1, c), F32)] * 3
    return _pcall(body, name, (nt,), in_specs, [pl.BlockSpec((tm, 2 * c), cur), acc(32), acc(1), acc(1), acc(1)], out_shape,
                  [pltpu.VMEM((hp + tm, c), F32), pltpu.VMEM((hp + tm, c), F32)], ("arbitrary",),
                  [proj, proj, proj, proj, dhc, dhc, hconv, hconv, w32, ln_g, ln_b], comm)


SSM_CH = 128
_GELU_C = 0.7978845608028654


def _gelu(x):
    return 0.5 * x * (1.0 + jnp.tanh(_GELU_C * (x + 0.044715 * x * x * x)))


def _gelu_grad(x):
    th = jnp.tanh(_GELU_C * (x + 0.044715 * x * x * x))
    return 0.5 * (1.0 + th) + 0.5 * x * (1.0 - th * th) * (_GELU_C * (1.0 + 3.0 * 0.044715 * x * x))


def _ssm_disc(lam_re, lam_im, log_dt, b_re, b_im):
    dt = jnp.exp(log_dt)[:, None]
    mag = jnp.exp(lam_re * dt)
    ab_re = mag * jnp.cos(lam_im * dt)
    ab_im = mag * jnp.sin(lam_im * dt)
    nr, ni = ab_re - 1.0, ab_im
    den = lam_re * lam_re + lam_im * lam_im
    z_re = ((nr * lam_re + ni * lam_im) / den)[..., None]
    z_im = ((ni * lam_re - nr * lam_im) / den)[..., None]
    return ab_re, ab_im, z_re * b_re - z_im * b_im, z_re * b_im + z_im * b_re


def _ssm_pack(ab_re, ab_im, bb_re, bb_im, c_re, c_im):
    g, p, h = bb_re.shape
    gc = SSM_CH // h
    nc = g // gc
    eye = jnp.eye(gc, dtype=F32)
    blk = lambda x: jnp.einsum("qgph,gk->qghkp", x.reshape(nc, gc, p, h), eye).reshape(nc, gc * h, gc * p)
    bbd = jnp.concatenate([blk(bb_re), blk(bb_im)], axis=2).astype(BF16)
    blc = lambda x: jnp.einsum("qghp,gk->qgpkh", x.reshape(nc, gc, h, p), eye).reshape(nc, gc * p, gc * h)
    cdm = jnp.concatenate([blc(c_re), blc(-c_im)], axis=1).astype(BF16)
    a = jnp.concatenate([ab_re.reshape(nc, gc * p), ab_im.reshape(nc, gc * p)], axis=1)
    a8 = jnp.broadcast_to(a[:, None, :], (nc, 8, 2 * gc * p)).reshape(nc * 8, 2 * gc * p)
    return bbd, cdm, a8


def _ssm_unpack(dbb, dcd, da, g, p, h):
    gc = SSM_CH // h
    nc = g // gc
    ph = gc * p
    eye = jnp.eye(gc, dtype=F32)
    dia = lambda x, o: jnp.einsum("qgpkh,gk->" + o, x.reshape(nc, gc, p, gc, h), eye).reshape((g, p, h) if o == "qgph" else (g, h, p))
    das = da.reshape(nc, 8, 2 * ph).sum(axis=1)
    return (das[:, :ph].reshape(g, p), das[:, ph:].reshape(g, p), dia(dbb[:, :ph], "qgph"), dia(dbb[:, ph:], "qgph"),
            dia(dcd[:, :ph], "qghp"), -dia(dcd[:, ph:], "qghp"))


def _ssm_fwd(proj, bbd, cdm, a8, dskip, bl, name, tm=1024, comm=None):
    n = proj.shape[0]
    nc, ch, p2 = bbd.shape
    ph = p2 // 2
    nt = n // tm
    nsub = 8 // bl

    def body(u_ref, bb_ref, cd_ref, a_ref, d_ref, ypre_ref, yg_ref, s_ref, bu, carry):
        t = pl.program_id(1)

        @pl.when(t == 0)
        def _():
            carry[...] = jnp.zeros_like(carry)

        u = u_ref[...]
        bu[...] = jnp.dot(u, bb_ref[0], preferred_element_type=F32)
        a_re, a_im = a_ref[:, :ph], a_ref[:, ph:]
        row = lax.broadcasted_iota(jnp.int32, (8, ph), 0)

        def step(k, c):
            cre, cim = c
            r0 = pl.multiple_of(k * 8, 8)
            bre, bim = bu[pl.ds(r0, 8), :ph], bu[pl.ds(r0, 8), ph:]
            sre, sim = cre, cim
            for sub in range(nsub):
                xre, xim = pltpu.roll(cre, bl, 0), pltpu.roll(cim, bl, 0)
                cre = a_re * xre - a_im * xim + bre
                cim = a_re * xim + a_im * xre + bim
                if sub == 0:
                    sre, sim = cre, cim
                else:
                    sel = row >= sub * bl
                    sre, sim = jnp.where(sel, cre, sre), jnp.where(sel, cim, sim)
            bu[pl.ds(r0, 8), :ph] = sre
            bu[pl.ds(r0, 8), ph:] = sim
            return sre, sim

        cre, cim = lax.fori_loop(0, tm // 8, step, (carry[:, :ph], carry[:, ph:]))
        carry[:, :ph] = cre
        carry[:, ph:] = cim
        s16 = bu[...].astype(BF16)
        s_ref[...] = s16
        y = jnp.dot(s16, cd_ref[0], preferred_element_type=F32) + d_ref[...] * u.astype(F32)
        ypre_ref[...] = y
        yg_ref[...] = _gelu(y).astype(yg_ref.dtype)

    in_specs = [pl.BlockSpec((tm, ch), lambda q, t: (t, q)), pl.BlockSpec((1, ch, p2), lambda q, t: (q, 0, 0)),
                pl.BlockSpec((1, p2, ch), lambda q, t: (q, 0, 0)), pl.BlockSpec((8, p2), lambda q, t: (q, 0)),
                pl.BlockSpec((1, ch), lambda q, t: (0, q))]
    out_specs = [pl.BlockSpec((tm, ch), lambda q, t: (t, q)), pl.BlockSpec((tm, ch), lambda q, t: (t, q)),
                 pl.BlockSpec((tm, p2), lambda q, t: (t, q))]
    out_shape = [jax.ShapeDtypeStruct((n, nc * ch), F32), jax.ShapeDtypeStruct((n, nc * ch), BF16),
                 jax.ShapeDtypeStruct((n, nc * p2), BF16)]
    return _pcall(body, name, (nc, nt), in_specs, out_specs, out_shape, [pltpu.VMEM((tm, p2), F32), pltpu.VMEM((8, p2), F32)],
                  ("parallel", "arbitrary"), [proj, bbd, cdm, a8, dskip], comm)


def _ssm_bwd(dyg, ypre, proj, s_all, cdt, bbt, a8, dskip, bl, name, tm=1024, comm=None):
    n = proj.shape[0]
    nc, ch, p2 = cdt.shape
    ph = p2 // 2
    nt = n // tm
    nsub = 8 // bl
    tn_dims = (((0,), (0,)), ((), ()))

    def body(dyg_ref, ypre_ref, u_ref, s_ref, cdt_ref, bbt_ref, a_ref, d_ref,
             du_ref, dbb_ref, dcd_ref, da_ref, dd_ref, ds, s32, carry):
        t = pl.program_id(1)

        @pl.when(t == 0)
        def _():
            carry[...] = jnp.zeros_like(carry)
            dbb_ref[...] = jnp.zeros_like(dbb_ref)
            dcd_ref[...] = jnp.zeros_like(dcd_ref)
            da_ref[...] = jnp.zeros_like(da_ref)
            dd_ref[...] = jnp.zeros_like(dd_ref)

        dyp = dyg_ref[...].astype(F32) * _gelu_grad(ypre_ref[...])
        u = u_ref[...]
        dd_ref[...] += _colsum(dyp * u.astype(F32))
        dyp16 = dyp.astype(BF16)
        ds[...] = jnp.dot(dyp16, cdt_ref[0], preferred_element_type=F32)
        s16 = s_ref[...]
        s32[...] = s16.astype(F32)
        a_re, a_im = a_ref[:, :ph], a_ref[:, ph:]
        row = lax.broadcasted_iota(jnp.int32, (8, ph), 0)
        back = 8 - bl

        def step(kk, c):
            lre, lim, acr, aci = c
            r0 = pl.multiple_of((tm // 8 - 1 - kk) * 8, 8)
            dre, dim = ds[pl.ds(r0, 8), :ph], ds[pl.ds(r0, 8), ph:]
            sre, sim = s32[pl.ds(r0, 8), :ph], s32[pl.ds(r0, 8), ph:]
            ore, oim, ire, iim = lre, lim, lre, lim
            for sub in range(nsub - 1, -1, -1):
                xre, xim = pltpu.roll(lre, back, 0), pltpu.roll(lim, back, 0)
                lre = a_re * xre + a_im * xim + dre
                lim = a_re * xim - a_im * xre + dim
                if sub == nsub - 1:
                    ore, oim, ire, iim = lre, lim, xre, xim
                else:
                    sel = row < (sub + 1) * bl
                    ore, oim = jnp.where(sel, lre, ore), jnp.where(sel, lim, oim)
                    ire, iim = jnp.where(sel, xre, ire), jnp.where(sel, xim, iim)
            ds[pl.ds(r0, 8), :ph] = ore
            ds[pl.ds(r0, 8), ph:] = oim
            acr = acr + sre * ire + sim * iim
            aci = aci + sre * iim - sim * ire
            return ore, oim, acr, aci

        z = jnp.zeros((8, ph), F32)
        lre, lim, acr, aci = lax.fori_loop(0, tm // 8, step, (carry[:, :ph], carry[:, ph:], z, z))
        carry[:, :ph] = lre
        carry[:, ph:] = lim
        da_ref[:, :ph] += acr
        da_ref[:, ph:] += aci
        lam16 = ds[...].astype(BF16)
        du = jnp.dot(lam16, bbt_ref[0], preferred_element_type=F32) + d_ref[...] * dyp
        du_ref[...] = du.astype(du_ref.dtype)
        dbb_ref[0] += lax.dot_general(lam16, u, tn_dims, preferred_element_type=F32)
        dcd_ref[0] += lax.dot_general(s16, dyp16, tn_dims, preferred_element_type=F32)

    rev = lambda q, t: (nt - 1 - t, q)
    in_specs = [pl.BlockSpec((tm, ch), rev), pl.BlockSpec((tm, ch), rev), pl.BlockSpec((tm, ch), rev),
                pl.BlockSpec((tm, p2), rev), pl.BlockSpec((1, ch, p2), lambda q, t: (q, 0, 0)),
                pl.BlockSpec((1, p2, ch), lambda q, t: (q, 0, 0)), pl.BlockSpec((8, p2), lambda q, t: (q, 0)),
                pl.BlockSpec((1, ch), lambda q, t: (0, q))]
    out_specs = [pl.BlockSpec((tm, ch), rev), pl.BlockSpec((1, p2, ch), lambda q, t: (q, 0, 0)),
                 pl.BlockSpec((1, p2, ch), lambda q, t: (q, 0, 0)), pl.BlockSpec((8, p2), lambda q, t: (q, 0)),
                 pl.BlockSpec((1, ch), lambda q, t: (0, q))]
    out_shape = [jax.ShapeDtypeStruct((n, nc * ch), BF16), jax.ShapeDtypeStruct((nc, p2, ch), F32),
                 jax.ShapeDtypeStruct((nc, p2, ch), F32), jax.ShapeDtypeStruct((nc * 8, p2), F32),
                 jax.ShapeDtypeStruct((1, nc * ch), F32)]
    return _pcall(body, name, (nc, nt), in_specs, out_specs, out_shape,
                  [pltpu.VMEM((tm, p2), F32), pltpu.VMEM((tm, p2), F32), pltpu.VMEM((8, p2), F32)],
                  ("parallel", "arbitrary"), [dyg, ypre, proj, s_all, cdt, bbt, a8, dskip], comm)


_MESH = pl.DeviceIdType.MESH
_HBM = pl.BlockSpec(memory_space=pltpu.HBM)


def _position():
    return lax.axis_index("x"), lax.axis_index("y"), lax.axis_index("c")


def _other_chips(x, y):
    return [((1 - x, y), 2 * (1 - x) + y), ((x, 1 - y), 2 * x + 1 - y), ((1 - x, 1 - y), 2 * (1 - x) + 1 - y)]


def _swap_sibling(v, name):
    def body(v_ref, got_ref, send_sem, recv_sem):
        x, y, c = _position()
        cp = pltpu.make_async_remote_copy(src_ref=v_ref, dst_ref=got_ref, send_sem=send_sem, recv_sem=recv_sem,
                                          device_id=(x, y, 1 - c), device_id_type=_MESH)
        cp.start()
        cp.wait()

    return pl.pallas_call(
        body, name=name, in_specs=[_HBM], out_specs=_HBM, out_shape=jax.ShapeDtypeStruct(v.shape, v.dtype),
        scratch_shapes=[pltpu.SemaphoreType.DMA, pltpu.SemaphoreType.DMA])(v)


def _own_slot(gathered, own):
    return lax.dynamic_update_index_in_dim(gathered, own, _chip_index(), 0)


def _chip_allgather(v, name):
    def body(v_ref, out_ref, send_sems, recv_sems):
        x, y, c = _position()
        me = 2 * x + y
        sends = []
        for k, (chip, idx) in enumerate(_other_chips(x, y)):
            cp = pltpu.make_async_remote_copy(src_ref=v_ref, dst_ref=out_ref.at[me], send_sem=send_sems.at[k],
                                              recv_sem=recv_sems.at[k], device_id=(*chip, c), device_id_type=_MESH)
            cp.start()
            sends.append(cp)
        for k, (chip, idx) in enumerate(_other_chips(x, y)):
            pltpu.make_async_remote_copy(src_ref=v_ref, dst_ref=out_ref.at[idx], send_sem=send_sems.at[k],
                                         recv_sem=recv_sems.at[k], device_id=(*chip, c), device_id_type=_MESH).wait_recv()
        for cp in sends:
            cp.wait_send()

    out = pl.pallas_call(
        body, name=name, in_specs=[_HBM], out_specs=_HBM, out_shape=jax.ShapeDtypeStruct((4,) + tuple(v.shape), v.dtype),
        scratch_shapes=[pltpu.SemaphoreType.DMA((3,)), pltpu.SemaphoreType.DMA((3,))])(v)
    return _own_slot(out, v)


def _remote(src, dst, send_sems, recv_sems, s, device):
    return pltpu.make_async_remote_copy(src_ref=src, dst_ref=dst, send_sem=send_sems.at[s], recv_sem=recv_sems.at[s],
                                        device_id=device, device_id_type=_MESH)


class _Exchange:
    def __init__(self, ins, out_shapes, n_sems, aliases=None):
        self.ins, self.out_shapes, self.n_sems, self.aliases = list(ins), list(out_shapes), n_sems, aliases or {}

    def sem_shapes(self):
        return [pltpu.SemaphoreType.DMA((self.n_sems,)), pltpu.SemaphoreType.DMA((self.n_sems,))]


def _halves(ref, c, axis=0):
    h = ref.shape[axis] // 2
    idx = (slice(None),) * axis
    return ref.at[idx + (pl.ds(c * h, h),)], ref.at[idx + (pl.ds((1 - c) * h, h),)]


class _GatherShards(_Exchange):
    def __init__(self, ws):
        super().__init__(ws, [jax.ShapeDtypeStruct((N_CHIPS,) + tuple(w.shape), w.dtype) for w in ws], 6 * len(ws))

    def start(self, w_refs, out_refs, sems):
        send_sems, recv_sems = sems
        x, y, c = _position()
        me = 2 * x + y
        for i, (w, out) in enumerate(zip(w_refs, out_refs)):
            for k, (chip, idx) in enumerate(_other_chips(x, y)):
                _remote(_halves(w, c)[0], _halves(out.at[me], c)[0], send_sems, recv_sems, 6 * i + k, (*chip, c)).start()

    def finish(self, w_refs, out_refs, sems):
        send_sems, recv_sems = sems
        x, y, c = _position()
        sibling = (x, y, 1 - c)
        others = _other_chips(x, y)
        for i, out in enumerate(out_refs):
            for k, (chip, idx) in enumerate(others):
                landed = _halves(out.at[idx], c)[0]
                _remote(landed, landed, send_sems, recv_sems, 6 * i + k, (*chip, c)).wait_recv()
                _remote(landed, landed, send_sems, recv_sems, 6 * i + 3 + k, sibling).start()
        for i, (w, out) in enumerate(zip(w_refs, out_refs)):
            for k, (chip, idx) in enumerate(others):
                mine, theirs = _halves(out.at[idx], c)
                _remote(theirs, theirs, send_sems, recv_sems, 6 * i + 3 + k, sibling).wait_recv()
                _remote(mine, mine, send_sems, recv_sems, 6 * i + 3 + k, sibling).wait_send()
                _remote(_halves(w, c)[0], mine, send_sems, recv_sems, 6 * i + k, (*chip, c)).wait_send()


class _SwapHalves(_Exchange):
    def __init__(self, gs):
        shapes = [jax.ShapeDtypeStruct((g.shape[0], g.shape[1] // 2) + tuple(g.shape[2:]), g.dtype) for g in gs]
        super().__init__(gs, shapes, N_CHIPS * len(gs))

    def _copies(self, g_refs, out_refs, sems):
        x, y, c = _position()
        return [_remote(_halves(g.at[j], c)[1], out.at[j], sems[0], sems[1], N_CHIPS * i + j, (x, y, 1 - c))
                for i, (g, out) in enumerate(zip(g_refs, out_refs)) for j in range(N_CHIPS)]

    def start(self, g_refs, out_refs, sems):
        for cp in self._copies(g_refs, out_refs, sems):
            cp.start()

    def finish(self, g_refs, out_refs, sems):
        for cp in self._copies(g_refs, out_refs, sems):
            cp.wait()


class _ScatterPieces(_Exchange):
    def __init__(self, ps):
        super().__init__(ps, [jax.ShapeDtypeStruct((3,) + tuple(p.shape[1:]), p.dtype) for p in ps], 3 * len(ps))

    def _copies(self, p_refs, out_refs, sems):
        x, y, c = _position()
        return [_remote(p.at[idx], out.at[k], sems[0], sems[1], 3 * i + k, (*chip, c))
                for i, (p, out) in enumerate(zip(p_refs, out_refs)) for k, (chip, idx) in enumerate(_other_chips(x, y))]

    def start(self, p_refs, out_refs, sems):
        for cp in self._copies(p_refs, out_refs, sems):
            cp.start()

    def finish(self, p_refs, out_refs, sems):
        for cp in self._copies(p_refs, out_refs, sems):
            cp.wait()


class _JoinHalves(_Exchange):
    def __init__(self, rs):
        super().__init__(rs, [jax.ShapeDtypeStruct(r.shape, r.dtype) for r in rs], len(rs), {i: i for i in range(len(rs))})

    def start(self, r_refs, out_refs, sems):
        x, y, c = _position()
        for i, out in enumerate(out_refs):
            _remote(out.at[c], out.at[c], sems[0], sems[1], i, (x, y, 1 - c)).start()

    def finish(self, r_refs, out_refs, sems):
        x, y, c = _position()
        for i, out in enumerate(out_refs):
            _remote(out.at[c], out.at[c], sems[0], sems[1], i, (x, y, 1 - c)).wait_send()
            _remote(out.at[1 - c], out.at[1 - c], sems[0], sems[1], i, (x, y, 1 - c)).wait_recv()


def _run_exchange(ex, name):
    def body(*refs):
        ins, outs, sems = refs[:len(ex.ins)], refs[len(ex.ins):len(ex.ins) + len(ex.out_shapes)], refs[-2:]
        ex.start(ins, outs, sems)
        ex.finish(ins, outs, sems)

    return pl.pallas_call(body, name=name, in_specs=[_HBM] * len(ex.ins), out_specs=[_HBM] * len(ex.out_shapes),
                          out_shape=ex.out_shapes, scratch_shapes=ex.sem_shapes(), input_output_aliases=ex.aliases)(*ex.ins)


def _pcall(body, name, grid, in_specs, out_specs, out_shape, scratch_shapes, semantics, args, comm=None):
    if comm is None:
        return pl.pallas_call(body, name=name, grid=grid, in_specs=in_specs, out_specs=out_specs, out_shape=out_shape,
                              scratch_shapes=scratch_shapes, compiler_params=_cparams(semantics))(*args)
    n_in, n_out, n_scr, ci, co = len(in_specs), len(out_specs), len(scratch_shapes), len(comm.ins), len(comm.out_shapes)

    def wrapped(*refs):
        parts, a = [], 0
        for k in (n_in, ci, n_out, co, n_scr, 2):
            parts.append(refs[a:a + k])
            a += k
        ins, cins, outs, couts, scr, sems = parts
        ids = [pl.program_id(i) for i in range(len(grid))]
        first = functools.reduce(jnp.logical_and, [i == 0 for i in ids])
        last = functools.reduce(jnp.logical_and, [i == g - 1 for i, g in zip(ids, grid)])

        @pl.when(first)
        def _():
            comm.start(cins, couts, sems)

        body(*ins, *outs, *scr)

        @pl.when(last)
        def _():
            comm.finish(cins, couts, sems)

    res = pl.pallas_call(
        wrapped, name=name, grid=grid, in_specs=list(in_specs) + [_HBM] * ci, out_specs=list(out_specs) + [_HBM] * co,
        out_shape=list(out_shape) + comm.out_shapes, scratch_shapes=list(scratch_shapes) + comm.sem_shapes(),
        compiler_params=_cparams(("arbitrary",) * len(grid)))(*args, *comm.ins)
    return res[:n_out], res[n_out:]


ATT_WINDOW = 128
PHASES = 16
_NT = (((1,), (1,)), ((), ()))
_TN = (((0,), (0,)), ((), ()))


PERM_LANES = 512


def _phase_perm(bl):
    t = 16 * PHASES * bl
    col = jnp.arange(t)
    i, r, b = col // (PHASES * bl), (col // bl) % PHASES, col % bl
    return (jnp.arange(t)[:, None] == ((b * PHASES + r) * 16 + i)[None, :]).astype(BF16)


def _to_phase_order(x, bl, col0=0, width=None):
    n = x.shape[0]
    width = width or x.shape[1]
    t = 16 * PHASES * bl
    g = n // bl // PHASES
    tn = min(PERM_LANES, width)

    def body(p_ref, x_ref, o_ref):
        o_ref[...] = jnp.dot(p_ref[...], x_ref[...], preferred_element_type=F32).astype(o_ref.dtype).reshape(o_ref.shape)

    out = pl.pallas_call(
        body, name="to_phase", grid=(n // t, width // tn),
        in_specs=[pl.BlockSpec((t, t), lambda i, j: (0, 0)), pl.BlockSpec((t, tn), lambda i, j: (i, col0 // tn + j))],
        out_specs=pl.BlockSpec((bl * PHASES, 16, tn), lambda i, j: (0, i, j)),
        out_shape=jax.ShapeDtypeStruct((bl * PHASES, g, width), x.dtype),
        compiler_params=_cparams(("parallel", "parallel")))(_phase_perm(bl), x)
    return out.reshape(n, width)


def _from_phase_order(y, bl):
    n, width = y.shape
    t = 16 * PHASES * bl
    g = n // bl // PHASES
    tn = min(PERM_LANES, width)

    def body(p_ref, y_ref, o_ref):
        o_ref[...] = jnp.dot(p_ref[...], y_ref[...].reshape(t, tn), preferred_element_type=F32).astype(o_ref.dtype)

    return pl.pallas_call(
        body, name="from_phase", grid=(n // t, width // tn),
        in_specs=[pl.BlockSpec((t, t), lambda i, j: (0, 0)), pl.BlockSpec((bl * PHASES, 16, tn), lambda i, j: (0, i, j))],
        out_specs=pl.BlockSpec((t, tn), lambda i, j: (i, j)), out_shape=jax.ShapeDtypeStruct((n, width), y.dtype),
        compiler_params=_cparams(("parallel", "parallel")))(_phase_perm(bl).T, y.reshape(bl * PHASES, g, width))


def _att_geometry(p, n, bl):
    g = n // bl // PHASES
    if p == 0:
        return ((bl, PHASES, g), (bl, g // 16), (None, PHASES, 16),
                lambda sh: (lambda b, a: (b, 0, jnp.maximum(a + sh, 0))), 256, 16, lambda ids: ids[1] == 0)
    if p == 1:
        return ((bl, 4, 4, g), (bl, 2, g // 32), (None, 4, 2, 32),
                lambda sh: (lambda b, r, a: (b, 0, r, jnp.maximum(a + sh, 0))), 128, 32, lambda ids: ids[2] == 0)
    return ((bl * PHASES, g), (bl * PHASES // 2,), (2, g), lambda sh: (lambda s: (s, 0)), g, g, None)


def _att_units(p, n, bl):
    g = n // bl // PHASES
    full = slice(None)
    if p == 0:
        return [(full, full)], (PHASES, 16)
    if p == 1:
        return [(full, u, full) for u in range(2)], (4, 32)
    return [(u, full) for u in range(2)], (g,)


def _att_masks(p, qb, chunk):
    def pos(idx):
        return (idx % chunk) * (qb // chunk) + idx // chunk

    dq = pos(lax.broadcasted_iota(jnp.int32, (qb, qb), 0))
    dk = pos(lax.broadcasted_iota(jnp.int32, (qb, qb), 1))
    dist = dq - dk
    return jnp.logical_and(dist >= 0, dist <= ATT_WINDOW), dist + qb <= ATT_WINDOW


def _att_call(p, n, bl, c, body, name, ins, outs):
    prefix, grid, blk, idx_fn, qb, chunk, _ = _att_geometry(p, n, bl)

    def spec(cb, sh):
        f = idx_fn(sh)
        return pl.BlockSpec(blk + (c,), lambda *ids, f=f, cb=cb: f(*ids) + (cb,))

    in_specs = [spec(cb, sh) for _, cb, sh in ins]
    out_specs = [spec(0, 0) for _ in outs]
    out_shape = [jax.ShapeDtypeStruct(prefix + (c,), dt) for dt in outs]
    res = pl.pallas_call(body, name=name, grid=grid, in_specs=in_specs, out_specs=out_specs, out_shape=out_shape,
                         compiler_params=_cparams(("parallel",) * len(grid)))(*[a.reshape(prefix + (a.shape[1],)) for a, _, _ in ins])
    return [r.reshape(n, c) for r in res]


def _att_fwd(p, qkv, qcb, bl, c, heads):
    n = qkv.shape[0]
    _, grid, _, _, qb, chunk, first_fn = _att_geometry(p, n, bl)
    units, unit_shape = _att_units(p, n, bl)
    n_grid = len(grid)
    has_prev = first_fn is not None
    e = c // heads
    scale = e ** -0.5

    def body(*refs):
        if has_prev:
            q_ref, kc_ref, kp_ref, vc_ref, vp_ref, o_ref, l_ref = refs
        else:
            q_ref, kc_ref, vc_ref, o_ref, l_ref = refs
        ids = [pl.program_id(a) for a in range(n_grid)]
        mc, mp = _att_masks(p, qb, chunk)
        if has_prev:
            mp = jnp.logical_and(mp, jnp.logical_not(first_fn(ids)))
        lo = lax.broadcasted_iota(jnp.int32, (qb, 128), 1) < e
        ones = jnp.ones((qb, 128), BF16)
        tiles = [(unit, pl.ds(lt * 128, 128)) for unit in units for lt in range(c // 128)]
        n_t = len(tiles)
        load = lambda r, t: r[tiles[t][0] + (tiles[t][1],)].reshape(qb, 128)
        items = [(t, h) for t in range(n_t) for h in range(2)]
        dot = functools.partial(jnp.dot, preferred_element_type=F32)
        q2 = [load(q_ref, t) for t in range(n_t)]
        kc = [load(kc_ref, t) for t in range(n_t)]
        qm = [jnp.where(lo if h == 0 else jnp.logical_not(lo), q2[t], jnp.zeros_like(q2[t])) for t, h in items]
        sc = [jnp.where(mc, lax.dot_general(qm[i], kc[t], _NT, preferred_element_type=F32) * scale, -jnp.inf)
              for i, (t, h) in enumerate(items)]
        m = [jnp.max(s, axis=1, keepdims=True) for s in sc]
        if has_prev:
            kp = [load(kp_ref, t) for t in range(n_t)]
            sp = [jnp.where(mp, lax.dot_general(qm[i], kp[t], _NT, preferred_element_type=F32) * scale, -jnp.inf)
                  for i, (t, h) in enumerate(items)]
            m = [jnp.maximum(a, jnp.max(s, axis=1, keepdims=True)) for a, s in zip(m, sp)]
        pc = [jnp.exp(s - a).astype(BF16) for s, a in zip(sc, m)]
        vc = [load(vc_ref, t) for t in range(n_t)]
        acc = [dot(pc[i], vc[t]) for i, (t, h) in enumerate(items)]
        den = [dot(x, ones) for x in pc]
        if has_prev:
            pp = [jnp.exp(s - a).astype(BF16) for s, a in zip(sp, m)]
            vp = [load(vp_ref, t) for t in range(n_t)]
            acc = [a + dot(pp[i], vp[t]) for i, ((t, h), a) in enumerate(zip(items, acc))]
            den = [d + dot(x, ones) for d, x in zip(den, pp)]
        oh = [a / d for a, d in zip(acc, den)]
        lh = [a + jnp.log(d) for a, d in zip(m, den)]
        for t, (unit, ls) in enumerate(tiles):
            o_ref[unit + (ls,)] = jnp.where(lo, oh[2 * t], oh[2 * t + 1]).astype(o_ref.dtype).reshape(unit_shape + (128,))
            l_ref[unit + (ls,)] = jnp.where(lo, lh[2 * t], lh[2 * t + 1]).reshape(unit_shape + (128,))

    kcb, vcb = 3, 4
    ins = [(qkv, qcb, 0), (qkv, kcb, 0)] + ([(qkv, kcb, -1)] if has_prev else []) + [(qkv, vcb, 0)] + ([(qkv, vcb, -1)] if has_prev else [])
    return _att_call(p, n, bl, c, body, name=f"att_fwd{p}", ins=ins, outs=[BF16, F32])


def _att_bwd(p, qkv, qcb, o, do, lse, bl, c, heads):
    n = qkv.shape[0]
    _, grid, _, _, qb, chunk, first_fn = _att_geometry(p, n, bl)
    units, unit_shape = _att_units(p, n, bl)
    n_grid = len(grid)
    has_prev = first_fn is not None
    e = c // heads
    scale = e ** -0.5

    def body(*refs):
        if has_prev:
            q_ref, kc_ref, kp_ref, vc_ref, vp_ref, o_ref, do_ref, l_ref, dq_ref, dkc_ref, dkp_ref, dvc_ref, dvp_ref = refs
        else:
            q_ref, kc_ref, vc_ref, o_ref, do_ref, l_ref, dq_ref, dkc_ref, dvc_ref = refs
        ids = [pl.program_id(a) for a in range(n_grid)]
        mc, mp = _att_masks(p, qb, chunk)
        if has_prev:
            mp = jnp.logical_and(mp, jnp.logical_not(first_fn(ids)))
        lo = lax.broadcasted_iota(jnp.int32, (qb, 128), 1) < e
        tiles = [(unit, pl.ds(lt * 128, 128)) for unit in units for lt in range(c // 128)]
        n_t = len(tiles)
        load = lambda r, t: r[tiles[t][0] + (tiles[t][1],)].reshape(qb, 128)
        items = [(t, h) for t in range(n_t) for h in range(2)]
        nt_dot = lambda a, b: lax.dot_general(a, b, _NT, preferred_element_type=F32)
        tn_dot = lambda a, b: lax.dot_general(a, b, _TN, preferred_element_type=F32)
        dot = functools.partial(jnp.dot, preferred_element_type=F32)

        def store(r, t, v):
            r[tiles[t][0] + (tiles[t][1],)] = v.astype(r.dtype).reshape(unit_shape + (128,))

        sel = [lo if h == 0 else jnp.logical_not(lo) for t, h in items]
        q2, kc, vc, do2 = ([load(r, t) for t in range(n_t)] for r in (q_ref, kc_ref, vc_ref, do_ref))
        qm = [jnp.where(sel[i], q2[t], jnp.zeros_like(q2[t])) for i, (t, h) in enumerate(items)]
        dom = [jnp.where(sel[i], do2[t], jnp.zeros_like(do2[t])) for i, (t, h) in enumerate(items)]
        dod = [do2[t].astype(F32) * load(o_ref, t).astype(F32) for t in range(n_t)]
        lcol = [load(l_ref, t)[:, h * e:h * e + 1] for t, h in items]
        corr = [-jnp.sum(jnp.where(sel[i], dod[t], 0.0), axis=1, keepdims=True) for i, (t, h) in enumerate(items)]
        pc = [jnp.exp(jnp.where(mc, nt_dot(qm[i], kc[t]) * scale, -jnp.inf) - lcol[i]) for i, (t, h) in enumerate(items)]
        dsc = [(pc[i] * (nt_dot(dom[i], vc[t]) + corr[i]) * scale).astype(BF16) for i, (t, h) in enumerate(items)]
        pc = [x.astype(BF16) for x in pc]
        dq = [dot(dsc[i], kc[t]) for i, (t, h) in enumerate(items)]
        dkc = [tn_dot(dsc[2 * t], qm[2 * t]) + tn_dot(dsc[2 * t + 1], qm[2 * t + 1]) for t in range(n_t)]
        dvc = [tn_dot(pc[2 * t], dom[2 * t]) + tn_dot(pc[2 * t + 1], dom[2 * t + 1]) for t in range(n_t)]
        if has_prev:
            kp, vp = ([load(r, t) for t in range(n_t)] for r in (kp_ref, vp_ref))
            pp = [jnp.exp(jnp.where(mp, nt_dot(qm[i], kp[t]) * scale, -jnp.inf) - lcol[i]) for i, (t, h) in enumerate(items)]
            dsp = [(pp[i] * (nt_dot(dom[i], vp[t]) + corr[i]) * scale).astype(BF16) for i, (t, h) in enumerate(items)]
            pp = [x.astype(BF16) for x in pp]
            dq = [a + dot(dsp[i], kp[t]) for i, ((t, h), a) in enumerate(zip(items, dq))]
            dkp = [tn_dot(dsp[2 * t], qm[2 * t]) + tn_dot(dsp[2 * t + 1], qm[2 * t + 1]) for t in range(n_t)]
            dvp = [tn_dot(pp[2 * t], dom[2 * t]) + tn_dot(pp[2 * t + 1], dom[2 * t + 1]) for t in range(n_t)]
        for t in range(n_t):
            store(dq_ref, t, jnp.where(lo, dq[2 * t], dq[2 * t + 1]))
            store(dkc_ref, t, dkc[t])
            store(dvc_ref, t, dvc[t])
            if has_prev:
                store(dkp_ref, t, dkp[t])
                store(dvp_ref, t, dvp[t])

    kcb, vcb = 3, 4
    ins = [(qkv, qcb, 0), (qkv, kcb, 0)] + ([(qkv, kcb, -1)] if has_prev else []) + [(qkv, vcb, 0)] + ([(qkv, vcb, -1)] if has_prev else [])
    ins += [(o, 0, 0), (do, 0, 0), (lse, 0, 0)]
    res = _att_call(p, n, bl, c, body, name=f"att_bwd{p}", ins=ins, outs=[BF16] * (5 if has_prev else 3))
    if has_prev:
        dq, dkc, dkp, dvc, dvp = res
        return dq, dkc, dkp, dvc, dvp
    dq, dkc, dvc = res
    return dq, dkc, None, dvc, None


def _att_fold_prev(p, cur, prv, bl):
    if prv is None:
        return cur.astype(F32)
    n, c = cur.shape
    prefix, _, _, _, _, chunk, _ = _att_geometry(p, n, bl)
    v = prv.reshape(prefix + (c,)).astype(F32)
    shifted = jnp.concatenate([v[..., chunk:, :], jnp.zeros_like(v[..., :chunk, :])], axis=-2)
    return cur.astype(F32) + shifted.reshape(n, c)


def _attention_fwd(proj, c, bl, heads):
    n = proj.shape[0]
    qkv = _to_phase_order(proj, bl, col0=c, width=5 * c)
    outs = [_att_fwd(p, qkv, p, bl, c, heads) for p in range(3)]
    ins = [("row", o, c, 0) for o, _ in outs] + [("row", l, c, 0) for _, l in outs]
    o, lse = _rowwise(_combine_fwd_fn, "comb_fwd", n, ins, [(c, BF16), (c, F32)])
    return _from_phase_order(o, bl), (qkv, o, lse)


def _attention_bwd(do_tb, saved, bl, heads):
    qkv, o, lse = saved
    n, c = do_tb.shape
    do = _to_phase_order(do_tb, bl)
    dqs, dk, dv = [], 0.0, 0.0
    for p in range(3):
        dq, dkc, dkp, dvc, dvp = _att_bwd(p, qkv, p, o, do, lse, bl, c, heads)
        dqs.append(dq)
        dk = dk + _att_fold_prev(p, dkc, dkp, bl)
        dv = dv + _att_fold_prev(p, dvc, dvp, bl)
    dqkv = jnp.concatenate(dqs + [dk.astype(BF16), dv.astype(BF16)], axis=1)
    return _from_phase_order(dqkv, bl)


ATT_HEADS = 8
SSM_GROUPS, SSM_STATE, SSM_GROUP = 32, 64, 16


def _row(v):
    return v.reshape(1, -1)


def _carried(result, carry, key, hidden):
    if carry.get(key) is None:
        return result
    result, hidden[key] = result
    return result


def _layer_fwd(x, w, p, bl, carry, late=None):
    n, d = x.shape
    c = d // 2
    hidden = {}
    h, = _rowwise(_rms_fwd_fn, "rms_fwd", n, [("row", x, d, 0), ("par", _row(p["norm1_g"]))], [(d, BF16)])
    proj = _carried(_mm(h, w["w_in"], "nn", BF16, "mm_in", comm=carry.get("mm_in")), carry, "mm_in", hidden)
    if late is not None:
        w = dict(w, **late(hidden["mm_in"]))
    disc, disc_vjp = jax.vjp(_ssm_disc, p["ssm_lambda_re"], p["ssm_lambda_im"], p["ssm_log_dt"], p["ssm_b_re"], p["ssm_b_im"])
    bbd, cdm, a8 = _ssm_pack(*disc, p["ssm_c_re"], p["ssm_c_im"])
    ypre, yg, s_all = _carried(_ssm_fwd(proj, bbd, cdm, a8, _row(p["ssm_d"]), bl, "ssm_fwd", comm=carry.get("ssm_fwd")),
                               carry, "ssm_fwd", hidden)
    zs = _mm(yg, w["w_ssm_glu"], "nn", BF16, "mm_glu")
    o, att = _attention_fwd(proj, c, bl, ATT_HEADS)
    ya = _mm(o, w["w_att_up"], "nn", BF16, "mm_att")
    w32 = jnp.concatenate([p["conv_w"], jnp.zeros((1, c), F32)], axis=0)
    hc, hconv = _conv_fwd(proj, 6, w32, _row(p["conv_b"]), _row(p["conv_ln_g"]), _row(p["conv_ln_b"]), bl, c, "conv_fwd")
    yc = _mm(hc, w["w_conv_pw2"], "nn", BF16, "mm_pw2")
    gates = [("row", proj, d, 4), ("row", proj, d, 5), ("row", proj, d, 6), ("par", _row(p["b_gate"]))]
    branches = [("row", zs, 2 * d, 0), ("row", ya, d, 0), ("row", yc, d, 0)]
    merged, = _rowwise(_merge_fwd_fn, "merge_fwd", n, gates + branches, [(d, BF16)])
    xm = _mm(merged, w["w_out"], "nn", F32, "mm_out", res=x)
    h2, = _rowwise(_rms_fwd_fn, "rms_fwd", n, [("row", xm, d, 0), ("par", _row(p["norm2_g"]))], [(d, BF16)])
    z = _carried(_mm(h2, w["w_ffn_in"], "nn", BF16, "mm_ffn_in", comm=carry.get("mm_ffn_in")), carry, "mm_ffn_in", hidden)
    f = z.shape[1] // 2
    a, = _rowwise(_swiglu_fwd_fn, "swiglu_fwd", n, [("row", z, 2 * f, 0)], [(f, BF16)], tm=256)
    xo = _mm(a, w["w_ffn_out"], "nn", F32, "mm_ffn_out", res=xm)
    saved = dict(x=x, h=h, proj=proj, disc_vjp=disc_vjp, bbd=bbd, cdm=cdm, a8=a8, ypre=ypre, yg=yg, s_all=s_all, zs=zs, o=o,
                 att=att, ya=ya, w32=w32, hc=hc, hconv=hconv, yc=yc, gates=gates, branches=branches, merged=merged, xm=xm,
                 h2=h2, z=z, a=a)
    return xo, saved, hidden, w


def _layer_bwd(dxo, s, w, p, bl, carry):
    n, d = dxo.shape
    c = d // 2
    g, bufs, hidden = {}, {}, {}
    f = s["a"].shape[1]

    def dw(key, a, dy, name):
        bufs[key] = _mm_dw(a, dy, name, 1 if key in ROW_SHARDED else N_CHIPS)

    da = _mm(dxo, w["w_ffn_out"], "nt", BF16, "mm_ffn_out_dx")
    dw("w_ffn_out", s["a"], dxo, "mm_ffn_out_dw")
    dz, = _rowwise(_swiglu_bwd_fn, "swiglu_bwd", n, [("row", s["z"], 2 * f, 0), ("row", da, f, 0)], [(2 * f, BF16)], tm=256)
    dh2 = _mm(dz, w["w_ffn_in"], "nt", F32, "mm_ffn_in_dx")
    dw("w_ffn_in", s["h2"], dz, "mm_ffn_in_dw")
    dxm, dg2 = _rowwise(_rms_bwd_fn, "rms_bwd", n, [("row", s["xm"], d, 0), ("par", _row(p["norm2_g"])), ("row", dh2, d, 0),
                                                   ("row", dxo, d, 0)], [(d, F32)], [d])
    g["norm2_g"] = dg2[0]
    dmerged = _mm(dxm, w["w_out"], "nt", BF16, "mm_out_dx")
    dw("w_out", s["merged"], dxm, "mm_out_dw")
    dgl, dzs, dya, dyc, dbg = _rowwise(_merge_bwd_fn, "merge_bwd", n, s["gates"] + s["branches"] + [("row", dmerged, d, 0)],
                                       [(3 * d, BF16), (2 * d, BF16), (d, BF16), (d, BF16)], [3 * d], tm=256)
    g["b_gate"] = dbg[0]
    dyg = _mm(dzs, w["w_ssm_glu"], "nt", BF16, "mm_glu_dx")
    dw("w_ssm_glu", s["yg"], dzs, "mm_glu_dw")
    du, dbb, dcd, dab, dd = _carried(
        _ssm_bwd(dyg, s["ypre"], s["proj"], s["s_all"], s["cdm"].transpose(0, 2, 1), s["bbd"].transpose(0, 2, 1), s["a8"],
                 _row(p["ssm_d"]), bl, "ssm_bwd", comm=carry.get("ssm_bwd")), carry, "ssm_bwd", hidden)
    dab_re, dab_im, dbb_re, dbb_im, g["ssm_c_re"], g["ssm_c_im"] = _ssm_unpack(dbb, dcd, dab, SSM_GROUPS, SSM_STATE, SSM_GROUP)
    (g["ssm_lambda_re"], g["ssm_lambda_im"], g["ssm_log_dt"], g["ssm_b_re"],
     g["ssm_b_im"]) = s["disc_vjp"]((dab_re, dab_im, dbb_re, dbb_im))
    g["ssm_d"] = dd[0]
    do = _mm(dya, w["w_att_up"], "nt", BF16, "mm_att_dx")
    dw("w_att_up", s["o"], dya, "mm_att_dw")
    dqkv = _attention_bwd(do, s["att"], bl, ATT_HEADS)
    dhc = _mm(dyc, w["w_conv_pw2"], "nt", BF16, "mm_pw2_dx")
    dw("w_conv_pw2", s["hc"], dyc, "mm_pw2_dw")
    dcv, dcw, dcb, dlg, dlb = _carried(
        _conv_bwd(s["proj"], 6, dhc, s["hconv"], s["w32"], _row(p["conv_ln_g"]), _row(p["conv_ln_b"]), bl, c, "conv_bwd",
                  comm=carry.get("conv_bwd")), carry, "conv_bwd", hidden)
    g["conv_w"], g["conv_b"], g["conv_ln_g"], g["conv_ln_b"] = dcw, dcb[0], dlg[0], dlb[0]
    dproj = jnp.concatenate([du, dqkv, dcv, dgl], axis=1)
    dh = _mm(dproj, w["w_in"], "nt", F32, "mm_in_dx")
    dw("w_in", s["h"], dproj, "mm_in_dw")
    dx, dg1 = _rowwise(_rms_bwd_fn, "rms_bwd", n, [("row", s["x"], d, 0), ("par", _row(p["norm1_g"])), ("row", dh, d, 0),
                                                  ("row", dxm, d, 0)], [(d, F32)], [d])
    g["norm1_g"] = dg1[0]
    return dx, g, bufs, hidden


WEIGHTS = ['norm1_g', 'w_in', 'b_gate', 'ssm_lambda_re', 'ssm_lambda_im', 'ssm_log_dt', 'ssm_b_re', 'ssm_b_im', 'ssm_c_re',
           'ssm_c_im', 'ssm_d', 'w_ssm_glu', 'w_att_up', 'conv_w', 'conv_b', 'conv_ln_g', 'conv_ln_b', 'w_conv_pw2', 'w_out',
           'norm2_g', 'w_ffn_in', 'w_ffn_out', 'final_g']
BIG = ['w_in', 'w_ssm_glu', 'w_att_up', 'w_conv_pw2', 'w_out', 'w_ffn_in', 'w_ffn_out']
ROW_SHARDED = ('w_out', 'w_ffn_out')
SMALL = [k for k in WEIGHTS if k not in BIG]
LANES = 1024
N_CHIPS = 4
ROW_TILE_BYTES = 36 * 1024 * 1024
MIN_SHARD_TILE = 1024


def _pad_rows(a, rows):
    return jnp.concatenate([a, jnp.zeros((rows - a.shape[0],) + a.shape[1:], a.dtype)], axis=0) if rows > a.shape[0] else a


def _row_tile(rows, width, n_arrays):
    best = 16
    for t in range(16, rows + 1, 16):
        if rows % t == 0 and t * width * 4 * n_arrays * 2 <= ROW_TILE_BYTES:
            best = t
    return best


def _flat_fn(fn, name, ins, n_out, rows):
    return _rowwise(fn, name, rows, [("row", a, LANES, 0) for a in ins], [(LANES, F32)] * n_out, tm=rows)


def _reduce_prepare(bufs):
    landed = _run_exchange(_SwapHalves([b16 for _, b16 in bufs]), "rs_swap")
    p32s, p16s = [], []
    for (b32, _), la in zip(bufs, landed):
        s, m, cs = b32.shape
        h = m // 2
        tm = _row_tile(h, cs, 4)

        def body(g_ref, l_ref, o32, o16):
            r = g_ref[...] + l_ref[...].astype(F32)
            o32[...] = r
            o16[...] = r.astype(BF16)

        piece = pl.BlockSpec((None, tm, cs), lambda j, i: (j, i, 0))
        mine = pl.BlockSpec((None, None, tm, cs), lambda j, i: (j, _core_index(), i, 0))
        p32, p16 = pl.pallas_call(
            body, name="rs_add", grid=(s, h // tm), in_specs=[mine, piece], out_specs=[piece, piece],
            out_shape=[jax.ShapeDtypeStruct((s, h, cs), F32), jax.ShapeDtypeStruct((s, h, cs), BF16)],
            compiler_params=_cparams(("parallel", "parallel")))(b32.reshape(s, 2, h, cs), la)
        p32s.append(p32)
        p16s.append(p16)
    return p32s, p16s


def _reduce_finish(p32s, arrived):
    reduced = []
    for p32, lb in zip(p32s, arrived):
        _, h, cs = lb.shape
        tm = _row_tile(h, cs, 5)

        def body(p_ref, a_ref, b_ref, c_ref, o_ref):
            o_ref[...] = ((p_ref[...] + a_ref[...].astype(F32)) + b_ref[...].astype(F32)) + c_ref[...].astype(F32)

        mine = pl.BlockSpec((None, tm, cs), lambda i: (_chip_index(), i, 0))
        other = [pl.BlockSpec((None, tm, cs), lambda i, k=k: (k, i, 0)) for k in range(3)]
        half = pl.BlockSpec((None, tm, cs), lambda i: (_core_index(), i, 0))
        reduced.append(pl.pallas_call(
            body, name="rs_sum", grid=(h // tm,), in_specs=[mine] + other, out_specs=half,
            out_shape=jax.ShapeDtypeStruct((2, h, cs), F32), compiler_params=_cparams(("parallel",)))(p32, lb, lb, lb))
    joined = _run_exchange(_JoinHalves(reduced), "rs_gather")
    return [j.reshape(2 * j.shape[1], j.shape[2]) for j in joined]


def _adamw_layers(w, g_layers, m, v):
    depth, rows, cs = w.shape
    tm = _row_tile(rows, cs, 8)
    nb = rows // tm

    def body(*refs):
        w_ref, m_ref, v_ref = refs[:3]
        g_refs = refs[3:3 + depth]
        go_ref, d_ref, mo_ref, vo_ref = refs[3 + depth:]
        layer = pl.program_id(0)
        g = g_refs[0][...]
        for l in range(1, depth):
            g = jnp.where(layer == l, g_refs[l][...], g)
        delta, mo, vo = _adamw_fn(w_ref[...], g, m_ref[...], v_ref[...])
        go_ref[...], d_ref[...], mo_ref[...], vo_ref[...] = g, delta, mo, vo

    stacked = pl.BlockSpec((None, tm, cs), lambda l, i: (l, i, 0))
    g_specs = [pl.BlockSpec((tm, cs), lambda l, i, k=k: (jnp.where(l == k, i, jnp.where(l < k, 0, nb - 1)), 0)) for k in range(depth)]
    return pl.pallas_call(
        body, name="adamw", grid=(depth, nb), in_specs=[stacked] * 3 + g_specs, out_specs=[stacked] * 4,
        out_shape=[jax.ShapeDtypeStruct(w.shape, F32)] * 4, compiler_params=_cparams(("arbitrary", "arbitrary")))(w, m, v, *g_layers)


def _sum4_fn(a, b, c, d):
    return (((a.astype(F32) + b.astype(F32)) + c.astype(F32)) + d.astype(F32),)


def _add2_fn(a, b):
    return (a + b,)


def kernel(x, norm1_g, w_in, b_gate, ssm_lambda_re, ssm_lambda_im, ssm_log_dt, ssm_b_re, ssm_b_im, ssm_c_re, ssm_c_im, ssm_d, w_ssm_glu, w_att_up, conv_w, conv_b, conv_ln_g, conv_ln_b, w_conv_pw2, w_out, norm2_g, w_ffn_in, w_ffn_out, final_g, loss_target, m_norm1_g, m_w_in, m_b_gate, m_ssm_lambda_re, m_ssm_lambda_im, m_ssm_log_dt, m_ssm_b_re, m_ssm_b_im, m_ssm_c_re, m_ssm_c_im, m_ssm_d, m_w_ssm_glu, m_w_att_up, m_conv_w, m_conv_b, m_conv_ln_g, m_conv_ln_b, m_w_conv_pw2, m_w_out, m_norm2_g, m_w_ffn_in, m_w_ffn_out, m_final_g, v_norm1_g, v_w_in, v_b_gate, v_ssm_lambda_re, v_ssm_lambda_im, v_ssm_log_dt, v_ssm_b_re, v_ssm_b_im, v_ssm_c_re, v_ssm_c_im, v_ssm_d, v_w_ssm_glu, v_w_att_up, v_conv_w, v_conv_b, v_conv_ln_g, v_conv_ln_b, v_w_conv_pw2, v_w_out, v_norm2_g, v_w_ffn_in, v_w_ffn_out, v_final_g):
    args = dict(locals())
    wts = {k: args[k] for k in WEIGHTS}
    mom = {k: args["m_" + k] for k in WEIGHTS}
    var = {k: args["v_" + k] for k in WEIGHTS}
    bl, seq, d = x.shape
    n = bl * seq
    depth = norm1_g.shape[0]
    cx, cy, cc = _position()
    me = 2 * cx + cy

    assert depth == 2, "the exchanges of layer 1 are hidden behind layer 0's kernels"
    first = BIG[:1]
    rest = BIG[1:]

    shards = lambda keys, l: [wts[k][l].astype(BF16) for k in keys]

    def whole(keys, gathered):
        out = {}
        for k, a in zip(keys, gathered):
            _, ks, cs = a.shape
            if k in ROW_SHARDED:
                out[k] = a.reshape(N_CHIPS * ks, cs)
            elif cs < MIN_SHARD_TILE:
                out[k] = a.transpose(1, 0, 2).reshape(ks, N_CHIPS * cs)
            else:
                out[k] = a
        return out

    fill = lambda gathered, own: [_own_slot(g, o) for g, o in zip(gathered, own)]
    own0 = shards(first, 0) + [conv_w]
    gathered = fill(_run_exchange(_GatherShards(own0), "gather_weights"), own0)
    conv_full = gathered[-1].transpose(1, 2, 0, 3).reshape(depth, CONV_WIDTH, -1)
    params = lambda l: dict({k: wts[k][l] for k in SMALL if k not in ("final_g", "conv_w")}, conv_w=conv_full[l])

    to_rows = lambda t: t.transpose(1, 0, 2).reshape(n, d)
    own = {"mm_in": shards(rest, 0), "ssm_fwd": shards(first, 1), "mm_ffn_in": shards(rest, 1)}
    xs, s0, hidden, w0 = _layer_fwd(to_rows(x), whole(first, gathered[:-1]), params(0), bl, {k: _GatherShards(v) for k, v in own.items()},
                                    late=lambda got: whole(rest, fill(got, own["mm_in"])))
    w1 = dict(whole(first, fill(hidden["ssm_fwd"], own["ssm_fwd"])), **whole(rest, fill(hidden["mm_ffn_in"], own["mm_ffn_in"])))
    full = [w0, w1]
    xs, s1, _, _ = _layer_fwd(xs, full[1], params(1), bl, {})
    dx, sq, dgf = _rowwise(_loss_fn, "loss_head", n, [("row", xs, d, 0), ("par", _row(final_g)), ("row", to_rows(loss_target), d, 0)],
                           [(d, F32)], [d, d])
    loss = lax.psum(0.5 * jnp.sum(sq) / d, ("x", "y", "c"))

    pieces = lambda bufs, keys: [tuple(b.reshape(N_CHIPS, -1, b.shape[-1]) for b in bufs[k]) for k in keys]
    dx, g1, bufs1, _ = _layer_bwd(dx, s1, full[1], params(1), bl, {})
    p32_1, p16_1 = _reduce_prepare(pieces(bufs1, BIG))
    dx, g0, bufs0, hidden = _layer_bwd(dx, s0, full[0], params(0), bl,
                                       {"ssm_bwd": _ScatterPieces(p16_1[:1]), "conv_bwd": _ScatterPieces(p16_1[1:])})
    red1 = _reduce_finish(p32_1, list(hidden["ssm_bwd"]) + list(hidden["conv_bwd"]))
    p32_0, p16_0 = _reduce_prepare(pieces(bufs0, BIG))
    red0 = _reduce_finish(p32_0, _run_exchange(_ScatterPieces(p16_0), "rs_scatter"))
    grads = {"final_g": dgf[0]}
    for k in SMALL:
        if k != "final_g":
            grads[k] = jnp.stack([g0[k], g1[k]])
    grad_x = dx.reshape(seq, bl, d).transpose(1, 0, 2)
    outs = {}
    for k, r0, r1 in zip(BIG, red0, red1):
        for tag, a in zip(("grad", "delta", "m", "v"), _adamw_layers(wts[k], [r0, r1], mom[k], var[k])):
            outs[tag, k] = a

    def flat1(t):
        v = jnp.concatenate([t[k].reshape(-1) for k in SMALL])
        rows = -(-v.size // (8 * LANES)) * 8
        return _pad_rows(v, rows * LANES).reshape(rows, LANES), rows

    def unflat1(flat, shapes):
        out, off, v = {}, 0, flat.reshape(-1)
        for k in SMALL:
            size = math.prod(shapes[k])
            out[k] = v[off:off + size].reshape(shapes[k])
            off += size
        return out

    grads["conv_w"] = grads["conv_w"][:, :CONV_WIDTH]
    gs, rows = flat1(grads)
    chip_sum, = _flat_fn(_add2_fn, "ar_add", [gs, _swap_sibling(gs, "ar_swap")], 1, rows)
    slots = _chip_allgather(chip_sum, "ar_gather")
    gs_red, = _flat_fn(_sum4_fn, "ar_sum", [slots[j] for j in range(N_CHIPS)], 1, rows)
    g_sm = unflat1(gs_red, {k: grads[k].shape for k in SMALL})
    cs = conv_w.shape[2]
    g_sm["conv_w"] = lax.dynamic_slice_in_dim(g_sm["conv_w"], me * cs, cs, axis=2)
    (w1, rows), (g1, _), (m1, _), (v1, _) = flat1(wts), flat1(g_sm), flat1(mom), flat1(var)
    sm_out = _flat_fn(_adamw_fn, "adamw_small", [w1, g1, m1, v1], 3, rows)
    shapes = {k: wts[k].shape for k in SMALL}
    for tag, a in zip(("delta", "m", "v"), sm_out):
        for k, t in unflat1(a, shapes).items():
            outs[tag, k] = t
    for k in SMALL:
        outs["grad", k] = g_sm[k]
    return (loss, grad_x, *[outs["grad", k] for k in WEIGHTS], *[outs["delta", k] for k in WEIGHTS],
            *[outs["m", k] for k in WEIGHTS], *[outs["v", k] for k in WEIGHTS])
```

```python
import functools
import math

import jax
import jax.numpy as jnp
from jax import lax
from jax.experimental import pallas as pl
from jax.experimental.pallas import tpu as pltpu

F32 = jnp.float32
BF16 = jnp.bfloat16
VMEM_LIMIT = 56 * 1024 * 1024


def _cparams(sem):
    return pltpu.CompilerParams(dimension_semantics=sem, vmem_limit_bytes=VMEM_LIMIT)


_DIMS = {"nn": (((1,), (0,)), ((), ())), "nt": (((1,), (1,)), ((), ())), "tn": (((0,), (0,)), ((), ()))}


MM_ROWS = 1024
MM_DW_VMEM_BYTES = 44 * 1024 * 1024


def _div_tile(n, cap):
    best = None
    for t in range(128, min(n, cap) + 1, 128):
        if n % t == 0:
            best = t
    return best or n


def _mm(a, b, form, out_dtype, name, res=None, comm=None):
    sharded = b.ndim == 3
    kdim, cs = b.shape[-2], b.shape[-1]
    s = b.shape[0] if sharded else 1
    m = a.shape[0]
    tm = MM_ROWS if m % MM_ROWS == 0 else _div_tile(m, MM_ROWS)
    if form == "nn":
        n, kd = s * cs, kdim
        tn, tk = _div_tile(cs, 1792), _div_tile(kdim, 2048)
        per = cs // tn
        b_blk = (tk, tn)
        b_idx = (lambda i, j, k: (j // per, k, j % per)) if sharded else (lambda i, j, k: (k, j))
    else:
        n, kd = kdim, s * cs
        tn, tk = _div_tile(kdim, 1408), _div_tile(cs, 1792)
        per = cs // tk
        b_blk = (tn, tk)
        b_idx = (lambda i, j, k: (k // per, j, k % per)) if sharded else (lambda i, j, k: (j, k))
    nk = kd // tk
    a_spec = pl.BlockSpec((tm, tk), lambda i, j, k: (i, k))
    b_spec = pl.BlockSpec(((None,) + b_blk) if sharded else b_blk, b_idx)
    o_spec = pl.BlockSpec((tm, tn), lambda i, j, k: (i, j))
    dims = _DIMS[form]

    def body(*refs):
        a_ref, b_ref = refs[:2]
        r_ref = refs[2] if res is not None else None
        o_ref = refs[3] if res is not None else refs[2]
        p = lax.dot_general(a_ref[...].astype(BF16), b_ref[...], dims, preferred_element_type=F32)

        def finish(r):
            if r_ref is not None:
                r = r + r_ref[...]
            o_ref[...] = r.astype(out_dtype)

        if nk == 1:
            finish(p)
            return
        acc = refs[-1]
        k = pl.program_id(2)

        @pl.when(k == 0)
        def _():
            acc[...] = p

        @pl.when(k > 0)
        def _():
            acc[...] += p

        @pl.when(k == nk - 1)
        def _():
            finish(acc[...])

    ins = [a, b] + ([] if res is None else [res])
    in_specs = [a_spec, b_spec] + ([] if res is None else [o_spec])
    out = _pcall(body, name, (m // tm, n // tn, nk), in_specs, [o_spec], [jax.ShapeDtypeStruct((m, n), out_dtype)],
                 [pltpu.VMEM((tm, tn), F32)] if nk > 1 else [], ("parallel", "parallel", "arbitrary"), ins, comm)
    return out[0] if comm is None else (out[0][0], out[1])


def _mm_dw(a, dy, name, shards):
    r, m = a.shape
    c = dy.shape[1]
    cs = c // shards
    tm, tn = _div_tile(m, 1408), _div_tile(cs, 1408)
    fixed = tm * tn * (4 + 2 * (4 + 2))
    per_row = 2 * (tm * a.dtype.itemsize + tn * dy.dtype.itemsize)
    tk = max(t for t in (256, 512, 1024, 2048) if r % t == 0 and (t == 256 or fixed + t * per_row <= MM_DW_VMEM_BYTES))
    per = cs // tn
    nk = r // tk

    def body(a_ref, b_ref, o32, o16, acc):
        k = pl.program_id(2)
        p = lax.dot_general(a_ref[...].astype(BF16), b_ref[...].astype(BF16), _DIMS["tn"], preferred_element_type=F32)

        @pl.when(k == 0)
        def _():
            acc[...] = p

        @pl.when(k > 0)
        def _():
            acc[...] += p

        @pl.when(k == nk - 1)
        def _():
            o32[...] = acc[...]
            o16[...] = acc[...].astype(BF16)

    o_spec = pl.BlockSpec((None, tm, tn), lambda i, j, k: (j // per, i, j % per))
    shape = (shards, m, cs)
    in_specs = [pl.BlockSpec((tk, tm), lambda i, j, k: (k, i)), pl.BlockSpec((tk, tn), lambda i, j, k: (k, j))]
    return _pcall(body, name, (m // tm, c // tn, nk), in_specs, [o_spec, o_spec],
                  [jax.ShapeDtypeStruct(shape, F32), jax.ShapeDtypeStruct(shape, BF16)], [pltpu.VMEM((tm, tn), F32)],
                  ("parallel", "parallel", "arbitrary"), [a, dy])


def _core_index():
    return lax.axis_index("c")


def _chip_index():
    return 2 * lax.axis_index("x") + lax.axis_index("y")


def _rowwise(fn, name, n_rows, ins, outs, accs=(), tm=512):
    n_in, n_out, n_acc = len(ins), len(outs), len(accs)
    in_specs, args = [], []
    for spec in ins:
        if spec[0] == "row":
            _, arr, w, cb = spec
            in_specs.append(pl.BlockSpec((tm, w), lambda i, cb=cb: (i, cb)))
        elif spec[0] == "rowoff":
            _, arr, w, cb, index_fn, span = spec
            in_specs.append(pl.BlockSpec((tm, w), lambda i, cb=cb, index_fn=index_fn, nb=span // tm: (index_fn() * nb + i, cb)))
        elif spec[0] == "rowblk":
            _, arr, w, cb, start = spec
            in_specs.append(pl.BlockSpec((tm, w), lambda i, cb=cb, nb=start // tm: (nb + i, cb)))
        else:
            arr = spec[1]
            in_specs.append(pl.BlockSpec(arr.shape, lambda i: (0, 0)))
        args.append(arr)
    out_specs = [pl.BlockSpec((tm, w), lambda i: (i, 0)) for w, _ in outs]
    out_specs += [pl.BlockSpec((1, w), lambda i: (0, 0)) for w in accs]
    out_shape = [jax.ShapeDtypeStruct((n_rows, w), dt) for w, dt in outs]
    out_shape += [jax.ShapeDtypeStruct((1, w), F32) for w in accs]

    def body(*refs):
        i = pl.program_id(0)
        res = fn(*[r[...] for r in refs[:n_in]])
        for o_ref, r in zip(refs[n_in:n_in + n_out], res[:n_out]):
            o_ref[...] = r.astype(o_ref.dtype)
        for a_ref, r in zip(refs[n_in + n_out:], res[n_out:]):
            @pl.when(i == 0)
            def _(a_ref=a_ref, r=r):
                a_ref[...] = r

            @pl.when(i > 0)
            def _(a_ref=a_ref, r=r):
                a_ref[...] += r

    return pl.pallas_call(
        body, name=name, grid=(n_rows // tm,), in_specs=in_specs, out_specs=out_specs, out_shape=out_shape,
        compiler_params=_cparams(("arbitrary",)))(*args)


EPS = 1e-6


def _sig(x):
    return 1.0 / (1.0 + jnp.exp(-x))


def _colsum(x):
    return jnp.sum(x, axis=0, keepdims=True)


def _rms_fwd_fn(x, g):
    r = lax.rsqrt(jnp.mean(x * x, axis=-1, keepdims=True) + EPS)
    return (x * r * g,)


def _rms_bwd_fn(x, g, dh, dres):
    dh = dh.astype(F32)
    r = lax.rsqrt(jnp.mean(x * x, axis=-1, keepdims=True) + EPS)
    xh = x * r
    dyg = dh * g
    dx = r * (dyg - xh * jnp.mean(dyg * xh, axis=-1, keepdims=True)) + dres
    return dx, _colsum(dh * xh)


def _loss_fn(x, g, t):
    d = x.shape[-1]
    r = lax.rsqrt(jnp.mean(x * x, axis=-1, keepdims=True) + EPS)
    xh = x * r
    err = xh * g - t
    dy = err * (1.0 / d)
    dyg = dy * g
    dx = r * (dyg - xh * jnp.mean(dyg * xh, axis=-1, keepdims=True))
    return dx, _colsum(err * err), _colsum(dy * xh)


def _swiglu_fwd_fn(z):
    f = z.shape[-1] // 2
    z1, z2 = z[:, :f].astype(F32), z[:, f:].astype(F32)
    return (z1 * _sig(z1) * z2,)


def _swiglu_bwd_fn(z, da):
    f = z.shape[-1] // 2
    z1, z2, da = z[:, :f].astype(F32), z[:, f:].astype(F32), da.astype(F32)
    s = _sig(z1)
    dz1 = da * z2 * (s * (1.0 + z1 * (1.0 - s)))
    dz2 = da * (z1 * s)
    return (jnp.concatenate([dz1, dz2], axis=1),)


def _merge_fwd_fn(g0, g1, g2, bg, zs, ya, yc):
    d = ya.shape[-1]
    bg = bg.astype(F32)
    zs = zs.astype(F32)
    ys = zs[:, :d] * _sig(zs[:, d:])
    m = _sig(g0.astype(F32) + bg[:, :d]) * ys
    m = m + _sig(g1.astype(F32) + bg[:, d:2 * d]) * ya.astype(F32)
    m = m + _sig(g2.astype(F32) + bg[:, 2 * d:]) * yc.astype(F32)
    return (m,)


def _merge_bwd_fn(g0, g1, g2, bg, zs, ya, yc, dm):
    d = ya.shape[-1]
    bg = bg.astype(F32)
    zs = zs.astype(F32)
    dm = dm.astype(F32)
    z1, s2 = zs[:, :d], _sig(zs[:, d:])
    ys = z1 * s2
    s0 = _sig(g0.astype(F32) + bg[:, :d])
    s1 = _sig(g1.astype(F32) + bg[:, d:2 * d])
    s3 = _sig(g2.astype(F32) + bg[:, 2 * d:])
    dgl = jnp.concatenate([dm * ys * s0 * (1.0 - s0), dm * ya.astype(F32) * s1 * (1.0 - s1),
                           dm * yc.astype(F32) * s3 * (1.0 - s3)], axis=1)
    dys = dm * s0
    dzs = jnp.concatenate([dys * s2, dys * z1 * s2 * (1.0 - s2)], axis=1)
    return dgl, dzs, dm * s1, dm * s3, _colsum(dgl)


def _combine_fwd_fn(o0, o1, o2, l0, l1, l2):
    m = jnp.maximum(jnp.maximum(l0, l1), l2)
    e0, e1, e2 = jnp.exp(l0 - m), jnp.exp(l1 - m), jnp.exp(l2 - m)
    den = e0 + e1 + e2
    return (e0 * o0.astype(F32) + e1 * o1.astype(F32) + e2 * o2.astype(F32)) / den, m + jnp.log(den)


ADAM_LR, ADAM_B1, ADAM_B2, ADAM_EPS, ADAM_WD, ADAM_STEP = 0.001, 0.9, 0.999, 1e-08, 0.01, 10


def _adamw_fn(w, g, m, v):
    m = ADAM_B1 * m + (1.0 - ADAM_B1) * g
    v = ADAM_B2 * v + (1.0 - ADAM_B2) * (g * g)
    m_hat = m / (1.0 - ADAM_B1 ** ADAM_STEP)
    v_hat = v / (1.0 - ADAM_B2 ** ADAM_STEP)
    delta = -ADAM_LR * (m_hat / (jnp.sqrt(v_hat) + ADAM_EPS) + ADAM_WD * w)
    return delta, m, v


CONV_WIDTH = 31


def _conv_fwd(proj, cb, w32, conv_b, ln_g, ln_b, bl, c, name, tm=512):
    n = proj.shape[0]
    hp = (CONV_WIDTH - 1) * bl
    nt = n // tm

    def body(ap_ref, gp_ref, a_ref, g_ref, w_ref, cb_ref, lg_ref, lb_ref, hc_ref, hconv_ref, ext):
        i = pl.program_id(0)
        ext[pl.ds(hp, tm), :] = a_ref[...].astype(F32) * _sig(g_ref[...].astype(F32))
        hgp = ap_ref[pl.ds(tm - hp, hp), :].astype(F32) * _sig(gp_ref[pl.ds(tm - hp, hp), :].astype(F32))
        ext[pl.ds(0, hp), :] = jnp.where(i > 0, hgp, 0.0)
        acc = jnp.zeros((tm, c), F32) + cb_ref[...]
        for j in range(CONV_WIDTH):
            acc = acc + w_ref[j:j + 1, :] * ext[pl.ds(j * bl, tm), :]
        hconv_ref[...] = acc.astype(hconv_ref.dtype)
        h = hconv_ref[...].astype(F32)
        mu = jnp.mean(h, axis=-1, keepdims=True)
        xc = h - mu
        var = jnp.mean(xc * xc, axis=-1, keepdims=True)
        hn = xc * lax.rsqrt(var + EPS) * lg_ref[...] + lb_ref[...]
        hc_ref[...] = (hn * _sig(hn)).astype(hc_ref.dtype)

    prev = lambda i, k: (jnp.maximum(i - 1, 0), k)
    par = lambda arr: pl.BlockSpec(arr.shape, lambda i: (0, 0))
    return pl.pallas_call(
        body, name=name, grid=(nt,),
        in_specs=[pl.BlockSpec((tm, c), functools.partial(prev, k=cb)), pl.BlockSpec((tm, c), functools.partial(prev, k=cb + 1)),
                  pl.BlockSpec((tm, c), lambda i: (i, cb)), pl.BlockSpec((tm, c), lambda i: (i, cb + 1)),
                  par(w32), par(conv_b), par(ln_g), par(ln_b)],
        out_specs=[pl.BlockSpec((tm, c), lambda i: (i, 0))] * 2,
        out_shape=[jax.ShapeDtypeStruct((n, c), BF16)] * 2,
        scratch_shapes=[pltpu.VMEM((hp + tm, c), F32)],
        compiler_params=_cparams(("arbitrary",)))(proj, proj, proj, proj, w32, conv_b, ln_g, ln_b)


def _conv_bwd(proj, cb, dhc, hconv, w32, ln_g, ln_b, bl, c, name, tm=512, comm=None):
    n = proj.shape[0]
    hp = (CONV_WIDTH - 1) * bl
    nt = n // tm

    def ln_bwd(d, h, lg, lb):
        d, h = d.astype(F32), h.astype(F32)
        mu = jnp.mean(h, axis=-1, keepdims=True)
        xc = h - mu
        rstd = lax.rsqrt(jnp.mean(xc * xc, axis=-1, keepdims=True) + EPS)
        xh = xc * rstd
        hn = xh * lg + lb
        s = _sig(hn)
        dhn = d * (s * (1.0 + hn * (1.0 - s)))
        dxh = dhn * lg
        dh = rstd * (dxh - jnp.mean(dxh, axis=-1, keepdims=True) - xh * jnp.mean(dxh * xh, axis=-1, keepdims=True))
        return dh, dhn, xh

    def body(ap_ref, gp_ref, a_ref, g_ref, d_ref, dn_ref, h_ref, hn_ref, w_ref, lg_ref, lb_ref,
             dcv_ref, dw_ref, dcb_ref, dlg_ref, dlb_ref, ext_h, ext_d):
        i = pl.program_id(0)
        lg, lb = lg_ref[...], lb_ref[...]
        a, g = a_ref[...].astype(F32), g_ref[...].astype(F32)
        sg = _sig(g)
        ext_h[pl.ds(hp, tm), :] = a * sg
        hgp = ap_ref[pl.ds(tm - hp, hp), :].astype(F32) * _sig(gp_ref[pl.ds(tm - hp, hp), :].astype(F32))
        ext_h[pl.ds(0, hp), :] = jnp.where(i > 0, hgp, 0.0)
        dh, dhn, xh = ln_bwd(d_ref[...], h_ref[...], lg, lb)
        ext_d[pl.ds(0, tm), :] = dh
        dh_n, _, _ = ln_bwd(dn_ref[pl.ds(0, hp), :], hn_ref[pl.ds(0, hp), :], lg, lb)
        ext_d[pl.ds(tm, hp), :] = jnp.where(i < nt - 1, dh_n, 0.0)

        @pl.when(i == 0)
        def _():
            dw_ref[...] = jnp.zeros_like(dw_ref)
            dcb_ref[...] = jnp.zeros_like(dcb_ref)
            dlg_ref[...] = jnp.zeros_like(dlg_ref)
            dlb_ref[...] = jnp.zeros_like(dlb_ref)

        dcb_ref[...] += _colsum(dh)
        dlg_ref[...] += _colsum(dhn * xh)
        dlb_ref[...] += _colsum(dhn)
        dhg = jnp.zeros((tm, c), F32)
        for j in range(CONV_WIDTH):
            dhg = dhg + w_ref[j:j + 1, :] * ext_d[pl.ds((CONV_WIDTH - 1 - j) * bl, tm), :]
            dw_ref[j:j + 1, :] += _colsum(dh * ext_h[pl.ds(j * bl, tm), :])
        dcv_ref[...] = jnp.concatenate([dhg * sg, dhg * a * sg * (1.0 - sg)], axis=1).astype(dcv_ref.dtype)

    prev = lambda i, k: (jnp.maximum(i - 1, 0), k)
    nxt = lambda i: (jnp.minimum(i + 1, nt - 1), 0)
    cur = lambda i: (i, 0)
    par = lambda arr: pl.BlockSpec(arr.shape, lambda i: (0, 0))
    acc = lambda r: pl.BlockSpec((r, c), lambda i: (0, 0))
    in_specs = [pl.BlockSpec((tm, c), functools.partial(prev, k=cb)), pl.BlockSpec((tm, c), functools.partial(prev, k=cb + 1)),
                pl.BlockSpec((tm, c), lambda i: (i, cb)), pl.BlockSpec((tm, c), lambda i: (i, cb + 1)),
                pl.BlockSpec((tm, c), cur), pl.BlockSpec((tm, c), nxt), pl.BlockSpec((tm, c), cur), pl.BlockSpec((tm, c), nxt),
                par(w32), par(ln_g), par(ln_b)]
    out_shape = [jax.ShapeDtypeStruct((n, 2 * c), BF16), jax.ShapeDtypeStruct((32, c), F32)] + [jax.ShapeDtypeStruct((1, c), F32)] * 3
    return _pcall(body, name, (nt,), in_specs, [pl.BlockSpec((tm, 2 * c), cur), acc(32), acc(1), acc(1), acc(1)], out_shape,
                  [pltpu.VMEM((hp + tm, c), F32), pltpu.VMEM((hp + tm, c), F32)], ("arbitrary",),
                  [proj, proj, proj, proj, dhc, dhc, hconv, hconv, w32, ln_g, ln_b], comm)


SSM_CH = 128
_GELU_C = 0.7978845608028654


def _gelu(x):
    return 0.5 * x * (1.0 + jnp.tanh(_GELU_C * (x + 0.044715 * x * x * x)))


def _gelu_grad(x):
    th = jnp.tanh(_GELU_C * (x + 0.044715 * x * x * x))
    return 0.5 * (1.0 + th) + 0.5 * x * (1.0 - th * th) * (_GELU_C * (1.0 + 3.0 * 0.044715 * x * x))


def _ssm_disc(lam_re, lam_im, log_dt, b_re, b_im):
    dt = jnp.exp(log_dt)[:, None]
    mag = jnp.exp(lam_re * dt)
    ab_re = mag * jnp.cos(lam_im * dt)
    ab_im = mag * jnp.sin(lam_im * dt)
    nr, ni = ab_re - 1.0, ab_im
    den = lam_re * lam_re + lam_im * lam_im
    z_re = ((nr * lam_re + ni * lam_im) / den)[..., None]
    z_im = ((ni * lam_re - nr * lam_im) / den)[..., None]
    return ab_re, ab_im, z_re * b_re - z_im * b_im, z_re * b_im + z_im * b_re


def _ssm_pack(ab_re, ab_im, bb_re, bb_im, c_re, c_im):
    g, p, h = bb_re.shape
    gc = SSM_CH // h
    nc = g // gc
    eye = jnp.eye(gc, dtype=F32)
    blk = lambda x: jnp.einsum("qgph,gk->qghkp", x.reshape(nc, gc, p, h), eye).reshape(nc, gc * h, gc * p)
    bbd = jnp.concatenate([blk(bb_re), blk(bb_im)], axis=2).astype(BF16)
    blc = lambda x: jnp.einsum("qghp,gk->qgpkh", x.reshape(nc, gc, h, p), eye).reshape(nc, gc * p, gc * h)
    cdm = jnp.concatenate([blc(c_re), blc(-c_im)], axis=1).astype(BF16)
    a = jnp.concatenate([ab_re.reshape(nc, gc * p), ab_im.reshape(nc, gc * p)], axis=1)
    a8 = jnp.broadcast_to(a[:, None, :], (nc, 8, 2 * gc * p)).reshape(nc * 8, 2 * gc * p)
    return bbd, cdm, a8


def _ssm_unpack(dbb, dcd, da, g, p, h):
    gc = SSM_CH // h
    nc = g // gc
    ph = gc * p
    eye = jnp.eye(gc, dtype=F32)
    dia = lambda x, o: jnp.einsum("qgpkh,gk->" + o, x.reshape(nc, gc, p, gc, h), eye).reshape((g, p, h) if o == "qgph" else (g, h, p))
    das = da.reshape(nc, 8, 2 * ph).sum(axis=1)
    return (das[:, :ph].reshape(g, p), das[:, ph:].reshape(g, p), dia(dbb[:, :ph], "qgph"), dia(dbb[:, ph:], "qgph"),
            dia(dcd[:, :ph], "qghp"), -dia(dcd[:, ph:], "qghp"))


def _ssm_fwd(proj, bbd, cdm, a8, dskip, bl, name, tm=1024, comm=None):
    n = proj.shape[0]
    nc, ch, p2 = bbd.shape
    ph = p2 // 2
    nt = n // tm
    nsub = 8 // bl

    def body(u_ref, bb_ref, cd_ref, a_ref, d_ref, ypre_ref, yg_ref, s_ref, bu, carry):
        t = pl.program_id(1)

        @pl.when(t == 0)
        def _():
            carry[...] = jnp.zeros_like(carry)

        u = u_ref[...]
        bu[...] = jnp.dot(u, bb_ref[0], preferred_element_type=F32)
        a_re, a_im = a_ref[:, :ph], a_ref[:, ph:]
        row = lax.broadcasted_iota(jnp.int32, (8, ph), 0)

        def step(k, c):
            cre, cim = c
            r0 = pl.multiple_of(k * 8, 8)
            bre, bim = bu[pl.ds(r0, 8), :ph], bu[pl.ds(r0, 8), ph:]
            sre, sim = cre, cim
            for sub in range(nsub):
                xre, xim = pltpu.roll(cre, bl, 0), pltpu.roll(cim, bl, 0)
                cre = a_re * xre - a_im * xim + bre
                cim = a_re * xim + a_im * xre + bim
                if sub == 0:
                    sre, sim = cre, cim
                else:
                    sel = row >= sub * bl
                    sre, sim = jnp.where(sel, cre, sre), jnp.where(sel, cim, sim)
            bu[pl.ds(r0, 8), :ph] = sre
            bu[pl.ds(r0, 8), ph:] = sim
            return sre, sim

        cre, cim = lax.fori_loop(0, tm // 8, step, (carry[:, :ph], carry[:, ph:]))
        carry[:, :ph] = cre
        carry[:, ph:] = cim
        s16 = bu[...].astype(BF16)
        s_ref[...] = s16
        y = jnp.dot(s16, cd_ref[0], preferred_element_type=F32) + d_ref[...] * u.astype(F32)
        ypre_ref[...] = y
        yg_ref[...] = _gelu(y).astype(yg_ref.dtype)

    in_specs = [pl.BlockSpec((tm, ch), lambda q, t: (t, q)), pl.BlockSpec((1, ch, p2), lambda q, t: (q, 0, 0)),
                pl.BlockSpec((1, p2, ch), lambda q, t: (q, 0, 0)), pl.BlockSpec((8, p2), lambda q, t: (q, 0)),
                pl.BlockSpec((1, ch), lambda q, t: (0, q))]
    out_specs = [pl.BlockSpec((tm, ch), lambda q, t: (t, q)), pl.BlockSpec((tm, ch), lambda q, t: (t, q)),
                 pl.BlockSpec((tm, p2), lambda q, t: (t, q))]
    out_shape = [jax.ShapeDtypeStruct((n, nc * ch), F32), jax.ShapeDtypeStruct((n, nc * ch), BF16),
                 jax.ShapeDtypeStruct((n, nc * p2), BF16)]
    return _pcall(body, name, (nc, nt), in_specs, out_specs, out_shape, [pltpu.VMEM((tm, p2), F32), pltpu.VMEM((8, p2), F32)],
                  ("parallel", "arbitrary"), [proj, bbd, cdm, a8, dskip], comm)


def _ssm_bwd(dyg, ypre, proj, s_all, cdt, bbt, a8, dskip, bl, name, tm=1024, comm=None):
    n = proj.shape[0]
    nc, ch, p2 = cdt.shape
    ph = p2 // 2
    nt = n // tm
    nsub = 8 // bl
    tn_dims = (((0,), (0,)), ((), ()))

    def body(dyg_ref, ypre_ref, u_ref, s_ref, cdt_ref, bbt_ref, a_ref, d_ref,
             du_ref, dbb_ref, dcd_ref, da_ref, dd_ref, ds, s32, carry):
        t = pl.program_id(1)

        @pl.when(t == 0)
        def _():
            carry[...] = jnp.zeros_like(carry)
            dbb_ref[...] = jnp.zeros_like(dbb_ref)
            dcd_ref[...] = jnp.zeros_like(dcd_ref)
            da_ref[...] = jnp.zeros_like(da_ref)
            dd_ref[...] = jnp.zeros_like(dd_ref)

        dyp = dyg_ref[...].astype(F32) * _gelu_grad(ypre_ref[...])
        u = u_ref[...]
        dd_ref[...] += _colsum(dyp * u.astype(F32))
        dyp16 = dyp.astype(BF16)
        ds[...] = jnp.dot(dyp16, cdt_ref[0], preferred_element_type=F32)
        s16 = s_ref[...]
        s32[...] = s16.astype(F32)
        a_re, a_im = a_ref[:, :ph], a_ref[:, ph:]
        row = lax.broadcasted_iota(jnp.int32, (8, ph), 0)
        back = 8 - bl

        def step(kk, c):
            lre, lim, acr, aci = c
            r0 = pl.multiple_of((tm // 8 - 1 - kk) * 8, 8)
            dre, dim = ds[pl.ds(r0, 8), :ph], ds[pl.ds(r0, 8), ph:]
            sre, sim = s32[pl.ds(r0, 8), :ph], s32[pl.ds(r0, 8), ph:]
            ore, oim, ire, iim = lre, lim, lre, lim
            for sub in range(nsub - 1, -1, -1):
                xre, xim = pltpu.roll(lre, back, 0), pltpu.roll(lim, back, 0)
                lre = a_re * xre + a_im * xim + dre
                lim = a_re * xim - a_im * xre + dim
                if sub == nsub - 1:
                    ore, oim, ire, iim = lre, lim, xre, xim
                else:
                    sel = row < (sub + 1) * bl
                    ore, oim = jnp.where(sel, lre, ore), jnp.where(sel, lim, oim)
                    ire, iim = jnp.where(sel, xre, ire), jnp.where(sel, xim, iim)
            ds[pl.ds(r0, 8), :ph] = ore
            ds[pl.ds(r0, 8), ph:] = oim
            acr = acr + sre * ire + sim * iim
            aci = aci + sre * iim - sim * ire
            return ore, oim, acr, aci

        z = jnp.zeros((8, ph), F32)
        lre, lim, acr, aci = lax.fori_loop(0, tm // 8, step, (carry[:, :ph], carry[:, ph:], z, z))
        carry[:, :ph] = lre
        carry[:, ph:] = lim
        da_ref[:, :ph] += acr
        da_ref[:, ph:] += aci
        lam16 = ds[...].astype(BF16)
        du = jnp.dot(lam16, bbt_ref[0], preferred_element_type=F32) + d_ref[...] * dyp
        du_ref[...] = du.astype(du_ref.dtype)
        dbb_ref[0] += lax.dot_general(lam16, u, tn_dims, preferred_element_type=F32)
        dcd_ref[0] += lax.dot_general(s16, dyp16, tn_dims, preferred_element_type=F32)

    rev = lambda q, t: (nt - 1 - t, q)
    in_specs = [pl.BlockSpec((tm, ch), rev), pl.BlockSpec((tm, ch), rev), pl.BlockSpec((tm, ch), rev),
                pl.BlockSpec((tm, p2), rev), pl.BlockSpec((1, ch, p2), lambda q, t: (q, 0, 0)),
                pl.BlockSpec((1, p2, ch), lambda q, t: (q, 0, 0)), pl.BlockSpec((8, p2), lambda q, t: (q, 0)),
                pl.BlockSpec((1, ch), lambda q, t: (0, q))]
    out_specs = [pl.BlockSpec((tm, ch), rev), pl.BlockSpec((1, p2, ch), lambda q, t: (q, 0, 0)),
                 pl.BlockSpec((1, p2, ch), lambda q, t: (q, 0, 0)), pl.BlockSpec((8, p2), lambda q, t: (q, 0)),
                 pl.BlockSpec((1, ch), lambda q, t: (0, q))]
    out_shape = [jax.ShapeDtypeStruct((n, nc * ch), BF16), jax.ShapeDtypeStruct((nc, p2, ch), F32),
                 jax.ShapeDtypeStruct((nc, p2, ch), F32), jax.ShapeDtypeStruct((nc * 8, p2), F32),
                 jax.ShapeDtypeStruct((1, nc * ch), F32)]
    return _pcall(body, name, (nc, nt), in_specs, out_specs, out_shape,
                  [pltpu.VMEM((tm, p2), F32), pltpu.VMEM((tm, p2), F32), pltpu.VMEM((8, p2), F32)],
                  ("parallel", "arbitrary"), [dyg, ypre, proj, s_all, cdt, bbt, a8, dskip], comm)


_MESH = pl.DeviceIdType.MESH
_HBM = pl.BlockSpec(memory_space=pltpu.HBM)


def _position():
    return lax.axis_index("x"), lax.axis_index("y"), lax.axis_index("c")


def _other_chips(x, y):
    return [((1 - x, y), 2 * (1 - x) + y), ((x, 1 - y), 2 * x + 1 - y), ((1 - x, 1 - y), 2 * (1 - x) + 1 - y)]


def _swap_sibling(v, name):
    def body(v_ref, got_ref, send_sem, recv_sem):
        x, y, c = _position()
        cp = pltpu.make_async_remote_copy(src_ref=v_ref, dst_ref=got_ref, send_sem=send_sem, recv_sem=recv_sem,
                                          device_id=(x, y, 1 - c), device_id_type=_MESH)
        cp.start()
        cp.wait()

    return pl.pallas_call(
        body, name=name, in_specs=[_HBM], out_specs=_HBM, out_shape=jax.ShapeDtypeStruct(v.shape, v.dtype),
        scratch_shapes=[pltpu.SemaphoreType.DMA, pltpu.SemaphoreType.DMA])(v)


def _own_slot(gathered, own):
    return lax.dynamic_update_index_in_dim(gathered, own, _chip_index(), 0)


def _chip_allgather(v, name):
    def body(v_ref, out_ref, send_sems, recv_sems):
        x, y, c = _position()
        me = 2 * x + y
        sends = []
        for k, (chip, idx) in enumerate(_other_chips(x, y)):
            cp = pltpu.make_async_remote_copy(src_ref=v_ref, dst_ref=out_ref.at[me], send_sem=send_sems.at[k],
                                              recv_sem=recv_sems.at[k], device_id=(*chip, c), device_id_type=_MESH)
            cp.start()
            sends.append(cp)
        for k, (chip, idx) in enumerate(_other_chips(x, y)):
            pltpu.make_async_remote_copy(src_ref=v_ref, dst_ref=out_ref.at[idx], send_sem=send_sems.at[k],
                                         recv_sem=recv_sems.at[k], device_id=(*chip, c), device_id_type=_MESH).wait_recv()
        for cp in sends:
            cp.wait_send()

    out = pl.pallas_call(
        body, name=name, in_specs=[_HBM], out_specs=_HBM, out_shape=jax.ShapeDtypeStruct((4,) + tuple(v.shape), v.dtype),
        scratch_shapes=[pltpu.SemaphoreType.DMA((3,)), pltpu.SemaphoreType.DMA((3,))])(v)
    return _own_slot(out, v)


def _remote(src, dst, send_sems, recv_sems, s, device):
    return pltpu.make_async_remote_copy(src_ref=src, dst_ref=dst, send_sem=send_sems.at[s], recv_sem=recv_sems.at[s],
                                        device_id=device, device_id_type=_MESH)


class _Exchange:
    def __init__(self, ins, out_shapes, n_sems, aliases=None):
        self.ins, self.out_shapes, self.n_sems, self.aliases = list(ins), list(out_shapes), n_sems, aliases or {}

    def sem_shapes(self):
        return [pltpu.SemaphoreType.DMA((self.n_sems,)), pltpu.SemaphoreType.DMA((self.n_sems,))]


def _halves(ref, c, axis=0):
    h = ref.shape[axis] // 2
    idx = (slice(None),) * axis
    return ref.at[idx + (pl.ds(c * h, h),)], ref.at[idx + (pl.ds((1 - c) * h, h),)]


class _GatherShards(_Exchange):
    def __init__(self, ws):
        super().__init__(ws, [jax.ShapeDtypeStruct((N_CHIPS,) + tuple(w.shape), w.dtype) for w in ws], 6 * len(ws))

    def start(self, w_refs, out_refs, sems):
        send_sems, recv_sems = sems
        x, y, c = _position()
        me = 2 * x + y
        for i, (w, out) in enumerate(zip(w_refs, out_refs)):
            for k, (chip, idx) in enumerate(_other_chips(x, y)):
                _remote(_halves(w, c)[0], _halves(out.at[me], c)[0], send_sems, recv_sems, 6 * i + k, (*chip, c)).start()

    def finish(self, w_refs, out_refs, sems):
        send_sems, recv_sems = sems
        x, y, c = _position()
        sibling = (x, y, 1 - c)
        others = _other_chips(x, y)
        for i, out in enumerate(out_refs):
            for k, (chip, idx) in enumerate(others):
                landed = _halves(out.at[idx], c)[0]
                _remote(landed, landed, send_sems, recv_sems, 6 * i + k, (*chip, c)).wait_recv()
                _remote(landed, landed, send_sems, recv_sems, 6 * i + 3 + k, sibling).start()
        for i, (w, out) in enumerate(zip(w_refs, out_refs)):
            for k, (chip, idx) in enumerate(others):
                mine, theirs = _halves(out.at[idx], c)
                _remote(theirs, theirs, send_sems, recv_sems, 6 * i + 3 + k, sibling).wait_recv()
                _remote(mine, mine, send_sems, recv_sems, 6 * i + 3 + k, sibling).wait_send()
                _remote(_halves(w, c)[0], mine, send_sems, recv_sems, 6 * i + k, (*chip, c)).wait_send()


class _SwapHalves(_Exchange):
    def __init__(self, gs):
        shapes = [jax.ShapeDtypeStruct((g.shape[0], g.shape[1] // 2) + tuple(g.shape[2:]), g.dtype) for g in gs]
        super().__init__(gs, shapes, N_CHIPS * len(gs))

    def _copies(self, g_refs, out_refs, sems):
        x, y, c = _position()
        return [_remote(_halves(g.at[j], c)[1], out.at[j], sems[0], sems[1], N_CHIPS * i + j, (x, y, 1 - c))
                for i, (g, out) in enumerate(zip(g_refs, out_refs)) for j in range(N_CHIPS)]

    def start(self, g_refs, out_refs, sems):
        for cp in self._copies(g_refs, out_refs, sems):
            cp.start()

    def finish(self, g_refs, out_refs, sems):
        for cp in self._copies(g_refs, out_refs, sems):
            cp.wait()


class _ScatterPieces(_Exchange):
    def __init__(self, ps):
        super().__init__(ps, [jax.ShapeDtypeStruct((3,) + tuple(p.shape[1:]), p.dtype) for p in ps], 3 * len(ps))

    def _copies(self, p_refs, out_refs, sems):
        x, y, c = _position()
        return [_remote(p.at[idx], out.at[k], sems[0], sems[1], 3 * i + k, (*chip, c))
                for i, (p, out) in enumerate(zip(p_refs, out_refs)) for k, (chip, idx) in enumerate(_other_chips(x, y))]

    def start(self, p_refs, out_refs, sems):
        for cp in self._copies(p_refs, out_refs, sems):
            cp.start()

    def finish(self, p_refs, out_refs, sems):
        for cp in self._copies(p_refs, out_refs, sems):
            cp.wait()


class _JoinHalves(_Exchange):
    def __init__(self, rs):
        super().__init__(rs, [jax.ShapeDtypeStruct(r.shape, r.dtype) for r in rs], len(rs), {i: i for i in range(len(rs))})

    def start(self, r_refs, out_refs, sems):
        x, y, c = _position()
        for i, out in enumerate(out_refs):
            _remote(out.at[c], out.at[c], sems[0], sems[1], i, (x, y, 1 - c)).start()

    def finish(self, r_refs, out_refs, sems):
        x, y, c = _position()
        for i, out in enumerate(out_refs):
            _remote(out.at[c], out.at[c], sems[0], sems[1], i, (x, y, 1 - c)).wait_send()
            _remote(out.at[1 - c], out.at[1 - c], sems[0], sems[1], i, (x, y, 1 - c)).wait_recv()


def _run_exchange(ex, name):
    def body(*refs):
        ins, outs, sems = refs[:len(ex.ins)], refs[len(ex.ins):len(ex.ins) + len(ex.out_shapes)], refs[-2:]
        ex.start(ins, outs, sems)
        ex.finish(ins, outs, sems)

    return pl.pallas_call(body, name=name, in_specs=[_HBM] * len(ex.ins), out_specs=[_HBM] * len(ex.out_shapes),
                          out_shape=ex.out_shapes, scratch_shapes=ex.sem_shapes(), input_output_aliases=ex.aliases)(*ex.ins)


def _pcall(body, name, grid, in_specs, out_specs, out_shape, scratch_shapes, semantics, args, comm=None):
    if comm is None:
        return pl.pallas_call(body, name=name, grid=grid, in_specs=in_specs, out_specs=out_specs, out_shape=out_shape,
                              scratch_shapes=scratch_shapes, compiler_params=_cparams(semantics))(*args)
    n_in, n_out, n_scr, ci, co = len(in_specs), len(out_specs), len(scratch_shapes), len(comm.ins), len(comm.out_shapes)

    def wrapped(*refs):
        parts, a = [], 0
        for k in (n_in, ci, n_out, co, n_scr, 2):
            parts.append(refs[a:a + k])
            a += k
        ins, cins, outs, couts, scr, sems = parts
        ids = [pl.program_id(i) for i in range(len(grid))]
        first = functools.reduce(jnp.logical_and, [i == 0 for i in ids])
        last = functools.reduce(jnp.logical_and, [i == g - 1 for i, g in zip(ids, grid)])

        @pl.when(first)
        def _():
            comm.start(cins, couts, sems)

        body(*ins, *outs, *scr)

        @pl.when(last)
        def _():
            comm.finish(cins, couts, sems)

    res = pl.pallas_call(
        wrapped, name=name, grid=grid, in_specs=list(in_specs) + [_HBM] * ci, out_specs=list(out_specs) + [_HBM] * co,
        out_shape=list(out_shape) + comm.out_shapes, scratch_shapes=list(scratch_shapes) + comm.sem_shapes(),
        compiler_params=_cparams(("arbitrary",) * len(grid)))(*args, *comm.ins)
    return res[:n_out], res[n_out:]


ATT_WINDOW = 128
PHASES = 16
_NT = (((1,), (1,)), ((), ()))
_TN = (((0,), (0,)), ((), ()))


PERM_LANES = 512


def _phase_perm(bl):
    t = 16 * PHASES * bl
    col = jnp.arange(t)
    i, r, b = col // (PHASES * bl), (col // bl) % PHASES, col % bl
    return (jnp.arange(t)[:, None] == ((b * PHASES + r) * 16 + i)[None, :]).astype(BF16)


def _to_phase_order(x, bl, col0=0, width=None):
    n = x.shape[0]
    width = width or x.shape[1]
    t = 16 * PHASES * bl
    g = n // bl // PHASES
    tn = min(PERM_LANES, width)

    def body(p_ref, x_ref, o_ref):
        o_ref[...] = jnp.dot(p_ref[...], x_ref[...], preferred_element_type=F32).astype(o_ref.dtype).reshape(o_ref.shape)

    out = pl.pallas_call(
        body, name="to_phase", grid=(n // t, width // tn),
        in_specs=[pl.BlockSpec((t, t), lambda i, j: (0, 0)), pl.BlockSpec((t, tn), lambda i, j: (i, col0 // tn + j))],
        out_specs=pl.BlockSpec((bl * PHASES, 16, tn), lambda i, j: (0, i, j)),
        out_shape=jax.ShapeDtypeStruct((bl * PHASES, g, width), x.dtype),
        compiler_params=_cparams(("parallel", "parallel")))(_phase_perm(bl), x)
    return out.reshape(n, width)


def _assemble_dproj(du, dqkv, dcv, dgl, bl):
    n, c = du.shape
    t = 16 * PHASES * bl
    g = n // bl // PHASES
    tn = PERM_LANES
    pieces = [(du, 0), (dqkv, c // tn), (dcv, 6 * c // tn), (dgl, 8 * c // tn)]
    n_tiles = [a.shape[1] // tn for a, _ in pieces]

    def body(p_ref, du_ref, dqkv_ref, dcv_ref, dgl_ref, o_ref):
        j = pl.program_id(1)
        for k, ref in enumerate((du_ref, dqkv_ref, dcv_ref, dgl_ref)):
            @pl.when(jnp.logical_and(j >= pieces[k][1], j < pieces[k][1] + n_tiles[k]))
            def _(k=k, ref=ref):
                if k == 1:
                    o_ref[...] = jnp.dot(p_ref[...], ref[...].reshape(t, tn), preferred_element_type=F32).astype(o_ref.dtype)
                else:
                    o_ref[...] = ref[...]

    def tile(k):
        return lambda i, j: jnp.clip(j - pieces[k][1], 0, n_tiles[k] - 1)

    rows = lambda k: pl.BlockSpec((t, tn), lambda i, j, f=tile(k): (i, f(i, j)))
    return pl.pallas_call(
        body, name="assemble_dproj", grid=(n // t, sum(n_tiles)),
        in_specs=[pl.BlockSpec((t, t), lambda i, j: (0, 0)), rows(0),
                  pl.BlockSpec((bl * PHASES, 16, tn), lambda i, j, f=tile(1): (0, i, f(i, j))), rows(2), rows(3)],
        out_specs=pl.BlockSpec((t, tn), lambda i, j: (i, j)), out_shape=jax.ShapeDtypeStruct((n, sum(n_tiles) * tn), BF16),
        compiler_params=_cparams(("parallel", "arbitrary")))(_phase_perm(bl).T, du, dqkv.reshape(bl * PHASES, g, dqkv.shape[1]), dcv, dgl)


def _from_phase_order(y, bl):
    n, width = y.shape
    t = 16 * PHASES * bl
    g = n // bl // PHASES
    tn = min(PERM_LANES, width)

    def body(p_ref, y_ref, o_ref):
        o_ref[...] = jnp.dot(p_ref[...], y_ref[...].reshape(t, tn), preferred_element_type=F32).astype(o_ref.dtype)

    return pl.pallas_call(
        body, name="from_phase", grid=(n // t, width // tn),
        in_specs=[pl.BlockSpec((t, t), lambda i, j: (0, 0)), pl.BlockSpec((bl * PHASES, 16, tn), lambda i, j: (0, i, j))],
        out_specs=pl.BlockSpec((t, tn), lambda i, j: (i, j)), out_shape=jax.ShapeDtypeStruct((n, width), y.dtype),
        compiler_params=_cparams(("parallel", "parallel")))(_phase_perm(bl).T, y.reshape(bl * PHASES, g, width))


def _att_geometry(p, n, bl):
    g = n // bl // PHASES
    if p == 0:
        return ((bl, PHASES, g), (bl, g // 16), (None, PHASES, 16),
                lambda sh: (lambda b, a: (b, 0, jnp.maximum(a + sh, 0))), 256, 16, lambda ids: ids[1] == 0)
    if p == 1:
        return ((bl, 4, 4, g), (bl, 2, g // 32), (None, 4, 2, 32),
                lambda sh: (lambda b, r, a: (b, 0, r, jnp.maximum(a + sh, 0))), 128, 32, lambda ids: ids[2] == 0)
    return ((bl * PHASES, g), (bl * PHASES // 2,), (2, g), lambda sh: (lambda s: (s, 0)), g, g, None)


def _att_units(p, n, bl):
    g = n // bl // PHASES
    full = slice(None)
    if p == 0:
        return [(full, full)], (PHASES, 16)
    if p == 1:
        return [(full, u, full) for u in range(2)], (4, 32)
    return [(u, full) for u in range(2)], (g,)


def _att_masks(p, qb, chunk):
    def pos(idx):
        return (idx % chunk) * (qb // chunk) + idx // chunk

    dq = pos(lax.broadcasted_iota(jnp.int32, (qb, qb), 0))
    dk = pos(lax.broadcasted_iota(jnp.int32, (qb, qb), 1))
    dist = dq - dk
    return jnp.logical_and(dist >= 0, dist <= ATT_WINDOW), dist + qb <= ATT_WINDOW


def _att_call(p, n, bl, c, body, name, ins, outs):
    prefix, grid, blk, idx_fn, qb, chunk, _ = _att_geometry(p, n, bl)

    def spec(cb, sh):
        f = idx_fn(sh)
        return pl.BlockSpec(blk + (c,), lambda *ids, f=f, cb=cb: f(*ids) + (cb,))

    in_specs = [spec(cb, sh) for _, cb, sh in ins]
    out_specs = [spec(0, 0) for _ in outs]
    out_shape = [jax.ShapeDtypeStruct(prefix + (c,), dt) for dt in outs]
    res = pl.pallas_call(body, name=name, grid=grid, in_specs=in_specs, out_specs=out_specs, out_shape=out_shape,
                         compiler_params=_cparams(("parallel",) * len(grid)))(*[a.reshape(prefix + (a.shape[1],)) for a, _, _ in ins])
    return [r.reshape(n, c) for r in res]


def _att_fwd(p, qkv, qcb, bl, c, heads):
    n = qkv.shape[0]
    _, grid, _, _, qb, chunk, first_fn = _att_geometry(p, n, bl)
    units, unit_shape = _att_units(p, n, bl)
    n_grid = len(grid)
    has_prev = first_fn is not None
    e = c // heads
    scale = e ** -0.5

    def body(*refs):
        if has_prev:
            q_ref, kc_ref, kp_ref, vc_ref, vp_ref, o_ref, l_ref = refs
        else:
            q_ref, kc_ref, vc_ref, o_ref, l_ref = refs
        ids = [pl.program_id(a) for a in range(n_grid)]
        mc, mp = _att_masks(p, qb, chunk)
        if has_prev:
            mp = jnp.logical_and(mp, jnp.logical_not(first_fn(ids)))
        lo = lax.broadcasted_iota(jnp.int32, (qb, 128), 1) < e
        ones = jnp.ones((qb, 128), BF16)
        tiles = [(unit, pl.ds(lt * 128, 128)) for unit in units for lt in range(c // 128)]
        n_t = len(tiles)
        load = lambda r, t: r[tiles[t][0] + (tiles[t][1],)].reshape(qb, 128)
        items = [(t, h) for t in range(n_t) for h in range(2)]
        dot = functools.partial(jnp.dot, preferred_element_type=F32)
        q2 = [load(q_ref, t) for t in range(n_t)]
        kc = [load(kc_ref, t) for t in range(n_t)]
        qm = [jnp.where(lo if h == 0 else jnp.logical_not(lo), q2[t], jnp.zeros_like(q2[t])) for t, h in items]
        sc = [jnp.where(mc, lax.dot_general(qm[i], kc[t], _NT, preferred_element_type=F32) * scale, -jnp.inf)
              for i, (t, h) in enumerate(items)]
        m = [jnp.max(s, axis=1, keepdims=True) for s in sc]
        if has_prev:
            kp = [load(kp_ref, t) for t in range(n_t)]
            sp = [jnp.where(mp, lax.dot_general(qm[i], kp[t], _NT, preferred_element_type=F32) * scale, -jnp.inf)
                  for i, (t, h) in enumerate(items)]
            m = [jnp.maximum(a, jnp.max(s, axis=1, keepdims=True)) for a, s in zip(m, sp)]
        pc = [jnp.exp(s - a).astype(BF16) for s, a in zip(sc, m)]
        vc = [load(vc_ref, t) for t in range(n_t)]
        acc = [dot(pc[i], vc[t]) for i, (t, h) in enumerate(items)]
        den = [dot(x, ones) for x in pc]
        if has_prev:
            pp = [jnp.exp(s - a).astype(BF16) for s, a in zip(sp, m)]
            vp = [load(vp_ref, t) for t in range(n_t)]
            acc = [a + dot(pp[i], vp[t]) for i, ((t, h), a) in enumerate(zip(items, acc))]
            den = [d + dot(x, ones) for d, x in zip(den, pp)]
        oh = [a / d for a, d in zip(acc, den)]
        lh = [a + jnp.log(d) for a, d in zip(m, den)]
        for t, (unit, ls) in enumerate(tiles):
            o_ref[unit + (ls,)] = jnp.where(lo, oh[2 * t], oh[2 * t + 1]).astype(o_ref.dtype).reshape(unit_shape + (128,))
            l_ref[unit + (ls,)] = jnp.where(lo, lh[2 * t], lh[2 * t + 1]).reshape(unit_shape + (128,))

    kcb, vcb = 3, 4
    ins = [(qkv, qcb, 0), (qkv, kcb, 0)] + ([(qkv, kcb, -1)] if has_prev else []) + [(qkv, vcb, 0)] + ([(qkv, vcb, -1)] if has_prev else [])
    return _att_call(p, n, bl, c, body, name=f"att_fwd{p}", ins=ins, outs=[BF16, F32])


def _att_bwd(p, qkv, qcb, o, do, lse, bl, c, heads):
    n = qkv.shape[0]
    _, grid, _, _, qb, chunk, first_fn = _att_geometry(p, n, bl)
    units, unit_shape = _att_units(p, n, bl)
    n_grid = len(grid)
    has_prev = first_fn is not None
    e = c // heads
    scale = e ** -0.5

    def body(*refs):
        if has_prev:
            q_ref, kc_ref, kp_ref, vc_ref, vp_ref, o_ref, do_ref, l_ref, dq_ref, dkc_ref, dkp_ref, dvc_ref, dvp_ref = refs
        else:
            q_ref, kc_ref, vc_ref, o_ref, do_ref, l_ref, dq_ref, dkc_ref, dvc_ref = refs
        ids = [pl.program_id(a) for a in range(n_grid)]
        mc, mp = _att_masks(p, qb, chunk)
        if has_prev:
            mp = jnp.logical_and(mp, jnp.logical_not(first_fn(ids)))
        lo = lax.broadcasted_iota(jnp.int32, (qb, 128), 1) < e
        tiles = [(unit, pl.ds(lt * 128, 128)) for unit in units for lt in range(c // 128)]
        n_t = len(tiles)
        load = lambda r, t: r[tiles[t][0] + (tiles[t][1],)].reshape(qb, 128)
        items = [(t, h) for t in range(n_t) for h in range(2)]
        nt_dot = lambda a, b: lax.dot_general(a, b, _NT, preferred_element_type=F32)
        tn_dot = lambda a, b: lax.dot_general(a, b, _TN, preferred_element_type=F32)
        dot = functools.partial(jnp.dot, preferred_element_type=F32)

        def store(r, t, v):
            r[tiles[t][0] + (tiles[t][1],)] = v.astype(r.dtype).reshape(unit_shape + (128,))

        sel = [lo if h == 0 else jnp.logical_not(lo) for t, h in items]
        q2, kc, vc, do2 = ([load(r, t) for t in range(n_t)] for r in (q_ref, kc_ref, vc_ref, do_ref))
        qm = [jnp.where(sel[i], q2[t], jnp.zeros_like(q2[t])) for i, (t, h) in enumerate(items)]
        dom = [jnp.where(sel[i], do2[t], jnp.zeros_like(do2[t])) for i, (t, h) in enumerate(items)]
        dod = [do2[t].astype(F32) * load(o_ref, t).astype(F32) for t in range(n_t)]
        lcol = [load(l_ref, t)[:, h * e:h * e + 1] for t, h in items]
        corr = [-jnp.sum(jnp.where(sel[i], dod[t], 0.0), axis=1, keepdims=True) for i, (t, h) in enumerate(items)]
        pc = [jnp.exp(jnp.where(mc, nt_dot(qm[i], kc[t]) * scale, -jnp.inf) - lcol[i]) for i, (t, h) in enumerate(items)]
        dsc = [(pc[i] * (nt_dot(dom[i], vc[t]) + corr[i]) * scale).astype(BF16) for i, (t, h) in enumerate(items)]
        pc = [x.astype(BF16) for x in pc]
        dq = [dot(dsc[i], kc[t]) for i, (t, h) in enumerate(items)]
        dkc = [tn_dot(dsc[2 * t], qm[2 * t]) + tn_dot(dsc[2 * t + 1], qm[2 * t + 1]) for t in range(n_t)]
        dvc = [tn_dot(pc[2 * t], dom[2 * t]) + tn_dot(pc[2 * t + 1], dom[2 * t + 1]) for t in range(n_t)]
        if has_prev:
            kp, vp = ([load(r, t) for t in range(n_t)] for r in (kp_ref, vp_ref))
            pp = [jnp.exp(jnp.where(mp, nt_dot(qm[i], kp[t]) * scale, -jnp.inf) - lcol[i]) for i, (t, h) in enumerate(items)]
            dsp = [(pp[i] * (nt_dot(dom[i], vp[t]) + corr[i]) * scale).astype(BF16) for i, (t, h) in enumerate(items)]
            pp = [x.astype(BF16) for x in pp]
            dq = [a + dot(dsp[i], kp[t]) for i, ((t, h), a) in enumerate(zip(items, dq))]
            dkp = [tn_dot(dsp[2 * t], qm[2 * t]) + tn_dot(dsp[2 * t + 1], qm[2 * t + 1]) for t in range(n_t)]
            dvp = [tn_dot(pp[2 * t], dom[2 * t]) + tn_dot(pp[2 * t + 1], dom[2 * t + 1]) for t in range(n_t)]
        for t in range(n_t):
            store(dq_ref, t, jnp.where(lo, dq[2 * t], dq[2 * t + 1]))
            store(dkc_ref, t, dkc[t])
            store(dvc_ref, t, dvc[t])
            if has_prev:
                store(dkp_ref, t, dkp[t])
                store(dvp_ref, t, dvp[t])

    kcb, vcb = 3, 4
    ins = [(qkv, qcb, 0), (qkv, kcb, 0)] + ([(qkv, kcb, -1)] if has_prev else []) + [(qkv, vcb, 0)] + ([(qkv, vcb, -1)] if has_prev else [])
    ins += [(o, 0, 0), (do, 0, 0), (lse, 0, 0)]
    res = _att_call(p, n, bl, c, body, name=f"att_bwd{p}", ins=ins, outs=[BF16] * (5 if has_prev else 3))
    if has_prev:
        dq, dkc, dkp, dvc, dvp = res
        return dq, dkc, dkp, dvc, dvp
    dq, dkc, dvc = res
    return dq, dkc, None, dvc, None


def _att_fold_prev(p, cur, prv, bl):
    if prv is None:
        return cur.astype(F32)
    n, c = cur.shape
    prefix, _, _, _, _, chunk, _ = _att_geometry(p, n, bl)
    v = prv.reshape(prefix + (c,)).astype(F32)
    shifted = jnp.concatenate([v[..., chunk:, :], jnp.zeros_like(v[..., :chunk, :])], axis=-2)
    return cur.astype(F32) + shifted.reshape(n, c)


def _attention_fwd(proj, c, bl, heads):
    n = proj.shape[0]
    qkv = _to_phase_order(proj, bl, col0=c, width=5 * c)
    outs = [_att_fwd(p, qkv, p, bl, c, heads) for p in range(3)]
    ins = [("row", o, c, 0) for o, _ in outs] + [("row", l, c, 0) for _, l in outs]
    o, lse = _rowwise(_combine_fwd_fn, "comb_fwd", n, ins, [(c, BF16), (c, F32)])
    return _from_phase_order(o, bl), (qkv, o, lse)


def _attention_bwd(do_tb, saved, bl, heads):
    qkv, o, lse = saved
    n, c = do_tb.shape
    do = _to_phase_order(do_tb, bl)
    dqs, dk, dv = [], 0.0, 0.0
    for p in range(3):
        dq, dkc, dkp, dvc, dvp = _att_bwd(p, qkv, p, o, do, lse, bl, c, heads)
        dqs.append(dq)
        dk = dk + _att_fold_prev(p, dkc, dkp, bl)
        dv = dv + _att_fold_prev(p, dvc, dvp, bl)
    return jnp.concatenate(dqs + [dk.astype(BF16), dv.astype(BF16)], axis=1)


ATT_HEADS = 8
SSM_GROUPS, SSM_STATE, SSM_GROUP = 32, 64, 16


def _row(v):
    return v.reshape(1, -1)


def _carried(result, carry, key, hidden):
    if carry.get(key) is None:
        return result
    result, hidden[key] = result
    return result


def _layer_fwd(x, w, p, bl, carry, late=None):
    n, d = x.shape
    c = d // 2
    hidden = {}
    h, = _rowwise(_rms_fwd_fn, "rms_fwd", n, [("row", x, d, 0), ("par", _row(p["norm1_g"]))], [(d, BF16)])
    proj = _carried(_mm(h, w["w_in"], "nn", BF16, "mm_in", comm=carry.get("mm_in")), carry, "mm_in", hidden)
    if late is not None:
        w = dict(w, **late(hidden["mm_in"]))
    disc, disc_vjp = jax.vjp(_ssm_disc, p["ssm_lambda_re"], p["ssm_lambda_im"], p["ssm_log_dt"], p["ssm_b_re"], p["ssm_b_im"])
    bbd, cdm, a8 = _ssm_pack(*disc, p["ssm_c_re"], p["ssm_c_im"])
    ypre, yg, s_all = _carried(_ssm_fwd(proj, bbd, cdm, a8, _row(p["ssm_d"]), bl, "ssm_fwd", comm=carry.get("ssm_fwd")),
                               carry, "ssm_fwd", hidden)
    zs = _mm(yg, w["w_ssm_glu"], "nn", BF16, "mm_glu")
    o, att = _attention_fwd(proj, c, bl, ATT_HEADS)
    ya = _mm(o, w["w_att_up"], "nn", BF16, "mm_att")
    w32 = jnp.concatenate([p["conv_w"], jnp.zeros((1, c), F32)], axis=0)
    hc, hconv = _conv_fwd(proj, 6, w32, _row(p["conv_b"]), _row(p["conv_ln_g"]), _row(p["conv_ln_b"]), bl, c, "conv_fwd")
    yc = _mm(hc, w["w_conv_pw2"], "nn", BF16, "mm_pw2")
    gates = [("row", proj, d, 4), ("row", proj, d, 5), ("row", proj, d, 6), ("par", _row(p["b_gate"]))]
    branches = [("row", zs, 2 * d, 0), ("row", ya, d, 0), ("row", yc, d, 0)]
    merged, = _rowwise(_merge_fwd_fn, "merge_fwd", n, gates + branches, [(d, BF16)])
    xm = _mm(merged, w["w_out"], "nn", F32, "mm_out", res=x)
    h2, = _rowwise(_rms_fwd_fn, "rms_fwd", n, [("row", xm, d, 0), ("par", _row(p["norm2_g"]))], [(d, BF16)])
    z = _carried(_mm(h2, w["w_ffn_in"], "nn", BF16, "mm_ffn_in", comm=carry.get("mm_ffn_in")), carry, "mm_ffn_in", hidden)
    f = z.shape[1] // 2
    a, = _rowwise(_swiglu_fwd_fn, "swiglu_fwd", n, [("row", z, 2 * f, 0)], [(f, BF16)], tm=256)
    xo = _mm(a, w["w_ffn_out"], "nn", F32, "mm_ffn_out", res=xm)
    saved = dict(x=x, h=h, proj=proj, disc_vjp=disc_vjp, bbd=bbd, cdm=cdm, a8=a8, ypre=ypre, yg=yg, s_all=s_all, zs=zs, o=o,
                 att=att, ya=ya, w32=w32, hc=hc, hconv=hconv, yc=yc, gates=gates, branches=branches, merged=merged, xm=xm,
                 h2=h2, z=z, a=a)
    return xo, saved, hidden, w


def _layer_bwd(dxo, s, w, p, bl, carry):
    n, d = dxo.shape
    c = d // 2
    g, bufs, hidden = {}, {}, {}
    f = s["a"].shape[1]

    def dw(key, a, dy, name):
        bufs[key] = _mm_dw(a, dy, name, 1 if key in ROW_SHARDED else N_CHIPS)

    da = _mm(dxo, w["w_ffn_out"], "nt", BF16, "mm_ffn_out_dx")
    dw("w_ffn_out", s["a"], dxo, "mm_ffn_out_dw")
    dz, = _rowwise(_swiglu_bwd_fn, "swiglu_bwd", n, [("row", s["z"], 2 * f, 0), ("row", da, f, 0)], [(2 * f, BF16)], tm=256)
    dh2 = _mm(dz, w["w_ffn_in"], "nt", F32, "mm_ffn_in_dx")
    dw("w_ffn_in", s["h2"], dz, "mm_ffn_in_dw")
    dxm, dg2 = _rowwise(_rms_bwd_fn, "rms_bwd", n, [("row", s["xm"], d, 0), ("par", _row(p["norm2_g"])), ("row", dh2, d, 0),
                                                   ("row", dxo, d, 0)], [(d, F32)], [d])
    g["norm2_g"] = dg2[0]
    dmerged = _mm(dxm, w["w_out"], "nt", BF16, "mm_out_dx")
    dw("w_out", s["merged"], dxm, "mm_out_dw")
    dgl, dzs, dya, dyc, dbg = _rowwise(_merge_bwd_fn, "merge_bwd", n, s["gates"] + s["branches"] + [("row", dmerged, d, 0)],
                                       [(3 * d, BF16), (2 * d, BF16), (d, BF16), (d, BF16)], [3 * d], tm=256)
    g["b_gate"] = dbg[0]
    dyg = _mm(dzs, w["w_ssm_glu"], "nt", BF16, "mm_glu_dx")
    dw("w_ssm_glu", s["yg"], dzs, "mm_glu_dw")
    du, dbb, dcd, dab, dd = _carried(
        _ssm_bwd(dyg, s["ypre"], s["proj"], s["s_all"], s["cdm"].transpose(0, 2, 1), s["bbd"].transpose(0, 2, 1), s["a8"],
                 _row(p["ssm_d"]), bl, "ssm_bwd", comm=carry.get("ssm_bwd")), carry, "ssm_bwd", hidden)
    dab_re, dab_im, dbb_re, dbb_im, g["ssm_c_re"], g["ssm_c_im"] = _ssm_unpack(dbb, dcd, dab, SSM_GROUPS, SSM_STATE, SSM_GROUP)
    (g["ssm_lambda_re"], g["ssm_lambda_im"], g["ssm_log_dt"], g["ssm_b_re"],
     g["ssm_b_im"]) = s["disc_vjp"]((dab_re, dab_im, dbb_re, dbb_im))
    g["ssm_d"] = dd[0]
    do = _mm(dya, w["w_att_up"], "nt", BF16, "mm_att_dx")
    dw("w_att_up", s["o"], dya, "mm_att_dw")
    dqkv = _attention_bwd(do, s["att"], bl, ATT_HEADS)
    dhc = _mm(dyc, w["w_conv_pw2"], "nt", BF16, "mm_pw2_dx")
    dw("w_conv_pw2", s["hc"], dyc, "mm_pw2_dw")
    dcv, dcw, dcb, dlg, dlb = _carried(
        _conv_bwd(s["proj"], 6, dhc, s["hconv"], s["w32"], _row(p["conv_ln_g"]), _row(p["conv_ln_b"]), bl, c, "conv_bwd",
                  comm=carry.get("conv_bwd")), carry, "conv_bwd", hidden)
    g["conv_w"], g["conv_b"], g["conv_ln_g"], g["conv_ln_b"] = dcw, dcb[0], dlg[0], dlb[0]
    dproj = _assemble_dproj(du, dqkv, dcv, dgl, bl)
    dh = _mm(dproj, w["w_in"], "nt", F32, "mm_in_dx")
    dw("w_in", s["h"], dproj, "mm_in_dw")
    dx, dg1 = _rowwise(_rms_bwd_fn, "rms_bwd", n, [("row", s["x"], d, 0), ("par", _row(p["norm1_g"])), ("row", dh, d, 0),
                                                  ("row", dxm, d, 0)], [(d, F32)], [d])
    g["norm1_g"] = dg1[0]
    return dx, g, bufs, hidden


WEIGHTS = ['norm1_g', 'w_in', 'b_gate', 'ssm_lambda_re', 'ssm_lambda_im', 'ssm_log_dt', 'ssm_b_re', 'ssm_b_im', 'ssm_c_re',
           'ssm_c_im', 'ssm_d', 'w_ssm_glu', 'w_att_up', 'conv_w', 'conv_b', 'conv_ln_g', 'conv_ln_b', 'w_conv_pw2', 'w_out',
           'norm2_g', 'w_ffn_in', 'w_ffn_out', 'final_g']
BIG = ['w_in', 'w_ssm_glu', 'w_att_up', 'w_conv_pw2', 'w_out', 'w_ffn_in', 'w_ffn_out']
ROW_SHARDED = ('w_out', 'w_ffn_out')
SMALL = [k for k in WEIGHTS if k not in BIG]
LANES = 1024
N_CHIPS = 4
ROW_TILE_BYTES = 36 * 1024 * 1024
MIN_SHARD_TILE = 1024


def _pad_rows(a, rows):
    return jnp.concatenate([a, jnp.zeros((rows - a.shape[0],) + a.shape[1:], a.dtype)], axis=0) if rows > a.shape[0] else a


def _row_tile(rows, width, n_arrays):
    best = 16
    for t in range(16, rows + 1, 16):
        if rows % t == 0 and t * width * 4 * n_arrays * 2 <= ROW_TILE_BYTES:
            best = t
    return best


def _flat_fn(fn, name, ins, n_out, rows):
    return _rowwise(fn, name, rows, [("row", a, LANES, 0) for a in ins], [(LANES, F32)] * n_out, tm=rows)


def _reduce_prepare(bufs):
    landed = _run_exchange(_SwapHalves([b16 for _, b16 in bufs]), "rs_swap")
    p32s, p16s = [], []
    for (b32, _), la in zip(bufs, landed):
        s, m, cs = b32.shape
        h = m // 2
        tm = _row_tile(h, cs, 4)

        def body(g_ref, l_ref, o32, o16):
            r = g_ref[...] + l_ref[...].astype(F32)
            o32[...] = r
            o16[...] = r.astype(BF16)

        piece = pl.BlockSpec((None, tm, cs), lambda j, i: (j, i, 0))
        mine = pl.BlockSpec((None, None, tm, cs), lambda j, i: (j, _core_index(), i, 0))
        p32, p16 = pl.pallas_call(
            body, name="rs_add", grid=(s, h // tm), in_specs=[mine, piece], out_specs=[piece, piece],
            out_shape=[jax.ShapeDtypeStruct((s, h, cs), F32), jax.ShapeDtypeStruct((s, h, cs), BF16)],
            compiler_params=_cparams(("parallel", "parallel")))(b32.reshape(s, 2, h, cs), la)
        p32s.append(p32)
        p16s.append(p16)
    return p32s, p16s


def _reduce_finish(p32s, arrived):
    reduced = []
    for p32, lb in zip(p32s, arrived):
        _, h, cs = lb.shape
        tm = _row_tile(h, cs, 5)

        def body(p_ref, a_ref, b_ref, c_ref, o_ref):
            o_ref[...] = ((p_ref[...] + a_ref[...].astype(F32)) + b_ref[...].astype(F32)) + c_ref[...].astype(F32)

        mine = pl.BlockSpec((None, tm, cs), lambda i: (_chip_index(), i, 0))
        other = [pl.BlockSpec((None, tm, cs), lambda i, k=k: (k, i, 0)) for k in range(3)]
        half = pl.BlockSpec((None, tm, cs), lambda i: (_core_index(), i, 0))
        reduced.append(pl.pallas_call(
            body, name="rs_sum", grid=(h // tm,), in_specs=[mine] + other, out_specs=half,
            out_shape=jax.ShapeDtypeStruct((2, h, cs), F32), compiler_params=_cparams(("parallel",)))(p32, lb, lb, lb))
    joined = _run_exchange(_JoinHalves(reduced), "rs_gather")
    return [j.reshape(2 * j.shape[1], j.shape[2]) for j in joined]


def _adamw_layers(w, g_layers, m, v):
    depth, rows, cs = w.shape
    tm = _row_tile(rows, cs, 8)
    nb = rows // tm

    def body(*refs):
        w_ref, m_ref, v_ref = refs[:3]
        g_refs = refs[3:3 + depth]
        go_ref, d_ref, mo_ref, vo_ref = refs[3 + depth:]
        layer = pl.program_id(0)
        g = g_refs[0][...]
        for l in range(1, depth):
            g = jnp.where(layer == l, g_refs[l][...], g)
        delta, mo, vo = _adamw_fn(w_ref[...], g, m_ref[...], v_ref[...])
        go_ref[...], d_ref[...], mo_ref[...], vo_ref[...] = g, delta, mo, vo

    stacked = pl.BlockSpec((None, tm, cs), lambda l, i: (l, i, 0))
    g_specs = [pl.BlockSpec((tm, cs), lambda l, i, k=k: (jnp.where(l == k, i, jnp.where(l < k, 0, nb - 1)), 0)) for k in range(depth)]
    return pl.pallas_call(
        body, name="adamw", grid=(depth, nb), in_specs=[stacked] * 3 + g_specs, out_specs=[stacked] * 4,
        out_shape=[jax.ShapeDtypeStruct(w.shape, F32)] * 4, compiler_params=_cparams(("arbitrary", "arbitrary")))(w, m, v, *g_layers)


def _sum4_fn(a, b, c, d):
    return (((a.astype(F32) + b.astype(F32)) + c.astype(F32)) + d.astype(F32),)


def _add2_fn(a, b):
    return (a + b,)


def kernel(x, norm1_g, w_in, b_gate, ssm_lambda_re, ssm_lambda_im, ssm_log_dt, ssm_b_re, ssm_b_im, ssm_c_re, ssm_c_im, ssm_d, w_ssm_glu, w_att_up, conv_w, conv_b, conv_ln_g, conv_ln_b, w_conv_pw2, w_out, norm2_g, w_ffn_in, w_ffn_out, final_g, loss_target, m_norm1_g, m_w_in, m_b_gate, m_ssm_lambda_re, m_ssm_lambda_im, m_ssm_log_dt, m_ssm_b_re, m_ssm_b_im, m_ssm_c_re, m_ssm_c_im, m_ssm_d, m_w_ssm_glu, m_w_att_up, m_conv_w, m_conv_b, m_conv_ln_g, m_conv_ln_b, m_w_conv_pw2, m_w_out, m_norm2_g, m_w_ffn_in, m_w_ffn_out, m_final_g, v_norm1_g, v_w_in, v_b_gate, v_ssm_lambda_re, v_ssm_lambda_im, v_ssm_log_dt, v_ssm_b_re, v_ssm_b_im, v_ssm_c_re, v_ssm_c_im, v_ssm_d, v_w_ssm_glu, v_w_att_up, v_conv_w, v_conv_b, v_conv_ln_g, v_conv_ln_b, v_w_conv_pw2, v_w_out, v_norm2_g, v_w_ffn_in, v_w_ffn_out, v_final_g):
    args = dict(locals())
    wts = {k: args[k] for k in WEIGHTS}
    mom = {k: args["m_" + k] for k in WEIGHTS}
    var = {k: args["v_" + k] for k in WEIGHTS}
    bl, seq, d = x.shape
    n = bl * seq
    depth = norm1_g.shape[0]
    cx, cy, cc = _position()
    me = 2 * cx + cy

    assert depth == 2, "the exchanges of layer 1 are hidden behind layer 0's kernels"
    first = BIG[:1]
    rest = BIG[1:]

    shards = lambda keys, l: [wts[k][l].astype(BF16) for k in keys]

    def whole(keys, gathered):
        out = {}
        for k, a in zip(keys, gathered):
            _, ks, cs = a.shape
            if k in ROW_SHARDED:
                out[k] = a.reshape(N_CHIPS * ks, cs)
            elif cs < MIN_SHARD_TILE:
                out[k] = a.transpose(1, 0, 2).reshape(ks, N_CHIPS * cs)
            else:
                out[k] = a
        return out

    fill = lambda gathered, own: [_own_slot(g, o) for g, o in zip(gathered, own)]
    own0 = shards(first, 0) + [conv_w]
    gathered = fill(_run_exchange(_GatherShards(own0), "gather_weights"), own0)
    conv_full = gathered[-1].transpose(1, 2, 0, 3).reshape(depth, CONV_WIDTH, -1)
    params = lambda l: dict({k: wts[k][l] for k in SMALL if k not in ("final_g", "conv_w")}, conv_w=conv_full[l])

    to_rows = lambda t: t.transpose(1, 0, 2).reshape(n, d)
    own = {"mm_in": shards(rest, 0), "ssm_fwd": shards(first, 1), "mm_ffn_in": shards(rest, 1)}
    xs, s0, hidden, w0 = _layer_fwd(to_rows(x), whole(first, gathered[:-1]), params(0), bl, {k: _GatherShards(v) for k, v in own.items()},
                                    late=lambda got: whole(rest, fill(got, own["mm_in"])))
    w1 = dict(whole(first, fill(hidden["ssm_fwd"], own["ssm_fwd"])), **whole(rest, fill(hidden["mm_ffn_in"], own["mm_ffn_in"])))
    full = [w0, w1]
    xs, s1, _, _ = _layer_fwd(xs, full[1], params(1), bl, {})
    dx, sq, dgf = _rowwise(_loss_fn, "loss_head", n, [("row", xs, d, 0), ("par", _row(final_g)), ("row", to_rows(loss_target), d, 0)],
                           [(d, F32)], [d, d])
    loss = lax.psum(0.5 * jnp.sum(sq) / d, ("x", "y", "c"))

    pieces = lambda bufs, keys: [tuple(b.reshape(N_CHIPS, -1, b.shape[-1]) for b in bufs[k]) for k in keys]
    dx, g1, bufs1, _ = _layer_bwd(dx, s1, full[1], params(1), bl, {})
    p32_1, p16_1 = _reduce_prepare(pieces(bufs1, BIG))
    dx, g0, bufs0, hidden = _layer_bwd(dx, s0, full[0], params(0), bl,
                                       {"ssm_bwd": _ScatterPieces(p16_1[:1]), "conv_bwd": _ScatterPieces(p16_1[1:])})
    red1 = _reduce_finish(p32_1, list(hidden["ssm_bwd"]) + list(hidden["conv_bwd"]))
    p32_0, p16_0 = _reduce_prepare(pieces(bufs0, BIG))
    red0 = _reduce_finish(p32_0, _run_exchange(_ScatterPieces(p16_0), "rs_scatter"))
    grads = {"final_g": dgf[0]}
    for k in SMALL:
        if k != "final_g":
            grads[k] = jnp.stack([g0[k], g1[k]])
    grad_x = dx.reshape(seq, bl, d).transpose(1, 0, 2)
    outs = {}
    for k, r0, r1 in zip(BIG, red0, red1):
        for tag, a in zip(("grad", "delta", "m", "v"), _adamw_layers(wts[k], [r0, r1], mom[k], var[k])):
            outs[tag, k] = a

    def flat1(t):
        v = jnp.concatenate([t[k].reshape(-1) for k in SMALL])
        rows = -(-v.size // (8 * LANES)) * 8
        return _pad_rows(v, rows * LANES).reshape(rows, LANES), rows

    def unflat1(flat, shapes):
        out, off, v = {}, 0, flat.reshape(-1)
        for k in SMALL:
            size = math.prod(shapes[k])
            out[k] = v[off:off + size].reshape(shapes[k])
            off += size
        return out

    grads["conv_w"] = grads["conv_w"][:, :CONV_WIDTH]
    gs, rows = flat1(grads)
    chip_sum, = _flat_fn(_add2_fn, "ar_add", [gs, _swap_sibling(gs, "ar_swap")], 1, rows)
    slots = _chip_allgather(chip_sum, "ar_gather")
    gs_red, = _flat_fn(_sum4_fn, "ar_sum", [slots[j] for j in range(N_CHIPS)], 1, rows)
    g_sm = unflat1(gs_red, {k: grads[k].shape for k in SMALL})
    cs = conv_w.shape[2]
    g_sm["conv_w"] = lax.dynamic_slice_in_dim(g_sm["conv_w"], me * cs, cs, axis=2)
    (w1, rows), (g1, _), (m1, _), (v1, _) = flat1(wts), flat1(g_sm), flat1(mom), flat1(var)
    sm_out = _flat_fn(_adamw_fn, "adamw_small", [w1, g1, m1, v1], 3, rows)
    shapes = {k: wts[k].shape for k in SMALL}
    for tag, a in zip(("delta", "m", "v"), sm_out):
        for k, t in unflat1(a, shapes).items():
            outs[tag, k] = t
    for k in SMALL:
        outs["grad", k] = g_sm[k]
    return (loss, grad_x, *[outs["grad", k] for k in WEIGHTS], *[outs["delta", k] for k in WEIGHTS],
            *[outs["m", k] for k in WEIGHTS], *[outs["v", k] for k in WEIGHTS])
```

```python
import functools
import math

import jax
import jax.numpy as jnp
from jax import lax
from jax.experimental import pallas as pl
from jax.experimental.pallas import tpu as pltpu

F32 = jnp.float32
BF16 = jnp.bfloat16
VMEM_LIMIT = 56 * 1024 * 1024


def _cparams(sem):
    return pltpu.CompilerParams(dimension_semantics=sem, vmem_limit_bytes=VMEM_LIMIT)


_DIMS = {"nn": (((1,), (0,)), ((), ())), "nt": (((1,), (1,)), ((), ())), "tn": (((0,), (0,)), ((), ()))}


MM_ROWS = 1024
MM_DW_VMEM_BYTES = 44 * 1024 * 1024
MM_SMALL_STEP = 1024 * 1024 * 512


def _div_tile(n, cap):
    best = None
    for t in range(128, min(n, cap) + 1, 128):
        if n % t == 0:
            best = t
    return best or n


def _mm(a, b, form, out_dtype, name, res=None, comm=None):
    sharded = b.ndim == 3
    kdim, cs = b.shape[-2], b.shape[-1]
    s = b.shape[0] if sharded else 1
    m = a.shape[0]
    tm = MM_ROWS if m % MM_ROWS == 0 else _div_tile(m, MM_ROWS)
    if form == "nn":
        n, kd = s * cs, kdim
        tn, tk = _div_tile(cs, 1792), _div_tile(kdim, 2048)
        per = cs // tn
        b_blk = (tk, tn)
        b_idx = (lambda i, j, k: (j // per, k, j % per)) if sharded else (lambda i, j, k: (k, j))
    else:
        n, kd = kdim, s * cs
        tn, tk = _div_tile(kdim, 1408), _div_tile(cs, 1792)
        per = cs // tk
        b_blk = (tn, tk)
        b_idx = (lambda i, j, k: (k // per, j, k % per)) if sharded else (lambda i, j, k: (j, k))
    nk = kd // tk
    if tm * tn * tk <= MM_SMALL_STEP and m % (2 * tm) == 0:
        tm *= 2
    a_spec = pl.BlockSpec((tm, tk), lambda i, j, k: (i, k))
    b_spec = pl.BlockSpec(((None,) + b_blk) if sharded else b_blk, b_idx)
    o_spec = pl.BlockSpec((tm, tn), lambda i, j, k: (i, j))
    dims = _DIMS[form]

    def body(*refs):
        a_ref, b_ref = refs[:2]
        r_ref = refs[2] if res is not None else None
        o_ref = refs[3] if res is not None else refs[2]
        p = lax.dot_general(a_ref[...].astype(BF16), b_ref[...], dims, preferred_element_type=F32)

        def finish(r):
            if r_ref is not None:
                r = r + r_ref[...]
            o_ref[...] = r.astype(out_dtype)

        if nk == 1:
            finish(p)
            return
        acc = refs[-1]
        k = pl.program_id(2)

        @pl.when(k == 0)
        def _():
            acc[...] = p

        @pl.when(k > 0)
        def _():
            acc[...] += p

        @pl.when(k == nk - 1)
        def _():
            finish(acc[...])

    ins = [a, b] + ([] if res is None else [res])
    in_specs = [a_spec, b_spec] + ([] if res is None else [o_spec])
    out = _pcall(body, name, (m // tm, n // tn, nk), in_specs, [o_spec], [jax.ShapeDtypeStruct((m, n), out_dtype)],
                 [pltpu.VMEM((tm, tn), F32)] if nk > 1 else [], ("parallel", "parallel", "arbitrary"), ins, comm)
    return out[0] if comm is None else (out[0][0], out[1])


def _mm_dw(a, dy, name, shards):
    r, m = a.shape
    c = dy.shape[1]
    cs = c // shards
    tm, tn = _div_tile(m, 1408), _div_tile(cs, 1408)
    fixed = tm * tn * (4 + 2 * (4 + 2))
    per_row = 2 * (tm * a.dtype.itemsize + tn * dy.dtype.itemsize)
    tk = max(t for t in (256, 512, 1024, 2048) if r % t == 0 and (t == 256 or fixed + t * per_row <= MM_DW_VMEM_BYTES))
    per = cs // tn
    nk = r // tk

    def body(a_ref, b_ref, o32, o16, acc):
        k = pl.program_id(2)
        p = lax.dot_general(a_ref[...].astype(BF16), b_ref[...].astype(BF16), _DIMS["tn"], preferred_element_type=F32)

        @pl.when(k == 0)
        def _():
            acc[...] = p

        @pl.when(k > 0)
        def _():
            acc[...] += p

        @pl.when(k == nk - 1)
        def _():
            o32[...] = acc[...]
            o16[...] = acc[...].astype(BF16)

    o_spec = pl.BlockSpec((None, tm, tn), lambda i, j, k: (j // per, i, j % per))
    shape = (shards, m, cs)
    in_specs = [pl.BlockSpec((tk, tm), lambda i, j, k: (k, i)), pl.BlockSpec((tk, tn), lambda i, j, k: (k, j))]
    return _pcall(body, name, (m // tm, c // tn, nk), in_specs, [o_spec, o_spec],
                  [jax.ShapeDtypeStruct(shape, F32), jax.ShapeDtypeStruct(shape, BF16)], [pltpu.VMEM((tm, tn), F32)],
                  ("parallel", "parallel", "arbitrary"), [a, dy])


def _core_index():
    return lax.axis_index("c")


def _chip_index():
    return 2 * lax.axis_index("x") + lax.axis_index("y")


def _rowwise(fn, name, n_rows, ins, outs, accs=(), tm=512):
    n_in, n_out, n_acc = len(ins), len(outs), len(accs)
    in_specs, args = [], []
    for spec in ins:
        if spec[0] == "row":
            _, arr, w, cb = spec
            in_specs.append(pl.BlockSpec((tm, w), lambda i, cb=cb: (i, cb)))
        else:
            arr = spec[1]
            in_specs.append(pl.BlockSpec(arr.shape, lambda i: (0, 0)))
        args.append(arr)
    out_specs = [pl.BlockSpec((tm, w), lambda i: (i, 0)) for w, _ in outs]
    out_specs += [pl.BlockSpec((1, w), lambda i: (0, 0)) for w in accs]
    out_shape = [jax.ShapeDtypeStruct((n_rows, w), dt) for w, dt in outs]
    out_shape += [jax.ShapeDtypeStruct((1, w), F32) for w in accs]

    def body(*refs):
        i = pl.program_id(0)
        res = fn(*[r[...] for r in refs[:n_in]])
        for o_ref, r in zip(refs[n_in:n_in + n_out], res[:n_out]):
            o_ref[...] = r.astype(o_ref.dtype)
        for a_ref, r in zip(refs[n_in + n_out:], res[n_out:]):
            @pl.when(i == 0)
            def _(a_ref=a_ref, r=r):
                a_ref[...] = r

            @pl.when(i > 0)
            def _(a_ref=a_ref, r=r):
                a_ref[...] += r

    return pl.pallas_call(
        body, name=name, grid=(n_rows // tm,), in_specs=in_specs, out_specs=out_specs, out_shape=out_shape,
        compiler_params=_cparams(("arbitrary",)))(*args)


EPS = 1e-6


def _sig(x):
    return 1.0 / (1.0 + jnp.exp(-x))


def _colsum(x):
    return jnp.sum(x, axis=0, keepdims=True)


def _rms_fwd_fn(x, g):
    r = lax.rsqrt(jnp.mean(x * x, axis=-1, keepdims=True) + EPS)
    return (x * r * g,)


def _rms_bwd_fn(x, g, dh, dres):
    dh = dh.astype(F32)
    r = lax.rsqrt(jnp.mean(x * x, axis=-1, keepdims=True) + EPS)
    xh = x * r
    dyg = dh * g
    dx = r * (dyg - xh * jnp.mean(dyg * xh, axis=-1, keepdims=True)) + dres
    return dx, _colsum(dh * xh)


def _loss_fn(x, g, t):
    d = x.shape[-1]
    r = lax.rsqrt(jnp.mean(x * x, axis=-1, keepdims=True) + EPS)
    xh = x * r
    err = xh * g - t
    dy = err * (1.0 / d)
    dyg = dy * g
    dx = r * (dyg - xh * jnp.mean(dyg * xh, axis=-1, keepdims=True))
    return dx, _colsum(err * err), _colsum(dy * xh)


def _swiglu_fwd_fn(z):
    f = z.shape[-1] // 2
    z1, z2 = z[:, :f].astype(F32), z[:, f:].astype(F32)
    return (z1 * _sig(z1) * z2,)


def _swiglu_bwd_fn(z, da):
    f = z.shape[-1] // 2
    z1, z2, da = z[:, :f].astype(F32), z[:, f:].astype(F32), da.astype(F32)
    s = _sig(z1)
    dz1 = da * z2 * (s * (1.0 + z1 * (1.0 - s)))
    dz2 = da * (z1 * s)
    return (jnp.concatenate([dz1, dz2], axis=1),)


def _merge_fwd_fn(g0, g1, g2, bg, zs, ya, yc):
    d = ya.shape[-1]
    bg = bg.astype(F32)
    zs = zs.astype(F32)
    ys = zs[:, :d] * _sig(zs[:, d:])
    m = _sig(g0.astype(F32) + bg[:, :d]) * ys
    m = m + _sig(g1.astype(F32) + bg[:, d:2 * d]) * ya.astype(F32)
    m = m + _sig(g2.astype(F32) + bg[:, 2 * d:]) * yc.astype(F32)
    return (m,)


def _merge_bwd_fn(g0, g1, g2, bg, zs, ya, yc, dm):
    d = ya.shape[-1]
    bg = bg.astype(F32)
    zs = zs.astype(F32)
    dm = dm.astype(F32)
    z1, s2 = zs[:, :d], _sig(zs[:, d:])
    ys = z1 * s2
    s0 = _sig(g0.astype(F32) + bg[:, :d])
    s1 = _sig(g1.astype(F32) + bg[:, d:2 * d])
    s3 = _sig(g2.astype(F32) + bg[:, 2 * d:])
    dgl = jnp.concatenate([dm * ys * s0 * (1.0 - s0), dm * ya.astype(F32) * s1 * (1.0 - s1),
                           dm * yc.astype(F32) * s3 * (1.0 - s3)], axis=1)
    dys = dm * s0
    dzs = jnp.concatenate([dys * s2, dys * z1 * s2 * (1.0 - s2)], axis=1)
    return dgl, dzs, dm * s1, dm * s3, _colsum(dgl)


def _combine_fwd_fn(o0, o1, o2, l0, l1, l2):
    m = jnp.maximum(jnp.maximum(l0, l1), l2)
    e0, e1, e2 = jnp.exp(l0 - m), jnp.exp(l1 - m), jnp.exp(l2 - m)
    den = e0 + e1 + e2
    return (e0 * o0.astype(F32) + e1 * o1.astype(F32) + e2 * o2.astype(F32)) / den, m + jnp.log(den)


ADAM_LR, ADAM_B1, ADAM_B2, ADAM_EPS, ADAM_WD, ADAM_STEP = 0.001, 0.9, 0.999, 1e-08, 0.01, 10


def _adamw_fn(w, g, m, v):
    m = ADAM_B1 * m + (1.0 - ADAM_B1) * g
    v = ADAM_B2 * v + (1.0 - ADAM_B2) * (g * g)
    m_hat = m / (1.0 - ADAM_B1 ** ADAM_STEP)
    v_hat = v / (1.0 - ADAM_B2 ** ADAM_STEP)
    delta = -ADAM_LR * (m_hat / (jnp.sqrt(v_hat) + ADAM_EPS) + ADAM_WD * w)
    return delta, m, v


CONV_WIDTH = 31


def _conv_fwd(proj, cb, w32, conv_b, ln_g, ln_b, bl, c, name, tm=512):
    n = proj.shape[0]
    hp = (CONV_WIDTH - 1) * bl
    nt = n // tm

    def body(ap_ref, gp_ref, a_ref, g_ref, w_ref, cb_ref, lg_ref, lb_ref, hc_ref, hconv_ref, ext):
        i = pl.program_id(0)
        ext[pl.ds(hp, tm), :] = a_ref[...].astype(F32) * _sig(g_ref[...].astype(F32))
        hgp = ap_ref[pl.ds(tm - hp, hp), :].astype(F32) * _sig(gp_ref[pl.ds(tm - hp, hp), :].astype(F32))
        ext[pl.ds(0, hp), :] = jnp.where(i > 0, hgp, 0.0)
        acc = jnp.zeros((tm, c), F32) + cb_ref[...]
        for j in range(CONV_WIDTH):
            acc = acc + w_ref[j:j + 1, :] * ext[pl.ds(j * bl, tm), :]
        hconv_ref[...] = acc.astype(hconv_ref.dtype)
        h = hconv_ref[...].astype(F32)
        mu = jnp.mean(h, axis=-1, keepdims=True)
        xc = h - mu
        var = jnp.mean(xc * xc, axis=-1, keepdims=True)
        hn = xc * lax.rsqrt(var + EPS) * lg_ref[...] + lb_ref[...]
        hc_ref[...] = (hn * _sig(hn)).astype(hc_ref.dtype)

    prev = lambda i, k: (jnp.maximum(i - 1, 0), k)
    par = lambda arr: pl.BlockSpec(arr.shape, lambda i: (0, 0))
    return pl.pallas_call(
        body, name=name, grid=(nt,),
        in_specs=[pl.BlockSpec((tm, c), functools.partial(prev, k=cb)), pl.BlockSpec((tm, c), functools.partial(prev, k=cb + 1)),
                  pl.BlockSpec((tm, c), lambda i: (i, cb)), pl.BlockSpec((tm, c), lambda i: (i, cb + 1)),
                  par(w32), par(conv_b), par(ln_g), par(ln_b)],
        out_specs=[pl.BlockSpec((tm, c), lambda i: (i, 0))] * 2,
        out_shape=[jax.ShapeDtypeStruct((n, c), BF16)] * 2,
        scratch_shapes=[pltpu.VMEM((hp + tm, c), F32)],
        compiler_params=_cparams(("arbitrary",)))(proj, proj, proj, proj, w32, conv_b, ln_g, ln_b)


def _conv_bwd(proj, cb, dhc, hconv, w32, ln_g, ln_b, bl, c, name, tm=512, comm=None):
    n = proj.shape[0]
    hp = (CONV_WIDTH - 1) * bl
    nt = n // tm

    def ln_bwd(d, h, lg, lb):
        d, h = d.astype(F32), h.astype(F32)
        mu = jnp.mean(h, axis=-1, keepdims=True)
        xc = h - mu
        rstd = lax.rsqrt(jnp.mean(xc * xc, axis=-1, keepdims=True) + EPS)
        xh = xc * rstd
        hn = xh * lg + lb
        s = _sig(hn)
        dhn = d * (s * (1.0 + hn * (1.0 - s)))
        dxh = dhn * lg
        dh = rstd * (dxh - jnp.mean(dxh, axis=-1, keepdims=True) - xh * jnp.mean(dxh * xh, axis=-1, keepdims=True))
        return dh, dhn, xh

    def body(ap_ref, gp_ref, a_ref, g_ref, d_ref, dn_ref, h_ref, hn_ref, w_ref, lg_ref, lb_ref,
             dcv_ref, dw_ref, dcb_ref, dlg_ref, dlb_ref, ext_h, ext_d):
        i = pl.program_id(0)
        lg, lb = lg_ref[...], lb_ref[...]
        a, g = a_ref[...].astype(F32), g_ref[...].astype(F32)
        sg = _sig(g)
        ext_h[pl.ds(hp, tm), :] = a * sg
        hgp = ap_ref[pl.ds(tm - hp, hp), :].astype(F32) * _sig(gp_ref[pl.ds(tm - hp, hp), :].astype(F32))
        ext_h[pl.ds(0, hp), :] = jnp.where(i > 0, hgp, 0.0)
        dh, dhn, xh = ln_bwd(d_ref[...], h_ref[...], lg, lb)
        ext_d[pl.ds(0, tm), :] = dh
        dh_n, _, _ = ln_bwd(dn_ref[pl.ds(0, hp), :], hn_ref[pl.ds(0, hp), :], lg, lb)
        ext_d[pl.ds(tm, hp), :] = jnp.where(i < nt - 1, dh_n, 0.0)

        @pl.when(i == 0)
        def _():
            dw_ref[...] = jnp.zeros_like(dw_ref)
            dcb_ref[...] = jnp.zeros_like(dcb_ref)
            dlg_ref[...] = jnp.zeros_like(dlg_ref)
            dlb_ref[...] = jnp.zeros_like(dlb_ref)

        dcb_ref[...] += _colsum(dh)
        dlg_ref[...] += _colsum(dhn * xh)
        dlb_ref[...] += _colsum(dhn)
        dhg = jnp.zeros((tm, c), F32)
        for j in range(CONV_WIDTH):
            dhg = dhg + w_ref[j:j + 1, :] * ext_d[pl.ds((CONV_WIDTH - 1 - j) * bl, tm), :]
            dw_ref[j:j + 1, :] += _colsum(dh * ext_h[pl.ds(j * bl, tm), :])
        dcv_ref[...] = jnp.concatenate([dhg * sg, dhg * a * sg * (1.0 - sg)], axis=1).astype(dcv_ref.dtype)

    prev = lambda i, k: (jnp.maximum(i - 1, 0), k)
    nxt = lambda i: (jnp.minimum(i + 1, nt - 1), 0)
    cur = lambda i: (i, 0)
    par = lambda arr: pl.BlockSpec(arr.shape, lambda i: (0, 0))
    acc = lambda r: pl.BlockSpec((r, c), lambda i: (0, 0))
    in_specs = [pl.BlockSpec((tm, c), functools.partial(prev, k=cb)), pl.BlockSpec((tm, c), functools.partial(prev, k=cb + 1)),
                pl.BlockSpec((tm, c), lambda i: (i, cb)), pl.BlockSpec((tm, c), lambda i: (i, cb + 1)),
                pl.BlockSpec((tm, c), cur), pl.BlockSpec((tm, c), nxt), pl.BlockSpec((tm, c), cur), pl.BlockSpec((tm, c), nxt),
                par(w32), par(ln_g), par(ln_b)]
    out_shape = [jax.ShapeDtypeStruct((n, 2 * c), BF16), jax.ShapeDtypeStruct((32, c), F32)] + [jax.ShapeDtypeStruct((1, c), F32)] * 3
    return _pcall(body, name, (nt,), in_specs, [pl.BlockSpec((tm, 2 * c), cur), acc(32), acc(1), acc(1), acc(1)], out_shape,
                  [pltpu.VMEM((hp + tm, c), F32), pltpu.VMEM((hp + tm, c), F32)], ("arbitrary",),
                  [proj, proj, proj, proj, dhc, dhc, hconv, hconv, w32, ln_g, ln_b], comm)


SSM_CH = 128
_GELU_C = 0.7978845608028654


def _gelu(x):
    return 0.5 * x * (1.0 + jnp.tanh(_GELU_C * (x + 0.044715 * x * x * x)))


def _gelu_grad(x):
    th = jnp.tanh(_GELU_C * (x + 0.044715 * x * x * x))
    return 0.5 * (1.0 + th) + 0.5 * x * (1.0 - th * th) * (_GELU_C * (1.0 + 3.0 * 0.044715 * x * x))


def _ssm_disc(lam_re, lam_im, log_dt, b_re, b_im):
    dt = jnp.exp(log_dt)[:, None]
    mag = jnp.exp(lam_re * dt)
    ab_re = mag * jnp.cos(lam_im * dt)
    ab_im = mag * jnp.sin(lam_im * dt)
    nr, ni = ab_re - 1.0, ab_im
    den = lam_re * lam_re + lam_im * lam_im
    z_re = ((nr * lam_re + ni * lam_im) / den)[..., None]
    z_im = ((ni * lam_re - nr * lam_im) / den)[..., None]
    return ab_re, ab_im, z_re * b_re - z_im * b_im, z_re * b_im + z_im * b_re


def _ssm_pack(ab_re, ab_im, bb_re, bb_im, c_re, c_im):
    g, p, h = bb_re.shape
    gc = SSM_CH // h
    nc = g // gc
    eye = jnp.eye(gc, dtype=F32)
    blk = lambda x: jnp.einsum("qgph,gk->qghkp", x.reshape(nc, gc, p, h), eye).reshape(nc, gc * h, gc * p)
    bbd = jnp.concatenate([blk(bb_re), blk(bb_im)], axis=2).astype(BF16)
    blc = lambda x: jnp.einsum("qghp,gk->qgpkh", x.reshape(nc, gc, h, p), eye).reshape(nc, gc * p, gc * h)
    cdm = jnp.concatenate([blc(c_re), blc(-c_im)], axis=1).astype(BF16)
    a = jnp.concatenate([ab_re.reshape(nc, gc * p), ab_im.reshape(nc, gc * p)], axis=1)
    a8 = jnp.broadcast_to(a[:, None, :], (nc, 8, 2 * gc * p)).reshape(nc * 8, 2 * gc * p)
    return bbd, cdm, a8


def _ssm_unpack(dbb, dcd, da, g, p, h):
    gc = SSM_CH // h
    nc = g // gc
    ph = gc * p
    eye = jnp.eye(gc, dtype=F32)
    dia = lambda x, o: jnp.einsum("qgpkh,gk->" + o, x.reshape(nc, gc, p, gc, h), eye).reshape((g, p, h) if o == "qgph" else (g, h, p))
    das = da.reshape(nc, 8, 2 * ph).sum(axis=1)
    return (das[:, :ph].reshape(g, p), das[:, ph:].reshape(g, p), dia(dbb[:, :ph], "qgph"), dia(dbb[:, ph:], "qgph"),
            dia(dcd[:, :ph], "qghp"), -dia(dcd[:, ph:], "qghp"))


def _ssm_fwd(proj, bbd, cdm, a8, dskip, bl, name, tm=1024, comm=None):
    n = proj.shape[0]
    nc, ch, p2 = bbd.shape
    ph = p2 // 2
    nt = n // tm
    nsub = 8 // bl

    def body(u_ref, bb_ref, cd_ref, a_ref, d_ref, ypre_ref, yg_ref, s_ref, bu, carry):
        t = pl.program_id(1)

        @pl.when(t == 0)
        def _():
            carry[...] = jnp.zeros_like(carry)

        u = u_ref[...]
        bu[...] = jnp.dot(u, bb_ref[0], preferred_element_type=F32)
        a_re, a_im = a_ref[:, :ph], a_ref[:, ph:]
        row = lax.broadcasted_iota(jnp.int32, (8, ph), 0)

        def step(k, c):
            cre, cim = c
            r0 = pl.multiple_of(k * 8, 8)
            bre, bim = bu[pl.ds(r0, 8), :ph], bu[pl.ds(r0, 8), ph:]
            sre, sim = cre, cim
            for sub in range(nsub):
                xre, xim = pltpu.roll(cre, bl, 0), pltpu.roll(cim, bl, 0)
                cre = a_re * xre - a_im * xim + bre
                cim = a_re * xim + a_im * xre + bim
                if sub == 0:
                    sre, sim = cre, cim
                else:
                    sel = row >= sub * bl
                    sre, sim = jnp.where(sel, cre, sre), jnp.where(sel, cim, sim)
            bu[pl.ds(r0, 8), :ph] = sre
            bu[pl.ds(r0, 8), ph:] = sim
            return sre, sim

        cre, cim = lax.fori_loop(0, tm // 8, step, (carry[:, :ph], carry[:, ph:]))
        carry[:, :ph] = cre
        carry[:, ph:] = cim
        s16 = bu[...].astype(BF16)
        s_ref[...] = s16
        y = jnp.dot(s16, cd_ref[0], preferred_element_type=F32) + d_ref[...] * u.astype(F32)
        ypre_ref[...] = y
        yg_ref[...] = _gelu(y).astype(yg_ref.dtype)

    in_specs = [pl.BlockSpec((tm, ch), lambda q, t: (t, q)), pl.BlockSpec((1, ch, p2), lambda q, t: (q, 0, 0)),
                pl.BlockSpec((1, p2, ch), lambda q, t: (q, 0, 0)), pl.BlockSpec((8, p2), lambda q, t: (q, 0)),
                pl.BlockSpec((1, ch), lambda q, t: (0, q))]
    out_specs = [pl.BlockSpec((tm, ch), lambda q, t: (t, q)), pl.BlockSpec((tm, ch), lambda q, t: (t, q)),
                 pl.BlockSpec((tm, p2), lambda q, t: (t, q))]
    out_shape = [jax.ShapeDtypeStruct((n, nc * ch), F32), jax.ShapeDtypeStruct((n, nc * ch), BF16),
                 jax.ShapeDtypeStruct((n, nc * p2), BF16)]
    return _pcall(body, name, (nc, nt), in_specs, out_specs, out_shape, [pltpu.VMEM((tm, p2), F32), pltpu.VMEM((8, p2), F32)],
                  ("parallel", "arbitrary"), [proj, bbd, cdm, a8, dskip], comm)


def _ssm_bwd(dyg, ypre, proj, s_all, cdt, bbt, a8, dskip, bl, name, tm=1024, comm=None):
    n = proj.shape[0]
    nc, ch, p2 = cdt.shape
    ph = p2 // 2
    nt = n // tm
    nsub = 8 // bl
    tn_dims = (((0,), (0,)), ((), ()))

    def body(dyg_ref, ypre_ref, u_ref, s_ref, cdt_ref, bbt_ref, a_ref, d_ref,
             du_ref, dbb_ref, dcd_ref, da_ref, dd_ref, ds, s32, carry):
        t = pl.program_id(1)

        @pl.when(t == 0)
        def _():
            carry[...] = jnp.zeros_like(carry)
            dbb_ref[...] = jnp.zeros_like(dbb_ref)
            dcd_ref[...] = jnp.zeros_like(dcd_ref)
            da_ref[...] = jnp.zeros_like(da_ref)
            dd_ref[...] = jnp.zeros_like(dd_ref)

        dyp = dyg_ref[...].astype(F32) * _gelu_grad(ypre_ref[...])
        u = u_ref[...]
        dd_ref[...] += _colsum(dyp * u.astype(F32))
        dyp16 = dyp.astype(BF16)
        ds[...] = jnp.dot(dyp16, cdt_ref[0], preferred_element_type=F32)
        s16 = s_ref[...]
        s32[...] = s16.astype(F32)
        a_re, a_im = a_ref[:, :ph], a_ref[:, ph:]
        row = lax.broadcasted_iota(jnp.int32, (8, ph), 0)
        back = 8 - bl

        def step(kk, c):
            lre, lim, acr, aci = c
            r0 = pl.multiple_of((tm // 8 - 1 - kk) * 8, 8)
            dre, dim = ds[pl.ds(r0, 8), :ph], ds[pl.ds(r0, 8), ph:]
            sre, sim = s32[pl.ds(r0, 8), :ph], s32[pl.ds(r0, 8), ph:]
            ore, oim, ire, iim = lre, lim, lre, lim
            for sub in range(nsub - 1, -1, -1):
                xre, xim = pltpu.roll(lre, back, 0), pltpu.roll(lim, back, 0)
                lre = a_re * xre + a_im * xim + dre
                lim = a_re * xim - a_im * xre + dim
                if sub == nsub - 1:
                    ore, oim, ire, iim = lre, lim, xre, xim
                else:
                    sel = row < (sub + 1) * bl
                    ore, oim = jnp.where(sel, lre, ore), jnp.where(sel, lim, oim)
                    ire, iim = jnp.where(sel, xre, ire), jnp.where(sel, xim, iim)
            ds[pl.ds(r0, 8), :ph] = ore
            ds[pl.ds(r0, 8), ph:] = oim
            acr = acr + sre * ire + sim * iim
            aci = aci + sre * iim - sim * ire
            return ore, oim, acr, aci

        z = jnp.zeros((8, ph), F32)
        lre, lim, acr, aci = lax.fori_loop(0, tm // 8, step, (carry[:, :ph], carry[:, ph:], z, z))
        carry[:, :ph] = lre
        carry[:, ph:] = lim
        da_ref[:, :ph] += acr
        da_ref[:, ph:] += aci
        lam16 = ds[...].astype(BF16)
        du = jnp.dot(lam16, bbt_ref[0], preferred_element_type=F32) + d_ref[...] * dyp
        du_ref[...] = du.astype(du_ref.dtype)
        dbb_ref[0] += lax.dot_general(lam16, u, tn_dims, preferred_element_type=F32)
        dcd_ref[0] += lax.dot_general(s16, dyp16, tn_dims, preferred_element_type=F32)

    rev = lambda q, t: (nt - 1 - t, q)
    in_specs = [pl.BlockSpec((tm, ch), rev), pl.BlockSpec((tm, ch), rev), pl.BlockSpec((tm, ch), rev),
                pl.BlockSpec((tm, p2), rev), pl.BlockSpec((1, ch, p2), lambda q, t: (q, 0, 0)),
                pl.BlockSpec((1, p2, ch), lambda q, t: (q, 0, 0)), pl.BlockSpec((8, p2), lambda q, t: (q, 0)),
                pl.BlockSpec((1, ch), lambda q, t: (0, q))]
    out_specs = [pl.BlockSpec((tm, ch), rev), pl.BlockSpec((1, p2, ch), lambda q, t: (q, 0, 0)),
                 pl.BlockSpec((1, p2, ch), lambda q, t: (q, 0, 0)), pl.BlockSpec((8, p2), lambda q, t: (q, 0)),
                 pl.BlockSpec((1, ch), lambda q, t: (0, q))]
    out_shape = [jax.ShapeDtypeStruct((n, nc * ch), BF16), jax.ShapeDtypeStruct((nc, p2, ch), F32),
                 jax.ShapeDtypeStruct((nc, p2, ch), F32), jax.ShapeDtypeStruct((nc * 8, p2), F32),
                 jax.ShapeDtypeStruct((1, nc * ch), F32)]
    return _pcall(body, name, (nc, nt), in_specs, out_specs, out_shape,
                  [pltpu.VMEM((tm, p2), F32), pltpu.VMEM((tm, p2), F32), pltpu.VMEM((8, p2), F32)],
                  ("parallel", "arbitrary"), [dyg, ypre, proj, s_all, cdt, bbt, a8, dskip], comm)


_MESH = pl.DeviceIdType.MESH
_HBM = pl.BlockSpec(memory_space=pltpu.HBM)


def _position():
    return lax.axis_index("x"), lax.axis_index("y"), lax.axis_index("c")


def _other_chips(x, y):
    return [((1 - x, y), 2 * (1 - x) + y), ((x, 1 - y), 2 * x + 1 - y), ((1 - x, 1 - y), 2 * (1 - x) + 1 - y)]


def _swap_sibling(v, name):
    def body(v_ref, got_ref, send_sem, recv_sem):
        x, y, c = _position()
        cp = pltpu.make_async_remote_copy(src_ref=v_ref, dst_ref=got_ref, send_sem=send_sem, recv_sem=recv_sem,
                                          device_id=(x, y, 1 - c), device_id_type=_MESH)
        cp.start()
        cp.wait()

    return pl.pallas_call(
        body, name=name, in_specs=[_HBM], out_specs=_HBM, out_shape=jax.ShapeDtypeStruct(v.shape, v.dtype),
        scratch_shapes=[pltpu.SemaphoreType.DMA, pltpu.SemaphoreType.DMA])(v)


def _own_slot(gathered, own):
    return lax.dynamic_update_index_in_dim(gathered, own, _chip_index(), 0)


def _chip_allgather(v, name):
    def body(v_ref, out_ref, send_sems, recv_sems):
        x, y, c = _position()
        me = 2 * x + y
        sends = []
        for k, (chip, idx) in enumerate(_other_chips(x, y)):
            cp = pltpu.make_async_remote_copy(src_ref=v_ref, dst_ref=out_ref.at[me], send_sem=send_sems.at[k],
                                              recv_sem=recv_sems.at[k], device_id=(*chip, c), device_id_type=_MESH)
            cp.start()
            sends.append(cp)
        for k, (chip, idx) in enumerate(_other_chips(x, y)):
            pltpu.make_async_remote_copy(src_ref=v_ref, dst_ref=out_ref.at[idx], send_sem=send_sems.at[k],
                                         recv_sem=recv_sems.at[k], device_id=(*chip, c), device_id_type=_MESH).wait_recv()
        for cp in sends:
            cp.wait_send()

    out = pl.pallas_call(
        body, name=name, in_specs=[_HBM], out_specs=_HBM, out_shape=jax.ShapeDtypeStruct((4,) + tuple(v.shape), v.dtype),
        scratch_shapes=[pltpu.SemaphoreType.DMA((3,)), pltpu.SemaphoreType.DMA((3,))])(v)
    return _own_slot(out, v)


def _remote(src, dst, send_sems, recv_sems, s, device):
    return pltpu.make_async_remote_copy(src_ref=src, dst_ref=dst, send_sem=send_sems.at[s], recv_sem=recv_sems.at[s],
                                        device_id=device, device_id_type=_MESH)


class _Exchange:
    def __init__(self, ins, out_shapes, n_sems, aliases=None):
        self.ins, self.out_shapes, self.n_sems, self.aliases = list(ins), list(out_shapes), n_sems, aliases or {}

    def sem_shapes(self):
        return [pltpu.SemaphoreType.DMA((self.n_sems,)), pltpu.SemaphoreType.DMA((self.n_sems,))]


def _halves(ref, c, axis=0):
    h = ref.shape[axis] // 2
    idx = (slice(None),) * axis
    return ref.at[idx + (pl.ds(c * h, h),)], ref.at[idx + (pl.ds((1 - c) * h, h),)]


class _GatherShards(_Exchange):
    def __init__(self, ws):
        super().__init__(ws, [jax.ShapeDtypeStruct((N_CHIPS,) + tuple(w.shape), w.dtype) for w in ws], 6 * len(ws))

    def start(self, w_refs, out_refs, sems):
        send_sems, recv_sems = sems
        x, y, c = _position()
        me = 2 * x + y
        for i, (w, out) in enumerate(zip(w_refs, out_refs)):
            for k, (chip, idx) in enumerate(_other_chips(x, y)):
                _remote(_halves(w, c)[0], _halves(out.at[me], c)[0], send_sems, recv_sems, 6 * i + k, (*chip, c)).start()

    def finish(self, w_refs, out_refs, sems):
        send_sems, recv_sems = sems
        x, y, c = _position()
        sibling = (x, y, 1 - c)
        others = _other_chips(x, y)
        for i, out in enumerate(out_refs):
            for k, (chip, idx) in enumerate(others):
                landed = _halves(out.at[idx], c)[0]
                _remote(landed, landed, send_sems, recv_sems, 6 * i + k, (*chip, c)).wait_recv()
                _remote(landed, landed, send_sems, recv_sems, 6 * i + 3 + k, sibling).start()
        for i, (w, out) in enumerate(zip(w_refs, out_refs)):
            for k, (chip, idx) in enumerate(others):
                mine, theirs = _halves(out.at[idx], c)
                _remote(theirs, theirs, send_sems, recv_sems, 6 * i + 3 + k, sibling).wait_recv()
                _remote(mine, mine, send_sems, recv_sems, 6 * i + 3 + k, sibling).wait_send()
                _remote(_halves(w, c)[0], mine, send_sems, recv_sems, 6 * i + k, (*chip, c)).wait_send()


class _SwapHalves(_Exchange):
    def __init__(self, gs):
        shapes = [jax.ShapeDtypeStruct((g.shape[0], g.shape[1] // 2) + tuple(g.shape[2:]), g.dtype) for g in gs]
        super().__init__(gs, shapes, N_CHIPS * len(gs))

    def _copies(self, g_refs, out_refs, sems):
        x, y, c = _position()
        return [_remote(_halves(g.at[j], c)[1], out.at[j], sems[0], sems[1], N_CHIPS * i + j, (x, y, 1 - c))
                for i, (g, out) in enumerate(zip(g_refs, out_refs)) for j in range(N_CHIPS)]

    def start(self, g_refs, out_refs, sems):
        for cp in self._copies(g_refs, out_refs, sems):
            cp.start()

    def finish(self, g_refs, out_refs, sems):
        for cp in self._copies(g_refs, out_refs, sems):
            cp.wait()


class _ScatterPieces(_Exchange):
    def __init__(self, ps):
        super().__init__(ps, [jax.ShapeDtypeStruct((3,) + tuple(p.shape[1:]), p.dtype) for p in ps], 3 * len(ps))

    def _copies(self, p_refs, out_refs, sems):
        x, y, c = _position()
        return [_remote(p.at[idx], out.at[k], sems[0], sems[1], 3 * i + k, (*chip, c))
                for i, (p, out) in enumerate(zip(p_refs, out_refs)) for k, (chip, idx) in enumerate(_other_chips(x, y))]

    def start(self, p_refs, out_refs, sems):
        for cp in self._copies(p_refs, out_refs, sems):
            cp.start()

    def finish(self, p_refs, out_refs, sems):
        for cp in self._copies(p_refs, out_refs, sems):
            cp.wait()


class _JoinHalves(_Exchange):
    def __init__(self, rs):
        super().__init__(rs, [jax.ShapeDtypeStruct(r.shape, r.dtype) for r in rs], len(rs), {i: i for i in range(len(rs))})

    def start(self, r_refs, out_refs, sems):
        x, y, c = _position()
        for i, out in enumerate(out_refs):
            _remote(out.at[c], out.at[c], sems[0], sems[1], i, (x, y, 1 - c)).start()

    def finish(self, r_refs, out_refs, sems):
        x, y, c = _position()
        for i, out in enumerate(out_refs):
            _remote(out.at[c], out.at[c], sems[0], sems[1], i, (x, y, 1 - c)).wait_send()
            _remote(out.at[1 - c], out.at[1 - c], sems[0], sems[1], i, (x, y, 1 - c)).wait_recv()


def _run_exchange(ex, name):
    def body(*refs):
        ins, outs, sems = refs[:len(ex.ins)], refs[len(ex.ins):len(ex.ins) + len(ex.out_shapes)], refs[-2:]
        ex.start(ins, outs, sems)
        ex.finish(ins, outs, sems)

    return pl.pallas_call(body, name=name, in_specs=[_HBM] * len(ex.ins), out_specs=[_HBM] * len(ex.out_shapes),
                          out_shape=ex.out_shapes, scratch_shapes=ex.sem_shapes(), input_output_aliases=ex.aliases)(*ex.ins)


def _pcall(body, name, grid, in_specs, out_specs, out_shape, scratch_shapes, semantics, args, comm=None):
    if comm is None:
        return pl.pallas_call(body, name=name, grid=grid, in_specs=in_specs, out_specs=out_specs, out_shape=out_shape,
                              scratch_shapes=scratch_shapes, compiler_params=_cparams(semantics))(*args)
    n_in, n_out, n_scr, ci, co = len(in_specs), len(out_specs), len(scratch_shapes), len(comm.ins), len(comm.out_shapes)

    def wrapped(*refs):
        parts, a = [], 0
        for k in (n_in, ci, n_out, co, n_scr, 2):
            parts.append(refs[a:a + k])
            a += k
        ins, cins, outs, couts, scr, sems = parts
        ids = [pl.program_id(i) for i in range(len(grid))]
        first = functools.reduce(jnp.logical_and, [i == 0 for i in ids])
        last = functools.reduce(jnp.logical_and, [i == g - 1 for i, g in zip(ids, grid)])

        @pl.when(first)
        def _():
            comm.start(cins, couts, sems)

        body(*ins, *outs, *scr)

        @pl.when(last)
        def _():
            comm.finish(cins, couts, sems)

    res = pl.pallas_call(
        wrapped, name=name, grid=grid, in_specs=list(in_specs) + [_HBM] * ci, out_specs=list(out_specs) + [_HBM] * co,
        out_shape=list(out_shape) + comm.out_shapes, scratch_shapes=list(scratch_shapes) + comm.sem_shapes(),
        compiler_params=_cparams(("arbitrary",) * len(grid)))(*args, *comm.ins)
    return res[:n_out], res[n_out:]


ATT_WINDOW = 128
PHASES = 16
_NT = (((1,), (1,)), ((), ()))
_TN = (((0,), (0,)), ((), ()))


PERM_LANES = 512


def _phase_perm(bl):
    t = 16 * PHASES * bl
    col = jnp.arange(t)
    i, r, b = col // (PHASES * bl), (col // bl) % PHASES, col % bl
    return (jnp.arange(t)[:, None] == ((b * PHASES + r) * 16 + i)[None, :]).astype(BF16)


def _to_phase_order(x, bl, col0=0, width=None):
    n = x.shape[0]
    width = width or x.shape[1]
    t = 16 * PHASES * bl
    g = n // bl // PHASES
    tn = min(PERM_LANES, width)

    def body(p_ref, x_ref, o_ref):
        o_ref[...] = jnp.dot(p_ref[...], x_ref[...], preferred_element_type=F32).astype(o_ref.dtype).reshape(o_ref.shape)

    out = pl.pallas_call(
        body, name="to_phase", grid=(n // t, width // tn),
        in_specs=[pl.BlockSpec((t, t), lambda i, j: (0, 0)), pl.BlockSpec((t, tn), lambda i, j: (i, col0 // tn + j))],
        out_specs=pl.BlockSpec((bl * PHASES, 16, tn), lambda i, j: (0, i, j)),
        out_shape=jax.ShapeDtypeStruct((bl * PHASES, g, width), x.dtype),
        compiler_params=_cparams(("parallel", "parallel")))(_phase_perm(bl), x)
    return out.reshape(n, width)


def _from_phase_order(y, bl):
    n, width = y.shape
    t = 16 * PHASES * bl
    g = n // bl // PHASES
    tn = min(PERM_LANES, width)

    def body(p_ref, y_ref, o_ref):
        o_ref[...] = jnp.dot(p_ref[...], y_ref[...].reshape(t, tn), preferred_element_type=F32).astype(o_ref.dtype)

    return pl.pallas_call(
        body, name="from_phase", grid=(n // t, width // tn),
        in_specs=[pl.BlockSpec((t, t), lambda i, j: (0, 0)), pl.BlockSpec((bl * PHASES, 16, tn), lambda i, j: (0, i, j))],
        out_specs=pl.BlockSpec((t, tn), lambda i, j: (i, j)), out_shape=jax.ShapeDtypeStruct((n, width), y.dtype),
        compiler_params=_cparams(("parallel", "parallel")))(_phase_perm(bl).T, y.reshape(bl * PHASES, g, width))


def _att_geometry(p, n, bl):
    g = n // bl // PHASES
    if p == 0:
        return ((bl, PHASES, g), (bl, g // 16), (None, PHASES, 16),
                lambda sh: (lambda b, a: (b, 0, jnp.maximum(a + sh, 0))), 256, 16, lambda ids: ids[1] == 0)
    if p == 1:
        return ((bl, 4, 4, g), (bl, 2, g // 32), (None, 4, 2, 32),
                lambda sh: (lambda b, r, a: (b, 0, r, jnp.maximum(a + sh, 0))), 128, 32, lambda ids: ids[2] == 0)
    return ((bl * PHASES, g), (bl * PHASES // 2,), (2, g), lambda sh: (lambda s: (s, 0)), g, g, None)


def _att_units(p, n, bl):
    g = n // bl // PHASES
    full = slice(None)
    if p == 0:
        return [(full, full)], (PHASES, 16)
    if p == 1:
        return [(full, u, full) for u in range(2)], (4, 32)
    return [(u, full) for u in range(2)], (g,)


def _att_masks(p, qb, chunk):
    def pos(idx):
        return (idx % chunk) * (qb // chunk) + idx // chunk

    dq = pos(lax.broadcasted_iota(jnp.int32, (qb, qb), 0))
    dk = pos(lax.broadcasted_iota(jnp.int32, (qb, qb), 1))
    dist = dq - dk
    return jnp.logical_and(dist >= 0, dist <= ATT_WINDOW), dist + qb <= ATT_WINDOW


def _att_call(p, n, bl, c, body, name, ins, outs):
    prefix, grid, blk, idx_fn, qb, chunk, _ = _att_geometry(p, n, bl)

    def spec(cb, sh):
        f = idx_fn(sh)
        return pl.BlockSpec(blk + (c,), lambda *ids, f=f, cb=cb: f(*ids) + (cb,))

    in_specs = [spec(cb, sh) for _, cb, sh in ins]
    out_specs = [spec(0, 0) for _ in outs]
    out_shape = [jax.ShapeDtypeStruct(prefix + (c,), dt) for dt in outs]
    res = pl.pallas_call(body, name=name, grid=grid, in_specs=in_specs, out_specs=out_specs, out_shape=out_shape,
                         compiler_params=_cparams(("parallel",) * len(grid)))(*[a.reshape(prefix + (a.shape[1],)) for a, _, _ in ins])
    return [r.reshape(n, c) for r in res]


class _AttTiles:
    def __init__(self, p, n, bl, c, first):
        _, _, _, _, qb, chunk, _ = _att_geometry(p, n, bl)
        units, self.unit_shape = _att_units(p, n, bl)
        self.split = p == 0
        self.rows = qb // 2 if self.split else qb
        halves = (0, 1) if self.split else (None,)
        self.tiles = [(u, pl.ds(lt * 128, 128), h) for u in units for lt in range(c // 128) for h in halves]
        self.mask_cur, mp = _att_masks(p, self.rows, chunk // 2 if self.split else chunk)
        gated = mp if first is None else jnp.logical_and(mp, jnp.logical_not(first))
        self.mask_prev = [mp if h == 1 else gated for _, _, h in self.tiles]

    def _half(self, x, h):
        return x.astype(F32)[:, 8 * h:8 * h + 8, :].reshape(self.rows, 128).astype(x.dtype)

    def cur(self, ref, t):
        u, ls, h = self.tiles[t]
        x = ref[u + (ls,)]
        return x.reshape(self.rows, 128) if h is None else self._half(x, h)

    def prev(self, cur_ref, prev_ref, t):
        u, ls, h = self.tiles[t]
        if h is None:
            return prev_ref[u + (ls,)].reshape(self.rows, 128)
        return self._half(prev_ref[u + (ls,)], 1) if h == 0 else self._half(cur_ref[u + (ls,)], 0)

    def store(self, ref, vals):
        if not self.split:
            for (u, ls, _), v in zip(self.tiles, vals):
                ref[u + (ls,)] = v.astype(ref.dtype).reshape(self.unit_shape + (128,))
            return
        for k in range(len(self.tiles) // 2):
            u, ls, _ = self.tiles[2 * k]
            parts = [v.astype(F32).reshape(self.unit_shape[0], 8, 128) for v in vals[2 * k:2 * k + 2]]
            ref[u + (ls,)] = jnp.concatenate(parts, axis=1).astype(ref.dtype)

    def fold_keys(self, cur_vals, prev_vals):
        if not self.split:
            return cur_vals, prev_vals
        own, before = [], []
        for k in range(len(self.tiles) // 2):
            own += [cur_vals[2 * k] + prev_vals[2 * k + 1], cur_vals[2 * k + 1]]
            before += [jnp.zeros_like(prev_vals[2 * k]), prev_vals[2 * k]]
        return own, before


def _att_fwd(p, qkv, qcb, bl, c, heads):
    n = qkv.shape[0]
    _, grid, _, _, _, _, first_fn = _att_geometry(p, n, bl)
    n_grid = len(grid)
    has_prev = first_fn is not None
    e = c // heads
    scale = e ** -0.5

    def body(*refs):
        if has_prev:
            q_ref, kc_ref, kp_ref, vc_ref, vp_ref, o_ref, l_ref = refs
        else:
            q_ref, kc_ref, vc_ref, o_ref, l_ref = refs
            kp_ref = vp_ref = None
        tl = _AttTiles(p, n, bl, c, first_fn([pl.program_id(a) for a in range(n_grid)]) if has_prev else None)
        mc, n_t = tl.mask_cur, len(tl.tiles)
        lo = lax.broadcasted_iota(jnp.int32, (tl.rows, 128), 1) < e
        ones = jnp.ones((tl.rows, 128), BF16)
        items = [(t, h) for t in range(n_t) for h in range(2)]
        dot = functools.partial(jnp.dot, preferred_element_type=F32)
        q2 = [tl.cur(q_ref, t) for t in range(n_t)]
        kc = [tl.cur(kc_ref, t) for t in range(n_t)]
        qm = [jnp.where(lo if h == 0 else jnp.logical_not(lo), q2[t], jnp.zeros_like(q2[t])) for t, h in items]
        sc = [jnp.where(mc, lax.dot_general(qm[i], kc[t], _NT, preferred_element_type=F32) * scale, -jnp.inf)
              for i, (t, h) in enumerate(items)]
        m = [jnp.max(s, axis=1, keepdims=True) for s in sc]
        if has_prev:
            kp = [tl.prev(kc_ref, kp_ref, t) for t in range(n_t)]
            sp = [jnp.where(tl.mask_prev[t], lax.dot_general(qm[i], kp[t], _NT, preferred_element_type=F32) * scale, -jnp.inf)
                  for i, (t, h) in enumerate(items)]
            m = [jnp.maximum(a, jnp.max(s, axis=1, keepdims=True)) for a, s in zip(m, sp)]
        pc = [jnp.exp(s - a).astype(BF16) for s, a in zip(sc, m)]
        vc = [tl.cur(vc_ref, t) for t in range(n_t)]
        acc = [dot(pc[i], vc[t]) for i, (t, h) in enumerate(items)]
        den = [dot(x, ones) for x in pc]
        if has_prev:
            pp = [jnp.exp(s - a).astype(BF16) for s, a in zip(sp, m)]
            vp = [tl.prev(vc_ref, vp_ref, t) for t in range(n_t)]
            acc = [a + dot(pp[i], vp[t]) for i, ((t, h), a) in enumerate(zip(items, acc))]
            den = [d + dot(x, ones) for d, x in zip(den, pp)]
        oh = [a / d for a, d in zip(acc, den)]
        lh = [a + jnp.log(d) for a, d in zip(m, den)]
        tl.store(o_ref, [jnp.where(lo, oh[2 * t], oh[2 * t + 1]) for t in range(n_t)])
        tl.store(l_ref, [jnp.where(lo, lh[2 * t], lh[2 * t + 1]) for t in range(n_t)])

    kcb, vcb = 3, 4
    ins = [(qkv, qcb, 0), (qkv, kcb, 0)] + ([(qkv, kcb, -1)] if has_prev else []) + [(qkv, vcb, 0)] + ([(qkv, vcb, -1)] if has_prev else [])
    return _att_call(p, n, bl, c, body, name=f"att_fwd{p}", ins=ins, outs=[BF16, F32])


def _att_bwd(p, qkv, qcb, o, do, lse, bl, c, heads):
    n = qkv.shape[0]
    _, grid, _, _, _, _, first_fn = _att_geometry(p, n, bl)
    n_grid = len(grid)
    has_prev = first_fn is not None
    e = c // heads
    scale = e ** -0.5

    def body(*refs):
        if has_prev:
            q_ref, kc_ref, kp_ref, vc_ref, vp_ref, o_ref, do_ref, l_ref, dq_ref, dkc_ref, dkp_ref, dvc_ref, dvp_ref = refs
        else:
            q_ref, kc_ref, vc_ref, o_ref, do_ref, l_ref, dq_ref, dkc_ref, dvc_ref = refs
            kp_ref = vp_ref = None
        tl = _AttTiles(p, n, bl, c, first_fn([pl.program_id(a) for a in range(n_grid)]) if has_prev else None)
        mc, n_t = tl.mask_cur, len(tl.tiles)
        lo = lax.broadcasted_iota(jnp.int32, (tl.rows, 128), 1) < e
        items = [(t, h) for t in range(n_t) for h in range(2)]
        nt_dot = lambda a, b: lax.dot_general(a, b, _NT, preferred_element_type=F32)
        tn_dot = lambda a, b: lax.dot_general(a, b, _TN, preferred_element_type=F32)
        dot = functools.partial(jnp.dot, preferred_element_type=F32)
        sel = [lo if h == 0 else jnp.logical_not(lo) for t, h in items]
        q2, kc, vc, do2 = ([tl.cur(r, t) for t in range(n_t)] for r in (q_ref, kc_ref, vc_ref, do_ref))
        qm = [jnp.where(sel[i], q2[t], jnp.zeros_like(q2[t])) for i, (t, h) in enumerate(items)]
        dom = [jnp.where(sel[i], do2[t], jnp.zeros_like(do2[t])) for i, (t, h) in enumerate(items)]
        dod = [do2[t].astype(F32) * tl.cur(o_ref, t).astype(F32) for t in range(n_t)]
        lcol = [tl.cur(l_ref, t)[:, h * e:h * e + 1] for t, h in items]
        corr = [-jnp.sum(jnp.where(sel[i], dod[t], 0.0), axis=1, keepdims=True) for i, (t, h) in enumerate(items)]
        pc = [jnp.exp(jnp.where(mc, nt_dot(qm[i], kc[t]) * scale, -jnp.inf) - lcol[i]) for i, (t, h) in enumerate(items)]
        dsc = [(pc[i] * (nt_dot(dom[i], vc[t]) + corr[i]) * scale).astype(BF16) for i, (t, h) in enumerate(items)]
        pc = [x.astype(BF16) for x in pc]
        dq = [dot(dsc[i], kc[t]) for i, (t, h) in enumerate(items)]
        dkc = [tn_dot(dsc[2 * t], qm[2 * t]) + tn_dot(dsc[2 * t + 1], qm[2 * t + 1]) for t in range(n_t)]
        dvc = [tn_dot(pc[2 * t], dom[2 * t]) + tn_dot(pc[2 * t + 1], dom[2 * t + 1]) for t in range(n_t)]
        if has_prev:
            kp = [tl.prev(kc_ref, kp_ref, t) for t in range(n_t)]
            vp = [tl.prev(vc_ref, vp_ref, t) for t in range(n_t)]
            pp = [jnp.exp(jnp.where(tl.mask_prev[t], nt_dot(qm[i], kp[t]) * scale, -jnp.inf) - lcol[i]) for i, (t, h) in enumerate(items)]
            dsp = [(pp[i] * (nt_dot(dom[i], vp[t]) + corr[i]) * scale).astype(BF16) for i, (t, h) in enumerate(items)]
            pp = [x.astype(BF16) for x in pp]
            dq = [a + dot(dsp[i], kp[t]) for i, ((t, h), a) in enumerate(zip(items, dq))]
            dkp = [tn_dot(dsp[2 * t], qm[2 * t]) + tn_dot(dsp[2 * t + 1], qm[2 * t + 1]) for t in range(n_t)]
            dvp = [tn_dot(pp[2 * t], dom[2 * t]) + tn_dot(pp[2 * t + 1], dom[2 * t + 1]) for t in range(n_t)]
            (dkc, dkp), (dvc, dvp) = tl.fold_keys(dkc, dkp), tl.fold_keys(dvc, dvp)
            tl.store(dkp_ref, dkp)
            tl.store(dvp_ref, dvp)
        tl.store(dq_ref, [jnp.where(lo, dq[2 * t], dq[2 * t + 1]) for t in range(n_t)])
        tl.store(dkc_ref, dkc)
        tl.store(dvc_ref, dvc)

    kcb, vcb = 3, 4
    ins = [(qkv, qcb, 0), (qkv, kcb, 0)] + ([(qkv, kcb, -1)] if has_prev else []) + [(qkv, vcb, 0)] + ([(qkv, vcb, -1)] if has_prev else [])
    ins += [(o, 0, 0), (do, 0, 0), (lse, 0, 0)]
    res = _att_call(p, n, bl, c, body, name=f"att_bwd{p}", ins=ins, outs=[BF16] * (5 if has_prev else 3))
    if has_prev:
        dq, dkc, dkp, dvc, dvp = res
        return dq, dkc, dkp, dvc, dvp
    dq, dkc, dvc = res
    return dq, dkc, None, dvc, None


def _att_fold_prev(p, cur, prv, bl):
    if prv is None:
        return cur.astype(F32)
    n, c = cur.shape
    prefix, _, _, _, _, chunk, _ = _att_geometry(p, n, bl)
    v = prv.reshape(prefix + (c,)).astype(F32)
    shifted = jnp.concatenate([v[..., chunk:, :], jnp.zeros_like(v[..., :chunk, :])], axis=-2)
    return cur.astype(F32) + shifted.reshape(n, c)


def _attention_fwd(proj, c, bl, heads):
    n = proj.shape[0]
    qkv = _to_phase_order(proj, bl, col0=c, width=5 * c)
    outs = [_att_fwd(p, qkv, p, bl, c, heads) for p in range(3)]
    ins = [("row", o, c, 0) for o, _ in outs] + [("row", l, c, 0) for _, l in outs]
    o, lse = _rowwise(_combine_fwd_fn, "comb_fwd", n, ins, [(c, BF16), (c, F32)])
    return _from_phase_order(o, bl), (qkv, o, lse)


def _attention_bwd(do_tb, saved, bl, heads):
    qkv, o, lse = saved
    n, c = do_tb.shape
    do = _to_phase_order(do_tb, bl)
    dqs, dk, dv = [], 0.0, 0.0
    for p in range(3):
        dq, dkc, dkp, dvc, dvp = _att_bwd(p, qkv, p, o, do, lse, bl, c, heads)
        dqs.append(dq)
        dk = dk + _att_fold_prev(p, dkc, dkp, bl)
        dv = dv + _att_fold_prev(p, dvc, dvp, bl)
    dqkv = jnp.concatenate(dqs + [dk.astype(BF16), dv.astype(BF16)], axis=1)
    return _from_phase_order(dqkv, bl)


ATT_HEADS = 8
SSM_GROUPS, SSM_STATE, SSM_GROUP = 32, 64, 16


def _row(v):
    return v.reshape(1, -1)


def _carried(result, carry, key, hidden):
    if carry.get(key) is None:
        return result
    result, hidden[key] = result
    return result


def _layer_fwd(x, w, p, bl, carry, late=None):
    n, d = x.shape
    c = d // 2
    hidden = {}
    h, = _rowwise(_rms_fwd_fn, "rms_fwd", n, [("row", x, d, 0), ("par", _row(p["norm1_g"]))], [(d, BF16)])
    proj = _carried(_mm(h, w["w_in"], "nn", BF16, "mm_in", comm=carry.get("mm_in")), carry, "mm_in", hidden)
    if late is not None:
        w = dict(w, **late(hidden["mm_in"]))
    disc, disc_vjp = jax.vjp(_ssm_disc, p["ssm_lambda_re"], p["ssm_lambda_im"], p["ssm_log_dt"], p["ssm_b_re"], p["ssm_b_im"])
    bbd, cdm, a8 = _ssm_pack(*disc, p["ssm_c_re"], p["ssm_c_im"])
    ypre, yg, s_all = _carried(_ssm_fwd(proj, bbd, cdm, a8, _row(p["ssm_d"]), bl, "ssm_fwd", comm=carry.get("ssm_fwd")),
                               carry, "ssm_fwd", hidden)
    zs = _mm(yg, w["w_ssm_glu"], "nn", BF16, "mm_glu")
    o, att = _attention_fwd(proj, c, bl, ATT_HEADS)
    ya = _mm(o, w["w_att_up"], "nn", BF16, "mm_att")
    w32 = jnp.concatenate([p["conv_w"], jnp.zeros((1, c), F32)], axis=0)
    hc, hconv = _conv_fwd(proj, 6, w32, _row(p["conv_b"]), _row(p["conv_ln_g"]), _row(p["conv_ln_b"]), bl, c, "conv_fwd")
    yc = _mm(hc, w["w_conv_pw2"], "nn", BF16, "mm_pw2")
    gates = [("row", proj, d, 4), ("row", proj, d, 5), ("row", proj, d, 6), ("par", _row(p["b_gate"]))]
    branches = [("row", zs, 2 * d, 0), ("row", ya, d, 0), ("row", yc, d, 0)]
    merged, = _rowwise(_merge_fwd_fn, "merge_fwd", n, gates + branches, [(d, BF16)])
    xm = _mm(merged, w["w_out"], "nn", F32, "mm_out", res=x)
    h2, = _rowwise(_rms_fwd_fn, "rms_fwd", n, [("row", xm, d, 0), ("par", _row(p["norm2_g"]))], [(d, BF16)])
    z = _carried(_mm(h2, w["w_ffn_in"], "nn", BF16, "mm_ffn_in", comm=carry.get("mm_ffn_in")), carry, "mm_ffn_in", hidden)
    f = z.shape[1] // 2
    a, = _rowwise(_swiglu_fwd_fn, "swiglu_fwd", n, [("row", z, 2 * f, 0)], [(f, BF16)], tm=256)
    xo = _mm(a, w["w_ffn_out"], "nn", F32, "mm_ffn_out", res=xm)
    saved = dict(x=x, h=h, proj=proj, disc_vjp=disc_vjp, bbd=bbd, cdm=cdm, a8=a8, ypre=ypre, yg=yg, s_all=s_all, zs=zs, o=o,
                 att=att, ya=ya, w32=w32, hc=hc, hconv=hconv, yc=yc, gates=gates, branches=branches, merged=merged, xm=xm,
                 h2=h2, z=z, a=a)
    return xo, saved, hidden, w


def _layer_bwd(dxo, s, w, p, bl, carry):
    n, d = dxo.shape
    c = d // 2
    g, bufs, hidden = {}, {}, {}
    f = s["a"].shape[1]

    def dw(key, a, dy, name):
        bufs[key] = _mm_dw(a, dy, name, 1 if key in ROW_SHARDED else N_CHIPS)

    da = _mm(dxo, w["w_ffn_out"], "nt", BF16, "mm_ffn_out_dx")
    dw("w_ffn_out", s["a"], dxo, "mm_ffn_out_dw")
    dz, = _rowwise(_swiglu_bwd_fn, "swiglu_bwd", n, [("row", s["z"], 2 * f, 0), ("row", da, f, 0)], [(2 * f, BF16)], tm=256)
    dh2 = _mm(dz, w["w_ffn_in"], "nt", F32, "mm_ffn_in_dx")
    dw("w_ffn_in", s["h2"], dz, "mm_ffn_in_dw")
    dxm, dg2 = _rowwise(_rms_bwd_fn, "rms_bwd", n, [("row", s["xm"], d, 0), ("par", _row(p["norm2_g"])), ("row", dh2, d, 0),
                                                   ("row", dxo, d, 0)], [(d, F32)], [d])
    g["norm2_g"] = dg2[0]
    dmerged = _mm(dxm, w["w_out"], "nt", BF16, "mm_out_dx")
    dw("w_out", s["merged"], dxm, "mm_out_dw")
    dgl, dzs, dya, dyc, dbg = _rowwise(_merge_bwd_fn, "merge_bwd", n, s["gates"] + s["branches"] + [("row", dmerged, d, 0)],
                                       [(3 * d, BF16), (2 * d, BF16), (d, BF16), (d, BF16)], [3 * d], tm=256)
    g["b_gate"] = dbg[0]
    dyg = _mm(dzs, w["w_ssm_glu"], "nt", BF16, "mm_glu_dx")
    dw("w_ssm_glu", s["yg"], dzs, "mm_glu_dw")
    du, dbb, dcd, dab, dd = _carried(
        _ssm_bwd(dyg, s["ypre"], s["proj"], s["s_all"], s["cdm"].transpose(0, 2, 1), s["bbd"].transpose(0, 2, 1), s["a8"],
                 _row(p["ssm_d"]), bl, "ssm_bwd", comm=carry.get("ssm_bwd")), carry, "ssm_bwd", hidden)
    dab_re, dab_im, dbb_re, dbb_im, g["ssm_c_re"], g["ssm_c_im"] = _ssm_unpack(dbb, dcd, dab, SSM_GROUPS, SSM_STATE, SSM_GROUP)
    (g["ssm_lambda_re"], g["ssm_lambda_im"], g["ssm_log_dt"], g["ssm_b_re"],
     g["ssm_b_im"]) = s["disc_vjp"]((dab_re, dab_im, dbb_re, dbb_im))
    g["ssm_d"] = dd[0]
    do = _mm(dya, w["w_att_up"], "nt", BF16, "mm_att_dx")
    dw("w_att_up", s["o"], dya, "mm_att_dw")
    dqkv = _attention_bwd(do, s["att"], bl, ATT_HEADS)
    dhc = _mm(dyc, w["w_conv_pw2"], "nt", BF16, "mm_pw2_dx")
    dw("w_conv_pw2", s["hc"], dyc, "mm_pw2_dw")
    dcv, dcw, dcb, dlg, dlb = _carried(
        _conv_bwd(s["proj"], 6, dhc, s["hconv"], s["w32"], _row(p["conv_ln_g"]), _row(p["conv_ln_b"]), bl, c, "conv_bwd",
                  comm=carry.get("conv_bwd")), carry, "conv_bwd", hidden)
    g["conv_w"], g["conv_b"], g["conv_ln_g"], g["conv_ln_b"] = dcw, dcb[0], dlg[0], dlb[0]
    dproj = jnp.concatenate([du, dqkv, dcv, dgl], axis=1)
    dh = _mm(dproj, w["w_in"], "nt", F32, "mm_in_dx")
    dw("w_in", s["h"], dproj, "mm_in_dw")
    dx, dg1 = _rowwise(_rms_bwd_fn, "rms_bwd", n, [("row", s["x"], d, 0), ("par", _row(p["norm1_g"])), ("row", dh, d, 0),
                                                  ("row", dxm, d, 0)], [(d, F32)], [d])
    g["norm1_g"] = dg1[0]
    return dx, g, bufs, hidden


WEIGHTS = ['norm1_g', 'w_in', 'b_gate', 'ssm_lambda_re', 'ssm_lambda_im', 'ssm_log_dt', 'ssm_b_re', 'ssm_b_im', 'ssm_c_re',
           'ssm_c_im', 'ssm_d', 'w_ssm_glu', 'w_att_up', 'conv_w', 'conv_b', 'conv_ln_g', 'conv_ln_b', 'w_conv_pw2', 'w_out',
           'norm2_g', 'w_ffn_in', 'w_ffn_out', 'final_g']
BIG = ['w_in', 'w_ssm_glu', 'w_att_up', 'w_conv_pw2', 'w_out', 'w_ffn_in', 'w_ffn_out']
ROW_SHARDED = ('w_out', 'w_ffn_out')
SMALL = [k for k in WEIGHTS if k not in BIG]
LANES = 1024
N_CHIPS = 4
ROW_TILE_BYTES = 36 * 1024 * 1024
MIN_SHARD_TILE = 1024


def _pad_rows(a, rows):
    return jnp.concatenate([a, jnp.zeros((rows - a.shape[0],) + a.shape[1:], a.dtype)], axis=0) if rows > a.shape[0] else a


def _row_tile(rows, width, n_arrays):
    best = 16
    for t in range(16, rows + 1, 16):
        if rows % t == 0 and t * width * 4 * n_arrays * 2 <= ROW_TILE_BYTES:
            best = t
    return best


def _flat_fn(fn, name, ins, n_out, rows):
    return _rowwise(fn, name, rows, [("row", a, LANES, 0) for a in ins], [(LANES, F32)] * n_out, tm=rows)


def _reduce_prepare(bufs):
    landed = _run_exchange(_SwapHalves([b16 for _, b16 in bufs]), "rs_swap")
    p32s, p16s = [], []
    for (b32, _), la in zip(bufs, landed):
        s, m, cs = b32.shape
        h = m // 2
        tm = _row_tile(h, cs, 4)

        def body(g_ref, l_ref, o32, o16):
            r = g_ref[...] + l_ref[...].astype(F32)
            o32[...] = r
            o16[...] = r.astype(BF16)

        piece = pl.BlockSpec((None, tm, cs), lambda j, i: (j, i, 0))
        mine = pl.BlockSpec((None, None, tm, cs), lambda j, i: (j, _core_index(), i, 0))
        p32, p16 = pl.pallas_call(
            body, name="rs_add", grid=(s, h // tm), in_specs=[mine, piece], out_specs=[piece, piece],
            out_shape=[jax.ShapeDtypeStruct((s, h, cs), F32), jax.ShapeDtypeStruct((s, h, cs), BF16)],
            compiler_params=_cparams(("parallel", "parallel")))(b32.reshape(s, 2, h, cs), la)
        p32s.append(p32)
        p16s.append(p16)
    return p32s, p16s


def _reduce_finish(p32s, arrived):
    reduced = []
    for p32, lb in zip(p32s, arrived):
        _, h, cs = lb.shape
        tm = _row_tile(h, cs, 5)

        def body(p_ref, a_ref, b_ref, c_ref, o_ref):
            o_ref[...] = ((p_ref[...] + a_ref[...].astype(F32)) + b_ref[...].astype(F32)) + c_ref[...].astype(F32)

        mine = pl.BlockSpec((None, tm, cs), lambda i: (_chip_index(), i, 0))
        other = [pl.BlockSpec((None, tm, cs), lambda i, k=k: (k, i, 0)) for k in range(3)]
        half = pl.BlockSpec((None, tm, cs), lambda i: (_core_index(), i, 0))
        reduced.append(pl.pallas_call(
            body, name="rs_sum", grid=(h // tm,), in_specs=[mine] + other, out_specs=half,
            out_shape=jax.ShapeDtypeStruct((2, h, cs), F32), compiler_params=_cparams(("parallel",)))(p32, lb, lb, lb))
    joined = _run_exchange(_JoinHalves(reduced), "rs_gather")
    return [j.reshape(2 * j.shape[1], j.shape[2]) for j in joined]


def _adamw_layers(w, g_layers, m, v):
    depth, rows, cs = w.shape
    tm = _row_tile(rows, cs, 8)
    nb = rows // tm

    def body(*refs):
        w_ref, m_ref, v_ref = refs[:3]
        g_refs = refs[3:3 + depth]
        go_ref, d_ref, mo_ref, vo_ref = refs[3 + depth:]
        layer = pl.program_id(0)
        g = g_refs[0][...]
        for l in range(1, depth):
            g = jnp.where(layer == l, g_refs[l][...], g)
        delta, mo, vo = _adamw_fn(w_ref[...], g, m_ref[...], v_ref[...])
        go_ref[...], d_ref[...], mo_ref[...], vo_ref[...] = g, delta, mo, vo

    stacked = pl.BlockSpec((None, tm, cs), lambda l, i: (l, i, 0))
    g_specs = [pl.BlockSpec((tm, cs), lambda l, i, k=k: (jnp.where(l == k, i, jnp.where(l < k, 0, nb - 1)), 0)) for k in range(depth)]
    return pl.pallas_call(
        body, name="adamw", grid=(depth, nb), in_specs=[stacked] * 3 + g_specs, out_specs=[stacked] * 4,
        out_shape=[jax.ShapeDtypeStruct(w.shape, F32)] * 4, compiler_params=_cparams(("arbitrary", "arbitrary")))(w, m, v, *g_layers)


def _sum4_fn(a, b, c, d):
    return (((a.astype(F32) + b.astype(F32)) + c.astype(F32)) + d.astype(F32),)


def _add2_fn(a, b):
    return (a + b,)


def kernel(x, norm1_g, w_in, b_gate, ssm_lambda_re, ssm_lambda_im, ssm_log_dt, ssm_b_re, ssm_b_im, ssm_c_re, ssm_c_im, ssm_d, w_ssm_glu, w_att_up, conv_w, conv_b, conv_ln_g, conv_ln_b, w_conv_pw2, w_out, norm2_g, w_ffn_in, w_ffn_out, final_g, loss_target, m_norm1_g, m_w_in, m_b_gate, m_ssm_lambda_re, m_ssm_lambda_im, m_ssm_log_dt, m_ssm_b_re, m_ssm_b_im, m_ssm_c_re, m_ssm_c_im, m_ssm_d, m_w_ssm_glu, m_w_att_up, m_conv_w, m_conv_b, m_conv_ln_g, m_conv_ln_b, m_w_conv_pw2, m_w_out, m_norm2_g, m_w_ffn_in, m_w_ffn_out, m_final_g, v_norm1_g, v_w_in, v_b_gate, v_ssm_lambda_re, v_ssm_lambda_im, v_ssm_log_dt, v_ssm_b_re, v_ssm_b_im, v_ssm_c_re, v_ssm_c_im, v_ssm_d, v_w_ssm_glu, v_w_att_up, v_conv_w, v_conv_b, v_conv_ln_g, v_conv_ln_b, v_w_conv_pw2, v_w_out, v_norm2_g, v_w_ffn_in, v_w_ffn_out, v_final_g):
    args = dict(locals())
    wts = {k: args[k] for k in WEIGHTS}
    mom = {k: args["m_" + k] for k in WEIGHTS}
    var = {k: args["v_" + k] for k in WEIGHTS}
    bl, seq, d = x.shape
    n = bl * seq
    depth = norm1_g.shape[0]
    cx, cy, cc = _position()
    me = 2 * cx + cy

    assert depth == 2, "the exchanges of layer 1 are hidden behind layer 0's kernels"
    first = BIG[:1]
    rest = BIG[1:]

    shards = lambda keys, l: [wts[k][l].astype(BF16) for k in keys]

    def whole(keys, gathered):
        out = {}
        for k, a in zip(keys, gathered):
            _, ks, cs = a.shape
            if k in ROW_SHARDED:
                out[k] = a.reshape(N_CHIPS * ks, cs)
            elif cs < MIN_SHARD_TILE:
                out[k] = a.transpose(1, 0, 2).reshape(ks, N_CHIPS * cs)
            else:
                out[k] = a
        return out

    fill = lambda gathered, own: [_own_slot(g, o) for g, o in zip(gathered, own)]
    own0 = shards(first, 0) + [conv_w]
    gathered = fill(_run_exchange(_GatherShards(own0), "gather_weights"), own0)
    conv_full = gathered[-1].transpose(1, 2, 0, 3).reshape(depth, CONV_WIDTH, -1)
    params = lambda l: dict({k: wts[k][l] for k in SMALL if k not in ("final_g", "conv_w")}, conv_w=conv_full[l])

    to_rows = lambda t: t.transpose(1, 0, 2).reshape(n, d)
    own = {"mm_in": shards(rest, 0), "ssm_fwd": shards(first, 1), "mm_ffn_in": shards(rest, 1)}
    xs, s0, hidden, w0 = _layer_fwd(to_rows(x), whole(first, gathered[:-1]), params(0), bl, {k: _GatherShards(v) for k, v in own.items()},
                                    late=lambda got: whole(rest, fill(got, own["mm_in"])))
    w1 = dict(whole(first, fill(hidden["ssm_fwd"], own["ssm_fwd"])), **whole(rest, fill(hidden["mm_ffn_in"], own["mm_ffn_in"])))
    full = [w0, w1]
    xs, s1, _, _ = _layer_fwd(xs, full[1], params(1), bl, {})
    dx, sq, dgf = _rowwise(_loss_fn, "loss_head", n, [("row", xs, d, 0), ("par", _row(final_g)), ("row", to_rows(loss_target), d, 0)],
                           [(d, F32)], [d, d])
    loss = lax.psum(0.5 * jnp.sum(sq) / d, ("x", "y", "c"))

    pieces = lambda bufs, keys: [tuple(b.reshape(N_CHIPS, -1, b.shape[-1]) for b in bufs[k]) for k in keys]
    dx, g1, bufs1, _ = _layer_bwd(dx, s1, full[1], params(1), bl, {})
    p32_1, p16_1 = _reduce_prepare(pieces(bufs1, BIG))
    dx, g0, bufs0, hidden = _layer_bwd(dx, s0, full[0], params(0), bl,
                                       {"ssm_bwd": _ScatterPieces(p16_1[:1]), "conv_bwd": _ScatterPieces(p16_1[1:])})
    red1 = _reduce_finish(p32_1, list(hidden["ssm_bwd"]) + list(hidden["conv_bwd"]))
    p32_0, p16_0 = _reduce_prepare(pieces(bufs0, BIG))
    red0 = _reduce_finish(p32_0, _run_exchange(_ScatterPieces(p16_0), "rs_scatter"))
    grads = {"final_g": dgf[0]}
    for k in SMALL:
        if k != "final_g":
            grads[k] = jnp.stack([g0[k], g1[k]])
    grad_x = dx.reshape(seq, bl, d).transpose(1, 0, 2)
    outs = {}
    for k, r0, r1 in zip(BIG, red0, red1):
        for tag, a in zip(("grad", "delta", "m", "v"), _adamw_layers(wts[k], [r0, r1], mom[k], var[k])):
            outs[tag, k] = a

    def flat1(t):
        v = jnp.concatenate([t[k].reshape(-1) for k in SMALL])
        rows = -(-v.size // (8 * LANES)) * 8
        return _pad_rows(v, rows * LANES).reshape(rows, LANES), rows

    def unflat1(flat, shapes):
        out, off, v = {}, 0, flat.reshape(-1)
        for k in SMALL:
            size = math.prod(shapes[k])
            out[k] = v[off:off + size].reshape(shapes[k])
            off += size
        return out

    grads["conv_w"] = grads["conv_w"][:, :CONV_WIDTH]
    gs, rows = flat1(grads)
    chip_sum, = _flat_fn(_add2_fn, "ar_add", [gs, _swap_sibling(gs, "ar_swap")], 1, rows)
    slots = _chip_allgather(chip_sum, "ar_gather")
    gs_red, = _flat_fn(_sum4_fn, "ar_sum", [slots[j] for j in range(N_CHIPS)], 1, rows)
    g_sm = unflat1(gs_red, {k: grads[k].shape for k in SMALL})
    cs = conv_w.shape[2]
    g_sm["conv_w"] = lax.dynamic_slice_in_dim(g_sm["conv_w"], me * cs, cs, axis=2)
    (w1, rows), (g1, _), (m1, _), (v1, _) = flat1(wts), flat1(g_sm), flat1(mom), flat1(var)
    sm_out = _flat_fn(_adamw_fn, "adamw_small", [w1, g1, m1, v1], 3, rows)
    shapes = {k: wts[k].shape for k in SMALL}
    for tag, a in zip(("delta", "m", "v"), sm_out):
        for k, t in unflat1(a, shapes).items():
            outs[tag, k] = t
    for k in SMALL:
        outs["grad", k] = g_sm[k]
    return (loss, grad_x, *[outs["grad", k] for k in WEIGHTS], *[outs["delta", k] for k in WEIGHTS],
            *[outs["m", k] for k in WEIGHTS], *[outs["v", k] for k in WEIGHTS])
```

```python
import functools
import math

import jax
import jax.numpy as jnp
from jax import lax
from jax.experimental import pallas as pl
from jax.experimental.pallas import tpu as pltpu

F32 = jnp.float32
BF16 = jnp.bfloat16
VMEM_LIMIT = 56 * 1024 * 1024


def _cparams(sem):
    return pltpu.CompilerParams(dimension_semantics=sem, vmem_limit_bytes=VMEM_LIMIT)


_DIMS = {"nn": (((1,), (0,)), ((), ())), "nt": (((1,), (1,)), ((), ())), "tn": (((0,), (0,)), ((), ()))}


MM_ROWS = 1024
MM_DW_VMEM_BYTES = 44 * 1024 * 1024
MM_SMALL_STEP = 1024 * 1024 * 512


def _div_tile(n, cap):
    best = None
    for t in range(128, min(n, cap) + 1, 128):
        if n % t == 0:
            best = t
    return best or n


def _mm(a, b, form, out_dtype, name, res=None, comm=None):
    sharded = b.ndim == 3
    kdim, cs = b.shape[-2], b.shape[-1]
    s = b.shape[0] if sharded else 1
    m = a.shape[0]
    tm = MM_ROWS if m % MM_ROWS == 0 else _div_tile(m, MM_ROWS)
    if form == "nn":
        n, kd = s * cs, kdim
        tn, tk = _div_tile(cs, 1792), _div_tile(kdim, 2048)
        per = cs // tn
        b_blk = (tk, tn)
        b_idx = (lambda i, j, k: (j // per, k, j % per)) if sharded else (lambda i, j, k: (k, j))
    else:
        n, kd = kdim, s * cs
        tn, tk = _div_tile(kdim, 1408), _div_tile(cs, 1792)
        per = cs // tk
        b_blk = (tn, tk)
        b_idx = (lambda i, j, k: (k // per, j, k % per)) if sharded else (lambda i, j, k: (j, k))
    nk = kd // tk
    if tm * tn * tk <= MM_SMALL_STEP and m % (2 * tm) == 0:
        tm *= 2
    a_spec = pl.BlockSpec((tm, tk), lambda i, j, k: (i, k))
    b_spec = pl.BlockSpec(((None,) + b_blk) if sharded else b_blk, b_idx)
    o_spec = pl.BlockSpec((tm, tn), lambda i, j, k: (i, j))
    dims = _DIMS[form]

    def body(*refs):
        a_ref, b_ref = refs[:2]
        r_ref = refs[2] if res is not None else None
        o_ref = refs[3] if res is not None else refs[2]
        p = lax.dot_general(a_ref[...].astype(BF16), b_ref[...], dims, preferred_element_type=F32)

        def finish(r):
            if r_ref is not None:
                r = r + r_ref[...]
            o_ref[...] = r.astype(out_dtype)

        if nk == 1:
            finish(p)
            return
        acc = refs[-1]
        k = pl.program_id(2)

        @pl.when(k == 0)
        def _():
            acc[...] = p

        @pl.when(k > 0)
        def _():
            acc[...] += p

        @pl.when(k == nk - 1)
        def _():
            finish(acc[...])

    ins = [a, b] + ([] if res is None else [res])
    in_specs = [a_spec, b_spec] + ([] if res is None else [o_spec])
    out = _pcall(body, name, (m // tm, n // tn, nk), in_specs, [o_spec], [jax.ShapeDtypeStruct((m, n), out_dtype)],
                 [pltpu.VMEM((tm, tn), F32)] if nk > 1 else [], ("parallel", "parallel", "arbitrary"), ins, comm)
    return out[0] if comm is None else (out[0][0], out[1])


def _mm_dw(a, dy, name, shards):
    r, m = a.shape
    c = dy.shape[1]
    cs = c // shards
    tm, tn = _div_tile(m, 1408), _div_tile(cs, 1408)
    fixed = tm * tn * (4 + 2 * (4 + 2))
    per_row = 2 * (tm * a.dtype.itemsize + tn * dy.dtype.itemsize)
    tk = max(t for t in (256, 512, 1024, 2048) if r % t == 0 and (t == 256 or fixed + t * per_row <= MM_DW_VMEM_BYTES))
    per = cs // tn
    nk = r // tk

    def body(a_ref, b_ref, o32, o16, acc):
        k = pl.program_id(2)
        p = lax.dot_general(a_ref[...].astype(BF16), b_ref[...].astype(BF16), _DIMS["tn"], preferred_element_type=F32)

        @pl.when(k == 0)
        def _():
            acc[...] = p

        @pl.when(k > 0)
        def _():
            acc[...] += p

        @pl.when(k == nk - 1)
        def _():
            o32[...] = acc[...]
            o16[...] = acc[...].astype(BF16)

    o_spec = pl.BlockSpec((None, tm, tn), lambda i, j, k: (j // per, i, j % per))
    shape = (shards, m, cs)
    in_specs = [pl.BlockSpec((tk, tm), lambda i, j, k: (k, i)), pl.BlockSpec((tk, tn), lambda i, j, k: (k, j))]
    return _pcall(body, name, (m // tm, c // tn, nk), in_specs, [o_spec, o_spec],
                  [jax.ShapeDtypeStruct(shape, F32), jax.ShapeDtypeStruct(shape, BF16)], [pltpu.VMEM((tm, tn), F32)],
                  ("parallel", "parallel", "arbitrary"), [a, dy])


def _core_index():
    return lax.axis_index("c")


def _chip_index():
    return 2 * lax.axis_index("x") + lax.axis_index("y")


def _rowwise(fn, name, n_rows, ins, outs, accs=(), tm=512):
    n_in, n_out, n_acc = len(ins), len(outs), len(accs)
    in_specs, args = [], []
    for spec in ins:
        if spec[0] == "row":
            _, arr, w, cb = spec
            in_specs.append(pl.BlockSpec((tm, w), lambda i, cb=cb: (i, cb)))
        else:
            arr = spec[1]
            in_specs.append(pl.BlockSpec(arr.shape, lambda i: (0, 0)))
        args.append(arr)
    out_specs = [pl.BlockSpec((tm, w), lambda i: (i, 0)) for w, _ in outs]
    out_specs += [pl.BlockSpec((1, w), lambda i: (0, 0)) for w in accs]
    out_shape = [jax.ShapeDtypeStruct((n_rows, w), dt) for w, dt in outs]
    out_shape += [jax.ShapeDtypeStruct((1, w), F32) for w in accs]

    def body(*refs):
        i = pl.program_id(0)
        res = fn(*[r[...] for r in refs[:n_in]])
        for o_ref, r in zip(refs[n_in:n_in + n_out], res[:n_out]):
            o_ref[...] = r.astype(o_ref.dtype)
        for a_ref, r in zip(refs[n_in + n_out:], res[n_out:]):
            @pl.when(i == 0)
            def _(a_ref=a_ref, r=r):
                a_ref[...] = r

            @pl.when(i > 0)
            def _(a_ref=a_ref, r=r):
                a_ref[...] += r

    return pl.pallas_call(
        body, name=name, grid=(n_rows // tm,), in_specs=in_specs, out_specs=out_specs, out_shape=out_shape,
        compiler_params=_cparams(("arbitrary",)))(*args)


EPS = 1e-6


def _sig(x):
    return 1.0 / (1.0 + jnp.exp(-x))


def _colsum(x):
    return jnp.sum(x, axis=0, keepdims=True)


def _rms_fwd_fn(x, g):
    r = lax.rsqrt(jnp.mean(x * x, axis=-1, keepdims=True) + EPS)
    return (x * r * g,)


def _rms_bwd_fn(x, g, dh, dres):
    dh = dh.astype(F32)
    r = lax.rsqrt(jnp.mean(x * x, axis=-1, keepdims=True) + EPS)
    xh = x * r
    dyg = dh * g
    dx = r * (dyg - xh * jnp.mean(dyg * xh, axis=-1, keepdims=True)) + dres
    return dx, _colsum(dh * xh)


def _loss_fn(x, g, t):
    d = x.shape[-1]
    r = lax.rsqrt(jnp.mean(x * x, axis=-1, keepdims=True) + EPS)
    xh = x * r
    err = xh * g - t
    dy = err * (1.0 / d)
    dyg = dy * g
    dx = r * (dyg - xh * jnp.mean(dyg * xh, axis=-1, keepdims=True))
    return dx, _colsum(err * err), _colsum(dy * xh)


def _swiglu_fwd_fn(z):
    f = z.shape[-1] // 2
    z1, z2 = z[:, :f].astype(F32), z[:, f:].astype(F32)
    return (z1 * _sig(z1) * z2,)


def _swiglu_bwd_fn(z, da):
    f = z.shape[-1] // 2
    z1, z2, da = z[:, :f].astype(F32), z[:, f:].astype(F32), da.astype(F32)
    s = _sig(z1)
    dz1 = da * z2 * (s * (1.0 + z1 * (1.0 - s)))
    dz2 = da * (z1 * s)
    return (jnp.concatenate([dz1, dz2], axis=1),)


def _merge_fwd_fn(g0, g1, g2, bg, zs, ya, yc):
    d = ya.shape[-1]
    bg = bg.astype(F32)
    zs = zs.astype(F32)
    ys = zs[:, :d] * _sig(zs[:, d:])
    m = _sig(g0.astype(F32) + bg[:, :d]) * ys
    m = m + _sig(g1.astype(F32) + bg[:, d:2 * d]) * ya.astype(F32)
    m = m + _sig(g2.astype(F32) + bg[:, 2 * d:]) * yc.astype(F32)
    return (m,)


def _merge_bwd_fn(g0, g1, g2, bg, zs, ya, yc, dm):
    d = ya.shape[-1]
    bg = bg.astype(F32)
    zs = zs.astype(F32)
    dm = dm.astype(F32)
    z1, s2 = zs[:, :d], _sig(zs[:, d:])
    ys = z1 * s2
    s0 = _sig(g0.astype(F32) + bg[:, :d])
    s1 = _sig(g1.astype(F32) + bg[:, d:2 * d])
    s3 = _sig(g2.astype(F32) + bg[:, 2 * d:])
    dgl = jnp.concatenate([dm * ys * s0 * (1.0 - s0), dm * ya.astype(F32) * s1 * (1.0 - s1),
                           dm * yc.astype(F32) * s3 * (1.0 - s3)], axis=1)
    dys = dm * s0
    dzs = jnp.concatenate([dys * s2, dys * z1 * s2 * (1.0 - s2)], axis=1)
    return dgl, dzs, dm * s1, dm * s3, _colsum(dgl)


def _combine_fwd_fn(o0, o1, o2, l0, l1, l2):
    m = jnp.maximum(jnp.maximum(l0, l1), l2)
    e0, e1, e2 = jnp.exp(l0 - m), jnp.exp(l1 - m), jnp.exp(l2 - m)
    den = e0 + e1 + e2
    return (e0 * o0.astype(F32) + e1 * o1.astype(F32) + e2 * o2.astype(F32)) / den, m + jnp.log(den)


ADAM_LR, ADAM_B1, ADAM_B2, ADAM_EPS, ADAM_WD, ADAM_STEP = 0.001, 0.9, 0.999, 1e-08, 0.01, 10


def _adamw_fn(w, g, m, v):
    m = ADAM_B1 * m + (1.0 - ADAM_B1) * g
    v = ADAM_B2 * v + (1.0 - ADAM_B2) * (g * g)
    m_hat = m / (1.0 - ADAM_B1 ** ADAM_STEP)
    v_hat = v / (1.0 - ADAM_B2 ** ADAM_STEP)
    delta = -ADAM_LR * (m_hat / (jnp.sqrt(v_hat) + ADAM_EPS) + ADAM_WD * w)
    return delta, m, v


CONV_WIDTH = 31


def _conv_fwd(proj, cb, w32, conv_b, ln_g, ln_b, bl, c, name, tm=512):
    n = proj.shape[0]
    hp = (CONV_WIDTH - 1) * bl
    nt = n // tm

    def body(ap_ref, gp_ref, a_ref, g_ref, w_ref, cb_ref, lg_ref, lb_ref, hc_ref, hconv_ref, ext):
        i = pl.program_id(0)
        ext[pl.ds(hp, tm), :] = a_ref[...].astype(F32) * _sig(g_ref[...].astype(F32))
        hgp = ap_ref[pl.ds(tm - hp, hp), :].astype(F32) * _sig(gp_ref[pl.ds(tm - hp, hp), :].astype(F32))
        ext[pl.ds(0, hp), :] = jnp.where(i > 0, hgp, 0.0)
        acc = jnp.zeros((tm, c), F32) + cb_ref[...]
        for j in range(CONV_WIDTH):
            acc = acc + w_ref[j:j + 1, :] * ext[pl.ds(j * bl, tm), :]
        hconv_ref[...] = acc.astype(hconv_ref.dtype)
        h = hconv_ref[...].astype(F32)
        mu = jnp.mean(h, axis=-1, keepdims=True)
        xc = h - mu
        var = jnp.mean(xc * xc, axis=-1, keepdims=True)
        hn = xc * lax.rsqrt(var + EPS) * lg_ref[...] + lb_ref[...]
        hc_ref[...] = (hn * _sig(hn)).astype(hc_ref.dtype)

    prev = lambda i, k: (jnp.maximum(i - 1, 0), k)
    par = lambda arr: pl.BlockSpec(arr.shape, lambda i: (0, 0))
    return pl.pallas_call(
        body, name=name, grid=(nt,),
        in_specs=[pl.BlockSpec((tm, c), functools.partial(prev, k=cb)), pl.BlockSpec((tm, c), functools.partial(prev, k=cb + 1)),
                  pl.BlockSpec((tm, c), lambda i: (i, cb)), pl.BlockSpec((tm, c), lambda i: (i, cb + 1)),
                  par(w32), par(conv_b), par(ln_g), par(ln_b)],
        out_specs=[pl.BlockSpec((tm, c), lambda i: (i, 0))] * 2,
        out_shape=[jax.ShapeDtypeStruct((n, c), BF16)] * 2,
        scratch_shapes=[pltpu.VMEM((hp + tm, c), F32)],
        compiler_params=_cparams(("arbitrary",)))(proj, proj, proj, proj, w32, conv_b, ln_g, ln_b)


def _conv_bwd(proj, cb, dhc, hconv, w32, ln_g, ln_b, bl, c, name, tm=512, comm=None):
    n = proj.shape[0]
    hp = (CONV_WIDTH - 1) * bl
    nt = n // tm

    def ln_bwd(d, h, lg, lb):
        d, h = d.astype(F32), h.astype(F32)
        mu = jnp.mean(h, axis=-1, keepdims=True)
        xc = h - mu
        rstd = lax.rsqrt(jnp.mean(xc * xc, axis=-1, keepdims=True) + EPS)
        xh = xc * rstd
        hn = xh * lg + lb
        s = _sig(hn)
        dhn = d * (s * (1.0 + hn * (1.0 - s)))
        dxh = dhn * lg
        dh = rstd * (dxh - jnp.mean(dxh, axis=-1, keepdims=True) - xh * jnp.mean(dxh * xh, axis=-1, keepdims=True))
        return dh, dhn, xh

    def body(ap_ref, gp_ref, a_ref, g_ref, d_ref, dn_ref, h_ref, hn_ref, w_ref, lg_ref, lb_ref,
             dcv_ref, dw_ref, dcb_ref, dlg_ref, dlb_ref, ext_h, ext_d):
        i = pl.program_id(0)
        lg, lb = lg_ref[...], lb_ref[...]
        a, g = a_ref[...].astype(F32), g_ref[...].astype(F32)
        sg = _sig(g)
        ext_h[pl.ds(hp, tm), :] = a * sg
        hgp = ap_ref[pl.ds(tm - hp, hp), :].astype(F32) * _sig(gp_ref[pl.ds(tm - hp, hp), :].astype(F32))
        ext_h[pl.ds(0, hp), :] = jnp.where(i > 0, hgp, 0.0)
        dh, dhn, xh = ln_bwd(d_ref[...], h_ref[...], lg, lb)
        ext_d[pl.ds(0, tm), :] = dh
        dh_n, _, _ = ln_bwd(dn_ref[pl.ds(0, hp), :], hn_ref[pl.ds(0, hp), :], lg, lb)
        ext_d[pl.ds(tm, hp), :] = jnp.where(i < nt - 1, dh_n, 0.0)

        @pl.when(i == 0)
        def _():
            dw_ref[...] = jnp.zeros_like(dw_ref)
            dcb_ref[...] = jnp.zeros_like(dcb_ref)
            dlg_ref[...] = jnp.zeros_like(dlg_ref)
            dlb_ref[...] = jnp.zeros_like(dlb_ref)

        dcb_ref[...] += _colsum(dh)
        dlg_ref[...] += _colsum(dhn * xh)
        dlb_ref[...] += _colsum(dhn)
        dhg = jnp.zeros((tm, c), F32)
        for j in range(CONV_WIDTH):
            dhg = dhg + w_ref[j:j + 1, :] * ext_d[pl.ds((CONV_WIDTH - 1 - j) * bl, tm), :]
            dw_ref[j:j + 1, :] += _colsum(dh * ext_h[pl.ds(j * bl, tm), :])
        dcv_ref[...] = jnp.concatenate([dhg * sg, dhg * a * sg * (1.0 - sg)], axis=1).astype(dcv_ref.dtype)

    prev = lambda i, k: (jnp.maximum(i - 1, 0), k)
    nxt = lambda i: (jnp.minimum(i + 1, nt - 1), 0)
    cur = lambda i: (i, 0)
    par = lambda arr: pl.BlockSpec(arr.shape, lambda i: (0, 0))
    acc = lambda r: pl.BlockSpec((r, c), lambda i: (0, 0))
    in_specs = [pl.BlockSpec((tm, c), functools.partial(prev, k=cb)), pl.BlockSpec((tm, c), functools.partial(prev, k=cb + 1)),
                pl.BlockSpec((tm, c), lambda i: (i, cb)), pl.BlockSpec((tm, c), lambda i: (i, cb + 1)),
                pl.BlockSpec((tm, c), cur), pl.BlockSpec((tm, c), nxt), pl.BlockSpec((tm, c), cur), pl.BlockSpec((tm, c), nxt),
                par(w32), par(ln_g), par(ln_b)]
    out_shape = [jax.ShapeDtypeStruct((n, 2 * c), BF16), jax.ShapeDtypeStruct((32, c), F32)] + [jax.ShapeDtypeStruct((1, c), F32)] * 3
    return _pcall(body, name, (nt,), in_specs, [pl.BlockSpec((tm, 2 * c), cur), acc(32), acc(1), acc(1), acc(1)], out_shape,
                  [pltpu.VMEM((hp + tm, c), F32), pltpu.VMEM((hp + tm, c), F32)], ("arbitrary",),
                  [proj, proj, proj, proj, dhc, dhc, hconv, hconv, w32, ln_g, ln_b], comm)


SSM_CH = 128
_GELU_C = 0.7978845608028654


def _gelu(x):
    return 0.5 * x * (1.0 + jnp.tanh(_GELU_C * (x + 0.044715 * x * x * x)))


def _gelu_grad(x):
    th = jnp.tanh(_GELU_C * (x + 0.044715 * x * x * x))
    return 0.5 * (1.0 + th) + 0.5 * x * (1.0 - th * th) * (_GELU_C * (1.0 + 3.0 * 0.044715 * x * x))


def _ssm_disc(lam_re, lam_im, log_dt, b_re, b_im):
    dt = jnp.exp(log_dt)[:, None]
    mag = jnp.exp(lam_re * dt)
    ab_re = mag * jnp.cos(lam_im * dt)
    ab_im = mag * jnp.sin(lam_im * dt)
    nr, ni = ab_re - 1.0, ab_im
    den = lam_re * lam_re + lam_im * lam_im
    z_re = ((nr * lam_re + ni * lam_im) / den)[..., None]
    z_im = ((ni * lam_re - nr * lam_im) / den)[..., None]
    return ab_re, ab_im, z_re * b_re - z_im * b_im, z_re * b_im + z_im * b_re


def _ssm_pack(ab_re, ab_im, bb_re, bb_im, c_re, c_im):
    g, p, h = bb_re.shape
    gc = SSM_CH // h
    nc = g // gc
    eye = jnp.eye(gc, dtype=F32)
    blk = lambda x: jnp.einsum("qgph,gk->qghkp", x.reshape(nc, gc, p, h), eye).reshape(nc, gc * h, gc * p)
    bbd = jnp.concatenate([blk(bb_re), blk(bb_im)], axis=2).astype(BF16)
    blc = lambda x: jnp.einsum("qghp,gk->qgpkh", x.reshape(nc, gc, h, p), eye).reshape(nc, gc * p, gc * h)
    cdm = jnp.concatenate([blc(c_re), blc(-c_im)], axis=1).astype(BF16)
    a = jnp.concatenate([ab_re.reshape(nc, gc * p), ab_im.reshape(nc, gc * p)], axis=1)
    a8 = jnp.broadcast_to(a[:, None, :], (nc, 8, 2 * gc * p)).reshape(nc * 8, 2 * gc * p)
    return bbd, cdm, a8


def _ssm_unpack(dbb, dcd, da, g, p, h):
    gc = SSM_CH // h
    nc = g // gc
    ph = gc * p
    eye = jnp.eye(gc, dtype=F32)
    dia = lambda x, o: jnp.einsum("qgpkh,gk->" + o, x.reshape(nc, gc, p, gc, h), eye).reshape((g, p, h) if o == "qgph" else (g, h, p))
    das = da.reshape(nc, 8, 2 * ph).sum(axis=1)
    return (das[:, :ph].reshape(g, p), das[:, ph:].reshape(g, p), dia(dbb[:, :ph], "qgph"), dia(dbb[:, ph:], "qgph"),
            dia(dcd[:, :ph], "qghp"), -dia(dcd[:, ph:], "qghp"))


def _ssm_fwd(proj, bbd, cdm, a8, dskip, bl, name, tm=1024, comm=None):
    n = proj.shape[0]
    nc, ch, p2 = bbd.shape
    ph = p2 // 2
    nt = n // tm
    nsub = 8 // bl

    def body(u_ref, bb_ref, cd_ref, a_ref, d_ref, ypre_ref, yg_ref, s_ref, bu, carry):
        t = pl.program_id(1)

        @pl.when(t == 0)
        def _():
            carry[...] = jnp.zeros_like(carry)

        u = u_ref[...]
        bu[...] = jnp.dot(u, bb_ref[0], preferred_element_type=F32)
        a_re, a_im = a_ref[:, :ph], a_ref[:, ph:]
        row = lax.broadcasted_iota(jnp.int32, (8, ph), 0)

        def step(k, c):
            cre, cim = c
            r0 = pl.multiple_of(k * 8, 8)
            bre, bim = bu[pl.ds(r0, 8), :ph], bu[pl.ds(r0, 8), ph:]
            sre, sim = cre, cim
            for sub in range(nsub):
                xre, xim = pltpu.roll(cre, bl, 0), pltpu.roll(cim, bl, 0)
                cre = a_re * xre - a_im * xim + bre
                cim = a_re * xim + a_im * xre + bim
                if sub == 0:
                    sre, sim = cre, cim
                else:
                    sel = row >= sub * bl
                    sre, sim = jnp.where(sel, cre, sre), jnp.where(sel, cim, sim)
            bu[pl.ds(r0, 8), :ph] = sre
            bu[pl.ds(r0, 8), ph:] = sim
            return sre, sim

        cre, cim = lax.fori_loop(0, tm // 8, step, (carry[:, :ph], carry[:, ph:]))
        carry[:, :ph] = cre
        carry[:, ph:] = cim
        s16 = bu[...].astype(BF16)
        s_ref[...] = s16
        y = jnp.dot(s16, cd_ref[0], preferred_element_type=F32) + d_ref[...] * u.astype(F32)
        ypre_ref[...] = y
        yg_ref[...] = _gelu(y).astype(yg_ref.dtype)

    in_specs = [pl.BlockSpec((tm, ch), lambda q, t: (t, q)), pl.BlockSpec((1, ch, p2), lambda q, t: (q, 0, 0)),
                pl.BlockSpec((1, p2, ch), lambda q, t: (q, 0, 0)), pl.BlockSpec((8, p2), lambda q, t: (q, 0)),
                pl.BlockSpec((1, ch), lambda q, t: (0, q))]
    out_specs = [pl.BlockSpec((tm, ch), lambda q, t: (t, q)), pl.BlockSpec((tm, ch), lambda q, t: (t, q)),
                 pl.BlockSpec((tm, p2), lambda q, t: (t, q))]
    out_shape = [jax.ShapeDtypeStruct((n, nc * ch), F32), jax.ShapeDtypeStruct((n, nc * ch), BF16),
                 jax.ShapeDtypeStruct((n, nc * p2), BF16)]
    return _pcall(body, name, (nc, nt), in_specs, out_specs, out_shape, [pltpu.VMEM((tm, p2), F32), pltpu.VMEM((8, p2), F32)],
                  ("parallel", "arbitrary"), [proj, bbd, cdm, a8, dskip], comm)


def _ssm_bwd(dyg, ypre, proj, s_all, cdt, bbt, a8, dskip, bl, name, tm=1024, comm=None):
    n = proj.shape[0]
    nc, ch, p2 = cdt.shape
    ph = p2 // 2
    nt = n // tm
    nsub = 8 // bl
    tn_dims = (((0,), (0,)), ((), ()))

    def body(dyg_ref, ypre_ref, u_ref, s_ref, cdt_ref, bbt_ref, a_ref, d_ref,
             du_ref, dbb_ref, dcd_ref, da_ref, dd_ref, ds, s32, carry):
        t = pl.program_id(1)

        @pl.when(t == 0)
        def _():
            carry[...] = jnp.zeros_like(carry)
            dbb_ref[...] = jnp.zeros_like(dbb_ref)
            dcd_ref[...] = jnp.zeros_like(dcd_ref)
            da_ref[...] = jnp.zeros_like(da_ref)
            dd_ref[...] = jnp.zeros_like(dd_ref)

        dyp = dyg_ref[...].astype(F32) * _gelu_grad(ypre_ref[...])
        u = u_ref[...]
        dd_ref[...] += _colsum(dyp * u.astype(F32))
        dyp16 = dyp.astype(BF16)
        ds[...] = jnp.dot(dyp16, cdt_ref[0], preferred_element_type=F32)
        s16 = s_ref[...]
        s32[...] = s16.astype(F32)
        a_re, a_im = a_ref[:, :ph], a_ref[:, ph:]
        row = lax.broadcasted_iota(jnp.int32, (8, ph), 0)
        back = 8 - bl

        def step(kk, c):
            lre, lim, acr, aci = c
            r0 = pl.multiple_of((tm // 8 - 1 - kk) * 8, 8)
            dre, dim = ds[pl.ds(r0, 8), :ph], ds[pl.ds(r0, 8), ph:]
            sre, sim = s32[pl.ds(r0, 8), :ph], s32[pl.ds(r0, 8), ph:]
            ore, oim, ire, iim = lre, lim, lre, lim
            for sub in range(nsub - 1, -1, -1):
                xre, xim = pltpu.roll(lre, back, 0), pltpu.roll(lim, back, 0)
                lre = a_re * xre + a_im * xim + dre
                lim = a_re * xim - a_im * xre + dim
                if sub == nsub - 1:
                    ore, oim, ire, iim = lre, lim, xre, xim
                else:
                    sel = row < (sub + 1) * bl
                    ore, oim = jnp.where(sel, lre, ore), jnp.where(sel, lim, oim)
                    ire, iim = jnp.where(sel, xre, ire), jnp.where(sel, xim, iim)
            ds[pl.ds(r0, 8), :ph] = ore
            ds[pl.ds(r0, 8), ph:] = oim
            acr = acr + sre * ire + sim * iim
            aci = aci + sre * iim - sim * ire
            return ore, oim, acr, aci

        z = jnp.zeros((8, ph), F32)
        lre, lim, acr, aci = lax.fori_loop(0, tm // 8, step, (carry[:, :ph], carry[:, ph:], z, z))
        carry[:, :ph] = lre
        carry[:, ph:] = lim
        da_ref[:, :ph] += acr
        da_ref[:, ph:] += aci
        lam16 = ds[...].astype(BF16)
        du = jnp.dot(lam16, bbt_ref[0], preferred_element_type=F32) + d_ref[...] * dyp
        du_ref[...] = du.astype(du_ref.dtype)
        dbb_ref[0] += lax.dot_general(lam16, u, tn_dims, preferred_element_type=F32)
        dcd_ref[0] += lax.dot_general(s16, dyp16, tn_dims, preferred_element_type=F32)

    rev = lambda q, t: (nt - 1 - t, q)
    in_specs = [pl.BlockSpec((tm, ch), rev), pl.BlockSpec((tm, ch), rev), pl.BlockSpec((tm, ch), rev),
                pl.BlockSpec((tm, p2), rev), pl.BlockSpec((1, ch, p2), lambda q, t: (q, 0, 0)),
                pl.BlockSpec((1, p2, ch), lambda q, t: (q, 0, 0)), pl.BlockSpec((8, p2), lambda q, t: (q, 0)),
                pl.BlockSpec((1, ch), lambda q, t: (0, q))]
    out_specs = [pl.BlockSpec((tm, ch), rev), pl.BlockSpec((1, p2, ch), lambda q, t: (q, 0, 0)),
                 pl.BlockSpec((1, p2, ch), lambda q, t: (q, 0, 0)), pl.BlockSpec((8, p2), lambda q, t: (q, 0)),
                 pl.BlockSpec((1, ch), lambda q, t: (0, q))]
    out_shape = [jax.ShapeDtypeStruct((n, nc * ch), BF16), jax.ShapeDtypeStruct((nc, p2, ch), F32),
                 jax.ShapeDtypeStruct((nc, p2, ch), F32), jax.ShapeDtypeStruct((nc * 8, p2), F32),
                 jax.ShapeDtypeStruct((1, nc * ch), F32)]
    return _pcall(body, name, (nc, nt), in_specs, out_specs, out_shape,
                  [pltpu.VMEM((tm, p2), F32), pltpu.VMEM((tm, p2), F32), pltpu.VMEM((8, p2), F32)],
                  ("parallel", "arbitrary"), [dyg, ypre, proj, s_all, cdt, bbt, a8, dskip], comm)


_MESH = pl.DeviceIdType.MESH
_HBM = pl.BlockSpec(memory_space=pltpu.HBM)


def _position():
    return lax.axis_index("x"), lax.axis_index("y"), lax.axis_index("c")


def _other_chips(x, y):
    return [((1 - x, y), 2 * (1 - x) + y), ((x, 1 - y), 2 * x + 1 - y), ((1 - x, 1 - y), 2 * (1 - x) + 1 - y)]


def _swap_sibling(v, name):
    def body(v_ref, got_ref, send_sem, recv_sem):
        x, y, c = _position()
        cp = pltpu.make_async_remote_copy(src_ref=v_ref, dst_ref=got_ref, send_sem=send_sem, recv_sem=recv_sem,
                                          device_id=(x, y, 1 - c), device_id_type=_MESH)
        cp.start()
        cp.wait()

    return pl.pallas_call(
        body, name=name, in_specs=[_HBM], out_specs=_HBM, out_shape=jax.ShapeDtypeStruct(v.shape, v.dtype),
        scratch_shapes=[pltpu.SemaphoreType.DMA, pltpu.SemaphoreType.DMA])(v)


def _own_slot(gathered, own):
    return lax.dynamic_update_index_in_dim(gathered, own, _chip_index(), 0)


def _chip_allgather(v, name):
    def body(v_ref, out_ref, send_sems, recv_sems):
        x, y, c = _position()
        me = 2 * x + y
        sends = []
        for k, (chip, idx) in enumerate(_other_chips(x, y)):
            cp = pltpu.make_async_remote_copy(src_ref=v_ref, dst_ref=out_ref.at[me], send_sem=send_sems.at[k],
                                              recv_sem=recv_sems.at[k], device_id=(*chip, c), device_id_type=_MESH)
            cp.start()
            sends.append(cp)
        for k, (chip, idx) in enumerate(_other_chips(x, y)):
            pltpu.make_async_remote_copy(src_ref=v_ref, dst_ref=out_ref.at[idx], send_sem=send_sems.at[k],
                                         recv_sem=recv_sems.at[k], device_id=(*chip, c), device_id_type=_MESH).wait_recv()
        for cp in sends:
            cp.wait_send()

    out = pl.pallas_call(
        body, name=name, in_specs=[_HBM], out_specs=_HBM, out_shape=jax.ShapeDtypeStruct((4,) + tuple(v.shape), v.dtype),
        scratch_shapes=[pltpu.SemaphoreType.DMA((3,)), pltpu.SemaphoreType.DMA((3,))])(v)
    return _own_slot(out, v)


def _remote(src, dst, send_sems, recv_sems, s, device):
    return pltpu.make_async_remote_copy(src_ref=src, dst_ref=dst, send_sem=send_sems.at[s], recv_sem=recv_sems.at[s],
                                        device_id=device, device_id_type=_MESH)


class _Exchange:
    def __init__(self, ins, out_shapes, n_sems, aliases=None):
        self.ins, self.out_shapes, self.n_sems, self.aliases = list(ins), list(out_shapes), n_sems, aliases or {}

    def sem_shapes(self):
        return [pltpu.SemaphoreType.DMA((self.n_sems,)), pltpu.SemaphoreType.DMA((self.n_sems,))]


def _halves(ref, c, axis=0):
    h = ref.shape[axis] // 2
    idx = (slice(None),) * axis
    return ref.at[idx + (pl.ds(c * h, h),)], ref.at[idx + (pl.ds((1 - c) * h, h),)]


class _GatherShards(_Exchange):
    def __init__(self, ws):
        super().__init__(ws, [jax.ShapeDtypeStruct((N_CHIPS,) + tuple(w.shape), w.dtype) for w in ws], 6 * len(ws))

    def start(self, w_refs, out_refs, sems):
        send_sems, recv_sems = sems
        x, y, c = _position()
        me = 2 * x + y
        for i, (w, out) in enumerate(zip(w_refs, out_refs)):
            for k, (chip, idx) in enumerate(_other_chips(x, y)):
                _remote(_halves(w, c)[0], _halves(out.at[me], c)[0], send_sems, recv_sems, 6 * i + k, (*chip, c)).start()

    def finish(self, w_refs, out_refs, sems):
        send_sems, recv_sems = sems
        x, y, c = _position()
        sibling = (x, y, 1 - c)
        others = _other_chips(x, y)
        for i, out in enumerate(out_refs):
            for k, (chip, idx) in enumerate(others):
                landed = _halves(out.at[idx], c)[0]
                _remote(landed, landed, send_sems, recv_sems, 6 * i + k, (*chip, c)).wait_recv()
                _remote(landed, landed, send_sems, recv_sems, 6 * i + 3 + k, sibling).start()
        for i, (w, out) in enumerate(zip(w_refs, out_refs)):
            for k, (chip, idx) in enumerate(others):
                mine, theirs = _halves(out.at[idx], c)
                _remote(theirs, theirs, send_sems, recv_sems, 6 * i + 3 + k, sibling).wait_recv()
                _remote(mine, mine, send_sems, recv_sems, 6 * i + 3 + k, sibling).wait_send()
                _remote(_halves(w, c)[0], mine, send_sems, recv_sems, 6 * i + k, (*chip, c)).wait_send()


class _SwapHalves(_Exchange):
    def __init__(self, gs):
        shapes = [jax.ShapeDtypeStruct((g.shape[0], g.shape[1] // 2) + tuple(g.shape[2:]), g.dtype) for g in gs]
        super().__init__(gs, shapes, N_CHIPS * len(gs))

    def _copies(self, g_refs, out_refs, sems):
        x, y, c = _position()
        return [_remote(_halves(g.at[j], c)[1], out.at[j], sems[0], sems[1], N_CHIPS * i + j, (x, y, 1 - c))
                for i, (g, out) in enumerate(zip(g_refs, out_refs)) for j in range(N_CHIPS)]

    def start(self, g_refs, out_refs, sems):
        for cp in self._copies(g_refs, out_refs, sems):
            cp.start()

    def finish(self, g_refs, out_refs, sems):
        for cp in self._copies(g_refs, out_refs, sems):
            cp.wait()


class _ScatterPieces(_Exchange):
    def __init__(self, ps):
        super().__init__(ps, [jax.ShapeDtypeStruct((3,) + tuple(p.shape[1:]), p.dtype) for p in ps], 3 * len(ps))

    def _copies(self, p_refs, out_refs, sems):
        x, y, c = _position()
        return [_remote(p.at[idx], out.at[k], sems[0], sems[1], 3 * i + k, (*chip, c))
                for i, (p, out) in enumerate(zip(p_refs, out_refs)) for k, (chip, idx) in enumerate(_other_chips(x, y))]

    def start(self, p_refs, out_refs, sems):
        for cp in self._copies(p_refs, out_refs, sems):
            cp.start()

    def finish(self, p_refs, out_refs, sems):
        for cp in self._copies(p_refs, out_refs, sems):
            cp.wait()


class _JoinHalves(_Exchange):
    def __init__(self, rs):
        super().__init__(rs, [jax.ShapeDtypeStruct(r.shape, r.dtype) for r in rs], len(rs), {i: i for i in range(len(rs))})

    def start(self, r_refs, out_refs, sems):
        x, y, c = _position()
        for i, out in enumerate(out_refs):
            _remote(out.at[c], out.at[c], sems[0], sems[1], i, (x, y, 1 - c)).start()

    def finish(self, r_refs, out_refs, sems):
        x, y, c = _position()
        for i, out in enumerate(out_refs):
            _remote(out.at[c], out.at[c], sems[0], sems[1], i, (x, y, 1 - c)).wait_send()
            _remote(out.at[1 - c], out.at[1 - c], sems[0], sems[1], i, (x, y, 1 - c)).wait_recv()


def _run_exchange(ex, name):
    def body(*refs):
        ins, outs, sems = refs[:len(ex.ins)], refs[len(ex.ins):len(ex.ins) + len(ex.out_shapes)], refs[-2:]
        ex.start(ins, outs, sems)
        ex.finish(ins, outs, sems)

    return pl.pallas_call(body, name=name, in_specs=[_HBM] * len(ex.ins), out_specs=[_HBM] * len(ex.out_shapes),
                          out_shape=ex.out_shapes, scratch_shapes=ex.sem_shapes(), input_output_aliases=ex.aliases)(*ex.ins)


def _pcall(body, name, grid, in_specs, out_specs, out_shape, scratch_shapes, semantics, args, comm=None):
    if comm is None:
        return pl.pallas_call(body, name=name, grid=grid, in_specs=in_specs, out_specs=out_specs, out_shape=out_shape,
                              scratch_shapes=scratch_shapes, compiler_params=_cparams(semantics))(*args)
    n_in, n_out, n_scr, ci, co = len(in_specs), len(out_specs), len(scratch_shapes), len(comm.ins), len(comm.out_shapes)

    def wrapped(*refs):
        parts, a = [], 0
        for k in (n_in, ci, n_out, co, n_scr, 2):
            parts.append(refs[a:a + k])
            a += k
        ins, cins, outs, couts, scr, sems = parts
        ids = [pl.program_id(i) for i in range(len(grid))]
        first = functools.reduce(jnp.logical_and, [i == 0 for i in ids])
        last = functools.reduce(jnp.logical_and, [i == g - 1 for i, g in zip(ids, grid)])

        @pl.when(first)
        def _():
            comm.start(cins, couts, sems)

        body(*ins, *outs, *scr)

        @pl.when(last)
        def _():
            comm.finish(cins, couts, sems)

    res = pl.pallas_call(
        wrapped, name=name, grid=grid, in_specs=list(in_specs) + [_HBM] * ci, out_specs=list(out_specs) + [_HBM] * co,
        out_shape=list(out_shape) + comm.out_shapes, scratch_shapes=list(scratch_shapes) + comm.sem_shapes(),
        compiler_params=_cparams(("arbitrary",) * len(grid)))(*args, *comm.ins)
    return res[:n_out], res[n_out:]


ATT_WINDOW = 128
PHASES = 16
_NT = (((1,), (1,)), ((), ()))
_TN = (((0,), (0,)), ((), ()))


PERM_LANES = 512


def _phase_perm(bl):
    t = 16 * PHASES * bl
    col = jnp.arange(t)
    i, r, b = col // (PHASES * bl), (col // bl) % PHASES, col % bl
    return (jnp.arange(t)[:, None] == ((b * PHASES + r) * 16 + i)[None, :]).astype(BF16)


def _to_phase_order(x, bl, col0=0, width=None):
    n = x.shape[0]
    width = width or x.shape[1]
    t = 16 * PHASES * bl
    g = n // bl // PHASES
    tn = min(PERM_LANES, width)

    def body(p_ref, x_ref, o_ref):
        o_ref[...] = jnp.dot(p_ref[...], x_ref[...], preferred_element_type=F32).astype(o_ref.dtype).reshape(o_ref.shape)

    out = pl.pallas_call(
        body, name="to_phase", grid=(n // t, width // tn),
        in_specs=[pl.BlockSpec((t, t), lambda i, j: (0, 0)), pl.BlockSpec((t, tn), lambda i, j: (i, col0 // tn + j))],
        out_specs=pl.BlockSpec((bl * PHASES, 16, tn), lambda i, j: (0, i, j)),
        out_shape=jax.ShapeDtypeStruct((bl * PHASES, g, width), x.dtype),
        compiler_params=_cparams(("parallel", "parallel")))(_phase_perm(bl), x)
    return out.reshape(n, width)


def _from_phase_order(y, bl):
    n, width = y.shape
    t = 16 * PHASES * bl
    g = n // bl // PHASES
    tn = min(PERM_LANES, width)

    def body(p_ref, y_ref, o_ref):
        o_ref[...] = jnp.dot(p_ref[...], y_ref[...].reshape(t, tn), preferred_element_type=F32).astype(o_ref.dtype)

    return pl.pallas_call(
        body, name="from_phase", grid=(n // t, width // tn),
        in_specs=[pl.BlockSpec((t, t), lambda i, j: (0, 0)), pl.BlockSpec((bl * PHASES, 16, tn), lambda i, j: (0, i, j))],
        out_specs=pl.BlockSpec((t, tn), lambda i, j: (i, j)), out_shape=jax.ShapeDtypeStruct((n, width), y.dtype),
        compiler_params=_cparams(("parallel", "parallel")))(_phase_perm(bl).T, y.reshape(bl * PHASES, g, width))


def _att_geometry(p, n, bl):
    g = n // bl // PHASES
    if p == 0:
        return ((bl, PHASES, g), (bl, g // 16), (None, PHASES, 16),
                lambda sh: (lambda b, a: (b, 0, jnp.maximum(a + sh, 0))), 256, 16, lambda ids: ids[1] == 0)
    if p == 1:
        return ((bl, 4, 4, g), (bl, 2, g // 32), (None, 4, 2, 32),
                lambda sh: (lambda b, r, a: (b, 0, r, jnp.maximum(a + sh, 0))), 128, 32, lambda ids: ids[2] == 0)
    return ((bl * PHASES, g), (bl * PHASES // 2,), (2, g), lambda sh: (lambda s: (s, 0)), g, g, None)


def _att_units(p, n, bl):
    g = n // bl // PHASES
    full = slice(None)
    if p == 0:
        return [(full, full)], (PHASES, 16)
    if p == 1:
        return [(full, u, full) for u in range(2)], (4, 32)
    return [(u, full) for u in range(2)], (g,)


def _att_masks(p, qb, chunk):
    def pos(idx):
        return (idx % chunk) * (qb // chunk) + idx // chunk

    dq = pos(lax.broadcasted_iota(jnp.int32, (qb, qb), 0))
    dk = pos(lax.broadcasted_iota(jnp.int32, (qb, qb), 1))
    dist = dq - dk
    return jnp.logical_and(dist >= 0, dist <= ATT_WINDOW), dist + qb <= ATT_WINDOW


def _att_call(p, n, bl, c, body, name, ins, outs):
    prefix, grid, blk, idx_fn, qb, chunk, _ = _att_geometry(p, n, bl)

    def spec(cb, sh):
        f = idx_fn(sh)
        return pl.BlockSpec(blk + (c,), lambda *ids, f=f, cb=cb: f(*ids) + (cb,))

    in_specs = [spec(cb, sh) for _, cb, sh in ins]
    out_specs = [spec(0, 0) for _ in outs]
    out_shape = [jax.ShapeDtypeStruct(prefix + (c,), dt) for dt in outs]
    res = pl.pallas_call(body, name=name, grid=grid, in_specs=in_specs, out_specs=out_specs, out_shape=out_shape,
                         compiler_params=_cparams(("parallel",) * len(grid)))(*[a.reshape(prefix + (a.shape[1],)) for a, _, _ in ins])
    return [r.reshape(n, c) for r in res]


class _AttTiles:
    def __init__(self, p, n, bl, c, first):
        _, _, _, _, qb, chunk, _ = _att_geometry(p, n, bl)
        units, self.unit_shape = _att_units(p, n, bl)
        self.split = p == 0
        self.rows = qb // 2 if self.split else qb
        halves = (0, 1) if self.split else (None,)
        self.tiles = [(u, pl.ds(lt * 128, 128), h) for u in units for lt in range(c // 128) for h in halves]
        self.mask_cur, mp = _att_masks(p, self.rows, chunk // 2 if self.split else chunk)
        gated = mp if first is None else jnp.logical_and(mp, jnp.logical_not(first))
        self.mask_prev = [mp if h == 1 else gated for _, _, h in self.tiles]

    def _half(self, x, h):
        return x.astype(F32)[:, 8 * h:8 * h + 8, :].reshape(self.rows, 128).astype(x.dtype)

    def cur(self, ref, t):
        u, ls, h = self.tiles[t]
        x = ref[u + (ls,)]
        return x.reshape(self.rows, 128) if h is None else self._half(x, h)

    def prev(self, cur_ref, prev_ref, t):
        u, ls, h = self.tiles[t]
        if h is None:
            return prev_ref[u + (ls,)].reshape(self.rows, 128)
        return self._half(prev_ref[u + (ls,)], 1) if h == 0 else self._half(cur_ref[u + (ls,)], 0)

    def store(self, ref, vals):
        if not self.split:
            for (u, ls, _), v in zip(self.tiles, vals):
                ref[u + (ls,)] = v.astype(ref.dtype).reshape(self.unit_shape + (128,))
            return
        for k in range(len(self.tiles) // 2):
            u, ls, _ = self.tiles[2 * k]
            parts = [v.astype(F32).reshape(self.unit_shape[0], 8, 128) for v in vals[2 * k:2 * k + 2]]
            ref[u + (ls,)] = jnp.concatenate(parts, axis=1).astype(ref.dtype)

    def fold_keys(self, cur_vals, prev_vals):
        if not self.split:
            return cur_vals, prev_vals
        own, before = [], []
        for k in range(len(self.tiles) // 2):
            own += [cur_vals[2 * k] + prev_vals[2 * k + 1], cur_vals[2 * k + 1]]
            before += [jnp.zeros_like(prev_vals[2 * k]), prev_vals[2 * k]]
        return own, before


def _att_fwd(p, qkv, qcb, bl, c, heads):
    n = qkv.shape[0]
    _, grid, _, _, _, _, first_fn = _att_geometry(p, n, bl)
    n_grid = len(grid)
    has_prev = first_fn is not None
    e = c // heads
    scale = e ** -0.5

    def body(*refs):
        if has_prev:
            q_ref, kc_ref, kp_ref, vc_ref, vp_ref, o_ref, l_ref = refs
        else:
            q_ref, kc_ref, vc_ref, o_ref, l_ref = refs
            kp_ref = vp_ref = None
        tl = _AttTiles(p, n, bl, c, first_fn([pl.program_id(a) for a in range(n_grid)]) if has_prev else None)
        mc, n_t = tl.mask_cur, len(tl.tiles)
        lo = lax.broadcasted_iota(jnp.int32, (tl.rows, 128), 1) < e
        ones = jnp.ones((tl.rows, 128), BF16)
        items = [(t, h) for t in range(n_t) for h in range(2)]
        dot = functools.partial(jnp.dot, preferred_element_type=F32)
        q2 = [tl.cur(q_ref, t) for t in range(n_t)]
        kc = [tl.cur(kc_ref, t) for t in range(n_t)]
        qm = [jnp.where(lo if h == 0 else jnp.logical_not(lo), q2[t], jnp.zeros_like(q2[t])) for t, h in items]
        sc = [jnp.where(mc, lax.dot_general(qm[i], kc[t], _NT, preferred_element_type=F32) * scale, -jnp.inf)
              for i, (t, h) in enumerate(items)]
        m = [jnp.max(s, axis=1, keepdims=True) for s in sc]
        if has_prev:
            kp = [tl.prev(kc_ref, kp_ref, t) for t in range(n_t)]
            sp = [jnp.where(tl.mask_prev[t], lax.dot_general(qm[i], kp[t], _NT, preferred_element_type=F32) * scale, -jnp.inf)
                  for i, (t, h) in enumerate(items)]
            m = [jnp.maximum(a, jnp.max(s, axis=1, keepdims=True)) for a, s in zip(m, sp)]
        pc = [jnp.exp(s - a).astype(BF16) for s, a in zip(sc, m)]
        vc = [tl.cur(vc_ref, t) for t in range(n_t)]
        acc = [dot(pc[i], vc[t]) for i, (t, h) in enumerate(items)]
        den = [dot(x, ones) for x in pc]
        if has_prev:
            pp = [jnp.exp(s - a).astype(BF16) for s, a in zip(sp, m)]
            vp = [tl.prev(vc_ref, vp_ref, t) for t in range(n_t)]
            acc = [a + dot(pp[i], vp[t]) for i, ((t, h), a) in enumerate(zip(items, acc))]
            den = [d + dot(x, ones) for d, x in zip(den, pp)]
        oh = [a / d for a, d in zip(acc, den)]
        lh = [a + jnp.log(d) for a, d in zip(m, den)]
        tl.store(o_ref, [jnp.where(lo, oh[2 * t], oh[2 * t + 1]) for t in range(n_t)])
        tl.store(l_ref, [jnp.where(lo, lh[2 * t], lh[2 * t + 1]) for t in range(n_t)])

    kcb, vcb = 3, 4
    ins = [(qkv, qcb, 0), (qkv, kcb, 0)] + ([(qkv, kcb, -1)] if has_prev else []) + [(qkv, vcb, 0)] + ([(qkv, vcb, -1)] if has_prev else [])
    return _att_call(p, n, bl, c, body, name=f"att_fwd{p}", ins=ins, outs=[BF16, F32])


def _att_bwd(p, qkv, qcb, o, do, lse, bl, c, heads):
    n = qkv.shape[0]
    _, grid, _, _, _, _, first_fn = _att_geometry(p, n, bl)
    n_grid = len(grid)
    has_prev = first_fn is not None
    e = c // heads
    scale = e ** -0.5

    def body(*refs):
        if has_prev:
            q_ref, kc_ref, kp_ref, vc_ref, vp_ref, o_ref, do_ref, l_ref, dq_ref, dkc_ref, dkp_ref, dvc_ref, dvp_ref = refs
        else:
            q_ref, kc_ref, vc_ref, o_ref, do_ref, l_ref, dq_ref, dkc_ref, dvc_ref = refs
            kp_ref = vp_ref = None
        tl = _AttTiles(p, n, bl, c, first_fn([pl.program_id(a) for a in range(n_grid)]) if has_prev else None)
        mc, n_t = tl.mask_cur, len(tl.tiles)
        lo = lax.broadcasted_iota(jnp.int32, (tl.rows, 128), 1) < e
        items = [(t, h) for t in range(n_t) for h in range(2)]
        nt_dot = lambda a, b: lax.dot_general(a, b, _NT, preferred_element_type=F32)
        tn_dot = lambda a, b: lax.dot_general(a, b, _TN, preferred_element_type=F32)
        dot = functools.partial(jnp.dot, preferred_element_type=F32)
        sel = [lo if h == 0 else jnp.logical_not(lo) for t, h in items]
        q2, kc, vc, do2 = ([tl.cur(r, t) for t in range(n_t)] for r in (q_ref, kc_ref, vc_ref, do_ref))
        qm = [jnp.where(sel[i], q2[t], jnp.zeros_like(q2[t])) for i, (t, h) in enumerate(items)]
        dom = [jnp.where(sel[i], do2[t], jnp.zeros_like(do2[t])) for i, (t, h) in enumerate(items)]
        dod = [do2[t].astype(F32) * tl.cur(o_ref, t).astype(F32) for t in range(n_t)]
        lcol = [tl.cur(l_ref, t)[:, h * e:h * e + 1] for t, h in items]
        corr = [-jnp.sum(jnp.where(sel[i], dod[t], 0.0), axis=1, keepdims=True) for i, (t, h) in enumerate(items)]
        pc = [jnp.exp(jnp.where(mc, nt_dot(qm[i], kc[t]) * scale, -jnp.inf) - lcol[i]) for i, (t, h) in enumerate(items)]
        dsc = [(pc[i] * (nt_dot(dom[i], vc[t]) + corr[i]) * scale).astype(BF16) for i, (t, h) in enumerate(items)]
        pc = [x.astype(BF16) for x in pc]
        dq = [dot(dsc[i], kc[t]) for i, (t, h) in enumerate(items)]
        dkc = [tn_dot(dsc[2 * t], qm[2 * t]) + tn_dot(dsc[2 * t + 1], qm[2 * t + 1]) for t in range(n_t)]
        dvc = [tn_dot(pc[2 * t], dom[2 * t]) + tn_dot(pc[2 * t + 1], dom[2 * t + 1]) for t in range(n_t)]
        if has_prev:
            kp = [tl.prev(kc_ref, kp_ref, t) for t in range(n_t)]
            vp = [tl.prev(vc_ref, vp_ref, t) for t in range(n_t)]
            pp = [jnp.exp(jnp.where(tl.mask_prev[t], nt_dot(qm[i], kp[t]) * scale, -jnp.inf) - lcol[i]) for i, (t, h) in enumerate(items)]
            dsp = [(pp[i] * (nt_dot(dom[i], vp[t]) + corr[i]) * scale).astype(BF16) for i, (t, h) in enumerate(items)]
            pp = [x.astype(BF16) for x in pp]
            dq = [a + dot(dsp[i], kp[t]) for i, ((t, h), a) in enumerate(zip(items, dq))]
            dkp = [tn_dot(dsp[2 * t], qm[2 * t]) + tn_dot(dsp[2 * t + 1], qm[2 * t + 1]) for t in range(n_t)]
            dvp = [tn_dot(pp[2 * t], dom[2 * t]) + tn_dot(pp[2 * t + 1], dom[2 * t + 1]) for t in range(n_t)]
            (dkc, dkp), (dvc, dvp) = tl.fold_keys(dkc, dkp), tl.fold_keys(dvc, dvp)
            tl.store(dkp_ref, dkp)
            tl.store(dvp_ref, dvp)
        tl.store(dq_ref, [jnp.where(lo, dq[2 * t], dq[2 * t + 1]) for t in range(n_t)])
        tl.store(dkc_ref, dkc)
        tl.store(dvc_ref, dvc)

    kcb, vcb = 3, 4
    ins = [(qkv, qcb, 0), (qkv, kcb, 0)] + ([(qkv, kcb, -1)] if has_prev else []) + [(qkv, vcb, 0)] + ([(qkv, vcb, -1)] if has_prev else [])
    ins += [(o, 0, 0), (do, 0, 0), (lse, 0, 0)]
    res = _att_call(p, n, bl, c, body, name=f"att_bwd{p}", ins=ins, outs=[BF16] * (5 if has_prev else 3))
    if has_prev:
        dq, dkc, dkp, dvc, dvp = res
        return dq, dkc, dkp, dvc, dvp
    dq, dkc, dvc = res
    return dq, dkc, None, dvc, None


def _att_fold_prev(p, cur, prv, bl):
    if prv is None:
        return cur.astype(F32)
    n, c = cur.shape
    prefix, _, _, _, _, chunk, _ = _att_geometry(p, n, bl)
    v = prv.reshape(prefix + (c,)).astype(F32)
    shifted = jnp.concatenate([v[..., chunk:, :], jnp.zeros_like(v[..., :chunk, :])], axis=-2)
    return cur.astype(F32) + shifted.reshape(n, c)


def _attention_fwd(proj, c, bl, heads):
    n = proj.shape[0]
    qkv = _to_phase_order(proj, bl, col0=c, width=5 * c)
    outs = [_att_fwd(p, qkv, p, bl, c, heads) for p in range(3)]
    ins = [("row", o, c, 0) for o, _ in outs] + [("row", l, c, 0) for _, l in outs]
    o, lse = _rowwise(_combine_fwd_fn, "comb_fwd", n, ins, [(c, BF16), (c, F32)])
    return _from_phase_order(o, bl), (qkv, o, lse)


def _attention_bwd(do_tb, saved, bl, heads):
    qkv, o, lse = saved
    n, c = do_tb.shape
    do = _to_phase_order(do_tb, bl)
    dqs, dk, dv = [], 0.0, 0.0
    for p in range(3):
        dq, dkc, dkp, dvc, dvp = _att_bwd(p, qkv, p, o, do, lse, bl, c, heads)
        dqs.append(dq)
        dk = dk + _att_fold_prev(p, dkc, dkp, bl)
        dv = dv + _att_fold_prev(p, dvc, dvp, bl)
    dqkv = jnp.concatenate(dqs + [dk.astype(BF16), dv.astype(BF16)], axis=1)
    return _from_phase_order(dqkv, bl)


ATT_HEADS = 8
SSM_GROUPS, SSM_STATE, SSM_GROUP = 32, 64, 16


def _row(v):
    return v.reshape(1, -1)


ROWS_TILE = 128


def _to_rows(x):
    bl, seq, d = x.shape

    def body(x_ref, o_ref):
        o_ref[...] = jnp.stack([x_ref[b] for b in range(bl)], axis=1).reshape(ROWS_TILE * bl, d)

    return pl.pallas_call(
        body, name="to_rows", grid=(seq // ROWS_TILE,), in_specs=[pl.BlockSpec((bl, ROWS_TILE, d), lambda i: (0, i, 0))],
        out_specs=pl.BlockSpec((ROWS_TILE * bl, d), lambda i: (i, 0)), out_shape=jax.ShapeDtypeStruct((seq * bl, d), x.dtype),
        compiler_params=_cparams(("parallel",)))(x)


def _from_rows(y, bl):
    n, d = y.shape
    seq = n // bl

    def body(y_ref, o_ref):
        v = y_ref[...].reshape(ROWS_TILE, bl, d)
        for b in range(bl):
            o_ref[b] = v[:, b, :]

    return pl.pallas_call(
        body, name="from_rows", grid=(seq // ROWS_TILE,), in_specs=[pl.BlockSpec((ROWS_TILE * bl, d), lambda i: (i, 0))],
        out_specs=pl.BlockSpec((bl, ROWS_TILE, d), lambda i: (0, i, 0)), out_shape=jax.ShapeDtypeStruct((bl, seq, d), y.dtype),
        compiler_params=_cparams(("parallel",)))(y)


def _carried(result, carry, key, hidden):
    if carry.get(key) is None:
        return result
    result, hidden[key] = result
    return result


def _layer_fwd(x, w, p, bl, carry, late=None):
    n, d = x.shape
    c = d // 2
    hidden = {}
    h, = _rowwise(_rms_fwd_fn, "rms_fwd", n, [("row", x, d, 0), ("par", _row(p["norm1_g"]))], [(d, BF16)])
    proj = _carried(_mm(h, w["w_in"], "nn", BF16, "mm_in", comm=carry.get("mm_in")), carry, "mm_in", hidden)
    if late is not None:
        w = dict(w, **late(hidden["mm_in"]))
    disc, disc_vjp = jax.vjp(_ssm_disc, p["ssm_lambda_re"], p["ssm_lambda_im"], p["ssm_log_dt"], p["ssm_b_re"], p["ssm_b_im"])
    bbd, cdm, a8 = _ssm_pack(*disc, p["ssm_c_re"], p["ssm_c_im"])
    ypre, yg, s_all = _carried(_ssm_fwd(proj, bbd, cdm, a8, _row(p["ssm_d"]), bl, "ssm_fwd", comm=carry.get("ssm_fwd")),
                               carry, "ssm_fwd", hidden)
    zs = _mm(yg, w["w_ssm_glu"], "nn", BF16, "mm_glu")
    o, att = _attention_fwd(proj, c, bl, ATT_HEADS)
    ya = _mm(o, w["w_att_up"], "nn", BF16, "mm_att")
    w32 = jnp.concatenate([p["conv_w"], jnp.zeros((1, c), F32)], axis=0)
    hc, hconv = _conv_fwd(proj, 6, w32, _row(p["conv_b"]), _row(p["conv_ln_g"]), _row(p["conv_ln_b"]), bl, c, "conv_fwd")
    yc = _mm(hc, w["w_conv_pw2"], "nn", BF16, "mm_pw2")
    gates = [("row", proj, d, 4), ("row", proj, d, 5), ("row", proj, d, 6), ("par", _row(p["b_gate"]))]
    branches = [("row", zs, 2 * d, 0), ("row", ya, d, 0), ("row", yc, d, 0)]
    merged, = _rowwise(_merge_fwd_fn, "merge_fwd", n, gates + branches, [(d, BF16)])
    xm = _mm(merged, w["w_out"], "nn", F32, "mm_out", res=x)
    h2, = _rowwise(_rms_fwd_fn, "rms_fwd", n, [("row", xm, d, 0), ("par", _row(p["norm2_g"]))], [(d, BF16)])
    z = _carried(_mm(h2, w["w_ffn_in"], "nn", BF16, "mm_ffn_in", comm=carry.get("mm_ffn_in")), carry, "mm_ffn_in", hidden)
    f = z.shape[1] // 2
    a, = _rowwise(_swiglu_fwd_fn, "swiglu_fwd", n, [("row", z, 2 * f, 0)], [(f, BF16)], tm=256)
    xo = _mm(a, w["w_ffn_out"], "nn", F32, "mm_ffn_out", res=xm)
    saved = dict(x=x, h=h, proj=proj, disc_vjp=disc_vjp, bbd=bbd, cdm=cdm, a8=a8, ypre=ypre, yg=yg, s_all=s_all, zs=zs, o=o,
                 att=att, ya=ya, w32=w32, hc=hc, hconv=hconv, yc=yc, gates=gates, branches=branches, merged=merged, xm=xm,
                 h2=h2, z=z, a=a)
    return xo, saved, hidden, w


def _layer_bwd(dxo, s, w, p, bl, carry):
    n, d = dxo.shape
    c = d // 2
    g, bufs, hidden = {}, {}, {}
    f = s["a"].shape[1]

    def dw(key, a, dy, name):
        bufs[key] = _mm_dw(a, dy, name, 1 if key in ROW_SHARDED else N_CHIPS)

    da = _mm(dxo, w["w_ffn_out"], "nt", BF16, "mm_ffn_out_dx")
    dw("w_ffn_out", s["a"], dxo, "mm_ffn_out_dw")
    dz, = _rowwise(_swiglu_bwd_fn, "swiglu_bwd", n, [("row", s["z"], 2 * f, 0), ("row", da, f, 0)], [(2 * f, BF16)], tm=256)
    dh2 = _mm(dz, w["w_ffn_in"], "nt", F32, "mm_ffn_in_dx")
    dw("w_ffn_in", s["h2"], dz, "mm_ffn_in_dw")
    dxm, dg2 = _rowwise(_rms_bwd_fn, "rms_bwd", n, [("row", s["xm"], d, 0), ("par", _row(p["norm2_g"])), ("row", dh2, d, 0),
                                                   ("row", dxo, d, 0)], [(d, F32)], [d])
    g["norm2_g"] = dg2[0]
    dmerged = _mm(dxm, w["w_out"], "nt", BF16, "mm_out_dx")
    dw("w_out", s["merged"], dxm, "mm_out_dw")
    dgl, dzs, dya, dyc, dbg = _rowwise(_merge_bwd_fn, "merge_bwd", n, s["gates"] + s["branches"] + [("row", dmerged, d, 0)],
                                       [(3 * d, BF16), (2 * d, BF16), (d, BF16), (d, BF16)], [3 * d], tm=256)
    g["b_gate"] = dbg[0]
    dyg = _mm(dzs, w["w_ssm_glu"], "nt", BF16, "mm_glu_dx")
    dw("w_ssm_glu", s["yg"], dzs, "mm_glu_dw")
    du, dbb, dcd, dab, dd = _carried(
        _ssm_bwd(dyg, s["ypre"], s["proj"], s["s_all"], s["cdm"].transpose(0, 2, 1), s["bbd"].transpose(0, 2, 1), s["a8"],
                 _row(p["ssm_d"]), bl, "ssm_bwd", comm=carry.get("ssm_bwd")), carry, "ssm_bwd", hidden)
    dab_re, dab_im, dbb_re, dbb_im, g["ssm_c_re"], g["ssm_c_im"] = _ssm_unpack(dbb, dcd, dab, SSM_GROUPS, SSM_STATE, SSM_GROUP)
    (g["ssm_lambda_re"], g["ssm_lambda_im"], g["ssm_log_dt"], g["ssm_b_re"],
     g["ssm_b_im"]) = s["disc_vjp"]((dab_re, dab_im, dbb_re, dbb_im))
    g["ssm_d"] = dd[0]
    do = _mm(dya, w["w_att_up"], "nt", BF16, "mm_att_dx")
    dw("w_att_up", s["o"], dya, "mm_att_dw")
    dqkv = _attention_bwd(do, s["att"], bl, ATT_HEADS)
    dhc = _mm(dyc, w["w_conv_pw2"], "nt", BF16, "mm_pw2_dx")
    dw("w_conv_pw2", s["hc"], dyc, "mm_pw2_dw")
    dcv, dcw, dcb, dlg, dlb = _carried(
        _conv_bwd(s["proj"], 6, dhc, s["hconv"], s["w32"], _row(p["conv_ln_g"]), _row(p["conv_ln_b"]), bl, c, "conv_bwd",
                  comm=carry.get("conv_bwd")), carry, "conv_bwd", hidden)
    g["conv_w"], g["conv_b"], g["conv_ln_g"], g["conv_ln_b"] = dcw, dcb[0], dlg[0], dlb[0]
    dproj = jnp.concatenate([du, dqkv, dcv, dgl], axis=1)
    dh = _mm(dproj, w["w_in"], "nt", F32, "mm_in_dx")
    dw("w_in", s["h"], dproj, "mm_in_dw")
    dx, dg1 = _rowwise(_rms_bwd_fn, "rms_bwd", n, [("row", s["x"], d, 0), ("par", _row(p["norm1_g"])), ("row", dh, d, 0),
                                                  ("row", dxm, d, 0)], [(d, F32)], [d])
    g["norm1_g"] = dg1[0]
    return dx, g, bufs, hidden


WEIGHTS = ['norm1_g', 'w_in', 'b_gate', 'ssm_lambda_re', 'ssm_lambda_im', 'ssm_log_dt', 'ssm_b_re', 'ssm_b_im', 'ssm_c_re',
           'ssm_c_im', 'ssm_d', 'w_ssm_glu', 'w_att_up', 'conv_w', 'conv_b', 'conv_ln_g', 'conv_ln_b', 'w_conv_pw2', 'w_out',
           'norm2_g', 'w_ffn_in', 'w_ffn_out', 'final_g']
BIG = ['w_in', 'w_ssm_glu', 'w_att_up', 'w_conv_pw2', 'w_out', 'w_ffn_in', 'w_ffn_out']
ROW_SHARDED = ('w_out', 'w_ffn_out')
SMALL = [k for k in WEIGHTS if k not in BIG]
LANES = 1024
N_CHIPS = 4
ROW_TILE_BYTES = 36 * 1024 * 1024
MIN_SHARD_TILE = 1024


def _pad_rows(a, rows):
    return jnp.concatenate([a, jnp.zeros((rows - a.shape[0],) + a.shape[1:], a.dtype)], axis=0) if rows > a.shape[0] else a


def _row_tile(rows, width, n_arrays):
    best = 16
    for t in range(16, rows + 1, 16):
        if rows % t == 0 and t * width * 4 * n_arrays * 2 <= ROW_TILE_BYTES:
            best = t
    return best


def _flat_fn(fn, name, ins, n_out, rows):
    return _rowwise(fn, name, rows, [("row", a, LANES, 0) for a in ins], [(LANES, F32)] * n_out, tm=rows)


def _reduce_prepare(bufs):
    landed = _run_exchange(_SwapHalves([b16 for _, b16 in bufs]), "rs_swap")
    p32s, p16s = [], []
    for (b32, _), la in zip(bufs, landed):
        s, m, cs = b32.shape
        h = m // 2
        tm = _row_tile(h, cs, 4)

        def body(g_ref, l_ref, o32, o16):
            r = g_ref[...] + l_ref[...].astype(F32)
            o32[...] = r
            o16[...] = r.astype(BF16)

        piece = pl.BlockSpec((None, tm, cs), lambda j, i: (j, i, 0))
        mine = pl.BlockSpec((None, None, tm, cs), lambda j, i: (j, _core_index(), i, 0))
        p32, p16 = pl.pallas_call(
            body, name="rs_add", grid=(s, h // tm), in_specs=[mine, piece], out_specs=[piece, piece],
            out_shape=[jax.ShapeDtypeStruct((s, h, cs), F32), jax.ShapeDtypeStruct((s, h, cs), BF16)],
            compiler_params=_cparams(("parallel", "parallel")))(b32.reshape(s, 2, h, cs), la)
        p32s.append(p32)
        p16s.append(p16)
    return p32s, p16s


def _reduce_finish(p32s, arrived):
    reduced = []
    for p32, lb in zip(p32s, arrived):
        _, h, cs = lb.shape
        tm = _row_tile(h, cs, 5)

        def body(p_ref, a_ref, b_ref, c_ref, o_ref):
            o_ref[...] = ((p_ref[...] + a_ref[...].astype(F32)) + b_ref[...].astype(F32)) + c_ref[...].astype(F32)

        mine = pl.BlockSpec((None, tm, cs), lambda i: (_chip_index(), i, 0))
        other = [pl.BlockSpec((None, tm, cs), lambda i, k=k: (k, i, 0)) for k in range(3)]
        half = pl.BlockSpec((None, tm, cs), lambda i: (_core_index(), i, 0))
        reduced.append(pl.pallas_call(
            body, name="rs_sum", grid=(h // tm,), in_specs=[mine] + other, out_specs=half,
            out_shape=jax.ShapeDtypeStruct((2, h, cs), F32), compiler_params=_cparams(("parallel",)))(p32, lb, lb, lb))
    joined = _run_exchange(_JoinHalves(reduced), "rs_gather")
    return [j.reshape(2 * j.shape[1], j.shape[2]) for j in joined]


def _adamw_layers(w, g_layers, m, v):
    depth, rows, cs = w.shape
    tm = _row_tile(rows, cs, 8)
    nb = rows // tm

    def body(*refs):
        w_ref, m_ref, v_ref = refs[:3]
        g_refs = refs[3:3 + depth]
        go_ref, d_ref, mo_ref, vo_ref = refs[3 + depth:]
        layer = pl.program_id(0)
        g = g_refs[0][...]
        for l in range(1, depth):
            g = jnp.where(layer == l, g_refs[l][...], g)
        delta, mo, vo = _adamw_fn(w_ref[...], g, m_ref[...], v_ref[...])
        go_ref[...], d_ref[...], mo_ref[...], vo_ref[...] = g, delta, mo, vo

    stacked = pl.BlockSpec((None, tm, cs), lambda l, i: (l, i, 0))
    g_specs = [pl.BlockSpec((tm, cs), lambda l, i, k=k: (jnp.where(l == k, i, jnp.where(l < k, 0, nb - 1)), 0)) for k in range(depth)]
    return pl.pallas_call(
        body, name="adamw", grid=(depth, nb), in_specs=[stacked] * 3 + g_specs, out_specs=[stacked] * 4,
        out_shape=[jax.ShapeDtypeStruct(w.shape, F32)] * 4, compiler_params=_cparams(("arbitrary", "arbitrary")))(w, m, v, *g_layers)


def _sum4_fn(a, b, c, d):
    return (((a.astype(F32) + b.astype(F32)) + c.astype(F32)) + d.astype(F32),)


def _add2_fn(a, b):
    return (a + b,)


def kernel(x, norm1_g, w_in, b_gate, ssm_lambda_re, ssm_lambda_im, ssm_log_dt, ssm_b_re, ssm_b_im, ssm_c_re, ssm_c_im, ssm_d, w_ssm_glu, w_att_up, conv_w, conv_b, conv_ln_g, conv_ln_b, w_conv_pw2, w_out, norm2_g, w_ffn_in, w_ffn_out, final_g, loss_target, m_norm1_g, m_w_in, m_b_gate, m_ssm_lambda_re, m_ssm_lambda_im, m_ssm_log_dt, m_ssm_b_re, m_ssm_b_im, m_ssm_c_re, m_ssm_c_im, m_ssm_d, m_w_ssm_glu, m_w_att_up, m_conv_w, m_conv_b, m_conv_ln_g, m_conv_ln_b, m_w_conv_pw2, m_w_out, m_norm2_g, m_w_ffn_in, m_w_ffn_out, m_final_g, v_norm1_g, v_w_in, v_b_gate, v_ssm_lambda_re, v_ssm_lambda_im, v_ssm_log_dt, v_ssm_b_re, v_ssm_b_im, v_ssm_c_re, v_ssm_c_im, v_ssm_d, v_w_ssm_glu, v_w_att_up, v_conv_w, v_conv_b, v_conv_ln_g, v_conv_ln_b, v_w_conv_pw2, v_w_out, v_norm2_g, v_w_ffn_in, v_w_ffn_out, v_final_g):
    args = dict(locals())
    wts = {k: args[k] for k in WEIGHTS}
    mom = {k: args["m_" + k] for k in WEIGHTS}
    var = {k: args["v_" + k] for k in WEIGHTS}
    bl, seq, d = x.shape
    n = bl * seq
    depth = norm1_g.shape[0]
    cx, cy, cc = _position()
    me = 2 * cx + cy

    assert depth == 2, "the exchanges of layer 1 are hidden behind layer 0's kernels"
    first = BIG[:1]
    rest = BIG[1:]

    shards = lambda keys, l: [wts[k][l].astype(BF16) for k in keys]

    def whole(keys, gathered):
        out = {}
        for k, a in zip(keys, gathered):
            _, ks, cs = a.shape
            if k in ROW_SHARDED:
                out[k] = a.reshape(N_CHIPS * ks, cs)
            elif cs < MIN_SHARD_TILE:
                out[k] = a.transpose(1, 0, 2).reshape(ks, N_CHIPS * cs)
            else:
                out[k] = a
        return out

    fill = lambda gathered, own: [_own_slot(g, o) for g, o in zip(gathered, own)]
    own0 = shards(first, 0) + [conv_w]
    gathered = fill(_run_exchange(_GatherShards(own0), "gather_weights"), own0)
    conv_full = gathered[-1].transpose(1, 2, 0, 3).reshape(depth, CONV_WIDTH, -1)
    params = lambda l: dict({k: wts[k][l] for k in SMALL if k not in ("final_g", "conv_w")}, conv_w=conv_full[l])

    to_rows = _to_rows
    own = {"mm_in": shards(rest, 0), "ssm_fwd": shards(first, 1), "mm_ffn_in": shards(rest, 1)}
    xs, s0, hidden, w0 = _layer_fwd(to_rows(x), whole(first, gathered[:-1]), params(0), bl, {k: _GatherShards(v) for k, v in own.items()},
                                    late=lambda got: whole(rest, fill(got, own["mm_in"])))
    w1 = dict(whole(first, fill(hidden["ssm_fwd"], own["ssm_fwd"])), **whole(rest, fill(hidden["mm_ffn_in"], own["mm_ffn_in"])))
    full = [w0, w1]
    xs, s1, _, _ = _layer_fwd(xs, full[1], params(1), bl, {})
    dx, sq, dgf = _rowwise(_loss_fn, "loss_head", n, [("row", xs, d, 0), ("par", _row(final_g)), ("row", to_rows(loss_target), d, 0)],
                           [(d, F32)], [d, d])
    loss = lax.psum(0.5 * jnp.sum(sq) / d, ("x", "y", "c"))

    pieces = lambda bufs, keys: [tuple(b.reshape(N_CHIPS, -1, b.shape[-1]) for b in bufs[k]) for k in keys]
    dx, g1, bufs1, _ = _layer_bwd(dx, s1, full[1], params(1), bl, {})
    p32_1, p16_1 = _reduce_prepare(pieces(bufs1, BIG))
    dx, g0, bufs0, hidden = _layer_bwd(dx, s0, full[0], params(0), bl,
                                       {"ssm_bwd": _ScatterPieces(p16_1[:1]), "conv_bwd": _ScatterPieces(p16_1[1:])})
    red1 = _reduce_finish(p32_1, list(hidden["ssm_bwd"]) + list(hidden["conv_bwd"]))
    p32_0, p16_0 = _reduce_prepare(pieces(bufs0, BIG))
    red0 = _reduce_finish(p32_0, _run_exchange(_ScatterPieces(p16_0), "rs_scatter"))
    grads = {"final_g": dgf[0]}
    for k in SMALL:
        if k != "final_g":
            grads[k] = jnp.stack([g0[k], g1[k]])
    grad_x = _from_rows(dx, bl)
    outs = {}
    for k, r0, r1 in zip(BIG, red0, red1):
        for tag, a in zip(("grad", "delta", "m", "v"), _adamw_layers(wts[k], [r0, r1], mom[k], var[k])):
            outs[tag, k] = a

    def flat1(t):
        v = jnp.concatenate([t[k].reshape(-1) for k in SMALL])
        rows = -(-v.size // (8 * LANES)) * 8
        return _pad_rows(v, rows * LANES).reshape(rows, LANES), rows

    def unflat1(flat, shapes):
        out, off, v = {}, 0, flat.reshape(-1)
        for k in SMALL:
            size = math.prod(shapes[k])
            out[k] = v[off:off + size].reshape(shapes[k])
            off += size
        return out

    grads["conv_w"] = grads["conv_w"][:, :CONV_WIDTH]
    gs, rows = flat1(grads)
    chip_sum, = _flat_fn(_add2_fn, "ar_add", [gs, _swap_sibling(gs, "ar_swap")], 1, rows)
    slots = _chip_allgather(chip_sum, "ar_gather")
    gs_red, = _flat_fn(_sum4_fn, "ar_sum", [slots[j] for j in range(N_CHIPS)], 1, rows)
    g_sm = unflat1(gs_red, {k: grads[k].shape for k in SMALL})
    cs = conv_w.shape[2]
    g_sm["conv_w"] = lax.dynamic_slice_in_dim(g_sm["conv_w"], me * cs, cs, axis=2)
    (w1, rows), (g1, _), (m1, _), (v1, _) = flat1(wts), flat1(g_sm), flat1(mom), flat1(var)
    sm_out = _flat_fn(_adamw_fn, "adamw_small", [w1, g1, m1, v1], 3, rows)
    shapes = {k: wts[k].shape for k in SMALL}
    for tag, a in zip(("delta", "m", "v"), sm_out):
        for k, t in unflat1(a, shapes).items():
            outs[tag, k] = t
    for k in SMALL:
        outs["grad", k] = g_sm[k]
    return (loss, grad_x, *[outs["grad", k] for k in WEIGHTS], *[outs["delta", k] for k in WEIGHTS],
            *[outs["m", k] for k in WEIGHTS], *[outs["v", k] for k in WEIGHTS])
```

```python
import functools
import math

import jax
import jax.numpy as jnp
from jax import lax
from jax.experimental import pallas as pl
from jax.experimental.pallas import tpu as pltpu

F32 = jnp.float32
BF16 = jnp.bfloat16
VMEM_LIMIT = 56 * 1024 * 1024


def _cparams(sem):
    return pltpu.CompilerParams(dimension_semantics=sem, vmem_limit_bytes=VMEM_LIMIT)


_DIMS = {"nn": (((1,), (0,)), ((), ())), "nt": (((1,), (1,)), ((), ())), "tn": (((0,), (0,)), ((), ()))}


MM_ROWS = 1024
MM_DW_VMEM_BYTES = 44 * 1024 * 1024
MM_SMALL_STEP = 1024 * 1024 * 512


def _div_tile(n, cap):
    best = None
    for t in range(128, min(n, cap) + 1, 128):
        if n % t == 0:
            best = t
    return best or n


def _mm(a, b, form, out_dtype, name, res=None, comm=None):
    sharded = b.ndim == 3
    kdim, cs = b.shape[-2], b.shape[-1]
    s = b.shape[0] if sharded else 1
    m = a.shape[0]
    tm = MM_ROWS if m % MM_ROWS == 0 else _div_tile(m, MM_ROWS)
    if form == "nn":
        n, kd = s * cs, kdim
        tn, tk = _div_tile(cs, 1792), _div_tile(kdim, 2048)
        per = cs // tn
        b_blk = (tk, tn)
        b_idx = (lambda i, j, k: (j // per, k, j % per)) if sharded else (lambda i, j, k: (k, j))
    else:
        n, kd = kdim, s * cs
        tn, tk = _div_tile(kdim, 1408), _div_tile(cs, 1792)
        per = cs // tk
        b_blk = (tn, tk)
        b_idx = (lambda i, j, k: (k // per, j, k % per)) if sharded else (lambda i, j, k: (j, k))
    nk = kd // tk
    if tm * tn * tk <= MM_SMALL_STEP and m % (2 * tm) == 0:
        tm *= 2
    a_spec = pl.BlockSpec((tm, tk), lambda i, j, k: (i, k))
    b_spec = pl.BlockSpec(((None,) + b_blk) if sharded else b_blk, b_idx)
    o_spec = pl.BlockSpec((tm, tn), lambda i, j, k: (i, j))
    dims = _DIMS[form]

    def body(*refs):
        a_ref, b_ref = refs[:2]
        r_ref = refs[2] if res is not None else None
        o_ref = refs[3] if res is not None else refs[2]
        p = lax.dot_general(a_ref[...].astype(BF16), b_ref[...], dims, preferred_element_type=F32)

        def finish(r):
            if r_ref is not None:
                r = r + r_ref[...]
            o_ref[...] = r.astype(out_dtype)

        if nk == 1:
            finish(p)
            return
        acc = refs[-1]
        k = pl.program_id(2)

        @pl.when(k == 0)
        def _():
            acc[...] = p

        @pl.when(k > 0)
        def _():
            acc[...] += p

        @pl.when(k == nk - 1)
        def _():
            finish(acc[...])

    ins = [a, b] + ([] if res is None else [res])
    in_specs = [a_spec, b_spec] + ([] if res is None else [o_spec])
    out = _pcall(body, name, (m // tm, n // tn, nk), in_specs, [o_spec], [jax.ShapeDtypeStruct((m, n), out_dtype)],
                 [pltpu.VMEM((tm, tn), F32)] if nk > 1 else [], ("parallel", "parallel", "arbitrary"), ins, comm)
    return out[0] if comm is None else (out[0][0], out[1])


def _mm_dw(a, dy, name, shards):
    r, m = a.shape
    c = dy.shape[1]
    cs = c // shards
    tm, tn = _div_tile(m, 1408), _div_tile(cs, 1408)
    fixed = tm * tn * (4 + 2 * (4 + 2))
    per_row = 2 * (tm * a.dtype.itemsize + tn * dy.dtype.itemsize)
    tk = max(t for t in (256, 512, 1024, 2048) if r % t == 0 and (t == 256 or fixed + t * per_row <= MM_DW_VMEM_BYTES))
    per = cs // tn
    nk = r // tk

    def body(a_ref, b_ref, o32, o16, acc):
        k = pl.program_id(2)
        p = lax.dot_general(a_ref[...].astype(BF16), b_ref[...].astype(BF16), _DIMS["tn"], preferred_element_type=F32)

        @pl.when(k == 0)
        def _():
            acc[...] = p

        @pl.when(k > 0)
        def _():
            acc[...] += p

        @pl.when(k == nk - 1)
        def _():
            o32[...] = acc[...]
            o16[...] = acc[...].astype(BF16)

    o_spec = pl.BlockSpec((None, tm, tn), lambda i, j, k: (j // per, i, j % per))
    shape = (shards, m, cs)
    in_specs = [pl.BlockSpec((tk, tm), lambda i, j, k: (k, i)), pl.BlockSpec((tk, tn), lambda i, j, k: (k, j))]
    return _pcall(body, name, (m // tm, c // tn, nk), in_specs, [o_spec, o_spec],
                  [jax.ShapeDtypeStruct(shape, F32), jax.ShapeDtypeStruct(shape, BF16)], [pltpu.VMEM((tm, tn), F32)],
                  ("parallel", "parallel", "arbitrary"), [a, dy])


def _core_index():
    return lax.axis_index("c")


def _chip_index():
    return 2 * lax.axis_index("x") + lax.axis_index("y")


def _rowwise(fn, name, n_rows, ins, outs, accs=(), tm=512):
    n_in, n_out, n_acc = len(ins), len(outs), len(accs)
    in_specs, args = [], []
    for spec in ins:
        if spec[0] == "row":
            _, arr, w, cb = spec
            in_specs.append(pl.BlockSpec((tm, w), lambda i, cb=cb: (i, cb)))
        else:
            arr = spec[1]
            in_specs.append(pl.BlockSpec(arr.shape, lambda i: (0, 0)))
        args.append(arr)
    out_specs = [pl.BlockSpec((tm, w), lambda i: (i, 0)) for w, _ in outs]
    out_specs += [pl.BlockSpec((1, w), lambda i: (0, 0)) for w in accs]
    out_shape = [jax.ShapeDtypeStruct((n_rows, w), dt) for w, dt in outs]
    out_shape += [jax.ShapeDtypeStruct((1, w), F32) for w in accs]

    def body(*refs):
        i = pl.program_id(0)
        res = fn(*[r[...] for r in refs[:n_in]])
        for o_ref, r in zip(refs[n_in:n_in + n_out], res[:n_out]):
            o_ref[...] = r.astype(o_ref.dtype)
        for a_ref, r in zip(refs[n_in + n_out:], res[n_out:]):
            @pl.when(i == 0)
            def _(a_ref=a_ref, r=r):
                a_ref[...] = r

            @pl.when(i > 0)
            def _(a_ref=a_ref, r=r):
                a_ref[...] += r

    return pl.pallas_call(
        body, name=name, grid=(n_rows // tm,), in_specs=in_specs, out_specs=out_specs, out_shape=out_shape,
        compiler_params=_cparams(("arbitrary",)))(*args)


EPS = 1e-6


def _sig(x):
    return 1.0 / (1.0 + jnp.exp(-x))


def _colsum(x):
    return jnp.sum(x, axis=0, keepdims=True)


def _rms_fwd_fn(x, g):
    r = lax.rsqrt(jnp.mean(x * x, axis=-1, keepdims=True) + EPS)
    return (x * r * g,)


def _rms_bwd_fn(x, g, dh, dres):
    dh = dh.astype(F32)
    r = lax.rsqrt(jnp.mean(x * x, axis=-1, keepdims=True) + EPS)
    xh = x * r
    dyg = dh * g
    dx = r * (dyg - xh * jnp.mean(dyg * xh, axis=-1, keepdims=True)) + dres
    return dx, _colsum(dh * xh)


def _loss_fn(x, g, t):
    d = x.shape[-1]
    r = lax.rsqrt(jnp.mean(x * x, axis=-1, keepdims=True) + EPS)
    xh = x * r
    err = xh * g - t
    dy = err * (1.0 / d)
    dyg = dy * g
    dx = r * (dyg - xh * jnp.mean(dyg * xh, axis=-1, keepdims=True))
    return dx, _colsum(err * err), _colsum(dy * xh)


def _swiglu_fwd_fn(z):
    f = z.shape[-1] // 2
    z1, z2 = z[:, :f].astype(F32), z[:, f:].astype(F32)
    return (z1 * _sig(z1) * z2,)


def _swiglu_bwd_fn(z, da):
    f = z.shape[-1] // 2
    z1, z2, da = z[:, :f].astype(F32), z[:, f:].astype(F32), da.astype(F32)
    s = _sig(z1)
    dz1 = da * z2 * (s * (1.0 + z1 * (1.0 - s)))
    dz2 = da * (z1 * s)
    return (jnp.concatenate([dz1, dz2], axis=1),)


def _merge_fwd_fn(g0, g1, g2, bg, zs, ya, yc):
    d = ya.shape[-1]
    bg = bg.astype(F32)
    zs = zs.astype(F32)
    ys = zs[:, :d] * _sig(zs[:, d:])
    m = _sig(g0.astype(F32) + bg[:, :d]) * ys
    m = m + _sig(g1.astype(F32) + bg[:, d:2 * d]) * ya.astype(F32)
    m = m + _sig(g2.astype(F32) + bg[:, 2 * d:]) * yc.astype(F32)
    return (m,)


def _merge_bwd_fn(g0, g1, g2, bg, zs, ya, yc, dm):
    d = ya.shape[-1]
    bg = bg.astype(F32)
    zs = zs.astype(F32)
    dm = dm.astype(F32)
    z1, s2 = zs[:, :d], _sig(zs[:, d:])
    ys = z1 * s2
    s0 = _sig(g0.astype(F32) + bg[:, :d])
    s1 = _sig(g1.astype(F32) + bg[:, d:2 * d])
    s3 = _sig(g2.astype(F32) + bg[:, 2 * d:])
    dgl = jnp.concatenate([dm * ys * s0 * (1.0 - s0), dm * ya.astype(F32) * s1 * (1.0 - s1),
                           dm * yc.astype(F32) * s3 * (1.0 - s3)], axis=1)
    dys = dm * s0
    dzs = jnp.concatenate([dys * s2, dys * z1 * s2 * (1.0 - s2)], axis=1)
    return dgl, dzs, dm * s1, dm * s3, _colsum(dgl)


def _combine_fwd_fn(o0, o1, o2, l0, l1, l2):
    m = jnp.maximum(jnp.maximum(l0, l1), l2)
    e0, e1, e2 = jnp.exp(l0 - m), jnp.exp(l1 - m), jnp.exp(l2 - m)
    den = e0 + e1 + e2
    return (e0 * o0.astype(F32) + e1 * o1.astype(F32) + e2 * o2.astype(F32)) / den, m + jnp.log(den)


ADAM_LR, ADAM_B1, ADAM_B2, ADAM_EPS, ADAM_WD, ADAM_STEP = 0.001, 0.9, 0.999, 1e-08, 0.01, 10


def _adamw_fn(w, g, m, v):
    m = ADAM_B1 * m + (1.0 - ADAM_B1) * g
    v = ADAM_B2 * v + (1.0 - ADAM_B2) * (g * g)
    m_hat = m / (1.0 - ADAM_B1 ** ADAM_STEP)
    v_hat = v / (1.0 - ADAM_B2 ** ADAM_STEP)
    delta = -ADAM_LR * (m_hat / (jnp.sqrt(v_hat) + ADAM_EPS) + ADAM_WD * w)
    return delta, m, v


CONV_WIDTH = 31


def _conv_fwd(proj, cb, w32, conv_b, ln_g, ln_b, bl, c, name, tm=512):
    n = proj.shape[0]
    hp = (CONV_WIDTH - 1) * bl
    nt = n // tm

    def body(ap_ref, gp_ref, a_ref, g_ref, w_ref, cb_ref, lg_ref, lb_ref, hc_ref, hconv_ref, ext):
        i = pl.program_id(0)
        ext[pl.ds(hp, tm), :] = a_ref[...].astype(F32) * _sig(g_ref[...].astype(F32))
        hgp = ap_ref[pl.ds(tm - hp, hp), :].astype(F32) * _sig(gp_ref[pl.ds(tm - hp, hp), :].astype(F32))
        ext[pl.ds(0, hp), :] = jnp.where(i > 0, hgp, 0.0)
        acc = jnp.zeros((tm, c), F32) + cb_ref[...]
        for j in range(CONV_WIDTH):
            acc = acc + w_ref[j:j + 1, :] * ext[pl.ds(j * bl, tm), :]
        hconv_ref[...] = acc.astype(hconv_ref.dtype)
        h = hconv_ref[...].astype(F32)
        mu = jnp.mean(h, axis=-1, keepdims=True)
        xc = h - mu
        var = jnp.mean(xc * xc, axis=-1, keepdims=True)
        hn = xc * lax.rsqrt(var + EPS) * lg_ref[...] + lb_ref[...]
        hc_ref[...] = (hn * _sig(hn)).astype(hc_ref.dtype)

    prev = lambda i, k: (jnp.maximum(i - 1, 0), k)
    par = lambda arr: pl.BlockSpec(arr.shape, lambda i: (0, 0))
    return pl.pallas_call(
        body, name=name, grid=(nt,),
        in_specs=[pl.BlockSpec((tm, c), functools.partial(prev, k=cb)), pl.BlockSpec((tm, c), functools.partial(prev, k=cb + 1)),
                  pl.BlockSpec((tm, c), lambda i: (i, cb)), pl.BlockSpec((tm, c), lambda i: (i, cb + 1)),
                  par(w32), par(conv_b), par(ln_g), par(ln_b)],
        out_specs=[pl.BlockSpec((tm, c), lambda i: (i, 0))] * 2,
        out_shape=[jax.ShapeDtypeStruct((n, c), BF16)] * 2,
        scratch_shapes=[pltpu.VMEM((hp + tm, c), F32)],
        compiler_params=_cparams(("arbitrary",)))(proj, proj, proj, proj, w32, conv_b, ln_g, ln_b)


def _conv_bwd(proj, cb, dhc, hconv, w32, ln_g, ln_b, bl, c, name, tm=512, comm=None):
    n = proj.shape[0]
    hp = (CONV_WIDTH - 1) * bl
    nt = n // tm

    def ln_bwd(d, h, lg, lb):
        d, h = d.astype(F32), h.astype(F32)
        mu = jnp.mean(h, axis=-1, keepdims=True)
        xc = h - mu
        rstd = lax.rsqrt(jnp.mean(xc * xc, axis=-1, keepdims=True) + EPS)
        xh = xc * rstd
        hn = xh * lg + lb
        s = _sig(hn)
        dhn = d * (s * (1.0 + hn * (1.0 - s)))
        dxh = dhn * lg
        dh = rstd * (dxh - jnp.mean(dxh, axis=-1, keepdims=True) - xh * jnp.mean(dxh * xh, axis=-1, keepdims=True))
        return dh, dhn, xh

    def body(ap_ref, gp_ref, a_ref, g_ref, d_ref, dn_ref, h_ref, hn_ref, w_ref, lg_ref, lb_ref,
             dcv_ref, dw_ref, dcb_ref, dlg_ref, dlb_ref, ext_h, ext_d):
        i = pl.program_id(0)
        lg, lb = lg_ref[...], lb_ref[...]
        a, g = a_ref[...].astype(F32), g_ref[...].astype(F32)
        sg = _sig(g)
        ext_h[pl.ds(hp, tm), :] = a * sg
        hgp = ap_ref[pl.ds(tm - hp, hp), :].astype(F32) * _sig(gp_ref[pl.ds(tm - hp, hp), :].astype(F32))
        ext_h[pl.ds(0, hp), :] = jnp.where(i > 0, hgp, 0.0)
        dh, dhn, xh = ln_bwd(d_ref[...], h_ref[...], lg, lb)
        ext_d[pl.ds(0, tm), :] = dh
        dh_n, _, _ = ln_bwd(dn_ref[pl.ds(0, hp), :], hn_ref[pl.ds(0, hp), :], lg, lb)
        ext_d[pl.ds(tm, hp), :] = jnp.where(i < nt - 1, dh_n, 0.0)

        @pl.when(i == 0)
        def _():
            dw_ref[...] = jnp.zeros_like(dw_ref)
            dcb_ref[...] = jnp.zeros_like(dcb_ref)
            dlg_ref[...] = jnp.zeros_like(dlg_ref)
            dlb_ref[...] = jnp.zeros_like(dlb_ref)

        dcb_ref[...] += _colsum(dh)
        dlg_ref[...] += _colsum(dhn * xh)
        dlb_ref[...] += _colsum(dhn)
        dhg = jnp.zeros((tm, c), F32)
        for j in range(CONV_WIDTH):
            dhg = dhg + w_ref[j:j + 1, :] * ext_d[pl.ds((CONV_WIDTH - 1 - j) * bl, tm), :]
            dw_ref[j:j + 1, :] += _colsum(dh * ext_h[pl.ds(j * bl, tm), :])
        dcv_ref[...] = jnp.concatenate([dhg * sg, dhg * a * sg * (1.0 - sg)], axis=1).astype(dcv_ref.dtype)

    prev = lambda i, k: (jnp.maximum(i - 1, 0), k)
    nxt = lambda i: (jnp.minimum(i + 1, nt - 1), 0)
    cur = lambda i: (i, 0)
    par = lambda arr: pl.BlockSpec(arr.shape, lambda i: (0, 0))
    acc = lambda r: pl.BlockSpec((r, c), lambda i: (0, 0))
    in_specs = [pl.BlockSpec((tm, c), functools.partial(prev, k=cb)), pl.BlockSpec((tm, c), functools.partial(prev, k=cb + 1)),
                pl.BlockSpec((tm, c), lambda i: (i, cb)), pl.BlockSpec((tm, c), lambda i: (i, cb + 1)),
                pl.BlockSpec((tm, c), cur), pl.BlockSpec((tm, c), nxt), pl.BlockSpec((tm, c), cur), pl.BlockSpec((tm, c), nxt),
                par(w32), par(ln_g), par(ln_b)]
    out_shape = [jax.ShapeDtypeStruct((n, 2 * c), BF16), jax.ShapeDtypeStruct((32, c), F32)] + [jax.ShapeDtypeStruct((1, c), F32)] * 3
    return _pcall(body, name, (nt,), in_specs, [pl.BlockSpec((tm, 2 * c), cur), acc(32), acc(1), acc(1), acc(1)], out_shape,
                  [pltpu.VMEM((hp + tm, c), F32), pltpu.VMEM((hp + tm, c), F32)], ("arbitrary",),
                  [proj, proj, proj, proj, dhc, dhc, hconv, hconv, w32, ln_g, ln_b], comm)


SSM_CH = 128
_GELU_C = 0.7978845608028654


def _gelu(x):
    return 0.5 * x * (1.0 + jnp.tanh(_GELU_C * (x + 0.044715 * x * x * x)))


def _gelu_grad(x):
    th = jnp.tanh(_GELU_C * (x + 0.044715 * x * x * x))
    return 0.5 * (1.0 + th) + 0.5 * x * (1.0 - th * th) * (_GELU_C * (1.0 + 3.0 * 0.044715 * x * x))


def _ssm_disc(lam_re, lam_im, log_dt, b_re, b_im):
    dt = jnp.exp(log_dt)[:, None]
    mag = jnp.exp(lam_re * dt)
    ab_re = mag * jnp.cos(lam_im * dt)
    ab_im = mag * jnp.sin(lam_im * dt)
    nr, ni = ab_re - 1.0, ab_im
    den = lam_re * lam_re + lam_im * lam_im
    z_re = ((nr * lam_re + ni * lam_im) / den)[..., None]
    z_im = ((ni * lam_re - nr * lam_im) / den)[..., None]
    return ab_re, ab_im, z_re * b_re - z_im * b_im, z_re * b_im + z_im * b_re


def _ssm_pack(ab_re, ab_im, bb_re, bb_im, c_re, c_im):
    g, p, h = bb_re.shape
    gc = SSM_CH // h
    nc = g // gc
    eye = jnp.eye(gc, dtype=F32)
    blk = lambda x: jnp.einsum("qgph,gk->qghkp", x.reshape(nc, gc, p, h), eye).reshape(nc, gc * h, gc * p)
    bbd = jnp.concatenate([blk(bb_re), blk(bb_im)], axis=2).astype(BF16)
    blc = lambda x: jnp.einsum("qghp,gk->qgpkh", x.reshape(nc, gc, h, p), eye).reshape(nc, gc * p, gc * h)
    cdm = jnp.concatenate([blc(c_re), blc(-c_im)], axis=1).astype(BF16)
    a = jnp.concatenate([ab_re.reshape(nc, gc * p), ab_im.reshape(nc, gc * p)], axis=1)
    a8 = jnp.broadcast_to(a[:, None, :], (nc, 8, 2 * gc * p)).reshape(nc * 8, 2 * gc * p)
    return bbd, cdm, a8


def _ssm_unpack(dbb, dcd, da, g, p, h):
    gc = SSM_CH // h
    nc = g // gc
    ph = gc * p
    eye = jnp.eye(gc, dtype=F32)
    dia = lambda x, o: jnp.einsum("qgpkh,gk->" + o, x.reshape(nc, gc, p, gc, h), eye).reshape((g, p, h) if o == "qgph" else (g, h, p))
    das = da.reshape(nc, 8, 2 * ph).sum(axis=1)
    return (das[:, :ph].reshape(g, p), das[:, ph:].reshape(g, p), dia(dbb[:, :ph], "qgph"), dia(dbb[:, ph:], "qgph"),
            dia(dcd[:, :ph], "qghp"), -dia(dcd[:, ph:], "qghp"))


def _ssm_fwd(proj, bbd, cdm, a8, dskip, bl, name, tm=1024, comm=None):
    n = proj.shape[0]
    nc, ch, p2 = bbd.shape
    ph = p2 // 2
    nt = n // tm
    nsub = 8 // bl

    def body(u_ref, bb_ref, cd_ref, a_ref, d_ref, ypre_ref, yg_ref, s_ref, bu, carry):
        t = pl.program_id(1)

        @pl.when(t == 0)
        def _():
            carry[...] = jnp.zeros_like(carry)

        u = u_ref[...]
        bu[...] = jnp.dot(u, bb_ref[0], preferred_element_type=F32)
        a_re, a_im = a_ref[:, :ph], a_ref[:, ph:]
        row = lax.broadcasted_iota(jnp.int32, (8, ph), 0)

        def step(k, c):
            cre, cim = c
            r0 = pl.multiple_of(k * 8, 8)
            bre, bim = bu[pl.ds(r0, 8), :ph], bu[pl.ds(r0, 8), ph:]
            sre, sim = cre, cim
            for sub in range(nsub):
                xre, xim = pltpu.roll(cre, bl, 0), pltpu.roll(cim, bl, 0)
                cre = a_re * xre - a_im * xim + bre
                cim = a_re * xim + a_im * xre + bim
                if sub == 0:
                    sre, sim = cre, cim
                else:
                    sel = row >= sub * bl
                    sre, sim = jnp.where(sel, cre, sre), jnp.where(sel, cim, sim)
            bu[pl.ds(r0, 8), :ph] = sre
            bu[pl.ds(r0, 8), ph:] = sim
            return sre, sim

        cre, cim = lax.fori_loop(0, tm // 8, step, (carry[:, :ph], carry[:, ph:]))
        carry[:, :ph] = cre
        carry[:, ph:] = cim
        s16 = bu[...].astype(BF16)
        s_ref[...] = s16
        y = jnp.dot(s16, cd_ref[0], preferred_element_type=F32) + d_ref[...] * u.astype(F32)
        ypre_ref[...] = y
        yg_ref[...] = _gelu(y).astype(yg_ref.dtype)

    in_specs = [pl.BlockSpec((tm, ch), lambda q, t: (t, q)), pl.BlockSpec((1, ch, p2), lambda q, t: (q, 0, 0)),
                pl.BlockSpec((1, p2, ch), lambda q, t: (q, 0, 0)), pl.BlockSpec((8, p2), lambda q, t: (q, 0)),
                pl.BlockSpec((1, ch), lambda q, t: (0, q))]
    out_specs = [pl.BlockSpec((tm, ch), lambda q, t: (t, q)), pl.BlockSpec((tm, ch), lambda q, t: (t, q)),
                 pl.BlockSpec((tm, p2), lambda q, t: (t, q))]
    out_shape = [jax.ShapeDtypeStruct((n, nc * ch), F32), jax.ShapeDtypeStruct((n, nc * ch), BF16),
                 jax.ShapeDtypeStruct((n, nc * p2), BF16)]
    return _pcall(body, name, (nc, nt), in_specs, out_specs, out_shape, [pltpu.VMEM((tm, p2), F32), pltpu.VMEM((8, p2), F32)],
                  ("parallel", "arbitrary"), [proj, bbd, cdm, a8, dskip], comm)


def _ssm_bwd(dyg, ypre, proj, s_all, cdt, bbt, a8, dskip, bl, name, tm=1024, comm=None):
    n = proj.shape[0]
    nc, ch, p2 = cdt.shape
    ph = p2 // 2
    nt = n // tm
    nsub = 8 // bl
    tn_dims = (((0,), (0,)), ((), ()))

    def body(dyg_ref, ypre_ref, u_ref, s_ref, cdt_ref, bbt_ref, a_ref, d_ref,
             du_ref, dbb_ref, dcd_ref, da_ref, dd_ref, ds, s32, carry):
        t = pl.program_id(1)

        @pl.when(t == 0)
        def _():
            carry[...] = jnp.zeros_like(carry)
            dbb_ref[...] = jnp.zeros_like(dbb_ref)
            dcd_ref[...] = jnp.zeros_like(dcd_ref)
            da_ref[...] = jnp.zeros_like(da_ref)
            dd_ref[...] = jnp.zeros_like(dd_ref)

        dyp = dyg_ref[...].astype(F32) * _gelu_grad(ypre_ref[...])
        u = u_ref[...]
        dd_ref[...] += _colsum(dyp * u.astype(F32))
        dyp16 = dyp.astype(BF16)
        ds[...] = jnp.dot(dyp16, cdt_ref[0], preferred_element_type=F32)
        s16 = s_ref[...]
        s32[...] = s16.astype(F32)
        a_re, a_im = a_ref[:, :ph], a_ref[:, ph:]
        row = lax.broadcasted_iota(jnp.int32, (8, ph), 0)
        back = 8 - bl

        def step(kk, c):
            lre, lim, acr, aci = c
            r0 = pl.multiple_of((tm // 8 - 1 - kk) * 8, 8)
            dre, dim = ds[pl.ds(r0, 8), :ph], ds[pl.ds(r0, 8), ph:]
            sre, sim = s32[pl.ds(r0, 8), :ph], s32[pl.ds(r0, 8), ph:]
            ore, oim, ire, iim = lre, lim, lre, lim
            for sub in range(nsub - 1, -1, -1):
                xre, xim = pltpu.roll(lre, back, 0), pltpu.roll(lim, back, 0)
                lre = a_re * xre + a_im * xim + dre
                lim = a_re * xim - a_im * xre + dim
                if sub == nsub - 1:
                    ore, oim, ire, iim = lre, lim, xre, xim
                else:
                    sel = row < (sub + 1) * bl
                    ore, oim = jnp.where(sel, lre, ore), jnp.where(sel, lim, oim)
                    ire, iim = jnp.where(sel, xre, ire), jnp.where(sel, xim, iim)
            ds[pl.ds(r0, 8), :ph] = ore
            ds[pl.ds(r0, 8), ph:] = oim
            acr = acr + sre * ire + sim * iim
            aci = aci + sre * iim - sim * ire
            return ore, oim, acr, aci

        z = jnp.zeros((8, ph), F32)
        lre, lim, acr, aci = lax.fori_loop(0, tm // 8, step, (carry[:, :ph], carry[:, ph:], z, z))
        carry[:, :ph] = lre
        carry[:, ph:] = lim
        da_ref[:, :ph] += acr
        da_ref[:, ph:] += aci
        lam16 = ds[...].astype(BF16)
        du = jnp.dot(lam16, bbt_ref[0], preferred_element_type=F32) + d_ref[...] * dyp
        du_ref[...] = du.astype(du_ref.dtype)
        dbb_ref[0] += lax.dot_general(lam16, u, tn_dims, preferred_element_type=F32)
        dcd_ref[0] += lax.dot_general(s16, dyp16, tn_dims, preferred_element_type=F32)

    rev = lambda q, t: (nt - 1 - t, q)
    in_specs = [pl.BlockSpec((tm, ch), rev), pl.BlockSpec((tm, ch), rev), pl.BlockSpec((tm, ch), rev),
                pl.BlockSpec((tm, p2), rev), pl.BlockSpec((1, ch, p2), lambda q, t: (q, 0, 0)),
                pl.BlockSpec((1, p2, ch), lambda q, t: (q, 0, 0)), pl.BlockSpec((8, p2), lambda q, t: (q, 0)),
                pl.BlockSpec((1, ch), lambda q, t: (0, q))]
    out_specs = [pl.BlockSpec((tm, ch), rev), pl.BlockSpec((1, p2, ch), lambda q, t: (q, 0, 0)),
                 pl.BlockSpec((1, p2, ch), lambda q, t: (q, 0, 0)), pl.BlockSpec((8, p2), lambda q, t: (q, 0)),
                 pl.BlockSpec((1, ch), lambda q, t: (0, q))]
    out_shape = [jax.ShapeDtypeStruct((n, nc * ch), BF16), jax.ShapeDtypeStruct((nc, p2, ch), F32),
                 jax.ShapeDtypeStruct((nc, p2, ch), F32), jax.ShapeDtypeStruct((nc * 8, p2), F32),
                 jax.ShapeDtypeStruct((1, nc * ch), F32)]
    return _pcall(body, name, (nc, nt), in_specs, out_specs, out_shape,
                  [pltpu.VMEM((tm, p2), F32), pltpu.VMEM((tm, p2), F32), pltpu.VMEM((8, p2), F32)],
                  ("parallel", "arbitrary"), [dyg, ypre, proj, s_all, cdt, bbt, a8, dskip], comm)


_MESH = pl.DeviceIdType.MESH
_HBM = pl.BlockSpec(memory_space=pltpu.HBM)


def _position():
    return lax.axis_index("x"), lax.axis_index("y"), lax.axis_index("c")


def _other_chips(x, y):
    return [((1 - x, y), 2 * (1 - x) + y), ((x, 1 - y), 2 * x + 1 - y), ((1 - x, 1 - y), 2 * (1 - x) + 1 - y)]


def _swap_sibling(v, name):
    def body(v_ref, got_ref, send_sem, recv_sem):
        x, y, c = _position()
        cp = pltpu.make_async_remote_copy(src_ref=v_ref, dst_ref=got_ref, send_sem=send_sem, recv_sem=recv_sem,
                                          device_id=(x, y, 1 - c), device_id_type=_MESH)
        cp.start()
        cp.wait()

    return pl.pallas_call(
        body, name=name, in_specs=[_HBM], out_specs=_HBM, out_shape=jax.ShapeDtypeStruct(v.shape, v.dtype),
        scratch_shapes=[pltpu.SemaphoreType.DMA, pltpu.SemaphoreType.DMA])(v)


def _own_slot(gathered, own):
    return lax.dynamic_update_index_in_dim(gathered, own, _chip_index(), 0)


def _chip_allgather(v, name):
    def body(v_ref, out_ref, send_sems, recv_sems):
        x, y, c = _position()
        me = 2 * x + y
        sends = []
        for k, (chip, idx) in enumerate(_other_chips(x, y)):
            cp = pltpu.make_async_remote_copy(src_ref=v_ref, dst_ref=out_ref.at[me], send_sem=send_sems.at[k],
                                              recv_sem=recv_sems.at[k], device_id=(*chip, c), device_id_type=_MESH)
            cp.start()
            sends.append(cp)
        for k, (chip, idx) in enumerate(_other_chips(x, y)):
            pltpu.make_async_remote_copy(src_ref=v_ref, dst_ref=out_ref.at[idx], send_sem=send_sems.at[k],
                                         recv_sem=recv_sems.at[k], device_id=(*chip, c), device_id_type=_MESH).wait_recv()
        for cp in sends:
            cp.wait_send()

    out = pl.pallas_call(
        body, name=name, in_specs=[_HBM], out_specs=_HBM, out_shape=jax.ShapeDtypeStruct((4,) + tuple(v.shape), v.dtype),
        scratch_shapes=[pltpu.SemaphoreType.DMA((3,)), pltpu.SemaphoreType.DMA((3,))])(v)
    return _own_slot(out, v)


def _remote(src, dst, send_sems, recv_sems, s, device):
    return pltpu.make_async_remote_copy(src_ref=src, dst_ref=dst, send_sem=send_sems.at[s], recv_sem=recv_sems.at[s],
                                        device_id=device, device_id_type=_MESH)


class _Exchange:
    def __init__(self, ins, out_shapes, n_sems, aliases=None):
        self.ins, self.out_shapes, self.n_sems, self.aliases = list(ins), list(out_shapes), n_sems, aliases or {}

    def sem_shapes(self):
        return [pltpu.SemaphoreType.DMA((self.n_sems,)), pltpu.SemaphoreType.DMA((self.n_sems,))]


def _halves(ref, c, axis=0):
    h = ref.shape[axis] // 2
    idx = (slice(None),) * axis
    return ref.at[idx + (pl.ds(c * h, h),)], ref.at[idx + (pl.ds((1 - c) * h, h),)]


class _GatherShards(_Exchange):
    def __init__(self, ws):
        super().__init__(ws, [jax.ShapeDtypeStruct((N_CHIPS,) + tuple(w.shape), w.dtype) for w in ws], 6 * len(ws))

    def start(self, w_refs, out_refs, sems):
        send_sems, recv_sems = sems
        x, y, c = _position()
        me = 2 * x + y
        for i, (w, out) in enumerate(zip(w_refs, out_refs)):
            for k, (chip, idx) in enumerate(_other_chips(x, y)):
                _remote(_halves(w, c)[0], _halves(out.at[me], c)[0], send_sems, recv_sems, 6 * i + k, (*chip, c)).start()

    def finish(self, w_refs, out_refs, sems):
        send_sems, recv_sems = sems
        x, y, c = _position()
        sibling = (x, y, 1 - c)
        others = _other_chips(x, y)
        for i, out in enumerate(out_refs):
            for k, (chip, idx) in enumerate(others):
                landed = _halves(out.at[idx], c)[0]
                _remote(landed, landed, send_sems, recv_sems, 6 * i + k, (*chip, c)).wait_recv()
                _remote(landed, landed, send_sems, recv_sems, 6 * i + 3 + k, sibling).start()
        for i, (w, out) in enumerate(zip(w_refs, out_refs)):
            for k, (chip, idx) in enumerate(others):
                mine, theirs = _halves(out.at[idx], c)
                _remote(theirs, theirs, send_sems, recv_sems, 6 * i + 3 + k, sibling).wait_recv()
                _remote(mine, mine, send_sems, recv_sems, 6 * i + 3 + k, sibling).wait_send()
                _remote(_halves(w, c)[0], mine, send_sems, recv_sems, 6 * i + k, (*chip, c)).wait_send()


class _SwapHalves(_Exchange):
    def __init__(self, gs):
        shapes = [jax.ShapeDtypeStruct((g.shape[0], g.shape[1] // 2) + tuple(g.shape[2:]), g.dtype) for g in gs]
        super().__init__(gs, shapes, N_CHIPS * len(gs))

    def _copies(self, g_refs, out_refs, sems):
        x, y, c = _position()
        return [_remote(_halves(g.at[j], c)[1], out.at[j], sems[0], sems[1], N_CHIPS * i + j, (x, y, 1 - c))
                for i, (g, out) in enumerate(zip(g_refs, out_refs)) for j in range(N_CHIPS)]

    def start(self, g_refs, out_refs, sems):
        for cp in self._copies(g_refs, out_refs, sems):
            cp.start()

    def finish(self, g_refs, out_refs, sems):
        for cp in self._copies(g_refs, out_refs, sems):
            cp.wait()


class _ScatterPieces(_Exchange):
    def __init__(self, ps):
        super().__init__(ps, [jax.ShapeDtypeStruct((3,) + tuple(p.shape[1:]), p.dtype) for p in ps], 3 * len(ps))

    def _copies(self, p_refs, out_refs, sems):
        x, y, c = _position()
        return [_remote(p.at[idx], out.at[k], sems[0], sems[1], 3 * i + k, (*chip, c))
                for i, (p, out) in enumerate(zip(p_refs, out_refs)) for k, (chip, idx) in enumerate(_other_chips(x, y))]

    def start(self, p_refs, out_refs, sems):
        for cp in self._copies(p_refs, out_refs, sems):
            cp.start()

    def finish(self, p_refs, out_refs, sems):
        for cp in self._copies(p_refs, out_refs, sems):
            cp.wait()


class _JoinHalves(_Exchange):
    def __init__(self, rs):
        super().__init__(rs, [jax.ShapeDtypeStruct(r.shape, r.dtype) for r in rs], len(rs), {i: i for i in range(len(rs))})

    def start(self, r_refs, out_refs, sems):
        x, y, c = _position()
        for i, out in enumerate(out_refs):
            _remote(out.at[c], out.at[c], sems[0], sems[1], i, (x, y, 1 - c)).start()

    def finish(self, r_refs, out_refs, sems):
        x, y, c = _position()
        for i, out in enumerate(out_refs):
            _remote(out.at[c], out.at[c], sems[0], sems[1], i, (x, y, 1 - c)).wait_send()
            _remote(out.at[1 - c], out.at[1 - c], sems[0], sems[1], i, (x, y, 1 - c)).wait_recv()


def _run_exchange(ex, name):
    def body(*refs):
        ins, outs, sems = refs[:len(ex.ins)], refs[len(ex.ins):len(ex.ins) + len(ex.out_shapes)], refs[-2:]
        ex.start(ins, outs, sems)
        ex.finish(ins, outs, sems)

    return pl.pallas_call(body, name=name, in_specs=[_HBM] * len(ex.ins), out_specs=[_HBM] * len(ex.out_shapes),
                          out_shape=ex.out_shapes, scratch_shapes=ex.sem_shapes(), input_output_aliases=ex.aliases)(*ex.ins)


def _pcall(body, name, grid, in_specs, out_specs, out_shape, scratch_shapes, semantics, args, comm=None):
    if comm is None:
        return pl.pallas_call(body, name=name, grid=grid, in_specs=in_specs, out_specs=out_specs, out_shape=out_shape,
                              scratch_shapes=scratch_shapes, compiler_params=_cparams(semantics))(*args)
    n_in, n_out, n_scr, ci, co = len(in_specs), len(out_specs), len(scratch_shapes), len(comm.ins), len(comm.out_shapes)

    def wrapped(*refs):
        parts, a = [], 0
        for k in (n_in, ci, n_out, co, n_scr, 2):
            parts.append(refs[a:a + k])
            a += k
        ins, cins, outs, couts, scr, sems = parts
        ids = [pl.program_id(i) for i in range(len(grid))]
        first = functools.reduce(jnp.logical_and, [i == 0 for i in ids])
        last = functools.reduce(jnp.logical_and, [i == g - 1 for i, g in zip(ids, grid)])

        @pl.when(first)
        def _():
            comm.start(cins, couts, sems)

        body(*ins, *outs, *scr)

        @pl.when(last)
        def _():
            comm.finish(cins, couts, sems)

    res = pl.pallas_call(
        wrapped, name=name, grid=grid, in_specs=list(in_specs) + [_HBM] * ci, out_specs=list(out_specs) + [_HBM] * co,
        out_shape=list(out_shape) + comm.out_shapes, scratch_shapes=list(scratch_shapes) + comm.sem_shapes(),
        compiler_params=_cparams(("arbitrary",) * len(grid)))(*args, *comm.ins)
    return res[:n_out], res[n_out:]


ATT_WINDOW = 128
PHASES = 16
_NT = (((1,), (1,)), ((), ()))
_TN = (((0,), (0,)), ((), ()))


PERM_LANES = 512


def _phase_perm(bl):
    t = 16 * PHASES * bl
    col = jnp.arange(t)
    i, r, b = col // (PHASES * bl), (col // bl) % PHASES, col % bl
    return (jnp.arange(t)[:, None] == ((b * PHASES + r) * 16 + i)[None, :]).astype(BF16)


def _to_phase_order(x, bl, col0=0, width=None):
    n = x.shape[0]
    width = width or x.shape[1]
    t = 16 * PHASES * bl
    g = n // bl // PHASES
    tn = min(PERM_LANES, width)

    def body(p_ref, x_ref, o_ref):
        o_ref[...] = jnp.dot(p_ref[...], x_ref[...], preferred_element_type=F32).astype(o_ref.dtype).reshape(o_ref.shape)

    out = pl.pallas_call(
        body, name="to_phase", grid=(n // t, width // tn),
        in_specs=[pl.BlockSpec((t, t), lambda i, j: (0, 0)), pl.BlockSpec((t, tn), lambda i, j: (i, col0 // tn + j))],
        out_specs=pl.BlockSpec((bl * PHASES, 16, tn), lambda i, j: (0, i, j)),
        out_shape=jax.ShapeDtypeStruct((bl * PHASES, g, width), x.dtype),
        compiler_params=_cparams(("parallel", "parallel")))(_phase_perm(bl), x)
    return out.reshape(n, width)


def _from_phase_order(y, bl):
    n, width = y.shape
    t = 16 * PHASES * bl
    g = n // bl // PHASES
    tn = min(PERM_LANES, width)

    def body(p_ref, y_ref, o_ref):
        o_ref[...] = jnp.dot(p_ref[...], y_ref[...].reshape(t, tn), preferred_element_type=F32).astype(o_ref.dtype)

    return pl.pallas_call(
        body, name="from_phase", grid=(n // t, width // tn),
        in_specs=[pl.BlockSpec((t, t), lambda i, j: (0, 0)), pl.BlockSpec((bl * PHASES, 16, tn), lambda i, j: (0, i, j))],
        out_specs=pl.BlockSpec((t, tn), lambda i, j: (i, j)), out_shape=jax.ShapeDtypeStruct((n, width), y.dtype),
        compiler_params=_cparams(("parallel", "parallel")))(_phase_perm(bl).T, y.reshape(bl * PHASES, g, width))


def _att_geometry(p, n, bl):
    g = n // bl // PHASES
    if p == 0:
        return ((bl, PHASES, g), (bl, g // 16), (None, PHASES, 16),
                lambda sh: (lambda b, a: (b, 0, jnp.maximum(a + sh, 0))), 256, 16, lambda ids: ids[1] == 0)
    if p == 1:
        return ((bl, 4, 4, g), (bl, 2, g // 32), (None, 4, 2, 32),
                lambda sh: (lambda b, r, a: (b, 0, r, jnp.maximum(a + sh, 0))), 128, 32, lambda ids: ids[2] == 0)
    return ((bl * PHASES, g), (bl * PHASES // 2,), (2, g), lambda sh: (lambda s: (s, 0)), g, g, None)


def _att_units(p, n, bl):
    g = n // bl // PHASES
    full = slice(None)
    if p == 0:
        return [(full, full)], (PHASES, 16)
    if p == 1:
        return [(full, u, full) for u in range(2)], (4, 32)
    return [(u, full) for u in range(2)], (g,)


def _att_masks(p, qb, chunk):
    def pos(idx):
        return (idx % chunk) * (qb // chunk) + idx // chunk

    dq = pos(lax.broadcasted_iota(jnp.int32, (qb, qb), 0))
    dk = pos(lax.broadcasted_iota(jnp.int32, (qb, qb), 1))
    dist = dq - dk
    return jnp.logical_and(dist >= 0, dist <= ATT_WINDOW), dist + qb <= ATT_WINDOW


def _att_call(p, n, bl, c, body, name, ins, outs):
    prefix, grid, blk, idx_fn, qb, chunk, _ = _att_geometry(p, n, bl)

    def spec(cb, sh):
        f = idx_fn(sh)
        return pl.BlockSpec(blk + (c,), lambda *ids, f=f, cb=cb: f(*ids) + (cb,))

    in_specs = [spec(cb, sh) for _, cb, sh in ins]
    out_specs = [spec(0, 0) for _ in outs]
    out_shape = [jax.ShapeDtypeStruct(prefix + (c,), dt) for dt in outs]
    res = pl.pallas_call(body, name=name, grid=grid, in_specs=in_specs, out_specs=out_specs, out_shape=out_shape,
                         compiler_params=_cparams(("parallel",) * len(grid)))(*[a.reshape(prefix + (a.shape[1],)) for a, _, _ in ins])
    return [r.reshape(n, c) for r in res]


class _AttTiles:
    def __init__(self, p, n, bl, c, first):
        _, _, _, _, qb, chunk, _ = _att_geometry(p, n, bl)
        units, self.unit_shape = _att_units(p, n, bl)
        self.split = p == 0
        self.rows = qb // 2 if self.split else qb
        halves = (0, 1) if self.split else (None,)
        self.tiles = [(u, pl.ds(lt * 128, 128), h) for u in units for lt in range(c // 128) for h in halves]
        self.mask_cur, mp = _att_masks(p, self.rows, chunk // 2 if self.split else chunk)
        gated = mp if first is None else jnp.logical_and(mp, jnp.logical_not(first))
        self.mask_prev = [mp if h == 1 else gated for _, _, h in self.tiles]

    def _half(self, x, h):
        return x.astype(F32)[:, 8 * h:8 * h + 8, :].reshape(self.rows, 128).astype(x.dtype)

    def cur(self, ref, t):
        u, ls, h = self.tiles[t]
        x = ref[u + (ls,)]
        return x.reshape(self.rows, 128) if h is None else self._half(x, h)

    def prev(self, cur_ref, prev_ref, t):
        u, ls, h = self.tiles[t]
        if h is None:
            return prev_ref[u + (ls,)].reshape(self.rows, 128)
        return self._half(prev_ref[u + (ls,)], 1) if h == 0 else self._half(cur_ref[u + (ls,)], 0)

    def store(self, ref, vals):
        if not self.split:
            for (u, ls, _), v in zip(self.tiles, vals):
                ref[u + (ls,)] = v.astype(ref.dtype).reshape(self.unit_shape + (128,))
            return
        for k in range(len(self.tiles) // 2):
            u, ls, _ = self.tiles[2 * k]
            parts = [v.astype(F32).reshape(self.unit_shape[0], 8, 128) for v in vals[2 * k:2 * k + 2]]
            ref[u + (ls,)] = jnp.concatenate(parts, axis=1).astype(ref.dtype)

    def fold_keys(self, cur_vals, prev_vals):
        if not self.split:
            return cur_vals, prev_vals
        own, before = [], []
        for k in range(len(self.tiles) // 2):
            own += [cur_vals[2 * k] + prev_vals[2 * k + 1], cur_vals[2 * k + 1]]
            before += [jnp.zeros_like(prev_vals[2 * k]), prev_vals[2 * k]]
        return own, before


def _att_fwd(p, qkv, qcb, bl, c, heads):
    n = qkv.shape[0]
    _, grid, _, _, _, _, first_fn = _att_geometry(p, n, bl)
    n_grid = len(grid)
    has_prev = first_fn is not None
    e = c // heads
    scale = e ** -0.5

    def body(*refs):
        if has_prev:
            q_ref, kc_ref, kp_ref, vc_ref, vp_ref, o_ref, l_ref = refs
        else:
            q_ref, kc_ref, vc_ref, o_ref, l_ref = refs
            kp_ref = vp_ref = None
        tl = _AttTiles(p, n, bl, c, first_fn([pl.program_id(a) for a in range(n_grid)]) if has_prev else None)
        mc, n_t = tl.mask_cur, len(tl.tiles)
        lo = lax.broadcasted_iota(jnp.int32, (tl.rows, 128), 1) < e
        ones = jnp.ones((tl.rows, 128), BF16)
        items = [(t, h) for t in range(n_t) for h in range(2)]
        dot = functools.partial(jnp.dot, preferred_element_type=F32)
        q2 = [tl.cur(q_ref, t) for t in range(n_t)]
        kc = [tl.cur(kc_ref, t) for t in range(n_t)]
        qm = [jnp.where(lo if h == 0 else jnp.logical_not(lo), q2[t], jnp.zeros_like(q2[t])) for t, h in items]
        sc = [jnp.where(mc, lax.dot_general(qm[i], kc[t], _NT, preferred_element_type=F32) * scale, -jnp.inf)
              for i, (t, h) in enumerate(items)]
        m = [jnp.max(s, axis=1, keepdims=True) for s in sc]
        if has_prev:
            kp = [tl.prev(kc_ref, kp_ref, t) for t in range(n_t)]
            sp = [jnp.where(tl.mask_prev[t], lax.dot_general(qm[i], kp[t], _NT, preferred_element_type=F32) * scale, -jnp.inf)
                  for i, (t, h) in enumerate(items)]
            m = [jnp.maximum(a, jnp.max(s, axis=1, keepdims=True)) for a, s in zip(m, sp)]
        pc = [jnp.exp(s - a).astype(BF16) for s, a in zip(sc, m)]
        vc = [tl.cur(vc_ref, t) for t in range(n_t)]
        acc = [dot(pc[i], vc[t]) for i, (t, h) in enumerate(items)]
        den = [dot(x, ones) for x in pc]
        if has_prev:
            pp = [jnp.exp(s - a).astype(BF16) for s, a in zip(sp, m)]
            vp = [tl.prev(vc_ref, vp_ref, t) for t in range(n_t)]
            acc = [a + dot(pp[i], vp[t]) for i, ((t, h), a) in enumerate(zip(items, acc))]
            den = [d + dot(x, ones) for d, x in zip(den, pp)]
        oh = [a / d for a, d in zip(acc, den)]
        lh = [a + jnp.log(d) for a, d in zip(m, den)]
        tl.store(o_ref, [jnp.where(lo, oh[2 * t], oh[2 * t + 1]) for t in range(n_t)])
        tl.store(l_ref, [jnp.where(lo, lh[2 * t], lh[2 * t + 1]) for t in range(n_t)])

    kcb, vcb = 3, 4
    ins = [(qkv, qcb, 0), (qkv, kcb, 0)] + ([(qkv, kcb, -1)] if has_prev else []) + [(qkv, vcb, 0)] + ([(qkv, vcb, -1)] if has_prev else [])
    return _att_call(p, n, bl, c, body, name=f"att_fwd{p}", ins=ins, outs=[BF16, F32])


def _att_bwd(p, qkv, qcb, o, do, lse, bl, c, heads):
    n = qkv.shape[0]
    _, grid, _, _, _, _, first_fn = _att_geometry(p, n, bl)
    n_grid = len(grid)
    has_prev = first_fn is not None
    e = c // heads
    scale = e ** -0.5

    def body(*refs):
        if has_prev:
            q_ref, kc_ref, kp_ref, vc_ref, vp_ref, o_ref, do_ref, l_ref, dq_ref, dkc_ref, dkp_ref, dvc_ref, dvp_ref = refs
        else:
            q_ref, kc_ref, vc_ref, o_ref, do_ref, l_ref, dq_ref, dkc_ref, dvc_ref = refs
            kp_ref = vp_ref = None
        tl = _AttTiles(p, n, bl, c, first_fn([pl.program_id(a) for a in range(n_grid)]) if has_prev else None)
        mc, n_t = tl.mask_cur, len(tl.tiles)
        lo = lax.broadcasted_iota(jnp.int32, (tl.rows, 128), 1) < e
        items = [(t, h) for t in range(n_t) for h in range(2)]
        nt_dot = lambda a, b: lax.dot_general(a, b, _NT, preferred_element_type=F32)
        tn_dot = lambda a, b: lax.dot_general(a, b, _TN, preferred_element_type=F32)
        dot = functools.partial(jnp.dot, preferred_element_type=F32)
        sel = [lo if h == 0 else jnp.logical_not(lo) for t, h in items]
        q2, kc, vc, do2 = ([tl.cur(r, t) for t in range(n_t)] for r in (q_ref, kc_ref, vc_ref, do_ref))
        qm = [jnp.where(sel[i], q2[t], jnp.zeros_like(q2[t])) for i, (t, h) in enumerate(items)]
        dom = [jnp.where(sel[i], do2[t], jnp.zeros_like(do2[t])) for i, (t, h) in enumerate(items)]
        dod = [do2[t].astype(F32) * tl.cur(o_ref, t).astype(F32) for t in range(n_t)]
        lcol = [tl.cur(l_ref, t)[:, h * e:h * e + 1] for t, h in items]
        corr = [-jnp.sum(jnp.where(sel[i], dod[t], 0.0), axis=1, keepdims=True) for i, (t, h) in enumerate(items)]
        pc = [jnp.exp(jnp.where(mc, nt_dot(qm[i], kc[t]) * scale, -jnp.inf) - lcol[i]) for i, (t, h) in enumerate(items)]
        dsc = [(pc[i] * (nt_dot(dom[i], vc[t]) + corr[i]) * scale).astype(BF16) for i, (t, h) in enumerate(items)]
        pc = [x.astype(BF16) for x in pc]
        dq = [dot(dsc[i], kc[t]) for i, (t, h) in enumerate(items)]
        dkc = [tn_dot(dsc[2 * t], qm[2 * t]) + tn_dot(dsc[2 * t + 1], qm[2 * t + 1]) for t in range(n_t)]
        dvc = [tn_dot(pc[2 * t], dom[2 * t]) + tn_dot(pc[2 * t + 1], dom[2 * t + 1]) for t in range(n_t)]
        if has_prev:
            kp = [tl.prev(kc_ref, kp_ref, t) for t in range(n_t)]
            vp = [tl.prev(vc_ref, vp_ref, t) for t in range(n_t)]
            pp = [jnp.exp(jnp.where(tl.mask_prev[t], nt_dot(qm[i], kp[t]) * scale, -jnp.inf) - lcol[i]) for i, (t, h) in enumerate(items)]
            dsp = [(pp[i] * (nt_dot(dom[i], vp[t]) + corr[i]) * scale).astype(BF16) for i, (t, h) in enumerate(items)]
            pp = [x.astype(BF16) for x in pp]
            dq = [a + dot(dsp[i], kp[t]) for i, ((t, h), a) in enumerate(zip(items, dq))]
            dkp = [tn_dot(dsp[2 * t], qm[2 * t]) + tn_dot(dsp[2 * t + 1], qm[2 * t + 1]) for t in range(n_t)]
            dvp = [tn_dot(pp[2 * t], dom[2 * t]) + tn_dot(pp[2 * t + 1], dom[2 * t + 1]) for t in range(n_t)]
            (dkc, dkp), (dvc, dvp) = tl.fold_keys(dkc, dkp), tl.fold_keys(dvc, dvp)
            tl.store(dkp_ref, dkp)
            tl.store(dvp_ref, dvp)
        tl.store(dq_ref, [jnp.where(lo, dq[2 * t], dq[2 * t + 1]) for t in range(n_t)])
        tl.store(dkc_ref, dkc)
        tl.store(dvc_ref, dvc)

    kcb, vcb = 3, 4
    ins = [(qkv, qcb, 0), (qkv, kcb, 0)] + ([(qkv, kcb, -1)] if has_prev else []) + [(qkv, vcb, 0)] + ([(qkv, vcb, -1)] if has_prev else [])
    ins += [(o, 0, 0), (do, 0, 0), (lse, 0, 0)]
    res = _att_call(p, n, bl, c, body, name=f"att_bwd{p}", ins=ins, outs=[BF16] * (5 if has_prev else 3))
    if has_prev:
        dq, dkc, dkp, dvc, dvp = res
        return dq, dkc, dkp, dvc, dvp
    dq, dkc, dvc = res
    return dq, dkc, None, dvc, None


def _dqkv_to_rows(dqs, dk_parts, dv_parts, bl):
    n, c = dqs[0].shape
    t = 16 * PHASES * bl
    g = n // bl // PHASES
    nb = n // t
    shifts = [_att_geometry(p, n, bl)[5] // 16 for p in range(len(dk_parts))]
    view = lambda a: a.reshape(bl * PHASES, g, c)
    blk = lambda s: pl.BlockSpec((bl * PHASES, 16, c), lambda i, j, s=s: (0, jnp.minimum(i + s, nb - 1), 0))
    ins, specs, layout = [], [], []
    for a in dqs:
        layout.append([(len(ins), 0)])
        ins.append(view(a))
        specs.append(blk(0))
    for parts in (dk_parts, dv_parts):
        terms = []
        for (own, prv), s in zip(parts, shifts):
            terms.append((len(ins), 0))
            ins.append(view(own))
            specs.append(blk(0))
            if prv is not None:
                terms.append((len(ins), s))
                ins.append(view(prv))
                specs.append(blk(s))
        layout.append(terms)

    def body(p_ref, *refs):
        o_ref = refs[-1]
        i, j = pl.program_id(0), pl.program_id(1)
        for col, terms in enumerate(layout):
            @pl.when(j == col)
            def _(terms=terms):
                if len(terms) == 1:
                    x = refs[terms[0][0]][...]
                else:
                    x = 0.0
                    for pos, s in terms:
                        v = refs[pos][...].astype(F32)
                        x = x + (v if s == 0 else jnp.where(i + s < nb, v, 0.0))
                    x = x.astype(BF16)
                o_ref[...] = jnp.dot(p_ref[...], x.reshape(t, c), preferred_element_type=F32).astype(o_ref.dtype)

    return pl.pallas_call(
        body, name="dqkv_to_rows", grid=(nb, len(layout)), in_specs=[pl.BlockSpec((t, t), lambda i, j: (0, 0))] + specs,
        out_specs=pl.BlockSpec((t, c), lambda i, j: (i, j)), out_shape=jax.ShapeDtypeStruct((n, len(layout) * c), BF16),
        compiler_params=_cparams(("parallel", "arbitrary")))(_phase_perm(bl).T, *ins)


def _attention_fwd(proj, c, bl, heads):
    n = proj.shape[0]
    qkv = _to_phase_order(proj, bl, col0=c, width=5 * c)
    outs = [_att_fwd(p, qkv, p, bl, c, heads) for p in range(3)]
    ins = [("row", o, c, 0) for o, _ in outs] + [("row", l, c, 0) for _, l in outs]
    o, lse = _rowwise(_combine_fwd_fn, "comb_fwd", n, ins, [(c, BF16), (c, F32)])
    return _from_phase_order(o, bl), (qkv, o, lse)


def _attention_bwd(do_tb, saved, bl, heads):
    qkv, o, lse = saved
    n, c = do_tb.shape
    do = _to_phase_order(do_tb, bl)
    dqs, dks, dvs = [], [], []
    for p in range(3):
        dq, dkc, dkp, dvc, dvp = _att_bwd(p, qkv, p, o, do, lse, bl, c, heads)
        dqs.append(dq)
        dks.append((dkc, dkp))
        dvs.append((dvc, dvp))
    return _dqkv_to_rows(dqs, dks, dvs, bl)


ATT_HEADS = 8
SSM_GROUPS, SSM_STATE, SSM_GROUP = 32, 64, 16


def _row(v):
    return v.reshape(1, -1)


ROWS_TILE = 128


def _to_rows(x):
    bl, seq, d = x.shape

    def body(x_ref, o_ref):
        o_ref[...] = jnp.stack([x_ref[b] for b in range(bl)], axis=1).reshape(ROWS_TILE * bl, d)

    return pl.pallas_call(
        body, name="to_rows", grid=(seq // ROWS_TILE,), in_specs=[pl.BlockSpec((bl, ROWS_TILE, d), lambda i: (0, i, 0))],
        out_specs=pl.BlockSpec((ROWS_TILE * bl, d), lambda i: (i, 0)), out_shape=jax.ShapeDtypeStruct((seq * bl, d), x.dtype),
        compiler_params=_cparams(("parallel",)))(x)


def _from_rows(y, bl):
    n, d = y.shape
    seq = n // bl

    def body(y_ref, o_ref):
        v = y_ref[...].reshape(ROWS_TILE, bl, d)
        for b in range(bl):
            o_ref[b] = v[:, b, :]

    return pl.pallas_call(
        body, name="from_rows", grid=(seq // ROWS_TILE,), in_specs=[pl.BlockSpec((ROWS_TILE * bl, d), lambda i: (i, 0))],
        out_specs=pl.BlockSpec((bl, ROWS_TILE, d), lambda i: (0, i, 0)), out_shape=jax.ShapeDtypeStruct((bl, seq, d), y.dtype),
        compiler_params=_cparams(("parallel",)))(y)


def _carried(result, carry, key, hidden):
    if carry.get(key) is None:
        return result
    result, hidden[key] = result
    return result


def _layer_fwd(x, w, p, bl, carry, late=None):
    n, d = x.shape
    c = d // 2
    hidden = {}
    h, = _rowwise(_rms_fwd_fn, "rms_fwd", n, [("row", x, d, 0), ("par", _row(p["norm1_g"]))], [(d, BF16)])
    proj = _carried(_mm(h, w["w_in"], "nn", BF16, "mm_in", comm=carry.get("mm_in")), carry, "mm_in", hidden)
    if late is not None:
        w = dict(w, **late(hidden["mm_in"]))
    disc, disc_vjp = jax.vjp(_ssm_disc, p["ssm_lambda_re"], p["ssm_lambda_im"], p["ssm_log_dt"], p["ssm_b_re"], p["ssm_b_im"])
    bbd, cdm, a8 = _ssm_pack(*disc, p["ssm_c_re"], p["ssm_c_im"])
    ypre, yg, s_all = _carried(_ssm_fwd(proj, bbd, cdm, a8, _row(p["ssm_d"]), bl, "ssm_fwd", comm=carry.get("ssm_fwd")),
                               carry, "ssm_fwd", hidden)
    zs = _mm(yg, w["w_ssm_glu"], "nn", BF16, "mm_glu")
    o, att = _attention_fwd(proj, c, bl, ATT_HEADS)
    ya = _mm(o, w["w_att_up"], "nn", BF16, "mm_att")
    w32 = jnp.concatenate([p["conv_w"], jnp.zeros((1, c), F32)], axis=0)
    hc, hconv = _conv_fwd(proj, 6, w32, _row(p["conv_b"]), _row(p["conv_ln_g"]), _row(p["conv_ln_b"]), bl, c, "conv_fwd")
    yc = _mm(hc, w["w_conv_pw2"], "nn", BF16, "mm_pw2")
    gates = [("row", proj, d, 4), ("row", proj, d, 5), ("row", proj, d, 6), ("par", _row(p["b_gate"]))]
    branches = [("row", zs, 2 * d, 0), ("row", ya, d, 0), ("row", yc, d, 0)]
    merged, = _rowwise(_merge_fwd_fn, "merge_fwd", n, gates + branches, [(d, BF16)])
    xm = _mm(merged, w["w_out"], "nn", F32, "mm_out", res=x)
    h2, = _rowwise(_rms_fwd_fn, "rms_fwd", n, [("row", xm, d, 0), ("par", _row(p["norm2_g"]))], [(d, BF16)])
    z = _carried(_mm(h2, w["w_ffn_in"], "nn", BF16, "mm_ffn_in", comm=carry.get("mm_ffn_in")), carry, "mm_ffn_in", hidden)
    f = z.shape[1] // 2
    a, = _rowwise(_swiglu_fwd_fn, "swiglu_fwd", n, [("row", z, 2 * f, 0)], [(f, BF16)], tm=256)
    xo = _mm(a, w["w_ffn_out"], "nn", F32, "mm_ffn_out", res=xm)
    saved = dict(x=x, h=h, proj=proj, disc_vjp=disc_vjp, bbd=bbd, cdm=cdm, a8=a8, ypre=ypre, yg=yg, s_all=s_all, zs=zs, o=o,
                 att=att, ya=ya, w32=w32, hc=hc, hconv=hconv, yc=yc, gates=gates, branches=branches, merged=merged, xm=xm,
                 h2=h2, z=z, a=a)
    return xo, saved, hidden, w


def _layer_bwd(dxo, s, w, p, bl, carry):
    n, d = dxo.shape
    c = d // 2
    g, bufs, hidden = {}, {}, {}
    f = s["a"].shape[1]

    def dw(key, a, dy, name):
        bufs[key] = _mm_dw(a, dy, name, 1 if key in ROW_SHARDED else N_CHIPS)

    da = _mm(dxo, w["w_ffn_out"], "nt", BF16, "mm_ffn_out_dx")
    dw("w_ffn_out", s["a"], dxo, "mm_ffn_out_dw")
    dz, = _rowwise(_swiglu_bwd_fn, "swiglu_bwd", n, [("row", s["z"], 2 * f, 0), ("row", da, f, 0)], [(2 * f, BF16)], tm=256)
    dh2 = _mm(dz, w["w_ffn_in"], "nt", F32, "mm_ffn_in_dx")
    dw("w_ffn_in", s["h2"], dz, "mm_ffn_in_dw")
    dxm, dg2 = _rowwise(_rms_bwd_fn, "rms_bwd", n, [("row", s["xm"], d, 0), ("par", _row(p["norm2_g"])), ("row", dh2, d, 0),
                                                   ("row", dxo, d, 0)], [(d, F32)], [d])
    g["norm2_g"] = dg2[0]
    dmerged = _mm(dxm, w["w_out"], "nt", BF16, "mm_out_dx")
    dw("w_out", s["merged"], dxm, "mm_out_dw")
    dgl, dzs, dya, dyc, dbg = _rowwise(_merge_bwd_fn, "merge_bwd", n, s["gates"] + s["branches"] + [("row", dmerged, d, 0)],
                                       [(3 * d, BF16), (2 * d, BF16), (d, BF16), (d, BF16)], [3 * d], tm=256)
    g["b_gate"] = dbg[0]
    dyg = _mm(dzs, w["w_ssm_glu"], "nt", BF16, "mm_glu_dx")
    dw("w_ssm_glu", s["yg"], dzs, "mm_glu_dw")
    du, dbb, dcd, dab, dd = _carried(
        _ssm_bwd(dyg, s["ypre"], s["proj"], s["s_all"], s["cdm"].transpose(0, 2, 1), s["bbd"].transpose(0, 2, 1), s["a8"],
                 _row(p["ssm_d"]), bl, "ssm_bwd", comm=carry.get("ssm_bwd")), carry, "ssm_bwd", hidden)
    dab_re, dab_im, dbb_re, dbb_im, g["ssm_c_re"], g["ssm_c_im"] = _ssm_unpack(dbb, dcd, dab, SSM_GROUPS, SSM_STATE, SSM_GROUP)
    (g["ssm_lambda_re"], g["ssm_lambda_im"], g["ssm_log_dt"], g["ssm_b_re"],
     g["ssm_b_im"]) = s["disc_vjp"]((dab_re, dab_im, dbb_re, dbb_im))
    g["ssm_d"] = dd[0]
    do = _mm(dya, w["w_att_up"], "nt", BF16, "mm_att_dx")
    dw("w_att_up", s["o"], dya, "mm_att_dw")
    dqkv = _attention_bwd(do, s["att"], bl, ATT_HEADS)
    dhc = _mm(dyc, w["w_conv_pw2"], "nt", BF16, "mm_pw2_dx")
    dw("w_conv_pw2", s["hc"], dyc, "mm_pw2_dw")
    dcv, dcw, dcb, dlg, dlb = _carried(
        _conv_bwd(s["proj"], 6, dhc, s["hconv"], s["w32"], _row(p["conv_ln_g"]), _row(p["conv_ln_b"]), bl, c, "conv_bwd",
                  comm=carry.get("conv_bwd")), carry, "conv_bwd", hidden)
    g["conv_w"], g["conv_b"], g["conv_ln_g"], g["conv_ln_b"] = dcw, dcb[0], dlg[0], dlb[0]
    dproj = jnp.concatenate([du, dqkv, dcv, dgl], axis=1)
    dh = _mm(dproj, w["w_in"], "nt", F32, "mm_in_dx")
    dw("w_in", s["h"], dproj, "mm_in_dw")
    dx, dg1 = _rowwise(_rms_bwd_fn, "rms_bwd", n, [("row", s["x"], d, 0), ("par", _row(p["norm1_g"])), ("row", dh, d, 0),
                                                  ("row", dxm, d, 0)], [(d, F32)], [d])
    g["norm1_g"] = dg1[0]
    return dx, g, bufs, hidden


WEIGHTS = ['norm1_g', 'w_in', 'b_gate', 'ssm_lambda_re', 'ssm_lambda_im', 'ssm_log_dt', 'ssm_b_re', 'ssm_b_im', 'ssm_c_re',
           'ssm_c_im', 'ssm_d', 'w_ssm_glu', 'w_att_up', 'conv_w', 'conv_b', 'conv_ln_g', 'conv_ln_b', 'w_conv_pw2', 'w_out',
           'norm2_g', 'w_ffn_in', 'w_ffn_out', 'final_g']
BIG = ['w_in', 'w_ssm_glu', 'w_att_up', 'w_conv_pw2', 'w_out', 'w_ffn_in', 'w_ffn_out']
ROW_SHARDED = ('w_out', 'w_ffn_out')
SMALL = [k for k in WEIGHTS if k not in BIG]
LANES = 1024
N_CHIPS = 4
ROW_TILE_BYTES = 36 * 1024 * 1024
MIN_SHARD_TILE = 1024


def _pad_rows(a, rows):
    return jnp.concatenate([a, jnp.zeros((rows - a.shape[0],) + a.shape[1:], a.dtype)], axis=0) if rows > a.shape[0] else a


def _row_tile(rows, width, n_arrays):
    best = 16
    for t in range(16, rows + 1, 16):
        if rows % t == 0 and t * width * 4 * n_arrays * 2 <= ROW_TILE_BYTES:
            best = t
    return best


def _flat_fn(fn, name, ins, n_out, rows):
    return _rowwise(fn, name, rows, [("row", a, LANES, 0) for a in ins], [(LANES, F32)] * n_out, tm=rows)


def _reduce_prepare(bufs):
    landed = _run_exchange(_SwapHalves([b16 for _, b16 in bufs]), "rs_swap")
    kept, p16s = [], []
    for (b32, _), la in zip(bufs, landed):
        s, m, cs = b32.shape
        h = m // 2
        tm = _row_tile(h, cs, 3)
        halves = b32.reshape(s, 2, h, cs)

        def body(g_ref, l_ref, o16):
            o16[...] = (g_ref[...] + l_ref[...].astype(F32)).astype(BF16)

        piece = pl.BlockSpec((None, tm, cs), lambda j, i: (j, i, 0))
        mine = pl.BlockSpec((None, None, tm, cs), lambda j, i: (j, _core_index(), i, 0))
        p16s.append(pl.pallas_call(
            body, name="rs_add", grid=(s, h // tm), in_specs=[mine, piece], out_specs=piece,
            out_shape=jax.ShapeDtypeStruct((s, h, cs), BF16), compiler_params=_cparams(("parallel", "parallel")))(halves, la))
        kept.append((halves, la))
    return kept, p16s


def _reduce_finish(kept, arrived):
    reduced = []
    for (halves, la), lb in zip(kept, arrived):
        _, h, cs = lb.shape
        tm = _row_tile(h, cs, 6)

        def body(g_ref, l_ref, a_ref, b_ref, c_ref, o_ref):
            own = g_ref[...] + l_ref[...].astype(F32)
            o_ref[...] = ((own + a_ref[...].astype(F32)) + b_ref[...].astype(F32)) + c_ref[...].astype(F32)

        mine = pl.BlockSpec((None, None, tm, cs), lambda i: (_chip_index(), _core_index(), i, 0))
        sibling = pl.BlockSpec((None, tm, cs), lambda i: (_chip_index(), i, 0))
        other = [pl.BlockSpec((None, tm, cs), lambda i, k=k: (k, i, 0)) for k in range(3)]
        half = pl.BlockSpec((None, tm, cs), lambda i: (_core_index(), i, 0))
        reduced.append(pl.pallas_call(
            body, name="rs_sum", grid=(h // tm,), in_specs=[mine, sibling] + other, out_specs=half,
            out_shape=jax.ShapeDtypeStruct((2, h, cs), F32), compiler_params=_cparams(("parallel",)))(halves, la, lb, lb, lb))
    joined = _run_exchange(_JoinHalves(reduced), "rs_gather")
    return [j.reshape(2 * j.shape[1], j.shape[2]) for j in joined]


def _adamw_layers(w, g_layers, m, v):
    depth, rows, cs = w.shape
    tm = _row_tile(rows, cs, 8)
    nb = rows // tm

    def body(*refs):
        w_ref, m_ref, v_ref = refs[:3]
        g_refs = refs[3:3 + depth]
        go_ref, d_ref, mo_ref, vo_ref = refs[3 + depth:]
        layer = pl.program_id(0)
        g = g_refs[0][...]
        for l in range(1, depth):
            g = jnp.where(layer == l, g_refs[l][...], g)
        delta, mo, vo = _adamw_fn(w_ref[...], g, m_ref[...], v_ref[...])
        go_ref[...], d_ref[...], mo_ref[...], vo_ref[...] = g, delta, mo, vo

    stacked = pl.BlockSpec((None, tm, cs), lambda l, i: (l, i, 0))
    g_specs = [pl.BlockSpec((tm, cs), lambda l, i, k=k: (jnp.where(l == k, i, jnp.where(l < k, 0, nb - 1)), 0)) for k in range(depth)]
    return pl.pallas_call(
        body, name="adamw", grid=(depth, nb), in_specs=[stacked] * 3 + g_specs, out_specs=[stacked] * 4,
        out_shape=[jax.ShapeDtypeStruct(w.shape, F32)] * 4, compiler_params=_cparams(("arbitrary", "arbitrary")))(w, m, v, *g_layers)


def _sum4_fn(a, b, c, d):
    return (((a.astype(F32) + b.astype(F32)) + c.astype(F32)) + d.astype(F32),)


def _add2_fn(a, b):
    return (a + b,)


def kernel(x, norm1_g, w_in, b_gate, ssm_lambda_re, ssm_lambda_im, ssm_log_dt, ssm_b_re, ssm_b_im, ssm_c_re, ssm_c_im, ssm_d, w_ssm_glu, w_att_up, conv_w, conv_b, conv_ln_g, conv_ln_b, w_conv_pw2, w_out, norm2_g, w_ffn_in, w_ffn_out, final_g, loss_target, m_norm1_g, m_w_in, m_b_gate, m_ssm_lambda_re, m_ssm_lambda_im, m_ssm_log_dt, m_ssm_b_re, m_ssm_b_im, m_ssm_c_re, m_ssm_c_im, m_ssm_d, m_w_ssm_glu, m_w_att_up, m_conv_w, m_conv_b, m_conv_ln_g, m_conv_ln_b, m_w_conv_pw2, m_w_out, m_norm2_g, m_w_ffn_in, m_w_ffn_out, m_final_g, v_norm1_g, v_w_in, v_b_gate, v_ssm_lambda_re, v_ssm_lambda_im, v_ssm_log_dt, v_ssm_b_re, v_ssm_b_im, v_ssm_c_re, v_ssm_c_im, v_ssm_d, v_w_ssm_glu, v_w_att_up, v_conv_w, v_conv_b, v_conv_ln_g, v_conv_ln_b, v_w_conv_pw2, v_w_out, v_norm2_g, v_w_ffn_in, v_w_ffn_out, v_final_g):
    args = dict(locals())
    wts = {k: args[k] for k in WEIGHTS}
    mom = {k: args["m_" + k] for k in WEIGHTS}
    var = {k: args["v_" + k] for k in WEIGHTS}
    bl, seq, d = x.shape
    n = bl * seq
    depth = norm1_g.shape[0]
    cx, cy, cc = _position()
    me = 2 * cx + cy

    assert depth == 2, "the exchanges of layer 1 are hidden behind layer 0's kernels"
    first = BIG[:1]
    rest = BIG[1:]

    shards = lambda keys, l: [wts[k][l].astype(BF16) for k in keys]

    def whole(keys, gathered):
        out = {}
        for k, a in zip(keys, gathered):
            _, ks, cs = a.shape
            if k in ROW_SHARDED:
                out[k] = a.reshape(N_CHIPS * ks, cs)
            elif cs < MIN_SHARD_TILE:
                out[k] = a.transpose(1, 0, 2).reshape(ks, N_CHIPS * cs)
            else:
                out[k] = a
        return out

    fill = lambda gathered, own: [_own_slot(g, o) for g, o in zip(gathered, own)]
    own0 = shards(first, 0) + [conv_w]
    gathered = fill(_run_exchange(_GatherShards(own0), "gather_weights"), own0)
    conv_full = gathered[-1].transpose(1, 2, 0, 3).reshape(depth, CONV_WIDTH, -1)
    params = lambda l: dict({k: wts[k][l] for k in SMALL if k not in ("final_g", "conv_w")}, conv_w=conv_full[l])

    to_rows = _to_rows
    own = {"mm_in": shards(rest, 0), "ssm_fwd": shards(first, 1), "mm_ffn_in": shards(rest, 1)}
    xs, s0, hidden, w0 = _layer_fwd(to_rows(x), whole(first, gathered[:-1]), params(0), bl, {k: _GatherShards(v) for k, v in own.items()},
                                    late=lambda got: whole(rest, fill(got, own["mm_in"])))
    w1 = dict(whole(first, fill(hidden["ssm_fwd"], own["ssm_fwd"])), **whole(rest, fill(hidden["mm_ffn_in"], own["mm_ffn_in"])))
    full = [w0, w1]
    xs, s1, _, _ = _layer_fwd(xs, full[1], params(1), bl, {})
    dx, sq, dgf = _rowwise(_loss_fn, "loss_head", n, [("row", xs, d, 0), ("par", _row(final_g)), ("row", to_rows(loss_target), d, 0)],
                           [(d, F32)], [d, d])
    loss = lax.psum(0.5 * jnp.sum(sq) / d, ("x", "y", "c"))

    pieces = lambda bufs, keys: [tuple(b.reshape(N_CHIPS, -1, b.shape[-1]) for b in bufs[k]) for k in keys]
    dx, g1, bufs1, _ = _layer_bwd(dx, s1, full[1], params(1), bl, {})
    p32_1, p16_1 = _reduce_prepare(pieces(bufs1, BIG))
    dx, g0, bufs0, hidden = _layer_bwd(dx, s0, full[0], params(0), bl,
                                       {"ssm_bwd": _ScatterPieces(p16_1[:1]), "conv_bwd": _ScatterPieces(p16_1[1:])})
    red1 = _reduce_finish(p32_1, list(hidden["ssm_bwd"]) + list(hidden["conv_bwd"]))
    p32_0, p16_0 = _reduce_prepare(pieces(bufs0, BIG))
    red0 = _reduce_finish(p32_0, _run_exchange(_ScatterPieces(p16_0), "rs_scatter"))
    grads = {"final_g": dgf[0]}
    for k in SMALL:
        if k != "final_g":
            grads[k] = jnp.stack([g0[k], g1[k]])
    grad_x = _from_rows(dx, bl)
    outs = {}
    for k, r0, r1 in zip(BIG, red0, red1):
        for tag, a in zip(("grad", "delta", "m", "v"), _adamw_layers(wts[k], [r0, r1], mom[k], var[k])):
            outs[tag, k] = a

    def flat1(t):
        v = jnp.concatenate([t[k].reshape(-1) for k in SMALL])
        rows = -(-v.size // (8 * LANES)) * 8
        return _pad_rows(v, rows * LANES).reshape(rows, LANES), rows

    def unflat1(flat, shapes):
        out, off, v = {}, 0, flat.reshape(-1)
        for k in SMALL:
            size = math.prod(shapes[k])
            out[k] = v[off:off + size].reshape(shapes[k])
            off += size
        return out

    grads["conv_w"] = grads["conv_w"][:, :CONV_WIDTH]
    gs, rows = flat1(grads)
    chip_sum, = _flat_fn(_add2_fn, "ar_add", [gs, _swap_sibling(gs, "ar_swap")], 1, rows)
    slots = _chip_allgather(chip_sum, "ar_gather")
    gs_red, = _flat_fn(_sum4_fn, "ar_sum", [slots[j] for j in range(N_CHIPS)], 1, rows)
    g_sm = unflat1(gs_red, {k: grads[k].shape for k in SMALL})
    cs = conv_w.shape[2]
    g_sm["conv_w"] = lax.dynamic_slice_in_dim(g_sm["conv_w"], me * cs, cs, axis=2)
    (w1, rows), (g1, _), (m1, _), (v1, _) = flat1(wts), flat1(g_sm), flat1(mom), flat1(var)
    sm_out = _flat_fn(_adamw_fn, "adamw_small", [w1, g1, m1, v1], 3, rows)
    shapes = {k: wts[k].shape for k in SMALL}
    for tag, a in zip(("delta", "m", "v"), sm_out):
        for k, t in unflat1(a, shapes).items():
            outs[tag, k] = t
    for k in SMALL:
        outs["grad", k] = g_sm[k]
    return (loss, grad_x, *[outs["grad", k] for k in WEIGHTS], *[outs["delta", k] for k in WEIGHTS],
            *[outs["m", k] for k in WEIGHTS], *[outs["v", k] for k in WEIGHTS])
```

```python
import functools
import math

import jax
import jax.numpy as jnp
from jax import lax
from jax.experimental import pallas as pl
from jax.experimental.pallas import tpu as pltpu

F32 = jnp.float32
BF16 = jnp.bfloat16
VMEM_LIMIT = 56 * 1024 * 1024


def _cparams(sem):
    return pltpu.CompilerParams(dimension_semantics=sem, vmem_limit_bytes=VMEM_LIMIT)


_DIMS = {"nn": (((1,), (0,)), ((), ())), "nt": (((1,), (1,)), ((), ())), "tn": (((0,), (0,)), ((), ()))}


MM_ROWS = 1024
MM_DW_VMEM_BYTES = 44 * 1024 * 1024
MM_SMALL_STEP = 1024 * 1024 * 512


def _div_tile(n, cap):
    best = None
    for t in range(128, min(n, cap) + 1, 128):
        if n % t == 0:
            best = t
    return best or n


def _mm(a, b, form, out_dtype, name, res=None, comm=None):
    sharded = b.ndim == 3
    kdim, cs = b.shape[-2], b.shape[-1]
    s = b.shape[0] if sharded else 1
    m = a.shape[0]
    tm = MM_ROWS if m % MM_ROWS == 0 else _div_tile(m, MM_ROWS)
    if form == "nn":
        n, kd = s * cs, kdim
        tn, tk = _div_tile(cs, 1792), _div_tile(kdim, 2048)
        per = cs // tn
        b_blk = (tk, tn)
        b_idx = (lambda i, j, k: (j // per, k, j % per)) if sharded else (lambda i, j, k: (k, j))
    else:
        n, kd = kdim, s * cs
        tn, tk = _div_tile(kdim, 1408), _div_tile(cs, 1792)
        per = cs // tk
        b_blk = (tn, tk)
        b_idx = (lambda i, j, k: (k // per, j, k % per)) if sharded else (lambda i, j, k: (j, k))
    nk = kd // tk
    if tm * tn * tk <= MM_SMALL_STEP and m % (2 * tm) == 0:
        tm *= 2
    a_spec = pl.BlockSpec((tm, tk), lambda i, j, k: (i, k))
    b_spec = pl.BlockSpec(((None,) + b_blk) if sharded else b_blk, b_idx)
    o_spec = pl.BlockSpec((tm, tn), lambda i, j, k: (i, j))
    dims = _DIMS[form]

    def body(*refs):
        a_ref, b_ref = refs[:2]
        r_ref = refs[2] if res is not None else None
        o_ref = refs[3] if res is not None else refs[2]
        p = lax.dot_general(a_ref[...].astype(BF16), b_ref[...], dims, preferred_element_type=F32)

        def finish(r):
            if r_ref is not None:
                r = r + r_ref[...]
            o_ref[...] = r.astype(out_dtype)

        if nk == 1:
            finish(p)
            return
        acc = refs[-1]
        k = pl.program_id(2)

        @pl.when(k == 0)
        def _():
            acc[...] = p

        @pl.when(k > 0)
        def _():
            acc[...] += p

        @pl.when(k == nk - 1)
        def _():
            finish(acc[...])

    ins = [a, b] + ([] if res is None else [res])
    in_specs = [a_spec, b_spec] + ([] if res is None else [o_spec])
    out = _pcall(body, name, (m // tm, n // tn, nk), in_specs, [o_spec], [jax.ShapeDtypeStruct((m, n), out_dtype)],
                 [pltpu.VMEM((tm, tn), F32)] if nk > 1 else [], ("parallel", "parallel", "arbitrary"), ins, comm)
    return out[0] if comm is None else (out[0][0], out[1])


def _mm_dw(a, dy, name, shards):
    r, m = a.shape
    c = dy.shape[1]
    cs = c // shards
    tm, tn = _div_tile(m, 1408), _div_tile(cs, 1408)
    fixed = tm * tn * (4 + 2 * (4 + 2))
    per_row = 2 * (tm * a.dtype.itemsize + tn * dy.dtype.itemsize)
    tk = max(t for t in (256, 512, 1024, 2048) if r % t == 0 and (t == 256 or fixed + t * per_row <= MM_DW_VMEM_BYTES))
    per = cs // tn
    nk = r // tk

    def body(a_ref, b_ref, o32, o16, acc):
        k = pl.program_id(2)
        p = lax.dot_general(a_ref[...].astype(BF16), b_ref[...].astype(BF16), _DIMS["tn"], preferred_element_type=F32)

        @pl.when(k == 0)
        def _():
            acc[...] = p

        @pl.when(k > 0)
        def _():
            acc[...] += p

        @pl.when(k == nk - 1)
        def _():
            o32[...] = acc[...]
            o16[...] = acc[...].astype(BF16)

    o_spec = pl.BlockSpec((None, tm, tn), lambda i, j, k: (j // per, i, j % per))
    shape = (shards, m, cs)
    in_specs = [pl.BlockSpec((tk, tm), lambda i, j, k: (k, i)), pl.BlockSpec((tk, tn), lambda i, j, k: (k, j))]
    return _pcall(body, name, (m // tm, c // tn, nk), in_specs, [o_spec, o_spec],
                  [jax.ShapeDtypeStruct(shape, F32), jax.ShapeDtypeStruct(shape, BF16)], [pltpu.VMEM((tm, tn), F32)],
                  ("parallel", "parallel", "arbitrary"), [a, dy])


def _core_index():
    return lax.axis_index("c")


def _chip_index():
    return 2 * lax.axis_index("x") + lax.axis_index("y")


def _rowwise(fn, name, n_rows, ins, outs, accs=(), tm=512):
    n_in, n_out = len(ins), len(outs)
    in_specs, args = [], []
    for spec in ins:
        if spec[0] == "row":
            _, arr, w, cb = spec
            in_specs.append(pl.BlockSpec((tm, w), lambda i, cb=cb: (i, cb)))
        else:
            arr = spec[1]
            in_specs.append(pl.BlockSpec(arr.shape, lambda i: (0, 0)))
        args.append(arr)
    out_specs = [pl.BlockSpec((tm, w), lambda i: (i, 0)) for w, _ in outs]
    out_specs += [pl.BlockSpec((1, w), lambda i: (0, 0)) for w in accs]
    out_shape = [jax.ShapeDtypeStruct((n_rows, w), dt) for w, dt in outs]
    out_shape += [jax.ShapeDtypeStruct((1, w), F32) for w in accs]

    def body(*refs):
        i = pl.program_id(0)
        res = fn(*[r[...] for r in refs[:n_in]])
        for o_ref, r in zip(refs[n_in:n_in + n_out], res[:n_out]):
            o_ref[...] = r.astype(o_ref.dtype)
        for a_ref, r in zip(refs[n_in + n_out:], res[n_out:]):
            @pl.when(i == 0)
            def _(a_ref=a_ref, r=r):
                a_ref[...] = r

            @pl.when(i > 0)
            def _(a_ref=a_ref, r=r):
                a_ref[...] += r

    return pl.pallas_call(
        body, name=name, grid=(n_rows // tm,), in_specs=in_specs, out_specs=out_specs, out_shape=out_shape,
        compiler_params=_cparams(("arbitrary",)))(*args)


EPS = 1e-6


def _sig(x):
    return 1.0 / (1.0 + jnp.exp(-x))


def _colsum(x):
    return jnp.sum(x, axis=0, keepdims=True)


def _rms_fwd_fn(x, g):
    r = lax.rsqrt(jnp.mean(x * x, axis=-1, keepdims=True) + EPS)
    return (x * r * g,)


def _rms_bwd_fn(x, g, dh, dres):
    dh = dh.astype(F32)
    r = lax.rsqrt(jnp.mean(x * x, axis=-1, keepdims=True) + EPS)
    xh = x * r
    dyg = dh * g
    dx = r * (dyg - xh * jnp.mean(dyg * xh, axis=-1, keepdims=True)) + dres
    return dx, _colsum(dh * xh)


def _loss_fn(x, g, t):
    d = x.shape[-1]
    r = lax.rsqrt(jnp.mean(x * x, axis=-1, keepdims=True) + EPS)
    xh = x * r
    err = xh * g - t
    dy = err * (1.0 / d)
    dyg = dy * g
    dx = r * (dyg - xh * jnp.mean(dyg * xh, axis=-1, keepdims=True))
    return dx, _colsum(err * err), _colsum(dy * xh)


def _swiglu_fwd_fn(z):
    f = z.shape[-1] // 2
    z1, z2 = z[:, :f].astype(F32), z[:, f:].astype(F32)
    return (z1 * _sig(z1) * z2,)


def _swiglu_bwd_fn(z, da):
    f = z.shape[-1] // 2
    z1, z2, da = z[:, :f].astype(F32), z[:, f:].astype(F32), da.astype(F32)
    s = _sig(z1)
    dz1 = da * z2 * (s * (1.0 + z1 * (1.0 - s)))
    dz2 = da * (z1 * s)
    return (jnp.concatenate([dz1, dz2], axis=1),)


def _merge_fwd_fn(g0, g1, g2, bg, zs, ya, yc):
    d = ya.shape[-1]
    bg = bg.astype(F32)
    zs = zs.astype(F32)
    ys = zs[:, :d] * _sig(zs[:, d:])
    m = _sig(g0.astype(F32) + bg[:, :d]) * ys
    m = m + _sig(g1.astype(F32) + bg[:, d:2 * d]) * ya.astype(F32)
    m = m + _sig(g2.astype(F32) + bg[:, 2 * d:]) * yc.astype(F32)
    return (m,)


def _merge_bwd_fn(g0, g1, g2, bg, zs, ya, yc, dm):
    d = ya.shape[-1]
    bg = bg.astype(F32)
    zs = zs.astype(F32)
    dm = dm.astype(F32)
    z1, s2 = zs[:, :d], _sig(zs[:, d:])
    ys = z1 * s2
    s0 = _sig(g0.astype(F32) + bg[:, :d])
    s1 = _sig(g1.astype(F32) + bg[:, d:2 * d])
    s3 = _sig(g2.astype(F32) + bg[:, 2 * d:])
    dgl = jnp.concatenate([dm * ys * s0 * (1.0 - s0), dm * ya.astype(F32) * s1 * (1.0 - s1),
                           dm * yc.astype(F32) * s3 * (1.0 - s3)], axis=1)
    dys = dm * s0
    dzs = jnp.concatenate([dys * s2, dys * z1 * s2 * (1.0 - s2)], axis=1)
    return dgl, dzs, dm * s1, dm * s3, _colsum(dgl)


def _combine_fwd_fn(o0, o1, o2, l0, l1, l2):
    m = jnp.maximum(jnp.maximum(l0, l1), l2)
    e0, e1, e2 = jnp.exp(l0 - m), jnp.exp(l1 - m), jnp.exp(l2 - m)
    den = e0 + e1 + e2
    return (e0 * o0.astype(F32) + e1 * o1.astype(F32) + e2 * o2.astype(F32)) / den, m + jnp.log(den)


ADAM_LR, ADAM_B1, ADAM_B2, ADAM_EPS, ADAM_WD, ADAM_STEP = 0.001, 0.9, 0.999, 1e-08, 0.01, 10


def _adamw_fn(w, g, m, v):
    m = ADAM_B1 * m + (1.0 - ADAM_B1) * g
    v = ADAM_B2 * v + (1.0 - ADAM_B2) * (g * g)
    m_hat = m / (1.0 - ADAM_B1 ** ADAM_STEP)
    v_hat = v / (1.0 - ADAM_B2 ** ADAM_STEP)
    delta = -ADAM_LR * (m_hat / (jnp.sqrt(v_hat) + ADAM_EPS) + ADAM_WD * w)
    return delta, m, v


CONV_WIDTH = 31


def _conv_fwd(proj, cb, w32, conv_b, ln_g, ln_b, bl, c, name, tm=512):
    n = proj.shape[0]
    hp = (CONV_WIDTH - 1) * bl
    nt = n // tm

    def body(ap_ref, gp_ref, a_ref, g_ref, w_ref, cb_ref, lg_ref, lb_ref, hc_ref, hconv_ref, ext):
        i = pl.program_id(0)
        ext[pl.ds(hp, tm), :] = a_ref[...].astype(F32) * _sig(g_ref[...].astype(F32))
        hgp = ap_ref[pl.ds(tm - hp, hp), :].astype(F32) * _sig(gp_ref[pl.ds(tm - hp, hp), :].astype(F32))
        ext[pl.ds(0, hp), :] = jnp.where(i > 0, hgp, 0.0)
        acc = jnp.zeros((tm, c), F32) + cb_ref[...]
        for j in range(CONV_WIDTH):
            acc = acc + w_ref[j:j + 1, :] * ext[pl.ds(j * bl, tm), :]
        hconv_ref[...] = acc.astype(hconv_ref.dtype)
        h = hconv_ref[...].astype(F32)
        mu = jnp.mean(h, axis=-1, keepdims=True)
        xc = h - mu
        var = jnp.mean(xc * xc, axis=-1, keepdims=True)
        hn = xc * lax.rsqrt(var + EPS) * lg_ref[...] + lb_ref[...]
        hc_ref[...] = (hn * _sig(hn)).astype(hc_ref.dtype)

    prev = lambda i, k: (jnp.maximum(i - 1, 0), k)
    par = lambda arr: pl.BlockSpec(arr.shape, lambda i: (0, 0))
    return pl.pallas_call(
        body, name=name, grid=(nt,),
        in_specs=[pl.BlockSpec((tm, c), functools.partial(prev, k=cb)), pl.BlockSpec((tm, c), functools.partial(prev, k=cb + 1)),
                  pl.BlockSpec((tm, c), lambda i: (i, cb)), pl.BlockSpec((tm, c), lambda i: (i, cb + 1)),
                  par(w32), par(conv_b), par(ln_g), par(ln_b)],
        out_specs=[pl.BlockSpec((tm, c), lambda i: (i, 0))] * 2,
        out_shape=[jax.ShapeDtypeStruct((n, c), BF16)] * 2,
        scratch_shapes=[pltpu.VMEM((hp + tm, c), F32)],
        compiler_params=_cparams(("arbitrary",)))(proj, proj, proj, proj, w32, conv_b, ln_g, ln_b)


def _conv_bwd(proj, cb, dhc, hconv, w32, ln_g, ln_b, bl, c, name, tm=512, comm=None):
    n = proj.shape[0]
    hp = (CONV_WIDTH - 1) * bl
    nt = n // tm

    def ln_bwd(d, h, lg, lb):
        d, h = d.astype(F32), h.astype(F32)
        mu = jnp.mean(h, axis=-1, keepdims=True)
        xc = h - mu
        rstd = lax.rsqrt(jnp.mean(xc * xc, axis=-1, keepdims=True) + EPS)
        xh = xc * rstd
        hn = xh * lg + lb
        s = _sig(hn)
        dhn = d * (s * (1.0 + hn * (1.0 - s)))
        dxh = dhn * lg
        dh = rstd * (dxh - jnp.mean(dxh, axis=-1, keepdims=True) - xh * jnp.mean(dxh * xh, axis=-1, keepdims=True))
        return dh, dhn, xh

    def body(ap_ref, gp_ref, a_ref, g_ref, d_ref, dn_ref, h_ref, hn_ref, w_ref, lg_ref, lb_ref,
             dcv_ref, dw_ref, dcb_ref, dlg_ref, dlb_ref, ext_h, ext_d):
        i = pl.program_id(0)
        lg, lb = lg_ref[...], lb_ref[...]
        a, g = a_ref[...].astype(F32), g_ref[...].astype(F32)
        sg = _sig(g)
        ext_h[pl.ds(hp, tm), :] = a * sg
        hgp = ap_ref[pl.ds(tm - hp, hp), :].astype(F32) * _sig(gp_ref[pl.ds(tm - hp, hp), :].astype(F32))
        ext_h[pl.ds(0, hp), :] = jnp.where(i > 0, hgp, 0.0)
        dh, dhn, xh = ln_bwd(d_ref[...], h_ref[...], lg, lb)
        ext_d[pl.ds(0, tm), :] = dh
        dh_n, _, _ = ln_bwd(dn_ref[pl.ds(0, hp), :], hn_ref[pl.ds(0, hp), :], lg, lb)
        ext_d[pl.ds(tm, hp), :] = jnp.where(i < nt - 1, dh_n, 0.0)

        @pl.when(i == 0)
        def _():
            dw_ref[...] = jnp.zeros_like(dw_ref)
            dcb_ref[...] = jnp.zeros_like(dcb_ref)
            dlg_ref[...] = jnp.zeros_like(dlg_ref)
            dlb_ref[...] = jnp.zeros_like(dlb_ref)

        dcb_ref[...] += _colsum(dh)
        dlg_ref[...] += _colsum(dhn * xh)
        dlb_ref[...] += _colsum(dhn)
        dhg = jnp.zeros((tm, c), F32)
        for j in range(CONV_WIDTH):
            dhg = dhg + w_ref[j:j + 1, :] * ext_d[pl.ds((CONV_WIDTH - 1 - j) * bl, tm), :]
            dw_ref[j:j + 1, :] += _colsum(dh * ext_h[pl.ds(j * bl, tm), :])
        dcv_ref[...] = jnp.concatenate([dhg * sg, dhg * a * sg * (1.0 - sg)], axis=1).astype(dcv_ref.dtype)

    prev = lambda i, k: (jnp.maximum(i - 1, 0), k)
    nxt = lambda i: (jnp.minimum(i + 1, nt - 1), 0)
    cur = lambda i: (i, 0)
    par = lambda arr: pl.BlockSpec(arr.shape, lambda i: (0, 0))
    acc = lambda r: pl.BlockSpec((r, c), lambda i: (0, 0))
    in_specs = [pl.BlockSpec((tm, c), functools.partial(prev, k=cb)), pl.BlockSpec((tm, c), functools.partial(prev, k=cb + 1)),
                pl.BlockSpec((tm, c), lambda i: (i, cb)), pl.BlockSpec((tm, c), lambda i: (i, cb + 1)),
                pl.BlockSpec((tm, c), cur), pl.BlockSpec((tm, c), nxt), pl.BlockSpec((tm, c), cur), pl.BlockSpec((tm, c), nxt),
                par(w32), par(ln_g), par(ln_b)]
    out_shape = [jax.ShapeDtypeStruct((n, 2 * c), BF16), jax.ShapeDtypeStruct((32, c), F32)] + [jax.ShapeDtypeStruct((1, c), F32)] * 3
    return _pcall(body, name, (nt,), in_specs, [pl.BlockSpec((tm, 2 * c), cur), acc(32), acc(1), acc(1), acc(1)], out_shape,
                  [pltpu.VMEM((hp + tm, c), F32), pltpu.VMEM((hp + tm, c), F32)], ("arbitrary",),
                  [proj, proj, proj, proj, dhc, dhc, hconv, hconv, w32, ln_g, ln_b], comm)


SSM_CH = 128
_GELU_C = 0.7978845608028654


def _gelu(x):
    return 0.5 * x * (1.0 + jnp.tanh(_GELU_C * (x + 0.044715 * x * x * x)))


def _gelu_grad(x):
    th = jnp.tanh(_GELU_C * (x + 0.044715 * x * x * x))
    return 0.5 * (1.0 + th) + 0.5 * x * (1.0 - th * th) * (_GELU_C * (1.0 + 3.0 * 0.044715 * x * x))


def _ssm_disc(lam_re, lam_im, log_dt, b_re, b_im):
    dt = jnp.exp(log_dt)[:, None]
    mag = jnp.exp(lam_re * dt)
    ab_re = mag * jnp.cos(lam_im * dt)
    ab_im = mag * jnp.sin(lam_im * dt)
    nr, ni = ab_re - 1.0, ab_im
    den = lam_re * lam_re + lam_im * lam_im
    z_re = ((nr * lam_re + ni * lam_im) / den)[..., None]
    z_im = ((ni * lam_re - nr * lam_im) / den)[..., None]
    return ab_re, ab_im, z_re * b_re - z_im * b_im, z_re * b_im + z_im * b_re


def _ssm_pack(ab_re, ab_im, bb_re, bb_im, c_re, c_im):
    g, p, h = bb_re.shape
    gc = SSM_CH // h
    nc = g // gc
    eye = jnp.eye(gc, dtype=F32)
    blk = lambda x: jnp.einsum("qgph,gk->qghkp", x.reshape(nc, gc, p, h), eye).reshape(nc, gc * h, gc * p)
    bbd = jnp.concatenate([blk(bb_re), blk(bb_im)], axis=2).astype(BF16)
    blc = lambda x: jnp.einsum("qghp,gk->qgpkh", x.reshape(nc, gc, h, p), eye).reshape(nc, gc * p, gc * h)
    cdm = jnp.concatenate([blc(c_re), blc(-c_im)], axis=1).astype(BF16)
    a = jnp.concatenate([ab_re.reshape(nc, gc * p), ab_im.reshape(nc, gc * p)], axis=1)
    a8 = jnp.broadcast_to(a[:, None, :], (nc, 8, 2 * gc * p)).reshape(nc * 8, 2 * gc * p)
    return bbd, cdm, a8


def _ssm_unpack(dbb, dcd, da, g, p, h):
    gc = SSM_CH // h
    nc = g // gc
    ph = gc * p
    eye = jnp.eye(gc, dtype=F32)
    dia = lambda x, o: jnp.einsum("qgpkh,gk->" + o, x.reshape(nc, gc, p, gc, h), eye).reshape((g, p, h) if o == "qgph" else (g, h, p))
    das = da.reshape(nc, 8, 2 * ph).sum(axis=1)
    return (das[:, :ph].reshape(g, p), das[:, ph:].reshape(g, p), dia(dbb[:, :ph], "qgph"), dia(dbb[:, ph:], "qgph"),
            dia(dcd[:, :ph], "qghp"), -dia(dcd[:, ph:], "qghp"))


def _ssm_fwd(proj, bbd, cdm, a8, dskip, bl, name, tm=1024, comm=None):
    n = proj.shape[0]
    nc, ch, p2 = bbd.shape
    ph = p2 // 2
    nt = n // tm
    nsub = 8 // bl

    def body(u_ref, bb_ref, cd_ref, a_ref, d_ref, ypre_ref, yg_ref, s_ref, bu, carry):
        t = pl.program_id(1)

        @pl.when(t == 0)
        def _():
            carry[...] = jnp.zeros_like(carry)

        u = u_ref[...]
        bu[...] = jnp.dot(u, bb_ref[0], preferred_element_type=F32)
        a_re, a_im = a_ref[:, :ph], a_ref[:, ph:]
        row = lax.broadcasted_iota(jnp.int32, (8, ph), 0)

        def step(k, c):
            cre, cim = c
            r0 = pl.multiple_of(k * 8, 8)
            bre, bim = bu[pl.ds(r0, 8), :ph], bu[pl.ds(r0, 8), ph:]
            sre, sim = cre, cim
            for sub in range(nsub):
                xre, xim = pltpu.roll(cre, bl, 0), pltpu.roll(cim, bl, 0)
                cre = a_re * xre - a_im * xim + bre
                cim = a_re * xim + a_im * xre + bim
                if sub == 0:
                    sre, sim = cre, cim
                else:
                    sel = row >= sub * bl
                    sre, sim = jnp.where(sel, cre, sre), jnp.where(sel, cim, sim)
            bu[pl.ds(r0, 8), :ph] = sre
            bu[pl.ds(r0, 8), ph:] = sim
            return sre, sim

        cre, cim = lax.fori_loop(0, tm // 8, step, (carry[:, :ph], carry[:, ph:]))
        carry[:, :ph] = cre
        carry[:, ph:] = cim
        s16 = bu[...].astype(BF16)
        s_ref[...] = s16
        y = jnp.dot(s16, cd_ref[0], preferred_element_type=F32) + d_ref[...] * u.astype(F32)
        ypre_ref[...] = y
        yg_ref[...] = _gelu(y).astype(yg_ref.dtype)

    in_specs = [pl.BlockSpec((tm, ch), lambda q, t: (t, q)), pl.BlockSpec((1, ch, p2), lambda q, t: (q, 0, 0)),
                pl.BlockSpec((1, p2, ch), lambda q, t: (q, 0, 0)), pl.BlockSpec((8, p2), lambda q, t: (q, 0)),
                pl.BlockSpec((1, ch), lambda q, t: (0, q))]
    out_specs = [pl.BlockSpec((tm, ch), lambda q, t: (t, q)), pl.BlockSpec((tm, ch), lambda q, t: (t, q)),
                 pl.BlockSpec((tm, p2), lambda q, t: (t, q))]
    out_shape = [jax.ShapeDtypeStruct((n, nc * ch), F32), jax.ShapeDtypeStruct((n, nc * ch), BF16),
                 jax.ShapeDtypeStruct((n, nc * p2), BF16)]
    return _pcall(body, name, (nc, nt), in_specs, out_specs, out_shape, [pltpu.VMEM((tm, p2), F32), pltpu.VMEM((8, p2), F32)],
                  ("parallel", "arbitrary"), [proj, bbd, cdm, a8, dskip], comm)


def _ssm_bwd(dyg, ypre, proj, s_all, cdt, bbt, a8, dskip, bl, name, tm=1024, comm=None):
    n = proj.shape[0]
    nc, ch, p2 = cdt.shape
    ph = p2 // 2
    nt = n // tm
    nsub = 8 // bl
    tn_dims = (((0,), (0,)), ((), ()))

    def body(dyg_ref, ypre_ref, u_ref, s_ref, cdt_ref, bbt_ref, a_ref, d_ref,
             du_ref, dbb_ref, dcd_ref, da_ref, dd_ref, ds, s32, carry):
        t = pl.program_id(1)

        @pl.when(t == 0)
        def _():
            carry[...] = jnp.zeros_like(carry)
            dbb_ref[...] = jnp.zeros_like(dbb_ref)
            dcd_ref[...] = jnp.zeros_like(dcd_ref)
            da_ref[...] = jnp.zeros_like(da_ref)
            dd_ref[...] = jnp.zeros_like(dd_ref)

        dyp = dyg_ref[...].astype(F32) * _gelu_grad(ypre_ref[...])
        u = u_ref[...]
        dd_ref[...] += _colsum(dyp * u.astype(F32))
        dyp16 = dyp.astype(BF16)
        ds[...] = jnp.dot(dyp16, cdt_ref[0], preferred_element_type=F32)
        s16 = s_ref[...]
        s32[...] = s16.astype(F32)
        a_re, a_im = a_ref[:, :ph], a_ref[:, ph:]
        row = lax.broadcasted_iota(jnp.int32, (8, ph), 0)
        back = 8 - bl

        def step(kk, c):
            lre, lim, acr, aci = c
            r0 = pl.multiple_of((tm // 8 - 1 - kk) * 8, 8)
            dre, dim = ds[pl.ds(r0, 8), :ph], ds[pl.ds(r0, 8), ph:]
            sre, sim = s32[pl.ds(r0, 8), :ph], s32[pl.ds(r0, 8), ph:]
            ore, oim, ire, iim = lre, lim, lre, lim
            for sub in range(nsub - 1, -1, -1):
                xre, xim = pltpu.roll(lre, back, 0), pltpu.roll(lim, back, 0)
                lre = a_re * xre + a_im * xim + dre
                lim = a_re * xim - a_im * xre + dim
                if sub == nsub - 1:
                    ore, oim, ire, iim = lre, lim, xre, xim
                else:
                    sel = row < (sub + 1) * bl
                    ore, oim = jnp.where(sel, lre, ore), jnp.where(sel, lim, oim)
                    ire, iim = jnp.where(sel, xre, ire), jnp.where(sel, xim, iim)
            ds[pl.ds(r0, 8), :ph] = ore
            ds[pl.ds(r0, 8), ph:] = oim
            acr = acr + sre * ire + sim * iim
            aci = aci + sre * iim - sim * ire
            return ore, oim, acr, aci

        z = jnp.zeros((8, ph), F32)
        lre, lim, acr, aci = lax.fori_loop(0, tm // 8, step, (carry[:, :ph], carry[:, ph:], z, z))
        carry[:, :ph] = lre
        carry[:, ph:] = lim
        da_ref[:, :ph] += acr
        da_ref[:, ph:] += aci
        lam16 = ds[...].astype(BF16)
        du = jnp.dot(lam16, bbt_ref[0], preferred_element_type=F32) + d_ref[...] * dyp
        du_ref[...] = du.astype(du_ref.dtype)
        dbb_ref[0] += lax.dot_general(lam16, u, tn_dims, preferred_element_type=F32)
        dcd_ref[0] += lax.dot_general(s16, dyp16, tn_dims, preferred_element_type=F32)

    rev = lambda q, t: (nt - 1 - t, q)
    in_specs = [pl.BlockSpec((tm, ch), rev), pl.BlockSpec((tm, ch), rev), pl.BlockSpec((tm, ch), rev),
                pl.BlockSpec((tm, p2), rev), pl.BlockSpec((1, ch, p2), lambda q, t: (q, 0, 0)),
                pl.BlockSpec((1, p2, ch), lambda q, t: (q, 0, 0)), pl.BlockSpec((8, p2), lambda q, t: (q, 0)),
                pl.BlockSpec((1, ch), lambda q, t: (0, q))]
    out_specs = [pl.BlockSpec((tm, ch), rev), pl.BlockSpec((1, p2, ch), lambda q, t: (q, 0, 0)),
                 pl.BlockSpec((1, p2, ch), lambda q, t: (q, 0, 0)), pl.BlockSpec((8, p2), lambda q, t: (q, 0)),
                 pl.BlockSpec((1, ch), lambda q, t: (0, q))]
    out_shape = [jax.ShapeDtypeStruct((n, nc * ch), BF16), jax.ShapeDtypeStruct((nc, p2, ch), F32),
                 jax.ShapeDtypeStruct((nc, p2, ch), F32), jax.ShapeDtypeStruct((nc * 8, p2), F32),
                 jax.ShapeDtypeStruct((1, nc * ch), F32)]
    return _pcall(body, name, (nc, nt), in_specs, out_specs, out_shape,
                  [pltpu.VMEM((tm, p2), F32), pltpu.VMEM((tm, p2), F32), pltpu.VMEM((8, p2), F32)],
                  ("parallel", "arbitrary"), [dyg, ypre, proj, s_all, cdt, bbt, a8, dskip], comm)


_MESH = pl.DeviceIdType.MESH
_HBM = pl.BlockSpec(memory_space=pltpu.HBM)


def _position():
    return lax.axis_index("x"), lax.axis_index("y"), lax.axis_index("c")


def _other_chips(x, y):
    return [((1 - x, y), 2 * (1 - x) + y), ((x, 1 - y), 2 * x + 1 - y), ((1 - x, 1 - y), 2 * (1 - x) + 1 - y)]


def _swap_sibling(v, name):
    def body(v_ref, got_ref, send_sem, recv_sem):
        x, y, c = _position()
        cp = pltpu.make_async_remote_copy(src_ref=v_ref, dst_ref=got_ref, send_sem=send_sem, recv_sem=recv_sem,
                                          device_id=(x, y, 1 - c), device_id_type=_MESH)
        cp.start()
        cp.wait()

    return pl.pallas_call(
        body, name=name, in_specs=[_HBM], out_specs=_HBM, out_shape=jax.ShapeDtypeStruct(v.shape, v.dtype),
        scratch_shapes=[pltpu.SemaphoreType.DMA, pltpu.SemaphoreType.DMA])(v)


def _own_slot(gathered, own):
    return lax.dynamic_update_index_in_dim(gathered, own, _chip_index(), 0)


def _chip_allgather(v, name):
    def body(v_ref, out_ref, send_sems, recv_sems):
        x, y, c = _position()
        me = 2 * x + y
        sends = []
        for k, (chip, idx) in enumerate(_other_chips(x, y)):
            cp = pltpu.make_async_remote_copy(src_ref=v_ref, dst_ref=out_ref.at[me], send_sem=send_sems.at[k],
                                              recv_sem=recv_sems.at[k], device_id=(*chip, c), device_id_type=_MESH)
            cp.start()
            sends.append(cp)
        for k, (chip, idx) in enumerate(_other_chips(x, y)):
            pltpu.make_async_remote_copy(src_ref=v_ref, dst_ref=out_ref.at[idx], send_sem=send_sems.at[k],
                                         recv_sem=recv_sems.at[k], device_id=(*chip, c), device_id_type=_MESH).wait_recv()
        for cp in sends:
            cp.wait_send()

    out = pl.pallas_call(
        body, name=name, in_specs=[_HBM], out_specs=_HBM, out_shape=jax.ShapeDtypeStruct((4,) + tuple(v.shape), v.dtype),
        scratch_shapes=[pltpu.SemaphoreType.DMA((3,)), pltpu.SemaphoreType.DMA((3,))])(v)
    return _own_slot(out, v)


def _remote(src, dst, send_sems, recv_sems, s, device):
    return pltpu.make_async_remote_copy(src_ref=src, dst_ref=dst, send_sem=send_sems.at[s], recv_sem=recv_sems.at[s],
                                        device_id=device, device_id_type=_MESH)


class _Exchange:
    def __init__(self, ins, out_shapes, n_sems, aliases=None):
        self.ins, self.out_shapes, self.n_sems, self.aliases = list(ins), list(out_shapes), n_sems, aliases or {}

    def sem_shapes(self):
        return [pltpu.SemaphoreType.DMA((self.n_sems,)), pltpu.SemaphoreType.DMA((self.n_sems,))]


def _halves(ref, c, axis=0):
    h = ref.shape[axis] // 2
    idx = (slice(None),) * axis
    return ref.at[idx + (pl.ds(c * h, h),)], ref.at[idx + (pl.ds((1 - c) * h, h),)]


class _GatherShards(_Exchange):
    def __init__(self, ws):
        super().__init__(ws, [jax.ShapeDtypeStruct((N_CHIPS,) + tuple(w.shape), w.dtype) for w in ws], 6 * len(ws))

    def start(self, w_refs, out_refs, sems):
        send_sems, recv_sems = sems
        x, y, c = _position()
        me = 2 * x + y
        for i, (w, out) in enumerate(zip(w_refs, out_refs)):
            for k, (chip, idx) in enumerate(_other_chips(x, y)):
                _remote(_halves(w, c)[0], _halves(out.at[me], c)[0], send_sems, recv_sems, 6 * i + k, (*chip, c)).start()

    def finish(self, w_refs, out_refs, sems):
        send_sems, recv_sems = sems
        x, y, c = _position()
        sibling = (x, y, 1 - c)
        others = _other_chips(x, y)
        for i, out in enumerate(out_refs):
            for k, (chip, idx) in enumerate(others):
                landed = _halves(out.at[idx], c)[0]
                _remote(landed, landed, send_sems, recv_sems, 6 * i + k, (*chip, c)).wait_recv()
                _remote(landed, landed, send_sems, recv_sems, 6 * i + 3 + k, sibling).start()
        for i, (w, out) in enumerate(zip(w_refs, out_refs)):
            for k, (chip, idx) in enumerate(others):
                mine, theirs = _halves(out.at[idx], c)
                _remote(theirs, theirs, send_sems, recv_sems, 6 * i + 3 + k, sibling).wait_recv()
                _remote(mine, mine, send_sems, recv_sems, 6 * i + 3 + k, sibling).wait_send()
                _remote(_halves(w, c)[0], mine, send_sems, recv_sems, 6 * i + k, (*chip, c)).wait_send()


class _SwapHalves(_Exchange):
    def __init__(self, gs):
        shapes = [jax.ShapeDtypeStruct((g.shape[0], g.shape[1] // 2) + tuple(g.shape[2:]), g.dtype) for g in gs]
        super().__init__(gs, shapes, N_CHIPS * len(gs))

    def _copies(self, g_refs, out_refs, sems):
        x, y, c = _position()
        return [_remote(_halves(g.at[j], c)[1], out.at[j], sems[0], sems[1], N_CHIPS * i + j, (x, y, 1 - c))
                for i, (g, out) in enumerate(zip(g_refs, out_refs)) for j in range(N_CHIPS)]

    def start(self, g_refs, out_refs, sems):
        for cp in self._copies(g_refs, out_refs, sems):
            cp.start()

    def finish(self, g_refs, out_refs, sems):
        for cp in self._copies(g_refs, out_refs, sems):
            cp.wait()


class _ScatterPieces(_Exchange):
    def __init__(self, ps):
        super().__init__(ps, [jax.ShapeDtypeStruct((3,) + tuple(p.shape[1:]), p.dtype) for p in ps], 3 * len(ps))

    def _copies(self, p_refs, out_refs, sems):
        x, y, c = _position()
        return [_remote(p.at[idx], out.at[k], sems[0], sems[1], 3 * i + k, (*chip, c))
                for i, (p, out) in enumerate(zip(p_refs, out_refs)) for k, (chip, idx) in enumerate(_other_chips(x, y))]

    def start(self, p_refs, out_refs, sems):
        for cp in self._copies(p_refs, out_refs, sems):
            cp.start()

    def finish(self, p_refs, out_refs, sems):
        for cp in self._copies(p_refs, out_refs, sems):
            cp.wait()


class _JoinHalves(_Exchange):
    def __init__(self, rs):
        super().__init__(rs, [jax.ShapeDtypeStruct(r.shape, r.dtype) for r in rs], len(rs), {i: i for i in range(len(rs))})

    def start(self, r_refs, out_refs, sems):
        x, y, c = _position()
        for i, out in enumerate(out_refs):
            _remote(out.at[c], out.at[c], sems[0], sems[1], i, (x, y, 1 - c)).start()

    def finish(self, r_refs, out_refs, sems):
        x, y, c = _position()
        for i, out in enumerate(out_refs):
            _remote(out.at[c], out.at[c], sems[0], sems[1], i, (x, y, 1 - c)).wait_send()
            _remote(out.at[1 - c], out.at[1 - c], sems[0], sems[1], i, (x, y, 1 - c)).wait_recv()


def _run_exchange(ex, name):
    def body(*refs):
        ins, outs, sems = refs[:len(ex.ins)], refs[len(ex.ins):len(ex.ins) + len(ex.out_shapes)], refs[-2:]
        ex.start(ins, outs, sems)
        ex.finish(ins, outs, sems)

    return pl.pallas_call(body, name=name, in_specs=[_HBM] * len(ex.ins), out_specs=[_HBM] * len(ex.out_shapes),
                          out_shape=ex.out_shapes, scratch_shapes=ex.sem_shapes(), input_output_aliases=ex.aliases)(*ex.ins)


def _pcall(body, name, grid, in_specs, out_specs, out_shape, scratch_shapes, semantics, args, comm=None):
    if comm is None:
        return pl.pallas_call(body, name=name, grid=grid, in_specs=in_specs, out_specs=out_specs, out_shape=out_shape,
                              scratch_shapes=scratch_shapes, compiler_params=_cparams(semantics))(*args)
    n_in, n_out, n_scr, ci, co = len(in_specs), len(out_specs), len(scratch_shapes), len(comm.ins), len(comm.out_shapes)

    def wrapped(*refs):
        parts, a = [], 0
        for k in (n_in, ci, n_out, co, n_scr, 2):
            parts.append(refs[a:a + k])
            a += k
        ins, cins, outs, couts, scr, sems = parts
        ids = [pl.program_id(i) for i in range(len(grid))]
        first = functools.reduce(jnp.logical_and, [i == 0 for i in ids])
        last = functools.reduce(jnp.logical_and, [i == g - 1 for i, g in zip(ids, grid)])

        @pl.when(first)
        def _():
            comm.start(cins, couts, sems)

        body(*ins, *outs, *scr)

        @pl.when(last)
        def _():
            comm.finish(cins, couts, sems)

    res = pl.pallas_call(
        wrapped, name=name, grid=grid, in_specs=list(in_specs) + [_HBM] * ci, out_specs=list(out_specs) + [_HBM] * co,
        out_shape=list(out_shape) + comm.out_shapes, scratch_shapes=list(scratch_shapes) + comm.sem_shapes(),
        compiler_params=_cparams(("arbitrary",) * len(grid)))(*args, *comm.ins)
    return res[:n_out], res[n_out:]


ATT_WINDOW = 128
PHASES = 16
_NT = (((1,), (1,)), ((), ()))
_TN = (((0,), (0,)), ((), ()))


PERM_LANES = 512


def _phase_perm(bl):
    t = 16 * PHASES * bl
    col = jnp.arange(t)
    i, r, b = col // (PHASES * bl), (col // bl) % PHASES, col % bl
    return (jnp.arange(t)[:, None] == ((b * PHASES + r) * 16 + i)[None, :]).astype(BF16)


def _to_phase_order(x, bl, col0=0, width=None):
    n = x.shape[0]
    width = width or x.shape[1]
    t = 16 * PHASES * bl
    g = n // bl // PHASES
    tn = min(PERM_LANES, width)

    def body(p_ref, x_ref, o_ref):
        o_ref[...] = jnp.dot(p_ref[...], x_ref[...], preferred_element_type=F32).astype(o_ref.dtype).reshape(o_ref.shape)

    out = pl.pallas_call(
        body, name="to_phase", grid=(n // t, width // tn),
        in_specs=[pl.BlockSpec((t, t), lambda i, j: (0, 0)), pl.BlockSpec((t, tn), lambda i, j: (i, col0 // tn + j))],
        out_specs=pl.BlockSpec((bl * PHASES, 16, tn), lambda i, j: (0, i, j)),
        out_shape=jax.ShapeDtypeStruct((bl * PHASES, g, width), x.dtype),
        compiler_params=_cparams(("parallel", "parallel")))(_phase_perm(bl), x)
    return out.reshape(n, width)


def _from_phase_order(y, bl):
    n, width = y.shape
    t = 16 * PHASES * bl
    g = n // bl // PHASES
    tn = min(PERM_LANES, width)

    def body(p_ref, y_ref, o_ref):
        o_ref[...] = jnp.dot(p_ref[...], y_ref[...].reshape(t, tn), preferred_element_type=F32).astype(o_ref.dtype)

    return pl.pallas_call(
        body, name="from_phase", grid=(n // t, width // tn),
        in_specs=[pl.BlockSpec((t, t), lambda i, j: (0, 0)), pl.BlockSpec((bl * PHASES, 16, tn), lambda i, j: (0, i, j))],
        out_specs=pl.BlockSpec((t, tn), lambda i, j: (i, j)), out_shape=jax.ShapeDtypeStruct((n, width), y.dtype),
        compiler_params=_cparams(("parallel", "parallel")))(_phase_perm(bl).T, y.reshape(bl * PHASES, g, width))


def _att_geometry(p, n, bl):
    g = n // bl // PHASES
    if p == 0:
        return ((bl, PHASES, g), (bl, g // 16), (None, PHASES, 16),
                lambda sh: (lambda b, a: (b, 0, jnp.maximum(a + sh, 0))), 256, 16, lambda ids: ids[1] == 0)
    if p == 1:
        return ((bl, 4, 4, g), (bl, 2, g // 32), (None, 4, 2, 32),
                lambda sh: (lambda b, r, a: (b, 0, r, jnp.maximum(a + sh, 0))), 128, 32, lambda ids: ids[2] == 0)
    return ((bl * PHASES, g), (bl * PHASES // 2,), (2, g), lambda sh: (lambda s: (s, 0)), g, g, None)


def _att_units(p, n, bl):
    g = n // bl // PHASES
    full = slice(None)
    if p == 0:
        return [(full, full)], (PHASES, 16)
    if p == 1:
        return [(full, u, full) for u in range(2)], (4, 32)
    return [(u, full) for u in range(2)], (g,)


def _att_masks(p, qb, chunk):
    def pos(idx):
        return (idx % chunk) * (qb // chunk) + idx // chunk

    dq = pos(lax.broadcasted_iota(jnp.int32, (qb, qb), 0))
    dk = pos(lax.broadcasted_iota(jnp.int32, (qb, qb), 1))
    dist = dq - dk
    return jnp.logical_and(dist >= 0, dist <= ATT_WINDOW), dist + qb <= ATT_WINDOW


def _att_call(p, n, bl, c, body, name, ins, outs):
    prefix, grid, blk, idx_fn, qb, chunk, _ = _att_geometry(p, n, bl)

    def spec(cb, sh):
        f = idx_fn(sh)
        return pl.BlockSpec(blk + (c,), lambda *ids, f=f, cb=cb: f(*ids) + (cb,))

    in_specs = [spec(cb, sh) for _, cb, sh in ins]
    out_specs = [spec(0, 0) for _ in outs]
    out_shape = [jax.ShapeDtypeStruct(prefix + (c,), dt) for dt in outs]
    res = pl.pallas_call(body, name=name, grid=grid, in_specs=in_specs, out_specs=out_specs, out_shape=out_shape,
                         compiler_params=_cparams(("parallel",) * len(grid)))(*[a.reshape(prefix + (a.shape[1],)) for a, _, _ in ins])
    return [r.reshape(n, c) for r in res]


class _AttTiles:
    def __init__(self, p, n, bl, c, first):
        _, _, _, _, qb, chunk, _ = _att_geometry(p, n, bl)
        units, self.unit_shape = _att_units(p, n, bl)
        self.split = p == 0
        self.rows = qb // 2 if self.split else qb
        halves = (0, 1) if self.split else (None,)
        self.tiles = [(u, pl.ds(lt * 128, 128), h) for u in units for lt in range(c // 128) for h in halves]
        self.mask_cur, mp = _att_masks(p, self.rows, chunk // 2 if self.split else chunk)
        gated = mp if first is None else jnp.logical_and(mp, jnp.logical_not(first))
        self.mask_prev = [mp if h == 1 else gated for _, _, h in self.tiles]

    def _half(self, x, h):
        return x.astype(F32)[:, 8 * h:8 * h + 8, :].reshape(self.rows, 128).astype(x.dtype)

    def cur(self, ref, t):
        u, ls, h = self.tiles[t]
        x = ref[u + (ls,)]
        return x.reshape(self.rows, 128) if h is None else self._half(x, h)

    def prev(self, cur_ref, prev_ref, t):
        u, ls, h = self.tiles[t]
        if h is None:
            return prev_ref[u + (ls,)].reshape(self.rows, 128)
        return self._half(prev_ref[u + (ls,)], 1) if h == 0 else self._half(cur_ref[u + (ls,)], 0)

    def store(self, ref, vals):
        if not self.split:
            for (u, ls, _), v in zip(self.tiles, vals):
                ref[u + (ls,)] = v.astype(ref.dtype).reshape(self.unit_shape + (128,))
            return
        for k in range(len(self.tiles) // 2):
            u, ls, _ = self.tiles[2 * k]
            parts = [v.astype(F32).reshape(self.unit_shape[0], 8, 128) for v in vals[2 * k:2 * k + 2]]
            ref[u + (ls,)] = jnp.concatenate(parts, axis=1).astype(ref.dtype)

    def fold_keys(self, cur_vals, prev_vals):
        if not self.split:
            return cur_vals, prev_vals
        own, before = [], []
        for k in range(len(self.tiles) // 2):
            own += [cur_vals[2 * k] + prev_vals[2 * k + 1], cur_vals[2 * k + 1]]
            before += [jnp.zeros_like(prev_vals[2 * k]), prev_vals[2 * k]]
        return own, before


def _att_fwd(p, qkv, qcb, bl, c, heads):
    n = qkv.shape[0]
    _, grid, _, _, _, _, first_fn = _att_geometry(p, n, bl)
    n_grid = len(grid)
    has_prev = first_fn is not None
    e = c // heads
    scale = e ** -0.5

    def body(*refs):
        if has_prev:
            q_ref, kc_ref, kp_ref, vc_ref, vp_ref, o_ref, l_ref = refs
        else:
            q_ref, kc_ref, vc_ref, o_ref, l_ref = refs
            kp_ref = vp_ref = None
        tl = _AttTiles(p, n, bl, c, first_fn([pl.program_id(a) for a in range(n_grid)]) if has_prev else None)
        mc, n_t = tl.mask_cur, len(tl.tiles)
        lo = lax.broadcasted_iota(jnp.int32, (tl.rows, 128), 1) < e
        ones = jnp.ones((tl.rows, 128), BF16)
        items = [(t, h) for t in range(n_t) for h in range(2)]
        dot = functools.partial(jnp.dot, preferred_element_type=F32)
        q2 = [tl.cur(q_ref, t) for t in range(n_t)]
        kc = [tl.cur(kc_ref, t) for t in range(n_t)]
        qm = [jnp.where(lo if h == 0 else jnp.logical_not(lo), q2[t], jnp.zeros_like(q2[t])) for t, h in items]
        sc = [jnp.where(mc, lax.dot_general(qm[i], kc[t], _NT, preferred_element_type=F32) * scale, -jnp.inf)
              for i, (t, h) in enumerate(items)]
        m = [jnp.max(s, axis=1, keepdims=True) for s in sc]
        if has_prev:
            kp = [tl.prev(kc_ref, kp_ref, t) for t in range(n_t)]
            sp = [jnp.where(tl.mask_prev[t], lax.dot_general(qm[i], kp[t], _NT, preferred_element_type=F32) * scale, -jnp.inf)
                  for i, (t, h) in enumerate(items)]
            m = [jnp.maximum(a, jnp.max(s, axis=1, keepdims=True)) for a, s in zip(m, sp)]
        pc = [jnp.exp(s - a).astype(BF16) for s, a in zip(sc, m)]
        vc = [tl.cur(vc_ref, t) for t in range(n_t)]
        acc = [dot(pc[i], vc[t]) for i, (t, h) in enumerate(items)]
        den = [dot(x, ones) for x in pc]
        if has_prev:
            pp = [jnp.exp(s - a).astype(BF16) for s, a in zip(sp, m)]
            vp = [tl.prev(vc_ref, vp_ref, t) for t in range(n_t)]
            acc = [a + dot(pp[i], vp[t]) for i, ((t, h), a) in enumerate(zip(items, acc))]
            den = [d + dot(x, ones) for d, x in zip(den, pp)]
        oh = [a / d for a, d in zip(acc, den)]
        lh = [a + jnp.log(d) for a, d in zip(m, den)]
        tl.store(o_ref, [jnp.where(lo, oh[2 * t], oh[2 * t + 1]) for t in range(n_t)])
        tl.store(l_ref, [jnp.where(lo, lh[2 * t], lh[2 * t + 1]) for t in range(n_t)])

    kcb, vcb = 3, 4
    ins = [(qkv, qcb, 0), (qkv, kcb, 0)] + ([(qkv, kcb, -1)] if has_prev else []) + [(qkv, vcb, 0)] + ([(qkv, vcb, -1)] if has_prev else [])
    return _att_call(p, n, bl, c, body, name=f"att_fwd{p}", ins=ins, outs=[BF16, F32])


def _att_bwd(p, qkv, qcb, o, do, lse, bl, c, heads):
    n = qkv.shape[0]
    _, grid, _, _, _, _, first_fn = _att_geometry(p, n, bl)
    n_grid = len(grid)
    has_prev = first_fn is not None
    e = c // heads
    scale = e ** -0.5

    def body(*refs):
        if has_prev:
            q_ref, kc_ref, kp_ref, vc_ref, vp_ref, o_ref, do_ref, l_ref, dq_ref, dkc_ref, dkp_ref, dvc_ref, dvp_ref = refs
        else:
            q_ref, kc_ref, vc_ref, o_ref, do_ref, l_ref, dq_ref, dkc_ref, dvc_ref = refs
            kp_ref = vp_ref = None
        tl = _AttTiles(p, n, bl, c, first_fn([pl.program_id(a) for a in range(n_grid)]) if has_prev else None)
        mc, n_t = tl.mask_cur, len(tl.tiles)
        lo = lax.broadcasted_iota(jnp.int32, (tl.rows, 128), 1) < e
        items = [(t, h) for t in range(n_t) for h in range(2)]
        nt_dot = lambda a, b: lax.dot_general(a, b, _NT, preferred_element_type=F32)
        tn_dot = lambda a, b: lax.dot_general(a, b, _TN, preferred_element_type=F32)
        dot = functools.partial(jnp.dot, preferred_element_type=F32)
        sel = [lo if h == 0 else jnp.logical_not(lo) for t, h in items]
        q2, kc, vc, do2 = ([tl.cur(r, t) for t in range(n_t)] for r in (q_ref, kc_ref, vc_ref, do_ref))
        qm = [jnp.where(sel[i], q2[t], jnp.zeros_like(q2[t])) for i, (t, h) in enumerate(items)]
        dom = [jnp.where(sel[i], do2[t], jnp.zeros_like(do2[t])) for i, (t, h) in enumerate(items)]
        dod = [do2[t].astype(F32) * tl.cur(o_ref, t).astype(F32) for t in range(n_t)]
        lcol = [tl.cur(l_ref, t)[:, h * e:h * e + 1] for t, h in items]
        corr = [-jnp.sum(jnp.where(sel[i], dod[t], 0.0), axis=1, keepdims=True) for i, (t, h) in enumerate(items)]
        pc = [jnp.exp(jnp.where(mc, nt_dot(qm[i], kc[t]) * scale, -jnp.inf) - lcol[i]) for i, (t, h) in enumerate(items)]
        dsc = [(pc[i] * (nt_dot(dom[i], vc[t]) + corr[i]) * scale).astype(BF16) for i, (t, h) in enumerate(items)]
        pc = [x.astype(BF16) for x in pc]
        dq = [dot(dsc[i], kc[t]) for i, (t, h) in enumerate(items)]
        dkc = [tn_dot(dsc[2 * t], qm[2 * t]) + tn_dot(dsc[2 * t + 1], qm[2 * t + 1]) for t in range(n_t)]
        dvc = [tn_dot(pc[2 * t], dom[2 * t]) + tn_dot(pc[2 * t + 1], dom[2 * t + 1]) for t in range(n_t)]
        if has_prev:
            kp = [tl.prev(kc_ref, kp_ref, t) for t in range(n_t)]
            vp = [tl.prev(vc_ref, vp_ref, t) for t in range(n_t)]
            pp = [jnp.exp(jnp.where(tl.mask_prev[t], nt_dot(qm[i], kp[t]) * scale, -jnp.inf) - lcol[i]) for i, (t, h) in enumerate(items)]
            dsp = [(pp[i] * (nt_dot(dom[i], vp[t]) + corr[i]) * scale).astype(BF16) for i, (t, h) in enumerate(items)]
            pp = [x.astype(BF16) for x in pp]
            dq = [a + dot(dsp[i], kp[t]) for i, ((t, h), a) in enumerate(zip(items, dq))]
            dkp = [tn_dot(dsp[2 * t], qm[2 * t]) + tn_dot(dsp[2 * t + 1], qm[2 * t + 1]) for t in range(n_t)]
            dvp = [tn_dot(pp[2 * t], dom[2 * t]) + tn_dot(pp[2 * t + 1], dom[2 * t + 1]) for t in range(n_t)]
            (dkc, dkp), (dvc, dvp) = tl.fold_keys(dkc, dkp), tl.fold_keys(dvc, dvp)
            tl.store(dkp_ref, dkp)
            tl.store(dvp_ref, dvp)
        tl.store(dq_ref, [jnp.where(lo, dq[2 * t], dq[2 * t + 1]) for t in range(n_t)])
        tl.store(dkc_ref, dkc)
        tl.store(dvc_ref, dvc)

    kcb, vcb = 3, 4
    ins = [(qkv, qcb, 0), (qkv, kcb, 0)] + ([(qkv, kcb, -1)] if has_prev else []) + [(qkv, vcb, 0)] + ([(qkv, vcb, -1)] if has_prev else [])
    ins += [(o, 0, 0), (do, 0, 0), (lse, 0, 0)]
    res = _att_call(p, n, bl, c, body, name=f"att_bwd{p}", ins=ins, outs=[BF16] * (5 if has_prev else 3))
    if has_prev:
        dq, dkc, dkp, dvc, dvp = res
        return dq, dkc, dkp, dvc, dvp
    dq, dkc, dvc = res
    return dq, dkc, None, dvc, None


def _dqkv_to_rows(dqs, dk_parts, dv_parts, bl):
    n, c = dqs[0].shape
    t = 16 * PHASES * bl
    g = n // bl // PHASES
    nb = n // t
    shifts = [_att_geometry(p, n, bl)[5] // 16 for p in range(len(dk_parts))]
    view = lambda a: a.reshape(bl * PHASES, g, c)
    blk = lambda s: pl.BlockSpec((bl * PHASES, 16, c), lambda i, j, s=s: (0, jnp.minimum(i + s, nb - 1), 0))
    ins, specs, layout = [], [], []
    for a in dqs:
        layout.append([(len(ins), 0)])
        ins.append(view(a))
        specs.append(blk(0))
    for parts in (dk_parts, dv_parts):
        terms = []
        for (own, prv), s in zip(parts, shifts):
            terms.append((len(ins), 0))
            ins.append(view(own))
            specs.append(blk(0))
            if prv is not None:
                terms.append((len(ins), s))
                ins.append(view(prv))
                specs.append(blk(s))
        layout.append(terms)

    def body(p_ref, *refs):
        o_ref = refs[-1]
        i, j = pl.program_id(0), pl.program_id(1)
        for col, terms in enumerate(layout):
            @pl.when(j == col)
            def _(terms=terms):
                if len(terms) == 1:
                    x = refs[terms[0][0]][...]
                else:
                    x = 0.0
                    for pos, s in terms:
                        v = refs[pos][...].astype(F32)
                        x = x + (v if s == 0 else jnp.where(i + s < nb, v, 0.0))
                    x = x.astype(BF16)
                o_ref[...] = jnp.dot(p_ref[...], x.reshape(t, c), preferred_element_type=F32).astype(o_ref.dtype)

    return pl.pallas_call(
        body, name="dqkv_to_rows", grid=(nb, len(layout)), in_specs=[pl.BlockSpec((t, t), lambda i, j: (0, 0))] + specs,
        out_specs=pl.BlockSpec((t, c), lambda i, j: (i, j)), out_shape=jax.ShapeDtypeStruct((n, len(layout) * c), BF16),
        compiler_params=_cparams(("parallel", "arbitrary")))(_phase_perm(bl).T, *ins)


def _attention_fwd(proj, c, bl, heads):
    n = proj.shape[0]
    qkv = _to_phase_order(proj, bl, col0=c, width=5 * c)
    outs = [_att_fwd(p, qkv, p, bl, c, heads) for p in range(3)]
    ins = [("row", o, c, 0) for o, _ in outs] + [("row", l, c, 0) for _, l in outs]
    o, lse = _rowwise(_combine_fwd_fn, "comb_fwd", n, ins, [(c, BF16), (c, F32)])
    return _from_phase_order(o, bl), (qkv, o, lse)


def _attention_bwd(do_tb, saved, bl, heads):
    qkv, o, lse = saved
    n, c = do_tb.shape
    do = _to_phase_order(do_tb, bl)
    dqs, dks, dvs = [], [], []
    for p in range(3):
        dq, dkc, dkp, dvc, dvp = _att_bwd(p, qkv, p, o, do, lse, bl, c, heads)
        dqs.append(dq)
        dks.append((dkc, dkp))
        dvs.append((dvc, dvp))
    return _dqkv_to_rows(dqs, dks, dvs, bl)


ATT_HEADS = 8
SSM_GROUPS, SSM_STATE, SSM_GROUP = 32, 64, 16


def _row(v):
    return v.reshape(1, -1)


ROWS_TILE = 128


def _to_rows(x):
    bl, seq, d = x.shape

    def body(x_ref, o_ref):
        o_ref[...] = jnp.stack([x_ref[b] for b in range(bl)], axis=1).reshape(ROWS_TILE * bl, d)

    return pl.pallas_call(
        body, name="to_rows", grid=(seq // ROWS_TILE,), in_specs=[pl.BlockSpec((bl, ROWS_TILE, d), lambda i: (0, i, 0))],
        out_specs=pl.BlockSpec((ROWS_TILE * bl, d), lambda i: (i, 0)), out_shape=jax.ShapeDtypeStruct((seq * bl, d), x.dtype),
        compiler_params=_cparams(("parallel",)))(x)


def _from_rows(y, bl):
    n, d = y.shape
    seq = n // bl

    def body(y_ref, o_ref):
        v = y_ref[...].reshape(ROWS_TILE, bl, d)
        for b in range(bl):
            o_ref[b] = v[:, b, :]

    return pl.pallas_call(
        body, name="from_rows", grid=(seq // ROWS_TILE,), in_specs=[pl.BlockSpec((ROWS_TILE * bl, d), lambda i: (i, 0))],
        out_specs=pl.BlockSpec((bl, ROWS_TILE, d), lambda i: (0, i, 0)), out_shape=jax.ShapeDtypeStruct((bl, seq, d), y.dtype),
        compiler_params=_cparams(("parallel",)))(y)


def _carried(result, carry, key, hidden):
    if carry.get(key) is None:
        return result
    result, hidden[key] = result
    return result


def _layer_fwd(x, w, p, bl, carry, late=None):
    n, d = x.shape
    c = d // 2
    hidden = {}
    h, = _rowwise(_rms_fwd_fn, "rms_fwd", n, [("row", x, d, 0), ("par", _row(p["norm1_g"]))], [(d, BF16)])
    proj = _carried(_mm(h, w["w_in"], "nn", BF16, "mm_in", comm=carry.get("mm_in")), carry, "mm_in", hidden)
    if late is not None:
        w = dict(w, **late(hidden["mm_in"]))
    disc, disc_vjp = jax.vjp(_ssm_disc, p["ssm_lambda_re"], p["ssm_lambda_im"], p["ssm_log_dt"], p["ssm_b_re"], p["ssm_b_im"])
    bbd, cdm, a8 = _ssm_pack(*disc, p["ssm_c_re"], p["ssm_c_im"])
    ypre, yg, s_all = _carried(_ssm_fwd(proj, bbd, cdm, a8, _row(p["ssm_d"]), bl, "ssm_fwd", comm=carry.get("ssm_fwd")),
                               carry, "ssm_fwd", hidden)
    zs = _mm(yg, w["w_ssm_glu"], "nn", BF16, "mm_glu")
    o, att = _attention_fwd(proj, c, bl, ATT_HEADS)
    ya = _mm(o, w["w_att_up"], "nn", BF16, "mm_att")
    w32 = jnp.concatenate([p["conv_w"], jnp.zeros((1, c), F32)], axis=0)
    hc, hconv = _conv_fwd(proj, 6, w32, _row(p["conv_b"]), _row(p["conv_ln_g"]), _row(p["conv_ln_b"]), bl, c, "conv_fwd")
    yc = _mm(hc, w["w_conv_pw2"], "nn", BF16, "mm_pw2")
    gates = [("row", proj, d, 4), ("row", proj, d, 5), ("row", proj, d, 6), ("par", _row(p["b_gate"]))]
    branches = [("row", zs, 2 * d, 0), ("row", ya, d, 0), ("row", yc, d, 0)]
    merged, = _rowwise(_merge_fwd_fn, "merge_fwd", n, gates + branches, [(d, BF16)])
    xm = _mm(merged, w["w_out"], "nn", F32, "mm_out", res=x)
    h2, = _rowwise(_rms_fwd_fn, "rms_fwd", n, [("row", xm, d, 0), ("par", _row(p["norm2_g"]))], [(d, BF16)])
    z = _carried(_mm(h2, w["w_ffn_in"], "nn", BF16, "mm_ffn_in", comm=carry.get("mm_ffn_in")), carry, "mm_ffn_in", hidden)
    f = z.shape[1] // 2
    a, = _rowwise(_swiglu_fwd_fn, "swiglu_fwd", n, [("row", z, 2 * f, 0)], [(f, BF16)], tm=256)
    xo = _mm(a, w["w_ffn_out"], "nn", F32, "mm_ffn_out", res=xm)
    saved = dict(x=x, h=h, proj=proj, disc_vjp=disc_vjp, bbd=bbd, cdm=cdm, a8=a8, ypre=ypre, yg=yg, s_all=s_all, zs=zs, o=o,
                 att=att, ya=ya, w32=w32, hc=hc, hconv=hconv, yc=yc, gates=gates, branches=branches, merged=merged, xm=xm,
                 h2=h2, z=z, a=a)
    return xo, saved, hidden, w


def _layer_bwd(dxo, s, w, p, bl, carry):
    n, d = dxo.shape
    c = d // 2
    g, bufs, hidden = {}, {}, {}
    f = s["a"].shape[1]

    def dw(key, a, dy, name):
        bufs[key] = _mm_dw(a, dy, name, 1 if key in ROW_SHARDED else N_CHIPS)

    da = _mm(dxo, w["w_ffn_out"], "nt", BF16, "mm_ffn_out_dx")
    dw("w_ffn_out", s["a"], dxo, "mm_ffn_out_dw")
    dz, = _rowwise(_swiglu_bwd_fn, "swiglu_bwd", n, [("row", s["z"], 2 * f, 0), ("row", da, f, 0)], [(2 * f, BF16)], tm=256)
    dh2 = _mm(dz, w["w_ffn_in"], "nt", F32, "mm_ffn_in_dx")
    dw("w_ffn_in", s["h2"], dz, "mm_ffn_in_dw")
    dxm, dg2 = _rowwise(_rms_bwd_fn, "rms_bwd", n, [("row", s["xm"], d, 0), ("par", _row(p["norm2_g"])), ("row", dh2, d, 0),
                                                   ("row", dxo, d, 0)], [(d, F32)], [d])
    g["norm2_g"] = dg2[0]
    dmerged = _mm(dxm, w["w_out"], "nt", BF16, "mm_out_dx")
    dw("w_out", s["merged"], dxm, "mm_out_dw")
    dgl, dzs, dya, dyc, dbg = _rowwise(_merge_bwd_fn, "merge_bwd", n, s["gates"] + s["branches"] + [("row", dmerged, d, 0)],
                                       [(3 * d, BF16), (2 * d, BF16), (d, BF16), (d, BF16)], [3 * d], tm=256)
    g["b_gate"] = dbg[0]
    dyg = _mm(dzs, w["w_ssm_glu"], "nt", BF16, "mm_glu_dx")
    dw("w_ssm_glu", s["yg"], dzs, "mm_glu_dw")
    du, dbb, dcd, dab, dd = _carried(
        _ssm_bwd(dyg, s["ypre"], s["proj"], s["s_all"], s["cdm"].transpose(0, 2, 1), s["bbd"].transpose(0, 2, 1), s["a8"],
                 _row(p["ssm_d"]), bl, "ssm_bwd", comm=carry.get("ssm_bwd")), carry, "ssm_bwd", hidden)
    dab_re, dab_im, dbb_re, dbb_im, g["ssm_c_re"], g["ssm_c_im"] = _ssm_unpack(dbb, dcd, dab, SSM_GROUPS, SSM_STATE, SSM_GROUP)
    (g["ssm_lambda_re"], g["ssm_lambda_im"], g["ssm_log_dt"], g["ssm_b_re"],
     g["ssm_b_im"]) = s["disc_vjp"]((dab_re, dab_im, dbb_re, dbb_im))
    g["ssm_d"] = dd[0]
    do = _mm(dya, w["w_att_up"], "nt", BF16, "mm_att_dx")
    dw("w_att_up", s["o"], dya, "mm_att_dw")
    dqkv = _attention_bwd(do, s["att"], bl, ATT_HEADS)
    dhc = _mm(dyc, w["w_conv_pw2"], "nt", BF16, "mm_pw2_dx")
    dw("w_conv_pw2", s["hc"], dyc, "mm_pw2_dw")
    dcv, dcw, dcb, dlg, dlb = _carried(
        _conv_bwd(s["proj"], 6, dhc, s["hconv"], s["w32"], _row(p["conv_ln_g"]), _row(p["conv_ln_b"]), bl, c, "conv_bwd",
                  comm=carry.get("conv_bwd")), carry, "conv_bwd", hidden)
    g["conv_w"], g["conv_b"], g["conv_ln_g"], g["conv_ln_b"] = dcw, dcb[0], dlg[0], dlb[0]
    dproj = jnp.concatenate([du, dqkv, dcv, dgl], axis=1)
    dh = _mm(dproj, w["w_in"], "nt", F32, "mm_in_dx")
    dw("w_in", s["h"], dproj, "mm_in_dw")
    dx, dg1 = _rowwise(_rms_bwd_fn, "rms_bwd", n, [("row", s["x"], d, 0), ("par", _row(p["norm1_g"])), ("row", dh, d, 0),
                                                  ("row", dxm, d, 0)], [(d, F32)], [d])
    g["norm1_g"] = dg1[0]
    return dx, g, bufs, hidden


WEIGHTS = ['norm1_g', 'w_in', 'b_gate', 'ssm_lambda_re', 'ssm_lambda_im', 'ssm_log_dt', 'ssm_b_re', 'ssm_b_im', 'ssm_c_re',
           'ssm_c_im', 'ssm_d', 'w_ssm_glu', 'w_att_up', 'conv_w', 'conv_b', 'conv_ln_g', 'conv_ln_b', 'w_conv_pw2', 'w_out',
           'norm2_g', 'w_ffn_in', 'w_ffn_out', 'final_g']
BIG = ['w_in', 'w_ssm_glu', 'w_att_up', 'w_conv_pw2', 'w_out', 'w_ffn_in', 'w_ffn_out']
ROW_SHARDED = ('w_out', 'w_ffn_out')
SMALL = [k for k in WEIGHTS if k not in BIG]
LANES = 1024
N_CHIPS = 4
ROW_TILE_BYTES = 36 * 1024 * 1024
MIN_SHARD_TILE = 1024


def _pad_rows(a, rows):
    return jnp.concatenate([a, jnp.zeros((rows - a.shape[0],) + a.shape[1:], a.dtype)], axis=0) if rows > a.shape[0] else a


def _row_tile(rows, width, n_arrays):
    best = 16
    for t in range(16, rows + 1, 16):
        if rows % t == 0 and t * width * 4 * n_arrays * 2 <= ROW_TILE_BYTES:
            best = t
    return best


def _flat_fn(fn, name, ins, n_out, rows):
    return _rowwise(fn, name, rows, [("row", a, LANES, 0) for a in ins], [(LANES, F32)] * n_out, tm=rows)


def _reduce_prepare(bufs):
    landed = _run_exchange(_SwapHalves([b16 for _, b16 in bufs]), "rs_swap")
    kept, p16s = [], []
    for (b32, _), la in zip(bufs, landed):
        s, m, cs = b32.shape
        h = m // 2
        tm = _row_tile(h, cs, 3)
        halves = b32.reshape(s, 2, h, cs)

        def body(g_ref, l_ref, o16):
            o16[...] = (g_ref[...] + l_ref[...].astype(F32)).astype(BF16)

        piece = pl.BlockSpec((None, tm, cs), lambda j, i: (j, i, 0))
        mine = pl.BlockSpec((None, None, tm, cs), lambda j, i: (j, _core_index(), i, 0))
        p16s.append(pl.pallas_call(
            body, name="rs_add", grid=(s, h // tm), in_specs=[mine, piece], out_specs=piece,
            out_shape=jax.ShapeDtypeStruct((s, h, cs), BF16), compiler_params=_cparams(("parallel", "parallel")))(halves, la))
        kept.append((halves, la))
    return kept, p16s


def _reduce_finish(kept, arrived):
    reduced = []
    for (halves, la), lb in zip(kept, arrived):
        _, h, cs = lb.shape
        tm = _row_tile(h, cs, 6)

        def body(g_ref, l_ref, a_ref, b_ref, c_ref, o_ref):
            own = g_ref[...] + l_ref[...].astype(F32)
            o_ref[...] = ((own + a_ref[...].astype(F32)) + b_ref[...].astype(F32)) + c_ref[...].astype(F32)

        mine = pl.BlockSpec((None, None, tm, cs), lambda i: (_chip_index(), _core_index(), i, 0))
        sibling = pl.BlockSpec((None, tm, cs), lambda i: (_chip_index(), i, 0))
        other = [pl.BlockSpec((None, tm, cs), lambda i, k=k: (k, i, 0)) for k in range(3)]
        half = pl.BlockSpec((None, tm, cs), lambda i: (_core_index(), i, 0))
        reduced.append(pl.pallas_call(
            body, name="rs_sum", grid=(h // tm,), in_specs=[mine, sibling] + other, out_specs=half,
            out_shape=jax.ShapeDtypeStruct((2, h, cs), F32), compiler_params=_cparams(("parallel",)))(halves, la, lb, lb, lb))
    joined = _run_exchange(_JoinHalves(reduced), "rs_gather")
    return [j.reshape(2 * j.shape[1], j.shape[2]) for j in joined]


def _adamw_layers(w, g_layers, m, v):
    depth, rows, cs = w.shape
    tm = _row_tile(rows, cs, 8)
    nb = rows // tm

    def body(*refs):
        w_ref, m_ref, v_ref = refs[:3]
        g_refs = refs[3:3 + depth]
        go_ref, d_ref, mo_ref, vo_ref = refs[3 + depth:]
        layer = pl.program_id(0)
        g = g_refs[0][...]
        for l in range(1, depth):
            g = jnp.where(layer == l, g_refs[l][...], g)
        delta, mo, vo = _adamw_fn(w_ref[...], g, m_ref[...], v_ref[...])
        go_ref[...], d_ref[...], mo_ref[...], vo_ref[...] = g, delta, mo, vo

    stacked = pl.BlockSpec((None, tm, cs), lambda l, i: (l, i, 0))
    g_specs = [pl.BlockSpec((tm, cs), lambda l, i, k=k: (jnp.where(l == k, i, jnp.where(l < k, 0, nb - 1)), 0)) for k in range(depth)]
    return pl.pallas_call(
        body, name="adamw", grid=(depth, nb), in_specs=[stacked] * 3 + g_specs, out_specs=[stacked] * 4,
        out_shape=[jax.ShapeDtypeStruct(w.shape, F32)] * 4, compiler_params=_cparams(("arbitrary", "arbitrary")))(w, m, v, *g_layers)


def _sum4_fn(a, b, c, d):
    return (((a.astype(F32) + b.astype(F32)) + c.astype(F32)) + d.astype(F32),)


def _add2_fn(a, b):
    return (a + b,)


def kernel(x, norm1_g, w_in, b_gate, ssm_lambda_re, ssm_lambda_im, ssm_log_dt, ssm_b_re, ssm_b_im, ssm_c_re, ssm_c_im, ssm_d, w_ssm_glu, w_att_up, conv_w, conv_b, conv_ln_g, conv_ln_b, w_conv_pw2, w_out, norm2_g, w_ffn_in, w_ffn_out, final_g, loss_target, m_norm1_g, m_w_in, m_b_gate, m_ssm_lambda_re, m_ssm_lambda_im, m_ssm_log_dt, m_ssm_b_re, m_ssm_b_im, m_ssm_c_re, m_ssm_c_im, m_ssm_d, m_w_ssm_glu, m_w_att_up, m_conv_w, m_conv_b, m_conv_ln_g, m_conv_ln_b, m_w_conv_pw2, m_w_out, m_norm2_g, m_w_ffn_in, m_w_ffn_out, m_final_g, v_norm1_g, v_w_in, v_b_gate, v_ssm_lambda_re, v_ssm_lambda_im, v_ssm_log_dt, v_ssm_b_re, v_ssm_b_im, v_ssm_c_re, v_ssm_c_im, v_ssm_d, v_w_ssm_glu, v_w_att_up, v_conv_w, v_conv_b, v_conv_ln_g, v_conv_ln_b, v_w_conv_pw2, v_w_out, v_norm2_g, v_w_ffn_in, v_w_ffn_out, v_final_g):
    args = dict(locals())
    wts = {k: args[k] for k in WEIGHTS}
    mom = {k: args["m_" + k] for k in WEIGHTS}
    var = {k: args["v_" + k] for k in WEIGHTS}
    bl, seq, d = x.shape
    n = bl * seq
    depth = norm1_g.shape[0]
    me = _chip_index()

    assert depth == 2, "the exchanges of layer 1 are hidden behind layer 0's kernels"
    first = BIG[:1]
    rest = BIG[1:]

    shards = lambda keys, l: [wts[k][l].astype(BF16) for k in keys]

    def whole(keys, gathered):
        out = {}
        for k, a in zip(keys, gathered):
            _, ks, cs = a.shape
            if k in ROW_SHARDED:
                out[k] = a.reshape(N_CHIPS * ks, cs)
            elif cs < MIN_SHARD_TILE:
                out[k] = a.transpose(1, 0, 2).reshape(ks, N_CHIPS * cs)
            else:
                out[k] = a
        return out

    fill = lambda gathered, own: [_own_slot(g, o) for g, o in zip(gathered, own)]
    own0 = shards(first, 0) + [conv_w]
    gathered = fill(_run_exchange(_GatherShards(own0), "gather_weights"), own0)
    conv_full = gathered[-1].transpose(1, 2, 0, 3).reshape(depth, CONV_WIDTH, -1)
    params = lambda l: dict({k: wts[k][l] for k in SMALL if k not in ("final_g", "conv_w")}, conv_w=conv_full[l])

    own = {"mm_in": shards(rest, 0), "ssm_fwd": shards(first, 1), "mm_ffn_in": shards(rest, 1)}
    xs, s0, hidden, w0 = _layer_fwd(_to_rows(x), whole(first, gathered[:-1]), params(0), bl, {k: _GatherShards(v) for k, v in own.items()},
                                    late=lambda got: whole(rest, fill(got, own["mm_in"])))
    w1 = dict(whole(first, fill(hidden["ssm_fwd"], own["ssm_fwd"])), **whole(rest, fill(hidden["mm_ffn_in"], own["mm_ffn_in"])))
    full = [w0, w1]
    xs, s1, _, _ = _layer_fwd(xs, full[1], params(1), bl, {})
    dx, sq, dgf = _rowwise(_loss_fn, "loss_head", n, [("row", xs, d, 0), ("par", _row(final_g)), ("row", _to_rows(loss_target), d, 0)],
                           [(d, F32)], [d, d])
    loss = lax.psum(0.5 * jnp.sum(sq) / d, ("x", "y", "c"))

    pieces = lambda bufs, keys: [tuple(b.reshape(N_CHIPS, -1, b.shape[-1]) for b in bufs[k]) for k in keys]
    dx, g1, bufs1, _ = _layer_bwd(dx, s1, full[1], params(1), bl, {})
    p32_1, p16_1 = _reduce_prepare(pieces(bufs1, BIG))
    dx, g0, bufs0, hidden = _layer_bwd(dx, s0, full[0], params(0), bl,
                                       {"ssm_bwd": _ScatterPieces(p16_1[:1]), "conv_bwd": _ScatterPieces(p16_1[1:])})
    red1 = _reduce_finish(p32_1, list(hidden["ssm_bwd"]) + list(hidden["conv_bwd"]))
    p32_0, p16_0 = _reduce_prepare(pieces(bufs0, BIG))
    red0 = _reduce_finish(p32_0, _run_exchange(_ScatterPieces(p16_0), "rs_scatter"))
    grads = {"final_g": dgf[0]}
    for k in SMALL:
        if k != "final_g":
            grads[k] = jnp.stack([g0[k], g1[k]])
    grad_x = _from_rows(dx, bl)
    outs = {}
    for k, r0, r1 in zip(BIG, red0, red1):
        for tag, a in zip(("grad", "delta", "m", "v"), _adamw_layers(wts[k], [r0, r1], mom[k], var[k])):
            outs[tag, k] = a

    def flat1(t):
        v = jnp.concatenate([t[k].reshape(-1) for k in SMALL])
        rows = -(-v.size // (8 * LANES)) * 8
        return _pad_rows(v, rows * LANES).reshape(rows, LANES), rows

    def unflat1(flat, shapes):
        out, off, v = {}, 0, flat.reshape(-1)
        for k in SMALL:
            size = math.prod(shapes[k])
            out[k] = v[off:off + size].reshape(shapes[k])
            off += size
        return out

    grads["conv_w"] = grads["conv_w"][:, :CONV_WIDTH]
    gs, rows = flat1(grads)
    chip_sum, = _flat_fn(_add2_fn, "ar_add", [gs, _swap_sibling(gs, "ar_swap")], 1, rows)
    slots = _chip_allgather(chip_sum, "ar_gather")
    gs_red, = _flat_fn(_sum4_fn, "ar_sum", [slots[j] for j in range(N_CHIPS)], 1, rows)
    g_sm = unflat1(gs_red, {k: grads[k].shape for k in SMALL})
    cs = conv_w.shape[2]
    g_sm["conv_w"] = lax.dynamic_slice_in_dim(g_sm["conv_w"], me * cs, cs, axis=2)
    (w1, rows), (g1, _), (m1, _), (v1, _) = flat1(wts), flat1(g_sm), flat1(mom), flat1(var)
    sm_out = _flat_fn(_adamw_fn, "adamw_small", [w1, g1, m1, v1], 3, rows)
    shapes = {k: wts[k].shape for k in SMALL}
    for tag, a in zip(("delta", "m", "v"), sm_out):
        for k, t in unflat1(a, shapes).items():
            outs[tag, k] = t
    for k in SMALL:
        outs["grad", k] = g_sm[k]
    return (loss, grad_x, *[outs["grad", k] for k in WEIGHTS], *[outs["delta", k] for k in WEIGHTS],
            *[outs["m", k] for k in WEIGHTS], *[outs["v", k] for k in WEIGHTS])
```

```python
import functools
import math

import jax
import jax.numpy as jnp
from jax import lax
from jax.experimental import pallas as pl
from jax.experimental.pallas import tpu as pltpu

F32 = jnp.float32
BF16 = jnp.bfloat16
VMEM_LIMIT = 56 * 1024 * 1024


def _cparams(sem):
    return pltpu.CompilerParams(dimension_semantics=sem, vmem_limit_bytes=VMEM_LIMIT)


_DIMS = {"nn": (((1,), (0,)), ((), ())), "nt": (((1,), (1,)), ((), ())), "tn": (((0,), (0,)), ((), ()))}


MM_ROWS = 1024
MM_DW_VMEM_BYTES = 44 * 1024 * 1024
MM_SMALL_STEP = 1024 * 1024 * 512


def _div_tile(n, cap):
    best = None
    for t in range(128, min(n, cap) + 1, 128):
        if n % t == 0:
            best = t
    return best or n


def _mm(a, b, form, out_dtype, name, res=None, comm=None):
    sharded = b.ndim == 3
    kdim, cs = b.shape[-2], b.shape[-1]
    s = b.shape[0] if sharded else 1
    m = a.shape[0]
    tm = MM_ROWS if m % MM_ROWS == 0 else _div_tile(m, MM_ROWS)
    if form == "nn":
        n, kd = s * cs, kdim
        tn, tk = _div_tile(cs, 1792), _div_tile(kdim, 2048)
        per = cs // tn
        b_blk = (tk, tn)
        b_idx = (lambda i, j, k: (j // per, k, j % per)) if sharded else (lambda i, j, k: (k, j))
    else:
        n, kd = kdim, s * cs
        tn, tk = _div_tile(kdim, 1408), _div_tile(cs, 1792)
        per = cs // tk
        b_blk = (tn, tk)
        b_idx = (lambda i, j, k: (k // per, j, k % per)) if sharded else (lambda i, j, k: (j, k))
    nk = kd // tk
    if tm * tn * tk <= MM_SMALL_STEP and m % (2 * tm) == 0:
        tm *= 2
    a_spec = pl.BlockSpec((tm, tk), lambda i, j, k: (i, k))
    b_spec = pl.BlockSpec(((None,) + b_blk) if sharded else b_blk, b_idx)
    o_spec = pl.BlockSpec((tm, tn), lambda i, j, k: (i, j))
    dims = _DIMS[form]

    def body(*refs):
        a_ref, b_ref = refs[:2]
        r_ref = refs[2] if res is not None else None
        o_ref = refs[3] if res is not None else refs[2]
        p = lax.dot_general(a_ref[...].astype(BF16), b_ref[...], dims, preferred_element_type=F32)

        def finish(r):
            if r_ref is not None:
                r = r + r_ref[...]
            o_ref[...] = r.astype(out_dtype)

        if nk == 1:
            finish(p)
            return
        acc = refs[-1]
        k = pl.program_id(2)

        @pl.when(k == 0)
        def _():
            acc[...] = p

        @pl.when(k > 0)
        def _():
            acc[...] += p

        @pl.when(k == nk - 1)
        def _():
            finish(acc[...])

    ins = [a, b] + ([] if res is None else [res])
    in_specs = [a_spec, b_spec] + ([] if res is None else [o_spec])
    out = _pcall(body, name, (m // tm, n // tn, nk), in_specs, [o_spec], [jax.ShapeDtypeStruct((m, n), out_dtype)],
                 [pltpu.VMEM((tm, tn), F32)] if nk > 1 else [], ("parallel", "parallel", "arbitrary"), ins, comm)
    return out[0] if comm is None else (out[0][0], out[1])


def _mm_dw(a, dy, name, shards):
    r, m = a.shape
    c = dy.shape[1]
    cs = c // shards
    tm, tn = _div_tile(m, 1408), _div_tile(cs, 1408)
    fixed = tm * tn * (4 + 2 * (4 + 2))
    per_row = 2 * (tm * a.dtype.itemsize + tn * dy.dtype.itemsize)
    tk = max(t for t in (256, 512, 1024, 2048) if r % t == 0 and (t == 256 or fixed + t * per_row <= MM_DW_VMEM_BYTES))
    per = cs // tn
    nk = r // tk

    def body(a_ref, b_ref, o32, o16, acc):
        k = pl.program_id(2)
        p = lax.dot_general(a_ref[...].astype(BF16), b_ref[...].astype(BF16), _DIMS["tn"], preferred_element_type=F32)

        @pl.when(k == 0)
        def _():
            acc[...] = p

        @pl.when(k > 0)
        def _():
            acc[...] += p

        @pl.when(k == nk - 1)
        def _():
            o32[...] = acc[...]
            o16[...] = acc[...].astype(BF16)

    o_spec = pl.BlockSpec((None, tm, tn), lambda i, j, k: (j // per, i, j % per))
    shape = (shards, m, cs)
    in_specs = [pl.BlockSpec((tk, tm), lambda i, j, k: (k, i)), pl.BlockSpec((tk, tn), lambda i, j, k: (k, j))]
    return _pcall(body, name, (m // tm, c // tn, nk), in_specs, [o_spec, o_spec],
                  [jax.ShapeDtypeStruct(shape, F32), jax.ShapeDtypeStruct(shape, BF16)], [pltpu.VMEM((tm, tn), F32)],
                  ("parallel", "parallel", "arbitrary"), [a, dy])


def _core_index():
    return lax.axis_index("c")


def _chip_index():
    return 2 * lax.axis_index("x") + lax.axis_index("y")


def _rowwise(fn, name, n_rows, ins, outs, accs=(), tm=512):
    n_in, n_out = len(ins), len(outs)
    in_specs, args = [], []
    for spec in ins:
        if spec[0] == "row":
            _, arr, w, cb = spec
            in_specs.append(pl.BlockSpec((tm, w), lambda i, cb=cb: (i, cb)))
        else:
            arr = spec[1]
            in_specs.append(pl.BlockSpec(arr.shape, lambda i: (0, 0)))
        args.append(arr)
    out_specs = [pl.BlockSpec((tm, w), lambda i: (i, 0)) for w, _ in outs]
    out_specs += [pl.BlockSpec((1, w), lambda i: (0, 0)) for w in accs]
    out_shape = [jax.ShapeDtypeStruct((n_rows, w), dt) for w, dt in outs]
    out_shape += [jax.ShapeDtypeStruct((1, w), F32) for w in accs]

    def body(*refs):
        i = pl.program_id(0)
        res = fn(*[r[...] for r in refs[:n_in]])
        for o_ref, r in zip(refs[n_in:n_in + n_out], res[:n_out]):
            o_ref[...] = r.astype(o_ref.dtype)
        for a_ref, r in zip(refs[n_in + n_out:], res[n_out:]):
            @pl.when(i == 0)
            def _(a_ref=a_ref, r=r):
                a_ref[...] = r

            @pl.when(i > 0)
            def _(a_ref=a_ref, r=r):
                a_ref[...] += r

    return pl.pallas_call(
        body, name=name, grid=(n_rows // tm,), in_specs=in_specs, out_specs=out_specs, out_shape=out_shape,
        compiler_params=_cparams(("arbitrary",)))(*args)


EPS = 1e-6


def _sig(x):
    return 1.0 / (1.0 + jnp.exp(-x))


def _colsum(x):
    return jnp.sum(x, axis=0, keepdims=True)


def _rms_fwd_fn(x, g):
    r = lax.rsqrt(jnp.mean(x * x, axis=-1, keepdims=True) + EPS)
    return (x * r * g,)


def _rms_bwd_fn(x, g, dh, dres):
    dh = dh.astype(F32)
    r = lax.rsqrt(jnp.mean(x * x, axis=-1, keepdims=True) + EPS)
    xh = x * r
    dyg = dh * g
    dx = r * (dyg - xh * jnp.mean(dyg * xh, axis=-1, keepdims=True)) + dres
    return dx, _colsum(dh * xh)


def _loss_fn(x, g, t):
    d = x.shape[-1]
    r = lax.rsqrt(jnp.mean(x * x, axis=-1, keepdims=True) + EPS)
    xh = x * r
    err = xh * g - t
    dy = err * (1.0 / d)
    dyg = dy * g
    dx = r * (dyg - xh * jnp.mean(dyg * xh, axis=-1, keepdims=True))
    return dx, _colsum(err * err), _colsum(dy * xh)


def _swiglu_fwd_fn(z):
    f = z.shape[-1] // 2
    z1, z2 = z[:, :f].astype(F32), z[:, f:].astype(F32)
    return (z1 * _sig(z1) * z2,)


def _swiglu_bwd_fn(z, da):
    f = z.shape[-1] // 2
    z1, z2, da = z[:, :f].astype(F32), z[:, f:].astype(F32), da.astype(F32)
    s = _sig(z1)
    dz1 = da * z2 * (s * (1.0 + z1 * (1.0 - s)))
    dz2 = da * (z1 * s)
    return (jnp.concatenate([dz1, dz2], axis=1),)


def _merge_fwd_fn(g0, g1, g2, bg, zs, ya, yc):
    d = ya.shape[-1]
    bg = bg.astype(F32)
    zs = zs.astype(F32)
    ys = zs[:, :d] * _sig(zs[:, d:])
    m = _sig(g0.astype(F32) + bg[:, :d]) * ys
    m = m + _sig(g1.astype(F32) + bg[:, d:2 * d]) * ya.astype(F32)
    m = m + _sig(g2.astype(F32) + bg[:, 2 * d:]) * yc.astype(F32)
    return (m,)


def _merge_bwd_fn(g0, g1, g2, bg, zs, ya, yc, dm):
    d = ya.shape[-1]
    bg = bg.astype(F32)
    zs = zs.astype(F32)
    dm = dm.astype(F32)
    z1, s2 = zs[:, :d], _sig(zs[:, d:])
    ys = z1 * s2
    s0 = _sig(g0.astype(F32) + bg[:, :d])
    s1 = _sig(g1.astype(F32) + bg[:, d:2 * d])
    s3 = _sig(g2.astype(F32) + bg[:, 2 * d:])
    dgl = jnp.concatenate([dm * ys * s0 * (1.0 - s0), dm * ya.astype(F32) * s1 * (1.0 - s1),
                           dm * yc.astype(F32) * s3 * (1.0 - s3)], axis=1)
    dys = dm * s0
    dzs = jnp.concatenate([dys * s2, dys * z1 * s2 * (1.0 - s2)], axis=1)
    return dgl, dzs, dm * s1, dm * s3, _colsum(dgl)


def _combine_fwd_fn(o0, o1, o2, l0, l1, l2):
    m = jnp.maximum(jnp.maximum(l0, l1), l2)
    e0, e1, e2 = jnp.exp(l0 - m), jnp.exp(l1 - m), jnp.exp(l2 - m)
    den = e0 + e1 + e2
    return (e0 * o0.astype(F32) + e1 * o1.astype(F32) + e2 * o2.astype(F32)) / den, m + jnp.log(den)


ADAM_LR, ADAM_B1, ADAM_B2, ADAM_EPS, ADAM_WD, ADAM_STEP = 0.001, 0.9, 0.999, 1e-08, 0.01, 10


def _adamw_fn(w, g, m, v):
    m = ADAM_B1 * m + (1.0 - ADAM_B1) * g
    v = ADAM_B2 * v + (1.0 - ADAM_B2) * (g * g)
    m_hat = m / (1.0 - ADAM_B1 ** ADAM_STEP)
    v_hat = v / (1.0 - ADAM_B2 ** ADAM_STEP)
    delta = -ADAM_LR * (m_hat / (jnp.sqrt(v_hat) + ADAM_EPS) + ADAM_WD * w)
    return delta, m, v


CONV_WIDTH = 31


def _conv_fwd(proj, cb, w32, conv_b, ln_g, ln_b, bl, c, name, tm=512):
    n = proj.shape[0]
    hp = (CONV_WIDTH - 1) * bl
    nt = n // tm

    def body(ap_ref, gp_ref, a_ref, g_ref, w_ref, cb_ref, lg_ref, lb_ref, hc_ref, hconv_ref, ext):
        i = pl.program_id(0)
        ext[pl.ds(hp, tm), :] = a_ref[...].astype(F32) * _sig(g_ref[...].astype(F32))
        hgp = ap_ref[pl.ds(tm - hp, hp), :].astype(F32) * _sig(gp_ref[pl.ds(tm - hp, hp), :].astype(F32))
        ext[pl.ds(0, hp), :] = jnp.where(i > 0, hgp, 0.0)
        acc = jnp.zeros((tm, c), F32) + cb_ref[...]
        for j in range(CONV_WIDTH):
            acc = acc + w_ref[j:j + 1, :] * ext[pl.ds(j * bl, tm), :]
        hconv_ref[...] = acc.astype(hconv_ref.dtype)
        h = hconv_ref[...].astype(F32)
        mu = jnp.mean(h, axis=-1, keepdims=True)
        xc = h - mu
        var = jnp.mean(xc * xc, axis=-1, keepdims=True)
        hn = xc * lax.rsqrt(var + EPS) * lg_ref[...] + lb_ref[...]
        hc_ref[...] = (hn * _sig(hn)).astype(hc_ref.dtype)

    prev = lambda i, k: (jnp.maximum(i - 1, 0), k)
    par = lambda arr: pl.BlockSpec(arr.shape, lambda i: (0, 0))
    return pl.pallas_call(
        body, name=name, grid=(nt,),
        in_specs=[pl.BlockSpec((tm, c), functools.partial(prev, k=cb)), pl.BlockSpec((tm, c), functools.partial(prev, k=cb + 1)),
                  pl.BlockSpec((tm, c), lambda i: (i, cb)), pl.BlockSpec((tm, c), lambda i: (i, cb + 1)),
                  par(w32), par(conv_b), par(ln_g), par(ln_b)],
        out_specs=[pl.BlockSpec((tm, c), lambda i: (i, 0))] * 2,
        out_shape=[jax.ShapeDtypeStruct((n, c), BF16)] * 2,
        scratch_shapes=[pltpu.VMEM((hp + tm, c), F32)],
        compiler_params=_cparams(("arbitrary",)))(proj, proj, proj, proj, w32, conv_b, ln_g, ln_b)


def _conv_bwd(proj, cb, dhc, hconv, w32, ln_g, ln_b, bl, c, name, tm=512, comm=None):
    n = proj.shape[0]
    hp = (CONV_WIDTH - 1) * bl
    nt = n // tm

    def ln_bwd(d, h, lg, lb):
        d, h = d.astype(F32), h.astype(F32)
        mu = jnp.mean(h, axis=-1, keepdims=True)
        xc = h - mu
        rstd = lax.rsqrt(jnp.mean(xc * xc, axis=-1, keepdims=True) + EPS)
        xh = xc * rstd
        hn = xh * lg + lb
        s = _sig(hn)
        dhn = d * (s * (1.0 + hn * (1.0 - s)))
        dxh = dhn * lg
        dh = rstd * (dxh - jnp.mean(dxh, axis=-1, keepdims=True) - xh * jnp.mean(dxh * xh, axis=-1, keepdims=True))
        return dh, dhn, xh

    def body(ap_ref, gp_ref, a_ref, g_ref, d_ref, dn_ref, h_ref, hn_ref, w_ref, lg_ref, lb_ref,
             dcv_ref, dw_ref, dcb_ref, dlg_ref, dlb_ref, ext_h, ext_d):
        i = pl.program_id(0)
        lg, lb = lg_ref[...], lb_ref[...]
        a, g = a_ref[...].astype(F32), g_ref[...].astype(F32)
        sg = _sig(g)
        ext_h[pl.ds(hp, tm), :] = a * sg
        hgp = ap_ref[pl.ds(tm - hp, hp), :].astype(F32) * _sig(gp_ref[pl.ds(tm - hp, hp), :].astype(F32))
        ext_h[pl.ds(0, hp), :] = jnp.where(i > 0, hgp, 0.0)
        dh, dhn, xh = ln_bwd(d_ref[...], h_ref[...], lg, lb)
        ext_d[pl.ds(0, tm), :] = dh
        dh_n, _, _ = ln_bwd(dn_ref[pl.ds(0, hp), :], hn_ref[pl.ds(0, hp), :], lg, lb)
        ext_d[pl.ds(tm, hp), :] = jnp.where(i < nt - 1, dh_n, 0.0)

        @pl.when(i == 0)
        def _():
            dw_ref[...] = jnp.zeros_like(dw_ref)
            dcb_ref[...] = jnp.zeros_like(dcb_ref)
            dlg_ref[...] = jnp.zeros_like(dlg_ref)
            dlb_ref[...] = jnp.zeros_like(dlb_ref)

        dcb_ref[...] += _colsum(dh)
        dlg_ref[...] += _colsum(dhn * xh)
        dlb_ref[...] += _colsum(dhn)
        dhg = jnp.zeros((tm, c), F32)
        for j in range(CONV_WIDTH):
            dhg = dhg + w_ref[j:j + 1, :] * ext_d[pl.ds((CONV_WIDTH - 1 - j) * bl, tm), :]
            dw_ref[j:j + 1, :] += _colsum(dh * ext_h[pl.ds(j * bl, tm), :])
        dcv_ref[...] = jnp.concatenate([dhg * sg, dhg * a * sg * (1.0 - sg)], axis=1).astype(dcv_ref.dtype)

    prev = lambda i, k: (jnp.maximum(i - 1, 0), k)
    nxt = lambda i: (jnp.minimum(i + 1, nt - 1), 0)
    cur = lambda i: (i, 0)
    par = lambda arr: pl.BlockSpec(arr.shape, lambda i: (0, 0))
    acc = lambda r: pl.BlockSpec((r, c), lambda i: (0, 0))
    in_specs = [pl.BlockSpec((tm, c), functools.partial(prev, k=cb)), pl.BlockSpec((tm, c), functools.partial(prev, k=cb + 1)),
                pl.BlockSpec((tm, c), lambda i: (i, cb)), pl.BlockSpec((tm, c), lambda i: (i, cb + 1)),
                pl.BlockSpec((tm, c), cur), pl.BlockSpec((tm, c), nxt), pl.BlockSpec((tm, c), cur), pl.BlockSpec((tm, c), nxt),
                par(w32), par(ln_g), par(ln_b)]
    out_shape = [jax.ShapeDtypeStruct((n, 2 * c), BF16), jax.ShapeDtypeStruct((32, c), F32)] + [jax.ShapeDtypeStruct((1, c), F32)] * 3
    return _pcall(body, name, (nt,), in_specs, [pl.BlockSpec((tm, 2 * c), cur), acc(32), acc(1), acc(1), acc(1)], out_shape,
                  [pltpu.VMEM((hp + tm, c), F32), pltpu.VMEM((hp + tm, c), F32)], ("arbitrary",),
                  [proj, proj, proj, proj, dhc, dhc, hconv, hconv, w32, ln_g, ln_b], comm)


SSM_CH = 128
_GELU_C = 0.7978845608028654


def _gelu(x):
    return 0.5 * x * (1.0 + jnp.tanh(_GELU_C * (x + 0.044715 * x * x * x)))


def _gelu_grad(x):
    th = jnp.tanh(_GELU_C * (x + 0.044715 * x * x * x))
    return 0.5 * (1.0 + th) + 0.5 * x * (1.0 - th * th) * (_GELU_C * (1.0 + 3.0 * 0.044715 * x * x))


def _ssm_disc(lam_re, lam_im, log_dt, b_re, b_im):
    dt = jnp.exp(log_dt)[:, None]
    mag = jnp.exp(lam_re * dt)
    ab_re = mag * jnp.cos(lam_im * dt)
    ab_im = mag * jnp.sin(lam_im * dt)
    nr, ni = ab_re - 1.0, ab_im
    den = lam_re * lam_re + lam_im * lam_im
    z_re = ((nr * lam_re + ni * lam_im) / den)[..., None]
    z_im = ((ni * lam_re - nr * lam_im) / den)[..., None]
    return ab_re, ab_im, z_re * b_re - z_im * b_im, z_re * b_im + z_im * b_re


def _ssm_pack(ab_re, ab_im, bb_re, bb_im, c_re, c_im):
    g, p, h = bb_re.shape
    gc = SSM_CH // h
    nc = g // gc
    eye = jnp.eye(gc, dtype=F32)
    blk = lambda x: jnp.einsum("qgph,gk->qghkp", x.reshape(nc, gc, p, h), eye).reshape(nc, gc * h, gc * p)
    bbd = jnp.concatenate([blk(bb_re), blk(bb_im)], axis=2).astype(BF16)
    blc = lambda x: jnp.einsum("qghp,gk->qgpkh", x.reshape(nc, gc, h, p), eye).reshape(nc, gc * p, gc * h)
    cdm = jnp.concatenate([blc(c_re), blc(-c_im)], axis=1).astype(BF16)
    a = jnp.concatenate([ab_re.reshape(nc, gc * p), ab_im.reshape(nc, gc * p)], axis=1)
    a8 = jnp.broadcast_to(a[:, None, :], (nc, 8, 2 * gc * p)).reshape(nc * 8, 2 * gc * p)
    return bbd, cdm, a8


def _ssm_unpack(dbb, dcd, da, g, p, h):
    gc = SSM_CH // h
    nc = g // gc
    ph = gc * p
    eye = jnp.eye(gc, dtype=F32)
    dia = lambda x, o: jnp.einsum("qgpkh,gk->" + o, x.reshape(nc, gc, p, gc, h), eye).reshape((g, p, h) if o == "qgph" else (g, h, p))
    das = da.reshape(nc, 8, 2 * ph).sum(axis=1)
    return (das[:, :ph].reshape(g, p), das[:, ph:].reshape(g, p), dia(dbb[:, :ph], "qgph"), dia(dbb[:, ph:], "qgph"),
            dia(dcd[:, :ph], "qghp"), -dia(dcd[:, ph:], "qghp"))


def _ssm_fwd(proj, bbd, cdm, a8, dskip, bl, name, tm=1024, comm=None):
    n = proj.shape[0]
    nc, ch, p2 = bbd.shape
    ph = p2 // 2
    nt = n // tm
    nsub = 8 // bl

    def body(u_ref, bb_ref, cd_ref, a_ref, d_ref, ypre_ref, yg_ref, s_ref, bu, carry):
        t = pl.program_id(1)

        @pl.when(t == 0)
        def _():
            carry[...] = jnp.zeros_like(carry)

        u = u_ref[...]
        bu[...] = jnp.dot(u, bb_ref[0], preferred_element_type=F32)
        a_re, a_im = a_ref[:, :ph], a_ref[:, ph:]
        row = lax.broadcasted_iota(jnp.int32, (8, ph), 0)

        def step(k, c):
            cre, cim = c
            r0 = pl.multiple_of(k * 8, 8)
            bre, bim = bu[pl.ds(r0, 8), :ph], bu[pl.ds(r0, 8), ph:]
            sre, sim = cre, cim
            for sub in range(nsub):
                xre, xim = pltpu.roll(cre, bl, 0), pltpu.roll(cim, bl, 0)
                cre = a_re * xre - a_im * xim + bre
                cim = a_re * xim + a_im * xre + bim
                if sub == 0:
                    sre, sim = cre, cim
                else:
                    sel = row >= sub * bl
                    sre, sim = jnp.where(sel, cre, sre), jnp.where(sel, cim, sim)
            bu[pl.ds(r0, 8), :ph] = sre
            bu[pl.ds(r0, 8), ph:] = sim
            return sre, sim

        cre, cim = lax.fori_loop(0, tm // 8, step, (carry[:, :ph], carry[:, ph:]))
        carry[:, :ph] = cre
        carry[:, ph:] = cim
        s16 = bu[...].astype(BF16)
        s_ref[...] = s16
        y = jnp.dot(s16, cd_ref[0], preferred_element_type=F32) + d_ref[...] * u.astype(F32)
        ypre_ref[...] = y
        yg_ref[...] = _gelu(y).astype(yg_ref.dtype)

    in_specs = [pl.BlockSpec((tm, ch), lambda q, t: (t, q)), pl.BlockSpec((1, ch, p2), lambda q, t: (q, 0, 0)),
                pl.BlockSpec((1, p2, ch), lambda q, t: (q, 0, 0)), pl.BlockSpec((8, p2), lambda q, t: (q, 0)),
                pl.BlockSpec((1, ch), lambda q, t: (0, q))]
    out_specs = [pl.BlockSpec((tm, ch), lambda q, t: (t, q)), pl.BlockSpec((tm, ch), lambda q, t: (t, q)),
                 pl.BlockSpec((tm, p2), lambda q, t: (t, q))]
    out_shape = [jax.ShapeDtypeStruct((n, nc * ch), F32), jax.ShapeDtypeStruct((n, nc * ch), BF16),
                 jax.ShapeDtypeStruct((n, nc * p2), BF16)]
    return _pcall(body, name, (nc, nt), in_specs, out_specs, out_shape, [pltpu.VMEM((tm, p2), F32), pltpu.VMEM((8, p2), F32)],
                  ("parallel", "arbitrary"), [proj, bbd, cdm, a8, dskip], comm)


def _ssm_bwd(dyg, ypre, proj, s_all, cdt, bbt, a8, dskip, bl, name, tm=1024, comm=None):
    n = proj.shape[0]
    nc, ch, p2 = cdt.shape
    ph = p2 // 2
    nt = n // tm
    nsub = 8 // bl
    tn_dims = (((0,), (0,)), ((), ()))

    def body(dyg_ref, ypre_ref, u_ref, s_ref, cdt_ref, bbt_ref, a_ref, d_ref,
             du_ref, dbb_ref, dcd_ref, da_ref, dd_ref, ds, s32, carry):
        t = pl.program_id(1)

        @pl.when(t == 0)
        def _():
            carry[...] = jnp.zeros_like(carry)
            dbb_ref[...] = jnp.zeros_like(dbb_ref)
            dcd_ref[...] = jnp.zeros_like(dcd_ref)
            da_ref[...] = jnp.zeros_like(da_ref)
            dd_ref[...] = jnp.zeros_like(dd_ref)

        dyp = dyg_ref[...].astype(F32) * _gelu_grad(ypre_ref[...])
        u = u_ref[...]
        dd_ref[...] += _colsum(dyp * u.astype(F32))
        dyp16 = dyp.astype(BF16)
        ds[...] = jnp.dot(dyp16, cdt_ref[0], preferred_element_type=F32)
        s16 = s_ref[...]
        s32[...] = s16.astype(F32)
        a_re, a_im = a_ref[:, :ph], a_ref[:, ph:]
        row = lax.broadcasted_iota(jnp.int32, (8, ph), 0)
        back = 8 - bl

        def step(kk, c):
            lre, lim, acr, aci = c
            r0 = pl.multiple_of((tm // 8 - 1 - kk) * 8, 8)
            dre, dim = ds[pl.ds(r0, 8), :ph], ds[pl.ds(r0, 8), ph:]
            sre, sim = s32[pl.ds(r0, 8), :ph], s32[pl.ds(r0, 8), ph:]
            ore, oim, ire, iim = lre, lim, lre, lim
            for sub in range(nsub - 1, -1, -1):
                xre, xim = pltpu.roll(lre, back, 0), pltpu.roll(lim, back, 0)
                lre = a_re * xre + a_im * xim + dre
                lim = a_re * xim - a_im * xre + dim
                if sub == nsub - 1:
                    ore, oim, ire, iim = lre, lim, xre, xim
                else:
                    sel = row < (sub + 1) * bl
                    ore, oim = jnp.where(sel, lre, ore), jnp.where(sel, lim, oim)
                    ire, iim = jnp.where(sel, xre, ire), jnp.where(sel, xim, iim)
            ds[pl.ds(r0, 8), :ph] = ore
            ds[pl.ds(r0, 8), ph:] = oim
            acr = acr + sre * ire + sim * iim
            aci = aci + sre * iim - sim * ire
            return ore, oim, acr, aci

        z = jnp.zeros((8, ph), F32)
        lre, lim, acr, aci = lax.fori_loop(0, tm // 8, step, (carry[:, :ph], carry[:, ph:], z, z))
        carry[:, :ph] = lre
        carry[:, ph:] = lim
        da_ref[:, :ph] += acr
        da_ref[:, ph:] += aci
        lam16 = ds[...].astype(BF16)
        du = jnp.dot(lam16, bbt_ref[0], preferred_element_type=F32) + d_ref[...] * dyp
        du_ref[...] = du.astype(du_ref.dtype)
        dbb_ref[0] += lax.dot_general(lam16, u, tn_dims, preferred_element_type=F32)
        dcd_ref[0] += lax.dot_general(s16, dyp16, tn_dims, preferred_element_type=F32)

    rev = lambda q, t: (nt - 1 - t, q)
    in_specs = [pl.BlockSpec((tm, ch), rev), pl.BlockSpec((tm, ch), rev), pl.BlockSpec((tm, ch), rev),
                pl.BlockSpec((tm, p2), rev), pl.BlockSpec((1, ch, p2), lambda q, t: (q, 0, 0)),
                pl.BlockSpec((1, p2, ch), lambda q, t: (q, 0, 0)), pl.BlockSpec((8, p2), lambda q, t: (q, 0)),
                pl.BlockSpec((1, ch), lambda q, t: (0, q))]
    out_specs = [pl.BlockSpec((tm, ch), rev), pl.BlockSpec((1, p2, ch), lambda q, t: (q, 0, 0)),
                 pl.BlockSpec((1, p2, ch), lambda q, t: (q, 0, 0)), pl.BlockSpec((8, p2), lambda q, t: (q, 0)),
                 pl.BlockSpec((1, ch), lambda q, t: (0, q))]
    out_shape = [jax.ShapeDtypeStruct((n, nc * ch), BF16), jax.ShapeDtypeStruct((nc, p2, ch), F32),
                 jax.ShapeDtypeStruct((nc, p2, ch), F32), jax.ShapeDtypeStruct((nc * 8, p2), F32),
                 jax.ShapeDtypeStruct((1, nc * ch), F32)]
    return _pcall(body, name, (nc, nt), in_specs, out_specs, out_shape,
                  [pltpu.VMEM((tm, p2), F32), pltpu.VMEM((tm, p2), F32), pltpu.VMEM((8, p2), F32)],
                  ("parallel", "arbitrary"), [dyg, ypre, proj, s_all, cdt, bbt, a8, dskip], comm)


_MESH = pl.DeviceIdType.MESH
_HBM = pl.BlockSpec(memory_space=pltpu.HBM)


def _position():
    return lax.axis_index("x"), lax.axis_index("y"), lax.axis_index("c")


def _other_chips(x, y):
    return [((1 - x, y), 2 * (1 - x) + y), ((x, 1 - y), 2 * x + 1 - y), ((1 - x, 1 - y), 2 * (1 - x) + 1 - y)]


def _swap_sibling(v, name):
    def body(v_ref, got_ref, send_sem, recv_sem):
        x, y, c = _position()
        cp = pltpu.make_async_remote_copy(src_ref=v_ref, dst_ref=got_ref, send_sem=send_sem, recv_sem=recv_sem,
                                          device_id=(x, y, 1 - c), device_id_type=_MESH)
        cp.start()
        cp.wait()

    return pl.pallas_call(
        body, name=name, in_specs=[_HBM], out_specs=_HBM, out_shape=jax.ShapeDtypeStruct(v.shape, v.dtype),
        scratch_shapes=[pltpu.SemaphoreType.DMA, pltpu.SemaphoreType.DMA])(v)


def _own_slot(gathered, own):
    return lax.dynamic_update_index_in_dim(gathered, own, _chip_index(), 0)


def _chip_allgather(v, name):
    def body(v_ref, out_ref, send_sems, recv_sems):
        x, y, c = _position()
        me = 2 * x + y
        sends = []
        for k, (chip, idx) in enumerate(_other_chips(x, y)):
            cp = pltpu.make_async_remote_copy(src_ref=v_ref, dst_ref=out_ref.at[me], send_sem=send_sems.at[k],
                                              recv_sem=recv_sems.at[k], device_id=(*chip, c), device_id_type=_MESH)
            cp.start()
            sends.append(cp)
        for k, (chip, idx) in enumerate(_other_chips(x, y)):
            pltpu.make_async_remote_copy(src_ref=v_ref, dst_ref=out_ref.at[idx], send_sem=send_sems.at[k],
                                         recv_sem=recv_sems.at[k], device_id=(*chip, c), device_id_type=_MESH).wait_recv()
        for cp in sends:
            cp.wait_send()

    out = pl.pallas_call(
        body, name=name, in_specs=[_HBM], out_specs=_HBM, out_shape=jax.ShapeDtypeStruct((4,) + tuple(v.shape), v.dtype),
        scratch_shapes=[pltpu.SemaphoreType.DMA((3,)), pltpu.SemaphoreType.DMA((3,))])(v)
    return _own_slot(out, v)


def _remote(src, dst, send_sems, recv_sems, s, device):
    return pltpu.make_async_remote_copy(src_ref=src, dst_ref=dst, send_sem=send_sems.at[s], recv_sem=recv_sems.at[s],
                                        device_id=device, device_id_type=_MESH)


class _Exchange:
    def __init__(self, ins, out_shapes, n_sems, aliases=None):
        self.ins, self.out_shapes, self.n_sems, self.aliases = list(ins), list(out_shapes), n_sems, aliases or {}

    def sem_shapes(self):
        return [pltpu.SemaphoreType.DMA((self.n_sems,)), pltpu.SemaphoreType.DMA((self.n_sems,))]


def _halves(ref, c, axis=0):
    h = ref.shape[axis] // 2
    idx = (slice(None),) * axis
    return ref.at[idx + (pl.ds(c * h, h),)], ref.at[idx + (pl.ds((1 - c) * h, h),)]


class _GatherShards(_Exchange):
    def __init__(self, ws):
        super().__init__(ws, [jax.ShapeDtypeStruct((N_CHIPS,) + tuple(w.shape), w.dtype) for w in ws], 6 * len(ws))

    def start(self, w_refs, out_refs, sems):
        send_sems, recv_sems = sems
        x, y, c = _position()
        me = 2 * x + y
        for i, (w, out) in enumerate(zip(w_refs, out_refs)):
            for k, (chip, idx) in enumerate(_other_chips(x, y)):
                _remote(_halves(w, c)[0], _halves(out.at[me], c)[0], send_sems, recv_sems, 6 * i + k, (*chip, c)).start()

    def finish(self, w_refs, out_refs, sems):
        send_sems, recv_sems = sems
        x, y, c = _position()
        sibling = (x, y, 1 - c)
        others = _other_chips(x, y)
        for i, out in enumerate(out_refs):
            for k, (chip, idx) in enumerate(others):
                landed = _halves(out.at[idx], c)[0]
                _remote(landed, landed, send_sems, recv_sems, 6 * i + k, (*chip, c)).wait_recv()
                _remote(landed, landed, send_sems, recv_sems, 6 * i + 3 + k, sibling).start()
        for i, (w, out) in enumerate(zip(w_refs, out_refs)):
            for k, (chip, idx) in enumerate(others):
                mine, theirs = _halves(out.at[idx], c)
                _remote(theirs, theirs, send_sems, recv_sems, 6 * i + 3 + k, sibling).wait_recv()
                _remote(mine, mine, send_sems, recv_sems, 6 * i + 3 + k, sibling).wait_send()
                _remote(_halves(w, c)[0], mine, send_sems, recv_sems, 6 * i + k, (*chip, c)).wait_send()


class _SwapHalves(_Exchange):
    def __init__(self, gs):
        shapes = [jax.ShapeDtypeStruct((g.shape[0], g.shape[1] // 2) + tuple(g.shape[2:]), g.dtype) for g in gs]
        super().__init__(gs, shapes, N_CHIPS * len(gs))

    def _copies(self, g_refs, out_refs, sems):
        x, y, c = _position()
        return [_remote(_halves(g.at[j], c)[1], out.at[j], sems[0], sems[1], N_CHIPS * i + j, (x, y, 1 - c))
                for i, (g, out) in enumerate(zip(g_refs, out_refs)) for j in range(N_CHIPS)]

    def start(self, g_refs, out_refs, sems):
        for cp in self._copies(g_refs, out_refs, sems):
            cp.start()

    def finish(self, g_refs, out_refs, sems):
        for cp in self._copies(g_refs, out_refs, sems):
            cp.wait()


class _ScatterPieces(_Exchange):
    def __init__(self, ps):
        super().__init__(ps, [jax.ShapeDtypeStruct((3,) + tuple(p.shape[1:]), p.dtype) for p in ps], 3 * len(ps))

    def _copies(self, p_refs, out_refs, sems):
        x, y, c = _position()
        return [_remote(p.at[idx], out.at[k], sems[0], sems[1], 3 * i + k, (*chip, c))
                for i, (p, out) in enumerate(zip(p_refs, out_refs)) for k, (chip, idx) in enumerate(_other_chips(x, y))]

    def start(self, p_refs, out_refs, sems):
        for cp in self._copies(p_refs, out_refs, sems):
            cp.start()

    def finish(self, p_refs, out_refs, sems):
        for cp in self._copies(p_refs, out_refs, sems):
            cp.wait()


class _JoinHalves(_Exchange):
    def __init__(self, rs):
        super().__init__(rs, [jax.ShapeDtypeStruct(r.shape, r.dtype) for r in rs], len(rs), {i: i for i in range(len(rs))})

    def start(self, r_refs, out_refs, sems):
        x, y, c = _position()
        for i, out in enumerate(out_refs):
            _remote(out.at[c], out.at[c], sems[0], sems[1], i, (x, y, 1 - c)).start()

    def finish(self, r_refs, out_refs, sems):
        x, y, c = _position()
        for i, out in enumerate(out_refs):
            _remote(out.at[c], out.at[c], sems[0], sems[1], i, (x, y, 1 - c)).wait_send()
            _remote(out.at[1 - c], out.at[1 - c], sems[0], sems[1], i, (x, y, 1 - c)).wait_recv()


def _run_exchange(ex, name):
    def body(*refs):
        ins, outs, sems = refs[:len(ex.ins)], refs[len(ex.ins):len(ex.ins) + len(ex.out_shapes)], refs[-2:]
        ex.start(ins, outs, sems)
        ex.finish(ins, outs, sems)

    return pl.pallas_call(body, name=name, in_specs=[_HBM] * len(ex.ins), out_specs=[_HBM] * len(ex.out_shapes),
                          out_shape=ex.out_shapes, scratch_shapes=ex.sem_shapes(), input_output_aliases=ex.aliases)(*ex.ins)


def _pcall(body, name, grid, in_specs, out_specs, out_shape, scratch_shapes, semantics, args, comm=None):
    if comm is None:
        return pl.pallas_call(body, name=name, grid=grid, in_specs=in_specs, out_specs=out_specs, out_shape=out_shape,
                              scratch_shapes=scratch_shapes, compiler_params=_cparams(semantics))(*args)
    n_in, n_out, n_scr, ci, co = len(in_specs), len(out_specs), len(scratch_shapes), len(comm.ins), len(comm.out_shapes)

    def wrapped(*refs):
        parts, a = [], 0
        for k in (n_in, ci, n_out, co, n_scr, 2):
            parts.append(refs[a:a + k])
            a += k
        ins, cins, outs, couts, scr, sems = parts
        ids = [pl.program_id(i) for i in range(len(grid))]
        first = functools.reduce(jnp.logical_and, [i == 0 for i in ids])
        last = functools.reduce(jnp.logical_and, [i == g - 1 for i, g in zip(ids, grid)])

        @pl.when(first)
        def _():
            comm.start(cins, couts, sems)

        body(*ins, *outs, *scr)

        @pl.when(last)
        def _():
            comm.finish(cins, couts, sems)

    res = pl.pallas_call(
        wrapped, name=name, grid=grid, in_specs=list(in_specs) + [_HBM] * ci, out_specs=list(out_specs) + [_HBM] * co,
        out_shape=list(out_shape) + comm.out_shapes, scratch_shapes=list(scratch_shapes) + comm.sem_shapes(),
        compiler_params=_cparams(("arbitrary",) * len(grid)))(*args, *comm.ins)
    return res[:n_out], res[n_out:]


ATT_WINDOW = 128
PHASES = 16
_NT = (((1,), (1,)), ((), ()))
_TN = (((0,), (0,)), ((), ()))


PERM_LANES = 512


def _phase_perm(bl):
    t = 16 * PHASES * bl
    col = jnp.arange(t)
    i, r, b = col // (PHASES * bl), (col // bl) % PHASES, col % bl
    return (jnp.arange(t)[:, None] == ((b * PHASES + r) * 16 + i)[None, :]).astype(BF16)


def _to_phase_order(x, bl, col0=0, width=None):
    n = x.shape[0]
    width = width or x.shape[1]
    t = 16 * PHASES * bl
    g = n // bl // PHASES
    tn = min(PERM_LANES, width)

    def body(p_ref, x_ref, o_ref):
        o_ref[...] = jnp.dot(p_ref[...], x_ref[...], preferred_element_type=F32).astype(o_ref.dtype).reshape(o_ref.shape)

    out = pl.pallas_call(
        body, name="to_phase", grid=(n // t, width // tn),
        in_specs=[pl.BlockSpec((t, t), lambda i, j: (0, 0)), pl.BlockSpec((t, tn), lambda i, j: (i, col0 // tn + j))],
        out_specs=pl.BlockSpec((bl * PHASES, 16, tn), lambda i, j: (0, i, j)),
        out_shape=jax.ShapeDtypeStruct((bl * PHASES, g, width), x.dtype),
        compiler_params=_cparams(("parallel", "parallel")))(_phase_perm(bl), x)
    return out.reshape(n, width)


def _from_phase_order(y, bl):
    n, width = y.shape
    t = 16 * PHASES * bl
    g = n // bl // PHASES
    tn = min(PERM_LANES, width)

    def body(p_ref, y_ref, o_ref):
        o_ref[...] = jnp.dot(p_ref[...], y_ref[...].reshape(t, tn), preferred_element_type=F32).astype(o_ref.dtype)

    return pl.pallas_call(
        body, name="from_phase", grid=(n // t, width // tn),
        in_specs=[pl.BlockSpec((t, t), lambda i, j: (0, 0)), pl.BlockSpec((bl * PHASES, 16, tn), lambda i, j: (0, i, j))],
        out_specs=pl.BlockSpec((t, tn), lambda i, j: (i, j)), out_shape=jax.ShapeDtypeStruct((n, width), y.dtype),
        compiler_params=_cparams(("parallel", "parallel")))(_phase_perm(bl).T, y.reshape(bl * PHASES, g, width))


def _att_geometry(p, n, bl):
    g = n // bl // PHASES
    if p == 0:
        return ((bl, PHASES, g), (bl, g // 16), (None, PHASES, 16),
                lambda sh: (lambda b, a: (b, 0, jnp.maximum(a + sh, 0))), 256, 16, lambda ids: ids[1] == 0)
    if p == 1:
        return ((bl, 4, 4, g), (bl, 2, g // 32), (None, 4, 2, 32),
                lambda sh: (lambda b, r, a: (b, 0, r, jnp.maximum(a + sh, 0))), 128, 32, lambda ids: ids[2] == 0)
    return ((bl * PHASES, g), (bl * PHASES // 2,), (2, g), lambda sh: (lambda s: (s, 0)), g, g, None)


def _att_units(p, n, bl):
    g = n // bl // PHASES
    full = slice(None)
    if p == 0:
        return [(full, full)], (PHASES, 16)
    if p == 1:
        return [(full, u, full) for u in range(2)], (4, 32)
    return [(u, full) for u in range(2)], (g,)


def _att_masks(p, qb, chunk):
    def pos(idx):
        return (idx % chunk) * (qb // chunk) + idx // chunk

    dq = pos(lax.broadcasted_iota(jnp.int32, (qb, qb), 0))
    dk = pos(lax.broadcasted_iota(jnp.int32, (qb, qb), 1))
    dist = dq - dk
    return jnp.logical_and(dist >= 0, dist <= ATT_WINDOW), dist + qb <= ATT_WINDOW


def _att_call(p, n, bl, c, body, name, ins, outs):
    prefix, grid, blk, idx_fn, qb, chunk, _ = _att_geometry(p, n, bl)

    def spec(cb, sh):
        f = idx_fn(sh)
        return pl.BlockSpec(blk + (c,), lambda *ids, f=f, cb=cb: f(*ids) + (cb,))

    in_specs = [spec(cb, sh) for _, cb, sh in ins]
    out_specs = [spec(0, 0) for _ in outs]
    out_shape = [jax.ShapeDtypeStruct(prefix + (c,), dt) for dt in outs]
    res = pl.pallas_call(body, name=name, grid=grid, in_specs=in_specs, out_specs=out_specs, out_shape=out_shape,
                         compiler_params=_cparams(("parallel",) * len(grid)))(*[a.reshape(prefix + (a.shape[1],)) for a, _, _ in ins])
    return [r.reshape(n, c) for r in res]


class _AttTiles:
    def __init__(self, p, n, bl, c, first):
        _, _, _, _, qb, chunk, _ = _att_geometry(p, n, bl)
        units, self.unit_shape = _att_units(p, n, bl)
        self.split = p == 0
        self.rows = qb // 2 if self.split else qb
        halves = (0, 1) if self.split else (None,)
        self.tiles = [(u, pl.ds(lt * 128, 128), h) for u in units for lt in range(c // 128) for h in halves]
        self.mask_cur, mp = _att_masks(p, self.rows, chunk // 2 if self.split else chunk)
        gated = mp if first is None else jnp.logical_and(mp, jnp.logical_not(first))
        self.mask_prev = [mp if h == 1 else gated for _, _, h in self.tiles]

    def _half(self, x, h):
        return x.astype(F32)[:, 8 * h:8 * h + 8, :].reshape(self.rows, 128).astype(x.dtype)

    def cur(self, ref, t):
        u, ls, h = self.tiles[t]
        x = ref[u + (ls,)]
        return x.reshape(self.rows, 128) if h is None else self._half(x, h)

    def prev(self, cur_ref, prev_ref, t):
        u, ls, h = self.tiles[t]
        if h is None:
            return prev_ref[u + (ls,)].reshape(self.rows, 128)
        return self._half(prev_ref[u + (ls,)], 1) if h == 0 else self._half(cur_ref[u + (ls,)], 0)

    def store(self, ref, vals):
        if not self.split:
            for (u, ls, _), v in zip(self.tiles, vals):
                ref[u + (ls,)] = v.astype(ref.dtype).reshape(self.unit_shape + (128,))
            return
        for k in range(len(self.tiles) // 2):
            u, ls, _ = self.tiles[2 * k]
            parts = [v.astype(F32).reshape(self.unit_shape[0], 8, 128) for v in vals[2 * k:2 * k + 2]]
            ref[u + (ls,)] = jnp.concatenate(parts, axis=1).astype(ref.dtype)

    def fold_keys(self, cur_vals, prev_vals):
        if not self.split:
            return cur_vals, prev_vals
        own, before = [], []
        for k in range(len(self.tiles) // 2):
            own += [cur_vals[2 * k] + prev_vals[2 * k + 1], cur_vals[2 * k + 1]]
            before += [jnp.zeros_like(prev_vals[2 * k]), prev_vals[2 * k]]
        return own, before


def _att_fwd(p, qkv, qcb, bl, c, heads):
    n = qkv.shape[0]
    _, grid, _, _, _, _, first_fn = _att_geometry(p, n, bl)
    n_grid = len(grid)
    has_prev = first_fn is not None
    e = c // heads
    scale = e ** -0.5

    def body(*refs):
        if has_prev:
            q_ref, kc_ref, kp_ref, vc_ref, vp_ref, o_ref, l_ref = refs
        else:
            q_ref, kc_ref, vc_ref, o_ref, l_ref = refs
            kp_ref = vp_ref = None
        tl = _AttTiles(p, n, bl, c, first_fn([pl.program_id(a) for a in range(n_grid)]) if has_prev else None)
        mc, n_t = tl.mask_cur, len(tl.tiles)
        lo = lax.broadcasted_iota(jnp.int32, (tl.rows, 128), 1) < e
        ones = jnp.ones((tl.rows, 128), BF16)
        items = [(t, h) for t in range(n_t) for h in range(2)]
        dot = functools.partial(jnp.dot, preferred_element_type=F32)
        q2 = [tl.cur(q_ref, t) for t in range(n_t)]
        kc = [tl.cur(kc_ref, t) for t in range(n_t)]
        qm = [jnp.where(lo if h == 0 else jnp.logical_not(lo), q2[t], jnp.zeros_like(q2[t])) for t, h in items]
        sc = [jnp.where(mc, lax.dot_general(qm[i], kc[t], _NT, preferred_element_type=F32) * scale, -jnp.inf)
              for i, (t, h) in enumerate(items)]
        m = [jnp.max(s, axis=1, keepdims=True) for s in sc]
        if has_prev:
            kp = [tl.prev(kc_ref, kp_ref, t) for t in range(n_t)]
            sp = [jnp.where(tl.mask_prev[t], lax.dot_general(qm[i], kp[t], _NT, preferred_element_type=F32) * scale, -jnp.inf)
                  for i, (t, h) in enumerate(items)]
            m = [jnp.maximum(a, jnp.max(s, axis=1, keepdims=True)) for a, s in zip(m, sp)]
        pc = [jnp.exp(s - a).astype(BF16) for s, a in zip(sc, m)]
        vc = [tl.cur(vc_ref, t) for t in range(n_t)]
        acc = [dot(pc[i], vc[t]) for i, (t, h) in enumerate(items)]
        den = [dot(x, ones) for x in pc]
        if has_prev:
            pp = [jnp.exp(s - a).astype(BF16) for s, a in zip(sp, m)]
            vp = [tl.prev(vc_ref, vp_ref, t) for t in range(n_t)]
            acc = [a + dot(pp[i], vp[t]) for i, ((t, h), a) in enumerate(zip(items, acc))]
            den = [d + dot(x, ones) for d, x in zip(den, pp)]
        oh = [a / d for a, d in zip(acc, den)]
        lh = [a + jnp.log(d) for a, d in zip(m, den)]
        tl.store(o_ref, [jnp.where(lo, oh[2 * t], oh[2 * t + 1]) for t in range(n_t)])
        tl.store(l_ref, [jnp.where(lo, lh[2 * t], lh[2 * t + 1]) for t in range(n_t)])

    kcb, vcb = 3, 4
    ins = [(qkv, qcb, 0), (qkv, kcb, 0)] + ([(qkv, kcb, -1)] if has_prev else []) + [(qkv, vcb, 0)] + ([(qkv, vcb, -1)] if has_prev else [])
    return _att_call(p, n, bl, c, body, name=f"att_fwd{p}", ins=ins, outs=[BF16, F32])


def _att_bwd(p, qkv, qcb, o, do, lse, bl, c, heads):
    n = qkv.shape[0]
    _, grid, _, _, _, _, first_fn = _att_geometry(p, n, bl)
    n_grid = len(grid)
    has_prev = first_fn is not None
    e = c // heads
    scale = e ** -0.5

    def body(*refs):
        if has_prev:
            q_ref, kc_ref, kp_ref, vc_ref, vp_ref, o_ref, do_ref, l_ref, dq_ref, dkc_ref, dkp_ref, dvc_ref, dvp_ref = refs
        else:
            q_ref, kc_ref, vc_ref, o_ref, do_ref, l_ref, dq_ref, dkc_ref, dvc_ref = refs
            kp_ref = vp_ref = None
        tl = _AttTiles(p, n, bl, c, first_fn([pl.program_id(a) for a in range(n_grid)]) if has_prev else None)
        mc, n_t = tl.mask_cur, len(tl.tiles)
        lo = lax.broadcasted_iota(jnp.int32, (tl.rows, 128), 1) < e
        items = [(t, h) for t in range(n_t) for h in range(2)]
        nt_dot = lambda a, b: lax.dot_general(a, b, _NT, preferred_element_type=F32)
        tn_dot = lambda a, b: lax.dot_general(a, b, _TN, preferred_element_type=F32)
        dot = functools.partial(jnp.dot, preferred_element_type=F32)
        sel = [lo if h == 0 else jnp.logical_not(lo) for t, h in items]
        q2, kc, vc, do2 = ([tl.cur(r, t) for t in range(n_t)] for r in (q_ref, kc_ref, vc_ref, do_ref))
        qm = [jnp.where(sel[i], q2[t], jnp.zeros_like(q2[t])) for i, (t, h) in enumerate(items)]
        dom = [jnp.where(sel[i], do2[t], jnp.zeros_like(do2[t])) for i, (t, h) in enumerate(items)]
        dod = [do2[t].astype(F32) * tl.cur(o_ref, t).astype(F32) for t in range(n_t)]
        lcol = [tl.cur(l_ref, t)[:, h * e:h * e + 1] for t, h in items]
        corr = [-jnp.sum(jnp.where(sel[i], dod[t], 0.0), axis=1, keepdims=True) for i, (t, h) in enumerate(items)]
        pc = [jnp.exp(jnp.where(mc, nt_dot(qm[i], kc[t]) * scale, -jnp.inf) - lcol[i]) for i, (t, h) in enumerate(items)]
        dsc = [(pc[i] * (nt_dot(dom[i], vc[t]) + corr[i]) * scale).astype(BF16) for i, (t, h) in enumerate(items)]
        pc = [x.astype(BF16) for x in pc]
        dq = [dot(dsc[i], kc[t]) for i, (t, h) in enumerate(items)]
        dkc = [tn_dot(dsc[2 * t], qm[2 * t]) + tn_dot(dsc[2 * t + 1], qm[2 * t + 1]) for t in range(n_t)]
        dvc = [tn_dot(pc[2 * t], dom[2 * t]) + tn_dot(pc[2 * t + 1], dom[2 * t + 1]) for t in range(n_t)]
        if has_prev:
            kp = [tl.prev(kc_ref, kp_ref, t) for t in range(n_t)]
            vp = [tl.prev(vc_ref, vp_ref, t) for t in range(n_t)]
            pp = [jnp.exp(jnp.where(tl.mask_prev[t], nt_dot(qm[i], kp[t]) * scale, -jnp.inf) - lcol[i]) for i, (t, h) in enumerate(items)]
            dsp = [(pp[i] * (nt_dot(dom[i], vp[t]) + corr[i]) * scale).astype(BF16) for i, (t, h) in enumerate(items)]
            pp = [x.astype(BF16) for x in pp]
            dq = [a + dot(dsp[i], kp[t]) for i, ((t, h), a) in enumerate(zip(items, dq))]
            dkp = [tn_dot(dsp[2 * t], qm[2 * t]) + tn_dot(dsp[2 * t + 1], qm[2 * t + 1]) for t in range(n_t)]
            dvp = [tn_dot(pp[2 * t], dom[2 * t]) + tn_dot(pp[2 * t + 1], dom[2 * t + 1]) for t in range(n_t)]
            (dkc, dkp), (dvc, dvp) = tl.fold_keys(dkc, dkp), tl.fold_keys(dvc, dvp)
            tl.store(dkp_ref, dkp)
            tl.store(dvp_ref, dvp)
        tl.store(dq_ref, [jnp.where(lo, dq[2 * t], dq[2 * t + 1]) for t in range(n_t)])
        tl.store(dkc_ref, dkc)
        tl.store(dvc_ref, dvc)

    kcb, vcb = 3, 4
    ins = [(qkv, qcb, 0), (qkv, kcb, 0)] + ([(qkv, kcb, -1)] if has_prev else []) + [(qkv, vcb, 0)] + ([(qkv, vcb, -1)] if has_prev else [])
    ins += [(o, 0, 0), (do, 0, 0), (lse, 0, 0)]
    res = _att_call(p, n, bl, c, body, name=f"att_bwd{p}", ins=ins, outs=[BF16] * (5 if has_prev else 3))
    if has_prev:
        dq, dkc, dkp, dvc, dvp = res
        return dq, dkc, dkp, dvc, dvp
    dq, dkc, dvc = res
    return dq, dkc, None, dvc, None


def _dqkv_to_rows(dqs, dk_parts, dv_parts, bl):
    n, c = dqs[0].shape
    t = 16 * PHASES * bl
    g = n // bl // PHASES
    nb = n // t
    shifts = [_att_geometry(p, n, bl)[5] // 16 for p in range(len(dk_parts))]
    view = lambda a: a.reshape(bl * PHASES, g, c)
    blk = lambda s: pl.BlockSpec((bl * PHASES, 16, c), lambda i, j, s=s: (0, jnp.minimum(i + s, nb - 1), 0))
    ins, specs, layout = [], [], []
    for a in dqs:
        layout.append([(len(ins), 0)])
        ins.append(view(a))
        specs.append(blk(0))
    for parts in (dk_parts, dv_parts):
        terms = []
        for (own, prv), s in zip(parts, shifts):
            terms.append((len(ins), 0))
            ins.append(view(own))
            specs.append(blk(0))
            if prv is not None:
                terms.append((len(ins), s))
                ins.append(view(prv))
                specs.append(blk(s))
        layout.append(terms)

    def body(p_ref, *refs):
        o_ref = refs[-1]
        i, j = pl.program_id(0), pl.program_id(1)
        for col, terms in enumerate(layout):
            @pl.when(j == col)
            def _(terms=terms):
                if len(terms) == 1:
                    x = refs[terms[0][0]][...]
                else:
                    x = 0.0
                    for pos, s in terms:
                        v = refs[pos][...].astype(F32)
                        x = x + (v if s == 0 else jnp.where(i + s < nb, v, 0.0))
                    x = x.astype(BF16)
                o_ref[...] = jnp.dot(p_ref[...], x.reshape(t, c), preferred_element_type=F32).astype(o_ref.dtype)

    return pl.pallas_call(
        body, name="dqkv_to_rows", grid=(nb, len(layout)), in_specs=[pl.BlockSpec((t, t), lambda i, j: (0, 0))] + specs,
        out_specs=pl.BlockSpec((t, c), lambda i, j: (i, j)), out_shape=jax.ShapeDtypeStruct((n, len(layout) * c), BF16),
        compiler_params=_cparams(("parallel", "arbitrary")))(_phase_perm(bl).T, *ins)


def _attention_fwd(proj, c, bl, heads):
    n = proj.shape[0]
    qkv = _to_phase_order(proj, bl, col0=c, width=5 * c)
    outs = [_att_fwd(p, qkv, p, bl, c, heads) for p in range(3)]
    ins = [("row", o, c, 0) for o, _ in outs] + [("row", l, c, 0) for _, l in outs]
    o, lse = _rowwise(_combine_fwd_fn, "comb_fwd", n, ins, [(c, BF16), (c, F32)])
    return _from_phase_order(o, bl), (qkv, o, lse)


def _attention_bwd(do_tb, saved, bl, heads):
    qkv, o, lse = saved
    n, c = do_tb.shape
    do = _to_phase_order(do_tb, bl)
    dqs, dks, dvs = [], [], []
    for p in range(3):
        dq, dkc, dkp, dvc, dvp = _att_bwd(p, qkv, p, o, do, lse, bl, c, heads)
        dqs.append(dq)
        dks.append((dkc, dkp))
        dvs.append((dvc, dvp))
    return _dqkv_to_rows(dqs, dks, dvs, bl)


ATT_HEADS = 8
SSM_GROUPS, SSM_STATE, SSM_GROUP = 32, 64, 16


def _row(v):
    return v.reshape(1, -1)


ROWS_TILE = 128


def _to_rows(x):
    bl, seq, d = x.shape

    def body(x_ref, o_ref):
        o_ref[...] = jnp.stack([x_ref[b] for b in range(bl)], axis=1).reshape(ROWS_TILE * bl, d)

    return pl.pallas_call(
        body, name="to_rows", grid=(seq // ROWS_TILE,), in_specs=[pl.BlockSpec((bl, ROWS_TILE, d), lambda i: (0, i, 0))],
        out_specs=pl.BlockSpec((ROWS_TILE * bl, d), lambda i: (i, 0)), out_shape=jax.ShapeDtypeStruct((seq * bl, d), x.dtype),
        compiler_params=_cparams(("parallel",)))(x)


def _from_rows(y, bl):
    n, d = y.shape
    seq = n // bl

    def body(y_ref, o_ref):
        v = y_ref[...].reshape(ROWS_TILE, bl, d)
        for b in range(bl):
            o_ref[b] = v[:, b, :]

    return pl.pallas_call(
        body, name="from_rows", grid=(seq // ROWS_TILE,), in_specs=[pl.BlockSpec((ROWS_TILE * bl, d), lambda i: (i, 0))],
        out_specs=pl.BlockSpec((bl, ROWS_TILE, d), lambda i: (0, i, 0)), out_shape=jax.ShapeDtypeStruct((bl, seq, d), y.dtype),
        compiler_params=_cparams(("parallel",)))(y)


def _carried(result, carry, key, hidden):
    if carry.get(key) is None:
        return result
    result, hidden[key] = result
    return result


def _layer_fwd(x, w, p, bl, carry, late=None):
    n, d = x.shape
    c = d // 2
    hidden = {}
    h, = _rowwise(_rms_fwd_fn, "rms_fwd", n, [("row", x, d, 0), ("par", _row(p["norm1_g"]))], [(d, BF16)])
    proj = _carried(_mm(h, w["w_in"], "nn", BF16, "mm_in", comm=carry.get("mm_in")), carry, "mm_in", hidden)
    if late is not None:
        w = dict(w, **late(hidden["mm_in"]))
    disc, disc_vjp = jax.vjp(_ssm_disc, p["ssm_lambda_re"], p["ssm_lambda_im"], p["ssm_log_dt"], p["ssm_b_re"], p["ssm_b_im"])
    bbd, cdm, a8 = _ssm_pack(*disc, p["ssm_c_re"], p["ssm_c_im"])
    ypre, yg, s_all = _carried(_ssm_fwd(proj, bbd, cdm, a8, _row(p["ssm_d"]), bl, "ssm_fwd", comm=carry.get("ssm_fwd")),
                               carry, "ssm_fwd", hidden)
    zs = _mm(yg, w["w_ssm_glu"], "nn", BF16, "mm_glu")
    o, att = _attention_fwd(proj, c, bl, ATT_HEADS)
    ya = _mm(o, w["w_att_up"], "nn", BF16, "mm_att")
    w32 = jnp.concatenate([p["conv_w"], jnp.zeros((1, c), F32)], axis=0)
    hc, hconv = _conv_fwd(proj, 6, w32, _row(p["conv_b"]), _row(p["conv_ln_g"]), _row(p["conv_ln_b"]), bl, c, "conv_fwd")
    yc = _mm(hc, w["w_conv_pw2"], "nn", BF16, "mm_pw2")
    gates = [("row", proj, d, 4), ("row", proj, d, 5), ("row", proj, d, 6), ("par", _row(p["b_gate"]))]
    branches = [("row", zs, 2 * d, 0), ("row", ya, d, 0), ("row", yc, d, 0)]
    merged, = _rowwise(_merge_fwd_fn, "merge_fwd", n, gates + branches, [(d, BF16)])
    xm = _mm(merged, w["w_out"], "nn", F32, "mm_out", res=x)
    h2, = _rowwise(_rms_fwd_fn, "rms_fwd", n, [("row", xm, d, 0), ("par", _row(p["norm2_g"]))], [(d, BF16)])
    z = _carried(_mm(h2, w["w_ffn_in"], "nn", BF16, "mm_ffn_in", comm=carry.get("mm_ffn_in")), carry, "mm_ffn_in", hidden)
    f = z.shape[1] // 2
    a, = _rowwise(_swiglu_fwd_fn, "swiglu_fwd", n, [("row", z, 2 * f, 0)], [(f, BF16)], tm=256)
    xo = _mm(a, w["w_ffn_out"], "nn", F32, "mm_ffn_out", res=xm)
    saved = dict(x=x, h=h, proj=proj, disc_vjp=disc_vjp, bbd=bbd, cdm=cdm, a8=a8, ypre=ypre, yg=yg, s_all=s_all, zs=zs, o=o,
                 att=att, ya=ya, w32=w32, hc=hc, hconv=hconv, yc=yc, gates=gates, branches=branches, merged=merged, xm=xm,
                 h2=h2, z=z, a=a)
    return xo, saved, hidden, w


EARLY_GRADS = ("w_ffn_out", "w_ffn_in", "w_out")


def _layer_bwd(dxo, s, w, p, bl, carry, early=None):
    n, d = dxo.shape
    c = d // 2
    g, bufs, hidden = {}, {}, {}
    f = s["a"].shape[1]

    def dw(key, a, dy, name):
        bufs[key] = _mm_dw(a, dy, name, 1 if key in ROW_SHARDED else N_CHIPS)

    da = _mm(dxo, w["w_ffn_out"], "nt", BF16, "mm_ffn_out_dx")
    dw("w_ffn_out", s["a"], dxo, "mm_ffn_out_dw")
    dz, = _rowwise(_swiglu_bwd_fn, "swiglu_bwd", n, [("row", s["z"], 2 * f, 0), ("row", da, f, 0)], [(2 * f, BF16)], tm=256)
    dh2 = _mm(dz, w["w_ffn_in"], "nt", F32, "mm_ffn_in_dx")
    dw("w_ffn_in", s["h2"], dz, "mm_ffn_in_dw")
    dxm, dg2 = _rowwise(_rms_bwd_fn, "rms_bwd", n, [("row", s["xm"], d, 0), ("par", _row(p["norm2_g"])), ("row", dh2, d, 0),
                                                   ("row", dxo, d, 0)], [(d, F32)], [d])
    g["norm2_g"] = dg2[0]
    dmerged = _mm(dxm, w["w_out"], "nt", BF16, "mm_out_dx")
    dw("w_out", s["merged"], dxm, "mm_out_dw")
    if early is not None:
        carry = dict(carry, mm_in_dx=early({k: bufs[k] for k in EARLY_GRADS}))
    dgl, dzs, dya, dyc, dbg = _rowwise(_merge_bwd_fn, "merge_bwd", n, s["gates"] + s["branches"] + [("row", dmerged, d, 0)],
                                       [(3 * d, BF16), (2 * d, BF16), (d, BF16), (d, BF16)], [3 * d], tm=256)
    g["b_gate"] = dbg[0]
    dyg = _mm(dzs, w["w_ssm_glu"], "nt", BF16, "mm_glu_dx")
    dw("w_ssm_glu", s["yg"], dzs, "mm_glu_dw")
    du, dbb, dcd, dab, dd = _carried(
        _ssm_bwd(dyg, s["ypre"], s["proj"], s["s_all"], s["cdm"].transpose(0, 2, 1), s["bbd"].transpose(0, 2, 1), s["a8"],
                 _row(p["ssm_d"]), bl, "ssm_bwd", comm=carry.get("ssm_bwd")), carry, "ssm_bwd", hidden)
    dab_re, dab_im, dbb_re, dbb_im, g["ssm_c_re"], g["ssm_c_im"] = _ssm_unpack(dbb, dcd, dab, SSM_GROUPS, SSM_STATE, SSM_GROUP)
    (g["ssm_lambda_re"], g["ssm_lambda_im"], g["ssm_log_dt"], g["ssm_b_re"],
     g["ssm_b_im"]) = s["disc_vjp"]((dab_re, dab_im, dbb_re, dbb_im))
    g["ssm_d"] = dd[0]
    do = _mm(dya, w["w_att_up"], "nt", BF16, "mm_att_dx")
    dw("w_att_up", s["o"], dya, "mm_att_dw")
    dqkv = _attention_bwd(do, s["att"], bl, ATT_HEADS)
    dhc = _mm(dyc, w["w_conv_pw2"], "nt", BF16, "mm_pw2_dx")
    dw("w_conv_pw2", s["hc"], dyc, "mm_pw2_dw")
    dcv, dcw, dcb, dlg, dlb = _carried(
        _conv_bwd(s["proj"], 6, dhc, s["hconv"], s["w32"], _row(p["conv_ln_g"]), _row(p["conv_ln_b"]), bl, c, "conv_bwd",
                  comm=carry.get("conv_bwd")), carry, "conv_bwd", hidden)
    g["conv_w"], g["conv_b"], g["conv_ln_g"], g["conv_ln_b"] = dcw, dcb[0], dlg[0], dlb[0]
    dproj = jnp.concatenate([du, dqkv, dcv, dgl], axis=1)
    dh = _carried(_mm(dproj, w["w_in"], "nt", F32, "mm_in_dx", comm=carry.get("mm_in_dx")), carry, "mm_in_dx", hidden)
    dw("w_in", s["h"], dproj, "mm_in_dw")
    dx, dg1 = _rowwise(_rms_bwd_fn, "rms_bwd", n, [("row", s["x"], d, 0), ("par", _row(p["norm1_g"])), ("row", dh, d, 0),
                                                  ("row", dxm, d, 0)], [(d, F32)], [d])
    g["norm1_g"] = dg1[0]
    return dx, g, bufs, hidden


WEIGHTS = ['norm1_g', 'w_in', 'b_gate', 'ssm_lambda_re', 'ssm_lambda_im', 'ssm_log_dt', 'ssm_b_re', 'ssm_b_im', 'ssm_c_re',
           'ssm_c_im', 'ssm_d', 'w_ssm_glu', 'w_att_up', 'conv_w', 'conv_b', 'conv_ln_g', 'conv_ln_b', 'w_conv_pw2', 'w_out',
           'norm2_g', 'w_ffn_in', 'w_ffn_out', 'final_g']
BIG = ['w_in', 'w_ssm_glu', 'w_att_up', 'w_conv_pw2', 'w_out', 'w_ffn_in', 'w_ffn_out']
ROW_SHARDED = ('w_out', 'w_ffn_out')
SMALL = [k for k in WEIGHTS if k not in BIG]
LANES = 1024
N_CHIPS = 4
ROW_TILE_BYTES = 36 * 1024 * 1024
MIN_SHARD_TILE = 1024


def _pad_rows(a, rows):
    return jnp.concatenate([a, jnp.zeros((rows - a.shape[0],) + a.shape[1:], a.dtype)], axis=0) if rows > a.shape[0] else a


def _row_tile(rows, width, n_arrays):
    best = 16
    for t in range(16, rows + 1, 16):
        if rows % t == 0 and t * width * 4 * n_arrays * 2 <= ROW_TILE_BYTES:
            best = t
    return best


def _flat_fn(fn, name, ins, n_out, rows):
    return _rowwise(fn, name, rows, [("row", a, LANES, 0) for a in ins], [(LANES, F32)] * n_out, tm=rows)


def _reduce_prepare(bufs):
    landed = _run_exchange(_SwapHalves([b16 for _, b16 in bufs]), "rs_swap")
    kept, p16s = [], []
    for (b32, _), la in zip(bufs, landed):
        s, m, cs = b32.shape
        h = m // 2
        tm = _row_tile(h, cs, 3)
        halves = b32.reshape(s, 2, h, cs)

        def body(g_ref, l_ref, o16):
            o16[...] = (g_ref[...] + l_ref[...].astype(F32)).astype(BF16)

        piece = pl.BlockSpec((None, tm, cs), lambda j, i: (j, i, 0))
        mine = pl.BlockSpec((None, None, tm, cs), lambda j, i: (j, _core_index(), i, 0))
        p16s.append(pl.pallas_call(
            body, name="rs_add", grid=(s, h // tm), in_specs=[mine, piece], out_specs=piece,
            out_shape=jax.ShapeDtypeStruct((s, h, cs), BF16), compiler_params=_cparams(("parallel", "parallel")))(halves, la))
        kept.append((halves, la))
    return kept, p16s


def _reduce_finish(kept, arrived):
    reduced = []
    for (halves, la), lb in zip(kept, arrived):
        _, h, cs = lb.shape
        tm = _row_tile(h, cs, 6)

        def body(g_ref, l_ref, a_ref, b_ref, c_ref, o_ref):
            own = g_ref[...] + l_ref[...].astype(F32)
            o_ref[...] = ((own + a_ref[...].astype(F32)) + b_ref[...].astype(F32)) + c_ref[...].astype(F32)

        mine = pl.BlockSpec((None, None, tm, cs), lambda i: (_chip_index(), _core_index(), i, 0))
        sibling = pl.BlockSpec((None, tm, cs), lambda i: (_chip_index(), i, 0))
        other = [pl.BlockSpec((None, tm, cs), lambda i, k=k: (k, i, 0)) for k in range(3)]
        half = pl.BlockSpec((None, tm, cs), lambda i: (_core_index(), i, 0))
        reduced.append(pl.pallas_call(
            body, name="rs_sum", grid=(h // tm,), in_specs=[mine, sibling] + other, out_specs=half,
            out_shape=jax.ShapeDtypeStruct((2, h, cs), F32), compiler_params=_cparams(("parallel",)))(halves, la, lb, lb, lb))
    joined = _run_exchange(_JoinHalves(reduced), "rs_gather")
    return [j.reshape(2 * j.shape[1], j.shape[2]) for j in joined]


def _adamw_layers(w, g_layers, m, v):
    depth, rows, cs = w.shape
    tm = _row_tile(rows, cs, 8)
    nb = rows // tm

    def body(*refs):
        w_ref, m_ref, v_ref = refs[:3]
        g_refs = refs[3:3 + depth]
        go_ref, d_ref, mo_ref, vo_ref = refs[3 + depth:]
        layer = pl.program_id(0)
        g = g_refs[0][...]
        for l in range(1, depth):
            g = jnp.where(layer == l, g_refs[l][...], g)
        delta, mo, vo = _adamw_fn(w_ref[...], g, m_ref[...], v_ref[...])
        go_ref[...], d_ref[...], mo_ref[...], vo_ref[...] = g, delta, mo, vo

    stacked = pl.BlockSpec((None, tm, cs), lambda l, i: (l, i, 0))
    g_specs = [pl.BlockSpec((tm, cs), lambda l, i, k=k: (jnp.where(l == k, i, jnp.where(l < k, 0, nb - 1)), 0)) for k in range(depth)]
    return pl.pallas_call(
        body, name="adamw", grid=(depth, nb), in_specs=[stacked] * 3 + g_specs, out_specs=[stacked] * 4,
        out_shape=[jax.ShapeDtypeStruct(w.shape, F32)] * 4, compiler_params=_cparams(("arbitrary", "arbitrary")))(w, m, v, *g_layers)


def _sum4_fn(a, b, c, d):
    return (((a.astype(F32) + b.astype(F32)) + c.astype(F32)) + d.astype(F32),)


def _add2_fn(a, b):
    return (a + b,)


def kernel(x, norm1_g, w_in, b_gate, ssm_lambda_re, ssm_lambda_im, ssm_log_dt, ssm_b_re, ssm_b_im, ssm_c_re, ssm_c_im, ssm_d, w_ssm_glu, w_att_up, conv_w, conv_b, conv_ln_g, conv_ln_b, w_conv_pw2, w_out, norm2_g, w_ffn_in, w_ffn_out, final_g, loss_target, m_norm1_g, m_w_in, m_b_gate, m_ssm_lambda_re, m_ssm_lambda_im, m_ssm_log_dt, m_ssm_b_re, m_ssm_b_im, m_ssm_c_re, m_ssm_c_im, m_ssm_d, m_w_ssm_glu, m_w_att_up, m_conv_w, m_conv_b, m_conv_ln_g, m_conv_ln_b, m_w_conv_pw2, m_w_out, m_norm2_g, m_w_ffn_in, m_w_ffn_out, m_final_g, v_norm1_g, v_w_in, v_b_gate, v_ssm_lambda_re, v_ssm_lambda_im, v_ssm_log_dt, v_ssm_b_re, v_ssm_b_im, v_ssm_c_re, v_ssm_c_im, v_ssm_d, v_w_ssm_glu, v_w_att_up, v_conv_w, v_conv_b, v_conv_ln_g, v_conv_ln_b, v_w_conv_pw2, v_w_out, v_norm2_g, v_w_ffn_in, v_w_ffn_out, v_final_g):
    args = dict(locals())
    wts = {k: args[k] for k in WEIGHTS}
    mom = {k: args["m_" + k] for k in WEIGHTS}
    var = {k: args["v_" + k] for k in WEIGHTS}
    bl, seq, d = x.shape
    n = bl * seq
    depth = norm1_g.shape[0]
    me = _chip_index()

    assert depth == 2, "the exchanges of layer 1 are hidden behind layer 0's kernels"
    first = BIG[:1]
    rest = BIG[1:]

    shards = lambda keys, l: [wts[k][l].astype(BF16) for k in keys]

    def whole(keys, gathered):
        out = {}
        for k, a in zip(keys, gathered):
            _, ks, cs = a.shape
            if k in ROW_SHARDED:
                out[k] = a.reshape(N_CHIPS * ks, cs)
            elif cs < MIN_SHARD_TILE:
                out[k] = a.transpose(1, 0, 2).reshape(ks, N_CHIPS * cs)
            else:
                out[k] = a
        return out

    fill = lambda gathered, own: [_own_slot(g, o) for g, o in zip(gathered, own)]
    own0 = shards(first, 0) + [conv_w]
    gathered = fill(_run_exchange(_GatherShards(own0), "gather_weights"), own0)
    conv_full = gathered[-1].transpose(1, 2, 0, 3).reshape(depth, CONV_WIDTH, -1)
    params = lambda l: dict({k: wts[k][l] for k in SMALL if k not in ("final_g", "conv_w")}, conv_w=conv_full[l])

    own = {"mm_in": shards(rest, 0), "ssm_fwd": shards(first, 1), "mm_ffn_in": shards(rest, 1)}
    xs, s0, hidden, w0 = _layer_fwd(_to_rows(x), whole(first, gathered[:-1]), params(0), bl, {k: _GatherShards(v) for k, v in own.items()},
                                    late=lambda got: whole(rest, fill(got, own["mm_in"])))
    w1 = dict(whole(first, fill(hidden["ssm_fwd"], own["ssm_fwd"])), **whole(rest, fill(hidden["mm_ffn_in"], own["mm_ffn_in"])))
    full = [w0, w1]
    xs, s1, _, _ = _layer_fwd(xs, full[1], params(1), bl, {})
    dx, sq, dgf = _rowwise(_loss_fn, "loss_head", n, [("row", xs, d, 0), ("par", _row(final_g)), ("row", _to_rows(loss_target), d, 0)],
                           [(d, F32)], [d, d])
    loss = lax.psum(0.5 * jnp.sum(sq) / d, ("x", "y", "c"))

    pieces = lambda bufs, keys: [tuple(b.reshape(N_CHIPS, -1, b.shape[-1]) for b in bufs[k]) for k in keys]
    dx, g1, bufs1, _ = _layer_bwd(dx, s1, full[1], params(1), bl, {})
    p32_1, p16_1 = _reduce_prepare(pieces(bufs1, BIG))
    kept_early = []

    def early(bufs):
        kept, p16 = _reduce_prepare(pieces(bufs, EARLY_GRADS))
        kept_early.extend(kept)
        return _ScatterPieces(p16)

    dx, g0, bufs0, hidden = _layer_bwd(dx, s0, full[0], params(0), bl,
                                       {"ssm_bwd": _ScatterPieces(p16_1[:1]), "conv_bwd": _ScatterPieces(p16_1[1:])}, early)
    red1 = _reduce_finish(p32_1, list(hidden["ssm_bwd"]) + list(hidden["conv_bwd"]))
    late = [k for k in BIG if k not in EARLY_GRADS]
    kept_late, p16_late = _reduce_prepare(pieces(bufs0, late))
    done = dict(zip(EARLY_GRADS, _reduce_finish(kept_early, hidden["mm_in_dx"])))
    done.update(zip(late, _reduce_finish(kept_late, _run_exchange(_ScatterPieces(p16_late), "rs_scatter"))))
    red0 = [done[k] for k in BIG]
    grads = {"final_g": dgf[0]}
    for k in SMALL:
        if k != "final_g":
            grads[k] = jnp.stack([g0[k], g1[k]])
    grad_x = _from_rows(dx, bl)
    outs = {}
    for k, r0, r1 in zip(BIG, red0, red1):
        for tag, a in zip(("grad", "delta", "m", "v"), _adamw_layers(wts[k], [r0, r1], mom[k], var[k])):
            outs[tag, k] = a

    def flat1(t):
        v = jnp.concatenate([t[k].reshape(-1) for k in SMALL])
        rows = -(-v.size // (8 * LANES)) * 8
        return _pad_rows(v, rows * LANES).reshape(rows, LANES), rows

    def unflat1(flat, shapes):
        out, off, v = {}, 0, flat.reshape(-1)
        for k in SMALL:
            size = math.prod(shapes[k])
            out[k] = v[off:off + size].reshape(shapes[k])
            off += size
        return out

    grads["conv_w"] = grads["conv_w"][:, :CONV_WIDTH]
    gs, rows = flat1(grads)
    chip_sum, = _flat_fn(_add2_fn, "ar_add", [gs, _swap_sibling(gs, "ar_swap")], 1, rows)
    slots = _chip_allgather(chip_sum, "ar_gather")
    gs_red, = _flat_fn(_sum4_fn, "ar_sum", [slots[j] for j in range(N_CHIPS)], 1, rows)
    g_sm = unflat1(gs_red, {k: grads[k].shape for k in SMALL})
    cs = conv_w.shape[2]
    g_sm["conv_w"] = lax.dynamic_slice_in_dim(g_sm["conv_w"], me * cs, cs, axis=2)
    (w1, rows), (g1, _), (m1, _), (v1, _) = flat1(wts), flat1(g_sm), flat1(mom), flat1(var)
    sm_out = _flat_fn(_adamw_fn, "adamw_small", [w1, g1, m1, v1], 3, rows)
    shapes = {k: wts[k].shape for k in SMALL}
    for tag, a in zip(("delta", "m", "v"), sm_out):
        for k, t in unflat1(a, shapes).items():
            outs[tag, k] = t
    for k in SMALL:
        outs["grad", k] = g_sm[k]
    return (loss, grad_x, *[outs["grad", k] for k in WEIGHTS], *[outs["delta", k] for k in WEIGHTS],
            *[outs["m", k] for k in WEIGHTS], *[outs["v", k] for k in WEIGHTS])
```

```python
import functools
import math

import jax
import jax.numpy as jnp
from jax import lax
from jax.experimental import pallas as pl
from jax.experimental.pallas import tpu as pltpu

F32 = jnp.float32
BF16 = jnp.bfloat16
VMEM_LIMIT = 56 * 1024 * 1024


def _cparams(sem):
    return pltpu.CompilerParams(dimension_semantics=sem, vmem_limit_bytes=VMEM_LIMIT)


_DIMS = {"nn": (((1,), (0,)), ((), ())), "nt": (((1,), (1,)), ((), ())), "tn": (((0,), (0,)), ((), ()))}


MM_ROWS = 1024
MM_DW_VMEM_BYTES = 44 * 1024 * 1024
MM_SMALL_STEP = 1024 * 1024 * 512


def _div_tile(n, cap):
    best = None
    for t in range(128, min(n, cap) + 1, 128):
        if n % t == 0:
            best = t
    return best or n


def _mm(a, b, form, out_dtype, name, res=None, comm=None):
    sharded = b.ndim == 3
    kdim, cs = b.shape[-2], b.shape[-1]
    s = b.shape[0] if sharded else 1
    m = a.shape[0]
    tm = MM_ROWS if m % MM_ROWS == 0 else _div_tile(m, MM_ROWS)
    if form == "nn":
        n, kd = s * cs, kdim
        tn, tk = _div_tile(cs, 1792), _div_tile(kdim, 2048)
        per = cs // tn
        b_blk = (tk, tn)
        b_idx = (lambda i, j, k: (j // per, k, j % per)) if sharded else (lambda i, j, k: (k, j))
    else:
        n, kd = kdim, s * cs
        tn, tk = _div_tile(kdim, 1408), _div_tile(cs, 1792)
        per = cs // tk
        b_blk = (tn, tk)
        b_idx = (lambda i, j, k: (k // per, j, k % per)) if sharded else (lambda i, j, k: (j, k))
    nk = kd // tk
    if tm * tn * tk <= MM_SMALL_STEP and m % (2 * tm) == 0:
        tm *= 2
    a_spec = pl.BlockSpec((tm, tk), lambda i, j, k: (i, k))
    b_spec = pl.BlockSpec(((None,) + b_blk) if sharded else b_blk, b_idx)
    o_spec = pl.BlockSpec((tm, tn), lambda i, j, k: (i, j))
    dims = _DIMS[form]

    def body(*refs):
        a_ref, b_ref = refs[:2]
        r_ref = refs[2] if res is not None else None
        o_ref = refs[3] if res is not None else refs[2]
        p = lax.dot_general(a_ref[...].astype(BF16), b_ref[...], dims, preferred_element_type=F32)

        def finish(r):
            if r_ref is not None:
                r = r + r_ref[...]
            o_ref[...] = r.astype(out_dtype)

        if nk == 1:
            finish(p)
            return
        acc = refs[-1]
        k = pl.program_id(2)

        @pl.when(k == 0)
        def _():
            acc[...] = p

        @pl.when(k > 0)
        def _():
            acc[...] += p

        @pl.when(k == nk - 1)
        def _():
            finish(acc[...])

    ins = [a, b] + ([] if res is None else [res])
    in_specs = [a_spec, b_spec] + ([] if res is None else [o_spec])
    out = _pcall(body, name, (m // tm, n // tn, nk), in_specs, [o_spec], [jax.ShapeDtypeStruct((m, n), out_dtype)],
                 [pltpu.VMEM((tm, tn), F32)] if nk > 1 else [], ("parallel", "parallel", "arbitrary"), ins, comm)
    return out[0] if comm is None else (out[0][0], out[1])


def _mm_dw(a, dy, name, shards):
    r, m = a.shape
    c = dy.shape[1]
    cs = c // shards
    tm, tn = _div_tile(m, 1408), _div_tile(cs, 1408)
    fixed = tm * tn * (4 + 2 * (4 + 2))
    per_row = 2 * (tm * a.dtype.itemsize + tn * dy.dtype.itemsize)
    tk = max(t for t in (256, 512, 1024, 2048) if r % t == 0 and (t == 256 or fixed + t * per_row <= MM_DW_VMEM_BYTES))
    per = cs // tn
    nk = r // tk

    def body(a_ref, b_ref, o32, o16, acc):
        k = pl.program_id(2)
        p = lax.dot_general(a_ref[...].astype(BF16), b_ref[...].astype(BF16), _DIMS["tn"], preferred_element_type=F32)

        @pl.when(k == 0)
        def _():
            acc[...] = p

        @pl.when(k > 0)
        def _():
            acc[...] += p

        @pl.when(k == nk - 1)
        def _():
            o32[...] = acc[...]
            o16[...] = acc[...].astype(BF16)

    o_spec = pl.BlockSpec((None, tm, tn), lambda i, j, k: (j // per, i, j % per))
    shape = (shards, m, cs)
    in_specs = [pl.BlockSpec((tk, tm), lambda i, j, k: (k, i)), pl.BlockSpec((tk, tn), lambda i, j, k: (k, j))]
    return _pcall(body, name, (m // tm, c // tn, nk), in_specs, [o_spec, o_spec],
                  [jax.ShapeDtypeStruct(shape, F32), jax.ShapeDtypeStruct(shape, BF16)], [pltpu.VMEM((tm, tn), F32)],
                  ("parallel", "parallel", "arbitrary"), [a, dy])


def _core_index():
    return lax.axis_index("c")


def _chip_index():
    return 2 * lax.axis_index("x") + lax.axis_index("y")


def _rowwise(fn, name, n_rows, ins, outs, accs=(), tm=512):
    n_in, n_out = len(ins), len(outs)
    in_specs, args = [], []
    for spec in ins:
        if spec[0] == "row":
            _, arr, w, cb = spec
            in_specs.append(pl.BlockSpec((tm, w), lambda i, cb=cb: (i, cb)))
        else:
            arr = spec[1]
            in_specs.append(pl.BlockSpec(arr.shape, lambda i: (0, 0)))
        args.append(arr)
    out_specs = [pl.BlockSpec((tm, w), lambda i: (i, 0)) for w, _ in outs]
    out_specs += [pl.BlockSpec((1, w), lambda i: (0, 0)) for w in accs]
    out_shape = [jax.ShapeDtypeStruct((n_rows, w), dt) for w, dt in outs]
    out_shape += [jax.ShapeDtypeStruct((1, w), F32) for w in accs]

    def body(*refs):
        i = pl.program_id(0)
        res = fn(*[r[...] for r in refs[:n_in]])
        for o_ref, r in zip(refs[n_in:n_in + n_out], res[:n_out]):
            o_ref[...] = r.astype(o_ref.dtype)
        for a_ref, r in zip(refs[n_in + n_out:], res[n_out:]):
            @pl.when(i == 0)
            def _(a_ref=a_ref, r=r):
                a_ref[...] = r

            @pl.when(i > 0)
            def _(a_ref=a_ref, r=r):
                a_ref[...] += r

    return pl.pallas_call(
        body, name=name, grid=(n_rows // tm,), in_specs=in_specs, out_specs=out_specs, out_shape=out_shape,
        compiler_params=_cparams(("arbitrary",)))(*args)


EPS = 1e-6


def _sig(x):
    return 1.0 / (1.0 + jnp.exp(-x))


def _colsum(x):
    return jnp.sum(x, axis=0, keepdims=True)


def _rms_fwd_fn(x, g):
    r = lax.rsqrt(jnp.mean(x * x, axis=-1, keepdims=True) + EPS)
    return (x * r * g,)


def _rms_bwd_fn(x, g, dh, dres):
    dh = dh.astype(F32)
    r = lax.rsqrt(jnp.mean(x * x, axis=-1, keepdims=True) + EPS)
    xh = x * r
    dyg = dh * g
    dx = r * (dyg - xh * jnp.mean(dyg * xh, axis=-1, keepdims=True)) + dres
    return dx, dx, _colsum(dh * xh)


def _loss_fn(x, g, t):
    d = x.shape[-1]
    r = lax.rsqrt(jnp.mean(x * x, axis=-1, keepdims=True) + EPS)
    xh = x * r
    err = xh * g - t
    dy = err * (1.0 / d)
    dyg = dy * g
    dx = r * (dyg - xh * jnp.mean(dyg * xh, axis=-1, keepdims=True))
    return dx, dx, _colsum(err * err), _colsum(dy * xh)


def _swiglu_fwd_fn(z):
    f = z.shape[-1] // 2
    z1, z2 = z[:, :f].astype(F32), z[:, f:].astype(F32)
    return (z1 * _sig(z1) * z2,)


def _swiglu_bwd_fn(z, da):
    f = z.shape[-1] // 2
    z1, z2, da = z[:, :f].astype(F32), z[:, f:].astype(F32), da.astype(F32)
    s = _sig(z1)
    dz1 = da * z2 * (s * (1.0 + z1 * (1.0 - s)))
    dz2 = da * (z1 * s)
    return (jnp.concatenate([dz1, dz2], axis=1),)


def _merge_fwd_fn(g0, g1, g2, bg, zs, ya, yc):
    d = ya.shape[-1]
    bg = bg.astype(F32)
    zs = zs.astype(F32)
    ys = zs[:, :d] * _sig(zs[:, d:])
    m = _sig(g0.astype(F32) + bg[:, :d]) * ys
    m = m + _sig(g1.astype(F32) + bg[:, d:2 * d]) * ya.astype(F32)
    m = m + _sig(g2.astype(F32) + bg[:, 2 * d:]) * yc.astype(F32)
    return (m,)


def _merge_bwd_fn(g0, g1, g2, bg, zs, ya, yc, dm):
    d = ya.shape[-1]
    bg = bg.astype(F32)
    zs = zs.astype(F32)
    dm = dm.astype(F32)
    z1, s2 = zs[:, :d], _sig(zs[:, d:])
    ys = z1 * s2
    s0 = _sig(g0.astype(F32) + bg[:, :d])
    s1 = _sig(g1.astype(F32) + bg[:, d:2 * d])
    s3 = _sig(g2.astype(F32) + bg[:, 2 * d:])
    dgl = jnp.concatenate([dm * ys * s0 * (1.0 - s0), dm * ya.astype(F32) * s1 * (1.0 - s1),
                           dm * yc.astype(F32) * s3 * (1.0 - s3)], axis=1)
    dys = dm * s0
    dzs = jnp.concatenate([dys * s2, dys * z1 * s2 * (1.0 - s2)], axis=1)
    return dgl, dzs, dm * s1, dm * s3, _colsum(dgl)


def _combine_fwd_fn(o0, o1, o2, l0, l1, l2):
    m = jnp.maximum(jnp.maximum(l0, l1), l2)
    e0, e1, e2 = jnp.exp(l0 - m), jnp.exp(l1 - m), jnp.exp(l2 - m)
    den = e0 + e1 + e2
    return (e0 * o0.astype(F32) + e1 * o1.astype(F32) + e2 * o2.astype(F32)) / den, m + jnp.log(den)


ADAM_LR, ADAM_B1, ADAM_B2, ADAM_EPS, ADAM_WD, ADAM_STEP = 0.001, 0.9, 0.999, 1e-08, 0.01, 10


def _adamw_fn(w, g, m, v):
    m = ADAM_B1 * m + (1.0 - ADAM_B1) * g
    v = ADAM_B2 * v + (1.0 - ADAM_B2) * (g * g)
    m_hat = m / (1.0 - ADAM_B1 ** ADAM_STEP)
    v_hat = v / (1.0 - ADAM_B2 ** ADAM_STEP)
    delta = -ADAM_LR * (m_hat / (jnp.sqrt(v_hat) + ADAM_EPS) + ADAM_WD * w)
    return delta, m, v


CONV_WIDTH = 31


def _conv_fwd(proj, cb, w32, conv_b, ln_g, ln_b, bl, c, name, tm=512):
    n = proj.shape[0]
    hp = (CONV_WIDTH - 1) * bl
    nt = n // tm

    def body(ap_ref, gp_ref, a_ref, g_ref, w_ref, cb_ref, lg_ref, lb_ref, hc_ref, hconv_ref, ext):
        i = pl.program_id(0)
        ext[pl.ds(hp, tm), :] = a_ref[...].astype(F32) * _sig(g_ref[...].astype(F32))
        hgp = ap_ref[pl.ds(tm - hp, hp), :].astype(F32) * _sig(gp_ref[pl.ds(tm - hp, hp), :].astype(F32))
        ext[pl.ds(0, hp), :] = jnp.where(i > 0, hgp, 0.0)
        acc = jnp.zeros((tm, c), F32) + cb_ref[...]
        for j in range(CONV_WIDTH):
            acc = acc + w_ref[j:j + 1, :] * ext[pl.ds(j * bl, tm), :]
        hconv_ref[...] = acc.astype(hconv_ref.dtype)
        h = hconv_ref[...].astype(F32)
        mu = jnp.mean(h, axis=-1, keepdims=True)
        xc = h - mu
        var = jnp.mean(xc * xc, axis=-1, keepdims=True)
        hn = xc * lax.rsqrt(var + EPS) * lg_ref[...] + lb_ref[...]
        hc_ref[...] = (hn * _sig(hn)).astype(hc_ref.dtype)

    prev = lambda i, k: (jnp.maximum(i - 1, 0), k)
    par = lambda arr: pl.BlockSpec(arr.shape, lambda i: (0, 0))
    return pl.pallas_call(
        body, name=name, grid=(nt,),
        in_specs=[pl.BlockSpec((tm, c), functools.partial(prev, k=cb)), pl.BlockSpec((tm, c), functools.partial(prev, k=cb + 1)),
                  pl.BlockSpec((tm, c), lambda i: (i, cb)), pl.BlockSpec((tm, c), lambda i: (i, cb + 1)),
                  par(w32), par(conv_b), par(ln_g), par(ln_b)],
        out_specs=[pl.BlockSpec((tm, c), lambda i: (i, 0))] * 2,
        out_shape=[jax.ShapeDtypeStruct((n, c), BF16)] * 2,
        scratch_shapes=[pltpu.VMEM((hp + tm, c), F32)],
        compiler_params=_cparams(("arbitrary",)))(proj, proj, proj, proj, w32, conv_b, ln_g, ln_b)


def _conv_bwd(proj, cb, dhc, hconv, w32, ln_g, ln_b, bl, c, name, tm=512, comm=None):
    n = proj.shape[0]
    hp = (CONV_WIDTH - 1) * bl
    nt = n // tm

    def ln_bwd(d, h, lg, lb):
        d, h = d.astype(F32), h.astype(F32)
        mu = jnp.mean(h, axis=-1, keepdims=True)
        xc = h - mu
        rstd = lax.rsqrt(jnp.mean(xc * xc, axis=-1, keepdims=True) + EPS)
        xh = xc * rstd
        hn = xh * lg + lb
        s = _sig(hn)
        dhn = d * (s * (1.0 + hn * (1.0 - s)))
        dxh = dhn * lg
        dh = rstd * (dxh - jnp.mean(dxh, axis=-1, keepdims=True) - xh * jnp.mean(dxh * xh, axis=-1, keepdims=True))
        return dh, dhn, xh

    def body(ap_ref, gp_ref, a_ref, g_ref, d_ref, dn_ref, h_ref, hn_ref, w_ref, lg_ref, lb_ref,
             dcv_ref, dw_ref, dcb_ref, dlg_ref, dlb_ref, ext_h, ext_d):
        i = pl.program_id(0)
        lg, lb = lg_ref[...], lb_ref[...]
        a, g = a_ref[...].astype(F32), g_ref[...].astype(F32)
        sg = _sig(g)
        ext_h[pl.ds(hp, tm), :] = a * sg
        hgp = ap_ref[pl.ds(tm - hp, hp), :].astype(F32) * _sig(gp_ref[pl.ds(tm - hp, hp), :].astype(F32))
        ext_h[pl.ds(0, hp), :] = jnp.where(i > 0, hgp, 0.0)
        dh, dhn, xh = ln_bwd(d_ref[...], h_ref[...], lg, lb)
        ext_d[pl.ds(0, tm), :] = dh
        dh_n, _, _ = ln_bwd(dn_ref[pl.ds(0, hp), :], hn_ref[pl.ds(0, hp), :], lg, lb)
        ext_d[pl.ds(tm, hp), :] = jnp.where(i < nt - 1, dh_n, 0.0)

        @pl.when(i == 0)
        def _():
            dw_ref[...] = jnp.zeros_like(dw_ref)
            dcb_ref[...] = jnp.zeros_like(dcb_ref)
            dlg_ref[...] = jnp.zeros_like(dlg_ref)
            dlb_ref[...] = jnp.zeros_like(dlb_ref)

        dcb_ref[...] += _colsum(dh)
        dlg_ref[...] += _colsum(dhn * xh)
        dlb_ref[...] += _colsum(dhn)
        dhg = jnp.zeros((tm, c), F32)
        for j in range(CONV_WIDTH):
            dhg = dhg + w_ref[j:j + 1, :] * ext_d[pl.ds((CONV_WIDTH - 1 - j) * bl, tm), :]
            dw_ref[j:j + 1, :] += _colsum(dh * ext_h[pl.ds(j * bl, tm), :])
        dcv_ref[...] = jnp.concatenate([dhg * sg, dhg * a * sg * (1.0 - sg)], axis=1).astype(dcv_ref.dtype)

    prev = lambda i, k: (jnp.maximum(i - 1, 0), k)
    nxt = lambda i: (jnp.minimum(i + 1, nt - 1), 0)
    cur = lambda i: (i, 0)
    par = lambda arr: pl.BlockSpec(arr.shape, lambda i: (0, 0))
    acc = lambda r: pl.BlockSpec((r, c), lambda i: (0, 0))
    in_specs = [pl.BlockSpec((tm, c), functools.partial(prev, k=cb)), pl.BlockSpec((tm, c), functools.partial(prev, k=cb + 1)),
                pl.BlockSpec((tm, c), lambda i: (i, cb)), pl.BlockSpec((tm, c), lambda i: (i, cb + 1)),
                pl.BlockSpec((tm, c), cur), pl.BlockSpec((tm, c), nxt), pl.BlockSpec((tm, c), cur), pl.BlockSpec((tm, c), nxt),
                par(w32), par(ln_g), par(ln_b)]
    out_shape = [jax.ShapeDtypeStruct((n, 2 * c), BF16), jax.ShapeDtypeStruct((32, c), F32)] + [jax.ShapeDtypeStruct((1, c), F32)] * 3
    return _pcall(body, name, (nt,), in_specs, [pl.BlockSpec((tm, 2 * c), cur), acc(32), acc(1), acc(1), acc(1)], out_shape,
                  [pltpu.VMEM((hp + tm, c), F32), pltpu.VMEM((hp + tm, c), F32)], ("arbitrary",),
                  [proj, proj, proj, proj, dhc, dhc, hconv, hconv, w32, ln_g, ln_b], comm)


SSM_CH = 128
_GELU_C = 0.7978845608028654


def _gelu(x):
    return 0.5 * x * (1.0 + jnp.tanh(_GELU_C * (x + 0.044715 * x * x * x)))


def _gelu_grad(x):
    th = jnp.tanh(_GELU_C * (x + 0.044715 * x * x * x))
    return 0.5 * (1.0 + th) + 0.5 * x * (1.0 - th * th) * (_GELU_C * (1.0 + 3.0 * 0.044715 * x * x))


def _ssm_disc(lam_re, lam_im, log_dt, b_re, b_im):
    dt = jnp.exp(log_dt)[:, None]
    mag = jnp.exp(lam_re * dt)
    ab_re = mag * jnp.cos(lam_im * dt)
    ab_im = mag * jnp.sin(lam_im * dt)
    nr, ni = ab_re - 1.0, ab_im
    den = lam_re * lam_re + lam_im * lam_im
    z_re = ((nr * lam_re + ni * lam_im) / den)[..., None]
    z_im = ((ni * lam_re - nr * lam_im) / den)[..., None]
    return ab_re, ab_im, z_re * b_re - z_im * b_im, z_re * b_im + z_im * b_re


def _ssm_pack(ab_re, ab_im, bb_re, bb_im, c_re, c_im):
    g, p, h = bb_re.shape
    gc = SSM_CH // h
    nc = g // gc
    eye = jnp.eye(gc, dtype=F32)
    blk = lambda x: jnp.einsum("qgph,gk->qghkp", x.reshape(nc, gc, p, h), eye).reshape(nc, gc * h, gc * p)
    bbd = jnp.concatenate([blk(bb_re), blk(bb_im)], axis=2).astype(BF16)
    blc = lambda x: jnp.einsum("qghp,gk->qgpkh", x.reshape(nc, gc, h, p), eye).reshape(nc, gc * p, gc * h)
    cdm = jnp.concatenate([blc(c_re), blc(-c_im)], axis=1).astype(BF16)
    a = jnp.concatenate([ab_re.reshape(nc, gc * p), ab_im.reshape(nc, gc * p)], axis=1)
    a8 = jnp.broadcast_to(a[:, None, :], (nc, 8, 2 * gc * p)).reshape(nc * 8, 2 * gc * p)
    return bbd, cdm, a8


def _ssm_unpack(dbb, dcd, da, g, p, h):
    gc = SSM_CH // h
    nc = g // gc
    ph = gc * p
    eye = jnp.eye(gc, dtype=F32)
    dia = lambda x, o: jnp.einsum("qgpkh,gk->" + o, x.reshape(nc, gc, p, gc, h), eye).reshape((g, p, h) if o == "qgph" else (g, h, p))
    das = da.reshape(nc, 8, 2 * ph).sum(axis=1)
    return (das[:, :ph].reshape(g, p), das[:, ph:].reshape(g, p), dia(dbb[:, :ph], "qgph"), dia(dbb[:, ph:], "qgph"),
            dia(dcd[:, :ph], "qghp"), -dia(dcd[:, ph:], "qghp"))


def _ssm_fwd(proj, bbd, cdm, a8, dskip, bl, name, tm=1024, comm=None):
    n = proj.shape[0]
    nc, ch, p2 = bbd.shape
    ph = p2 // 2
    nt = n // tm
    nsub = 8 // bl

    def body(u_ref, bb_ref, cd_ref, a_ref, d_ref, ypre_ref, yg_ref, s_ref, bu, carry):
        t = pl.program_id(1)

        @pl.when(t == 0)
        def _():
            carry[...] = jnp.zeros_like(carry)

        u = u_ref[...]
        bu[...] = jnp.dot(u, bb_ref[0], preferred_element_type=F32)
        a_re, a_im = a_ref[:, :ph], a_ref[:, ph:]
        row = lax.broadcasted_iota(jnp.int32, (8, ph), 0)

        def step(k, c):
            cre, cim = c
            r0 = pl.multiple_of(k * 8, 8)
            bre, bim = bu[pl.ds(r0, 8), :ph], bu[pl.ds(r0, 8), ph:]
            sre, sim = cre, cim
            for sub in range(nsub):
                xre, xim = pltpu.roll(cre, bl, 0), pltpu.roll(cim, bl, 0)
                cre = a_re * xre - a_im * xim + bre
                cim = a_re * xim + a_im * xre + bim
                if sub == 0:
                    sre, sim = cre, cim
                else:
                    sel = row >= sub * bl
                    sre, sim = jnp.where(sel, cre, sre), jnp.where(sel, cim, sim)
            bu[pl.ds(r0, 8), :ph] = sre
            bu[pl.ds(r0, 8), ph:] = sim
            return sre, sim

        cre, cim = lax.fori_loop(0, tm // 8, step, (carry[:, :ph], carry[:, ph:]))
        carry[:, :ph] = cre
        carry[:, ph:] = cim
        s16 = bu[...].astype(BF16)
        s_ref[...] = s16
        y = jnp.dot(s16, cd_ref[0], preferred_element_type=F32) + d_ref[...] * u.astype(F32)
        ypre_ref[...] = y
        yg_ref[...] = _gelu(y).astype(yg_ref.dtype)

    in_specs = [pl.BlockSpec((tm, ch), lambda q, t: (t, q)), pl.BlockSpec((1, ch, p2), lambda q, t: (q, 0, 0)),
                pl.BlockSpec((1, p2, ch), lambda q, t: (q, 0, 0)), pl.BlockSpec((8, p2), lambda q, t: (q, 0)),
                pl.BlockSpec((1, ch), lambda q, t: (0, q))]
    out_specs = [pl.BlockSpec((tm, ch), lambda q, t: (t, q)), pl.BlockSpec((tm, ch), lambda q, t: (t, q)),
                 pl.BlockSpec((tm, p2), lambda q, t: (t, q))]
    out_shape = [jax.ShapeDtypeStruct((n, nc * ch), F32), jax.ShapeDtypeStruct((n, nc * ch), BF16),
                 jax.ShapeDtypeStruct((n, nc * p2), BF16)]
    return _pcall(body, name, (nc, nt), in_specs, out_specs, out_shape, [pltpu.VMEM((tm, p2), F32), pltpu.VMEM((8, p2), F32)],
                  ("parallel", "arbitrary"), [proj, bbd, cdm, a8, dskip], comm)


def _ssm_bwd(dyg, ypre, proj, s_all, cdt, bbt, a8, dskip, bl, name, tm=1024, comm=None):
    n = proj.shape[0]
    nc, ch, p2 = cdt.shape
    ph = p2 // 2
    nt = n // tm
    nsub = 8 // bl
    tn_dims = (((0,), (0,)), ((), ()))

    def body(dyg_ref, ypre_ref, u_ref, s_ref, cdt_ref, bbt_ref, a_ref, d_ref,
             du_ref, dbb_ref, dcd_ref, da_ref, dd_ref, ds, s32, carry):
        t = pl.program_id(1)

        @pl.when(t == 0)
        def _():
            carry[...] = jnp.zeros_like(carry)
            dbb_ref[...] = jnp.zeros_like(dbb_ref)
            dcd_ref[...] = jnp.zeros_like(dcd_ref)
            da_ref[...] = jnp.zeros_like(da_ref)
            dd_ref[...] = jnp.zeros_like(dd_ref)

        dyp = dyg_ref[...].astype(F32) * _gelu_grad(ypre_ref[...])
        u = u_ref[...]
        dd_ref[...] += _colsum(dyp * u.astype(F32))
        dyp16 = dyp.astype(BF16)
        ds[...] = jnp.dot(dyp16, cdt_ref[0], preferred_element_type=F32)
        s16 = s_ref[...]
        s32[...] = s16.astype(F32)
        a_re, a_im = a_ref[:, :ph], a_ref[:, ph:]
        row = lax.broadcasted_iota(jnp.int32, (8, ph), 0)
        back = 8 - bl

        def step(kk, c):
            lre, lim, acr, aci = c
            r0 = pl.multiple_of((tm // 8 - 1 - kk) * 8, 8)
            dre, dim = ds[pl.ds(r0, 8), :ph], ds[pl.ds(r0, 8), ph:]
            sre, sim = s32[pl.ds(r0, 8), :ph], s32[pl.ds(r0, 8), ph:]
            ore, oim, ire, iim = lre, lim, lre, lim
            for sub in range(nsub - 1, -1, -1):
                xre, xim = pltpu.roll(lre, back, 0), pltpu.roll(lim, back, 0)
                lre = a_re * xre + a_im * xim + dre
                lim = a_re * xim - a_im * xre + dim
                if sub == nsub - 1:
                    ore, oim, ire, iim = lre, lim, xre, xim
                else:
                    sel = row < (sub + 1) * bl
                    ore, oim = jnp.where(sel, lre, ore), jnp.where(sel, lim, oim)
                    ire, iim = jnp.where(sel, xre, ire), jnp.where(sel, xim, iim)
            ds[pl.ds(r0, 8), :ph] = ore
            ds[pl.ds(r0, 8), ph:] = oim
            acr = acr + sre * ire + sim * iim
            aci = aci + sre * iim - sim * ire
            return ore, oim, acr, aci

        z = jnp.zeros((8, ph), F32)
        lre, lim, acr, aci = lax.fori_loop(0, tm // 8, step, (carry[:, :ph], carry[:, ph:], z, z))
        carry[:, :ph] = lre
        carry[:, ph:] = lim
        da_ref[:, :ph] += acr
        da_ref[:, ph:] += aci
        lam16 = ds[...].astype(BF16)
        du = jnp.dot(lam16, bbt_ref[0], preferred_element_type=F32) + d_ref[...] * dyp
        du_ref[...] = du.astype(du_ref.dtype)
        dbb_ref[0] += lax.dot_general(lam16, u, tn_dims, preferred_element_type=F32)
        dcd_ref[0] += lax.dot_general(s16, dyp16, tn_dims, preferred_element_type=F32)

    rev = lambda q, t: (nt - 1 - t, q)
    in_specs = [pl.BlockSpec((tm, ch), rev), pl.BlockSpec((tm, ch), rev), pl.BlockSpec((tm, ch), rev),
                pl.BlockSpec((tm, p2), rev), pl.BlockSpec((1, ch, p2), lambda q, t: (q, 0, 0)),
                pl.BlockSpec((1, p2, ch), lambda q, t: (q, 0, 0)), pl.BlockSpec((8, p2), lambda q, t: (q, 0)),
                pl.BlockSpec((1, ch), lambda q, t: (0, q))]
    out_specs = [pl.BlockSpec((tm, ch), rev), pl.BlockSpec((1, p2, ch), lambda q, t: (q, 0, 0)),
                 pl.BlockSpec((1, p2, ch), lambda q, t: (q, 0, 0)), pl.BlockSpec((8, p2), lambda q, t: (q, 0)),
                 pl.BlockSpec((1, ch), lambda q, t: (0, q))]
    out_shape = [jax.ShapeDtypeStruct((n, nc * ch), BF16), jax.ShapeDtypeStruct((nc, p2, ch), F32),
                 jax.ShapeDtypeStruct((nc, p2, ch), F32), jax.ShapeDtypeStruct((nc * 8, p2), F32),
                 jax.ShapeDtypeStruct((1, nc * ch), F32)]
    return _pcall(body, name, (nc, nt), in_specs, out_specs, out_shape,
                  [pltpu.VMEM((tm, p2), F32), pltpu.VMEM((tm, p2), F32), pltpu.VMEM((8, p2), F32)],
                  ("parallel", "arbitrary"), [dyg, ypre, proj, s_all, cdt, bbt, a8, dskip], comm)


_MESH = pl.DeviceIdType.MESH
_HBM = pl.BlockSpec(memory_space=pltpu.HBM)


def _position():
    return lax.axis_index("x"), lax.axis_index("y"), lax.axis_index("c")


def _other_chips(x, y):
    return [((1 - x, y), 2 * (1 - x) + y), ((x, 1 - y), 2 * x + 1 - y), ((1 - x, 1 - y), 2 * (1 - x) + 1 - y)]


def _swap_sibling(v, name):
    def body(v_ref, got_ref, send_sem, recv_sem):
        x, y, c = _position()
        cp = pltpu.make_async_remote_copy(src_ref=v_ref, dst_ref=got_ref, send_sem=send_sem, recv_sem=recv_sem,
                                          device_id=(x, y, 1 - c), device_id_type=_MESH)
        cp.start()
        cp.wait()

    return pl.pallas_call(
        body, name=name, in_specs=[_HBM], out_specs=_HBM, out_shape=jax.ShapeDtypeStruct(v.shape, v.dtype),
        scratch_shapes=[pltpu.SemaphoreType.DMA, pltpu.SemaphoreType.DMA])(v)


def _own_slot(gathered, own):
    return lax.dynamic_update_index_in_dim(gathered, own, _chip_index(), 0)


def _chip_allgather(v, name):
    def body(v_ref, out_ref, send_sems, recv_sems):
        x, y, c = _position()
        me = 2 * x + y
        sends = []
        for k, (chip, idx) in enumerate(_other_chips(x, y)):
            cp = pltpu.make_async_remote_copy(src_ref=v_ref, dst_ref=out_ref.at[me], send_sem=send_sems.at[k],
                                              recv_sem=recv_sems.at[k], device_id=(*chip, c), device_id_type=_MESH)
            cp.start()
            sends.append(cp)
        for k, (chip, idx) in enumerate(_other_chips(x, y)):
            pltpu.make_async_remote_copy(src_ref=v_ref, dst_ref=out_ref.at[idx], send_sem=send_sems.at[k],
                                         recv_sem=recv_sems.at[k], device_id=(*chip, c), device_id_type=_MESH).wait_recv()
        for cp in sends:
            cp.wait_send()

    out = pl.pallas_call(
        body, name=name, in_specs=[_HBM], out_specs=_HBM, out_shape=jax.ShapeDtypeStruct((4,) + tuple(v.shape), v.dtype),
        scratch_shapes=[pltpu.SemaphoreType.DMA((3,)), pltpu.SemaphoreType.DMA((3,))])(v)
    return _own_slot(out, v)


def _remote(src, dst, send_sems, recv_sems, s, device):
    return pltpu.make_async_remote_copy(src_ref=src, dst_ref=dst, send_sem=send_sems.at[s], recv_sem=recv_sems.at[s],
                                        device_id=device, device_id_type=_MESH)


class _Exchange:
    def __init__(self, ins, out_shapes, n_sems, aliases=None):
        self.ins, self.out_shapes, self.n_sems, self.aliases = list(ins), list(out_shapes), n_sems, aliases or {}

    def sem_shapes(self):
        return [pltpu.SemaphoreType.DMA((self.n_sems,)), pltpu.SemaphoreType.DMA((self.n_sems,))]


def _halves(ref, c, axis=0):
    h = ref.shape[axis] // 2
    idx = (slice(None),) * axis
    return ref.at[idx + (pl.ds(c * h, h),)], ref.at[idx + (pl.ds((1 - c) * h, h),)]


class _GatherShards(_Exchange):
    def __init__(self, ws):
        super().__init__(ws, [jax.ShapeDtypeStruct((N_CHIPS,) + tuple(w.shape), w.dtype) for w in ws], 6 * len(ws))

    def start(self, w_refs, out_refs, sems):
        send_sems, recv_sems = sems
        x, y, c = _position()
        me = 2 * x + y
        for i, (w, out) in enumerate(zip(w_refs, out_refs)):
            for k, (chip, idx) in enumerate(_other_chips(x, y)):
                _remote(_halves(w, c)[0], _halves(out.at[me], c)[0], send_sems, recv_sems, 6 * i + k, (*chip, c)).start()

    def finish(self, w_refs, out_refs, sems):
        send_sems, recv_sems = sems
        x, y, c = _position()
        sibling = (x, y, 1 - c)
        others = _other_chips(x, y)
        for i, out in enumerate(out_refs):
            for k, (chip, idx) in enumerate(others):
                landed = _halves(out.at[idx], c)[0]
                _remote(landed, landed, send_sems, recv_sems, 6 * i + k, (*chip, c)).wait_recv()
                _remote(landed, landed, send_sems, recv_sems, 6 * i + 3 + k, sibling).start()
        for i, (w, out) in enumerate(zip(w_refs, out_refs)):
            for k, (chip, idx) in enumerate(others):
                mine, theirs = _halves(out.at[idx], c)
                _remote(theirs, theirs, send_sems, recv_sems, 6 * i + 3 + k, sibling).wait_recv()
                _remote(mine, mine, send_sems, recv_sems, 6 * i + 3 + k, sibling).wait_send()
                _remote(_halves(w, c)[0], mine, send_sems, recv_sems, 6 * i + k, (*chip, c)).wait_send()


class _SwapHalves(_Exchange):
    def __init__(self, gs):
        shapes = [jax.ShapeDtypeStruct((g.shape[0], g.shape[1] // 2) + tuple(g.shape[2:]), g.dtype) for g in gs]
        super().__init__(gs, shapes, N_CHIPS * len(gs))

    def _copies(self, g_refs, out_refs, sems):
        x, y, c = _position()
        return [_remote(_halves(g.at[j], c)[1], out.at[j], sems[0], sems[1], N_CHIPS * i + j, (x, y, 1 - c))
                for i, (g, out) in enumerate(zip(g_refs, out_refs)) for j in range(N_CHIPS)]

    def start(self, g_refs, out_refs, sems):
        for cp in self._copies(g_refs, out_refs, sems):
            cp.start()

    def finish(self, g_refs, out_refs, sems):
        for cp in self._copies(g_refs, out_refs, sems):
            cp.wait()


class _ScatterPieces(_Exchange):
    def __init__(self, ps):
        super().__init__(ps, [jax.ShapeDtypeStruct((3,) + tuple(p.shape[1:]), p.dtype) for p in ps], 3 * len(ps))

    def _copies(self, p_refs, out_refs, sems):
        x, y, c = _position()
        return [_remote(p.at[idx], out.at[k], sems[0], sems[1], 3 * i + k, (*chip, c))
                for i, (p, out) in enumerate(zip(p_refs, out_refs)) for k, (chip, idx) in enumerate(_other_chips(x, y))]

    def start(self, p_refs, out_refs, sems):
        for cp in self._copies(p_refs, out_refs, sems):
            cp.start()

    def finish(self, p_refs, out_refs, sems):
        for cp in self._copies(p_refs, out_refs, sems):
            cp.wait()


class _JoinHalves(_Exchange):
    def __init__(self, rs):
        super().__init__(rs, [jax.ShapeDtypeStruct(r.shape, r.dtype) for r in rs], len(rs), {i: i for i in range(len(rs))})

    def start(self, r_refs, out_refs, sems):
        x, y, c = _position()
        for i, out in enumerate(out_refs):
            _remote(out.at[c], out.at[c], sems[0], sems[1], i, (x, y, 1 - c)).start()

    def finish(self, r_refs, out_refs, sems):
        x, y, c = _position()
        for i, out in enumerate(out_refs):
            _remote(out.at[c], out.at[c], sems[0], sems[1], i, (x, y, 1 - c)).wait_send()
            _remote(out.at[1 - c], out.at[1 - c], sems[0], sems[1], i, (x, y, 1 - c)).wait_recv()


def _run_exchange(ex, name):
    def body(*refs):
        ins, outs, sems = refs[:len(ex.ins)], refs[len(ex.ins):len(ex.ins) + len(ex.out_shapes)], refs[-2:]
        ex.start(ins, outs, sems)
        ex.finish(ins, outs, sems)

    return pl.pallas_call(body, name=name, in_specs=[_HBM] * len(ex.ins), out_specs=[_HBM] * len(ex.out_shapes),
                          out_shape=ex.out_shapes, scratch_shapes=ex.sem_shapes(), input_output_aliases=ex.aliases)(*ex.ins)


def _pcall(body, name, grid, in_specs, out_specs, out_shape, scratch_shapes, semantics, args, comm=None):
    if comm is None:
        return pl.pallas_call(body, name=name, grid=grid, in_specs=in_specs, out_specs=out_specs, out_shape=out_shape,
                              scratch_shapes=scratch_shapes, compiler_params=_cparams(semantics))(*args)
    n_in, n_out, n_scr, ci, co = len(in_specs), len(out_specs), len(scratch_shapes), len(comm.ins), len(comm.out_shapes)

    def wrapped(*refs):
        parts, a = [], 0
        for k in (n_in, ci, n_out, co, n_scr, 2):
            parts.append(refs[a:a + k])
            a += k
        ins, cins, outs, couts, scr, sems = parts
        ids = [pl.program_id(i) for i in range(len(grid))]
        first = functools.reduce(jnp.logical_and, [i == 0 for i in ids])
        last = functools.reduce(jnp.logical_and, [i == g - 1 for i, g in zip(ids, grid)])

        @pl.when(first)
        def _():
            comm.start(cins, couts, sems)

        body(*ins, *outs, *scr)

        @pl.when(last)
        def _():
            comm.finish(cins, couts, sems)

    res = pl.pallas_call(
        wrapped, name=name, grid=grid, in_specs=list(in_specs) + [_HBM] * ci, out_specs=list(out_specs) + [_HBM] * co,
        out_shape=list(out_shape) + comm.out_shapes, scratch_shapes=list(scratch_shapes) + comm.sem_shapes(),
        compiler_params=_cparams(("arbitrary",) * len(grid)))(*args, *comm.ins)
    return res[:n_out], res[n_out:]


ATT_WINDOW = 128
PHASES = 16
_NT = (((1,), (1,)), ((), ()))
_TN = (((0,), (0,)), ((), ()))


PERM_LANES = 512


def _phase_perm(bl):
    t = 16 * PHASES * bl
    col = jnp.arange(t)
    i, r, b = col // (PHASES * bl), (col // bl) % PHASES, col % bl
    return (jnp.arange(t)[:, None] == ((b * PHASES + r) * 16 + i)[None, :]).astype(BF16)


def _to_phase_order(x, bl, col0=0, width=None):
    n = x.shape[0]
    width = width or x.shape[1]
    t = 16 * PHASES * bl
    g = n // bl // PHASES
    tn = min(PERM_LANES, width)

    def body(p_ref, x_ref, o_ref):
        o_ref[...] = jnp.dot(p_ref[...], x_ref[...], preferred_element_type=F32).astype(o_ref.dtype).reshape(o_ref.shape)

    out = pl.pallas_call(
        body, name="to_phase", grid=(n // t, width // tn),
        in_specs=[pl.BlockSpec((t, t), lambda i, j: (0, 0)), pl.BlockSpec((t, tn), lambda i, j: (i, col0 // tn + j))],
        out_specs=pl.BlockSpec((bl * PHASES, 16, tn), lambda i, j: (0, i, j)),
        out_shape=jax.ShapeDtypeStruct((bl * PHASES, g, width), x.dtype),
        compiler_params=_cparams(("parallel", "parallel")))(_phase_perm(bl), x)
    return out.reshape(n, width)


def _from_phase_order(y, bl):
    n, width = y.shape
    t = 16 * PHASES * bl
    g = n // bl // PHASES
    tn = min(PERM_LANES, width)

    def body(p_ref, y_ref, o_ref):
        o_ref[...] = jnp.dot(p_ref[...], y_ref[...].reshape(t, tn), preferred_element_type=F32).astype(o_ref.dtype)

    return pl.pallas_call(
        body, name="from_phase", grid=(n // t, width // tn),
        in_specs=[pl.BlockSpec((t, t), lambda i, j: (0, 0)), pl.BlockSpec((bl * PHASES, 16, tn), lambda i, j: (0, i, j))],
        out_specs=pl.BlockSpec((t, tn), lambda i, j: (i, j)), out_shape=jax.ShapeDtypeStruct((n, width), y.dtype),
        compiler_params=_cparams(("parallel", "parallel")))(_phase_perm(bl).T, y.reshape(bl * PHASES, g, width))


def _att_geometry(p, n, bl):
    g = n // bl // PHASES
    if p == 0:
        return ((bl, PHASES, g), (bl, g // 16), (None, PHASES, 16),
                lambda sh: (lambda b, a: (b, 0, jnp.maximum(a + sh, 0))), 256, 16, lambda ids: ids[1] == 0)
    if p == 1:
        return ((bl, 4, 4, g), (bl, 2, g // 32), (None, 4, 2, 32),
                lambda sh: (lambda b, r, a: (b, 0, r, jnp.maximum(a + sh, 0))), 128, 32, lambda ids: ids[2] == 0)
    return ((bl * PHASES, g), (bl * PHASES // 2,), (2, g), lambda sh: (lambda s: (s, 0)), g, g, None)


def _att_units(p, n, bl):
    g = n // bl // PHASES
    full = slice(None)
    if p == 0:
        return [(full, full)], (PHASES, 16)
    if p == 1:
        return [(full, u, full) for u in range(2)], (4, 32)
    return [(u, full) for u in range(2)], (g,)


def _att_masks(p, qb, chunk):
    def pos(idx):
        return (idx % chunk) * (qb // chunk) + idx // chunk

    dq = pos(lax.broadcasted_iota(jnp.int32, (qb, qb), 0))
    dk = pos(lax.broadcasted_iota(jnp.int32, (qb, qb), 1))
    dist = dq - dk
    return jnp.logical_and(dist >= 0, dist <= ATT_WINDOW), dist + qb <= ATT_WINDOW


def _att_call(p, n, bl, c, body, name, ins, outs):
    prefix, grid, blk, idx_fn, qb, chunk, _ = _att_geometry(p, n, bl)

    def spec(cb, sh):
        f = idx_fn(sh)
        return pl.BlockSpec(blk + (c,), lambda *ids, f=f, cb=cb: f(*ids) + (cb,))

    in_specs = [spec(cb, sh) for _, cb, sh in ins]
    out_specs = [spec(0, 0) for _ in outs]
    out_shape = [jax.ShapeDtypeStruct(prefix + (c,), dt) for dt in outs]
    res = pl.pallas_call(body, name=name, grid=grid, in_specs=in_specs, out_specs=out_specs, out_shape=out_shape,
                         compiler_params=_cparams(("parallel",) * len(grid)))(*[a.reshape(prefix + (a.shape[1],)) for a, _, _ in ins])
    return [r.reshape(n, c) for r in res]


class _AttTiles:
    def __init__(self, p, n, bl, c, first):
        _, _, _, _, qb, chunk, _ = _att_geometry(p, n, bl)
        units, self.unit_shape = _att_units(p, n, bl)
        self.split = p == 0
        self.rows = qb // 2 if self.split else qb
        halves = (0, 1) if self.split else (None,)
        self.tiles = [(u, pl.ds(lt * 128, 128), h) for u in units for lt in range(c // 128) for h in halves]
        self.mask_cur, mp = _att_masks(p, self.rows, chunk // 2 if self.split else chunk)
        gated = mp if first is None else jnp.logical_and(mp, jnp.logical_not(first))
        self.mask_prev = [mp if h == 1 else gated for _, _, h in self.tiles]

    def _half(self, x, h):
        return x.astype(F32)[:, 8 * h:8 * h + 8, :].reshape(self.rows, 128).astype(x.dtype)

    def cur(self, ref, t):
        u, ls, h = self.tiles[t]
        x = ref[u + (ls,)]
        return x.reshape(self.rows, 128) if h is None else self._half(x, h)

    def prev(self, cur_ref, prev_ref, t):
        u, ls, h = self.tiles[t]
        if h is None:
            return prev_ref[u + (ls,)].reshape(self.rows, 128)
        return self._half(prev_ref[u + (ls,)], 1) if h == 0 else self._half(cur_ref[u + (ls,)], 0)

    def store(self, ref, vals):
        if not self.split:
            for (u, ls, _), v in zip(self.tiles, vals):
                ref[u + (ls,)] = v.astype(ref.dtype).reshape(self.unit_shape + (128,))
            return
        for k in range(len(self.tiles) // 2):
            u, ls, _ = self.tiles[2 * k]
            parts = [v.astype(F32).reshape(self.unit_shape[0], 8, 128) for v in vals[2 * k:2 * k + 2]]
            ref[u + (ls,)] = jnp.concatenate(parts, axis=1).astype(ref.dtype)

    def fold_keys(self, cur_vals, prev_vals):
        if not self.split:
            return cur_vals, prev_vals
        own, before = [], []
        for k in range(len(self.tiles) // 2):
            own += [cur_vals[2 * k] + prev_vals[2 * k + 1], cur_vals[2 * k + 1]]
            before += [jnp.zeros_like(prev_vals[2 * k]), prev_vals[2 * k]]
        return own, before


def _att_fwd(p, qkv, qcb, bl, c, heads):
    n = qkv.shape[0]
    _, grid, _, _, _, _, first_fn = _att_geometry(p, n, bl)
    n_grid = len(grid)
    has_prev = first_fn is not None
    e = c // heads
    scale = e ** -0.5

    def body(*refs):
        if has_prev:
            q_ref, kc_ref, kp_ref, vc_ref, vp_ref, o_ref, l_ref = refs
        else:
            q_ref, kc_ref, vc_ref, o_ref, l_ref = refs
            kp_ref = vp_ref = None
        tl = _AttTiles(p, n, bl, c, first_fn([pl.program_id(a) for a in range(n_grid)]) if has_prev else None)
        mc, n_t = tl.mask_cur, len(tl.tiles)
        lo = lax.broadcasted_iota(jnp.int32, (tl.rows, 128), 1) < e
        ones = jnp.ones((tl.rows, 128), BF16)
        items = [(t, h) for t in range(n_t) for h in range(2)]
        dot = functools.partial(jnp.dot, preferred_element_type=F32)
        q2 = [tl.cur(q_ref, t) for t in range(n_t)]
        kc = [tl.cur(kc_ref, t) for t in range(n_t)]
        qm = [jnp.where(lo if h == 0 else jnp.logical_not(lo), q2[t], jnp.zeros_like(q2[t])) for t, h in items]
        sc = [jnp.where(mc, lax.dot_general(qm[i], kc[t], _NT, preferred_element_type=F32) * scale, -jnp.inf)
              for i, (t, h) in enumerate(items)]
        m = [jnp.max(s, axis=1, keepdims=True) for s in sc]
        if has_prev:
            kp = [tl.prev(kc_ref, kp_ref, t) for t in range(n_t)]
            sp = [jnp.where(tl.mask_prev[t], lax.dot_general(qm[i], kp[t], _NT, preferred_element_type=F32) * scale, -jnp.inf)
                  for i, (t, h) in enumerate(items)]
            m = [jnp.maximum(a, jnp.max(s, axis=1, keepdims=True)) for a, s in zip(m, sp)]
        pc = [jnp.exp(s - a).astype(BF16) for s, a in zip(sc, m)]
        vc = [tl.cur(vc_ref, t) for t in range(n_t)]
        acc = [dot(pc[i], vc[t]) for i, (t, h) in enumerate(items)]
        den = [dot(x, ones) for x in pc]
        if has_prev:
            pp = [jnp.exp(s - a).astype(BF16) for s, a in zip(sp, m)]
            vp = [tl.prev(vc_ref, vp_ref, t) for t in range(n_t)]
            acc = [a + dot(pp[i], vp[t]) for i, ((t, h), a) in enumerate(zip(items, acc))]
            den = [d + dot(x, ones) for d, x in zip(den, pp)]
        oh = [a / d for a, d in zip(acc, den)]
        lh = [a + jnp.log(d) for a, d in zip(m, den)]
        tl.store(o_ref, [jnp.where(lo, oh[2 * t], oh[2 * t + 1]) for t in range(n_t)])
        tl.store(l_ref, [jnp.where(lo, lh[2 * t], lh[2 * t + 1]) for t in range(n_t)])

    kcb, vcb = 3, 4
    ins = [(qkv, qcb, 0), (qkv, kcb, 0)] + ([(qkv, kcb, -1)] if has_prev else []) + [(qkv, vcb, 0)] + ([(qkv, vcb, -1)] if has_prev else [])
    return _att_call(p, n, bl, c, body, name=f"att_fwd{p}", ins=ins, outs=[BF16, F32])


def _att_bwd(p, qkv, qcb, o, do, lse, bl, c, heads):
    n = qkv.shape[0]
    _, grid, _, _, _, _, first_fn = _att_geometry(p, n, bl)
    n_grid = len(grid)
    has_prev = first_fn is not None
    e = c // heads
    scale = e ** -0.5

    def body(*refs):
        if has_prev:
            q_ref, kc_ref, kp_ref, vc_ref, vp_ref, o_ref, do_ref, l_ref, dq_ref, dkc_ref, dkp_ref, dvc_ref, dvp_ref = refs
        else:
            q_ref, kc_ref, vc_ref, o_ref, do_ref, l_ref, dq_ref, dkc_ref, dvc_ref = refs
            kp_ref = vp_ref = None
        tl = _AttTiles(p, n, bl, c, first_fn([pl.program_id(a) for a in range(n_grid)]) if has_prev else None)
        mc, n_t = tl.mask_cur, len(tl.tiles)
        lo = lax.broadcasted_iota(jnp.int32, (tl.rows, 128), 1) < e
        items = [(t, h) for t in range(n_t) for h in range(2)]
        nt_dot = lambda a, b: lax.dot_general(a, b, _NT, preferred_element_type=F32)
        tn_dot = lambda a, b: lax.dot_general(a, b, _TN, preferred_element_type=F32)
        dot = functools.partial(jnp.dot, preferred_element_type=F32)
        sel = [lo if h == 0 else jnp.logical_not(lo) for t, h in items]
        q2, kc, vc, do2 = ([tl.cur(r, t) for t in range(n_t)] for r in (q_ref, kc_ref, vc_ref, do_ref))
        qm = [jnp.where(sel[i], q2[t], jnp.zeros_like(q2[t])) for i, (t, h) in enumerate(items)]
        dom = [jnp.where(sel[i], do2[t], jnp.zeros_like(do2[t])) for i, (t, h) in enumerate(items)]
        dod = [do2[t].astype(F32) * tl.cur(o_ref, t).astype(F32) for t in range(n_t)]
        lcol = [tl.cur(l_ref, t)[:, h * e:h * e + 1] for t, h in items]
        corr = [-jnp.sum(jnp.where(sel[i], dod[t], 0.0), axis=1, keepdims=True) for i, (t, h) in enumerate(items)]
        pc = [jnp.exp(jnp.where(mc, nt_dot(qm[i], kc[t]) * scale, -jnp.inf) - lcol[i]) for i, (t, h) in enumerate(items)]
        dsc = [(pc[i] * (nt_dot(dom[i], vc[t]) + corr[i]) * scale).astype(BF16) for i, (t, h) in enumerate(items)]
        pc = [x.astype(BF16) for x in pc]
        dq = [dot(dsc[i], kc[t]) for i, (t, h) in enumerate(items)]
        dkc = [tn_dot(dsc[2 * t], qm[2 * t]) + tn_dot(dsc[2 * t + 1], qm[2 * t + 1]) for t in range(n_t)]
        dvc = [tn_dot(pc[2 * t], dom[2 * t]) + tn_dot(pc[2 * t + 1], dom[2 * t + 1]) for t in range(n_t)]
        if has_prev:
            kp = [tl.prev(kc_ref, kp_ref, t) for t in range(n_t)]
            vp = [tl.prev(vc_ref, vp_ref, t) for t in range(n_t)]
            pp = [jnp.exp(jnp.where(tl.mask_prev[t], nt_dot(qm[i], kp[t]) * scale, -jnp.inf) - lcol[i]) for i, (t, h) in enumerate(items)]
            dsp = [(pp[i] * (nt_dot(dom[i], vp[t]) + corr[i]) * scale).astype(BF16) for i, (t, h) in enumerate(items)]
            pp = [x.astype(BF16) for x in pp]
            dq = [a + dot(dsp[i], kp[t]) for i, ((t, h), a) in enumerate(zip(items, dq))]
            dkp = [tn_dot(dsp[2 * t], qm[2 * t]) + tn_dot(dsp[2 * t + 1], qm[2 * t + 1]) for t in range(n_t)]
            dvp = [tn_dot(pp[2 * t], dom[2 * t]) + tn_dot(pp[2 * t + 1], dom[2 * t + 1]) for t in range(n_t)]
            (dkc, dkp), (dvc, dvp) = tl.fold_keys(dkc, dkp), tl.fold_keys(dvc, dvp)
            tl.store(dkp_ref, dkp)
            tl.store(dvp_ref, dvp)
        tl.store(dq_ref, [jnp.where(lo, dq[2 * t], dq[2 * t + 1]) for t in range(n_t)])
        tl.store(dkc_ref, dkc)
        tl.store(dvc_ref, dvc)

    kcb, vcb = 3, 4
    ins = [(qkv, qcb, 0), (qkv, kcb, 0)] + ([(qkv, kcb, -1)] if has_prev else []) + [(qkv, vcb, 0)] + ([(qkv, vcb, -1)] if has_prev else [])
    ins += [(o, 0, 0), (do, 0, 0), (lse, 0, 0)]
    res = _att_call(p, n, bl, c, body, name=f"att_bwd{p}", ins=ins, outs=[BF16] * (5 if has_prev else 3))
    if has_prev:
        dq, dkc, dkp, dvc, dvp = res
        return dq, dkc, dkp, dvc, dvp
    dq, dkc, dvc = res
    return dq, dkc, None, dvc, None


def _dqkv_to_rows(dqs, dk_parts, dv_parts, bl):
    n, c = dqs[0].shape
    t = 16 * PHASES * bl
    g = n // bl // PHASES
    nb = n // t
    shifts = [_att_geometry(p, n, bl)[5] // 16 for p in range(len(dk_parts))]
    view = lambda a: a.reshape(bl * PHASES, g, c)
    blk = lambda s: pl.BlockSpec((bl * PHASES, 16, c), lambda i, j, s=s: (0, jnp.minimum(i + s, nb - 1), 0))
    ins, specs, layout = [], [], []
    for a in dqs:
        layout.append([(len(ins), 0)])
        ins.append(view(a))
        specs.append(blk(0))
    for parts in (dk_parts, dv_parts):
        terms = []
        for (own, prv), s in zip(parts, shifts):
            terms.append((len(ins), 0))
            ins.append(view(own))
            specs.append(blk(0))
            if prv is not None:
                terms.append((len(ins), s))
                ins.append(view(prv))
                specs.append(blk(s))
        layout.append(terms)

    def body(p_ref, *refs):
        o_ref = refs[-1]
        i, j = pl.program_id(0), pl.program_id(1)
        for col, terms in enumerate(layout):
            @pl.when(j == col)
            def _(terms=terms):
                if len(terms) == 1:
                    x = refs[terms[0][0]][...]
                else:
                    x = 0.0
                    for pos, s in terms:
                        v = refs[pos][...].astype(F32)
                        x = x + (v if s == 0 else jnp.where(i + s < nb, v, 0.0))
                    x = x.astype(BF16)
                o_ref[...] = jnp.dot(p_ref[...], x.reshape(t, c), preferred_element_type=F32).astype(o_ref.dtype)

    return pl.pallas_call(
        body, name="dqkv_to_rows", grid=(nb, len(layout)), in_specs=[pl.BlockSpec((t, t), lambda i, j: (0, 0))] + specs,
        out_specs=pl.BlockSpec((t, c), lambda i, j: (i, j)), out_shape=jax.ShapeDtypeStruct((n, len(layout) * c), BF16),
        compiler_params=_cparams(("parallel", "arbitrary")))(_phase_perm(bl).T, *ins)


def _attention_fwd(proj, c, bl, heads):
    n = proj.shape[0]
    qkv = _to_phase_order(proj, bl, col0=c, width=5 * c)
    outs = [_att_fwd(p, qkv, p, bl, c, heads) for p in range(3)]
    ins = [("row", o, c, 0) for o, _ in outs] + [("row", l, c, 0) for _, l in outs]
    o, lse = _rowwise(_combine_fwd_fn, "comb_fwd", n, ins, [(c, BF16), (c, F32)])
    return _from_phase_order(o, bl), (qkv, o, lse)


def _attention_bwd(do_tb, saved, bl, heads):
    qkv, o, lse = saved
    n, c = do_tb.shape
    do = _to_phase_order(do_tb, bl)
    dqs, dks, dvs = [], [], []
    for p in range(3):
        dq, dkc, dkp, dvc, dvp = _att_bwd(p, qkv, p, o, do, lse, bl, c, heads)
        dqs.append(dq)
        dks.append((dkc, dkp))
        dvs.append((dvc, dvp))
    return _dqkv_to_rows(dqs, dks, dvs, bl)


ATT_HEADS = 8
SSM_GROUPS, SSM_STATE, SSM_GROUP = 32, 64, 16


def _row(v):
    return v.reshape(1, -1)


ROWS_TILE = 128


def _to_rows(x):
    bl, seq, d = x.shape

    def body(x_ref, o_ref):
        o_ref[...] = jnp.stack([x_ref[b] for b in range(bl)], axis=1).reshape(ROWS_TILE * bl, d)

    return pl.pallas_call(
        body, name="to_rows", grid=(seq // ROWS_TILE,), in_specs=[pl.BlockSpec((bl, ROWS_TILE, d), lambda i: (0, i, 0))],
        out_specs=pl.BlockSpec((ROWS_TILE * bl, d), lambda i: (i, 0)), out_shape=jax.ShapeDtypeStruct((seq * bl, d), x.dtype),
        compiler_params=_cparams(("parallel",)))(x)


def _from_rows(y, bl):
    n, d = y.shape
    seq = n // bl

    def body(y_ref, o_ref):
        v = y_ref[...].reshape(ROWS_TILE, bl, d)
        for b in range(bl):
            o_ref[b] = v[:, b, :]

    return pl.pallas_call(
        body, name="from_rows", grid=(seq // ROWS_TILE,), in_specs=[pl.BlockSpec((ROWS_TILE * bl, d), lambda i: (i, 0))],
        out_specs=pl.BlockSpec((bl, ROWS_TILE, d), lambda i: (0, i, 0)), out_shape=jax.ShapeDtypeStruct((bl, seq, d), y.dtype),
        compiler_params=_cparams(("parallel",)))(y)


def _carried(result, carry, key, hidden):
    if carry.get(key) is None:
        return result
    result, hidden[key] = result
    return result


def _layer_fwd(x, w, p, bl, carry, late=None):
    n, d = x.shape
    c = d // 2
    hidden = {}
    h, = _rowwise(_rms_fwd_fn, "rms_fwd", n, [("row", x, d, 0), ("par", _row(p["norm1_g"]))], [(d, BF16)])
    proj = _carried(_mm(h, w["w_in"], "nn", BF16, "mm_in", comm=carry.get("mm_in")), carry, "mm_in", hidden)
    if late is not None:
        w = dict(w, **late(hidden["mm_in"]))
    disc, disc_vjp = jax.vjp(_ssm_disc, p["ssm_lambda_re"], p["ssm_lambda_im"], p["ssm_log_dt"], p["ssm_b_re"], p["ssm_b_im"])
    bbd, cdm, a8 = _ssm_pack(*disc, p["ssm_c_re"], p["ssm_c_im"])
    ypre, yg, s_all = _carried(_ssm_fwd(proj, bbd, cdm, a8, _row(p["ssm_d"]), bl, "ssm_fwd", comm=carry.get("ssm_fwd")),
                               carry, "ssm_fwd", hidden)
    zs = _mm(yg, w["w_ssm_glu"], "nn", BF16, "mm_glu")
    o, att = _attention_fwd(proj, c, bl, ATT_HEADS)
    ya = _mm(o, w["w_att_up"], "nn", BF16, "mm_att")
    w32 = jnp.concatenate([p["conv_w"], jnp.zeros((1, c), F32)], axis=0)
    hc, hconv = _conv_fwd(proj, 6, w32, _row(p["conv_b"]), _row(p["conv_ln_g"]), _row(p["conv_ln_b"]), bl, c, "conv_fwd")
    yc = _mm(hc, w["w_conv_pw2"], "nn", BF16, "mm_pw2")
    gates = [("row", proj, d, 4), ("row", proj, d, 5), ("row", proj, d, 6), ("par", _row(p["b_gate"]))]
    branches = [("row", zs, 2 * d, 0), ("row", ya, d, 0), ("row", yc, d, 0)]
    merged, = _rowwise(_merge_fwd_fn, "merge_fwd", n, gates + branches, [(d, BF16)])
    xm = _mm(merged, w["w_out"], "nn", F32, "mm_out", res=x)
    h2, = _rowwise(_rms_fwd_fn, "rms_fwd", n, [("row", xm, d, 0), ("par", _row(p["norm2_g"]))], [(d, BF16)])
    z = _carried(_mm(h2, w["w_ffn_in"], "nn", BF16, "mm_ffn_in", comm=carry.get("mm_ffn_in")), carry, "mm_ffn_in", hidden)
    f = z.shape[1] // 2
    a, = _rowwise(_swiglu_fwd_fn, "swiglu_fwd", n, [("row", z, 2 * f, 0)], [(f, BF16)], tm=256)
    xo = _mm(a, w["w_ffn_out"], "nn", F32, "mm_ffn_out", res=xm)
    saved = dict(x=x, h=h, proj=proj, disc_vjp=disc_vjp, bbd=bbd, cdm=cdm, a8=a8, ypre=ypre, yg=yg, s_all=s_all, zs=zs, o=o,
                 att=att, ya=ya, w32=w32, hc=hc, hconv=hconv, yc=yc, gates=gates, branches=branches, merged=merged, xm=xm,
                 h2=h2, z=z, a=a)
    return xo, saved, hidden, w


EARLY_GRADS = ("w_ffn_out", "w_ffn_in", "w_out")


def _layer_bwd(dxo, s, w, p, bl, carry, early=None):
    dxo, dxo16 = dxo
    n, d = dxo.shape
    c = d // 2
    g, bufs, hidden = {}, {}, {}
    f = s["a"].shape[1]

    def dw(key, a, dy, name):
        bufs[key] = _mm_dw(a, dy, name, 1 if key in ROW_SHARDED else N_CHIPS)

    da = _mm(dxo16, w["w_ffn_out"], "nt", BF16, "mm_ffn_out_dx")
    dw("w_ffn_out", s["a"], dxo16, "mm_ffn_out_dw")
    dz, = _rowwise(_swiglu_bwd_fn, "swiglu_bwd", n, [("row", s["z"], 2 * f, 0), ("row", da, f, 0)], [(2 * f, BF16)], tm=256)
    dh2 = _mm(dz, w["w_ffn_in"], "nt", F32, "mm_ffn_in_dx")
    dw("w_ffn_in", s["h2"], dz, "mm_ffn_in_dw")
    dxm, dxm16, dg2 = _rowwise(_rms_bwd_fn, "rms_bwd", n, [("row", s["xm"], d, 0), ("par", _row(p["norm2_g"])), ("row", dh2, d, 0),
                                                          ("row", dxo, d, 0)], [(d, F32), (d, BF16)], [d])
    g["norm2_g"] = dg2[0]
    dmerged = _mm(dxm16, w["w_out"], "nt", BF16, "mm_out_dx")
    dw("w_out", s["merged"], dxm16, "mm_out_dw")
    if early is not None:
        carry = dict(carry, mm_in_dx=early({k: bufs[k] for k in EARLY_GRADS}))
    dgl, dzs, dya, dyc, dbg = _rowwise(_merge_bwd_fn, "merge_bwd", n, s["gates"] + s["branches"] + [("row", dmerged, d, 0)],
                                       [(3 * d, BF16), (2 * d, BF16), (d, BF16), (d, BF16)], [3 * d], tm=256)
    g["b_gate"] = dbg[0]
    dyg = _mm(dzs, w["w_ssm_glu"], "nt", BF16, "mm_glu_dx")
    dw("w_ssm_glu", s["yg"], dzs, "mm_glu_dw")
    du, dbb, dcd, dab, dd = _carried(
        _ssm_bwd(dyg, s["ypre"], s["proj"], s["s_all"], s["cdm"].transpose(0, 2, 1), s["bbd"].transpose(0, 2, 1), s["a8"],
                 _row(p["ssm_d"]), bl, "ssm_bwd", comm=carry.get("ssm_bwd")), carry, "ssm_bwd", hidden)
    dab_re, dab_im, dbb_re, dbb_im, g["ssm_c_re"], g["ssm_c_im"] = _ssm_unpack(dbb, dcd, dab, SSM_GROUPS, SSM_STATE, SSM_GROUP)
    (g["ssm_lambda_re"], g["ssm_lambda_im"], g["ssm_log_dt"], g["ssm_b_re"],
     g["ssm_b_im"]) = s["disc_vjp"]((dab_re, dab_im, dbb_re, dbb_im))
    g["ssm_d"] = dd[0]
    do = _mm(dya, w["w_att_up"], "nt", BF16, "mm_att_dx")
    dw("w_att_up", s["o"], dya, "mm_att_dw")
    dqkv = _attention_bwd(do, s["att"], bl, ATT_HEADS)
    dhc = _mm(dyc, w["w_conv_pw2"], "nt", BF16, "mm_pw2_dx")
    dw("w_conv_pw2", s["hc"], dyc, "mm_pw2_dw")
    dcv, dcw, dcb, dlg, dlb = _carried(
        _conv_bwd(s["proj"], 6, dhc, s["hconv"], s["w32"], _row(p["conv_ln_g"]), _row(p["conv_ln_b"]), bl, c, "conv_bwd",
                  comm=carry.get("conv_bwd")), carry, "conv_bwd", hidden)
    g["conv_w"], g["conv_b"], g["conv_ln_g"], g["conv_ln_b"] = dcw, dcb[0], dlg[0], dlb[0]
    dproj = jnp.concatenate([du, dqkv, dcv, dgl], axis=1)
    dh = _carried(_mm(dproj, w["w_in"], "nt", F32, "mm_in_dx", comm=carry.get("mm_in_dx")), carry, "mm_in_dx", hidden)
    dw("w_in", s["h"], dproj, "mm_in_dw")
    dx, dx16, dg1 = _rowwise(_rms_bwd_fn, "rms_bwd", n, [("row", s["x"], d, 0), ("par", _row(p["norm1_g"])), ("row", dh, d, 0),
                                                        ("row", dxm, d, 0)], [(d, F32), (d, BF16)], [d])
    g["norm1_g"] = dg1[0]
    return (dx, dx16), g, bufs, hidden


WEIGHTS = ['norm1_g', 'w_in', 'b_gate', 'ssm_lambda_re', 'ssm_lambda_im', 'ssm_log_dt', 'ssm_b_re', 'ssm_b_im', 'ssm_c_re',
           'ssm_c_im', 'ssm_d', 'w_ssm_glu', 'w_att_up', 'conv_w', 'conv_b', 'conv_ln_g', 'conv_ln_b', 'w_conv_pw2', 'w_out',
           'norm2_g', 'w_ffn_in', 'w_ffn_out', 'final_g']
BIG = ['w_in', 'w_ssm_glu', 'w_att_up', 'w_conv_pw2', 'w_out', 'w_ffn_in', 'w_ffn_out']
ROW_SHARDED = ('w_out', 'w_ffn_out')
SMALL = [k for k in WEIGHTS if k not in BIG]
LANES = 1024
N_CHIPS = 4
ROW_TILE_BYTES = 36 * 1024 * 1024
MIN_SHARD_TILE = 1024


def _pad_rows(a, rows):
    return jnp.concatenate([a, jnp.zeros((rows - a.shape[0],) + a.shape[1:], a.dtype)], axis=0) if rows > a.shape[0] else a


def _row_tile(rows, width, n_arrays):
    best = 16
    for t in range(16, rows + 1, 16):
        if rows % t == 0 and t * width * 4 * n_arrays * 2 <= ROW_TILE_BYTES:
            best = t
    return best


def _flat_fn(fn, name, ins, n_out, rows):
    return _rowwise(fn, name, rows, [("row", a, LANES, 0) for a in ins], [(LANES, F32)] * n_out, tm=rows)


def _reduce_prepare(bufs):
    landed = _run_exchange(_SwapHalves([b16 for _, b16 in bufs]), "rs_swap")
    kept, p16s = [], []
    for (b32, _), la in zip(bufs, landed):
        s, m, cs = b32.shape
        h = m // 2
        tm = _row_tile(h, cs, 3)
        halves = b32.reshape(s, 2, h, cs)

        def body(g_ref, l_ref, o16):
            o16[...] = (g_ref[...] + l_ref[...].astype(F32)).astype(BF16)

        piece = pl.BlockSpec((None, tm, cs), lambda j, i: (j, i, 0))
        mine = pl.BlockSpec((None, None, tm, cs), lambda j, i: (j, _core_index(), i, 0))
        p16s.append(pl.pallas_call(
            body, name="rs_add", grid=(s, h // tm), in_specs=[mine, piece], out_specs=piece,
            out_shape=jax.ShapeDtypeStruct((s, h, cs), BF16), compiler_params=_cparams(("parallel", "parallel")))(halves, la))
        kept.append((halves, la))
    return kept, p16s


def _reduce_finish(kept, arrived):
    reduced = []
    for (halves, la), lb in zip(kept, arrived):
        _, h, cs = lb.shape
        tm = _row_tile(h, cs, 6)

        def body(g_ref, l_ref, a_ref, b_ref, c_ref, o_ref):
            own = g_ref[...] + l_ref[...].astype(F32)
            o_ref[...] = ((own + a_ref[...].astype(F32)) + b_ref[...].astype(F32)) + c_ref[...].astype(F32)

        mine = pl.BlockSpec((None, None, tm, cs), lambda i: (_chip_index(), _core_index(), i, 0))
        sibling = pl.BlockSpec((None, tm, cs), lambda i: (_chip_index(), i, 0))
        other = [pl.BlockSpec((None, tm, cs), lambda i, k=k: (k, i, 0)) for k in range(3)]
        half = pl.BlockSpec((None, tm, cs), lambda i: (_core_index(), i, 0))
        reduced.append(pl.pallas_call(
            body, name="rs_sum", grid=(h // tm,), in_specs=[mine, sibling] + other, out_specs=half,
            out_shape=jax.ShapeDtypeStruct((2, h, cs), F32), compiler_params=_cparams(("parallel",)))(halves, la, lb, lb, lb))
    joined = _run_exchange(_JoinHalves(reduced), "rs_gather")
    return [j.reshape(2 * j.shape[1], j.shape[2]) for j in joined]


def _adamw_layers(w, g_layers, m, v):
    depth, rows, cs = w.shape
    tm = _row_tile(rows, cs, 8)
    nb = rows // tm

    def body(*refs):
        w_ref, m_ref, v_ref = refs[:3]
        g_refs = refs[3:3 + depth]
        go_ref, d_ref, mo_ref, vo_ref = refs[3 + depth:]
        layer = pl.program_id(0)
        g = g_refs[0][...]
        for l in range(1, depth):
            g = jnp.where(layer == l, g_refs[l][...], g)
        delta, mo, vo = _adamw_fn(w_ref[...], g, m_ref[...], v_ref[...])
        go_ref[...], d_ref[...], mo_ref[...], vo_ref[...] = g, delta, mo, vo

    stacked = pl.BlockSpec((None, tm, cs), lambda l, i: (l, i, 0))
    g_specs = [pl.BlockSpec((tm, cs), lambda l, i, k=k: (jnp.where(l == k, i, jnp.where(l < k, 0, nb - 1)), 0)) for k in range(depth)]
    return pl.pallas_call(
        body, name="adamw", grid=(depth, nb), in_specs=[stacked] * 3 + g_specs, out_specs=[stacked] * 4,
        out_shape=[jax.ShapeDtypeStruct(w.shape, F32)] * 4, compiler_params=_cparams(("arbitrary", "arbitrary")))(w, m, v, *g_layers)


def _sum4_fn(a, b, c, d):
    return (((a.astype(F32) + b.astype(F32)) + c.astype(F32)) + d.astype(F32),)


def _add2_fn(a, b):
    return (a + b,)


def kernel(x, norm1_g, w_in, b_gate, ssm_lambda_re, ssm_lambda_im, ssm_log_dt, ssm_b_re, ssm_b_im, ssm_c_re, ssm_c_im, ssm_d, w_ssm_glu, w_att_up, conv_w, conv_b, conv_ln_g, conv_ln_b, w_conv_pw2, w_out, norm2_g, w_ffn_in, w_ffn_out, final_g, loss_target, m_norm1_g, m_w_in, m_b_gate, m_ssm_lambda_re, m_ssm_lambda_im, m_ssm_log_dt, m_ssm_b_re, m_ssm_b_im, m_ssm_c_re, m_ssm_c_im, m_ssm_d, m_w_ssm_glu, m_w_att_up, m_conv_w, m_conv_b, m_conv_ln_g, m_conv_ln_b, m_w_conv_pw2, m_w_out, m_norm2_g, m_w_ffn_in, m_w_ffn_out, m_final_g, v_norm1_g, v_w_in, v_b_gate, v_ssm_lambda_re, v_ssm_lambda_im, v_ssm_log_dt, v_ssm_b_re, v_ssm_b_im, v_ssm_c_re, v_ssm_c_im, v_ssm_d, v_w_ssm_glu, v_w_att_up, v_conv_w, v_conv_b, v_conv_ln_g, v_conv_ln_b, v_w_conv_pw2, v_w_out, v_norm2_g, v_w_ffn_in, v_w_ffn_out, v_final_g):
    args = dict(locals())
    wts = {k: args[k] for k in WEIGHTS}
    mom = {k: args["m_" + k] for k in WEIGHTS}
    var = {k: args["v_" + k] for k in WEIGHTS}
    bl, seq, d = x.shape
    n = bl * seq
    depth = norm1_g.shape[0]
    me = _chip_index()

    assert depth == 2, "the exchanges of layer 1 are hidden behind layer 0's kernels"
    first = BIG[:1]
    rest = BIG[1:]

    shards = lambda keys, l: [wts[k][l].astype(BF16) for k in keys]

    def whole(keys, gathered):
        out = {}
        for k, a in zip(keys, gathered):
            _, ks, cs = a.shape
            if k in ROW_SHARDED:
                out[k] = a.reshape(N_CHIPS * ks, cs)
            elif cs < MIN_SHARD_TILE:
                out[k] = a.transpose(1, 0, 2).reshape(ks, N_CHIPS * cs)
            else:
                out[k] = a
        return out

    fill = lambda gathered, own: [_own_slot(g, o) for g, o in zip(gathered, own)]
    own0 = shards(first, 0) + [conv_w]
    gathered = fill(_run_exchange(_GatherShards(own0), "gather_weights"), own0)
    conv_full = gathered[-1].transpose(1, 2, 0, 3).reshape(depth, CONV_WIDTH, -1)
    params = lambda l: dict({k: wts[k][l] for k in SMALL if k not in ("final_g", "conv_w")}, conv_w=conv_full[l])

    own = {"mm_in": shards(rest, 0), "ssm_fwd": shards(first, 1), "mm_ffn_in": shards(rest, 1)}
    xs, s0, hidden, w0 = _layer_fwd(_to_rows(x), whole(first, gathered[:-1]), params(0), bl, {k: _GatherShards(v) for k, v in own.items()},
                                    late=lambda got: whole(rest, fill(got, own["mm_in"])))
    w1 = dict(whole(first, fill(hidden["ssm_fwd"], own["ssm_fwd"])), **whole(rest, fill(hidden["mm_ffn_in"], own["mm_ffn_in"])))
    full = [w0, w1]
    xs, s1, _, _ = _layer_fwd(xs, full[1], params(1), bl, {})
    dx32, dx16, sq, dgf = _rowwise(_loss_fn, "loss_head", n, [("row", xs, d, 0), ("par", _row(final_g)), ("row", _to_rows(loss_target), d, 0)],
                                   [(d, F32), (d, BF16)], [d, d])
    dx = (dx32, dx16)
    loss = lax.psum(0.5 * jnp.sum(sq) / d, ("x", "y", "c"))

    pieces = lambda bufs, keys: [tuple(b.reshape(N_CHIPS, -1, b.shape[-1]) for b in bufs[k]) for k in keys]
    dx, g1, bufs1, _ = _layer_bwd(dx, s1, full[1], params(1), bl, {})
    p32_1, p16_1 = _reduce_prepare(pieces(bufs1, BIG))
    kept_early = []

    def early(bufs):
        kept, p16 = _reduce_prepare(pieces(bufs, EARLY_GRADS))
        kept_early.extend(kept)
        return _ScatterPieces(p16)

    dx, g0, bufs0, hidden = _layer_bwd(dx, s0, full[0], params(0), bl,
                                       {"ssm_bwd": _ScatterPieces(p16_1[:1]), "conv_bwd": _ScatterPieces(p16_1[1:])}, early)
    red1 = _reduce_finish(p32_1, list(hidden["ssm_bwd"]) + list(hidden["conv_bwd"]))
    late = [k for k in BIG if k not in EARLY_GRADS]
    kept_late, p16_late = _reduce_prepare(pieces(bufs0, late))
    done = dict(zip(EARLY_GRADS, _reduce_finish(kept_early, hidden["mm_in_dx"])))
    done.update(zip(late, _reduce_finish(kept_late, _run_exchange(_ScatterPieces(p16_late), "rs_scatter"))))
    red0 = [done[k] for k in BIG]
    grads = {"final_g": dgf[0]}
    for k in SMALL:
        if k != "final_g":
            grads[k] = jnp.stack([g0[k], g1[k]])
    grad_x = _from_rows(dx[0], bl)
    outs = {}
    for k, r0, r1 in zip(BIG, red0, red1):
        for tag, a in zip(("grad", "delta", "m", "v"), _adamw_layers(wts[k], [r0, r1], mom[k], var[k])):
            outs[tag, k] = a

    def flat1(t):
        v = jnp.concatenate([t[k].reshape(-1) for k in SMALL])
        rows = -(-v.size // (8 * LANES)) * 8
        return _pad_rows(v, rows * LANES).reshape(rows, LANES), rows

    def unflat1(flat, shapes):
        out, off, v = {}, 0, flat.reshape(-1)
        for k in SMALL:
            size = math.prod(shapes[k])
            out[k] = v[off:off + size].reshape(shapes[k])
            off += size
        return out

    grads["conv_w"] = grads["conv_w"][:, :CONV_WIDTH]
    gs, rows = flat1(grads)
    chip_sum, = _flat_fn(_add2_fn, "ar_add", [gs, _swap_sibling(gs, "ar_swap")], 1, rows)
    slots = _chip_allgather(chip_sum, "ar_gather")
    gs_red, = _flat_fn(_sum4_fn, "ar_sum", [slots[j] for j in range(N_CHIPS)], 1, rows)
    g_sm = unflat1(gs_red, {k: grads[k].shape for k in SMALL})
    cs = conv_w.shape[2]
    g_sm["conv_w"] = lax.dynamic_slice_in_dim(g_sm["conv_w"], me * cs, cs, axis=2)
    (w1, rows), (g1, _), (m1, _), (v1, _) = flat1(wts), flat1(g_sm), flat1(mom), flat1(var)
    sm_out = _flat_fn(_adamw_fn, "adamw_small", [w1, g1, m1, v1], 3, rows)
    shapes = {k: wts[k].shape for k in SMALL}
    for tag, a in zip(("delta", "m", "v"), sm_out):
        for k, t in unflat1(a, shapes).items():
            outs[tag, k] = t
    for k in SMALL:
        outs["grad", k] = g_sm[k]
    return (loss, grad_x, *[outs["grad", k] for k in WEIGHTS], *[outs["delta", k] for k in WEIGHTS],
            *[outs["m", k] for k in WEIGHTS], *[outs["v", k] for k in WEIGHTS])
```
